```python
import jax, jax.numpy as jnp
from jax import lax
import numpy as np

D_MODEL = 2048
BATCH = 4
SEQ = 2048
DEPTH = 1
DEC_BATCH = 16
DEC_SEQ = 64
PAST_LEN = 2048

CHUNK = 64
N_HEADS = 8
QK_NOPE = 128
ROPE_DIM = 64
V_DIM = 128
Q_LORA = 512
KV_LORA = 512
ATTN_W = N_HEADS * V_DIM
CONV_CH = D_MODEL - ATTN_W
D_MIX = ATTN_W + CONV_CH
CONV_W = 3
D_IN = Q_LORA + KV_LORA + ROPE_DIM + 3 * CONV_CH
IN_SPLITS = (Q_LORA, Q_LORA + KV_LORA, Q_LORA + KV_LORA + ROPE_DIM,
             Q_LORA + KV_LORA + ROPE_DIM + CONV_CH, Q_LORA + KV_LORA + ROPE_DIM + 2 * CONV_CH)
N_GROUPS = 4
EXPERTS_PER_GROUP = 8
N_EXPERTS = N_GROUPS * EXPERTS_PER_GROUP
TOP_K = 2
D_FF_EXPERT = 512
MOE_BLOCK = 128
QBLK = 128
ROPE_THETA = 10000.0
EPS = 1e-6
ATTN_SCALE = (QK_NOPE + ROPE_DIM) ** -0.5

kernel_name = 'hymba_mla_shortconv_hiermoe_stream_step'


def rmsnorm(x, g):
    xf = x.astype(jnp.float32)
    y = xf * lax.rsqrt(jnp.mean(xf * xf, axis=-1, keepdims=True) + EPS)
    return (y * g.astype(jnp.float32)).astype(x.dtype)


def rope(x, pos):
    inv = ROPE_THETA ** (-jnp.arange(0, ROPE_DIM, 2, dtype=jnp.float32) / ROPE_DIM)
    ang = pos.astype(jnp.float32)[:, None] * inv[None, :]
    cos = jnp.cos(ang)[:, None, :]
    sin = jnp.sin(ang)[:, None, :]
    xf = x.astype(jnp.float32)
    x1, x2 = xf[..., :ROPE_DIM // 2], xf[..., ROPE_DIM // 2:]
    return jnp.concatenate([x1 * cos - x2 * sin, x1 * sin + x2 * cos], axis=-1).astype(x.dtype)


def mla_attend(q_lat, q_rope, ckv, krope, q_pos, k_pos):
    B, Tq, H, L = q_lat.shape
    blk = min(QBLK, Tq)
    nb = Tq // blk
    k_chunk = k_pos // CHUNK

    def one_block(args):
        ql, qr, qp = args
        s = (jnp.einsum('bqhl,bkl->bhqk', ql, ckv) + jnp.einsum('bqhr,bkr->bhqk', qr, krope))
        s = s.astype(jnp.float32) * ATTN_SCALE
        mask = (qp[:, None] // CHUNK) >= k_chunk[None, :]
        p = jax.nn.softmax(jnp.where(mask, s, -jnp.inf), axis=-1).astype(ckv.dtype)
        return jnp.einsum('bhqk,bkl->bqhl', p, ckv)

    qlb = q_lat.reshape(B, nb, blk, H, L).swapaxes(0, 1)
    qrb = q_rope.reshape(B, nb, blk, H, ROPE_DIM).swapaxes(0, 1)
    o = lax.map(one_block, (qlb, qrb, q_pos.reshape(nb, blk)))
    return o.swapaxes(0, 1).reshape(B, Tq, H, L)


def token_mixer(xn, pos, past_ckv, past_krope, conv_state, w_in, g_q, w_uq, g_kv, w_uk, w_uv,
                conv_w, g_ao, g_co, w_o):
    B, T, _ = xn.shape
    z = xn @ w_in
    c_q, c_kv, k_r, gate_b, gate_c, h = jnp.split(z, IN_SPLITS, axis=-1)
    q = (rmsnorm(c_q, g_q) @ w_uq).reshape(B, T, N_HEADS, QK_NOPE + ROPE_DIM)
    q_nope = q[..., :QK_NOPE]
    q_rope = rope(q[..., QK_NOPE:], pos)
    c_kv = rmsnorm(c_kv, g_kv)
    k_rope = rope(k_r[:, :, None, :], pos)[:, :, 0, :]
    q_lat = jnp.einsum('bthn,lhn->bthl', q_nope, w_uk)
    if past_ckv is None:
        keys_ckv, keys_kr = c_kv, k_rope
    else:
        keys_ckv = jnp.concatenate([past_ckv, c_kv], axis=1)
        keys_kr = jnp.concatenate([past_krope, k_rope], axis=1)
    k_pos = jnp.arange(keys_ckv.shape[1], dtype=jnp.int32)
    o_lat = mla_attend(q_lat, q_rope, keys_ckv, keys_kr, pos, k_pos)
    attn = jnp.einsum('bthl,lhv->bthv', o_lat, w_uv).reshape(B, T, ATTN_W)
    u = gate_c * h
    u_ext = jnp.concatenate([conv_state, u], axis=1)
    conv = sum(conv_w[j] * u_ext[:, j:j + T] for j in range(CONV_W))
    conv_out = gate_b * conv
    y = jnp.concatenate([rmsnorm(attn, g_ao), rmsnorm(conv_out, g_co)], axis=-1) @ w_o
    return y, c_kv, k_rope, u_ext[:, -(CONV_W - 1):]


def hier_moe(x2, w_rg, b_rg, w_re, b_re, w_gate, w_up, w_down):
    T, D = x2.shape
    xf = x2.astype(jnp.float32)
    tok = jnp.arange(T)
    g_logits = xf @ w_rg.astype(jnp.float32) + b_rg.astype(jnp.float32)
    g_prob = jax.nn.softmax(g_logits, axis=-1)
    g_idx = jnp.argmax(g_logits, axis=-1).astype(jnp.int32)
    g_p = g_prob[tok, g_idx][:, None]
    e_logits = jnp.einsum('td,dge->tge', xf, w_re.astype(jnp.float32)) + b_re.astype(jnp.float32)
    e_prob = jax.nn.softmax(e_logits[tok, g_idx], axis=-1)
    top_p, top_i = lax.top_k(e_prob, TOP_K)
    gate = (g_p * top_p / jnp.sum(top_p, axis=-1, keepdims=True)).astype(x2.dtype)
    expert_idx = g_idx[:, None] * EXPERTS_PER_GROUP + top_i.astype(jnp.int32)
    A = T * TOP_K
    flat_e = expert_idx.reshape(A)
    flat_tok = jnp.arange(A, dtype=jnp.int32) // TOP_K
    flat_g = gate.reshape(A)
    order = jnp.argsort(flat_e)
    se = flat_e[order]
    counts = jnp.bincount(flat_e, length=N_EXPERTS)
    padded = (counts + MOE_BLOCK - 1) // MOE_BLOCK * MOE_BLOCK
    pad_end = jnp.cumsum(padded)
    pad_start = pad_end - padded
    start = jnp.cumsum(counts) - counts
    dest = pad_start[se] + jnp.arange(A) - start[se]
    n_blocks = -(-A // MOE_BLOCK) + N_EXPERTS
    P = n_blocks * MOE_BLOCK
    row_tok = jnp.full((P,), T, jnp.int32).at[dest].set(flat_tok[order])
    row_gate = jnp.zeros((P,), x2.dtype).at[dest].set(flat_g[order])
    block_e = jnp.minimum(jnp.searchsorted(pad_end, jnp.arange(n_blocks) * MOE_BLOCK, side='right'),
                          N_EXPERTS - 1)
    x_pad = jnp.concatenate([x2, jnp.zeros((1, D), x2.dtype)], axis=0)

    def expert_block(args):
        rt, rg, e = args
        xb = x_pad[rt]
        hb = jax.nn.silu(xb @ w_gate[e]) * (xb @ w_up[e])
        return (hb @ w_down[e]) * rg[:, None]

    out = lax.map(expert_block, (row_tok.reshape(n_blocks, MOE_BLOCK),
                                 row_gate.reshape(n_blocks, MOE_BLOCK), block_e))
    return jax.ops.segment_sum(out.reshape(P, D), row_tok, num_segments=T + 1)[:T]


def trunk_layer(x, pos, past_ckv, past_krope, conv_state, norm_mix, w_in, norm_q, w_uq, norm_kv,
                w_uk, w_uv, conv_w, norm_attn_out, norm_conv_out, w_o, norm_ffn, w_router_group,
                b_router_group, w_router_expert, b_router_expert, w_gate, w_up, w_down):
    mix, c_kv, k_rope, new_conv = token_mixer(rmsnorm(x, norm_mix), pos, past_ckv, past_krope,
                                              conv_state, w_in, norm_q, w_uq, norm_kv, w_uk, w_uv,
                                              conv_w, norm_attn_out, norm_conv_out, w_o)
    h = x + mix
    B, T, D = h.shape
    ffn = hier_moe(rmsnorm(h, norm_ffn).reshape(B * T, D), w_router_group, b_router_group,
                   w_router_expert, b_router_expert, w_gate, w_up, w_down).reshape(B, T, D)
    return h + ffn, c_kv, k_rope, new_conv


def setup_inputs(seed: int = 0) -> dict:
    key = jax.random.key(seed)
    ks = jax.random.split(key, 26)

    def nrm(k, shape, scale):
        return jax.random.normal(k, shape, jnp.float32) * scale

    def gain(k, shape):
        return 1.0 + 0.01 * jax.random.normal(k, shape, jnp.float32)

    return {
        'x_prompt': nrm(ks[0], (BATCH, SEQ, D_MODEL), 1.0),
        'x_sample': nrm(ks[1], (DEC_BATCH, DEC_SEQ, D_MODEL), 1.0),
        'cache_kv_latent': nrm(ks[2], (DEPTH, DEC_BATCH, PAST_LEN, KV_LORA), 1.0),
        'cache_k_rope': nrm(ks[3], (DEPTH, DEC_BATCH, PAST_LEN, ROPE_DIM), 1.0),
        'state_conv': nrm(ks[4], (DEPTH, DEC_BATCH, CONV_W - 1, CONV_CH), 1.0),
        'norm_mix': gain(ks[5], (DEPTH, D_MODEL)),
        'w_in': nrm(ks[6], (DEPTH, D_MODEL, D_IN), D_MODEL ** -0.5),
        'norm_q': gain(ks[7], (DEPTH, Q_LORA)),
        'w_uq': nrm(ks[8], (DEPTH, Q_LORA, N_HEADS * (QK_NOPE + ROPE_DIM)), Q_LORA ** -0.5),
        'norm_kv': gain(ks[9], (DEPTH, KV_LORA)),
        'w_uk': nrm(ks[10], (DEPTH, KV_LORA, N_HEADS, QK_NOPE), KV_LORA ** -0.5),
        'w_uv': nrm(ks[11], (DEPTH, KV_LORA, N_HEADS, V_DIM), KV_LORA ** -0.5),
        'conv_w': nrm(ks[12], (DEPTH, CONV_W, CONV_CH), CONV_W ** -0.5),
        'norm_attn_out': gain(ks[13], (DEPTH, ATTN_W)),
        'norm_conv_out': gain(ks[14], (DEPTH, CONV_CH)),
        'w_o': nrm(ks[15], (DEPTH, D_MIX, D_MODEL), D_MIX ** -0.5),
        'norm_ffn': gain(ks[16], (DEPTH, D_MODEL)),
        'w_router_group': nrm(ks[17], (DEPTH, D_MODEL, N_GROUPS), D_MODEL ** -0.5),
        'b_router_group': nrm(ks[18], (DEPTH, N_GROUPS), 0.01),
        'w_router_expert': nrm(ks[19], (DEPTH, D_MODEL, N_GROUPS, EXPERTS_PER_GROUP), D_MODEL ** -0.5),
        'b_router_expert': nrm(ks[20], (DEPTH, N_GROUPS, EXPERTS_PER_GROUP), 0.01),
        'w_gate': nrm(ks[21], (DEPTH, N_EXPERTS, D_MODEL, D_FF_EXPERT), D_MODEL ** -0.5),
        'w_up': nrm(ks[22], (DEPTH, N_EXPERTS, D_MODEL, D_FF_EXPERT), D_MODEL ** -0.5),
        'w_down': nrm(ks[23], (DEPTH, N_EXPERTS, D_FF_EXPERT, D_MODEL), D_FF_EXPERT ** -0.5),
        'norm_final': gain(ks[24], (D_MODEL,)),
    }


def reference(x_prompt, x_sample, cache_kv_latent, cache_k_rope, state_conv, norm_mix, w_in, norm_q,
              w_uq, norm_kv, w_uk, w_uv, conv_w, norm_attn_out, norm_conv_out, w_o, norm_ffn,
              w_router_group, b_router_group, w_router_expert, b_router_expert, w_gate, w_up, w_down,
              norm_final):
    pos_p = jnp.arange(x_prompt.shape[1], dtype=jnp.int32)
    pos_s = cache_kv_latent.shape[2] + jnp.arange(x_sample.shape[1], dtype=jnp.int32)
    xp, xs = x_prompt, x_sample
    kvp, krp, cvp, kvs, krs, cvs = [], [], [], [], [], []
    for l in range(DEPTH):
        lw = (norm_mix[l], w_in[l], norm_q[l], w_uq[l], norm_kv[l], w_uk[l], w_uv[l], conv_w[l],
              norm_attn_out[l], norm_conv_out[l], w_o[l], norm_ffn[l], w_router_group[l],
              b_router_group[l], w_router_expert[l], b_router_expert[l], w_gate[l], w_up[l], w_down[l])
        zero_conv = jnp.zeros((xp.shape[0], CONV_W - 1, CONV_CH), xp.dtype)
        xp, a, b, c = trunk_layer(xp, pos_p, None, None, zero_conv, *lw)
        kvp.append(a); krp.append(b); cvp.append(c)
        xs, a, b, c = trunk_layer(xs, pos_s, cache_kv_latent[l], cache_k_rope[l], state_conv[l], *lw)
        kvs.append(a); krs.append(b); cvs.append(c)
    y_prompt = rmsnorm(xp, norm_final)
    y_sample = rmsnorm(xs, norm_final)
    return (y_prompt, y_sample, jnp.stack(kvp), jnp.stack(krp), jnp.stack(cvp),
            jnp.stack(kvs), jnp.stack(krs), jnp.stack(cvs))
```

```python
import functools

import jax
import jax.numpy as jnp
from jax import lax
from jax.experimental import pallas as pl
from jax.experimental.pallas import tpu as pltpu

F32 = jnp.float32
BF16 = jnp.bfloat16

D_MODEL = 2048
N_HEADS = 8
QK_NOPE = 128
ROPE_DIM = 64
V_DIM = 128
Q_LORA = 512
KV_LORA = 512
ATTN_W = N_HEADS * V_DIM
CONV_CH = D_MODEL - ATTN_W
CONV_W = 3
CHUNK = 64
N_GROUPS = 4
EXPERTS_PER_GROUP = 8
N_EXPERTS = N_GROUPS * EXPERTS_PER_GROUP
D_FF = 512
ROPE_THETA = 10000.0
EPS = 1e-6
ATTN_SCALE = (QK_NOPE + ROPE_DIM) ** -0.5

LANES = 128
SUBLANES = 8
TM = 256
MOE_BLOCK = 128
TQ = 128
TK = 256
NEG_BIG = -1e30
VMEM_LIMIT = 56 * 1024 * 1024


def _rms(v, g):
    return v * lax.rsqrt(jnp.mean(v * v, axis=-1, keepdims=True) + EPS) * g


def _const_spec(shape):
    nd = len(shape)
    return pl.BlockSpec(shape, lambda *_: (0,) * nd, pipeline_mode=pl.Buffered(1))


def _in_proj_kernel(xp_ref, xs_ref, gmix_ref, wa_ref, wc_ref, gq_ref, gkv_ref, gco_ref, convw_ref,
                    cos_ref, sin_ref, state_ref,
                    cqn_ref, ckv_ref, kr_ref, convn_ref, utail_ref, ext_ref,
                    *, n_prompt_tiles, tiles_per_seq, n_prompt_seq, sample_seq_len):
    i = pl.program_id(0)

    def conv_block(u_sub, gate_sub, row0, length):
        ext_ref[SUBLANES:SUBLANES + length, :] = u_sub
        um1 = ext_ref[SUBLANES - 1:SUBLANES - 1 + length, :]
        um2 = ext_ref[SUBLANES - 2:SUBLANES - 2 + length, :]
        cw = convw_ref[...]
        conv = cw[0:1] * um2 + cw[1:2] * um1 + cw[2:3] * u_sub
        convn_ref[row0:row0 + length, :] = _rms(gate_sub * conv, gco_ref[...]).astype(BF16)

    def tile(x_ref, is_prompt):
        x = x_ref[...]
        xn = _rms(x, gmix_ref[...]).astype(BF16)
        za = jnp.dot(xn, wa_ref[...], preferred_element_type=F32)
        cqn_ref[...] = _rms(za[:, :Q_LORA], gq_ref[...]).astype(BF16)
        ckv_ref[...] = _rms(za[:, Q_LORA:Q_LORA + KV_LORA], gkv_ref[...])
        zk = za[:, Q_LORA + KV_LORA:]
        kr_ref[...] = zk[:, :ROPE_DIM] * cos_ref[...] + zk[:, ROPE_DIM:] * sin_ref[...]

        zc = jnp.dot(xn, wc_ref[...], preferred_element_type=F32)
        gate_b = zc[:, :CONV_CH]
        u = zc[:, CONV_CH:2 * CONV_CH] * zc[:, 2 * CONV_CH:]
        for j in range(TM // CHUNK):
            utail_ref[j] = u[CHUNK * (j + 1) - SUBLANES:CHUNK * (j + 1), :]

        if is_prompt:
            first = (i % tiles_per_seq) == 0

            @pl.when(first)
            def _():
                ext_ref[SUBLANES - 2:SUBLANES, :] = state_ref[i // tiles_per_seq]

            @pl.when(jnp.logical_not(first))
            def _():
                ext_ref[SUBLANES - 2:SUBLANES, :] = ext_ref[TM + SUBLANES - 2:TM + SUBLANES, :]

            conv_block(u, gate_b, 0, TM)
        else:
            n_sub = TM // sample_seq_len
            seq0 = n_prompt_seq + (i - n_prompt_tiles) * n_sub
            for k in range(n_sub):
                ext_ref[SUBLANES - 2:SUBLANES, :] = state_ref[seq0 + k]
                lo = k * sample_seq_len
                conv_block(u[lo:lo + sample_seq_len], gate_b[lo:lo + sample_seq_len], lo, sample_seq_len)

    @pl.when(i < n_prompt_tiles)
    def _():
        tile(xp_ref, True)

    @pl.when(i >= n_prompt_tiles)
    def _():
        tile(xs_ref, False)


def _in_proj(xp, xs, gmix, w_a, w_c, gq, gkv, gco, convw, cosk, sink, state, *, seq_p, seq_s):
    np_rows, ns_rows = xp.shape[0], xs.shape[0]
    m = np_rows + ns_rows
    npt, nst = np_rows // TM, ns_rows // TM
    tps = seq_p // TM
    n_prompt_seq = np_rows // seq_p
    last_p = npt - 1

    def tab_idx(i):
        return (jnp.where(i < npt, i % tps, tps), 0)

    row = lambda i: (i, 0)
    kern = functools.partial(_in_proj_kernel, n_prompt_tiles=npt, tiles_per_seq=tps,
                             n_prompt_seq=n_prompt_seq, sample_seq_len=seq_s)
    return pl.pallas_call(
        kern,
        grid=(npt + nst,),
        in_specs=[
            pl.BlockSpec((TM, D_MODEL), lambda i: (jnp.minimum(i, last_p), 0)),
            pl.BlockSpec((TM, D_MODEL), lambda i: (jnp.maximum(i - npt, 0), 0)),
            _const_spec((1, D_MODEL)),
            _const_spec(w_a.shape),
            _const_spec(w_c.shape),
            _const_spec((1, Q_LORA)),
            _const_spec((1, KV_LORA)),
            _const_spec((1, CONV_CH)),
            _const_spec((CONV_W, CONV_CH)),
            pl.BlockSpec((TM, ROPE_DIM), tab_idx),
            pl.BlockSpec((TM, ROPE_DIM), tab_idx),
            _const_spec(state.shape),
        ],
        out_specs=[
            pl.BlockSpec((TM, Q_LORA), row),
            pl.BlockSpec((TM, KV_LORA), row),
            pl.BlockSpec((TM, ROPE_DIM), row),
            pl.BlockSpec((TM, CONV_CH), row),
            pl.BlockSpec((TM // CHUNK, SUBLANES, CONV_CH), lambda i: (i, 0, 0)),
        ],
        out_shape=[
            jax.ShapeDtypeStruct((m, Q_LORA), BF16),
            jax.ShapeDtypeStruct((m, KV_LORA), F32),
            jax.ShapeDtypeStruct((m, ROPE_DIM), F32),
            jax.ShapeDtypeStruct((m, CONV_CH), BF16),
            jax.ShapeDtypeStruct((m // CHUNK, SUBLANES, CONV_CH), F32),
        ],
        scratch_shapes=[pltpu.VMEM((TM + SUBLANES, CONV_CH), F32)],
        compiler_params=pltpu.CompilerParams(dimension_semantics=("arbitrary",),
                                             vmem_limit_bytes=VMEM_LIMIT),
        name="in_proj",
    )(xp, xs, gmix, w_a, w_c, gq, gkv, gco, convw, cosk, sink, state)


def _attn_kernel(*refs, tq, n_past, causal):
    refs = list(refs)
    cqn_ref, wq_ref, wuk_ref, wuv_ref, cos_ref, sin_ref, gao_ref = refs[:7]
    refs = refs[7:]
    if n_past:
        pkv_ref, pkr_ref = refs[:2]
        refs = refs[2:]
    kv_ref, kr_ref, out_ref, qlat_ref, qr_ref, m_ref, l_ref, acc_ref = refs

    qi = pl.program_id(1)
    rows = N_HEADS * tq

    q = jnp.dot(cqn_ref[...], wq_ref[...], preferred_element_type=F32)
    nope_w = N_HEADS * QK_NOPE
    rope_w = N_HEADS * ROPE_DIM
    qrope = q[:, nope_w:nope_w + rope_w] * cos_ref[...] + q[:, nope_w + rope_w:] * sin_ref[...]
    for h in range(N_HEADS):
        qn = q[:, h * QK_NOPE:(h + 1) * QK_NOPE].astype(BF16)
        ql = jnp.dot(qn, wuk_ref[h], preferred_element_type=F32)
        qlat_ref[h * tq:(h + 1) * tq, :] = ql.astype(BF16)
        qr_ref[h * tq:(h + 1) * tq, :] = qrope[:, h * ROPE_DIM:(h + 1) * ROPE_DIM].astype(BF16)

    m_ref[...] = jnp.full(m_ref.shape, NEG_BIG, F32)
    l_ref[...] = jnp.zeros(l_ref.shape, F32)
    acc_ref[...] = jnp.zeros(acc_ref.shape, F32)

    nt = (((1,), (1,)), ((), ()))

    def step(kc_f32, kr_f32, mask):
        kc = kc_f32.astype(BF16)
        kr = kr_f32.astype(BF16)
        s = lax.dot_general(qlat_ref[...], kc, nt, preferred_element_type=F32)
        s = s + lax.dot_general(qr_ref[...], kr, nt, preferred_element_type=F32)
        s = s * ATTN_SCALE
        if mask is not None:
            s = jnp.where(mask, s, NEG_BIG)
        m_prev = m_ref[...]
        m_new = jnp.maximum(m_prev, jnp.max(s, axis=-1, keepdims=True))
        alpha = jnp.exp(m_prev - m_new)
        p = jnp.exp(s - m_new)
        l_ref[...] = alpha * l_ref[...] + jnp.sum(p, axis=-1, keepdims=True)
        acc_ref[...] = alpha * acc_ref[...] + jnp.dot(p.astype(BF16), kc, preferred_element_type=F32)
        m_ref[...] = m_new

    if n_past:
        def past_body(j, c):
            k0 = pl.multiple_of(j * TK, TK)
            step(pkv_ref[pl.ds(k0, TK), :], pkr_ref[pl.ds(k0, TK), :], None)
            return c
        lax.fori_loop(0, n_past // TK, past_body, 0)

    if causal:
        n_blocks = ((qi + 1) * tq + TK - 1) // TK

        def new_body(j, c):
            k0 = pl.multiple_of(j * TK, TK)
            r = lax.broadcasted_iota(jnp.int32, (rows, TK), 0)
            cidx = lax.broadcasted_iota(jnp.int32, (rows, TK), 1)
            q_chunk = (qi * tq + (r % tq)) // CHUNK
            k_chunk = (k0 + cidx) // CHUNK
            step(kv_ref[pl.ds(k0, TK), :], kr_ref[pl.ds(k0, TK), :], q_chunk >= k_chunk)
            return c
        lax.fori_loop(0, n_blocks, new_body, 0)
    else:
        step(kv_ref[...], kr_ref[...], None)

    o = acc_ref[...] / l_ref[...]
    parts = []
    for h in range(N_HEADS):
        oh = o[h * tq:(h + 1) * tq, :].astype(BF16)
        parts.append(jnp.dot(oh, wuv_ref[h], preferred_element_type=F32))
    attn = jnp.concatenate(parts, axis=-1)
    out_ref[...] = _rms(attn, gao_ref[...]).astype(BF16)


def _attention(cqn, w_q, w_ukt, w_uv, cosq, sinq, gao, ckv, krope, *, n_batch, seq, row0,
               past_kv=None, past_kr=None):
    causal = past_kv is None
    tq = TQ if causal else seq
    nq = seq // tq
    n_past = 0 if causal else past_kv.shape[1]
    if not causal:
        assert n_past % CHUNK == 0 and seq <= CHUNK and n_past % TK == 0
    blk0 = row0 // tq
    qrow = lambda b, q: (blk0 + b * nq + q, 0)
    in_specs = [
        pl.BlockSpec((tq, Q_LORA), qrow),
        _const_spec(w_q.shape),
        _const_spec(w_ukt.shape),
        _const_spec(w_uv.shape),
        pl.BlockSpec((tq, N_HEADS * ROPE_DIM), lambda b, q: (q, 0)),
        pl.BlockSpec((tq, N_HEADS * ROPE_DIM), lambda b, q: (q, 0)),
        _const_spec((1, ATTN_W)),
    ]
    args = [cqn, w_q, w_ukt, w_uv, cosq, sinq, gao]
    if n_past:
        in_specs += [pl.BlockSpec((None, n_past, KV_LORA), lambda b, q: (b, 0, 0)),
                     pl.BlockSpec((None, n_past, ROPE_DIM), lambda b, q: (b, 0, 0))]
        args += [past_kv, past_kr]
    sblk0 = row0 // seq
    in_specs += [pl.BlockSpec((seq, KV_LORA), lambda b, q: (sblk0 + b, 0)),
                 pl.BlockSpec((seq, ROPE_DIM), lambda b, q: (sblk0 + b, 0))]
    args += [ckv, krope]
    rows = N_HEADS * tq
    kern = functools.partial(_attn_kernel, tq=tq, n_past=n_past, causal=causal)
    return pl.pallas_call(
        kern,
        grid=(n_batch, nq),
        in_specs=in_specs,
        out_specs=pl.BlockSpec((tq, ATTN_W), lambda b, q: (b * nq + q, 0)),
        out_shape=jax.ShapeDtypeStruct((n_batch * seq, ATTN_W), BF16),
        scratch_shapes=[
            pltpu.VMEM((rows, KV_LORA), BF16),
            pltpu.VMEM((rows, ROPE_DIM), BF16),
            pltpu.VMEM((rows, 1), F32),
            pltpu.VMEM((rows, 1), F32),
            pltpu.VMEM((rows, KV_LORA), F32),
        ],
        compiler_params=pltpu.CompilerParams(dimension_semantics=("arbitrary", "arbitrary"),
                                             vmem_limit_bytes=VMEM_LIMIT),
        name="attn_prompt" if causal else "attn_sample",
    )(*args)


def _out_proj_kernel(attnp_ref, attns_ref, convn_ref, xp_ref, xs_ref, woa_ref, woc_ref, gffn_ref, wr_ref, br_ref,
                     h_ref, xpk_ref, mi_ref, mf_ref, cnt_ref, carry_ref, *, n_prompt_tiles):
    i = pl.program_id(0)

    @pl.when(i == 0)
    def _():
        carry_ref[...] = jnp.zeros(carry_ref.shape, F32)

    def tile(x_ref, attn_ref):
        y = jnp.dot(attn_ref[...], woa_ref[...], preferred_element_type=F32)
        y = y + jnp.dot(convn_ref[...], woc_ref[...], preferred_element_type=F32)
        h = x_ref[...] + y
        h_ref[...] = h
        xn = _rms(h, gffn_ref[...])

        half = D_MODEL // 2
        lo = lax.bitcast_convert_type(xn[:, :half].astype(BF16).astype(F32), jnp.uint32)
        hi = lax.bitcast_convert_type(xn[:, half:].astype(BF16).astype(F32), jnp.uint32)
        xpk_ref[...] = (lo >> 16) | (hi & jnp.uint32(0xFFFF0000))

        logits = jnp.dot(xn, wr_ref[...], preferred_element_type=F32,
                         precision=lax.Precision.HIGHEST) + br_ref[...]
        lane = lax.broadcasted_iota(jnp.int32, (TM, LANES), 1).astype(F32)
        ninf = -jnp.inf
        far = float(LANES)

        def first_argmax(v):
            vmax = jnp.max(v, axis=-1, keepdims=True)
            return vmax, jnp.min(jnp.where(v == vmax, lane, far), axis=-1, keepdims=True)

        gl = jnp.where(lane < N_GROUPS, logits, ninf)
        gmax, gidx = first_argmax(gl)
        g_p = 1.0 / jnp.sum(jnp.exp(gl - gmax), axis=-1, keepdims=True)
        e_lo = N_GROUPS + EXPERTS_PER_GROUP * gidx
        el = jnp.where((lane >= e_lo) & (lane < e_lo + EXPERTS_PER_GROUP), logits, ninf)
        e1max, i1 = first_argmax(el)
        z = jnp.sum(jnp.exp(el - e1max), axis=-1, keepdims=True)
        el2 = jnp.where(lane == i1, ninf, el)
        e2max, i2 = first_argmax(el2)
        p1 = 1.0 / z
        p2 = jnp.exp(e2max - e1max) / z
        den = p1 + p2
        g0 = g_p * p1 / den
        g1 = g_p * p2 / den
        e0 = i1 - N_GROUPS
        e1 = i2 - N_GROUPS

        oh0 = lane == e0
        oh1 = lane == e1
        oh = jnp.where(oh0 | oh1, 1.0, 0.0)
        r = lax.broadcasted_iota(jnp.int32, (TM, TM), 0)
        c = lax.broadcasted_iota(jnp.int32, (TM, TM), 1)
        ltri = jnp.where(r > c, 1.0, 0.0).astype(BF16)
        before = jnp.dot(ltri, oh.astype(BF16), preferred_element_type=F32) + carry_ref[...]
        rank0 = jnp.sum(jnp.where(oh0, before, 0.0), axis=-1, keepdims=True)
        rank1 = jnp.sum(jnp.where(oh1, before, 0.0), axis=-1, keepdims=True)
        total = carry_ref[...] + jnp.sum(oh, axis=0, keepdims=True)
        carry_ref[...] = total
        cnt_ref[...] = jnp.broadcast_to(total, cnt_ref.shape)

        mi = jnp.where(lane == 0, e0, jnp.where(lane == 1, e1, jnp.where(lane == 2, rank0, rank1)))
        mi_ref[...] = mi.astype(jnp.int32)
        mf_ref[...] = jnp.where(lane == 0, g0, g1)

    @pl.when(i < n_prompt_tiles)
    def _():
        tile(xp_ref, attnp_ref)

    @pl.when(i >= n_prompt_tiles)
    def _():
        tile(xs_ref, attns_ref)


def _out_proj(attn_p, attn_s, conv_n, xp, xs, w_oa, w_oc, gffn, w_r, b_r):
    m = conv_n.shape[0]
    npt = xp.shape[0] // TM
    last_p = npt - 1
    row = lambda i: (i, 0)
    return pl.pallas_call(
        functools.partial(_out_proj_kernel, n_prompt_tiles=npt),
        grid=(m // TM,),
        in_specs=[
            pl.BlockSpec((TM, ATTN_W), lambda i: (jnp.minimum(i, last_p), 0)),
            pl.BlockSpec((TM, ATTN_W), lambda i: (jnp.maximum(i - npt, 0), 0)),
            pl.BlockSpec((TM, CONV_CH), row),
            pl.BlockSpec((TM, D_MODEL), lambda i: (jnp.minimum(i, last_p), 0)),
            pl.BlockSpec((TM, D_MODEL), lambda i: (jnp.maximum(i - npt, 0), 0)),
            _const_spec(w_oa.shape),
            _const_spec(w_oc.shape),
            _const_spec((1, D_MODEL)),
            _const_spec(w_r.shape),
            _const_spec((1, LANES)),
        ],
        out_specs=[
            pl.BlockSpec((TM, D_MODEL), row),
            pl.BlockSpec((TM, D_MODEL // 2), row),
            pl.BlockSpec((TM, LANES), row),
            pl.BlockSpec((TM, LANES), row),
            pl.BlockSpec((SUBLANES, LANES), lambda i: (0, 0)),
        ],
        out_shape=[
            jax.ShapeDtypeStruct((m, D_MODEL), F32),
            jax.ShapeDtypeStruct((m, D_MODEL // 2), jnp.uint32),
            jax.ShapeDtypeStruct((m, LANES), jnp.int32),
            jax.ShapeDtypeStruct((m, LANES), F32),
            jax.ShapeDtypeStruct((SUBLANES, LANES), F32),
        ],
        scratch_shapes=[pltpu.VMEM((1, LANES), F32)],
        compiler_params=pltpu.CompilerParams(dimension_semantics=("arbitrary",),
                                             vmem_limit_bytes=VMEM_LIMIT),
        name="out_proj",
    )(attn_p, attn_s, conv_n, xp, xs, w_oa, w_oc, gffn, w_r, b_r)


def _dispatch_kernel(d0_ref, d1_ref, zlo_ref, zn_ref, nu_ref, xpk_ref, xs_hbm, zeros_ref, sems, *, n_blocks):
    i = pl.program_id(0)
    sem = sems.at[0]
    zsem = sems.at[1]

    def row_copy(src_ref, src_row, dst_row):
        return pltpu.make_async_copy(src_ref.at[pl.ds(src_row, 1)], xs_hbm.at[pl.ds(dst_row, 1)], sem)

    def zero_fill(act):
        def per_expert(e, c):
            lo = zlo_ref[e]
            n = zn_ref[e]
            head = (-lo) & (SUBLANES - 1)
            for r in range(SUBLANES - 1):
                @pl.when(r < head)
                def _(r=r):
                    act(pltpu.make_async_copy(zeros_ref.at[pl.ds(0, 1)], xs_hbm.at[pl.ds(lo + r, 1)], zsem))
            off = lo + head
            rest = n - head
            size = MOE_BLOCK // 2
            while size >= SUBLANES:
                @pl.when((rest & size) != 0)
                def _(off=off, size=size):
                    dst = xs_hbm.at[pl.ds(pl.multiple_of(off, SUBLANES), size)]
                    act(pltpu.make_async_copy(zeros_ref.at[pl.ds(0, size)], dst, zsem))
                off = off + (rest & size)
                size //= 2
            return c

        def per_block(b, c):
            dst = xs_hbm.at[pl.ds(pl.multiple_of(b * MOE_BLOCK, MOE_BLOCK), MOE_BLOCK)]
            act(pltpu.make_async_copy(zeros_ref, dst, zsem))
            return c

        lax.fori_loop(0, N_EXPERTS, per_expert, 0)
        lax.fori_loop(nu_ref[0], n_blocks, per_block, 0)

    @pl.when(i == 0)
    def _():
        zeros_ref[...] = jnp.zeros(zeros_ref.shape, zeros_ref.dtype)
        zero_fill(lambda cp: cp.start())
        zero_fill(lambda cp: cp.wait())

    base = i * TM

    def start(r, c):
        row_copy(xpk_ref, r, d0_ref[base + r]).start()
        row_copy(xpk_ref, r, d1_ref[base + r]).start()
        return c

    def wait(r, c):
        row_copy(xpk_ref, r, d0_ref[base + r]).wait()
        row_copy(xpk_ref, r, d1_ref[base + r]).wait()
        return c

    lax.fori_loop(0, TM, start, 0)
    lax.fori_loop(0, TM, wait, 0)


def _dispatch(dest0, dest1, pad_lo, n_pad, n_used, xpk, n_blocks):
    m = xpk.shape[0]
    grid_spec = pltpu.PrefetchScalarGridSpec(
        num_scalar_prefetch=5,
        grid=(m // TM,),
        in_specs=[pl.BlockSpec((TM, D_MODEL // 2), lambda i, *_: (i, 0))],
        out_specs=pl.BlockSpec(memory_space=pl.ANY),
        scratch_shapes=[pltpu.VMEM((MOE_BLOCK, D_MODEL // 2), jnp.uint32),
                        pltpu.SemaphoreType.DMA((2,))],
    )
    return pl.pallas_call(
        functools.partial(_dispatch_kernel, n_blocks=n_blocks),
        grid_spec=grid_spec,
        out_shape=jax.ShapeDtypeStruct((n_blocks * MOE_BLOCK, D_MODEL // 2), jnp.uint32),
        compiler_params=pltpu.CompilerParams(dimension_semantics=("arbitrary",)),
        name="dispatch",
    )(dest0, dest1, pad_lo, n_pad, n_used, xpk)


def _experts_kernel(be_ref, nu_ref, x_ref, wg_ref, wu_ref, wd_ref, y_ref):
    b = pl.program_id(0)

    @pl.when(b < nu_ref[0])
    def _():
        half = D_MODEL // 2
        xw = x_ref[...]
        xa = lax.bitcast_convert_type(xw << 16, F32).astype(BF16)
        xb = lax.bitcast_convert_type(xw & jnp.uint32(0xFFFF0000), F32).astype(BF16)
        g = jnp.dot(xa, wg_ref[:half, :], preferred_element_type=F32)
        g = g + jnp.dot(xb, wg_ref[half:, :], preferred_element_type=F32)
        u = jnp.dot(xa, wu_ref[:half, :], preferred_element_type=F32)
        u = u + jnp.dot(xb, wu_ref[half:, :], preferred_element_type=F32)
        hmid = (g * jax.nn.sigmoid(g)) * u
        y_ref[...] = jnp.dot(hmid.astype(BF16), wd_ref[...], preferred_element_type=F32)

    @pl.when(b >= nu_ref[0])
    def _():
        y_ref[...] = jnp.zeros(y_ref.shape, y_ref.dtype)


def _experts(block_e, n_used, x_sorted, w_gate, w_up, w_down):
    p = x_sorted.shape[0]
    nb = p // MOE_BLOCK

    def xrow(b, be, nu):
        return (jnp.minimum(b, nu[0] - 1), 0)

    def wsel(b, be, nu):
        return (be[b], 0, 0)

    grid_spec = pltpu.PrefetchScalarGridSpec(
        num_scalar_prefetch=2,
        grid=(nb,),
        in_specs=[
            pl.BlockSpec((MOE_BLOCK, D_MODEL // 2), xrow),
            pl.BlockSpec((None, D_MODEL, D_FF), wsel),
            pl.BlockSpec((None, D_MODEL, D_FF), wsel),
            pl.BlockSpec((None, D_FF, D_MODEL), wsel),
        ],
        out_specs=pl.BlockSpec((MOE_BLOCK, D_MODEL), lambda b, be, nu: (b, 0)),
    )
    return pl.pallas_call(
        _experts_kernel,
        grid_spec=grid_spec,
        out_shape=jax.ShapeDtypeStruct((p, D_MODEL), F32),
        compiler_params=pltpu.CompilerParams(dimension_semantics=("arbitrary",),
                                             vmem_limit_bytes=VMEM_LIMIT),
        name="experts",
    )(block_e, n_used, x_sorted, w_gate, w_up, w_down)


def _combine_kernel(d0_ref, d1_ref, h_ref, mf_ref, gfin_ref, y_hbm, out_ref, y0_ref, y1_ref, sem, *, row0):
    i = pl.program_id(0)
    base = row0 + i * TM

    def copies(r):
        return (pltpu.make_async_copy(y_hbm.at[pl.ds(d0_ref[base + r], 1)], y0_ref.at[pl.ds(r, 1)], sem),
                pltpu.make_async_copy(y_hbm.at[pl.ds(d1_ref[base + r], 1)], y1_ref.at[pl.ds(r, 1)], sem))

    def start(r, c):
        for cp in copies(r):
            cp.start()
        return c

    def wait(r, c):
        for cp in copies(r):
            cp.wait()
        return c

    lax.fori_loop(0, TM, start, 0)
    lax.fori_loop(0, TM, wait, 0)
    mf = mf_ref[...]
    ffn = mf[:, 0:1] * y0_ref[...] + mf[:, 1:2] * y1_ref[...]
    out_ref[...] = _rms(h_ref[...] + ffn, gfin_ref[...])


def _combine(dest0, dest1, h, mf, gfin, y_sorted, *, row0, n_rows):
    blk0 = row0 // TM
    grid_spec = pltpu.PrefetchScalarGridSpec(
        num_scalar_prefetch=2,
        grid=(n_rows // TM,),
        in_specs=[
            pl.BlockSpec((TM, D_MODEL), lambda i, *_: (blk0 + i, 0)),
            pl.BlockSpec((TM, LANES), lambda i, *_: (blk0 + i, 0)),
            pl.BlockSpec((1, D_MODEL), lambda i, *_: (0, 0)),
            pl.BlockSpec(memory_space=pl.ANY),
        ],
        out_specs=pl.BlockSpec((TM, D_MODEL), lambda i, *_: (i, 0)),
        scratch_shapes=[pltpu.VMEM((TM, D_MODEL), F32), pltpu.VMEM((TM, D_MODEL), F32),
                        pltpu.SemaphoreType.DMA(())],
    )
    return pl.pallas_call(
        functools.partial(_combine_kernel, row0=row0),
        grid_spec=grid_spec,
        out_shape=jax.ShapeDtypeStruct((n_rows, D_MODEL), F32),
        compiler_params=pltpu.CompilerParams(dimension_semantics=("arbitrary",),
                                             vmem_limit_bytes=VMEM_LIMIT),
        name="combine",
    )(dest0, dest1, h, mf, gfin, y_sorted)


def _rope_tables(pos):
    inv = ROPE_THETA ** (-jnp.arange(0, ROPE_DIM, 2, dtype=F32) / ROPE_DIM)
    ang = pos.astype(F32)[:, None] * inv[None, :]
    cos, sin = jnp.cos(ang), jnp.sin(ang)
    return jnp.concatenate([cos, cos], axis=-1), jnp.concatenate([-sin, sin], axis=-1)


def _swap_halves(w):
    return jnp.concatenate([w[..., ROPE_DIM // 2:], w[..., :ROPE_DIM // 2]], axis=-1)


def kernel(x_prompt, x_sample, cache_kv_latent, cache_k_rope, state_conv, norm_mix, w_in, norm_q, w_uq,
           norm_kv, w_uk, w_uv, conv_w, norm_attn_out, norm_conv_out, w_o, norm_ffn, w_router_group,
           b_router_group, w_router_expert, b_router_expert, w_gate, w_up, w_down, norm_final):
    assert w_in.shape[0] == 1, "single-layer trunk"
    bp, seq_p, _ = x_prompt.shape
    bs, seq_s, _ = x_sample.shape
    past_len = cache_kv_latent.shape[2]
    np_rows, ns_rows = bp * seq_p, bs * seq_s
    m = np_rows + ns_rows
    assert seq_p % TM == 0 and TM % seq_s == 0 and ns_rows % TM == 0 and seq_s == CHUNK

    xp = x_prompt.reshape(np_rows, D_MODEL)
    xs = x_sample.reshape(ns_rows, D_MODEL)
    row_vec = lambda v: v.reshape(1, -1)

    w_in0 = w_in[0]
    mla_w = Q_LORA + KV_LORA + ROPE_DIM
    w_a = jnp.concatenate([w_in0[:, :mla_w], _swap_halves(w_in0[:, mla_w - ROPE_DIM:mla_w])], axis=1).astype(BF16)
    w_c = w_in0[:, mla_w:].astype(BF16)
    wq4 = w_uq[0].reshape(Q_LORA, N_HEADS, QK_NOPE + ROPE_DIM)
    wq_rope = wq4[:, :, QK_NOPE:]
    w_q = jnp.concatenate([wq4[:, :, :QK_NOPE].reshape(Q_LORA, -1), wq_rope.reshape(Q_LORA, -1),
                           _swap_halves(wq_rope).reshape(Q_LORA, -1)], axis=1).astype(BF16)
    w_ukt = jnp.transpose(w_uk[0], (1, 2, 0)).astype(BF16)
    w_uvh = jnp.transpose(w_uv[0], (1, 0, 2)).astype(BF16)
    w_oa = w_o[0, :ATTN_W].astype(BF16)
    w_oc = w_o[0, ATTN_W:].astype(BF16)
    n_router = N_GROUPS + N_EXPERTS
    w_r = jnp.concatenate([w_router_group[0], w_router_expert[0].reshape(D_MODEL, N_EXPERTS)], axis=1)
    w_r = jnp.pad(w_r, ((0, 0), (0, LANES - n_router)))
    b_r = jnp.pad(jnp.concatenate([b_router_group[0], b_router_expert[0].reshape(N_EXPERTS)]),
                  (0, LANES - n_router)).reshape(1, LANES)

    pos_p = jnp.arange(seq_p, dtype=jnp.int32)
    pos_s = past_len + jnp.arange(seq_s, dtype=jnp.int32)
    cos_p, sin_p = _rope_tables(pos_p)
    cos_s, sin_s = _rope_tables(pos_s)
    cosk = jnp.concatenate([cos_p, jnp.tile(cos_s, (TM // seq_s, 1))], axis=0)
    sink = jnp.concatenate([sin_p, jnp.tile(sin_s, (TM // seq_s, 1))], axis=0)
    state = jnp.concatenate([jnp.zeros((bp, CONV_W - 1, CONV_CH), F32), state_conv[0]], axis=0)

    cqn, ckv, krope, conv_n, utail = _in_proj(
        xp, xs, row_vec(norm_mix[0]), w_a, w_c, row_vec(norm_q[0]), row_vec(norm_kv[0]),
        row_vec(norm_conv_out[0]), conv_w[0], cosk, sink, state, seq_p=seq_p, seq_s=seq_s)

    gao = row_vec(norm_attn_out[0])
    attn_p = _attention(cqn, w_q, w_ukt, w_uvh, jnp.tile(cos_p, (1, N_HEADS)), jnp.tile(sin_p, (1, N_HEADS)),
                        gao, ckv, krope, n_batch=bp, seq=seq_p, row0=0)
    attn_s = _attention(cqn, w_q, w_ukt, w_uvh, jnp.tile(cos_s, (1, N_HEADS)), jnp.tile(sin_s, (1, N_HEADS)),
                        gao, ckv, krope, n_batch=bs, seq=seq_s, row0=np_rows,
                        past_kv=cache_kv_latent[0], past_kr=cache_k_rope[0])

    h, xpk, mi, mf, cnt = _out_proj(attn_p, attn_s, conv_n, xp, xs, w_oa, w_oc, row_vec(norm_ffn[0]), w_r, b_r)

    counts = cnt[0, :N_EXPERTS].astype(jnp.int32)
    padded = (counts + MOE_BLOCK - 1) // MOE_BLOCK * MOE_BLOCK
    pad_end = jnp.cumsum(padded)
    pad_start = pad_end - padded
    n_blocks = -(-(m * 2) // MOE_BLOCK) + N_EXPERTS
    block_e = jnp.minimum(jnp.searchsorted(pad_end, jnp.arange(n_blocks, dtype=jnp.int32) * MOE_BLOCK,
                                           side='right'), N_EXPERTS - 1).astype(jnp.int32)
    n_used = (pad_end[-1:] // MOE_BLOCK).astype(jnp.int32)
    dest0 = pad_start[mi[:, 0]] + mi[:, 2]
    dest1 = pad_start[mi[:, 1]] + mi[:, 3]

    x_sorted = _dispatch(dest0, dest1, pad_start + counts, padded - counts, n_used, xpk, n_blocks)
    y_sorted = _experts(block_e, n_used, x_sorted, w_gate[0].astype(BF16), w_up[0].astype(BF16),
                        w_down[0].astype(BF16))
    gfin = row_vec(norm_final)
    y_p = _combine(dest0, dest1, h, mf, gfin, y_sorted, row0=0, n_rows=np_rows)
    y_s = _combine(dest0, dest1, h, mf, gfin, y_sorted, row0=np_rows, n_rows=ns_rows)

    ut = utail.reshape(m // CHUNK, SUBLANES, CONV_CH)
    tails = ut[:, SUBLANES - (CONV_W - 1):, :]
    p_last = (jnp.arange(bp) + 1) * (seq_p // CHUNK) - 1
    s_last = np_rows // CHUNK + (jnp.arange(bs) + 1) * (seq_s // CHUNK) - 1
    return (y_p.reshape(bp, seq_p, D_MODEL),
            y_s.reshape(bs, seq_s, D_MODEL),
            ckv[:np_rows].reshape(1, bp, seq_p, KV_LORA),
            krope[:np_rows].reshape(1, bp, seq_p, ROPE_DIM),
            tails[p_last][None],
            ckv[np_rows:].reshape(1, bs, seq_s, KV_LORA),
            krope[np_rows:].reshape(1, bs, seq_s, ROPE_DIM),
            tails[s_last][None])
```

```python
import functools

import jax
import jax.numpy as jnp
from jax import lax
from jax.experimental import pallas as pl
from jax.experimental.pallas import tpu as pltpu

F32 = jnp.float32
BF16 = jnp.bfloat16

D_MODEL = 2048
N_HEADS = 8
QK_NOPE = 128
ROPE_DIM = 64
V_DIM = 128
Q_LORA = 512
KV_LORA = 512
ATTN_W = N_HEADS * V_DIM
CONV_CH = D_MODEL - ATTN_W
CONV_W = 3
CHUNK = 64
N_GROUPS = 4
EXPERTS_PER_GROUP = 8
N_EXPERTS = N_GROUPS * EXPERTS_PER_GROUP
D_FF = 512
ROPE_THETA = 10000.0
EPS = 1e-6
ATTN_SCALE = (QK_NOPE + ROPE_DIM) ** -0.5
EXP2_SCALE = ATTN_SCALE * 1.4426950408889634

LANES = 128
SUBLANES = 8
TM = 256
MOE_BLOCK = 128
TQ = 128
TK = 256
NEG_BIG = -1e30
VMEM_LIMIT = 56 * 1024 * 1024


def _rms(v, g):
    return v * lax.rsqrt(jnp.mean(v * v, axis=-1, keepdims=True) + EPS) * g


def _const_spec(shape):
    nd = len(shape)
    return pl.BlockSpec(shape, lambda *_: (0,) * nd, pipeline_mode=pl.Buffered(1))


def _in_proj_kernel(xp_ref, xs_ref, gmix_ref, wa_ref, wc_ref, gq_ref, gkv_ref, gco_ref, convw_ref,
                    cos_ref, sin_ref, state_ref,
                    cqn_ref, ckv_ref, kr_ref, convn_ref, utail_ref, ext_ref,
                    *, n_prompt_tiles, tiles_per_seq, n_prompt_seq, sample_seq_len):
    i = pl.program_id(0)

    def conv_block(u_sub, gate_sub, row0, length):
        ext_ref[SUBLANES:SUBLANES + length, :] = u_sub
        um1 = ext_ref[SUBLANES - 1:SUBLANES - 1 + length, :]
        um2 = ext_ref[SUBLANES - 2:SUBLANES - 2 + length, :]
        cw = convw_ref[...]
        conv = cw[0:1] * um2 + cw[1:2] * um1 + cw[2:3] * u_sub
        convn_ref[row0:row0 + length, :] = _rms(gate_sub * conv, gco_ref[...]).astype(BF16)

    def tile(x_ref, is_prompt):
        x = x_ref[...]
        xn = _rms(x, gmix_ref[...]).astype(BF16)
        za = jnp.dot(xn, wa_ref[...], preferred_element_type=F32)
        cqn_ref[...] = _rms(za[:, :Q_LORA], gq_ref[...]).astype(BF16)
        ckv_ref[...] = _rms(za[:, Q_LORA:Q_LORA + KV_LORA], gkv_ref[...])
        zk = za[:, Q_LORA + KV_LORA:]
        kr_ref[...] = zk[:, :ROPE_DIM] * cos_ref[...] + zk[:, ROPE_DIM:] * sin_ref[...]

        zc = jnp.dot(xn, wc_ref[...], preferred_element_type=F32)
        gate_b = zc[:, :CONV_CH]
        u = zc[:, CONV_CH:2 * CONV_CH] * zc[:, 2 * CONV_CH:]
        for j in range(TM // CHUNK):
            utail_ref[j] = u[CHUNK * (j + 1) - SUBLANES:CHUNK * (j + 1), :]

        if is_prompt:
            first = (i % tiles_per_seq) == 0

            @pl.when(first)
            def _():
                ext_ref[SUBLANES - 2:SUBLANES, :] = state_ref[i // tiles_per_seq]

            @pl.when(jnp.logical_not(first))
            def _():
                ext_ref[SUBLANES - 2:SUBLANES, :] = ext_ref[TM + SUBLANES - 2:TM + SUBLANES, :]

            conv_block(u, gate_b, 0, TM)
        else:
            n_sub = TM // sample_seq_len
            seq0 = n_prompt_seq + (i - n_prompt_tiles) * n_sub
            for k in range(n_sub):
                ext_ref[SUBLANES - 2:SUBLANES, :] = state_ref[seq0 + k]
                lo = k * sample_seq_len
                conv_block(u[lo:lo + sample_seq_len], gate_b[lo:lo + sample_seq_len], lo, sample_seq_len)

    @pl.when(i < n_prompt_tiles)
    def _():
        tile(xp_ref, True)

    @pl.when(i >= n_prompt_tiles)
    def _():
        tile(xs_ref, False)


def _in_proj(xp, xs, gmix, w_a, w_c, gq, gkv, gco, convw, cosk, sink, state, *, seq_p, seq_s):
    np_rows, ns_rows = xp.shape[0], xs.shape[0]
    m = np_rows + ns_rows
    npt, nst = np_rows // TM, ns_rows // TM
    tps = seq_p // TM
    n_prompt_seq = np_rows // seq_p
    last_p = npt - 1

    def tab_idx(i):
        return (jnp.where(i < npt, i % tps, tps), 0)

    row = lambda i: (i, 0)
    kern = functools.partial(_in_proj_kernel, n_prompt_tiles=npt, tiles_per_seq=tps,
                             n_prompt_seq=n_prompt_seq, sample_seq_len=seq_s)
    return pl.pallas_call(
        kern,
        grid=(npt + nst,),
        in_specs=[
            pl.BlockSpec((TM, D_MODEL), lambda i: (jnp.minimum(i, last_p), 0)),
            pl.BlockSpec((TM, D_MODEL), lambda i: (jnp.maximum(i - npt, 0), 0)),
            _const_spec((1, D_MODEL)),
            _const_spec(w_a.shape),
            _const_spec(w_c.shape),
            _const_spec((1, Q_LORA)),
            _const_spec((1, KV_LORA)),
            _const_spec((1, CONV_CH)),
            _const_spec((CONV_W, CONV_CH)),
            pl.BlockSpec((TM, ROPE_DIM), tab_idx),
            pl.BlockSpec((TM, ROPE_DIM), tab_idx),
            _const_spec(state.shape),
        ],
        out_specs=[
            pl.BlockSpec((TM, Q_LORA), row),
            pl.BlockSpec((TM, KV_LORA), row),
            pl.BlockSpec((TM, ROPE_DIM), row),
            pl.BlockSpec((TM, CONV_CH), row),
            pl.BlockSpec((TM // CHUNK, SUBLANES, CONV_CH), lambda i: (i, 0, 0)),
        ],
        out_shape=[
            jax.ShapeDtypeStruct((m, Q_LORA), BF16),
            jax.ShapeDtypeStruct((m, KV_LORA), F32),
            jax.ShapeDtypeStruct((m, ROPE_DIM), F32),
            jax.ShapeDtypeStruct((m, CONV_CH), BF16),
            jax.ShapeDtypeStruct((m // CHUNK, SUBLANES, CONV_CH), F32),
        ],
        scratch_shapes=[pltpu.VMEM((TM + SUBLANES, CONV_CH), F32)],
        compiler_params=pltpu.CompilerParams(dimension_semantics=("arbitrary",),
                                             vmem_limit_bytes=VMEM_LIMIT),
        name="in_proj",
    )(xp, xs, gmix, w_a, w_c, gq, gkv, gco, convw, cosk, sink, state)


def _attn_kernel(*refs, tq, n_past, causal):
    refs = list(refs)
    cqn_ref, wq_ref, wuk_ref, wuv_ref, cos_ref, sin_ref, gao_ref = refs[:7]
    refs = refs[7:]
    if n_past:
        pkv_ref, pkr_ref = refs[:2]
        refs = refs[2:]
    kv_ref, kr_ref, out_ref, qlat_ref, qr_ref, m_ref, l_ref, acc_ref = refs

    qi = pl.program_id(1)
    rows = N_HEADS * tq

    q = jnp.dot(cqn_ref[...], wq_ref[...], preferred_element_type=F32)
    nope_w = N_HEADS * QK_NOPE
    rope_w = N_HEADS * ROPE_DIM
    qrope = q[:, nope_w:nope_w + rope_w] * cos_ref[...] + q[:, nope_w + rope_w:] * sin_ref[...]
    for h in range(N_HEADS):
        qn = q[:, h * QK_NOPE:(h + 1) * QK_NOPE].astype(BF16)
        ql = jnp.dot(qn, wuk_ref[h], preferred_element_type=F32)
        qlat_ref[h * tq:(h + 1) * tq, :] = ql.astype(BF16)
        qr_ref[h * tq:(h + 1) * tq, :] = qrope[:, h * ROPE_DIM:(h + 1) * ROPE_DIM].astype(BF16)

    m_ref[...] = jnp.full(m_ref.shape, NEG_BIG, F32)
    l_ref[...] = jnp.zeros(l_ref.shape, F32)
    acc_ref[...] = jnp.zeros(acc_ref.shape, F32)

    nt = (((1,), (1,)), ((), ()))

    def step(kc_f32, kr_f32, mask):
        kc = kc_f32.astype(BF16)
        kr = kr_f32.astype(BF16)
        s = lax.dot_general(qlat_ref[...], kc, nt, preferred_element_type=F32)
        s = s + lax.dot_general(qr_ref[...], kr, nt, preferred_element_type=F32)
        if mask is not None:
            s = jnp.where(mask, s, NEG_BIG)
        m_prev = m_ref[...]
        m_new = jnp.maximum(m_prev, jnp.max(s, axis=-1, keepdims=True))
        alpha = jnp.exp2((m_prev - m_new) * EXP2_SCALE)
        p = jnp.exp2((s - m_new) * EXP2_SCALE)
        l_ref[...] = alpha * l_ref[...] + jnp.sum(p, axis=-1, keepdims=True)
        acc_ref[...] = alpha * acc_ref[...] + jnp.dot(p.astype(BF16), kc, preferred_element_type=F32)
        m_ref[...] = m_new

    if n_past:
        def past_body(j, c):
            k0 = pl.multiple_of(j * TK, TK)
            step(pkv_ref[pl.ds(k0, TK), :], pkr_ref[pl.ds(k0, TK), :], None)
            return c
        lax.fori_loop(0, n_past // TK, past_body, 0)

    if causal:
        n_blocks = ((qi + 1) * tq + TK - 1) // TK
        n_full = jnp.minimum((qi * tq // CHUNK + 1) * CHUNK // TK, n_blocks)

        def full_body(j, c):
            k0 = pl.multiple_of(j * TK, TK)
            step(kv_ref[pl.ds(k0, TK), :], kr_ref[pl.ds(k0, TK), :], None)
            return c

        def masked_body(j, c):
            k0 = pl.multiple_of(j * TK, TK)
            r = lax.broadcasted_iota(jnp.int32, (rows, TK), 0)
            cidx = lax.broadcasted_iota(jnp.int32, (rows, TK), 1)
            q_chunk = (qi * tq + (r % tq)) // CHUNK
            k_chunk = (k0 + cidx) // CHUNK
            step(kv_ref[pl.ds(k0, TK), :], kr_ref[pl.ds(k0, TK), :], q_chunk >= k_chunk)
            return c

        lax.fori_loop(0, n_full, full_body, 0)
        lax.fori_loop(n_full, n_blocks, masked_body, 0)
    else:
        step(kv_ref[...], kr_ref[...], None)

    o = acc_ref[...] / l_ref[...]
    parts = []
    for h in range(N_HEADS):
        oh = o[h * tq:(h + 1) * tq, :].astype(BF16)
        parts.append(jnp.dot(oh, wuv_ref[h], preferred_element_type=F32))
    attn = jnp.concatenate(parts, axis=-1)
    out_ref[...] = _rms(attn, gao_ref[...]).astype(BF16)


def _attention(cqn, w_q, w_ukt, w_uv, cosq, sinq, gao, ckv, krope, *, n_batch, seq, row0,
               past_kv=None, past_kr=None):
    causal = past_kv is None
    tq = TQ if causal else seq
    nq = seq // tq
    n_past = 0 if causal else past_kv.shape[1]
    if not causal:
        assert n_past % CHUNK == 0 and seq <= CHUNK and n_past % TK == 0
    blk0 = row0 // tq
    qrow = lambda b, q: (blk0 + b * nq + q, 0)
    in_specs = [
        pl.BlockSpec((tq, Q_LORA), qrow),
        _const_spec(w_q.shape),
        _const_spec(w_ukt.shape),
        _const_spec(w_uv.shape),
        pl.BlockSpec((tq, N_HEADS * ROPE_DIM), lambda b, q: (q, 0)),
        pl.BlockSpec((tq, N_HEADS * ROPE_DIM), lambda b, q: (q, 0)),
        _const_spec((1, ATTN_W)),
    ]
    args = [cqn, w_q, w_ukt, w_uv, cosq, sinq, gao]
    if n_past:
        in_specs += [pl.BlockSpec((None, n_past, KV_LORA), lambda b, q: (b, 0, 0)),
                     pl.BlockSpec((None, n_past, ROPE_DIM), lambda b, q: (b, 0, 0))]
        args += [past_kv, past_kr]
    sblk0 = row0 // seq
    in_specs += [pl.BlockSpec((seq, KV_LORA), lambda b, q: (sblk0 + b, 0)),
                 pl.BlockSpec((seq, ROPE_DIM), lambda b, q: (sblk0 + b, 0))]
    args += [ckv, krope]
    rows = N_HEADS * tq
    kern = functools.partial(_attn_kernel, tq=tq, n_past=n_past, causal=causal)
    return pl.pallas_call(
        kern,
        grid=(n_batch, nq),
        in_specs=in_specs,
        out_specs=pl.BlockSpec((tq, ATTN_W), lambda b, q: (b * nq + q, 0)),
        out_shape=jax.ShapeDtypeStruct((n_batch * seq, ATTN_W), BF16),
        scratch_shapes=[
            pltpu.VMEM((rows, KV_LORA), BF16),
            pltpu.VMEM((rows, ROPE_DIM), BF16),
            pltpu.VMEM((rows, 1), F32),
            pltpu.VMEM((rows, 1), F32),
            pltpu.VMEM((rows, KV_LORA), F32),
        ],
        compiler_params=pltpu.CompilerParams(dimension_semantics=("arbitrary", "arbitrary"),
                                             vmem_limit_bytes=VMEM_LIMIT),
        name="attn_prompt" if causal else "attn_sample",
    )(*args)


def _out_proj_kernel(attnp_ref, attns_ref, convn_ref, xp_ref, xs_ref, woa_ref, woc_ref, gffn_ref, wrh_ref, wrl_ref,
                     br_ref, h_ref, xpk_ref, mi_ref, mf_ref, cnt_ref, carry_ref, *, n_prompt_tiles):
    i = pl.program_id(0)

    @pl.when(i == 0)
    def _():
        carry_ref[...] = jnp.zeros(carry_ref.shape, F32)

    def tile(x_ref, attn_ref):
        y = jnp.dot(attn_ref[...], woa_ref[...], preferred_element_type=F32)
        y = y + jnp.dot(convn_ref[...], woc_ref[...], preferred_element_type=F32)
        h = x_ref[...] + y
        h_ref[...] = h
        xn = _rms(h, gffn_ref[...])

        half = D_MODEL // 2
        xh = xn.astype(BF16)
        xh32 = xh.astype(F32)
        lo = lax.bitcast_convert_type(xh32[:, :half], jnp.uint32)
        hi = lax.bitcast_convert_type(xh32[:, half:], jnp.uint32)
        xpk_ref[...] = (lo >> 16) | (hi & jnp.uint32(0xFFFF0000))

        xl = (xn - xh32).astype(BF16)
        logits = jnp.dot(xh, wrh_ref[...], preferred_element_type=F32)
        logits = logits + (jnp.dot(xl, wrh_ref[...], preferred_element_type=F32)
                           + jnp.dot(xh, wrl_ref[...], preferred_element_type=F32))
        logits = logits + br_ref[...]
        lane = lax.broadcasted_iota(jnp.int32, (TM, LANES), 1).astype(F32)
        ninf = -jnp.inf
        far = float(LANES)

        def first_argmax(v):
            vmax = jnp.max(v, axis=-1, keepdims=True)
            return vmax, jnp.min(jnp.where(v == vmax, lane, far), axis=-1, keepdims=True)

        gl = jnp.where(lane < N_GROUPS, logits, ninf)
        gmax, gidx = first_argmax(gl)
        g_p = 1.0 / jnp.sum(jnp.exp(gl - gmax), axis=-1, keepdims=True)
        e_lo = N_GROUPS + EXPERTS_PER_GROUP * gidx
        el = jnp.where((lane >= e_lo) & (lane < e_lo + EXPERTS_PER_GROUP), logits, ninf)
        e1max, i1 = first_argmax(el)
        z = jnp.sum(jnp.exp(el - e1max), axis=-1, keepdims=True)
        el2 = jnp.where(lane == i1, ninf, el)
        e2max, i2 = first_argmax(el2)
        p1 = 1.0 / z
        p2 = jnp.exp(e2max - e1max) / z
        den = p1 + p2
        g0 = g_p * p1 / den
        g1 = g_p * p2 / den
        e0 = i1 - N_GROUPS
        e1 = i2 - N_GROUPS

        oh0 = lane == e0
        oh1 = lane == e1
        oh = jnp.where(oh0 | oh1, 1.0, 0.0)
        r = lax.broadcasted_iota(jnp.int32, (TM, TM), 0)
        c = lax.broadcasted_iota(jnp.int32, (TM, TM), 1)
        ltri = jnp.where(r > c, 1.0, 0.0).astype(BF16)
        before = jnp.dot(ltri, oh.astype(BF16), preferred_element_type=F32) + carry_ref[...]
        rank0 = jnp.sum(jnp.where(oh0, before, 0.0), axis=-1, keepdims=True)
        rank1 = jnp.sum(jnp.where(oh1, before, 0.0), axis=-1, keepdims=True)
        total = carry_ref[...] + jnp.sum(oh, axis=0, keepdims=True)
        carry_ref[...] = total
        cnt_ref[...] = jnp.broadcast_to(total, cnt_ref.shape)

        mi = jnp.where(lane == 0, e0, jnp.where(lane == 1, e1, jnp.where(lane == 2, rank0, rank1)))
        mi_ref[...] = jnp.transpose(mi)[:SUBLANES, :].astype(jnp.int32)
        mf_ref[...] = jnp.where(lane == 0, g0, g1)

    @pl.when(i < n_prompt_tiles)
    def _():
        tile(xp_ref, attnp_ref)

    @pl.when(i >= n_prompt_tiles)
    def _():
        tile(xs_ref, attns_ref)


def _out_proj(attn_p, attn_s, conv_n, xp, xs, w_oa, w_oc, gffn, w_rh, w_rl, b_r):
    m = conv_n.shape[0]
    npt = xp.shape[0] // TM
    last_p = npt - 1
    row = lambda i: (i, 0)
    return pl.pallas_call(
        functools.partial(_out_proj_kernel, n_prompt_tiles=npt),
        grid=(m // TM,),
        in_specs=[
            pl.BlockSpec((TM, ATTN_W), lambda i: (jnp.minimum(i, last_p), 0)),
            pl.BlockSpec((TM, ATTN_W), lambda i: (jnp.maximum(i - npt, 0), 0)),
            pl.BlockSpec((TM, CONV_CH), row),
            pl.BlockSpec((TM, D_MODEL), lambda i: (jnp.minimum(i, last_p), 0)),
            pl.BlockSpec((TM, D_MODEL), lambda i: (jnp.maximum(i - npt, 0), 0)),
            _const_spec(w_oa.shape),
            _const_spec(w_oc.shape),
            _const_spec((1, D_MODEL)),
            _const_spec(w_rh.shape),
            _const_spec(w_rl.shape),
            _const_spec((1, LANES)),
        ],
        out_specs=[
            pl.BlockSpec((TM, D_MODEL), row),
            pl.BlockSpec((TM, D_MODEL // 2), row),
            pl.BlockSpec((SUBLANES, TM), lambda i: (0, i)),
            pl.BlockSpec((TM, LANES), row),
            pl.BlockSpec((SUBLANES, LANES), lambda i: (0, 0)),
        ],
        out_shape=[
            jax.ShapeDtypeStruct((m, D_MODEL), F32),
            jax.ShapeDtypeStruct((m, D_MODEL // 2), jnp.uint32),
            jax.ShapeDtypeStruct((SUBLANES, m), jnp.int32),
            jax.ShapeDtypeStruct((m, LANES), F32),
            jax.ShapeDtypeStruct((SUBLANES, LANES), F32),
        ],
        scratch_shapes=[pltpu.VMEM((1, LANES), F32)],
        compiler_params=pltpu.CompilerParams(dimension_semantics=("arbitrary",),
                                             vmem_limit_bytes=VMEM_LIMIT),
        name="out_proj",
    )(attn_p, attn_s, conv_n, xp, xs, w_oa, w_oc, gffn, w_rh, w_rl, b_r)


def _dispatch_kernel(d0_ref, d1_ref, zlo_ref, zn_ref, nu_ref, xpk_ref, xs_hbm, zeros_ref, sems, *, n_blocks):
    i = pl.program_id(0)
    sem = sems.at[0]
    zsem = sems.at[1]

    def row_copy(src_ref, src_row, dst_row):
        return pltpu.make_async_copy(src_ref.at[pl.ds(src_row, 1)], xs_hbm.at[pl.ds(dst_row, 1)], sem)

    def zero_fill(act):
        def per_expert(e, c):
            lo = zlo_ref[e]
            n = zn_ref[e]
            head = (-lo) & (SUBLANES - 1)
            for r in range(SUBLANES - 1):
                @pl.when(r < head)
                def _(r=r):
                    act(pltpu.make_async_copy(zeros_ref.at[pl.ds(0, 1)], xs_hbm.at[pl.ds(lo + r, 1)], zsem))
            off = lo + head
            rest = n - head
            size = MOE_BLOCK // 2
            while size >= SUBLANES:
                @pl.when((rest & size) != 0)
                def _(off=off, size=size):
                    dst = xs_hbm.at[pl.ds(pl.multiple_of(off, SUBLANES), size)]
                    act(pltpu.make_async_copy(zeros_ref.at[pl.ds(0, size)], dst, zsem))
                off = off + (rest & size)
                size //= 2
            return c

        def per_block(b, c):
            dst = xs_hbm.at[pl.ds(pl.multiple_of(b * MOE_BLOCK, MOE_BLOCK), MOE_BLOCK)]
            act(pltpu.make_async_copy(zeros_ref, dst, zsem))
            return c

        lax.fori_loop(0, N_EXPERTS, per_expert, 0)
        lax.fori_loop(nu_ref[0], n_blocks, per_block, 0)

    @pl.when(i == 0)
    def _():
        zeros_ref[...] = jnp.zeros(zeros_ref.shape, zeros_ref.dtype)
        zero_fill(lambda cp: cp.start())
        zero_fill(lambda cp: cp.wait())

    base = i * TM

    def start(r, c):
        row_copy(xpk_ref, r, d0_ref[base + r]).start()
        row_copy(xpk_ref, r, d1_ref[base + r]).start()
        return c

    lax.fori_loop(0, TM, start, 0)
    for _ in range(2):
        pltpu.make_async_copy(xpk_ref, xs_hbm.at[pl.ds(0, TM)], sem).wait()


def _dispatch(dest0, dest1, pad_lo, n_pad, n_used, xpk, n_blocks):
    m = xpk.shape[0]
    grid_spec = pltpu.PrefetchScalarGridSpec(
        num_scalar_prefetch=5,
        grid=(m // TM,),
        in_specs=[pl.BlockSpec((TM, D_MODEL // 2), lambda i, *_: (i, 0))],
        out_specs=pl.BlockSpec(memory_space=pl.ANY),
        scratch_shapes=[pltpu.VMEM((MOE_BLOCK, D_MODEL // 2), jnp.uint32),
                        pltpu.SemaphoreType.DMA((2,))],
    )
    return pl.pallas_call(
        functools.partial(_dispatch_kernel, n_blocks=n_blocks),
        grid_spec=grid_spec,
        out_shape=jax.ShapeDtypeStruct((n_blocks * MOE_BLOCK, D_MODEL // 2), jnp.uint32),
        compiler_params=pltpu.CompilerParams(dimension_semantics=("arbitrary",)),
        name="dispatch",
    )(dest0, dest1, pad_lo, n_pad, n_used, xpk)


def _experts_kernel(be_ref, nu_ref, x_ref, wg_ref, wu_ref, wd_ref, y_ref, wgb_ref, wub_ref, wdb_ref):
    b = pl.program_id(0)
    active = b < nu_ref[0]
    new_expert = jnp.logical_or(b == 0, be_ref[b] != be_ref[jnp.maximum(b - 1, 0)])

    @pl.when(jnp.logical_and(active, new_expert))
    def _():
        wgb_ref[...] = wg_ref[...].astype(BF16)
        wub_ref[...] = wu_ref[...].astype(BF16)
        wdb_ref[...] = wd_ref[...].astype(BF16)

    @pl.when(active)
    def _():
        half = D_MODEL // 2
        xw = x_ref[...]
        xa = lax.bitcast_convert_type(xw << 16, F32).astype(BF16)
        xb = lax.bitcast_convert_type(xw & jnp.uint32(0xFFFF0000), F32).astype(BF16)
        g = jnp.dot(xa, wgb_ref[:half, :], preferred_element_type=F32)
        g = g + jnp.dot(xb, wgb_ref[half:, :], preferred_element_type=F32)
        u = jnp.dot(xa, wub_ref[:half, :], preferred_element_type=F32)
        u = u + jnp.dot(xb, wub_ref[half:, :], preferred_element_type=F32)
        hmid = (g * jax.nn.sigmoid(g)) * u
        y_ref[...] = jnp.dot(hmid.astype(BF16), wdb_ref[...], preferred_element_type=F32)

    @pl.when(b >= nu_ref[0])
    def _():
        y_ref[...] = jnp.zeros(y_ref.shape, y_ref.dtype)


def _experts(block_e, n_used, x_sorted, w_gate, w_up, w_down):
    p = x_sorted.shape[0]
    nb = p // MOE_BLOCK

    def xrow(b, be, nu):
        return (jnp.minimum(b, nu[0] - 1), 0)

    def wsel(b, be, nu):
        return (be[b], 0, 0)

    grid_spec = pltpu.PrefetchScalarGridSpec(
        num_scalar_prefetch=2,
        grid=(nb,),
        in_specs=[
            pl.BlockSpec((MOE_BLOCK, D_MODEL // 2), xrow),
            pl.BlockSpec((None, D_MODEL, D_FF), wsel),
            pl.BlockSpec((None, D_MODEL, D_FF), wsel),
            pl.BlockSpec((None, D_FF, D_MODEL), wsel),
        ],
        out_specs=pl.BlockSpec((MOE_BLOCK, D_MODEL), lambda b, be, nu: (b, 0)),
        scratch_shapes=[pltpu.VMEM((D_MODEL, D_FF), BF16), pltpu.VMEM((D_MODEL, D_FF), BF16),
                        pltpu.VMEM((D_FF, D_MODEL), BF16)],
    )
    return pl.pallas_call(
        _experts_kernel,
        grid_spec=grid_spec,
        out_shape=jax.ShapeDtypeStruct((p, D_MODEL), F32),
        compiler_params=pltpu.CompilerParams(dimension_semantics=("arbitrary",),
                                             vmem_limit_bytes=VMEM_LIMIT),
        name="experts",
    )(block_e, n_used, x_sorted, w_gate, w_up, w_down)


def _combine_kernel(d0_ref, d1_ref, h_ref, mf_ref, gfin_ref, y_hbm, out_ref, y0_ref, y1_ref, sem, *, row0):
    i = pl.program_id(0)
    base = row0 + i * TM

    def copies(r):
        return (pltpu.make_async_copy(y_hbm.at[pl.ds(d0_ref[base + r], 1)], y0_ref.at[pl.ds(r, 1)], sem),
                pltpu.make_async_copy(y_hbm.at[pl.ds(d1_ref[base + r], 1)], y1_ref.at[pl.ds(r, 1)], sem))

    def start(r, c):
        for cp in copies(r):
            cp.start()
        return c

    lax.fori_loop(0, TM, start, 0)
    pltpu.make_async_copy(y_hbm.at[pl.ds(0, TM)], y0_ref, sem).wait()
    pltpu.make_async_copy(y_hbm.at[pl.ds(0, TM)], y1_ref, sem).wait()
    mf = mf_ref[...]
    ffn = mf[:, 0:1] * y0_ref[...] + mf[:, 1:2] * y1_ref[...]
    out_ref[...] = _rms(h_ref[...] + ffn, gfin_ref[...])


def _combine(dest0, dest1, h, mf, gfin, y_sorted, *, row0, n_rows):
    blk0 = row0 // TM
    grid_spec = pltpu.PrefetchScalarGridSpec(
        num_scalar_prefetch=2,
        grid=(n_rows // TM,),
        in_specs=[
            pl.BlockSpec((TM, D_MODEL), lambda i, *_: (blk0 + i, 0)),
            pl.BlockSpec((TM, LANES), lambda i, *_: (blk0 + i, 0)),
            pl.BlockSpec((1, D_MODEL), lambda i, *_: (0, 0)),
            pl.BlockSpec(memory_space=pl.ANY),
        ],
        out_specs=pl.BlockSpec((TM, D_MODEL), lambda i, *_: (i, 0)),
        scratch_shapes=[pltpu.VMEM((TM, D_MODEL), F32), pltpu.VMEM((TM, D_MODEL), F32),
                        pltpu.SemaphoreType.DMA(())],
    )
    return pl.pallas_call(
        functools.partial(_combine_kernel, row0=row0),
        grid_spec=grid_spec,
        out_shape=jax.ShapeDtypeStruct((n_rows, D_MODEL), F32),
        compiler_params=pltpu.CompilerParams(dimension_semantics=("arbitrary",),
                                             vmem_limit_bytes=VMEM_LIMIT),
        name="combine",
    )(dest0, dest1, h, mf, gfin, y_sorted)


def _rope_tables(pos):
    inv = ROPE_THETA ** (-jnp.arange(0, ROPE_DIM, 2, dtype=F32) / ROPE_DIM)
    ang = pos.astype(F32)[:, None] * inv[None, :]
    cos, sin = jnp.cos(ang), jnp.sin(ang)
    return jnp.concatenate([cos, cos], axis=-1), jnp.concatenate([-sin, sin], axis=-1)


def _swap_halves(w):
    return jnp.concatenate([w[..., ROPE_DIM // 2:], w[..., :ROPE_DIM // 2]], axis=-1)


def kernel(x_prompt, x_sample, cache_kv_latent, cache_k_rope, state_conv, norm_mix, w_in, norm_q, w_uq,
           norm_kv, w_uk, w_uv, conv_w, norm_attn_out, norm_conv_out, w_o, norm_ffn, w_router_group,
           b_router_group, w_router_expert, b_router_expert, w_gate, w_up, w_down, norm_final):
    assert w_in.shape[0] == 1, "single-layer trunk"
    bp, seq_p, _ = x_prompt.shape
    bs, seq_s, _ = x_sample.shape
    past_len = cache_kv_latent.shape[2]
    np_rows, ns_rows = bp * seq_p, bs * seq_s
    m = np_rows + ns_rows
    assert seq_p % TM == 0 and TM % seq_s == 0 and ns_rows % TM == 0 and seq_s == CHUNK

    xp = x_prompt.reshape(np_rows, D_MODEL)
    xs = x_sample.reshape(ns_rows, D_MODEL)
    row_vec = lambda v: v.reshape(1, -1)

    w_in0 = w_in[0]
    mla_w = Q_LORA + KV_LORA + ROPE_DIM
    w_a = jnp.concatenate([w_in0[:, :mla_w], _swap_halves(w_in0[:, mla_w - ROPE_DIM:mla_w])], axis=1).astype(BF16)
    w_c = w_in0[:, mla_w:].astype(BF16)
    wq4 = w_uq[0].reshape(Q_LORA, N_HEADS, QK_NOPE + ROPE_DIM)
    wq_rope = wq4[:, :, QK_NOPE:]
    w_q = jnp.concatenate([wq4[:, :, :QK_NOPE].reshape(Q_LORA, -1), wq_rope.reshape(Q_LORA, -1),
                           _swap_halves(wq_rope).reshape(Q_LORA, -1)], axis=1).astype(BF16)
    w_ukt = jnp.transpose(w_uk[0], (1, 2, 0)).astype(BF16)
    w_uvh = jnp.transpose(w_uv[0], (1, 0, 2)).astype(BF16)
    w_oa = w_o[0, :ATTN_W].astype(BF16)
    w_oc = w_o[0, ATTN_W:].astype(BF16)
    n_router = N_GROUPS + N_EXPERTS
    w_r = jnp.concatenate([w_router_group[0], w_router_expert[0].reshape(D_MODEL, N_EXPERTS)], axis=1)
    w_r = jnp.pad(w_r, ((0, 0), (0, LANES - n_router)))
    w_rh = w_r.astype(BF16)
    w_rl = (w_r - w_rh.astype(F32)).astype(BF16)
    b_r =jnp.pad(jnp.concatenate([b_router_group[0], b_router_expert[0].reshape(N_EXPERTS)]),
                  (0, LANES - n_router)).reshape(1, LANES)

    pos_p = jnp.arange(seq_p, dtype=jnp.int32)
    pos_s = past_len + jnp.arange(seq_s, dtype=jnp.int32)
    cos_p, sin_p = _rope_tables(pos_p)
    cos_s, sin_s = _rope_tables(pos_s)
    cosk = jnp.concatenate([cos_p, jnp.tile(cos_s, (TM // seq_s, 1))], axis=0)
    sink = jnp.concatenate([sin_p, jnp.tile(sin_s, (TM // seq_s, 1))], axis=0)
    state = jnp.concatenate([jnp.zeros((bp, CONV_W - 1, CONV_CH), F32), state_conv[0]], axis=0)

    cqn, ckv, krope, conv_n, utail = _in_proj(
        xp, xs, row_vec(norm_mix[0]), w_a, w_c, row_vec(norm_q[0]), row_vec(norm_kv[0]),
        row_vec(norm_conv_out[0]), conv_w[0], cosk, sink, state, seq_p=seq_p, seq_s=seq_s)

    gao = row_vec(norm_attn_out[0])
    attn_p = _attention(cqn, w_q, w_ukt, w_uvh, jnp.tile(cos_p, (1, N_HEADS)), jnp.tile(sin_p, (1, N_HEADS)),
                        gao, ckv, krope, n_batch=bp, seq=seq_p, row0=0)
    attn_s = _attention(cqn, w_q, w_ukt, w_uvh, jnp.tile(cos_s, (1, N_HEADS)), jnp.tile(sin_s, (1, N_HEADS)),
                        gao, ckv, krope, n_batch=bs, seq=seq_s, row0=np_rows,
                        past_kv=cache_kv_latent[0], past_kr=cache_k_rope[0])

    h, xpk, mi, mf, cnt = _out_proj(attn_p, attn_s, conv_n, xp, xs, w_oa, w_oc, row_vec(norm_ffn[0]),
                                    w_rh, w_rl, b_r)

    counts = cnt[0, :N_EXPERTS].astype(jnp.int32)
    padded = (counts + MOE_BLOCK - 1) // MOE_BLOCK * MOE_BLOCK
    pad_end = jnp.cumsum(padded)
    pad_start = pad_end - padded
    n_blocks = -(-(m * 2) // MOE_BLOCK) + N_EXPERTS
    block_row0 = jnp.arange(n_blocks, dtype=jnp.int32) * MOE_BLOCK
    block_e = jnp.minimum(jnp.sum((pad_end[None, :] <= block_row0[:, None]).astype(jnp.int32), axis=1),
                          N_EXPERTS - 1)
    n_used = (pad_end[-1:] // MOE_BLOCK).astype(jnp.int32)
    expert_ids = jnp.arange(N_EXPERTS, dtype=jnp.int32)[:, None]

    def seg_start(e):
        return jnp.sum(jnp.where(expert_ids == e[None, :], pad_start[:, None], 0), axis=0)

    dest0 = seg_start(mi[0]) + mi[2]
    dest1 = seg_start(mi[1]) + mi[3]

    x_sorted = _dispatch(dest0, dest1, pad_start + counts, padded - counts, n_used, xpk, n_blocks)
    y_sorted = _experts(block_e, n_used, x_sorted, w_gate[0], w_up[0], w_down[0])
    gfin = row_vec(norm_final)
    y_p = _combine(dest0, dest1, h, mf, gfin, y_sorted, row0=0, n_rows=np_rows)
    y_s = _combine(dest0, dest1, h, mf, gfin, y_sorted, row0=np_rows, n_rows=ns_rows)

    ut = utail.reshape(m // CHUNK, SUBLANES, CONV_CH)
    tails = ut[:, SUBLANES - (CONV_W - 1):, :]
    p_last = (jnp.arange(bp) + 1) * (seq_p // CHUNK) - 1
    s_last = np_rows // CHUNK + (jnp.arange(bs) + 1) * (seq_s // CHUNK) - 1
    return (y_p.reshape(bp, seq_p, D_MODEL),
            y_s.reshape(bs, seq_s, D_MODEL),
            ckv[:np_rows].reshape(1, bp, seq_p, KV_LORA),
            krope[:np_rows].reshape(1, bp, seq_p, ROPE_DIM),
            tails[p_last][None],
            ckv[np_rows:].reshape(1, bs, seq_s, KV_LORA),
            krope[np_rows:].reshape(1, bs, seq_s, ROPE_DIM),
            tails[s_last][None])
```

```python
import functools

import jax
import jax.numpy as jnp
from jax import lax
from jax.experimental import pallas as pl
from jax.experimental.pallas import tpu as pltpu

F32 = jnp.float32
BF16 = jnp.bfloat16

D_MODEL = 2048
N_HEADS = 8
QK_NOPE = 128
ROPE_DIM = 64
V_DIM = 128
Q_LORA = 512
KV_LORA = 512
ATTN_W = N_HEADS * V_DIM
CONV_CH = D_MODEL - ATTN_W
CONV_W = 3
CHUNK = 64
N_GROUPS = 4
EXPERTS_PER_GROUP = 8
N_EXPERTS = N_GROUPS * EXPERTS_PER_GROUP
D_FF = 512
ROPE_THETA = 10000.0
EPS = 1e-6
ATTN_SCALE = (QK_NOPE + ROPE_DIM) ** -0.5
EXP2_SCALE = ATTN_SCALE * 1.4426950408889634

LANES = 128
SUBLANES = 8
TM = 256
MOE_BLOCK = 128
TQ = 128
TK = 256
NEG_BIG = -1e30
VMEM_LIMIT = 56 * 1024 * 1024


def _rms(v, g):
    return v * lax.rsqrt(jnp.mean(v * v, axis=-1, keepdims=True) + EPS) * g


def _lane_bcast(v, width):
    if width % LANES == 0:
        return jnp.concatenate([v] * (width // LANES), axis=1)
    assert width < LANES
    return v[:, :width]


def _const_spec(shape):
    nd = len(shape)
    return pl.BlockSpec(shape, lambda *_: (0,) * nd, pipeline_mode=pl.Buffered(1))


def _in_proj_kernel(xp_ref, xs_ref, gmix_ref, wa_ref, wc_ref, gq_ref, gkv_ref, gco_ref, convw_ref,
                    cos_ref, sin_ref, state_ref,
                    cqn_ref, ckv_ref, kr_ref, convn_ref, utail_ref, ext_ref,
                    *, n_prompt_tiles, tiles_per_seq, n_prompt_seq, sample_seq_len):
    i = pl.program_id(0)

    def conv_block(u_sub, gate_sub, row0, length):
        ext_ref[SUBLANES:SUBLANES + length, :] = u_sub
        um1 = ext_ref[SUBLANES - 1:SUBLANES - 1 + length, :]
        um2 = ext_ref[SUBLANES - 2:SUBLANES - 2 + length, :]
        cw = convw_ref[...]
        conv = cw[0:1] * um2 + cw[1:2] * um1 + cw[2:3] * u_sub
        convn_ref[row0:row0 + length, :] = _rms(gate_sub * conv, gco_ref[...]).astype(BF16)

    def tile(x_ref, is_prompt):
        x = x_ref[...]
        xn = _rms(x, gmix_ref[...]).astype(BF16)
        za = jnp.dot(xn, wa_ref[...], preferred_element_type=F32)
        cqn_ref[...] = _rms(za[:, :Q_LORA], gq_ref[...]).astype(BF16)
        ckv_ref[...] = _rms(za[:, Q_LORA:Q_LORA + KV_LORA], gkv_ref[...])
        zk = za[:, Q_LORA + KV_LORA:]
        kr_ref[...] = zk[:, :ROPE_DIM] * cos_ref[...] + zk[:, ROPE_DIM:] * sin_ref[...]

        zc = jnp.dot(xn, wc_ref[...], preferred_element_type=F32)
        gate_b = zc[:, :CONV_CH]
        u = zc[:, CONV_CH:2 * CONV_CH] * zc[:, 2 * CONV_CH:]
        for j in range(TM // CHUNK):
            utail_ref[j] = u[CHUNK * (j + 1) - SUBLANES:CHUNK * (j + 1), :]

        if is_prompt:
            first = (i % tiles_per_seq) == 0

            @pl.when(first)
            def _():
                ext_ref[SUBLANES - 2:SUBLANES, :] = state_ref[i // tiles_per_seq]

            @pl.when(jnp.logical_not(first))
            def _():
                ext_ref[SUBLANES - 2:SUBLANES, :] = ext_ref[TM + SUBLANES - 2:TM + SUBLANES, :]

            conv_block(u, gate_b, 0, TM)
        else:
            n_sub = TM // sample_seq_len
            seq0 = n_prompt_seq + (i - n_prompt_tiles) * n_sub
            for k in range(n_sub):
                ext_ref[SUBLANES - 2:SUBLANES, :] = state_ref[seq0 + k]
                lo = k * sample_seq_len
                conv_block(u[lo:lo + sample_seq_len], gate_b[lo:lo + sample_seq_len], lo, sample_seq_len)

    @pl.when(i < n_prompt_tiles)
    def _():
        tile(xp_ref, True)

    @pl.when(i >= n_prompt_tiles)
    def _():
        tile(xs_ref, False)


def _in_proj(xp, xs, gmix, w_a, w_c, gq, gkv, gco, convw, cosk, sink, state, *, seq_p, seq_s):
    np_rows, ns_rows = xp.shape[0], xs.shape[0]
    m = np_rows + ns_rows
    npt, nst = np_rows // TM, ns_rows // TM
    tps = seq_p // TM
    n_prompt_seq = np_rows // seq_p
    last_p = npt - 1

    def tab_idx(i):
        return (jnp.where(i < npt, i % tps, tps), 0)

    row = lambda i: (i, 0)
    kern = functools.partial(_in_proj_kernel, n_prompt_tiles=npt, tiles_per_seq=tps,
                             n_prompt_seq=n_prompt_seq, sample_seq_len=seq_s)
    return pl.pallas_call(
        kern,
        grid=(npt + nst,),
        in_specs=[
            pl.BlockSpec((TM, D_MODEL), lambda i: (jnp.minimum(i, last_p), 0)),
            pl.BlockSpec((TM, D_MODEL), lambda i: (jnp.maximum(i - npt, 0), 0)),
            _const_spec((1, D_MODEL)),
            _const_spec(w_a.shape),
            _const_spec(w_c.shape),
            _const_spec((1, Q_LORA)),
            _const_spec((1, KV_LORA)),
            _const_spec((1, CONV_CH)),
            _const_spec((CONV_W, CONV_CH)),
            pl.BlockSpec((TM, ROPE_DIM), tab_idx),
            pl.BlockSpec((TM, ROPE_DIM), tab_idx),
            _const_spec(state.shape),
        ],
        out_specs=[
            pl.BlockSpec((TM, Q_LORA), row),
            pl.BlockSpec((TM, KV_LORA), row),
            pl.BlockSpec((TM, ROPE_DIM), row),
            pl.BlockSpec((TM, CONV_CH), row),
            pl.BlockSpec((TM // CHUNK, SUBLANES, CONV_CH), lambda i: (i, 0, 0)),
        ],
        out_shape=[
            jax.ShapeDtypeStruct((m, Q_LORA), BF16),
            jax.ShapeDtypeStruct((m, KV_LORA), F32),
            jax.ShapeDtypeStruct((m, ROPE_DIM), F32),
            jax.ShapeDtypeStruct((m, CONV_CH), BF16),
            jax.ShapeDtypeStruct((m // CHUNK, SUBLANES, CONV_CH), F32),
        ],
        scratch_shapes=[pltpu.VMEM((TM + SUBLANES, CONV_CH), F32)],
        compiler_params=pltpu.CompilerParams(dimension_semantics=("arbitrary",),
                                             vmem_limit_bytes=VMEM_LIMIT),
        name="in_proj",
    )(xp, xs, gmix, w_a, w_c, gq, gkv, gco, convw, cosk, sink, state)


def _attn_kernel(*refs, tq, n_past, causal):
    refs = list(refs)
    cqn_ref, wq_ref, wuk_ref, wuv_ref, cos_ref, sin_ref, gao_ref = refs[:7]
    refs = refs[7:]
    if n_past:
        pkv_ref, pkr_ref = refs[:2]
        refs = refs[2:]
    kv_ref, kr_ref, out_ref, qlat_ref, qr_ref, m_ref, l_ref, acc_ref, s_ref, klim_ref = refs

    qi = pl.program_id(1)
    rows = N_HEADS * tq

    q = jnp.dot(cqn_ref[...], wq_ref[...], preferred_element_type=F32)
    nope_w = N_HEADS * QK_NOPE
    rope_w = N_HEADS * ROPE_DIM
    qrope = q[:, nope_w:nope_w + rope_w] * cos_ref[...] + q[:, nope_w + rope_w:] * sin_ref[...]
    for h in range(N_HEADS):
        qn = q[:, h * QK_NOPE:(h + 1) * QK_NOPE].astype(BF16)
        ql = jnp.dot(qn, wuk_ref[h], preferred_element_type=F32)
        qlat_ref[h * tq:(h + 1) * tq, :] = ql.astype(BF16)
        qr_ref[h * tq:(h + 1) * tq, :] = qrope[:, h * ROPE_DIM:(h + 1) * ROPE_DIM].astype(BF16)

    m_ref[...] = jnp.full(m_ref.shape, NEG_BIG, F32)
    l_ref[...] = jnp.zeros(l_ref.shape, F32)
    acc_ref[...] = jnp.zeros(acc_ref.shape, F32)

    nt = (((1,), (1,)), ((), ()))

    def scores(kc_f32, kr_f32):
        s = lax.dot_general(qlat_ref[...], kc_f32.astype(BF16), nt, preferred_element_type=F32)
        return s + lax.dot_general(qr_ref[...], kr_f32.astype(BF16), nt, preferred_element_type=F32)

    def update(s, kc_f32, mask):
        if mask is not None:
            s = jnp.where(mask, s, NEG_BIG)
        m_prev = m_ref[...]
        m_new = jnp.maximum(m_prev, jnp.max(s, axis=-1, keepdims=True))
        alpha = jnp.exp2((m_prev - m_new) * EXP2_SCALE)
        p = jnp.exp2((s - _lane_bcast(m_new, s.shape[1])) * EXP2_SCALE)
        l_ref[...] = alpha * l_ref[...] + jnp.sum(p, axis=-1, keepdims=True)
        pv = jnp.dot(p.astype(BF16), kc_f32.astype(BF16), preferred_element_type=F32)
        acc_ref[...] = _lane_bcast(alpha, KV_LORA) * acc_ref[...] + pv
        m_ref[...] = m_new

    def pipelined(kv, kr, lo, hi, last, mask_fn):
        def body(j, c):
            k0 = pl.multiple_of(j * TK, TK)
            k1 = pl.multiple_of(jnp.minimum(j + 1, last) * TK, TK)
            s_cur = s_ref[j % 2]
            s_ref[(j + 1) % 2] = scores(kv[pl.ds(k1, TK), :], kr[pl.ds(k1, TK), :])
            update(s_cur, kv[pl.ds(k0, TK), :], None if mask_fn is None else mask_fn(k0))
            return c
        lax.fori_loop(lo, hi, body, 0)

    if n_past:
        n_pb = n_past // TK
        s_ref[0] = scores(pkv_ref[pl.ds(0, TK), :], pkr_ref[pl.ds(0, TK), :])
        pipelined(pkv_ref, pkr_ref, 0, n_pb, n_pb - 1, None)

    if causal:
        n_blocks = ((qi + 1) * tq + TK - 1) // TK
        n_full = jnp.minimum((qi * tq // CHUNK + 1) * CHUNK // TK, n_blocks)

        assert tq & (tq - 1) == 0 and CHUNK & (CHUNK - 1) == 0
        r = lax.broadcasted_iota(jnp.int32, (rows, LANES), 0)
        q_pos = qi * tq + (r & (tq - 1))
        klim_ref[...] = (q_pos & ~(CHUNK - 1)) + CHUNK

        def mask_fn(k0):
            cidx = lax.broadcasted_iota(jnp.int32, (rows, TK), 1)
            return cidx < _lane_bcast(klim_ref[...] - k0, TK)

        s_ref[0] = scores(kv_ref[pl.ds(0, TK), :], kr_ref[pl.ds(0, TK), :])
        pipelined(kv_ref, kr_ref, 0, n_full, n_blocks - 1, None)
        pipelined(kv_ref, kr_ref, n_full, n_blocks, n_blocks - 1, mask_fn)
    else:
        update(scores(kv_ref[...], kr_ref[...]), kv_ref[...], None)

    o = acc_ref[...] / _lane_bcast(l_ref[...], KV_LORA)
    parts = []
    for h in range(N_HEADS):
        oh = o[h * tq:(h + 1) * tq, :].astype(BF16)
        parts.append(jnp.dot(oh, wuv_ref[h], preferred_element_type=F32))
    attn = jnp.concatenate(parts, axis=-1)
    out_ref[...] = _rms(attn, gao_ref[...]).astype(BF16)


def _attention(cqn, w_q, w_ukt, w_uv, cosq, sinq, gao, ckv, krope, *, n_batch, seq, row0,
               past_kv=None, past_kr=None):
    causal = past_kv is None
    tq = TQ if causal else seq
    nq = seq // tq
    n_past = 0 if causal else past_kv.shape[1]
    if not causal:
        assert n_past % CHUNK == 0 and seq <= CHUNK and n_past % TK == 0
    blk0 = row0 // tq
    qrow = lambda b, q: (blk0 + b * nq + q, 0)
    in_specs = [
        pl.BlockSpec((tq, Q_LORA), qrow),
        _const_spec(w_q.shape),
        _const_spec(w_ukt.shape),
        _const_spec(w_uv.shape),
        pl.BlockSpec((tq, N_HEADS * ROPE_DIM), lambda b, q: (q, 0)),
        pl.BlockSpec((tq, N_HEADS * ROPE_DIM), lambda b, q: (q, 0)),
        _const_spec((1, ATTN_W)),
    ]
    args = [cqn, w_q, w_ukt, w_uv, cosq, sinq, gao]
    if n_past:
        in_specs += [pl.BlockSpec((None, n_past, KV_LORA), lambda b, q: (b, 0, 0)),
                     pl.BlockSpec((None, n_past, ROPE_DIM), lambda b, q: (b, 0, 0))]
        args += [past_kv, past_kr]
    sblk0 = row0 // seq
    in_specs += [pl.BlockSpec((seq, KV_LORA), lambda b, q: (sblk0 + b, 0)),
                 pl.BlockSpec((seq, ROPE_DIM), lambda b, q: (sblk0 + b, 0))]
    args += [ckv, krope]
    rows = N_HEADS * tq
    kern = functools.partial(_attn_kernel, tq=tq, n_past=n_past, causal=causal)
    return pl.pallas_call(
        kern,
        grid=(n_batch, nq),
        in_specs=in_specs,
        out_specs=pl.BlockSpec((tq, ATTN_W), lambda b, q: (b * nq + q, 0)),
        out_shape=jax.ShapeDtypeStruct((n_batch * seq, ATTN_W), BF16),
        scratch_shapes=[
            pltpu.VMEM((rows, KV_LORA), BF16),
            pltpu.VMEM((rows, ROPE_DIM), BF16),
            pltpu.VMEM((rows, LANES), F32),
            pltpu.VMEM((rows, LANES), F32),
            pltpu.VMEM((rows, KV_LORA), F32),
            pltpu.VMEM((2, rows, TK), F32),
            pltpu.VMEM((rows, LANES), jnp.int32),
        ],
        compiler_params=pltpu.CompilerParams(dimension_semantics=("arbitrary", "arbitrary"),
                                             vmem_limit_bytes=VMEM_LIMIT),
        name="attn_prompt" if causal else "attn_sample",
    )(*args)


def _out_proj_kernel(attnp_ref, attns_ref, convn_ref, xp_ref, xs_ref, woa_ref, woc_ref, gffn_ref, wrh_ref, wrl_ref,
                     br_ref, h_ref, xpk_ref, mi_ref, mf_ref, cnt_ref, carry_ref, *, n_prompt_tiles):
    i = pl.program_id(0)

    @pl.when(i == 0)
    def _():
        carry_ref[...] = jnp.zeros(carry_ref.shape, F32)

    def tile(x_ref, attn_ref):
        y = jnp.dot(attn_ref[...], woa_ref[...], preferred_element_type=F32)
        y = y + jnp.dot(convn_ref[...], woc_ref[...], preferred_element_type=F32)
        h = x_ref[...] + y
        h_ref[...] = h
        xn = _rms(h, gffn_ref[...])

        half = D_MODEL // 2
        xh = xn.astype(BF16)
        xh32 = xh.astype(F32)
        lo = lax.bitcast_convert_type(xh32[:, :half], jnp.uint32)
        hi = lax.bitcast_convert_type(xh32[:, half:], jnp.uint32)
        xpk_ref[...] = (lo >> 16) | (hi & jnp.uint32(0xFFFF0000))

        xl = (xn - xh32).astype(BF16)
        logits = jnp.dot(xh, wrh_ref[...], preferred_element_type=F32)
        logits = logits + (jnp.dot(xl, wrh_ref[...], preferred_element_type=F32)
                           + jnp.dot(xh, wrl_ref[...], preferred_element_type=F32))
        logits = logits + br_ref[...]
        lane = lax.broadcasted_iota(jnp.int32, (TM, LANES), 1).astype(F32)
        ninf = -jnp.inf
        far = float(LANES)

        def first_argmax(v):
            vmax = jnp.max(v, axis=-1, keepdims=True)
            return vmax, jnp.min(jnp.where(v == vmax, lane, far), axis=-1, keepdims=True)

        gl = jnp.where(lane < N_GROUPS, logits, ninf)
        gmax, gidx = first_argmax(gl)
        g_p = 1.0 / jnp.sum(jnp.exp(gl - gmax), axis=-1, keepdims=True)
        e_lo = N_GROUPS + EXPERTS_PER_GROUP * gidx
        el = jnp.where((lane >= e_lo) & (lane < e_lo + EXPERTS_PER_GROUP), logits, ninf)
        e1max, i1 = first_argmax(el)
        z = jnp.sum(jnp.exp(el - e1max), axis=-1, keepdims=True)
        el2 = jnp.where(lane == i1, ninf, el)
        e2max, i2 = first_argmax(el2)
        p1 = 1.0 / z
        p2 = jnp.exp(e2max - e1max) / z
        den = p1 + p2
        g0 = g_p * p1 / den
        g1 = g_p * p2 / den
        e0 = i1 - N_GROUPS
        e1 = i2 - N_GROUPS

        oh0 = lane == e0
        oh1 = lane == e1
        oh = jnp.where(oh0 | oh1, 1.0, 0.0)
        r = lax.broadcasted_iota(jnp.int32, (TM, TM), 0)
        c = lax.broadcasted_iota(jnp.int32, (TM, TM), 1)
        ltri = jnp.where(r > c, 1.0, 0.0).astype(BF16)
        before = jnp.dot(ltri, oh.astype(BF16), preferred_element_type=F32) + carry_ref[...]
        rank0 = jnp.sum(jnp.where(oh0, before, 0.0), axis=-1, keepdims=True)
        rank1 = jnp.sum(jnp.where(oh1, before, 0.0), axis=-1, keepdims=True)
        total = carry_ref[...] + jnp.sum(oh, axis=0, keepdims=True)
        carry_ref[...] = total
        cnt_ref[...] = jnp.broadcast_to(total, cnt_ref.shape)

        mi = jnp.where(lane == 0, e0, jnp.where(lane == 1, e1, jnp.where(lane == 2, rank0, rank1)))
        mi_ref[...] = jnp.transpose(mi)[:SUBLANES, :].astype(jnp.int32)
        mf_ref[...] = jnp.where(lane == 0, g0, g1)

    @pl.when(i < n_prompt_tiles)
    def _():
        tile(xp_ref, attnp_ref)

    @pl.when(i >= n_prompt_tiles)
    def _():
        tile(xs_ref, attns_ref)


def _out_proj(attn_p, attn_s, conv_n, xp, xs, w_oa, w_oc, gffn, w_rh, w_rl, b_r):
    m = conv_n.shape[0]
    npt = xp.shape[0] // TM
    last_p = npt - 1
    row = lambda i: (i, 0)
    return pl.pallas_call(
        functools.partial(_out_proj_kernel, n_prompt_tiles=npt),
        grid=(m // TM,),
        in_specs=[
            pl.BlockSpec((TM, ATTN_W), lambda i: (jnp.minimum(i, last_p), 0)),
            pl.BlockSpec((TM, ATTN_W), lambda i: (jnp.maximum(i - npt, 0), 0)),
            pl.BlockSpec((TM, CONV_CH), row),
            pl.BlockSpec((TM, D_MODEL), lambda i: (jnp.minimum(i, last_p), 0)),
            pl.BlockSpec((TM, D_MODEL), lambda i: (jnp.maximum(i - npt, 0), 0)),
            _const_spec(w_oa.shape),
            _const_spec(w_oc.shape),
            _const_spec((1, D_MODEL)),
            _const_spec(w_rh.shape),
            _const_spec(w_rl.shape),
            _const_spec((1, LANES)),
        ],
        out_specs=[
            pl.BlockSpec((TM, D_MODEL), row),
            pl.BlockSpec((TM, D_MODEL // 2), row),
            pl.BlockSpec((SUBLANES, TM), lambda i: (0, i)),
            pl.BlockSpec((TM, LANES), row),
            pl.BlockSpec((SUBLANES, LANES), lambda i: (0, 0)),
        ],
        out_shape=[
            jax.ShapeDtypeStruct((m, D_MODEL), F32),
            jax.ShapeDtypeStruct((m, D_MODEL // 2), jnp.uint32),
            jax.ShapeDtypeStruct((SUBLANES, m), jnp.int32),
            jax.ShapeDtypeStruct((m, LANES), F32),
            jax.ShapeDtypeStruct((SUBLANES, LANES), F32),
        ],
        scratch_shapes=[pltpu.VMEM((1, LANES), F32)],
        compiler_params=pltpu.CompilerParams(dimension_semantics=("arbitrary",),
                                             vmem_limit_bytes=VMEM_LIMIT),
        name="out_proj",
    )(attn_p, attn_s, conv_n, xp, xs, w_oa, w_oc, gffn, w_rh, w_rl, b_r)


def _dispatch_kernel(d0_ref, d1_ref, zlo_ref, zn_ref, nu_ref, xpk_ref, xs_hbm, zeros_ref, sems, *, n_blocks):
    i = pl.program_id(0)
    sem = sems.at[0]
    zsem = sems.at[1]

    def row_copy(src_ref, src_row, dst_row):
        return pltpu.make_async_copy(src_ref.at[pl.ds(src_row, 1)], xs_hbm.at[pl.ds(dst_row, 1)], sem)

    def zero_fill(act):
        def per_expert(e, c):
            lo = zlo_ref[e]
            n = zn_ref[e]
            head = (-lo) & (SUBLANES - 1)
            for r in range(SUBLANES - 1):
                @pl.when(r < head)
                def _(r=r):
                    act(pltpu.make_async_copy(zeros_ref.at[pl.ds(0, 1)], xs_hbm.at[pl.ds(lo + r, 1)], zsem))
            off = lo + head
            rest = n - head
            size = MOE_BLOCK // 2
            while size >= SUBLANES:
                @pl.when((rest & size) != 0)
                def _(off=off, size=size):
                    dst = xs_hbm.at[pl.ds(pl.multiple_of(off, SUBLANES), size)]
                    act(pltpu.make_async_copy(zeros_ref.at[pl.ds(0, size)], dst, zsem))
                off = off + (rest & size)
                size //= 2
            return c

        def per_block(b, c):
            dst = xs_hbm.at[pl.ds(pl.multiple_of(b * MOE_BLOCK, MOE_BLOCK), MOE_BLOCK)]
            act(pltpu.make_async_copy(zeros_ref, dst, zsem))
            return c

        lax.fori_loop(0, N_EXPERTS, per_expert, 0)
        lax.fori_loop(nu_ref[0], n_blocks, per_block, 0)

    @pl.when(i == 0)
    def _():
        zeros_ref[...] = jnp.zeros(zeros_ref.shape, zeros_ref.dtype)
        zero_fill(lambda cp: cp.start())
        zero_fill(lambda cp: cp.wait())

    base = i * TM

    def start(r, c):
        row_copy(xpk_ref, r, d0_ref[base + r]).start()
        row_copy(xpk_ref, r, d1_ref[base + r]).start()
        return c

    lax.fori_loop(0, TM, start, 0)
    for _ in range(2):
        pltpu.make_async_copy(xpk_ref, xs_hbm.at[pl.ds(0, TM)], sem).wait()


def _dispatch(dest0, dest1, pad_lo, n_pad, n_used, xpk, n_blocks):
    m = xpk.shape[0]
    grid_spec = pltpu.PrefetchScalarGridSpec(
        num_scalar_prefetch=5,
        grid=(m // TM,),
        in_specs=[pl.BlockSpec((TM, D_MODEL // 2), lambda i, *_: (i, 0))],
        out_specs=pl.BlockSpec(memory_space=pl.ANY),
        scratch_shapes=[pltpu.VMEM((MOE_BLOCK, D_MODEL // 2), jnp.uint32),
                        pltpu.SemaphoreType.DMA((2,))],
    )
    return pl.pallas_call(
        functools.partial(_dispatch_kernel, n_blocks=n_blocks),
        grid_spec=grid_spec,
        out_shape=jax.ShapeDtypeStruct((n_blocks * MOE_BLOCK, D_MODEL // 2), jnp.uint32),
        compiler_params=pltpu.CompilerParams(dimension_semantics=("arbitrary",)),
        name="dispatch",
    )(dest0, dest1, pad_lo, n_pad, n_used, xpk)


def _experts_kernel(be_ref, nu_ref, nxt_ref, x_ref, wg_hbm, wu_hbm, wd_hbm, y_ref,
                    sg_ref, su_ref, sd_ref, wgb_ref, wub_ref, wdb_ref, sems):
    b = pl.program_id(0)
    active = b < nu_ref[0]
    new_expert = jnp.logical_or(b == 0, be_ref[b] != be_ref[jnp.maximum(b - 1, 0)])

    def weight_copies(e):
        return (pltpu.make_async_copy(wg_hbm.at[e], sg_ref, sems.at[0]),
                pltpu.make_async_copy(wu_hbm.at[e], su_ref, sems.at[1]),
                pltpu.make_async_copy(wd_hbm.at[e], sd_ref, sems.at[2]))

    @pl.when(b == 0)
    def _():
        for cp in weight_copies(be_ref[0]):
            cp.start()

    @pl.when(jnp.logical_and(active, new_expert))
    def _():
        for cp in weight_copies(be_ref[b]):
            cp.wait()
        wgb_ref[...] = sg_ref[...].astype(BF16)
        wub_ref[...] = su_ref[...].astype(BF16)
        wdb_ref[...] = sd_ref[...].astype(BF16)

        @pl.when(nxt_ref[b] >= 0)
        def _():
            for cp in weight_copies(nxt_ref[b]):
                cp.start()

    @pl.when(active)
    def _():
        half = D_MODEL // 2
        xw = x_ref[...]
        xa = lax.bitcast_convert_type(xw << 16, F32).astype(BF16)
        xb = lax.bitcast_convert_type(xw & jnp.uint32(0xFFFF0000), F32).astype(BF16)
        g = jnp.dot(xa, wgb_ref[:half, :], preferred_element_type=F32)
        g = g + jnp.dot(xb, wgb_ref[half:, :], preferred_element_type=F32)
        u = jnp.dot(xa, wub_ref[:half, :], preferred_element_type=F32)
        u = u + jnp.dot(xb, wub_ref[half:, :], preferred_element_type=F32)
        hmid = (g * jax.nn.sigmoid(g)) * u
        y_ref[...] = jnp.dot(hmid.astype(BF16), wdb_ref[...], preferred_element_type=F32)

    @pl.when(b >= nu_ref[0])
    def _():
        y_ref[...] = jnp.zeros(y_ref.shape, y_ref.dtype)


def _experts(block_e, n_used, next_e, x_sorted, w_gate, w_up, w_down):
    p = x_sorted.shape[0]
    nb = p // MOE_BLOCK

    def xrow(b, be, nu, nxt):
        return (jnp.maximum(jnp.minimum(b, nu[0] - 1), 0), 0)

    grid_spec = pltpu.PrefetchScalarGridSpec(
        num_scalar_prefetch=3,
        grid=(nb,),
        in_specs=[
            pl.BlockSpec((MOE_BLOCK, D_MODEL // 2), xrow),
            pl.BlockSpec(memory_space=pl.ANY),
            pl.BlockSpec(memory_space=pl.ANY),
            pl.BlockSpec(memory_space=pl.ANY),
        ],
        out_specs=pl.BlockSpec((MOE_BLOCK, D_MODEL), lambda b, be, nu, nxt: (b, 0)),
        scratch_shapes=[pltpu.VMEM((D_MODEL, D_FF), F32), pltpu.VMEM((D_MODEL, D_FF), F32),
                        pltpu.VMEM((D_FF, D_MODEL), F32),
                        pltpu.VMEM((D_MODEL, D_FF), BF16), pltpu.VMEM((D_MODEL, D_FF), BF16),
                        pltpu.VMEM((D_FF, D_MODEL), BF16),
                        pltpu.SemaphoreType.DMA((3,))],
    )
    return pl.pallas_call(
        _experts_kernel,
        grid_spec=grid_spec,
        out_shape=jax.ShapeDtypeStruct((p, D_MODEL), F32),
        compiler_params=pltpu.CompilerParams(dimension_semantics=("arbitrary",),
                                             vmem_limit_bytes=VMEM_LIMIT),
        name="experts",
    )(block_e, n_used, next_e, x_sorted, w_gate, w_up, w_down)


def _combine_kernel(d0_ref, d1_ref, h_ref, mf_ref, gfin_ref, y_hbm, out_ref, y0_ref, y1_ref, sem, *, row0):
    i = pl.program_id(0)
    base = row0 + i * TM

    def copies(r):
        return (pltpu.make_async_copy(y_hbm.at[pl.ds(d0_ref[base + r], 1)], y0_ref.at[pl.ds(r, 1)], sem),
                pltpu.make_async_copy(y_hbm.at[pl.ds(d1_ref[base + r], 1)], y1_ref.at[pl.ds(r, 1)], sem))

    def start(r, c):
        for cp in copies(r):
            cp.start()
        return c

    lax.fori_loop(0, TM, start, 0)
    pltpu.make_async_copy(y_hbm.at[pl.ds(0, TM)], y0_ref, sem).wait()
    pltpu.make_async_copy(y_hbm.at[pl.ds(0, TM)], y1_ref, sem).wait()
    mf = mf_ref[...]
    ffn = mf[:, 0:1] * y0_ref[...] + mf[:, 1:2] * y1_ref[...]
    out_ref[...] = _rms(h_ref[...] + ffn, gfin_ref[...])


def _combine(dest0, dest1, h, mf, gfin, y_sorted, *, row0, n_rows):
    blk0 = row0 // TM
    grid_spec = pltpu.PrefetchScalarGridSpec(
        num_scalar_prefetch=2,
        grid=(n_rows // TM,),
        in_specs=[
            pl.BlockSpec((TM, D_MODEL), lambda i, *_: (blk0 + i, 0)),
            pl.BlockSpec((TM, LANES), lambda i, *_: (blk0 + i, 0)),
            pl.BlockSpec((1, D_MODEL), lambda i, *_: (0, 0)),
            pl.BlockSpec(memory_space=pl.ANY),
        ],
        out_specs=pl.BlockSpec((TM, D_MODEL), lambda i, *_: (i, 0)),
        scratch_shapes=[pltpu.VMEM((TM, D_MODEL), F32), pltpu.VMEM((TM, D_MODEL), F32),
                        pltpu.SemaphoreType.DMA(())],
    )
    return pl.pallas_call(
        functools.partial(_combine_kernel, row0=row0),
        grid_spec=grid_spec,
        out_shape=jax.ShapeDtypeStruct((n_rows, D_MODEL), F32),
        compiler_params=pltpu.CompilerParams(dimension_semantics=("arbitrary",),
                                             vmem_limit_bytes=VMEM_LIMIT),
        name="combine",
    )(dest0, dest1, h, mf, gfin, y_sorted)


def _rope_tables(pos):
    inv = ROPE_THETA ** (-jnp.arange(0, ROPE_DIM, 2, dtype=F32) / ROPE_DIM)
    ang = pos.astype(F32)[:, None] * inv[None, :]
    cos, sin = jnp.cos(ang), jnp.sin(ang)
    return jnp.concatenate([cos, cos], axis=-1), jnp.concatenate([-sin, sin], axis=-1)


def _swap_halves(w):
    return jnp.concatenate([w[..., ROPE_DIM // 2:], w[..., :ROPE_DIM // 2]], axis=-1)


def kernel(x_prompt, x_sample, cache_kv_latent, cache_k_rope, state_conv, norm_mix, w_in, norm_q, w_uq,
           norm_kv, w_uk, w_uv, conv_w, norm_attn_out, norm_conv_out, w_o, norm_ffn, w_router_group,
           b_router_group, w_router_expert, b_router_expert, w_gate, w_up, w_down, norm_final):
    assert w_in.shape[0] == 1, "single-layer trunk"
    bp, seq_p, _ = x_prompt.shape
    bs, seq_s, _ = x_sample.shape
    past_len = cache_kv_latent.shape[2]
    np_rows, ns_rows = bp * seq_p, bs * seq_s
    m = np_rows + ns_rows
    assert seq_p % TM == 0 and TM % seq_s == 0 and ns_rows % TM == 0 and seq_s == CHUNK

    xp = x_prompt.reshape(np_rows, D_MODEL)
    xs = x_sample.reshape(ns_rows, D_MODEL)
    row_vec = lambda v: v.reshape(1, -1)

    w_in0 = w_in[0]
    mla_w = Q_LORA + KV_LORA + ROPE_DIM
    w_a = jnp.concatenate([w_in0[:, :mla_w], _swap_halves(w_in0[:, mla_w - ROPE_DIM:mla_w])], axis=1).astype(BF16)
    w_c = w_in0[:, mla_w:].astype(BF16)
    wq4 = w_uq[0].reshape(Q_LORA, N_HEADS, QK_NOPE + ROPE_DIM)
    wq_rope = wq4[:, :, QK_NOPE:]
    w_q = jnp.concatenate([wq4[:, :, :QK_NOPE].reshape(Q_LORA, -1), wq_rope.reshape(Q_LORA, -1),
                           _swap_halves(wq_rope).reshape(Q_LORA, -1)], axis=1).astype(BF16)
    w_ukt = jnp.transpose(w_uk[0], (1, 2, 0)).astype(BF16)
    w_uvh = jnp.transpose(w_uv[0], (1, 0, 2)).astype(BF16)
    w_oa = w_o[0, :ATTN_W].astype(BF16)
    w_oc = w_o[0, ATTN_W:].astype(BF16)
    n_router = N_GROUPS + N_EXPERTS
    w_r = jnp.concatenate([w_router_group[0], w_router_expert[0].reshape(D_MODEL, N_EXPERTS)], axis=1)
    w_r = jnp.pad(w_r, ((0, 0), (0, LANES - n_router)))
    w_rh = w_r.astype(BF16)
    w_rl = (w_r - w_rh.astype(F32)).astype(BF16)
    b_r =jnp.pad(jnp.concatenate([b_router_group[0], b_router_expert[0].reshape(N_EXPERTS)]),
                  (0, LANES - n_router)).reshape(1, LANES)

    pos_p = jnp.arange(seq_p, dtype=jnp.int32)
    pos_s = past_len + jnp.arange(seq_s, dtype=jnp.int32)
    cos_p, sin_p = _rope_tables(pos_p)
    cos_s, sin_s = _rope_tables(pos_s)
    cosk = jnp.concatenate([cos_p, jnp.tile(cos_s, (TM // seq_s, 1))], axis=0)
    sink = jnp.concatenate([sin_p, jnp.tile(sin_s, (TM // seq_s, 1))], axis=0)
    state = jnp.concatenate([jnp.zeros((bp, CONV_W - 1, CONV_CH), F32), state_conv[0]], axis=0)

    cqn, ckv, krope, conv_n, utail = _in_proj(
        xp, xs, row_vec(norm_mix[0]), w_a, w_c, row_vec(norm_q[0]), row_vec(norm_kv[0]),
        row_vec(norm_conv_out[0]), conv_w[0], cosk, sink, state, seq_p=seq_p, seq_s=seq_s)

    gao = row_vec(norm_attn_out[0])
    attn_p = _attention(cqn, w_q, w_ukt, w_uvh, jnp.tile(cos_p, (1, N_HEADS)), jnp.tile(sin_p, (1, N_HEADS)),
                        gao, ckv, krope, n_batch=bp, seq=seq_p, row0=0)
    attn_s = _attention(cqn, w_q, w_ukt, w_uvh, jnp.tile(cos_s, (1, N_HEADS)), jnp.tile(sin_s, (1, N_HEADS)),
                        gao, ckv, krope, n_batch=bs, seq=seq_s, row0=np_rows,
                        past_kv=cache_kv_latent[0], past_kr=cache_k_rope[0])

    h, xpk, mi, mf, cnt = _out_proj(attn_p, attn_s, conv_n, xp, xs, w_oa, w_oc, row_vec(norm_ffn[0]),
                                    w_rh, w_rl, b_r)

    counts = cnt[0, :N_EXPERTS].astype(jnp.int32)
    padded = (counts + MOE_BLOCK - 1) // MOE_BLOCK * MOE_BLOCK
    pad_end = jnp.cumsum(padded)
    pad_start = pad_end - padded
    n_blocks = -(-(m * 2) // MOE_BLOCK) + N_EXPERTS
    block_row0 = jnp.arange(n_blocks, dtype=jnp.int32) * MOE_BLOCK
    block_e = jnp.minimum(jnp.sum((pad_end[None, :] <= block_row0[:, None]).astype(jnp.int32), axis=1),
                          N_EXPERTS - 1)
    n_used = (pad_end[-1:] // MOE_BLOCK).astype(jnp.int32)
    expert_ids = jnp.arange(N_EXPERTS, dtype=jnp.int32)[:, None]

    def seg_start(e):
        return jnp.sum(jnp.where(expert_ids == e[None, :], pad_start[:, None], 0), axis=0)

    dest0 = seg_start(mi[0]) + mi[2]
    dest1 = seg_start(mi[1]) + mi[3]

    x_sorted = _dispatch(dest0, dest1, pad_start + counts, padded - counts, n_used, xpk, n_blocks)
    later = (expert_ids.T > block_e[:, None]) & (padded > 0)[None, :]
    next_e = jnp.min(jnp.where(later, expert_ids.T, N_EXPERTS), axis=1)
    next_e = jnp.where(next_e == N_EXPERTS, -1, next_e).astype(jnp.int32)
    y_sorted = _experts(block_e, n_used, next_e, x_sorted, w_gate[0], w_up[0], w_down[0])
    gfin = row_vec(norm_final)
    y_p = _combine(dest0, dest1, h, mf, gfin, y_sorted, row0=0, n_rows=np_rows)
    y_s = _combine(dest0, dest1, h, mf, gfin, y_sorted, row0=np_rows, n_rows=ns_rows)

    ut = utail.reshape(m // CHUNK, SUBLANES, CONV_CH)
    tails = ut[:, SUBLANES - (CONV_W - 1):, :]
    p_last = (jnp.arange(bp) + 1) * (seq_p // CHUNK) - 1
    s_last = np_rows // CHUNK + (jnp.arange(bs) + 1) * (seq_s // CHUNK) - 1
    return (y_p.reshape(bp, seq_p, D_MODEL),
            y_s.reshape(bs, seq_s, D_MODEL),
            ckv[:np_rows].reshape(1, bp, seq_p, KV_LORA),
            krope[:np_rows].reshape(1, bp, seq_p, ROPE_DIM),
            tails[p_last][None],
            ckv[np_rows:].reshape(1, bs, seq_s, KV_LORA),
            krope[np_rows:].reshape(1, bs, seq_s, ROPE_DIM),
            tails[s_last][None])
```

```python
import functools

import jax
import jax.numpy as jnp
from jax import lax
from jax.experimental import pallas as pl
from jax.experimental.pallas import tpu as pltpu

F32 = jnp.float32
BF16 = jnp.bfloat16

D_MODEL = 2048
N_HEADS = 8
QK_NOPE = 128
ROPE_DIM = 64
V_DIM = 128
Q_LORA = 512
KV_LORA = 512
ATTN_W = N_HEADS * V_DIM
CONV_CH = D_MODEL - ATTN_W
CONV_W = 3
CHUNK = 64
N_GROUPS = 4
EXPERTS_PER_GROUP = 8
N_EXPERTS = N_GROUPS * EXPERTS_PER_GROUP
D_FF = 512
ROPE_THETA = 10000.0
EPS = 1e-6
ATTN_SCALE = (QK_NOPE + ROPE_DIM) ** -0.5
EXP2_SCALE = ATTN_SCALE * 1.4426950408889634

LANES = 128
SUBLANES = 8
TM = 256
MOE_BLOCK = 256
TQ = 128
TK = 256
DMA_ISSUE_UNROLL = 8
NEG_BIG = -1e30
VMEM_LIMIT = 56 * 1024 * 1024


def _rms(v, g):
    return v * lax.rsqrt(jnp.mean(v * v, axis=-1, keepdims=True) + EPS) * g


def _lane_bcast(v, width):
    if width % LANES == 0:
        return jnp.concatenate([v] * (width // LANES), axis=1)
    assert width < LANES
    return v[:, :width]


def _const_spec(shape):
    nd = len(shape)
    return pl.BlockSpec(shape, lambda *_: (0,) * nd, pipeline_mode=pl.Buffered(1))


def _in_proj_kernel(xp_ref, xs_ref, gmix_ref, wa_ref, wc_ref, gq_ref, gkv_ref, gco_ref, convw_ref,
                    cos_ref, sin_ref, state_ref,
                    cqn_ref, ckvp_ref, krp_ref, ckvs_ref, krs_ref, convn_ref, utail_ref, ext_ref,
                    *, n_prompt_tiles, tiles_per_seq, n_prompt_seq, sample_seq_len):
    i = pl.program_id(0)

    def conv_block(u_sub, gate_sub, row0, length):
        ext_ref[SUBLANES:SUBLANES + length, :] = u_sub
        um1 = ext_ref[SUBLANES - 1:SUBLANES - 1 + length, :]
        um2 = ext_ref[SUBLANES - 2:SUBLANES - 2 + length, :]
        cw = convw_ref[...]
        conv = cw[0:1] * um2 + cw[1:2] * um1 + cw[2:3] * u_sub
        convn_ref[row0:row0 + length, :] = _rms(gate_sub * conv, gco_ref[...]).astype(BF16)

    def tile(x_ref, is_prompt):
        ckv_ref, kr_ref = (ckvp_ref, krp_ref) if is_prompt else (ckvs_ref, krs_ref)
        x = x_ref[...]
        xn = _rms(x, gmix_ref[...]).astype(BF16)
        za = jnp.dot(xn, wa_ref[...], preferred_element_type=F32)
        cqn_ref[...] = _rms(za[:, :Q_LORA], gq_ref[...]).astype(BF16)
        ckv_ref[...] = _rms(za[:, Q_LORA:Q_LORA + KV_LORA], gkv_ref[...])
        zk = za[:, Q_LORA + KV_LORA:]
        kr_ref[...] = zk[:, :ROPE_DIM] * cos_ref[...] + zk[:, ROPE_DIM:] * sin_ref[...]

        zc = jnp.dot(xn, wc_ref[...], preferred_element_type=F32)
        gate_b = zc[:, :CONV_CH]
        u = zc[:, CONV_CH:2 * CONV_CH] * zc[:, 2 * CONV_CH:]
        for j in range(TM // CHUNK):
            utail_ref[j] = u[CHUNK * (j + 1) - SUBLANES:CHUNK * (j + 1), :]

        if is_prompt:
            first = (i % tiles_per_seq) == 0

            @pl.when(first)
            def _():
                ext_ref[SUBLANES - 2:SUBLANES, :] = state_ref[i // tiles_per_seq]

            @pl.when(jnp.logical_not(first))
            def _():
                ext_ref[SUBLANES - 2:SUBLANES, :] = ext_ref[TM + SUBLANES - 2:TM + SUBLANES, :]

            conv_block(u, gate_b, 0, TM)
        else:
            n_sub = TM // sample_seq_len
            seq0 = n_prompt_seq + (i - n_prompt_tiles) * n_sub
            for k in range(n_sub):
                ext_ref[SUBLANES - 2:SUBLANES, :] = state_ref[seq0 + k]
                lo = k * sample_seq_len
                conv_block(u[lo:lo + sample_seq_len], gate_b[lo:lo + sample_seq_len], lo, sample_seq_len)

    @pl.when(i < n_prompt_tiles)
    def _():
        tile(xp_ref, True)

    @pl.when(i >= n_prompt_tiles)
    def _():
        tile(xs_ref, False)


def _in_proj(xp, xs, gmix, w_a, w_c, gq, gkv, gco, convw, cosk, sink, state, *, seq_p, seq_s):
    np_rows, ns_rows = xp.shape[0], xs.shape[0]
    m = np_rows + ns_rows
    npt, nst = np_rows // TM, ns_rows // TM
    tps = seq_p // TM
    n_prompt_seq = np_rows // seq_p
    last_p = npt - 1

    def tab_idx(i):
        return (jnp.where(i < npt, i % tps, tps), 0)

    row = lambda i: (i, 0)
    prow = lambda i: (jnp.minimum(i, last_p), 0)
    srow = lambda i: (jnp.maximum(i - npt, 0), 0)
    kern = functools.partial(_in_proj_kernel, n_prompt_tiles=npt, tiles_per_seq=tps,
                             n_prompt_seq=n_prompt_seq, sample_seq_len=seq_s)
    return pl.pallas_call(
        kern,
        grid=(npt + nst,),
        in_specs=[
            pl.BlockSpec((TM, D_MODEL), prow),
            pl.BlockSpec((TM, D_MODEL), srow),
            _const_spec((1, D_MODEL)),
            _const_spec(w_a.shape),
            _const_spec(w_c.shape),
            _const_spec((1, Q_LORA)),
            _const_spec((1, KV_LORA)),
            _const_spec((1, CONV_CH)),
            _const_spec((CONV_W, CONV_CH)),
            pl.BlockSpec((TM, ROPE_DIM), tab_idx),
            pl.BlockSpec((TM, ROPE_DIM), tab_idx),
            _const_spec(state.shape),
        ],
        out_specs=[
            pl.BlockSpec((TM, Q_LORA), row),
            pl.BlockSpec((TM, KV_LORA), prow),
            pl.BlockSpec((TM, ROPE_DIM), prow),
            pl.BlockSpec((TM, KV_LORA), srow),
            pl.BlockSpec((TM, ROPE_DIM), srow),
            pl.BlockSpec((TM, CONV_CH), row),
            pl.BlockSpec((TM // CHUNK, SUBLANES, CONV_CH), lambda i: (i, 0, 0)),
        ],
        out_shape=[
            jax.ShapeDtypeStruct((m, Q_LORA), BF16),
            jax.ShapeDtypeStruct((np_rows, KV_LORA), F32),
            jax.ShapeDtypeStruct((np_rows, ROPE_DIM), F32),
            jax.ShapeDtypeStruct((ns_rows, KV_LORA), F32),
            jax.ShapeDtypeStruct((ns_rows, ROPE_DIM), F32),
            jax.ShapeDtypeStruct((m, CONV_CH), BF16),
            jax.ShapeDtypeStruct((m // CHUNK, SUBLANES, CONV_CH), F32),
        ],
        scratch_shapes=[pltpu.VMEM((TM + SUBLANES, CONV_CH), F32)],
        compiler_params=pltpu.CompilerParams(dimension_semantics=("arbitrary",),
                                             vmem_limit_bytes=VMEM_LIMIT),
        name="in_proj",
    )(xp, xs, gmix, w_a, w_c, gq, gkv, gco, convw, cosk, sink, state)


def _attn_kernel(*refs, tq, n_past, causal):
    refs = list(refs)
    cqn_ref, wq_ref, wuk_ref, wuv_ref, cos_ref, sin_ref, gao_ref = refs[:7]
    refs = refs[7:]
    if n_past:
        pkv_ref, pkr_ref = refs[:2]
        refs = refs[2:]
    kv_ref, kr_ref, out_ref, qlat_ref, qr_ref, m_ref, l_ref, acc_ref, s_ref, klim_ref = refs

    qi = pl.program_id(1)
    rows = N_HEADS * tq

    q = jnp.dot(cqn_ref[...], wq_ref[...], preferred_element_type=F32)
    nope_w = N_HEADS * QK_NOPE
    rope_w = N_HEADS * ROPE_DIM
    qrope = q[:, nope_w:nope_w + rope_w] * cos_ref[...] + q[:, nope_w + rope_w:] * sin_ref[...]
    for h in range(N_HEADS):
        qn = q[:, h * QK_NOPE:(h + 1) * QK_NOPE].astype(BF16)
        ql = jnp.dot(qn, wuk_ref[h], preferred_element_type=F32)
        qlat_ref[h * tq:(h + 1) * tq, :] = ql.astype(BF16)
        qr_ref[h * tq:(h + 1) * tq, :] = qrope[:, h * ROPE_DIM:(h + 1) * ROPE_DIM].astype(BF16)

    m_ref[...] = jnp.full(m_ref.shape, NEG_BIG, F32)
    l_ref[...] = jnp.zeros(l_ref.shape, F32)
    acc_ref[...] = jnp.zeros(acc_ref.shape, F32)

    nt = (((1,), (1,)), ((), ()))

    def scores(kc_f32, kr_f32):
        s = lax.dot_general(qlat_ref[...], kc_f32.astype(BF16), nt, preferred_element_type=F32)
        return s + lax.dot_general(qr_ref[...], kr_f32.astype(BF16), nt, preferred_element_type=F32)

    def update(s, kc_f32, mask):
        if mask is not None:
            s = jnp.where(mask, s, NEG_BIG)
        m_prev = m_ref[...]
        m_new = jnp.maximum(m_prev, jnp.max(s, axis=-1, keepdims=True))
        alpha = jnp.exp2((m_prev - m_new) * EXP2_SCALE)
        p = jnp.exp2((s - _lane_bcast(m_new, s.shape[1])) * EXP2_SCALE)
        l_ref[...] = alpha * l_ref[...] + jnp.sum(p, axis=-1, keepdims=True)
        pv = jnp.dot(p.astype(BF16), kc_f32.astype(BF16), preferred_element_type=F32)
        acc_ref[...] = _lane_bcast(alpha, KV_LORA) * acc_ref[...] + pv
        m_ref[...] = m_new

    def pipelined(kv, kr, lo, hi, last, mask_fn):
        def body(j, c):
            k0 = pl.multiple_of(j * TK, TK)
            k1 = pl.multiple_of(jnp.minimum(j + 1, last) * TK, TK)
            s_cur = s_ref[j % 2]
            s_ref[(j + 1) % 2] = scores(kv[pl.ds(k1, TK), :], kr[pl.ds(k1, TK), :])
            update(s_cur, kv[pl.ds(k0, TK), :], None if mask_fn is None else mask_fn(k0))
            return c
        lax.fori_loop(lo, hi, body, 0)

    if n_past:
        n_pb = n_past // TK
        s_ref[0] = scores(pkv_ref[pl.ds(0, TK), :], pkr_ref[pl.ds(0, TK), :])
        pipelined(pkv_ref, pkr_ref, 0, n_pb, n_pb - 1, None)

    if causal:
        n_blocks = ((qi + 1) * tq + TK - 1) // TK
        n_full = jnp.minimum((qi * tq // CHUNK + 1) * CHUNK // TK, n_blocks)

        assert tq & (tq - 1) == 0 and CHUNK & (CHUNK - 1) == 0
        r = lax.broadcasted_iota(jnp.int32, (rows, LANES), 0)
        q_pos = qi * tq + (r & (tq - 1))
        klim_ref[...] = (q_pos & ~(CHUNK - 1)) + CHUNK

        def mask_fn(k0):
            cidx = lax.broadcasted_iota(jnp.int32, (rows, TK), 1)
            return cidx < _lane_bcast(klim_ref[...] - k0, TK)

        s_ref[0] = scores(kv_ref[pl.ds(0, TK), :], kr_ref[pl.ds(0, TK), :])
        pipelined(kv_ref, kr_ref, 0, n_full, n_blocks - 1, None)
        pipelined(kv_ref, kr_ref, n_full, n_blocks, n_blocks - 1, mask_fn)
    else:
        update(scores(kv_ref[...], kr_ref[...]), kv_ref[...], None)

    o = acc_ref[...] / _lane_bcast(l_ref[...], KV_LORA)
    parts = []
    for h in range(N_HEADS):
        oh = o[h * tq:(h + 1) * tq, :].astype(BF16)
        parts.append(jnp.dot(oh, wuv_ref[h], preferred_element_type=F32))
    attn = jnp.concatenate(parts, axis=-1)
    out_ref[...] = _rms(attn, gao_ref[...]).astype(BF16)


def _attention(cqn, w_q, w_ukt, w_uv, cosq, sinq, gao, ckv, krope, *, n_batch, seq, row0,
               past_kv=None, past_kr=None):
    causal = past_kv is None
    tq = TQ if causal else seq
    nq = seq // tq
    n_past = 0 if causal else past_kv.shape[1]
    if not causal:
        assert n_past % CHUNK == 0 and seq <= CHUNK and n_past % TK == 0
    blk0 = row0 // tq
    qrow = lambda b, q: (blk0 + b * nq + q, 0)
    in_specs = [
        pl.BlockSpec((tq, Q_LORA), qrow),
        _const_spec(w_q.shape),
        _const_spec(w_ukt.shape),
        _const_spec(w_uv.shape),
        pl.BlockSpec((tq, N_HEADS * ROPE_DIM), lambda b, q: (q, 0)),
        pl.BlockSpec((tq, N_HEADS * ROPE_DIM), lambda b, q: (q, 0)),
        _const_spec((1, ATTN_W)),
    ]
    args = [cqn, w_q, w_ukt, w_uv, cosq, sinq, gao]
    if n_past:
        in_specs += [pl.BlockSpec((None, n_past, KV_LORA), lambda b, q: (b, 0, 0)),
                     pl.BlockSpec((None, n_past, ROPE_DIM), lambda b, q: (b, 0, 0))]
        args += [past_kv, past_kr]
    in_specs += [pl.BlockSpec((seq, KV_LORA), lambda b, q: (b, 0)),
                 pl.BlockSpec((seq, ROPE_DIM), lambda b, q: (b, 0))]
    args += [ckv, krope]
    rows = N_HEADS * tq
    kern = functools.partial(_attn_kernel, tq=tq, n_past=n_past, causal=causal)
    return pl.pallas_call(
        kern,
        grid=(n_batch, nq),
        in_specs=in_specs,
        out_specs=pl.BlockSpec((tq, ATTN_W), lambda b, q: (b * nq + q, 0)),
        out_shape=jax.ShapeDtypeStruct((n_batch * seq, ATTN_W), BF16),
        scratch_shapes=[
            pltpu.VMEM((rows, KV_LORA), BF16),
            pltpu.VMEM((rows, ROPE_DIM), BF16),
            pltpu.VMEM((rows, LANES), F32),
            pltpu.VMEM((rows, LANES), F32),
            pltpu.VMEM((rows, KV_LORA), F32),
            pltpu.VMEM((2, rows, TK), F32),
            pltpu.VMEM((rows, LANES), jnp.int32),
        ],
        compiler_params=pltpu.CompilerParams(dimension_semantics=("arbitrary", "arbitrary"),
                                             vmem_limit_bytes=VMEM_LIMIT),
        name="attn_prompt" if causal else "attn_sample",
    )(*args)


def _out_proj_kernel(attnp_ref, attns_ref, convn_ref, xp_ref, xs_ref, woa_ref, woc_ref, gffn_ref, wrh_ref, wrl_ref,
                     br_ref, h_ref, xpk_ref, mi_ref, mf_ref, cnt_ref, carry_ref, *, n_prompt_tiles):
    i = pl.program_id(0)

    @pl.when(i == 0)
    def _():
        carry_ref[...] = jnp.zeros(carry_ref.shape, F32)

    def tile(x_ref, attn_ref):
        y = jnp.dot(attn_ref[...], woa_ref[...], preferred_element_type=F32)
        y = y + jnp.dot(convn_ref[...], woc_ref[...], preferred_element_type=F32)
        h = x_ref[...] + y
        h_ref[...] = h
        xn = _rms(h, gffn_ref[...])

        half = D_MODEL // 2
        xh = xn.astype(BF16)
        xh32 = xh.astype(F32)
        lo = lax.bitcast_convert_type(xh32[:, :half], jnp.uint32)
        hi = lax.bitcast_convert_type(xh32[:, half:], jnp.uint32)
        xpk_ref[...] = (lo >> 16) | (hi & jnp.uint32(0xFFFF0000))

        xl = (xn - xh32).astype(BF16)
        logits = jnp.dot(xh, wrh_ref[...], preferred_element_type=F32)
        logits = logits + (jnp.dot(xl, wrh_ref[...], preferred_element_type=F32)
                           + jnp.dot(xh, wrl_ref[...], preferred_element_type=F32))
        logits = logits + br_ref[...]
        lane = lax.broadcasted_iota(jnp.int32, (TM, LANES), 1).astype(F32)
        ninf = -jnp.inf
        far = float(LANES)

        def first_argmax(v):
            vmax = jnp.max(v, axis=-1, keepdims=True)
            return vmax, jnp.min(jnp.where(v == vmax, lane, far), axis=-1, keepdims=True)

        gl = jnp.where(lane < N_GROUPS, logits, ninf)
        gmax, gidx = first_argmax(gl)
        g_p = 1.0 / jnp.sum(jnp.exp(gl - gmax), axis=-1, keepdims=True)
        e_lo = N_GROUPS + EXPERTS_PER_GROUP * gidx
        el = jnp.where((lane >= e_lo) & (lane < e_lo + EXPERTS_PER_GROUP), logits, ninf)
        e1max, i1 = first_argmax(el)
        z = jnp.sum(jnp.exp(el - e1max), axis=-1, keepdims=True)
        el2 = jnp.where(lane == i1, ninf, el)
        e2max, i2 = first_argmax(el2)
        p1 = 1.0 / z
        p2 = jnp.exp(e2max - e1max) / z
        den = p1 + p2
        g0 = g_p * p1 / den
        g1 = g_p * p2 / den
        e0 = i1 - N_GROUPS
        e1 = i2 - N_GROUPS

        oh0 = lane == e0
        oh1 = lane == e1
        oh = jnp.where(oh0 | oh1, 1.0, 0.0)
        r = lax.broadcasted_iota(jnp.int32, (TM, TM), 0)
        c = lax.broadcasted_iota(jnp.int32, (TM, TM), 1)
        ltri = jnp.where(r > c, 1.0, 0.0).astype(BF16)
        before = jnp.dot(ltri, oh.astype(BF16), preferred_element_type=F32) + carry_ref[...]
        rank0 = jnp.sum(jnp.where(oh0, before, 0.0), axis=-1, keepdims=True)
        rank1 = jnp.sum(jnp.where(oh1, before, 0.0), axis=-1, keepdims=True)
        total = carry_ref[...] + jnp.sum(oh, axis=0, keepdims=True)
        carry_ref[...] = total
        cnt_ref[...] = jnp.broadcast_to(total, cnt_ref.shape)

        mi = jnp.where(lane == 0, e0, jnp.where(lane == 1, e1, jnp.where(lane == 2, rank0, rank1)))
        mi_ref[...] = jnp.transpose(mi)[:SUBLANES, :].astype(jnp.int32)
        mf_ref[...] = jnp.where(lane == 0, g0, g1)

    @pl.when(i < n_prompt_tiles)
    def _():
        tile(xp_ref, attnp_ref)

    @pl.when(i >= n_prompt_tiles)
    def _():
        tile(xs_ref, attns_ref)


def _out_proj(attn_p, attn_s, conv_n, xp, xs, w_oa, w_oc, gffn, w_rh, w_rl, b_r):
    m = conv_n.shape[0]
    npt = xp.shape[0] // TM
    last_p = npt - 1
    row = lambda i: (i, 0)
    return pl.pallas_call(
        functools.partial(_out_proj_kernel, n_prompt_tiles=npt),
        grid=(m // TM,),
        in_specs=[
            pl.BlockSpec((TM, ATTN_W), lambda i: (jnp.minimum(i, last_p), 0)),
            pl.BlockSpec((TM, ATTN_W), lambda i: (jnp.maximum(i - npt, 0), 0)),
            pl.BlockSpec((TM, CONV_CH), row),
            pl.BlockSpec((TM, D_MODEL), lambda i: (jnp.minimum(i, last_p), 0)),
            pl.BlockSpec((TM, D_MODEL), lambda i: (jnp.maximum(i - npt, 0), 0)),
            _const_spec(w_oa.shape),
            _const_spec(w_oc.shape),
            _const_spec((1, D_MODEL)),
            _const_spec(w_rh.shape),
            _const_spec(w_rl.shape),
            _const_spec((1, LANES)),
        ],
        out_specs=[
            pl.BlockSpec((TM, D_MODEL), row),
            pl.BlockSpec((TM, D_MODEL // 2), row),
            pl.BlockSpec((SUBLANES, TM), lambda i: (0, i)),
            pl.BlockSpec((TM, LANES), row),
            pl.BlockSpec((SUBLANES, LANES), lambda i: (0, 0)),
        ],
        out_shape=[
            jax.ShapeDtypeStruct((m, D_MODEL), F32),
            jax.ShapeDtypeStruct((m, D_MODEL // 2), jnp.uint32),
            jax.ShapeDtypeStruct((SUBLANES, m), jnp.int32),
            jax.ShapeDtypeStruct((m, LANES), F32),
            jax.ShapeDtypeStruct((SUBLANES, LANES), F32),
        ],
        scratch_shapes=[pltpu.VMEM((1, LANES), F32)],
        compiler_params=pltpu.CompilerParams(dimension_semantics=("arbitrary",),
                                             vmem_limit_bytes=VMEM_LIMIT),
        name="out_proj",
    )(attn_p, attn_s, conv_n, xp, xs, w_oa, w_oc, gffn, w_rh, w_rl, b_r)


def _dispatch_kernel(d0_ref, d1_ref, zlo_ref, zn_ref, nu_ref, xpk_ref, xs_hbm, zeros_ref, sems, *, n_blocks):
    i = pl.program_id(0)
    sem = sems.at[0]
    zsem = sems.at[1]

    def row_copy(src_ref, src_row, dst_row):
        return pltpu.make_async_copy(src_ref.at[pl.ds(src_row, 1)], xs_hbm.at[pl.ds(dst_row, 1)], sem)

    def zero_fill(act):
        def per_expert(e, c):
            lo = zlo_ref[e]
            n = zn_ref[e]
            head = (-lo) & (SUBLANES - 1)
            for r in range(SUBLANES - 1):
                @pl.when(r < head)
                def _(r=r):
                    act(pltpu.make_async_copy(zeros_ref.at[pl.ds(0, 1)], xs_hbm.at[pl.ds(lo + r, 1)], zsem))
            off = lo + head
            rest = n - head
            size = MOE_BLOCK // 2
            while size >= SUBLANES:
                @pl.when((rest & size) != 0)
                def _(off=off, size=size):
                    dst = xs_hbm.at[pl.ds(pl.multiple_of(off, SUBLANES), size)]
                    act(pltpu.make_async_copy(zeros_ref.at[pl.ds(0, size)], dst, zsem))
                off = off + (rest & size)
                size //= 2
            return c

        def per_block(b, c):
            dst = xs_hbm.at[pl.ds(pl.multiple_of(b * MOE_BLOCK, MOE_BLOCK), MOE_BLOCK)]
            act(pltpu.make_async_copy(zeros_ref, dst, zsem))
            return c

        lax.fori_loop(0, N_EXPERTS, per_expert, 0)
        lax.fori_loop(nu_ref[0], n_blocks, per_block, 0)

    @pl.when(i == 0)
    def _():
        zeros_ref[...] = jnp.zeros(zeros_ref.shape, zeros_ref.dtype)
        zero_fill(lambda cp: cp.start())
        zero_fill(lambda cp: cp.wait())

    base = i * TM

    def start(r, c):
        row_copy(xpk_ref, r, d0_ref[base + r]).start()
        row_copy(xpk_ref, r, d1_ref[base + r]).start()
        return c

    lax.fori_loop(0, TM, start, 0, unroll=DMA_ISSUE_UNROLL)
    for _ in range(2):
        pltpu.make_async_copy(xpk_ref, xs_hbm.at[pl.ds(0, TM)], sem).wait()


def _dispatch(dest0, dest1, pad_lo, n_pad, n_used, xpk, n_blocks):
    m = xpk.shape[0]
    grid_spec = pltpu.PrefetchScalarGridSpec(
        num_scalar_prefetch=5,
        grid=(m // TM,),
        in_specs=[pl.BlockSpec((TM, D_MODEL // 2), lambda i, *_: (i, 0))],
        out_specs=pl.BlockSpec(memory_space=pl.ANY),
        scratch_shapes=[pltpu.VMEM((MOE_BLOCK, D_MODEL // 2), jnp.uint32),
                        pltpu.SemaphoreType.DMA((2,))],
    )
    return pl.pallas_call(
        functools.partial(_dispatch_kernel, n_blocks=n_blocks),
        grid_spec=grid_spec,
        out_shape=jax.ShapeDtypeStruct((n_blocks * MOE_BLOCK, D_MODEL // 2), jnp.uint32),
        compiler_params=pltpu.CompilerParams(dimension_semantics=("arbitrary",)),
        name="dispatch",
    )(dest0, dest1, pad_lo, n_pad, n_used, xpk)


def _experts_kernel(be_ref, nu_ref, nxt_ref, x_ref, wg_hbm, wu_hbm, wd_hbm, y_ref,
                    sg_ref, su_ref, sd_ref, wgb_ref, wub_ref, wdb_ref, sems):
    b = pl.program_id(0)
    active = b < nu_ref[0]
    new_expert = jnp.logical_or(b == 0, be_ref[b] != be_ref[jnp.maximum(b - 1, 0)])

    def weight_copies(e):
        return (pltpu.make_async_copy(wg_hbm.at[e], sg_ref, sems.at[0]),
                pltpu.make_async_copy(wu_hbm.at[e], su_ref, sems.at[1]),
                pltpu.make_async_copy(wd_hbm.at[e], sd_ref, sems.at[2]))

    @pl.when(b == 0)
    def _():
        for cp in weight_copies(be_ref[0]):
            cp.start()

    @pl.when(jnp.logical_and(active, new_expert))
    def _():
        for cp in weight_copies(be_ref[b]):
            cp.wait()
        wgb_ref[...] = sg_ref[...].astype(BF16)
        wub_ref[...] = su_ref[...].astype(BF16)
        wdb_ref[...] = sd_ref[...].astype(BF16)

        @pl.when(nxt_ref[b] >= 0)
        def _():
            for cp in weight_copies(nxt_ref[b]):
                cp.start()

    @pl.when(active)
    def _():
        half = D_MODEL // 2
        xw = x_ref[...]
        xa = lax.bitcast_convert_type(xw << 16, F32).astype(BF16)
        xb = lax.bitcast_convert_type(xw & jnp.uint32(0xFFFF0000), F32).astype(BF16)
        g = jnp.dot(xa, wgb_ref[:half, :], preferred_element_type=F32)
        g = g + jnp.dot(xb, wgb_ref[half:, :], preferred_element_type=F32)
        u = jnp.dot(xa, wub_ref[:half, :], preferred_element_type=F32)
        u = u + jnp.dot(xb, wub_ref[half:, :], preferred_element_type=F32)
        hmid = (g * jax.nn.sigmoid(g)) * u
        y_ref[...] = jnp.dot(hmid.astype(BF16), wdb_ref[...], preferred_element_type=F32)

    @pl.when(b >= nu_ref[0])
    def _():
        y_ref[...] = jnp.zeros(y_ref.shape, y_ref.dtype)


def _experts(block_e, n_used, next_e, x_sorted, w_gate, w_up, w_down):
    p = x_sorted.shape[0]
    nb = p // MOE_BLOCK

    def xrow(b, be, nu, nxt):
        return (jnp.maximum(jnp.minimum(b, nu[0] - 1), 0), 0)

    grid_spec = pltpu.PrefetchScalarGridSpec(
        num_scalar_prefetch=3,
        grid=(nb,),
        in_specs=[
            pl.BlockSpec((MOE_BLOCK, D_MODEL // 2), xrow),
            pl.BlockSpec(memory_space=pl.ANY),
            pl.BlockSpec(memory_space=pl.ANY),
            pl.BlockSpec(memory_space=pl.ANY),
        ],
        out_specs=pl.BlockSpec((MOE_BLOCK, D_MODEL), lambda b, be, nu, nxt: (b, 0)),
        scratch_shapes=[pltpu.VMEM((D_MODEL, D_FF), F32), pltpu.VMEM((D_MODEL, D_FF), F32),
                        pltpu.VMEM((D_FF, D_MODEL), F32),
                        pltpu.VMEM((D_MODEL, D_FF), BF16), pltpu.VMEM((D_MODEL, D_FF), BF16),
                        pltpu.VMEM((D_FF, D_MODEL), BF16),
                        pltpu.SemaphoreType.DMA((3,))],
    )
    return pl.pallas_call(
        _experts_kernel,
        grid_spec=grid_spec,
        out_shape=jax.ShapeDtypeStruct((p, D_MODEL), F32),
        compiler_params=pltpu.CompilerParams(dimension_semantics=("arbitrary",),
                                             vmem_limit_bytes=VMEM_LIMIT),
        name="experts",
    )(block_e, n_used, next_e, x_sorted, w_gate, w_up, w_down)


def _combine_kernel(d0_ref, d1_ref, h_ref, mf_ref, gfin_ref, y_hbm, out_ref, y0_ref, y1_ref, sem, *, row0):
    i = pl.program_id(0)
    base = row0 + i * TM

    def copies(r):
        return (pltpu.make_async_copy(y_hbm.at[pl.ds(d0_ref[base + r], 1)], y0_ref.at[pl.ds(r, 1)], sem),
                pltpu.make_async_copy(y_hbm.at[pl.ds(d1_ref[base + r], 1)], y1_ref.at[pl.ds(r, 1)], sem))

    def start(r, c):
        for cp in copies(r):
            cp.start()
        return c

    lax.fori_loop(0, TM, start, 0, unroll=DMA_ISSUE_UNROLL)
    pltpu.make_async_copy(y_hbm.at[pl.ds(0, TM)], y0_ref, sem).wait()
    pltpu.make_async_copy(y_hbm.at[pl.ds(0, TM)], y1_ref, sem).wait()
    mf = mf_ref[...]
    ffn = mf[:, 0:1] * y0_ref[...] + mf[:, 1:2] * y1_ref[...]
    out_ref[...] = _rms(h_ref[...] + ffn, gfin_ref[...])


def _combine(dest0, dest1, h, mf, gfin, y_sorted, *, row0, n_rows):
    blk0 = row0 // TM
    grid_spec = pltpu.PrefetchScalarGridSpec(
        num_scalar_prefetch=2,
        grid=(n_rows // TM,),
        in_specs=[
            pl.BlockSpec((TM, D_MODEL), lambda i, *_: (blk0 + i, 0)),
            pl.BlockSpec((TM, LANES), lambda i, *_: (blk0 + i, 0)),
            pl.BlockSpec((1, D_MODEL), lambda i, *_: (0, 0)),
            pl.BlockSpec(memory_space=pl.ANY),
        ],
        out_specs=pl.BlockSpec((TM, D_MODEL), lambda i, *_: (i, 0)),
        scratch_shapes=[pltpu.VMEM((TM, D_MODEL), F32), pltpu.VMEM((TM, D_MODEL), F32),
                        pltpu.SemaphoreType.DMA(())],
    )
    return pl.pallas_call(
        functools.partial(_combine_kernel, row0=row0),
        grid_spec=grid_spec,
        out_shape=jax.ShapeDtypeStruct((n_rows, D_MODEL), F32),
        compiler_params=pltpu.CompilerParams(dimension_semantics=("arbitrary",),
                                             vmem_limit_bytes=VMEM_LIMIT),
        name="combine",
    )(dest0, dest1, h, mf, gfin, y_sorted)


def _rope_tables(pos):
    inv = ROPE_THETA ** (-jnp.arange(0, ROPE_DIM, 2, dtype=F32) / ROPE_DIM)
    ang = pos.astype(F32)[:, None] * inv[None, :]
    cos, sin = jnp.cos(ang), jnp.sin(ang)
    return jnp.concatenate([cos, cos], axis=-1), jnp.concatenate([-sin, sin], axis=-1)


def _swap_halves(w):
    return jnp.concatenate([w[..., ROPE_DIM // 2:], w[..., :ROPE_DIM // 2]], axis=-1)


def kernel(x_prompt, x_sample, cache_kv_latent, cache_k_rope, state_conv, norm_mix, w_in, norm_q, w_uq,
           norm_kv, w_uk, w_uv, conv_w, norm_attn_out, norm_conv_out, w_o, norm_ffn, w_router_group,
           b_router_group, w_router_expert, b_router_expert, w_gate, w_up, w_down, norm_final):
    assert w_in.shape[0] == 1, "single-layer trunk"
    bp, seq_p, _ = x_prompt.shape
    bs, seq_s, _ = x_sample.shape
    past_len = cache_kv_latent.shape[2]
    np_rows, ns_rows = bp * seq_p, bs * seq_s
    m = np_rows + ns_rows
    assert seq_p % TM == 0 and TM % seq_s == 0 and ns_rows % TM == 0 and seq_s == CHUNK

    xp = x_prompt.reshape(np_rows, D_MODEL)
    xs = x_sample.reshape(ns_rows, D_MODEL)
    row_vec = lambda v: v.reshape(1, -1)

    w_in0 = w_in[0]
    mla_w = Q_LORA + KV_LORA + ROPE_DIM
    w_a = jnp.concatenate([w_in0[:, :mla_w], _swap_halves(w_in0[:, mla_w - ROPE_DIM:mla_w])], axis=1).astype(BF16)
    w_c = w_in0[:, mla_w:].astype(BF16)
    wq4 = w_uq[0].reshape(Q_LORA, N_HEADS, QK_NOPE + ROPE_DIM)
    wq_rope = wq4[:, :, QK_NOPE:]
    w_q = jnp.concatenate([wq4[:, :, :QK_NOPE].reshape(Q_LORA, -1), wq_rope.reshape(Q_LORA, -1),
                           _swap_halves(wq_rope).reshape(Q_LORA, -1)], axis=1).astype(BF16)
    w_ukt = jnp.transpose(w_uk[0], (1, 2, 0)).astype(BF16)
    w_uvh = jnp.transpose(w_uv[0], (1, 0, 2)).astype(BF16)
    w_oa = w_o[0, :ATTN_W].astype(BF16)
    w_oc = w_o[0, ATTN_W:].astype(BF16)
    n_router = N_GROUPS + N_EXPERTS
    w_r = jnp.concatenate([w_router_group[0], w_router_expert[0].reshape(D_MODEL, N_EXPERTS)], axis=1)
    w_r = jnp.pad(w_r, ((0, 0), (0, LANES - n_router)))
    w_rh = w_r.astype(BF16)
    w_rl = (w_r - w_rh.astype(F32)).astype(BF16)
    b_r =jnp.pad(jnp.concatenate([b_router_group[0], b_router_expert[0].reshape(N_EXPERTS)]),
                  (0, LANES - n_router)).reshape(1, LANES)

    pos_p = jnp.arange(seq_p, dtype=jnp.int32)
    pos_s = past_len + jnp.arange(seq_s, dtype=jnp.int32)
    cos_p, sin_p = _rope_tables(pos_p)
    cos_s, sin_s = _rope_tables(pos_s)
    cosk = jnp.concatenate([cos_p, jnp.tile(cos_s, (TM // seq_s, 1))], axis=0)
    sink = jnp.concatenate([sin_p, jnp.tile(sin_s, (TM // seq_s, 1))], axis=0)
    state = jnp.concatenate([jnp.zeros((bp, CONV_W - 1, CONV_CH), F32), state_conv[0]], axis=0)

    cqn, ckv_p, kr_p, ckv_s, kr_s, conv_n, utail = _in_proj(
        xp, xs, row_vec(norm_mix[0]), w_a, w_c, row_vec(norm_q[0]), row_vec(norm_kv[0]),
        row_vec(norm_conv_out[0]), conv_w[0], cosk, sink, state, seq_p=seq_p, seq_s=seq_s)

    gao = row_vec(norm_attn_out[0])
    attn_p = _attention(cqn, w_q, w_ukt, w_uvh, jnp.tile(cos_p, (1, N_HEADS)), jnp.tile(sin_p, (1, N_HEADS)),
                        gao, ckv_p, kr_p, n_batch=bp, seq=seq_p, row0=0)
    attn_s = _attention(cqn, w_q, w_ukt, w_uvh, jnp.tile(cos_s, (1, N_HEADS)), jnp.tile(sin_s, (1, N_HEADS)),
                        gao, ckv_s, kr_s, n_batch=bs, seq=seq_s, row0=np_rows,
                        past_kv=cache_kv_latent[0], past_kr=cache_k_rope[0])

    h, xpk, mi, mf, cnt = _out_proj(attn_p, attn_s, conv_n, xp, xs, w_oa, w_oc, row_vec(norm_ffn[0]),
                                    w_rh, w_rl, b_r)

    counts = cnt[0, :N_EXPERTS].astype(jnp.int32)
    padded = (counts + MOE_BLOCK - 1) // MOE_BLOCK * MOE_BLOCK
    pad_end = jnp.cumsum(padded)
    pad_start = pad_end - padded
    n_blocks = -(-(m * 2) // MOE_BLOCK) + N_EXPERTS
    block_row0 = jnp.arange(n_blocks, dtype=jnp.int32) * MOE_BLOCK
    block_e = jnp.minimum(jnp.sum((pad_end[None, :] <= block_row0[:, None]).astype(jnp.int32), axis=1),
                          N_EXPERTS - 1)
    n_used = (pad_end[-1:] // MOE_BLOCK).astype(jnp.int32)
    expert_ids = jnp.arange(N_EXPERTS, dtype=jnp.int32)[:, None]

    def seg_start(e):
        return jnp.sum(jnp.where(expert_ids == e[None, :], pad_start[:, None], 0), axis=0)

    dest0 = seg_start(mi[0]) + mi[2]
    dest1 = seg_start(mi[1]) + mi[3]

    x_sorted = _dispatch(dest0, dest1, pad_start + counts, padded - counts, n_used, xpk, n_blocks)
    later = (expert_ids.T > block_e[:, None]) & (padded > 0)[None, :]
    next_e = jnp.min(jnp.where(later, expert_ids.T, N_EXPERTS), axis=1)
    next_e = jnp.where(next_e == N_EXPERTS, -1, next_e).astype(jnp.int32)
    y_sorted = _experts(block_e, n_used, next_e, x_sorted, w_gate[0], w_up[0], w_down[0])
    gfin = row_vec(norm_final)
    y_p = _combine(dest0, dest1, h, mf, gfin, y_sorted, row0=0, n_rows=np_rows)
    y_s = _combine(dest0, dest1, h, mf, gfin, y_sorted, row0=np_rows, n_rows=ns_rows)

    ut = utail.reshape(m // CHUNK, SUBLANES, CONV_CH)
    tails = ut[:, SUBLANES - (CONV_W - 1):, :]
    p_last = (jnp.arange(bp) + 1) * (seq_p // CHUNK) - 1
    s_last = np_rows // CHUNK + (jnp.arange(bs) + 1) * (seq_s // CHUNK) - 1
    return (y_p.reshape(bp, seq_p, D_MODEL),
            y_s.reshape(bs, seq_s, D_MODEL),
            ckv_p.reshape(1, bp, seq_p, KV_LORA),
            kr_p.reshape(1, bp, seq_p, ROPE_DIM),
            tails[p_last][None],
            ckv_s.reshape(1, bs, seq_s, KV_LORA),
            kr_s.reshape(1, bs, seq_s, ROPE_DIM),
            tails[s_last][None])
```

```python
import functools

import jax
import jax.numpy as jnp
from jax import lax
from jax.experimental import pallas as pl
from jax.experimental.pallas import tpu as pltpu

F32 = jnp.float32
BF16 = jnp.bfloat16

D_MODEL = 2048
N_HEADS = 8
QK_NOPE = 128
ROPE_DIM = 64
V_DIM = 128
Q_LORA = 512
KV_LORA = 512
ATTN_W = N_HEADS * V_DIM
CONV_CH = D_MODEL - ATTN_W
CONV_W = 3
CHUNK = 64
N_GROUPS = 4
EXPERTS_PER_GROUP = 8
N_EXPERTS = N_GROUPS * EXPERTS_PER_GROUP
D_FF = 512
ROPE_THETA = 10000.0
EPS = 1e-6
ATTN_SCALE = (QK_NOPE + ROPE_DIM) ** -0.5
EXP2_SCALE = ATTN_SCALE * 1.4426950408889634

LANES = 128
SUBLANES = 8
TM = 256
MOE_BLOCK = 256
TQ = 256
TK = 256
DMA_ISSUE_UNROLL = 8
NEG_BIG = -1e30
VMEM_LIMIT = 56 * 1024 * 1024


def _rms(v, g):
    return v * lax.rsqrt(jnp.mean(v * v, axis=-1, keepdims=True) + EPS) * g


def _lane_bcast(v, width):
    if width % LANES == 0:
        return jnp.concatenate([v] * (width // LANES), axis=1)
    assert width < LANES
    return v[:, :width]


def _const_spec(shape):
    nd = len(shape)
    return pl.BlockSpec(shape, lambda *_: (0,) * nd, pipeline_mode=pl.Buffered(1))


def _in_proj_kernel(xp_ref, xs_ref, gmix_ref, wa_ref, wc_ref, gq_ref, gkv_ref, gco_ref, convw_ref,
                    cos_ref, sin_ref, state_ref,
                    cqn_ref, ckvp_ref, krp_ref, ckvs_ref, krs_ref, convn_ref, utail_ref, ext_ref,
                    *, n_prompt_tiles, tiles_per_seq, n_prompt_seq, sample_seq_len):
    i = pl.program_id(0)

    def conv_block(u_sub, gate_sub, row0, length):
        ext_ref[SUBLANES:SUBLANES + length, :] = u_sub
        um1 = ext_ref[SUBLANES - 1:SUBLANES - 1 + length, :]
        um2 = ext_ref[SUBLANES - 2:SUBLANES - 2 + length, :]
        cw = convw_ref[...]
        conv = cw[0:1] * um2 + cw[1:2] * um1 + cw[2:3] * u_sub
        convn_ref[row0:row0 + length, :] = _rms(gate_sub * conv, gco_ref[...]).astype(BF16)

    def tile(x_ref, is_prompt):
        ckv_ref, kr_ref = (ckvp_ref, krp_ref) if is_prompt else (ckvs_ref, krs_ref)
        x = x_ref[...]
        xn = _rms(x, gmix_ref[...]).astype(BF16)
        za = jnp.dot(xn, wa_ref[...], preferred_element_type=F32)
        cqn_ref[...] = _rms(za[:, :Q_LORA], gq_ref[...]).astype(BF16)
        ckv_ref[...] = _rms(za[:, Q_LORA:Q_LORA + KV_LORA], gkv_ref[...])
        zk = za[:, Q_LORA + KV_LORA:]
        kr_ref[...] = zk[:, :ROPE_DIM] * cos_ref[...] + zk[:, ROPE_DIM:] * sin_ref[...]

        zc = jnp.dot(xn, wc_ref[...], preferred_element_type=F32)
        gate_b = zc[:, :CONV_CH]
        u = zc[:, CONV_CH:2 * CONV_CH] * zc[:, 2 * CONV_CH:]
        for j in range(TM // CHUNK):
            utail_ref[j] = u[CHUNK * (j + 1) - SUBLANES:CHUNK * (j + 1), :]

        if is_prompt:
            first = (i % tiles_per_seq) == 0

            @pl.when(first)
            def _():
                ext_ref[SUBLANES - 2:SUBLANES, :] = state_ref[i // tiles_per_seq]

            @pl.when(jnp.logical_not(first))
            def _():
                ext_ref[SUBLANES - 2:SUBLANES, :] = ext_ref[TM + SUBLANES - 2:TM + SUBLANES, :]

            conv_block(u, gate_b, 0, TM)
        else:
            n_sub = TM // sample_seq_len
            seq0 = n_prompt_seq + (i - n_prompt_tiles) * n_sub
            for k in range(n_sub):
                ext_ref[SUBLANES - 2:SUBLANES, :] = state_ref[seq0 + k]
                lo = k * sample_seq_len
                conv_block(u[lo:lo + sample_seq_len], gate_b[lo:lo + sample_seq_len], lo, sample_seq_len)

    @pl.when(i < n_prompt_tiles)
    def _():
        tile(xp_ref, True)

    @pl.when(i >= n_prompt_tiles)
    def _():
        tile(xs_ref, False)


def _in_proj(xp, xs, gmix, w_a, w_c, gq, gkv, gco, convw, cosk, sink, state, *, seq_p, seq_s):
    np_rows, ns_rows = xp.shape[0], xs.shape[0]
    m = np_rows + ns_rows
    npt, nst = np_rows // TM, ns_rows // TM
    tps = seq_p // TM
    n_prompt_seq = np_rows // seq_p
    last_p = npt - 1

    def tab_idx(i):
        return (jnp.where(i < npt, i % tps, tps), 0)

    row = lambda i: (i, 0)
    prow = lambda i: (jnp.minimum(i, last_p), 0)
    srow = lambda i: (jnp.maximum(i - npt, 0), 0)
    kern = functools.partial(_in_proj_kernel, n_prompt_tiles=npt, tiles_per_seq=tps,
                             n_prompt_seq=n_prompt_seq, sample_seq_len=seq_s)
    return pl.pallas_call(
        kern,
        grid=(npt + nst,),
        in_specs=[
            pl.BlockSpec((TM, D_MODEL), prow),
            pl.BlockSpec((TM, D_MODEL), srow),
            _const_spec((1, D_MODEL)),
            _const_spec(w_a.shape),
            _const_spec(w_c.shape),
            _const_spec((1, Q_LORA)),
            _const_spec((1, KV_LORA)),
            _const_spec((1, CONV_CH)),
            _const_spec((CONV_W, CONV_CH)),
            pl.BlockSpec((TM, ROPE_DIM), tab_idx),
            pl.BlockSpec((TM, ROPE_DIM), tab_idx),
            _const_spec(state.shape),
        ],
        out_specs=[
            pl.BlockSpec((TM, Q_LORA), row),
            pl.BlockSpec((TM, KV_LORA), prow),
            pl.BlockSpec((TM, ROPE_DIM), prow),
            pl.BlockSpec((TM, KV_LORA), srow),
            pl.BlockSpec((TM, ROPE_DIM), srow),
            pl.BlockSpec((TM, CONV_CH), row),
            pl.BlockSpec((TM // CHUNK, SUBLANES, CONV_CH), lambda i: (i, 0, 0)),
        ],
        out_shape=[
            jax.ShapeDtypeStruct((m, Q_LORA), BF16),
            jax.ShapeDtypeStruct((np_rows, KV_LORA), F32),
            jax.ShapeDtypeStruct((np_rows, ROPE_DIM), F32),
            jax.ShapeDtypeStruct((ns_rows, KV_LORA), F32),
            jax.ShapeDtypeStruct((ns_rows, ROPE_DIM), F32),
            jax.ShapeDtypeStruct((m, CONV_CH), BF16),
            jax.ShapeDtypeStruct((m // CHUNK, SUBLANES, CONV_CH), F32),
        ],
        scratch_shapes=[pltpu.VMEM((TM + SUBLANES, CONV_CH), F32)],
        compiler_params=pltpu.CompilerParams(dimension_semantics=("arbitrary",),
                                             vmem_limit_bytes=VMEM_LIMIT),
        name="in_proj",
    )(xp, xs, gmix, w_a, w_c, gq, gkv, gco, convw, cosk, sink, state)


def _attn_kernel(*refs, tq, n_past, causal):
    refs = list(refs)
    cqn_ref, wq_ref, wuk_ref, wuv_ref, cos_ref, sin_ref, gao_ref = refs[:7]
    refs = refs[7:]
    if n_past:
        pkv_ref, pkr_ref = refs[:2]
        refs = refs[2:]
    kv_ref, kr_ref, out_ref, qlat_ref, qr_ref, m_ref, l_ref, acc_ref, s_ref, klim_ref = refs

    qi = pl.program_id(1)
    rows = N_HEADS * tq

    q = jnp.dot(cqn_ref[...], wq_ref[...], preferred_element_type=F32)
    nope_w = N_HEADS * QK_NOPE
    rope_w = N_HEADS * ROPE_DIM
    qrope = q[:, nope_w:nope_w + rope_w] * cos_ref[...] + q[:, nope_w + rope_w:] * sin_ref[...]
    for h in range(N_HEADS):
        qn = q[:, h * QK_NOPE:(h + 1) * QK_NOPE].astype(BF16)
        ql = jnp.dot(qn, wuk_ref[h], preferred_element_type=F32)
        qlat_ref[h * tq:(h + 1) * tq, :] = ql.astype(BF16)
        qr_ref[h * tq:(h + 1) * tq, :] = qrope[:, h * ROPE_DIM:(h + 1) * ROPE_DIM].astype(BF16)

    m_ref[...] = jnp.full(m_ref.shape, NEG_BIG, F32)
    l_ref[...] = jnp.zeros(l_ref.shape, F32)
    acc_ref[...] = jnp.zeros(acc_ref.shape, F32)

    nt = (((1,), (1,)), ((), ()))

    def scores(kc_f32, kr_f32):
        s = lax.dot_general(qlat_ref[...], kc_f32.astype(BF16), nt, preferred_element_type=F32)
        return s + lax.dot_general(qr_ref[...], kr_f32.astype(BF16), nt, preferred_element_type=F32)

    def update(s, kc_f32, mask):
        if mask is not None:
            s = jnp.where(mask, s, NEG_BIG)
        m_prev = m_ref[...]
        m_new = jnp.maximum(m_prev, jnp.max(s, axis=-1, keepdims=True))
        alpha = jnp.exp2((m_prev - m_new) * EXP2_SCALE)
        p = jnp.exp2((s - _lane_bcast(m_new, s.shape[1])) * EXP2_SCALE)
        l_ref[...] = alpha * l_ref[...] + jnp.sum(p, axis=-1, keepdims=True)
        pv = jnp.dot(p.astype(BF16), kc_f32.astype(BF16), preferred_element_type=F32)
        acc_ref[...] = _lane_bcast(alpha, KV_LORA) * acc_ref[...] + pv
        m_ref[...] = m_new

    def pipelined(kv, kr, lo, hi, last, mask_fn):
        def body(j, c):
            k0 = pl.multiple_of(j * TK, TK)
            k1 = pl.multiple_of(jnp.minimum(j + 1, last) * TK, TK)
            s_cur = s_ref[j % 2]
            s_ref[(j + 1) % 2] = scores(kv[pl.ds(k1, TK), :], kr[pl.ds(k1, TK), :])
            update(s_cur, kv[pl.ds(k0, TK), :], None if mask_fn is None else mask_fn(k0))
            return c
        lax.fori_loop(lo, hi, body, 0)

    if n_past:
        n_pb = n_past // TK
        s_ref[0] = scores(pkv_ref[pl.ds(0, TK), :], pkr_ref[pl.ds(0, TK), :])
        pipelined(pkv_ref, pkr_ref, 0, n_pb, n_pb - 1, None)

    if causal:
        n_blocks = ((qi + 1) * tq + TK - 1) // TK
        n_full = jnp.minimum((qi * tq // CHUNK + 1) * CHUNK // TK, n_blocks)

        assert tq & (tq - 1) == 0 and CHUNK & (CHUNK - 1) == 0
        r = lax.broadcasted_iota(jnp.int32, (rows, LANES), 0)
        q_pos = qi * tq + (r & (tq - 1))
        klim_ref[...] = (q_pos & ~(CHUNK - 1)) + CHUNK

        def mask_fn(k0):
            cidx = lax.broadcasted_iota(jnp.int32, (rows, TK), 1)
            return cidx < _lane_bcast(klim_ref[...] - k0, TK)

        s_ref[0] = scores(kv_ref[pl.ds(0, TK), :], kr_ref[pl.ds(0, TK), :])
        pipelined(kv_ref, kr_ref, 0, n_full, n_blocks - 1, None)
        pipelined(kv_ref, kr_ref, n_full, n_blocks, n_blocks - 1, mask_fn)
    else:
        update(scores(kv_ref[...], kr_ref[...]), kv_ref[...], None)

    o = acc_ref[...] / _lane_bcast(l_ref[...], KV_LORA)
    parts = []
    for h in range(N_HEADS):
        oh = o[h * tq:(h + 1) * tq, :].astype(BF16)
        parts.append(jnp.dot(oh, wuv_ref[h], preferred_element_type=F32))
    attn = jnp.concatenate(parts, axis=-1)
    out_ref[...] = _rms(attn, gao_ref[...]).astype(BF16)


def _attention(cqn, w_q, w_ukt, w_uv, cosq, sinq, gao, ckv, krope, *, n_batch, seq, row0,
               past_kv=None, past_kr=None):
    causal = past_kv is None
    tq = TQ if causal else seq
    nq = seq // tq
    n_past = 0 if causal else past_kv.shape[1]
    if not causal:
        assert n_past % CHUNK == 0 and seq <= CHUNK and n_past % TK == 0
    blk0 = row0 // tq
    qrow = lambda b, q: (blk0 + b * nq + q, 0)
    in_specs = [
        pl.BlockSpec((tq, Q_LORA), qrow),
        _const_spec(w_q.shape),
        _const_spec(w_ukt.shape),
        _const_spec(w_uv.shape),
        pl.BlockSpec((tq, N_HEADS * ROPE_DIM), lambda b, q: (q, 0)),
        pl.BlockSpec((tq, N_HEADS * ROPE_DIM), lambda b, q: (q, 0)),
        _const_spec((1, ATTN_W)),
    ]
    args = [cqn, w_q, w_ukt, w_uv, cosq, sinq, gao]
    if n_past:
        in_specs += [pl.BlockSpec((None, n_past, KV_LORA), lambda b, q: (b, 0, 0)),
                     pl.BlockSpec((None, n_past, ROPE_DIM), lambda b, q: (b, 0, 0))]
        args += [past_kv, past_kr]
    in_specs += [pl.BlockSpec((seq, KV_LORA), lambda b, q: (b, 0)),
                 pl.BlockSpec((seq, ROPE_DIM), lambda b, q: (b, 0))]
    args += [ckv, krope]
    rows = N_HEADS * tq
    kern = functools.partial(_attn_kernel, tq=tq, n_past=n_past, causal=causal)
    return pl.pallas_call(
        kern,
        grid=(n_batch, nq),
        in_specs=in_specs,
        out_specs=pl.BlockSpec((tq, ATTN_W), lambda b, q: (b * nq + q, 0)),
        out_shape=jax.ShapeDtypeStruct((n_batch * seq, ATTN_W), BF16),
        scratch_shapes=[
            pltpu.VMEM((rows, KV_LORA), BF16),
            pltpu.VMEM((rows, ROPE_DIM), BF16),
            pltpu.VMEM((rows, LANES), F32),
            pltpu.VMEM((rows, LANES), F32),
            pltpu.VMEM((rows, KV_LORA), F32),
            pltpu.VMEM((2, rows, TK), F32),
            pltpu.VMEM((rows, LANES), jnp.int32),
        ],
        compiler_params=pltpu.CompilerParams(dimension_semantics=("arbitrary", "arbitrary"),
                                             vmem_limit_bytes=VMEM_LIMIT),
        name="attn_prompt" if causal else "attn_sample",
    )(*args)


def _out_proj_kernel(attnp_ref, attns_ref, convn_ref, xp_ref, xs_ref, woa_ref, woc_ref, gffn_ref, wr_ref,
                     br_ref, h_ref, xpk_ref, mi_ref, mf_ref, cnt_ref, carry_ref, *, n_prompt_tiles):
    i = pl.program_id(0)

    @pl.when(i == 0)
    def _():
        carry_ref[...] = jnp.zeros(carry_ref.shape, F32)

    def tile(x_ref, attn_ref):
        y = jnp.dot(attn_ref[...], woa_ref[...], preferred_element_type=F32)
        y = y + jnp.dot(convn_ref[...], woc_ref[...], preferred_element_type=F32)
        h = x_ref[...] + y
        h_ref[...] = h
        xn = _rms(h, gffn_ref[...])

        half = D_MODEL // 2
        xh = xn.astype(BF16)
        xh32 = xh.astype(F32)
        lo = lax.bitcast_convert_type(xh32[:, :half], jnp.uint32)
        hi = lax.bitcast_convert_type(xh32[:, half:], jnp.uint32)
        xpk_ref[...] = (lo >> 16) | (hi & jnp.uint32(0xFFFF0000))

        xl = (xn - xh32).astype(BF16)
        hh_hl = jnp.dot(xh, wr_ref[...], preferred_element_type=F32)
        lh = jnp.dot(xl, wr_ref[:, :LANES], preferred_element_type=F32)
        logits = hh_hl[:, :LANES] + (lh + hh_hl[:, LANES:]) + br_ref[...]
        lane = lax.broadcasted_iota(jnp.int32, (TM, LANES), 1).astype(F32)
        ninf = -jnp.inf
        far = float(LANES)

        def first_argmax(v):
            vmax = jnp.max(v, axis=-1, keepdims=True)
            return vmax, jnp.min(jnp.where(v == vmax, lane, far), axis=-1, keepdims=True)

        gl = jnp.where(lane < N_GROUPS, logits, ninf)
        gmax, gidx = first_argmax(gl)
        g_p = 1.0 / jnp.sum(jnp.exp(gl - gmax), axis=-1, keepdims=True)
        e_lo = N_GROUPS + EXPERTS_PER_GROUP * gidx
        el = jnp.where((lane >= e_lo) & (lane < e_lo + EXPERTS_PER_GROUP), logits, ninf)
        e1max, i1 = first_argmax(el)
        z = jnp.sum(jnp.exp(el - e1max), axis=-1, keepdims=True)
        el2 = jnp.where(lane == i1, ninf, el)
        e2max, i2 = first_argmax(el2)
        p1 = 1.0 / z
        p2 = jnp.exp(e2max - e1max) / z
        den = p1 + p2
        g0 = g_p * p1 / den
        g1 = g_p * p2 / den
        e0 = i1 - N_GROUPS
        e1 = i2 - N_GROUPS

        oh0 = lane == e0
        oh1 = lane == e1
        oh = jnp.where(oh0 | oh1, 1.0, 0.0)
        r = lax.broadcasted_iota(jnp.int32, (TM, TM), 0)
        c = lax.broadcasted_iota(jnp.int32, (TM, TM), 1)
        ltri = jnp.where(r > c, 1.0, 0.0).astype(BF16)
        before = jnp.dot(ltri, oh.astype(BF16), preferred_element_type=F32) + carry_ref[...]
        rank0 = jnp.sum(jnp.where(oh0, before, 0.0), axis=-1, keepdims=True)
        rank1 = jnp.sum(jnp.where(oh1, before, 0.0), axis=-1, keepdims=True)
        total = carry_ref[...] + jnp.sum(oh, axis=0, keepdims=True)
        carry_ref[...] = total
        cnt_ref[...] = jnp.broadcast_to(total, cnt_ref.shape)

        mi = jnp.where(lane == 0, e0, jnp.where(lane == 1, e1, jnp.where(lane == 2, rank0, rank1)))
        mi_ref[...] = jnp.transpose(mi)[:SUBLANES, :].astype(jnp.int32)
        mf_ref[...] = jnp.where(lane == 0, g0, g1)

    @pl.when(i < n_prompt_tiles)
    def _():
        tile(xp_ref, attnp_ref)

    @pl.when(i >= n_prompt_tiles)
    def _():
        tile(xs_ref, attns_ref)


def _out_proj(attn_p, attn_s, conv_n, xp, xs, w_oa, w_oc, gffn, w_r2, b_r):
    m = conv_n.shape[0]
    npt = xp.shape[0] // TM
    last_p = npt - 1
    row = lambda i: (i, 0)
    return pl.pallas_call(
        functools.partial(_out_proj_kernel, n_prompt_tiles=npt),
        grid=(m // TM,),
        in_specs=[
            pl.BlockSpec((TM, ATTN_W), lambda i: (jnp.minimum(i, last_p), 0)),
            pl.BlockSpec((TM, ATTN_W), lambda i: (jnp.maximum(i - npt, 0), 0)),
            pl.BlockSpec((TM, CONV_CH), row),
            pl.BlockSpec((TM, D_MODEL), lambda i: (jnp.minimum(i, last_p), 0)),
            pl.BlockSpec((TM, D_MODEL), lambda i: (jnp.maximum(i - npt, 0), 0)),
            _const_spec(w_oa.shape),
            _const_spec(w_oc.shape),
            _const_spec((1, D_MODEL)),
            _const_spec(w_r2.shape),
            _const_spec((1, LANES)),
        ],
        out_specs=[
            pl.BlockSpec((TM, D_MODEL), row),
            pl.BlockSpec((TM, D_MODEL // 2), row),
            pl.BlockSpec((SUBLANES, TM), lambda i: (0, i)),
            pl.BlockSpec((TM, LANES), row),
            pl.BlockSpec((SUBLANES, LANES), lambda i: (0, 0)),
        ],
        out_shape=[
            jax.ShapeDtypeStruct((m, D_MODEL), F32),
            jax.ShapeDtypeStruct((m, D_MODEL // 2), jnp.uint32),
            jax.ShapeDtypeStruct((SUBLANES, m), jnp.int32),
            jax.ShapeDtypeStruct((m, LANES), F32),
            jax.ShapeDtypeStruct((SUBLANES, LANES), F32),
        ],
        scratch_shapes=[pltpu.VMEM((1, LANES), F32)],
        compiler_params=pltpu.CompilerParams(dimension_semantics=("arbitrary",),
                                             vmem_limit_bytes=VMEM_LIMIT),
        name="out_proj",
    )(attn_p, attn_s, conv_n, xp, xs, w_oa, w_oc, gffn, w_r2, b_r)


def _dispatch_kernel(d0_ref, d1_ref, zlo_ref, zn_ref, nu_ref, xpk_ref, xs_hbm, zeros_ref, sems, *, n_blocks):
    i = pl.program_id(0)
    sem = sems.at[0]
    zsem = sems.at[1]

    def row_copy(src_ref, src_row, dst_row):
        return pltpu.make_async_copy(src_ref.at[pl.ds(src_row, 1)], xs_hbm.at[pl.ds(dst_row, 1)], sem)

    def zero_fill(act):
        def per_expert(e, c):
            lo = zlo_ref[e]
            n = zn_ref[e]
            head = (-lo) & (SUBLANES - 1)
            for r in range(SUBLANES - 1):
                @pl.when(r < head)
                def _(r=r):
                    act(pltpu.make_async_copy(zeros_ref.at[pl.ds(0, 1)], xs_hbm.at[pl.ds(lo + r, 1)], zsem))
            off = lo + head
            rest = n - head
            size = MOE_BLOCK // 2
            while size >= SUBLANES:
                @pl.when((rest & size) != 0)
                def _(off=off, size=size):
                    dst = xs_hbm.at[pl.ds(pl.multiple_of(off, SUBLANES), size)]
                    act(pltpu.make_async_copy(zeros_ref.at[pl.ds(0, size)], dst, zsem))
                off = off + (rest & size)
                size //= 2
            return c

        def per_block(b, c):
            dst = xs_hbm.at[pl.ds(pl.multiple_of(b * MOE_BLOCK, MOE_BLOCK), MOE_BLOCK)]
            act(pltpu.make_async_copy(zeros_ref, dst, zsem))
            return c

        lax.fori_loop(0, N_EXPERTS, per_expert, 0)
        lax.fori_loop(nu_ref[0], n_blocks, per_block, 0)

    @pl.when(i == 0)
    def _():
        zeros_ref[...] = jnp.zeros(zeros_ref.shape, zeros_ref.dtype)
        zero_fill(lambda cp: cp.start())
        zero_fill(lambda cp: cp.wait())

    base = i * TM

    def start(r, c):
        row_copy(xpk_ref, r, d0_ref[base + r]).start()
        row_copy(xpk_ref, r, d1_ref[base + r]).start()
        return c

    lax.fori_loop(0, TM, start, 0, unroll=DMA_ISSUE_UNROLL)
    for _ in range(2):
        pltpu.make_async_copy(xpk_ref, xs_hbm.at[pl.ds(0, TM)], sem).wait()


def _dispatch(dest0, dest1, pad_lo, n_pad, n_used, xpk, n_blocks):
    m = xpk.shape[0]
    grid_spec = pltpu.PrefetchScalarGridSpec(
        num_scalar_prefetch=5,
        grid=(m // TM,),
        in_specs=[pl.BlockSpec((TM, D_MODEL // 2), lambda i, *_: (i, 0))],
        out_specs=pl.BlockSpec(memory_space=pl.ANY),
        scratch_shapes=[pltpu.VMEM((MOE_BLOCK, D_MODEL // 2), jnp.uint32),
                        pltpu.SemaphoreType.DMA((2,))],
    )
    return pl.pallas_call(
        functools.partial(_dispatch_kernel, n_blocks=n_blocks),
        grid_spec=grid_spec,
        out_shape=jax.ShapeDtypeStruct((n_blocks * MOE_BLOCK, D_MODEL // 2), jnp.uint32),
        compiler_params=pltpu.CompilerParams(dimension_semantics=("arbitrary",)),
        name="dispatch",
    )(dest0, dest1, pad_lo, n_pad, n_used, xpk)


def _experts_kernel(be_ref, nu_ref, nxt_ref, x_ref, wg_hbm, wu_hbm, wd_hbm, y_ref,
                    sg_ref, su_ref, sd_ref, wgb_ref, wub_ref, wdb_ref, sems):
    b = pl.program_id(0)
    active = b < nu_ref[0]
    new_expert = jnp.logical_or(b == 0, be_ref[b] != be_ref[jnp.maximum(b - 1, 0)])

    def weight_copies(e):
        return (pltpu.make_async_copy(wg_hbm.at[e], sg_ref, sems.at[0]),
                pltpu.make_async_copy(wu_hbm.at[e], su_ref, sems.at[1]),
                pltpu.make_async_copy(wd_hbm.at[e], sd_ref, sems.at[2]))

    @pl.when(b == 0)
    def _():
        for cp in weight_copies(be_ref[0]):
            cp.start()

    @pl.when(jnp.logical_and(active, new_expert))
    def _():
        for cp in weight_copies(be_ref[b]):
            cp.wait()
        wgb_ref[...] = sg_ref[...].astype(BF16)
        wub_ref[...] = su_ref[...].astype(BF16)
        wdb_ref[...] = sd_ref[...].astype(BF16)

        @pl.when(nxt_ref[b] >= 0)
        def _():
            for cp in weight_copies(nxt_ref[b]):
                cp.start()

    @pl.when(active)
    def _():
        half = D_MODEL // 2
        xw = x_ref[...]
        xa = lax.bitcast_convert_type(xw << 16, F32).astype(BF16)
        xb = lax.bitcast_convert_type(xw & jnp.uint32(0xFFFF0000), F32).astype(BF16)
        g = jnp.dot(xa, wgb_ref[:half, :], preferred_element_type=F32)
        g = g + jnp.dot(xb, wgb_ref[half:, :], preferred_element_type=F32)
        u = jnp.dot(xa, wub_ref[:half, :], preferred_element_type=F32)
        u = u + jnp.dot(xb, wub_ref[half:, :], preferred_element_type=F32)
        hmid = (g * jax.nn.sigmoid(g)) * u
        y_ref[...] = jnp.dot(hmid.astype(BF16), wdb_ref[...], preferred_element_type=F32)

    @pl.when(b >= nu_ref[0])
    def _():
        y_ref[...] = jnp.zeros(y_ref.shape, y_ref.dtype)


def _experts(block_e, n_used, next_e, x_sorted, w_gate, w_up, w_down):
    p = x_sorted.shape[0]
    nb = p // MOE_BLOCK

    def xrow(b, be, nu, nxt):
        return (jnp.maximum(jnp.minimum(b, nu[0] - 1), 0), 0)

    grid_spec = pltpu.PrefetchScalarGridSpec(
        num_scalar_prefetch=3,
        grid=(nb,),
        in_specs=[
            pl.BlockSpec((MOE_BLOCK, D_MODEL // 2), xrow),
            pl.BlockSpec(memory_space=pl.ANY),
            pl.BlockSpec(memory_space=pl.ANY),
            pl.BlockSpec(memory_space=pl.ANY),
        ],
        out_specs=pl.BlockSpec((MOE_BLOCK, D_MODEL), lambda b, be, nu, nxt: (b, 0)),
        scratch_shapes=[pltpu.VMEM((D_MODEL, D_FF), F32), pltpu.VMEM((D_MODEL, D_FF), F32),
                        pltpu.VMEM((D_FF, D_MODEL), F32),
                        pltpu.VMEM((D_MODEL, D_FF), BF16), pltpu.VMEM((D_MODEL, D_FF), BF16),
                        pltpu.VMEM((D_FF, D_MODEL), BF16),
                        pltpu.SemaphoreType.DMA((3,))],
    )
    return pl.pallas_call(
        _experts_kernel,
        grid_spec=grid_spec,
        out_shape=jax.ShapeDtypeStruct((p, D_MODEL), F32),
        compiler_params=pltpu.CompilerParams(dimension_semantics=("arbitrary",),
                                             vmem_limit_bytes=VMEM_LIMIT),
        name="experts",
    )(block_e, n_used, next_e, x_sorted, w_gate, w_up, w_down)


def _combine_kernel(d0_ref, d1_ref, h_ref, mf_ref, gfin_ref, y_hbm, outp_ref, outs_ref, y0_ref, y1_ref, sems,
                    *, n_tiles, n_prompt_tiles):
    i = pl.program_id(0)

    def gather(tile, slot, act):
        base = tile * TM

        def body(r, c):
            act(pltpu.make_async_copy(y_hbm.at[pl.ds(d0_ref[base + r], 1)], y0_ref.at[slot, pl.ds(r, 1)],
                                      sems.at[slot]))
            act(pltpu.make_async_copy(y_hbm.at[pl.ds(d1_ref[base + r], 1)], y1_ref.at[slot, pl.ds(r, 1)],
                                      sems.at[slot]))
            return c
        lax.fori_loop(0, TM, body, 0, unroll=DMA_ISSUE_UNROLL)

    @pl.when(i == 0)
    def _():
        gather(0, 0, lambda cp: cp.start())

    @pl.when(i + 1 < n_tiles)
    def _():
        gather(i + 1, (i + 1) % 2, lambda cp: cp.start())

    slot = i % 2
    pltpu.make_async_copy(y_hbm.at[pl.ds(0, TM)], y0_ref.at[slot], sems.at[slot]).wait()
    pltpu.make_async_copy(y_hbm.at[pl.ds(0, TM)], y1_ref.at[slot], sems.at[slot]).wait()
    mf = mf_ref[...]
    ffn = mf[:, 0:1] * y0_ref[slot] + mf[:, 1:2] * y1_ref[slot]
    out = _rms(h_ref[...] + ffn, gfin_ref[...])

    @pl.when(i < n_prompt_tiles)
    def _():
        outp_ref[...] = out

    @pl.when(i >= n_prompt_tiles)
    def _():
        outs_ref[...] = out


def _combine(dest0, dest1, h, mf, gfin, y_sorted, *, n_prompt_rows):
    m = h.shape[0]
    npt = n_prompt_rows // TM
    grid_spec = pltpu.PrefetchScalarGridSpec(
        num_scalar_prefetch=2,
        grid=(m // TM,),
        in_specs=[
            pl.BlockSpec((TM, D_MODEL), lambda i, *_: (i, 0)),
            pl.BlockSpec((TM, LANES), lambda i, *_: (i, 0)),
            pl.BlockSpec((1, D_MODEL), lambda i, *_: (0, 0)),
            pl.BlockSpec(memory_space=pl.ANY),
        ],
        out_specs=[pl.BlockSpec((TM, D_MODEL), lambda i, *_: (jnp.minimum(i, npt - 1), 0)),
                   pl.BlockSpec((TM, D_MODEL), lambda i, *_: (jnp.maximum(i - npt, 0), 0))],
        scratch_shapes=[pltpu.VMEM((2, TM, D_MODEL), F32), pltpu.VMEM((2, TM, D_MODEL), F32),
                        pltpu.SemaphoreType.DMA((2,))],
    )
    return pl.pallas_call(
        functools.partial(_combine_kernel, n_tiles=m // TM, n_prompt_tiles=npt),
        grid_spec=grid_spec,
        out_shape=[jax.ShapeDtypeStruct((n_prompt_rows, D_MODEL), F32),
                   jax.ShapeDtypeStruct((m - n_prompt_rows, D_MODEL), F32)],
        compiler_params=pltpu.CompilerParams(dimension_semantics=("arbitrary",),
                                             vmem_limit_bytes=VMEM_LIMIT),
        name="combine",
    )(dest0, dest1, h, mf, gfin, y_sorted)


def _rope_tables(pos):
    inv = ROPE_THETA ** (-jnp.arange(0, ROPE_DIM, 2, dtype=F32) / ROPE_DIM)
    ang = pos.astype(F32)[:, None] * inv[None, :]
    cos, sin = jnp.cos(ang), jnp.sin(ang)
    return jnp.concatenate([cos, cos], axis=-1), jnp.concatenate([-sin, sin], axis=-1)


def _swap_halves(w):
    return jnp.concatenate([w[..., ROPE_DIM // 2:], w[..., :ROPE_DIM // 2]], axis=-1)


def kernel(x_prompt, x_sample, cache_kv_latent, cache_k_rope, state_conv, norm_mix, w_in, norm_q, w_uq,
           norm_kv, w_uk, w_uv, conv_w, norm_attn_out, norm_conv_out, w_o, norm_ffn, w_router_group,
           b_router_group, w_router_expert, b_router_expert, w_gate, w_up, w_down, norm_final):
    assert w_in.shape[0] == 1, "single-layer trunk"
    bp, seq_p, _ = x_prompt.shape
    bs, seq_s, _ = x_sample.shape
    past_len = cache_kv_latent.shape[2]
    np_rows, ns_rows = bp * seq_p, bs * seq_s
    m = np_rows + ns_rows
    assert seq_p % TM == 0 and TM % seq_s == 0 and ns_rows % TM == 0 and seq_s == CHUNK

    xp = x_prompt.reshape(np_rows, D_MODEL)
    xs = x_sample.reshape(ns_rows, D_MODEL)
    row_vec = lambda v: v.reshape(1, -1)

    w_in0 = w_in[0]
    mla_w = Q_LORA + KV_LORA + ROPE_DIM
    w_a = jnp.concatenate([w_in0[:, :mla_w], _swap_halves(w_in0[:, mla_w - ROPE_DIM:mla_w])], axis=1).astype(BF16)
    w_c = w_in0[:, mla_w:].astype(BF16)
    wq4 = w_uq[0].reshape(Q_LORA, N_HEADS, QK_NOPE + ROPE_DIM)
    wq_rope = wq4[:, :, QK_NOPE:]
    w_q = jnp.concatenate([wq4[:, :, :QK_NOPE].reshape(Q_LORA, -1), wq_rope.reshape(Q_LORA, -1),
                           _swap_halves(wq_rope).reshape(Q_LORA, -1)], axis=1).astype(BF16)
    w_ukt = jnp.transpose(w_uk[0], (1, 2, 0)).astype(BF16)
    w_uvh = jnp.transpose(w_uv[0], (1, 0, 2)).astype(BF16)
    w_oa = w_o[0, :ATTN_W].astype(BF16)
    w_oc = w_o[0, ATTN_W:].astype(BF16)
    n_router = N_GROUPS + N_EXPERTS
    w_r = jnp.concatenate([w_router_group[0], w_router_expert[0].reshape(D_MODEL, N_EXPERTS)], axis=1)
    w_r = jnp.pad(w_r, ((0, 0), (0, LANES - n_router)))
    w_rh = w_r.astype(BF16)
    w_rl = (w_r - w_rh.astype(F32)).astype(BF16)
    w_r2 = jnp.concatenate([w_rh, w_rl], axis=1)
    b_r =jnp.pad(jnp.concatenate([b_router_group[0], b_router_expert[0].reshape(N_EXPERTS)]),
                  (0, LANES - n_router)).reshape(1, LANES)

    pos_p = jnp.arange(seq_p, dtype=jnp.int32)
    pos_s = past_len + jnp.arange(seq_s, dtype=jnp.int32)
    cos_p, sin_p = _rope_tables(pos_p)
    cos_s, sin_s = _rope_tables(pos_s)
    cosk = jnp.concatenate([cos_p, jnp.tile(cos_s, (TM // seq_s, 1))], axis=0)
    sink = jnp.concatenate([sin_p, jnp.tile(sin_s, (TM // seq_s, 1))], axis=0)
    state = jnp.concatenate([jnp.zeros((bp, CONV_W - 1, CONV_CH), F32), state_conv[0]], axis=0)

    cqn, ckv_p, kr_p, ckv_s, kr_s, conv_n, utail = _in_proj(
        xp, xs, row_vec(norm_mix[0]), w_a, w_c, row_vec(norm_q[0]), row_vec(norm_kv[0]),
        row_vec(norm_conv_out[0]), conv_w[0], cosk, sink, state, seq_p=seq_p, seq_s=seq_s)

    gao = row_vec(norm_attn_out[0])
    attn_p = _attention(cqn, w_q, w_ukt, w_uvh, jnp.tile(cos_p, (1, N_HEADS)), jnp.tile(sin_p, (1, N_HEADS)),
                        gao, ckv_p, kr_p, n_batch=bp, seq=seq_p, row0=0)
    attn_s = _attention(cqn, w_q, w_ukt, w_uvh, jnp.tile(cos_s, (1, N_HEADS)), jnp.tile(sin_s, (1, N_HEADS)),
                        gao, ckv_s, kr_s, n_batch=bs, seq=seq_s, row0=np_rows,
                        past_kv=cache_kv_latent[0], past_kr=cache_k_rope[0])

    h, xpk, mi, mf, cnt = _out_proj(attn_p, attn_s, conv_n, xp, xs, w_oa, w_oc, row_vec(norm_ffn[0]),
                                    w_r2, b_r)

    counts = cnt[0, :N_EXPERTS].astype(jnp.int32)
    padded = (counts + MOE_BLOCK - 1) // MOE_BLOCK * MOE_BLOCK
    pad_end = jnp.cumsum(padded)
    pad_start = pad_end - padded
    n_blocks = -(-(m * 2) // MOE_BLOCK) + N_EXPERTS
    block_row0 = jnp.arange(n_blocks, dtype=jnp.int32) * MOE_BLOCK
    block_e = jnp.minimum(jnp.sum((pad_end[None, :] <= block_row0[:, None]).astype(jnp.int32), axis=1),
                          N_EXPERTS - 1)
    n_used = (pad_end[-1:] // MOE_BLOCK).astype(jnp.int32)
    expert_ids = jnp.arange(N_EXPERTS, dtype=jnp.int32)[:, None]

    def seg_start(e):
        return jnp.sum(jnp.where(expert_ids == e[None, :], pad_start[:, None], 0), axis=0)

    dest0 = seg_start(mi[0]) + mi[2]
    dest1 = seg_start(mi[1]) + mi[3]

    x_sorted = _dispatch(dest0, dest1, pad_start + counts, padded - counts, n_used, xpk, n_blocks)
    later = (expert_ids.T > block_e[:, None]) & (padded > 0)[None, :]
    next_e = jnp.min(jnp.where(later, expert_ids.T, N_EXPERTS), axis=1)
    next_e = jnp.where(next_e == N_EXPERTS, -1, next_e).astype(jnp.int32)
    y_sorted = _experts(block_e, n_used, next_e, x_sorted, w_gate[0], w_up[0], w_down[0])
    gfin = row_vec(norm_final)
    y_p, y_s = _combine(dest0, dest1, h, mf, gfin, y_sorted, n_prompt_rows=np_rows)

    ut = utail.reshape(m // CHUNK, SUBLANES, CONV_CH)
    tails = ut[:, SUBLANES - (CONV_W - 1):, :]
    p_last = (jnp.arange(bp) + 1) * (seq_p // CHUNK) - 1
    s_last = np_rows // CHUNK + (jnp.arange(bs) + 1) * (seq_s // CHUNK) - 1
    return (y_p.reshape(bp, seq_p, D_MODEL),
            y_s.reshape(bs, seq_s, D_MODEL),
            ckv_p.reshape(1, bp, seq_p, KV_LORA),
            kr_p.reshape(1, bp, seq_p, ROPE_DIM),
            tails[p_last][None],
            ckv_s.reshape(1, bs, seq_s, KV_LORA),
            kr_s.reshape(1, bs, seq_s, ROPE_DIM),
            tails[s_last][None])
```

```python
import functools

import jax
import jax.numpy as jnp
from jax import lax
from jax.experimental import pallas as pl
from jax.experimental.pallas import tpu as pltpu

F32 = jnp.float32
BF16 = jnp.bfloat16

D_MODEL = 2048
N_HEADS = 8
QK_NOPE = 128
ROPE_DIM = 64
V_DIM = 128
Q_LORA = 512
KV_LORA = 512
ATTN_W = N_HEADS * V_DIM
CONV_CH = D_MODEL - ATTN_W
CONV_W = 3
CHUNK = 64
N_GROUPS = 4
EXPERTS_PER_GROUP = 8
N_EXPERTS = N_GROUPS * EXPERTS_PER_GROUP
D_FF = 512
ROPE_THETA = 10000.0
EPS = 1e-6
ATTN_SCALE = (QK_NOPE + ROPE_DIM) ** -0.5
EXP2_SCALE = ATTN_SCALE * 1.4426950408889634

LANES = 128
SUBLANES = 8
TM = 256
MOE_BLOCK = 256
TQ = 256
TK = 256
DMA_ISSUE_UNROLL = 8
NEG_BIG = -1e30
VMEM_LIMIT = 56 * 1024 * 1024


def _rms(v, g):
    return v * lax.rsqrt(jnp.mean(v * v, axis=-1, keepdims=True) + EPS) * g


def _lane_bcast(v, width):
    if width % LANES == 0:
        return jnp.concatenate([v] * (width // LANES), axis=1)
    assert width < LANES
    return v[:, :width]


def _const_spec(shape):
    nd = len(shape)
    return pl.BlockSpec(shape, lambda *_: (0,) * nd, pipeline_mode=pl.Buffered(1))


def _in_proj_kernel(xp_ref, xs_ref, gmix_ref, w_ref, gq_ref, gkv_ref, gco_ref, convw_ref,
                    cos_ref, sin_ref, state_ref,
                    cqn_ref, ckvp_ref, krp_ref, ckvs_ref, krs_ref, convn_ref, utail_ref, ext_ref,
                    *, n_prompt_tiles, tiles_per_seq, n_prompt_seq, sample_seq_len):
    i = pl.program_id(0)

    def conv_block(u_sub, gate_sub, row0, length):
        ext_ref[SUBLANES:SUBLANES + length, :] = u_sub
        um1 = ext_ref[SUBLANES - 1:SUBLANES - 1 + length, :]
        um2 = ext_ref[SUBLANES - 2:SUBLANES - 2 + length, :]
        cw = convw_ref[...]
        conv = cw[0:1] * um2 + cw[1:2] * um1 + cw[2:3] * u_sub
        convn_ref[row0:row0 + length, :] = _rms(gate_sub * conv, gco_ref[...]).astype(BF16)

    def tile(x_ref, is_prompt):
        ckv_ref, kr_ref = (ckvp_ref, krp_ref) if is_prompt else (ckvs_ref, krs_ref)
        x = x_ref[...]
        xn = _rms(x, gmix_ref[...]).astype(BF16)
        lat_w = Q_LORA + KV_LORA
        conv_end = lat_w + 3 * CONV_CH
        za = jnp.dot(xn, w_ref[:, :lat_w], preferred_element_type=F32)
        cqn_ref[...] = _rms(za[:, :Q_LORA], gq_ref[...]).astype(BF16)
        ckv_ref[...] = _rms(za[:, Q_LORA:], gkv_ref[...])
        zk = jnp.dot(xn, w_ref[:, conv_end:], preferred_element_type=F32)
        kr_ref[...] = zk[:, :ROPE_DIM] * cos_ref[...] + zk[:, ROPE_DIM:] * sin_ref[...]

        zc = jnp.dot(xn, w_ref[:, lat_w:conv_end], preferred_element_type=F32)
        gate_b = zc[:, :CONV_CH]
        u = zc[:, CONV_CH:2 * CONV_CH] * zc[:, 2 * CONV_CH:]
        for j in range(TM // CHUNK):
            utail_ref[j] = u[CHUNK * (j + 1) - SUBLANES:CHUNK * (j + 1), :]

        if is_prompt:
            first = (i % tiles_per_seq) == 0

            @pl.when(first)
            def _():
                ext_ref[SUBLANES - 2:SUBLANES, :] = state_ref[i // tiles_per_seq]

            @pl.when(jnp.logical_not(first))
            def _():
                ext_ref[SUBLANES - 2:SUBLANES, :] = ext_ref[TM + SUBLANES - 2:TM + SUBLANES, :]

            conv_block(u, gate_b, 0, TM)
        else:
            n_sub = TM // sample_seq_len
            seq0 = n_prompt_seq + (i - n_prompt_tiles) * n_sub
            for k in range(n_sub):
                ext_ref[SUBLANES - 2:SUBLANES, :] = state_ref[seq0 + k]
                lo = k * sample_seq_len
                conv_block(u[lo:lo + sample_seq_len], gate_b[lo:lo + sample_seq_len], lo, sample_seq_len)

    @pl.when(i < n_prompt_tiles)
    def _():
        tile(xp_ref, True)

    @pl.when(i >= n_prompt_tiles)
    def _():
        tile(xs_ref, False)


def _in_proj(xp, xs, gmix, w_all, gq, gkv, gco, convw, cosk, sink, state, *, seq_p, seq_s):
    np_rows, ns_rows = xp.shape[0], xs.shape[0]
    m = np_rows + ns_rows
    npt, nst = np_rows // TM, ns_rows // TM
    tps = seq_p // TM
    n_prompt_seq = np_rows // seq_p
    last_p = npt - 1

    def tab_idx(i):
        return (jnp.where(i < npt, i % tps, tps), 0)

    row = lambda i: (i, 0)
    prow = lambda i: (jnp.minimum(i, last_p), 0)
    srow = lambda i: (jnp.maximum(i - npt, 0), 0)
    kern = functools.partial(_in_proj_kernel, n_prompt_tiles=npt, tiles_per_seq=tps,
                             n_prompt_seq=n_prompt_seq, sample_seq_len=seq_s)
    return pl.pallas_call(
        kern,
        grid=(npt + nst,),
        in_specs=[
            pl.BlockSpec((TM, D_MODEL), prow),
            pl.BlockSpec((TM, D_MODEL), srow),
            _const_spec((1, D_MODEL)),
            _const_spec(w_all.shape),
            _const_spec((1, Q_LORA)),
            _const_spec((1, KV_LORA)),
            _const_spec((1, CONV_CH)),
            _const_spec((CONV_W, CONV_CH)),
            pl.BlockSpec((TM, ROPE_DIM), tab_idx),
            pl.BlockSpec((TM, ROPE_DIM), tab_idx),
            _const_spec(state.shape),
        ],
        out_specs=[
            pl.BlockSpec((TM, Q_LORA), row),
            pl.BlockSpec((TM, KV_LORA), prow),
            pl.BlockSpec((TM, ROPE_DIM), prow),
            pl.BlockSpec((TM, KV_LORA), srow),
            pl.BlockSpec((TM, ROPE_DIM), srow),
            pl.BlockSpec((TM, CONV_CH), row),
            pl.BlockSpec((TM // CHUNK, SUBLANES, CONV_CH), lambda i: (i, 0, 0)),
        ],
        out_shape=[
            jax.ShapeDtypeStruct((m, Q_LORA), BF16),
            jax.ShapeDtypeStruct((np_rows, KV_LORA), F32),
            jax.ShapeDtypeStruct((np_rows, ROPE_DIM), F32),
            jax.ShapeDtypeStruct((ns_rows, KV_LORA), F32),
            jax.ShapeDtypeStruct((ns_rows, ROPE_DIM), F32),
            jax.ShapeDtypeStruct((m, CONV_CH), BF16),
            jax.ShapeDtypeStruct((m // CHUNK, SUBLANES, CONV_CH), F32),
        ],
        scratch_shapes=[pltpu.VMEM((TM + SUBLANES, CONV_CH), F32)],
        compiler_params=pltpu.CompilerParams(dimension_semantics=("arbitrary",),
                                             vmem_limit_bytes=VMEM_LIMIT),
        name="in_proj",
    )(xp, xs, gmix, w_all, gq, gkv, gco, convw, cosk, sink, state)


def _attn_kernel(*refs, tq, n_past, causal):
    refs = list(refs)
    cqn_ref, wq_ref, wuk_ref, wuv_ref, cos_ref, sin_ref, gao_ref = refs[:7]
    refs = refs[7:]
    if n_past:
        pkv_ref, pkr_ref = refs[:2]
        refs = refs[2:]
    kv_ref, kr_ref, out_ref, qlat_ref, qr_ref, m_ref, l_ref, acc_ref, s_ref, klim_ref = refs

    qi = pl.program_id(1)
    rows = N_HEADS * tq

    q = jnp.dot(cqn_ref[...], wq_ref[...], preferred_element_type=F32)
    nope_w = N_HEADS * QK_NOPE
    rope_w = N_HEADS * ROPE_DIM
    qrope = q[:, nope_w:nope_w + rope_w] * cos_ref[...] + q[:, nope_w + rope_w:] * sin_ref[...]
    for h in range(N_HEADS):
        qn = q[:, h * QK_NOPE:(h + 1) * QK_NOPE].astype(BF16)
        ql = jnp.dot(qn, wuk_ref[h], preferred_element_type=F32)
        qlat_ref[h * tq:(h + 1) * tq, :] = ql.astype(BF16)
        qr_ref[h * tq:(h + 1) * tq, :] = qrope[:, h * ROPE_DIM:(h + 1) * ROPE_DIM].astype(BF16)

    m_ref[...] = jnp.full(m_ref.shape, NEG_BIG, F32)
    l_ref[...] = jnp.zeros(l_ref.shape, F32)
    acc_ref[...] = jnp.zeros(acc_ref.shape, F32)

    nt = (((1,), (1,)), ((), ()))

    def scores(kc_f32, kr_f32):
        s = lax.dot_general(qlat_ref[...], kc_f32.astype(BF16), nt, preferred_element_type=F32)
        return s + lax.dot_general(qr_ref[...], kr_f32.astype(BF16), nt, preferred_element_type=F32)

    def update(s, kc_f32, mask):
        if mask is not None:
            s = jnp.where(mask, s, NEG_BIG)
        m_prev = m_ref[...]
        m_new = jnp.maximum(m_prev, jnp.max(s, axis=-1, keepdims=True))
        alpha = jnp.exp2((m_prev - m_new) * EXP2_SCALE)
        p = jnp.exp2((s - _lane_bcast(m_new, s.shape[1])) * EXP2_SCALE)
        l_ref[...] = alpha * l_ref[...] + jnp.sum(p, axis=-1, keepdims=True)
        pv = jnp.dot(p.astype(BF16), kc_f32.astype(BF16), preferred_element_type=F32)
        acc_ref[...] = _lane_bcast(alpha, KV_LORA) * acc_ref[...] + pv
        m_ref[...] = m_new

    def pipelined(kv, kr, lo, hi, last, mask_fn):
        def body(j, c):
            k0 = pl.multiple_of(j * TK, TK)
            k1 = pl.multiple_of(jnp.minimum(j + 1, last) * TK, TK)
            s_cur = s_ref[j % 2]
            s_ref[(j + 1) % 2] = scores(kv[pl.ds(k1, TK), :], kr[pl.ds(k1, TK), :])
            update(s_cur, kv[pl.ds(k0, TK), :], None if mask_fn is None else mask_fn(k0))
            return c
        lax.fori_loop(lo, hi, body, 0)

    def pipelined_pairs(kv, kr, n_pairs, last):
        def body(i, c):
            k0 = pl.multiple_of(2 * i * TK, TK)
            k1 = pl.multiple_of((2 * i + 1) * TK, TK)
            k2 = pl.multiple_of(jnp.minimum(2 * i + 2, last) * TK, TK)
            s_ref[1] = scores(kv[pl.ds(k1, TK), :], kr[pl.ds(k1, TK), :])
            update(s_ref[0], kv[pl.ds(k0, TK), :], None)
            s_ref[0] = scores(kv[pl.ds(k2, TK), :], kr[pl.ds(k2, TK), :])
            update(s_ref[1], kv[pl.ds(k1, TK), :], None)
            return c
        lax.fori_loop(0, n_pairs, body, 0)

    if n_past:
        n_pb = n_past // TK
        s_ref[0] = scores(pkv_ref[pl.ds(0, TK), :], pkr_ref[pl.ds(0, TK), :])
        pipelined_pairs(pkv_ref, pkr_ref, n_pb // 2, n_pb - 1)
        if n_pb % 2:
            pipelined(pkv_ref, pkr_ref, n_pb - 1, n_pb, n_pb - 1, None)

    if causal:
        n_blocks = ((qi + 1) * tq + TK - 1) // TK
        n_full = jnp.minimum((qi * tq // CHUNK + 1) * CHUNK // TK, n_blocks)

        assert tq & (tq - 1) == 0 and CHUNK & (CHUNK - 1) == 0
        r = lax.broadcasted_iota(jnp.int32, (rows, LANES), 0)
        q_pos = qi * tq + (r & (tq - 1))
        klim_ref[...] = (q_pos & ~(CHUNK - 1)) + CHUNK

        def mask_fn(k0):
            cidx = lax.broadcasted_iota(jnp.int32, (rows, TK), 1)
            return cidx < _lane_bcast(klim_ref[...] - k0, TK)

        s_ref[0] = scores(kv_ref[pl.ds(0, TK), :], kr_ref[pl.ds(0, TK), :])
        pipelined_pairs(kv_ref, kr_ref, n_full // 2, n_blocks - 1)
        pipelined(kv_ref, kr_ref, n_full // 2 * 2, n_full, n_blocks - 1, None)
        pipelined(kv_ref, kr_ref, n_full, n_blocks, n_blocks - 1, mask_fn)
    else:
        update(scores(kv_ref[...], kr_ref[...]), kv_ref[...], None)

    o = acc_ref[...] / _lane_bcast(l_ref[...], KV_LORA)
    parts = []
    for h in range(N_HEADS):
        oh = o[h * tq:(h + 1) * tq, :].astype(BF16)
        parts.append(jnp.dot(oh, wuv_ref[h], preferred_element_type=F32))
    attn = jnp.concatenate(parts, axis=-1)
    out_ref[...] = _rms(attn, gao_ref[...]).astype(BF16)


def _attention(cqn, w_q, w_ukt, w_uv, cosq, sinq, gao, ckv, krope, *, n_batch, seq, row0,
               past_kv=None, past_kr=None):
    causal = past_kv is None
    tq = TQ if causal else seq
    nq = seq // tq
    n_past = 0 if causal else past_kv.shape[1]
    if not causal:
        assert n_past % CHUNK == 0 and seq <= CHUNK and n_past % TK == 0
    blk0 = row0 // tq
    qrow = lambda b, q: (blk0 + b * nq + q, 0)
    in_specs = [
        pl.BlockSpec((tq, Q_LORA), qrow),
        _const_spec(w_q.shape),
        _const_spec(w_ukt.shape),
        _const_spec(w_uv.shape),
        pl.BlockSpec((tq, N_HEADS * ROPE_DIM), lambda b, q: (q, 0)),
        pl.BlockSpec((tq, N_HEADS * ROPE_DIM), lambda b, q: (q, 0)),
        _const_spec((1, ATTN_W)),
    ]
    args = [cqn, w_q, w_ukt, w_uv, cosq, sinq, gao]
    if n_past:
        in_specs += [pl.BlockSpec((None, n_past, KV_LORA), lambda b, q: (b, 0, 0)),
                     pl.BlockSpec((None, n_past, ROPE_DIM), lambda b, q: (b, 0, 0))]
        args += [past_kv, past_kr]
    in_specs += [pl.BlockSpec((seq, KV_LORA), lambda b, q: (b, 0)),
                 pl.BlockSpec((seq, ROPE_DIM), lambda b, q: (b, 0))]
    args += [ckv, krope]
    rows = N_HEADS * tq
    kern = functools.partial(_attn_kernel, tq=tq, n_past=n_past, causal=causal)
    return pl.pallas_call(
        kern,
        grid=(n_batch, nq),
        in_specs=in_specs,
        out_specs=pl.BlockSpec((tq, ATTN_W), lambda b, q: (b * nq + q, 0)),
        out_shape=jax.ShapeDtypeStruct((n_batch * seq, ATTN_W), BF16),
        scratch_shapes=[
            pltpu.VMEM((rows, KV_LORA), BF16),
            pltpu.VMEM((rows, ROPE_DIM), BF16),
            pltpu.VMEM((rows, LANES), F32),
            pltpu.VMEM((rows, LANES), F32),
            pltpu.VMEM((rows, KV_LORA), F32),
            pltpu.VMEM((2, rows, TK), F32),
            pltpu.VMEM((rows, LANES), jnp.int32),
        ],
        compiler_params=pltpu.CompilerParams(dimension_semantics=("arbitrary", "arbitrary"),
                                             vmem_limit_bytes=VMEM_LIMIT),
        name="attn_prompt" if causal else "attn_sample",
    )(*args)


def _out_proj_kernel(attnp_ref, attns_ref, convn_ref, xp_ref, xs_ref, wo_ref, gffn_ref, wr_ref,
                     br_ref, h_ref, xpk_ref, mi_ref, mf_ref, cnt_ref, carry_ref, *, n_prompt_tiles):
    i = pl.program_id(0)

    @pl.when(i == 0)
    def _():
        carry_ref[...] = jnp.zeros(carry_ref.shape, F32)

    def tile(x_ref, attn_ref):
        y = jnp.dot(attn_ref[...], wo_ref[:ATTN_W, :], preferred_element_type=F32)
        y = y + jnp.dot(convn_ref[...], wo_ref[ATTN_W:, :], preferred_element_type=F32)
        h = x_ref[...] + y
        h_ref[...] = h
        xn = _rms(h, gffn_ref[...])

        half = D_MODEL // 2
        xh = xn.astype(BF16)
        xh32 = xh.astype(F32)
        lo = lax.bitcast_convert_type(xh32[:, :half], jnp.uint32)
        hi = lax.bitcast_convert_type(xh32[:, half:], jnp.uint32)
        xpk_ref[...] = (lo >> 16) | (hi & jnp.uint32(0xFFFF0000))

        xl = (xn - xh32).astype(BF16)
        hh_hl = jnp.dot(xh, wr_ref[...], preferred_element_type=F32)
        lh = jnp.dot(xl, wr_ref[:, :LANES], preferred_element_type=F32)
        logits = hh_hl[:, :LANES] + (lh + hh_hl[:, LANES:]) + br_ref[...]
        lane = lax.broadcasted_iota(jnp.int32, (TM, LANES), 1).astype(F32)
        ninf = -jnp.inf
        far = float(LANES)

        def first_argmax(v):
            vmax = jnp.max(v, axis=-1, keepdims=True)
            return vmax, jnp.min(jnp.where(v == vmax, lane, far), axis=-1, keepdims=True)

        gl = jnp.where(lane < N_GROUPS, logits, ninf)
        gmax, gidx = first_argmax(gl)
        g_p = 1.0 / jnp.sum(jnp.exp(gl - gmax), axis=-1, keepdims=True)
        e_lo = N_GROUPS + EXPERTS_PER_GROUP * gidx
        el = jnp.where((lane >= e_lo) & (lane < e_lo + EXPERTS_PER_GROUP), logits, ninf)
        e1max, i1 = first_argmax(el)
        z = jnp.sum(jnp.exp(el - e1max), axis=-1, keepdims=True)
        el2 = jnp.where(lane == i1, ninf, el)
        e2max, i2 = first_argmax(el2)
        p1 = 1.0 / z
        p2 = jnp.exp(e2max - e1max) / z
        den = p1 + p2
        g0 = g_p * p1 / den
        g1 = g_p * p2 / den
        e0 = i1 - N_GROUPS
        e1 = i2 - N_GROUPS

        oh0 = lane == e0
        oh1 = lane == e1
        oh = jnp.where(oh0 | oh1, 1.0, 0.0)
        r = lax.broadcasted_iota(jnp.int32, (TM, TM), 0)
        c = lax.broadcasted_iota(jnp.int32, (TM, TM), 1)
        ltri = jnp.where(r > c, 1.0, 0.0).astype(BF16)
        before = jnp.dot(ltri, oh.astype(BF16), preferred_element_type=F32) + carry_ref[...]
        rank0 = jnp.sum(jnp.where(oh0, before, 0.0), axis=-1, keepdims=True)
        rank1 = jnp.sum(jnp.where(oh1, before, 0.0), axis=-1, keepdims=True)
        total = carry_ref[...] + jnp.sum(oh, axis=0, keepdims=True)
        carry_ref[...] = total
        cnt_ref[...] = jnp.broadcast_to(total, cnt_ref.shape)

        mi = jnp.where(lane == 0, e0, jnp.where(lane == 1, e1, jnp.where(lane == 2, rank0, rank1)))
        mi_ref[...] = jnp.transpose(mi)[:SUBLANES, :].astype(jnp.int32)
        mf_ref[...] = jnp.where(lane == 0, g0, g1)

    @pl.when(i < n_prompt_tiles)
    def _():
        tile(xp_ref, attnp_ref)

    @pl.when(i >= n_prompt_tiles)
    def _():
        tile(xs_ref, attns_ref)


def _out_proj(attn_p, attn_s, conv_n, xp, xs, w_ob, gffn, w_r2, b_r):
    m = conv_n.shape[0]
    npt = xp.shape[0] // TM
    last_p = npt - 1
    row = lambda i: (i, 0)
    return pl.pallas_call(
        functools.partial(_out_proj_kernel, n_prompt_tiles=npt),
        grid=(m // TM,),
        in_specs=[
            pl.BlockSpec((TM, ATTN_W), lambda i: (jnp.minimum(i, last_p), 0)),
            pl.BlockSpec((TM, ATTN_W), lambda i: (jnp.maximum(i - npt, 0), 0)),
            pl.BlockSpec((TM, CONV_CH), row),
            pl.BlockSpec((TM, D_MODEL), lambda i: (jnp.minimum(i, last_p), 0)),
            pl.BlockSpec((TM, D_MODEL), lambda i: (jnp.maximum(i - npt, 0), 0)),
            _const_spec(w_ob.shape),
            _const_spec((1, D_MODEL)),
            _const_spec(w_r2.shape),
            _const_spec((1, LANES)),
        ],
        out_specs=[
            pl.BlockSpec((TM, D_MODEL), row),
            pl.BlockSpec((TM, D_MODEL // 2), row),
            pl.BlockSpec((SUBLANES, TM), lambda i: (0, i)),
            pl.BlockSpec((TM, LANES), row),
            pl.BlockSpec((SUBLANES, LANES), lambda i: (0, 0)),
        ],
        out_shape=[
            jax.ShapeDtypeStruct((m, D_MODEL), F32),
            jax.ShapeDtypeStruct((m, D_MODEL // 2), jnp.uint32),
            jax.ShapeDtypeStruct((SUBLANES, m), jnp.int32),
            jax.ShapeDtypeStruct((m, LANES), F32),
            jax.ShapeDtypeStruct((SUBLANES, LANES), F32),
        ],
        scratch_shapes=[pltpu.VMEM((1, LANES), F32)],
        compiler_params=pltpu.CompilerParams(dimension_semantics=("arbitrary",),
                                             vmem_limit_bytes=VMEM_LIMIT),
        name="out_proj",
    )(attn_p, attn_s, conv_n, xp, xs, w_ob, gffn, w_r2, b_r)


def _dispatch_kernel(d0_ref, d1_ref, zlo_ref, zn_ref, nu_ref, xpk_ref, xs_hbm, zeros_ref, sems, *, n_blocks):
    i = pl.program_id(0)
    sem = sems.at[0]
    zsem = sems.at[1]

    def row_copy(src_ref, src_row, dst_row):
        return pltpu.make_async_copy(src_ref.at[pl.ds(src_row, 1)], xs_hbm.at[pl.ds(dst_row, 1)], sem)

    def zero_fill(act):
        def per_expert(e, c):
            lo = zlo_ref[e]
            n = zn_ref[e]
            head = (-lo) & (SUBLANES - 1)
            for r in range(SUBLANES - 1):
                @pl.when(r < head)
                def _(r=r):
                    act(pltpu.make_async_copy(zeros_ref.at[pl.ds(0, 1)], xs_hbm.at[pl.ds(lo + r, 1)], zsem))
            off = lo + head
            rest = n - head
            size = MOE_BLOCK // 2
            while size >= SUBLANES:
                @pl.when((rest & size) != 0)
                def _(off=off, size=size):
                    dst = xs_hbm.at[pl.ds(pl.multiple_of(off, SUBLANES), size)]
                    act(pltpu.make_async_copy(zeros_ref.at[pl.ds(0, size)], dst, zsem))
                off = off + (rest & size)
                size //= 2
            return c

        def per_block(b, c):
            dst = xs_hbm.at[pl.ds(pl.multiple_of(b * MOE_BLOCK, MOE_BLOCK), MOE_BLOCK)]
            act(pltpu.make_async_copy(zeros_ref, dst, zsem))
            return c

        lax.fori_loop(0, N_EXPERTS, per_expert, 0)
        lax.fori_loop(nu_ref[0], n_blocks, per_block, 0)

    @pl.when(i == 0)
    def _():
        zeros_ref[...] = jnp.zeros(zeros_ref.shape, zeros_ref.dtype)
        zero_fill(lambda cp: cp.start())
        zero_fill(lambda cp: cp.wait())

    base = i * TM

    def start(r, c):
        row_copy(xpk_ref, r, d0_ref[base + r]).start()
        row_copy(xpk_ref, r, d1_ref[base + r]).start()
        return c

    lax.fori_loop(0, TM, start, 0, unroll=DMA_ISSUE_UNROLL)
    for _ in range(2):
        pltpu.make_async_copy(xpk_ref, xs_hbm.at[pl.ds(0, TM)], sem).wait()


def _dispatch(dest0, dest1, pad_lo, n_pad, n_used, xpk, n_blocks):
    m = xpk.shape[0]
    grid_spec = pltpu.PrefetchScalarGridSpec(
        num_scalar_prefetch=5,
        grid=(m // TM,),
        in_specs=[pl.BlockSpec((TM, D_MODEL // 2), lambda i, *_: (i, 0))],
        out_specs=pl.BlockSpec(memory_space=pl.ANY),
        scratch_shapes=[pltpu.VMEM((MOE_BLOCK, D_MODEL // 2), jnp.uint32),
                        pltpu.SemaphoreType.DMA((2,))],
    )
    return pl.pallas_call(
        functools.partial(_dispatch_kernel, n_blocks=n_blocks),
        grid_spec=grid_spec,
        out_shape=jax.ShapeDtypeStruct((n_blocks * MOE_BLOCK, D_MODEL // 2), jnp.uint32),
        compiler_params=pltpu.CompilerParams(dimension_semantics=("arbitrary",)),
        name="dispatch",
    )(dest0, dest1, pad_lo, n_pad, n_used, xpk)


def _experts_kernel(be_ref, nu_ref, nxt_ref, x_ref, wg_hbm, wu_hbm, wd_hbm, y_ref,
                    sg_ref, su_ref, sd_ref, wgb_ref, wub_ref, wdb_ref, sems):
    b = pl.program_id(0)
    active = b < nu_ref[0]
    new_expert = jnp.logical_or(b == 0, be_ref[b] != be_ref[jnp.maximum(b - 1, 0)])

    def weight_copies(e):
        return (pltpu.make_async_copy(wg_hbm.at[e], sg_ref, sems.at[0]),
                pltpu.make_async_copy(wu_hbm.at[e], su_ref, sems.at[1]),
                pltpu.make_async_copy(wd_hbm.at[e], sd_ref, sems.at[2]))

    @pl.when(b == 0)
    def _():
        for cp in weight_copies(be_ref[0]):
            cp.start()

    @pl.when(jnp.logical_and(active, new_expert))
    def _():
        for cp in weight_copies(be_ref[b]):
            cp.wait()
        wgb_ref[...] = sg_ref[...].astype(BF16)
        wub_ref[...] = su_ref[...].astype(BF16)
        wdb_ref[...] = sd_ref[...].astype(BF16)

        @pl.when(nxt_ref[b] >= 0)
        def _():
            for cp in weight_copies(nxt_ref[b]):
                cp.start()

    @pl.when(active)
    def _():
        half = D_MODEL // 2
        xw = x_ref[...]
        xa = lax.bitcast_convert_type(xw << 16, F32).astype(BF16)
        xb = lax.bitcast_convert_type(xw & jnp.uint32(0xFFFF0000), F32).astype(BF16)
        g = jnp.dot(xa, wgb_ref[:half, :], preferred_element_type=F32)
        g = g + jnp.dot(xb, wgb_ref[half:, :], preferred_element_type=F32)
        u = jnp.dot(xa, wub_ref[:half, :], preferred_element_type=F32)
        u = u + jnp.dot(xb, wub_ref[half:, :], preferred_element_type=F32)
        hmid = (g * jax.nn.sigmoid(g)) * u
        y_ref[...] = jnp.dot(hmid.astype(BF16), wdb_ref[...], preferred_element_type=F32)

    @pl.when(b >= nu_ref[0])
    def _():
        y_ref[...] = jnp.zeros(y_ref.shape, y_ref.dtype)


def _experts(block_e, n_used, next_e, x_sorted, w_gate, w_up, w_down):
    p = x_sorted.shape[0]
    nb = p // MOE_BLOCK

    def xrow(b, be, nu, nxt):
        return (jnp.maximum(jnp.minimum(b, nu[0] - 1), 0), 0)

    grid_spec = pltpu.PrefetchScalarGridSpec(
        num_scalar_prefetch=3,
        grid=(nb,),
        in_specs=[
            pl.BlockSpec((MOE_BLOCK, D_MODEL // 2), xrow),
            pl.BlockSpec(memory_space=pl.ANY),
            pl.BlockSpec(memory_space=pl.ANY),
            pl.BlockSpec(memory_space=pl.ANY),
        ],
        out_specs=pl.BlockSpec((MOE_BLOCK, D_MODEL), lambda b, be, nu, nxt: (b, 0)),
        scratch_shapes=[pltpu.VMEM((D_MODEL, D_FF), F32), pltpu.VMEM((D_MODEL, D_FF), F32),
                        pltpu.VMEM((D_FF, D_MODEL), F32),
                        pltpu.VMEM((D_MODEL, D_FF), BF16), pltpu.VMEM((D_MODEL, D_FF), BF16),
                        pltpu.VMEM((D_FF, D_MODEL), BF16),
                        pltpu.SemaphoreType.DMA((3,))],
    )
    return pl.pallas_call(
        _experts_kernel,
        grid_spec=grid_spec,
        out_shape=jax.ShapeDtypeStruct((p, D_MODEL), F32),
        compiler_params=pltpu.CompilerParams(dimension_semantics=("arbitrary",),
                                             vmem_limit_bytes=VMEM_LIMIT),
        name="experts",
    )(block_e, n_used, next_e, x_sorted, w_gate, w_up, w_down)


def _combine_kernel(d0_ref, d1_ref, h_ref, mf_ref, gfin_ref, y_hbm, outp_ref, outs_ref, y0_ref, y1_ref, sems,
                    *, n_tiles, n_prompt_tiles):
    i = pl.program_id(0)

    def gather(tile, slot, act):
        base = tile * TM

        def body(r, c):
            act(pltpu.make_async_copy(y_hbm.at[pl.ds(d0_ref[base + r], 1)], y0_ref.at[slot, pl.ds(r, 1)],
                                      sems.at[slot]))
            act(pltpu.make_async_copy(y_hbm.at[pl.ds(d1_ref[base + r], 1)], y1_ref.at[slot, pl.ds(r, 1)],
                                      sems.at[slot]))
            return c
        lax.fori_loop(0, TM, body, 0, unroll=DMA_ISSUE_UNROLL)

    @pl.when(i == 0)
    def _():
        gather(0, 0, lambda cp: cp.start())

    @pl.when(i + 1 < n_tiles)
    def _():
        gather(i + 1, (i + 1) % 2, lambda cp: cp.start())

    slot = i % 2
    pltpu.make_async_copy(y_hbm.at[pl.ds(0, TM)], y0_ref.at[slot], sems.at[slot]).wait()
    pltpu.make_async_copy(y_hbm.at[pl.ds(0, TM)], y1_ref.at[slot], sems.at[slot]).wait()
    mf = mf_ref[...]
    ffn = mf[:, 0:1] * y0_ref[slot] + mf[:, 1:2] * y1_ref[slot]
    out = _rms(h_ref[...] + ffn, gfin_ref[...])

    @pl.when(i < n_prompt_tiles)
    def _():
        outp_ref[...] = out

    @pl.when(i >= n_prompt_tiles)
    def _():
        outs_ref[...] = out


def _combine(dest0, dest1, h, mf, gfin, y_sorted, *, n_prompt_rows):
    m = h.shape[0]
    npt = n_prompt_rows // TM
    grid_spec = pltpu.PrefetchScalarGridSpec(
        num_scalar_prefetch=2,
        grid=(m // TM,),
        in_specs=[
            pl.BlockSpec((TM, D_MODEL), lambda i, *_: (i, 0)),
            pl.BlockSpec((TM, LANES), lambda i, *_: (i, 0)),
            pl.BlockSpec((1, D_MODEL), lambda i, *_: (0, 0)),
            pl.BlockSpec(memory_space=pl.ANY),
        ],
        out_specs=[pl.BlockSpec((TM, D_MODEL), lambda i, *_: (jnp.minimum(i, npt - 1), 0)),
                   pl.BlockSpec((TM, D_MODEL), lambda i, *_: (jnp.maximum(i - npt, 0), 0))],
        scratch_shapes=[pltpu.VMEM((2, TM, D_MODEL), F32), pltpu.VMEM((2, TM, D_MODEL), F32),
                        pltpu.SemaphoreType.DMA((2,))],
    )
    return pl.pallas_call(
        functools.partial(_combine_kernel, n_tiles=m // TM, n_prompt_tiles=npt),
        grid_spec=grid_spec,
        out_shape=[jax.ShapeDtypeStruct((n_prompt_rows, D_MODEL), F32),
                   jax.ShapeDtypeStruct((m - n_prompt_rows, D_MODEL), F32)],
        compiler_params=pltpu.CompilerParams(dimension_semantics=("arbitrary",),
                                             vmem_limit_bytes=VMEM_LIMIT),
        name="combine",
    )(dest0, dest1, h, mf, gfin, y_sorted)


def _rope_tables(pos):
    inv = ROPE_THETA ** (-jnp.arange(0, ROPE_DIM, 2, dtype=F32) / ROPE_DIM)
    ang = pos.astype(F32)[:, None] * inv[None, :]
    cos, sin = jnp.cos(ang), jnp.sin(ang)
    return jnp.concatenate([cos, cos], axis=-1), jnp.concatenate([-sin, sin], axis=-1)


def _swap_halves(w):
    return jnp.concatenate([w[..., ROPE_DIM // 2:], w[..., :ROPE_DIM // 2]], axis=-1)


def kernel(x_prompt, x_sample, cache_kv_latent, cache_k_rope, state_conv, norm_mix, w_in, norm_q, w_uq,
           norm_kv, w_uk, w_uv, conv_w, norm_attn_out, norm_conv_out, w_o, norm_ffn, w_router_group,
           b_router_group, w_router_expert, b_router_expert, w_gate, w_up, w_down, norm_final):
    assert w_in.shape[0] == 1, "single-layer trunk"
    bp, seq_p, _ = x_prompt.shape
    bs, seq_s, _ = x_sample.shape
    past_len = cache_kv_latent.shape[2]
    np_rows, ns_rows = bp * seq_p, bs * seq_s
    m = np_rows + ns_rows
    assert seq_p % TM == 0 and TM % seq_s == 0 and ns_rows % TM == 0 and seq_s == CHUNK

    xp = x_prompt.reshape(np_rows, D_MODEL)
    xs = x_sample.reshape(ns_rows, D_MODEL)
    row_vec = lambda v: v.reshape(1, -1)

    w_in0 = w_in[0]
    mla_w = Q_LORA + KV_LORA + ROPE_DIM
    w_kr = w_in0[:, mla_w - ROPE_DIM:mla_w]
    w_all = jnp.concatenate([w_in0[:, :mla_w - ROPE_DIM], w_in0[:, mla_w:], w_kr, _swap_halves(w_kr)],
                            axis=1).astype(BF16)
    wq4 = w_uq[0].reshape(Q_LORA, N_HEADS, QK_NOPE + ROPE_DIM)
    wq_rope = wq4[:, :, QK_NOPE:]
    w_q = jnp.concatenate([wq4[:, :, :QK_NOPE].reshape(Q_LORA, -1), wq_rope.reshape(Q_LORA, -1),
                           _swap_halves(wq_rope).reshape(Q_LORA, -1)], axis=1).astype(BF16)
    w_ukt = jnp.transpose(w_uk[0], (1, 2, 0)).astype(BF16)
    w_uvh = jnp.transpose(w_uv[0], (1, 0, 2)).astype(BF16)
    w_ob = w_o[0].astype(BF16)
    n_router = N_GROUPS + N_EXPERTS
    w_r = jnp.concatenate([w_router_group[0], w_router_expert[0].reshape(D_MODEL, N_EXPERTS)], axis=1)
    w_r = jnp.pad(w_r, ((0, 0), (0, LANES - n_router)))
    w_rh = w_r.astype(BF16)
    w_rl = (w_r - w_rh.astype(F32)).astype(BF16)
    w_r2 = jnp.concatenate([w_rh, w_rl], axis=1)
    b_r =jnp.pad(jnp.concatenate([b_router_group[0], b_router_expert[0].reshape(N_EXPERTS)]),
                  (0, LANES - n_router)).reshape(1, LANES)

    pos_p = jnp.arange(seq_p, dtype=jnp.int32)
    pos_s = past_len + jnp.arange(seq_s, dtype=jnp.int32)
    cos_p, sin_p = _rope_tables(pos_p)
    cos_s, sin_s = _rope_tables(pos_s)
    cosk = jnp.concatenate([cos_p, jnp.tile(cos_s, (TM // seq_s, 1))], axis=0)
    sink = jnp.concatenate([sin_p, jnp.tile(sin_s, (TM // seq_s, 1))], axis=0)
    state = jnp.concatenate([jnp.zeros((bp, CONV_W - 1, CONV_CH), F32), state_conv[0]], axis=0)

    cqn, ckv_p, kr_p, ckv_s, kr_s, conv_n, utail = _in_proj(
        xp, xs, row_vec(norm_mix[0]), w_all, row_vec(norm_q[0]), row_vec(norm_kv[0]),
        row_vec(norm_conv_out[0]), conv_w[0], cosk, sink, state, seq_p=seq_p, seq_s=seq_s)

    gao = row_vec(norm_attn_out[0])
    attn_p = _attention(cqn, w_q, w_ukt, w_uvh, jnp.tile(cos_p, (1, N_HEADS)), jnp.tile(sin_p, (1, N_HEADS)),
                        gao, ckv_p, kr_p, n_batch=bp, seq=seq_p, row0=0)
    attn_s = _attention(cqn, w_q, w_ukt, w_uvh, jnp.tile(cos_s, (1, N_HEADS)), jnp.tile(sin_s, (1, N_HEADS)),
                        gao, ckv_s, kr_s, n_batch=bs, seq=seq_s, row0=np_rows,
                        past_kv=cache_kv_latent[0], past_kr=cache_k_rope[0])

    h, xpk, mi, mf, cnt = _out_proj(attn_p, attn_s, conv_n, xp, xs, w_ob, row_vec(norm_ffn[0]),
                                    w_r2, b_r)

    counts = cnt[0, :N_EXPERTS].astype(jnp.int32)
    padded = (counts + MOE_BLOCK - 1) // MOE_BLOCK * MOE_BLOCK
    pad_end = jnp.cumsum(padded)
    pad_start = pad_end - padded
    n_blocks = -(-(m * 2) // MOE_BLOCK) + N_EXPERTS
    block_row0 = jnp.arange(n_blocks, dtype=jnp.int32) * MOE_BLOCK
    block_e = jnp.minimum(jnp.sum((pad_end[None, :] <= block_row0[:, None]).astype(jnp.int32), axis=1),
                          N_EXPERTS - 1)
    n_used = (pad_end[-1:] // MOE_BLOCK).astype(jnp.int32)
    expert_ids = jnp.arange(N_EXPERTS, dtype=jnp.int32)[:, None]

    def seg_start(e):
        return jnp.sum(jnp.where(expert_ids == e[None, :], pad_start[:, None], 0), axis=0)

    dest0 = seg_start(mi[0]) + mi[2]
    dest1 = seg_start(mi[1]) + mi[3]

    x_sorted = _dispatch(dest0, dest1, pad_start + counts, padded - counts, n_used, xpk, n_blocks)
    later = (expert_ids.T > block_e[:, None]) & (padded > 0)[None, :]
    next_e = jnp.min(jnp.where(later, expert_ids.T, N_EXPERTS), axis=1)
    next_e = jnp.where(next_e == N_EXPERTS, -1, next_e).astype(jnp.int32)
    y_sorted = _experts(block_e, n_used, next_e, x_sorted, w_gate[0], w_up[0], w_down[0])
    gfin = row_vec(norm_final)
    y_p, y_s = _combine(dest0, dest1, h, mf, gfin, y_sorted, n_prompt_rows=np_rows)

    ut = utail.reshape(m // CHUNK, SUBLANES, CONV_CH)
    tails = ut[:, SUBLANES - (CONV_W - 1):, :]
    p_last = (jnp.arange(bp) + 1) * (seq_p // CHUNK) - 1
    s_last = np_rows // CHUNK + (jnp.arange(bs) + 1) * (seq_s // CHUNK) - 1
    return (y_p.reshape(bp, seq_p, D_MODEL),
            y_s.reshape(bs, seq_s, D_MODEL),
            ckv_p.reshape(1, bp, seq_p, KV_LORA),
            kr_p.reshape(1, bp, seq_p, ROPE_DIM),
            tails[p_last][None],
            ckv_s.reshape(1, bs, seq_s, KV_LORA),
            kr_s.reshape(1, bs, seq_s, ROPE_DIM),
            tails[s_last][None])
```

```python
import functools

import jax
import jax.numpy as jnp
import numpy as np
from jax import lax
from jax.experimental import pallas as pl
from jax.experimental.pallas import tpu as pltpu

F32 = jnp.float32
BF16 = jnp.bfloat16

D_MODEL = 2048
N_HEADS = 8
QK_NOPE = 128
ROPE_DIM = 64
V_DIM = 128
Q_LORA = 512
KV_LORA = 512
ATTN_W = N_HEADS * V_DIM
CONV_CH = D_MODEL - ATTN_W
CONV_W = 3
CHUNK = 64
N_GROUPS = 4
EXPERTS_PER_GROUP = 8
N_EXPERTS = N_GROUPS * EXPERTS_PER_GROUP
D_FF = 512
ROPE_THETA = 10000.0
EPS = 1e-6
ATTN_SCALE = (QK_NOPE + ROPE_DIM) ** -0.5
EXP2_SCALE = ATTN_SCALE * 1.4426950408889634

LANES = 128
SUBLANES = 8
TM = 256
MOE_BLOCK = 256
TQ = 256
TK = 256
DMA_ISSUE_UNROLL = 8
NEG_BIG = -1e30
VMEM_LIMIT = 56 * 1024 * 1024


def _rms(v, g):
    return v * lax.rsqrt(jnp.mean(v * v, axis=-1, keepdims=True) + EPS) * g


def _lane_bcast(v, width):
    if width % LANES == 0:
        return jnp.concatenate([v] * (width // LANES), axis=1)
    assert width < LANES
    return v[:, :width]


def _const_spec(shape):
    nd = len(shape)
    return pl.BlockSpec(shape, lambda *_: (0,) * nd, pipeline_mode=pl.Buffered(1))


def _w_in_layout_kernel(w_ref, out_ref):
    lat_w = Q_LORA + KV_LORA
    w = w_ref[...]
    kr = w[:, lat_w:lat_w + ROPE_DIM]
    out_ref[:, :lat_w] = w[:, :lat_w].astype(BF16)
    out_ref[:, lat_w:lat_w + 3 * CONV_CH] = w[:, lat_w + ROPE_DIM:].astype(BF16)
    out_ref[:, lat_w + 3 * CONV_CH:] = jnp.concatenate(
        [kr, kr[:, ROPE_DIM // 2:], kr[:, :ROPE_DIM // 2]], axis=1).astype(BF16)


def _w_in_layout(w_in0):
    k, n = w_in0.shape
    n_out = n + ROPE_DIM
    return pl.pallas_call(
        _w_in_layout_kernel,
        grid=(k // TM,),
        in_specs=[pl.BlockSpec((TM, n), lambda i: (i, 0))],
        out_specs=pl.BlockSpec((TM, n_out), lambda i: (i, 0)),
        out_shape=jax.ShapeDtypeStruct((k, n_out), BF16),
        compiler_params=pltpu.CompilerParams(dimension_semantics=("arbitrary",)),
        name="w_in_layout",
    )(w_in0)


def _in_proj_kernel(xp_ref, xs_ref, gmix_ref, w_ref, gq_ref, gkv_ref, gco_ref, convw_ref,
                    cos_ref, sin_ref, state_ref,
                    cqn_ref, ckvp_ref, krp_ref, ckvs_ref, krs_ref, convn_ref, utail_ref, ext_ref,
                    *, n_prompt_tiles, tiles_per_seq, n_prompt_seq, sample_seq_len):
    i = pl.program_id(0)

    def conv_block(u_sub, gate_sub, row0, length):
        ext_ref[SUBLANES:SUBLANES + length, :] = u_sub
        um1 = ext_ref[SUBLANES - 1:SUBLANES - 1 + length, :]
        um2 = ext_ref[SUBLANES - 2:SUBLANES - 2 + length, :]
        cw = convw_ref[...]
        conv = cw[0:1] * um2 + cw[1:2] * um1 + cw[2:3] * u_sub
        convn_ref[row0:row0 + length, :] = _rms(gate_sub * conv, gco_ref[...]).astype(BF16)

    def tile(x_ref, is_prompt):
        ckv_ref, kr_ref = (ckvp_ref, krp_ref) if is_prompt else (ckvs_ref, krs_ref)
        x = x_ref[...]
        xn = _rms(x, gmix_ref[...]).astype(BF16)
        lat_w = Q_LORA + KV_LORA
        conv_end = lat_w + 3 * CONV_CH
        za = jnp.dot(xn, w_ref[:, :lat_w], preferred_element_type=F32)
        cqn_ref[...] = _rms(za[:, :Q_LORA], gq_ref[...]).astype(BF16)
        ckv_ref[...] = _rms(za[:, Q_LORA:], gkv_ref[...])
        zk = jnp.dot(xn, w_ref[:, conv_end:], preferred_element_type=F32)
        kr_ref[...] = zk[:, :ROPE_DIM] * cos_ref[...] + zk[:, ROPE_DIM:] * sin_ref[...]

        zc = jnp.dot(xn, w_ref[:, lat_w:conv_end], preferred_element_type=F32)
        gate_b = zc[:, :CONV_CH]
        u = zc[:, CONV_CH:2 * CONV_CH] * zc[:, 2 * CONV_CH:]
        for j in range(TM // CHUNK):
            utail_ref[j] = u[CHUNK * (j + 1) - SUBLANES:CHUNK * (j + 1), :]

        if is_prompt:
            first = (i % tiles_per_seq) == 0

            @pl.when(first)
            def _():
                ext_ref[SUBLANES - 2:SUBLANES, :] = state_ref[i // tiles_per_seq]

            @pl.when(jnp.logical_not(first))
            def _():
                ext_ref[SUBLANES - 2:SUBLANES, :] = ext_ref[TM + SUBLANES - 2:TM + SUBLANES, :]

            conv_block(u, gate_b, 0, TM)
        else:
            n_sub = TM // sample_seq_len
            seq0 = n_prompt_seq + (i - n_prompt_tiles) * n_sub
            for k in range(n_sub):
                ext_ref[SUBLANES - 2:SUBLANES, :] = state_ref[seq0 + k]
                lo = k * sample_seq_len
                conv_block(u[lo:lo + sample_seq_len], gate_b[lo:lo + sample_seq_len], lo, sample_seq_len)

    @pl.when(i < n_prompt_tiles)
    def _():
        tile(xp_ref, True)

    @pl.when(i >= n_prompt_tiles)
    def _():
        tile(xs_ref, False)


def _in_proj(xp, xs, gmix, w_all, gq, gkv, gco, convw, cosk, sink, state, *, seq_p, seq_s):
    np_rows, ns_rows = xp.shape[0], xs.shape[0]
    m = np_rows + ns_rows
    npt, nst = np_rows // TM, ns_rows // TM
    tps = seq_p // TM
    n_prompt_seq = np_rows // seq_p
    last_p = npt - 1

    def tab_idx(i):
        return (jnp.where(i < npt, i % tps, tps), 0)

    row = lambda i: (i, 0)
    prow = lambda i: (jnp.minimum(i, last_p), 0)
    srow = lambda i: (jnp.maximum(i - npt, 0), 0)
    kern = functools.partial(_in_proj_kernel, n_prompt_tiles=npt, tiles_per_seq=tps,
                             n_prompt_seq=n_prompt_seq, sample_seq_len=seq_s)
    return pl.pallas_call(
        kern,
        grid=(npt + nst,),
        in_specs=[
            pl.BlockSpec((TM, D_MODEL), prow),
            pl.BlockSpec((TM, D_MODEL), srow),
            _const_spec((1, D_MODEL)),
            _const_spec(w_all.shape),
            _const_spec((1, Q_LORA)),
            _const_spec((1, KV_LORA)),
            _const_spec((1, CONV_CH)),
            _const_spec((CONV_W, CONV_CH)),
            pl.BlockSpec((TM, ROPE_DIM), tab_idx),
            pl.BlockSpec((TM, ROPE_DIM), tab_idx),
            _const_spec(state.shape),
        ],
        out_specs=[
            pl.BlockSpec((TM, Q_LORA), row),
            pl.BlockSpec((TM, KV_LORA), prow),
            pl.BlockSpec((TM, ROPE_DIM), prow),
            pl.BlockSpec((TM, KV_LORA), srow),
            pl.BlockSpec((TM, ROPE_DIM), srow),
            pl.BlockSpec((TM, CONV_CH), row),
            pl.BlockSpec((TM // CHUNK, SUBLANES, CONV_CH), lambda i: (i, 0, 0)),
        ],
        out_shape=[
            jax.ShapeDtypeStruct((m, Q_LORA), BF16),
            jax.ShapeDtypeStruct((np_rows, KV_LORA), F32),
            jax.ShapeDtypeStruct((np_rows, ROPE_DIM), F32),
            jax.ShapeDtypeStruct((ns_rows, KV_LORA), F32),
            jax.ShapeDtypeStruct((ns_rows, ROPE_DIM), F32),
            jax.ShapeDtypeStruct((m, CONV_CH), BF16),
            jax.ShapeDtypeStruct((m // CHUNK, SUBLANES, CONV_CH), F32),
        ],
        scratch_shapes=[pltpu.VMEM((TM + SUBLANES, CONV_CH), F32)],
        compiler_params=pltpu.CompilerParams(dimension_semantics=("arbitrary",),
                                             vmem_limit_bytes=VMEM_LIMIT),
        name="in_proj",
    )(xp, xs, gmix, w_all, gq, gkv, gco, convw, cosk, sink, state)


def _attn_kernel(*refs, tq, n_past, causal):
    refs = list(refs)
    cqn_ref, wq_ref, wuk_ref, wuv_ref, cos_ref, sin_ref, gao_ref = refs[:7]
    refs = refs[7:]
    if n_past:
        pkv_ref, pkr_ref = refs[:2]
        refs = refs[2:]
    kv_ref, kr_ref, out_ref, qlat_ref, qr_ref, m_ref, l_ref, acc_ref, s_ref, klim_ref = refs

    qi = pl.program_id(1)
    rows = N_HEADS * tq

    q = jnp.dot(cqn_ref[...], wq_ref[...], preferred_element_type=F32)
    nope_w = N_HEADS * QK_NOPE
    rope_w = N_HEADS * ROPE_DIM
    qrope = q[:, nope_w:nope_w + rope_w] * cos_ref[...] + q[:, nope_w + rope_w:] * sin_ref[...]
    for h in range(N_HEADS):
        qn = q[:, h * QK_NOPE:(h + 1) * QK_NOPE].astype(BF16)
        ql = jnp.dot(qn, wuk_ref[h], preferred_element_type=F32)
        qlat_ref[h * tq:(h + 1) * tq, :] = ql.astype(BF16)
        qr_ref[h * tq:(h + 1) * tq, :] = qrope[:, h * ROPE_DIM:(h + 1) * ROPE_DIM].astype(BF16)

    m_ref[...] = jnp.full(m_ref.shape, NEG_BIG, F32)
    l_ref[...] = jnp.zeros(l_ref.shape, F32)
    acc_ref[...] = jnp.zeros(acc_ref.shape, F32)

    nt = (((1,), (1,)), ((), ()))

    def scores(kc_f32, kr_f32):
        s = lax.dot_general(qlat_ref[...], kc_f32.astype(BF16), nt, preferred_element_type=F32)
        return s + lax.dot_general(qr_ref[...], kr_f32.astype(BF16), nt, preferred_element_type=F32)

    def update(s, kc_f32, mask):
        if mask is not None:
            s = jnp.where(mask, s, NEG_BIG)
        m_prev = m_ref[...]
        m_new = jnp.maximum(m_prev, jnp.max(s, axis=-1, keepdims=True))
        alpha = jnp.exp2((m_prev - m_new) * EXP2_SCALE)
        p = jnp.exp2((s - _lane_bcast(m_new, s.shape[1])) * EXP2_SCALE)
        l_ref[...] = alpha * l_ref[...] + jnp.sum(p, axis=-1, keepdims=True)
        pv = jnp.dot(p.astype(BF16), kc_f32.astype(BF16), preferred_element_type=F32)
        acc_ref[...] = _lane_bcast(alpha, KV_LORA) * acc_ref[...] + pv
        m_ref[...] = m_new

    def pipelined(kv, kr, lo, hi, last, mask_fn):
        def body(j, c):
            k0 = pl.multiple_of(j * TK, TK)
            k1 = pl.multiple_of(jnp.minimum(j + 1, last) * TK, TK)
            s_cur = s_ref[j % 2]
            s_ref[(j + 1) % 2] = scores(kv[pl.ds(k1, TK), :], kr[pl.ds(k1, TK), :])
            update(s_cur, kv[pl.ds(k0, TK), :], None if mask_fn is None else mask_fn(k0))
            return c
        lax.fori_loop(lo, hi, body, 0)

    def pipelined_pairs(kv, kr, n_pairs, last):
        def body(i, c):
            k0 = pl.multiple_of(2 * i * TK, TK)
            k1 = pl.multiple_of((2 * i + 1) * TK, TK)
            k2 = pl.multiple_of(jnp.minimum(2 * i + 2, last) * TK, TK)
            s_ref[1] = scores(kv[pl.ds(k1, TK), :], kr[pl.ds(k1, TK), :])
            update(s_ref[0], kv[pl.ds(k0, TK), :], None)
            s_ref[0] = scores(kv[pl.ds(k2, TK), :], kr[pl.ds(k2, TK), :])
            update(s_ref[1], kv[pl.ds(k1, TK), :], None)
            return c
        lax.fori_loop(0, n_pairs, body, 0)

    if n_past:
        n_pb = n_past // TK
        s_ref[0] = scores(pkv_ref[pl.ds(0, TK), :], pkr_ref[pl.ds(0, TK), :])
        pipelined_pairs(pkv_ref, pkr_ref, n_pb // 2, n_pb - 1)
        if n_pb % 2:
            pipelined(pkv_ref, pkr_ref, n_pb - 1, n_pb, n_pb - 1, None)

    if causal:
        n_blocks = ((qi + 1) * tq + TK - 1) // TK
        n_full = jnp.minimum((qi * tq // CHUNK + 1) * CHUNK // TK, n_blocks)

        assert tq & (tq - 1) == 0 and CHUNK & (CHUNK - 1) == 0
        r = lax.broadcasted_iota(jnp.int32, (rows, LANES), 0)
        q_pos = qi * tq + (r & (tq - 1))
        klim_ref[...] = (q_pos & ~(CHUNK - 1)) + CHUNK

        def mask_fn(k0):
            cidx = lax.broadcasted_iota(jnp.int32, (rows, TK), 1)
            return cidx < _lane_bcast(klim_ref[...] - k0, TK)

        s_ref[0] = scores(kv_ref[pl.ds(0, TK), :], kr_ref[pl.ds(0, TK), :])
        pipelined_pairs(kv_ref, kr_ref, n_full // 2, n_blocks - 1)
        pipelined(kv_ref, kr_ref, n_full // 2 * 2, n_full, n_blocks - 1, None)
        pipelined(kv_ref, kr_ref, n_full, n_blocks, n_blocks - 1, mask_fn)
    else:
        update(scores(kv_ref[...], kr_ref[...]), kv_ref[...], None)

    o = acc_ref[...] / _lane_bcast(l_ref[...], KV_LORA)
    parts = []
    for h in range(N_HEADS):
        oh = o[h * tq:(h + 1) * tq, :].astype(BF16)
        parts.append(jnp.dot(oh, wuv_ref[h], preferred_element_type=F32))
    attn = jnp.concatenate(parts, axis=-1)
    out_ref[...] = _rms(attn, gao_ref[...]).astype(BF16)


def _attention(cqn, w_q, w_ukt, w_uv, cosq, sinq, gao, ckv, krope, *, n_batch, seq, row0,
               past_kv=None, past_kr=None):
    causal = past_kv is None
    tq = TQ if causal else seq
    nq = seq // tq
    n_past = 0 if causal else past_kv.shape[1]
    if not causal:
        assert n_past % CHUNK == 0 and seq <= CHUNK and n_past % TK == 0
    blk0 = row0 // tq
    qrow = lambda b, q: (blk0 + b * nq + q, 0)
    in_specs = [
        pl.BlockSpec((tq, Q_LORA), qrow),
        _const_spec(w_q.shape),
        _const_spec(w_ukt.shape),
        _const_spec(w_uv.shape),
        pl.BlockSpec((tq, N_HEADS * ROPE_DIM), lambda b, q: (q, 0)),
        pl.BlockSpec((tq, N_HEADS * ROPE_DIM), lambda b, q: (q, 0)),
        _const_spec((1, ATTN_W)),
    ]
    args = [cqn, w_q, w_ukt, w_uv, cosq, sinq, gao]
    if n_past:
        in_specs += [pl.BlockSpec((None, n_past, KV_LORA), lambda b, q: (b, 0, 0)),
                     pl.BlockSpec((None, n_past, ROPE_DIM), lambda b, q: (b, 0, 0))]
        args += [past_kv, past_kr]
    in_specs += [pl.BlockSpec((seq, KV_LORA), lambda b, q: (b, 0)),
                 pl.BlockSpec((seq, ROPE_DIM), lambda b, q: (b, 0))]
    args += [ckv, krope]
    rows = N_HEADS * tq
    kern = functools.partial(_attn_kernel, tq=tq, n_past=n_past, causal=causal)
    return pl.pallas_call(
        kern,
        grid=(n_batch, nq),
        in_specs=in_specs,
        out_specs=pl.BlockSpec((tq, ATTN_W), lambda b, q: (b * nq + q, 0)),
        out_shape=jax.ShapeDtypeStruct((n_batch * seq, ATTN_W), BF16),
        scratch_shapes=[
            pltpu.VMEM((rows, KV_LORA), BF16),
            pltpu.VMEM((rows, ROPE_DIM), BF16),
            pltpu.VMEM((rows, LANES), F32),
            pltpu.VMEM((rows, LANES), F32),
            pltpu.VMEM((rows, KV_LORA), F32),
            pltpu.VMEM((2, rows, TK), F32),
            pltpu.VMEM((rows, LANES), jnp.int32),
        ],
        compiler_params=pltpu.CompilerParams(dimension_semantics=("arbitrary", "arbitrary"),
                                             vmem_limit_bytes=VMEM_LIMIT),
        name="attn_prompt" if causal else "attn_sample",
    )(*args)


def _out_proj_kernel(attnp_ref, attns_ref, convn_ref, xp_ref, xs_ref, wo_ref, gffn_ref, wr_ref,
                     br_ref, h_ref, xpk_ref, mi_ref, mf_ref, cnt_ref, carry_ref, *, n_prompt_tiles):
    i = pl.program_id(0)

    @pl.when(i == 0)
    def _():
        carry_ref[...] = jnp.zeros(carry_ref.shape, F32)

    def tile(x_ref, attn_ref):
        y = jnp.dot(attn_ref[...], wo_ref[:ATTN_W, :], preferred_element_type=F32)
        y = y + jnp.dot(convn_ref[...], wo_ref[ATTN_W:, :], preferred_element_type=F32)
        h = x_ref[...] + y
        h_ref[...] = h
        xn = _rms(h, gffn_ref[...])

        half = D_MODEL // 2
        xh = xn.astype(BF16)
        xh32 = xh.astype(F32)
        lo = lax.bitcast_convert_type(xh32[:, :half], jnp.uint32)
        hi = lax.bitcast_convert_type(xh32[:, half:], jnp.uint32)
        xpk_ref[...] = (lo >> 16) | (hi & jnp.uint32(0xFFFF0000))

        xl = (xn - xh32).astype(BF16)
        hh_hl = jnp.dot(xh, wr_ref[...], preferred_element_type=F32)
        lh = jnp.dot(xl, wr_ref[:, :LANES], preferred_element_type=F32)
        logits = hh_hl[:, :LANES] + (lh + hh_hl[:, LANES:]) + br_ref[...]
        lane = lax.broadcasted_iota(jnp.int32, (TM, LANES), 1).astype(F32)
        ninf = -jnp.inf
        far = float(LANES)

        def first_argmax(v):
            vmax = jnp.max(v, axis=-1, keepdims=True)
            return vmax, jnp.min(jnp.where(v == vmax, lane, far), axis=-1, keepdims=True)

        gl = jnp.where(lane < N_GROUPS, logits, ninf)
        gmax, gidx = first_argmax(gl)
        g_p = 1.0 / jnp.sum(jnp.exp(gl - gmax), axis=-1, keepdims=True)
        e_lo = N_GROUPS + EXPERTS_PER_GROUP * gidx
        el = jnp.where((lane >= e_lo) & (lane < e_lo + EXPERTS_PER_GROUP), logits, ninf)
        e1max, i1 = first_argmax(el)
        z = jnp.sum(jnp.exp(el - e1max), axis=-1, keepdims=True)
        el2 = jnp.where(lane == i1, ninf, el)
        e2max, i2 = first_argmax(el2)
        p1 = 1.0 / z
        p2 = jnp.exp(e2max - e1max) / z
        den = p1 + p2
        g0 = g_p * p1 / den
        g1 = g_p * p2 / den
        e0 = i1 - N_GROUPS
        e1 = i2 - N_GROUPS

        oh0 = lane == e0
        oh1 = lane == e1
        oh = jnp.where(oh0 | oh1, 1.0, 0.0)
        r = lax.broadcasted_iota(jnp.int32, (TM, TM), 0)
        c = lax.broadcasted_iota(jnp.int32, (TM, TM), 1)
        ltri = jnp.where(r > c, 1.0, 0.0).astype(BF16)
        before = jnp.dot(ltri, oh.astype(BF16), preferred_element_type=F32) + carry_ref[...]
        rank0 = jnp.sum(jnp.where(oh0, before, 0.0), axis=-1, keepdims=True)
        rank1 = jnp.sum(jnp.where(oh1, before, 0.0), axis=-1, keepdims=True)
        total = carry_ref[...] + jnp.sum(oh, axis=0, keepdims=True)
        carry_ref[...] = total
        cnt_ref[...] = jnp.broadcast_to(total, cnt_ref.shape)

        mi = jnp.where(lane == 0, e0, jnp.where(lane == 1, e1, jnp.where(lane == 2, rank0, rank1)))
        mi_ref[...] = jnp.transpose(mi)[:SUBLANES, :].astype(jnp.int32)
        mf_ref[...] = jnp.where(lane == 0, g0, g1)

    @pl.when(i < n_prompt_tiles)
    def _():
        tile(xp_ref, attnp_ref)

    @pl.when(i >= n_prompt_tiles)
    def _():
        tile(xs_ref, attns_ref)


def _out_proj(attn_p, attn_s, conv_n, xp, xs, w_ob, gffn, w_r2, b_r):
    m = conv_n.shape[0]
    npt = xp.shape[0] // TM
    last_p = npt - 1
    row = lambda i: (i, 0)
    return pl.pallas_call(
        functools.partial(_out_proj_kernel, n_prompt_tiles=npt),
        grid=(m // TM,),
        in_specs=[
            pl.BlockSpec((TM, ATTN_W), lambda i: (jnp.minimum(i, last_p), 0)),
            pl.BlockSpec((TM, ATTN_W), lambda i: (jnp.maximum(i - npt, 0), 0)),
            pl.BlockSpec((TM, CONV_CH), row),
            pl.BlockSpec((TM, D_MODEL), lambda i: (jnp.minimum(i, last_p), 0)),
            pl.BlockSpec((TM, D_MODEL), lambda i: (jnp.maximum(i - npt, 0), 0)),
            _const_spec(w_ob.shape),
            _const_spec((1, D_MODEL)),
            _const_spec(w_r2.shape),
            _const_spec((1, LANES)),
        ],
        out_specs=[
            pl.BlockSpec((TM, D_MODEL), row),
            pl.BlockSpec((TM, D_MODEL // 2), row),
            pl.BlockSpec((SUBLANES, TM), lambda i: (0, i)),
            pl.BlockSpec((TM, LANES), row),
            pl.BlockSpec((SUBLANES, LANES), lambda i: (0, 0)),
        ],
        out_shape=[
            jax.ShapeDtypeStruct((m, D_MODEL), F32),
            jax.ShapeDtypeStruct((m, D_MODEL // 2), jnp.uint32),
            jax.ShapeDtypeStruct((SUBLANES, m), jnp.int32),
            jax.ShapeDtypeStruct((m, LANES), F32),
            jax.ShapeDtypeStruct((SUBLANES, LANES), F32),
        ],
        scratch_shapes=[pltpu.VMEM((1, LANES), F32)],
        compiler_params=pltpu.CompilerParams(dimension_semantics=("arbitrary",),
                                             vmem_limit_bytes=VMEM_LIMIT),
        name="out_proj",
    )(attn_p, attn_s, conv_n, xp, xs, w_ob, gffn, w_r2, b_r)


def _dispatch_kernel(d0_ref, d1_ref, zlo_ref, zn_ref, nu_ref, xpk_ref, xs_hbm, zeros_ref, sems, *, n_blocks):
    i = pl.program_id(0)
    sem = sems.at[0]
    zsem = sems.at[1]

    def row_copy(src_ref, src_row, dst_row):
        return pltpu.make_async_copy(src_ref.at[pl.ds(src_row, 1)], xs_hbm.at[pl.ds(dst_row, 1)], sem)

    def zero_fill(act):
        def per_expert(e, c):
            lo = zlo_ref[e]
            n = zn_ref[e]
            head = (-lo) & (SUBLANES - 1)
            for r in range(SUBLANES - 1):
                @pl.when(r < head)
                def _(r=r):
                    act(pltpu.make_async_copy(zeros_ref.at[pl.ds(0, 1)], xs_hbm.at[pl.ds(lo + r, 1)], zsem))
            off = lo + head
            rest = n - head
            size = MOE_BLOCK // 2
            while size >= SUBLANES:
                @pl.when((rest & size) != 0)
                def _(off=off, size=size):
                    dst = xs_hbm.at[pl.ds(pl.multiple_of(off, SUBLANES), size)]
                    act(pltpu.make_async_copy(zeros_ref.at[pl.ds(0, size)], dst, zsem))
                off = off + (rest & size)
                size //= 2
            return c

        def per_block(b, c):
            dst = xs_hbm.at[pl.ds(pl.multiple_of(b * MOE_BLOCK, MOE_BLOCK), MOE_BLOCK)]
            act(pltpu.make_async_copy(zeros_ref, dst, zsem))
            return c

        lax.fori_loop(0, N_EXPERTS, per_expert, 0)
        lax.fori_loop(nu_ref[0], n_blocks, per_block, 0)

    @pl.when(i == 0)
    def _():
        zeros_ref[...] = jnp.zeros(zeros_ref.shape, zeros_ref.dtype)
        zero_fill(lambda cp: cp.start())
        zero_fill(lambda cp: cp.wait())

    base = i * TM

    def start(r, c):
        row_copy(xpk_ref, r, d0_ref[base + r]).start()
        row_copy(xpk_ref, r, d1_ref[base + r]).start()
        return c

    lax.fori_loop(0, TM, start, 0, unroll=DMA_ISSUE_UNROLL)
    for _ in range(2):
        pltpu.make_async_copy(xpk_ref, xs_hbm.at[pl.ds(0, TM)], sem).wait()


def _dispatch(dest0, dest1, pad_lo, n_pad, n_used, xpk, n_blocks):
    m = xpk.shape[0]
    grid_spec = pltpu.PrefetchScalarGridSpec(
        num_scalar_prefetch=5,
        grid=(m // TM,),
        in_specs=[pl.BlockSpec((TM, D_MODEL // 2), lambda i, *_: (i, 0))],
        out_specs=pl.BlockSpec(memory_space=pl.ANY),
        scratch_shapes=[pltpu.VMEM((MOE_BLOCK, D_MODEL // 2), jnp.uint32),
                        pltpu.SemaphoreType.DMA((2,))],
    )
    return pl.pallas_call(
        functools.partial(_dispatch_kernel, n_blocks=n_blocks),
        grid_spec=grid_spec,
        out_shape=jax.ShapeDtypeStruct((n_blocks * MOE_BLOCK, D_MODEL // 2), jnp.uint32),
        compiler_params=pltpu.CompilerParams(dimension_semantics=("arbitrary",)),
        name="dispatch",
    )(dest0, dest1, pad_lo, n_pad, n_used, xpk)


def _experts_kernel(be_ref, nu_ref, nxt_ref, x_ref, wg_hbm, wu_hbm, wd_hbm, y_ref,
                    sg_ref, su_ref, sd_ref, wgb_ref, wub_ref, wdb_ref, sems):
    b = pl.program_id(0)
    active = b < nu_ref[0]
    new_expert = jnp.logical_or(b == 0, be_ref[b] != be_ref[jnp.maximum(b - 1, 0)])

    def weight_copies(e):
        return (pltpu.make_async_copy(wg_hbm.at[e], sg_ref, sems.at[0]),
                pltpu.make_async_copy(wu_hbm.at[e], su_ref, sems.at[1]),
                pltpu.make_async_copy(wd_hbm.at[e], sd_ref, sems.at[2]))

    @pl.when(b == 0)
    def _():
        for cp in weight_copies(be_ref[0]):
            cp.start()

    @pl.when(jnp.logical_and(active, new_expert))
    def _():
        for cp in weight_copies(be_ref[b]):
            cp.wait()
        wgb_ref[...] = sg_ref[...].astype(BF16)
        wub_ref[...] = su_ref[...].astype(BF16)
        wdb_ref[...] = sd_ref[...].astype(BF16)

        @pl.when(nxt_ref[b] >= 0)
        def _():
            for cp in weight_copies(nxt_ref[b]):
                cp.start()

    @pl.when(active)
    def _():
        half = D_MODEL // 2
        xw = x_ref[...]
        xa = lax.bitcast_convert_type(xw << 16, F32).astype(BF16)
        xb = lax.bitcast_convert_type(xw & jnp.uint32(0xFFFF0000), F32).astype(BF16)
        g = jnp.dot(xa, wgb_ref[:half, :], preferred_element_type=F32)
        g = g + jnp.dot(xb, wgb_ref[half:, :], preferred_element_type=F32)
        u = jnp.dot(xa, wub_ref[:half, :], preferred_element_type=F32)
        u = u + jnp.dot(xb, wub_ref[half:, :], preferred_element_type=F32)
        hmid = (g * jax.nn.sigmoid(g)) * u
        y_ref[...] = jnp.dot(hmid.astype(BF16), wdb_ref[...], preferred_element_type=F32)

    @pl.when(b >= nu_ref[0])
    def _():
        y_ref[...] = jnp.zeros(y_ref.shape, y_ref.dtype)


def _experts(block_e, n_used, next_e, x_sorted, w_gate, w_up, w_down):
    p = x_sorted.shape[0]
    nb = p // MOE_BLOCK

    def xrow(b, be, nu, nxt):
        return (jnp.maximum(jnp.minimum(b, nu[0] - 1), 0), 0)

    grid_spec = pltpu.PrefetchScalarGridSpec(
        num_scalar_prefetch=3,
        grid=(nb,),
        in_specs=[
            pl.BlockSpec((MOE_BLOCK, D_MODEL // 2), xrow),
            pl.BlockSpec(memory_space=pl.ANY),
            pl.BlockSpec(memory_space=pl.ANY),
            pl.BlockSpec(memory_space=pl.ANY),
        ],
        out_specs=pl.BlockSpec((MOE_BLOCK, D_MODEL), lambda b, be, nu, nxt: (b, 0)),
        scratch_shapes=[pltpu.VMEM((D_MODEL, D_FF), F32), pltpu.VMEM((D_MODEL, D_FF), F32),
                        pltpu.VMEM((D_FF, D_MODEL), F32),
                        pltpu.VMEM((D_MODEL, D_FF), BF16), pltpu.VMEM((D_MODEL, D_FF), BF16),
                        pltpu.VMEM((D_FF, D_MODEL), BF16),
                        pltpu.SemaphoreType.DMA((3,))],
    )
    return pl.pallas_call(
        _experts_kernel,
        grid_spec=grid_spec,
        out_shape=jax.ShapeDtypeStruct((p, D_MODEL), F32),
        compiler_params=pltpu.CompilerParams(dimension_semantics=("arbitrary",),
                                             vmem_limit_bytes=VMEM_LIMIT),
        name="experts",
    )(block_e, n_used, next_e, x_sorted, w_gate, w_up, w_down)


def _combine_kernel(d0_ref, d1_ref, h_ref, mf_ref, gfin_ref, y_hbm, outp_ref, outs_ref, y0_ref, y1_ref, sems,
                    *, n_tiles, n_prompt_tiles):
    i = pl.program_id(0)

    def gather(tile, slot, act):
        base = tile * TM

        def body(r, c):
            act(pltpu.make_async_copy(y_hbm.at[pl.ds(d0_ref[base + r], 1)], y0_ref.at[slot, pl.ds(r, 1)],
                                      sems.at[slot]))
            act(pltpu.make_async_copy(y_hbm.at[pl.ds(d1_ref[base + r], 1)], y1_ref.at[slot, pl.ds(r, 1)],
                                      sems.at[slot]))
            return c
        lax.fori_loop(0, TM, body, 0, unroll=DMA_ISSUE_UNROLL)

    @pl.when(i == 0)
    def _():
        gather(0, 0, lambda cp: cp.start())

    @pl.when(i + 1 < n_tiles)
    def _():
        gather(i + 1, (i + 1) % 2, lambda cp: cp.start())

    slot = i % 2
    pltpu.make_async_copy(y_hbm.at[pl.ds(0, TM)], y0_ref.at[slot], sems.at[slot]).wait()
    pltpu.make_async_copy(y_hbm.at[pl.ds(0, TM)], y1_ref.at[slot], sems.at[slot]).wait()
    mf = mf_ref[...]
    ffn = mf[:, 0:1] * y0_ref[slot] + mf[:, 1:2] * y1_ref[slot]
    out = _rms(h_ref[...] + ffn, gfin_ref[...])

    @pl.when(i < n_prompt_tiles)
    def _():
        outp_ref[...] = out

    @pl.when(i >= n_prompt_tiles)
    def _():
        outs_ref[...] = out


def _combine(dest0, dest1, h, mf, gfin, y_sorted, *, n_prompt_rows):
    m = h.shape[0]
    npt = n_prompt_rows // TM
    grid_spec = pltpu.PrefetchScalarGridSpec(
        num_scalar_prefetch=2,
        grid=(m // TM,),
        in_specs=[
            pl.BlockSpec((TM, D_MODEL), lambda i, *_: (i, 0)),
            pl.BlockSpec((TM, LANES), lambda i, *_: (i, 0)),
            pl.BlockSpec((1, D_MODEL), lambda i, *_: (0, 0)),
            pl.BlockSpec(memory_space=pl.ANY),
        ],
        out_specs=[pl.BlockSpec((TM, D_MODEL), lambda i, *_: (jnp.minimum(i, npt - 1), 0)),
                   pl.BlockSpec((TM, D_MODEL), lambda i, *_: (jnp.maximum(i - npt, 0), 0))],
        scratch_shapes=[pltpu.VMEM((2, TM, D_MODEL), F32), pltpu.VMEM((2, TM, D_MODEL), F32),
                        pltpu.SemaphoreType.DMA((2,))],
    )
    return pl.pallas_call(
        functools.partial(_combine_kernel, n_tiles=m // TM, n_prompt_tiles=npt),
        grid_spec=grid_spec,
        out_shape=[jax.ShapeDtypeStruct((n_prompt_rows, D_MODEL), F32),
                   jax.ShapeDtypeStruct((m - n_prompt_rows, D_MODEL), F32)],
        compiler_params=pltpu.CompilerParams(dimension_semantics=("arbitrary",),
                                             vmem_limit_bytes=VMEM_LIMIT),
        name="combine",
    )(dest0, dest1, h, mf, gfin, y_sorted)


def _rope_tables(pos):
    f32 = np.float32
    inv = np.power(f32(ROPE_THETA), -np.arange(0, ROPE_DIM, 2, dtype=f32) / f32(ROPE_DIM)).astype(f32)
    ang = (pos.astype(f32)[:, None] * inv[None, :]).astype(f32)
    cos, sin = np.cos(ang).astype(f32), np.sin(ang).astype(f32)
    return np.concatenate([cos, cos], axis=-1), np.concatenate([-sin, sin], axis=-1)


def _swap_halves(w):
    return jnp.concatenate([w[..., ROPE_DIM // 2:], w[..., :ROPE_DIM // 2]], axis=-1)


def kernel(x_prompt, x_sample, cache_kv_latent, cache_k_rope, state_conv, norm_mix, w_in, norm_q, w_uq,
           norm_kv, w_uk, w_uv, conv_w, norm_attn_out, norm_conv_out, w_o, norm_ffn, w_router_group,
           b_router_group, w_router_expert, b_router_expert, w_gate, w_up, w_down, norm_final):
    assert w_in.shape[0] == 1, "single-layer trunk"
    bp, seq_p, _ = x_prompt.shape
    bs, seq_s, _ = x_sample.shape
    past_len = cache_kv_latent.shape[2]
    np_rows, ns_rows = bp * seq_p, bs * seq_s
    m = np_rows + ns_rows
    assert seq_p % TM == 0 and TM % seq_s == 0 and ns_rows % TM == 0 and seq_s == CHUNK

    xp = x_prompt.reshape(np_rows, D_MODEL)
    xs = x_sample.reshape(ns_rows, D_MODEL)
    row_vec = lambda v: v.reshape(1, -1)

    w_in0 = w_in[0]
    mla_w = Q_LORA + KV_LORA + ROPE_DIM
    assert w_in0.shape[1] == mla_w + 3 * CONV_CH
    w_all = _w_in_layout(w_in0)
    wq4 = w_uq[0].reshape(Q_LORA, N_HEADS, QK_NOPE + ROPE_DIM)
    wq_rope = wq4[:, :, QK_NOPE:]
    w_q = jnp.concatenate([wq4[:, :, :QK_NOPE].reshape(Q_LORA, -1), wq_rope.reshape(Q_LORA, -1),
                           _swap_halves(wq_rope).reshape(Q_LORA, -1)], axis=1).astype(BF16)
    w_ukt = jnp.transpose(w_uk[0], (1, 2, 0)).astype(BF16)
    w_uvh = jnp.transpose(w_uv[0], (1, 0, 2)).astype(BF16)
    w_ob = w_o[0].astype(BF16)
    n_router = N_GROUPS + N_EXPERTS
    w_r = jnp.concatenate([w_router_group[0], w_router_expert[0].reshape(D_MODEL, N_EXPERTS)], axis=1)
    w_r = jnp.pad(w_r, ((0, 0), (0, LANES - n_router)))
    w_rh = w_r.astype(BF16)
    w_rl = (w_r - w_rh.astype(F32)).astype(BF16)
    w_r2 = jnp.concatenate([w_rh, w_rl], axis=1)
    b_r =jnp.pad(jnp.concatenate([b_router_group[0], b_router_expert[0].reshape(N_EXPERTS)]),
                  (0, LANES - n_router)).reshape(1, LANES)

    cos_p, sin_p = _rope_tables(np.arange(seq_p))
    cos_s, sin_s = _rope_tables(past_len + np.arange(seq_s))
    cosk = np.concatenate([cos_p, np.tile(cos_s, (TM // seq_s, 1))], axis=0)
    sink = np.concatenate([sin_p, np.tile(sin_s, (TM // seq_s, 1))], axis=0)
    state = jnp.concatenate([jnp.zeros((bp, CONV_W - 1, CONV_CH), F32), state_conv[0]], axis=0)

    cqn, ckv_p, kr_p, ckv_s, kr_s, conv_n, utail = _in_proj(
        xp, xs, row_vec(norm_mix[0]), w_all, row_vec(norm_q[0]), row_vec(norm_kv[0]),
        row_vec(norm_conv_out[0]), conv_w[0], cosk, sink, state, seq_p=seq_p, seq_s=seq_s)

    gao = row_vec(norm_attn_out[0])
    attn_p = _attention(cqn, w_q, w_ukt, w_uvh, np.tile(cos_p, (1, N_HEADS)), np.tile(sin_p, (1, N_HEADS)),
                        gao, ckv_p, kr_p, n_batch=bp, seq=seq_p, row0=0)
    attn_s = _attention(cqn, w_q, w_ukt, w_uvh, np.tile(cos_s, (1, N_HEADS)), np.tile(sin_s, (1, N_HEADS)),
                        gao, ckv_s, kr_s, n_batch=bs, seq=seq_s, row0=np_rows,
                        past_kv=cache_kv_latent[0], past_kr=cache_k_rope[0])

    h, xpk, mi, mf, cnt = _out_proj(attn_p, attn_s, conv_n, xp, xs, w_ob, row_vec(norm_ffn[0]),
                                    w_r2, b_r)

    counts = cnt[0, :N_EXPERTS].astype(jnp.int32)
    padded = (counts + MOE_BLOCK - 1) // MOE_BLOCK * MOE_BLOCK
    pad_end = jnp.cumsum(padded)
    pad_start = pad_end - padded
    n_blocks = -(-(m * 2) // MOE_BLOCK) + N_EXPERTS
    block_row0 = jnp.arange(n_blocks, dtype=jnp.int32) * MOE_BLOCK
    block_e = jnp.minimum(jnp.sum((pad_end[None, :] <= block_row0[:, None]).astype(jnp.int32), axis=1),
                          N_EXPERTS - 1)
    n_used = (pad_end[-1:] // MOE_BLOCK).astype(jnp.int32)
    expert_ids = jnp.arange(N_EXPERTS, dtype=jnp.int32)[:, None]

    def seg_start(e):
        return jnp.sum(jnp.where(expert_ids == e[None, :], pad_start[:, None], 0), axis=0)

    dest0 = seg_start(mi[0]) + mi[2]
    dest1 = seg_start(mi[1]) + mi[3]

    x_sorted = _dispatch(dest0, dest1, pad_start + counts, padded - counts, n_used, xpk, n_blocks)
    later = (expert_ids.T > block_e[:, None]) & (padded > 0)[None, :]
    next_e = jnp.min(jnp.where(later, expert_ids.T, N_EXPERTS), axis=1)
    next_e = jnp.where(next_e == N_EXPERTS, -1, next_e).astype(jnp.int32)
    y_sorted = _experts(block_e, n_used, next_e, x_sorted, w_gate[0], w_up[0], w_down[0])
    gfin = row_vec(norm_final)
    y_p, y_s = _combine(dest0, dest1, h, mf, gfin, y_sorted, n_prompt_rows=np_rows)

    ut = utail.reshape(m // CHUNK, SUBLANES, CONV_CH)
    tails = ut[:, SUBLANES - (CONV_W - 1):, :]
    p_last = (jnp.arange(bp) + 1) * (seq_p // CHUNK) - 1
    s_last = np_rows // CHUNK + (jnp.arange(bs) + 1) * (seq_s // CHUNK) - 1
    return (y_p.reshape(bp, seq_p, D_MODEL),
            y_s.reshape(bs, seq_s, D_MODEL),
            ckv_p.reshape(1, bp, seq_p, KV_LORA),
            kr_p.reshape(1, bp, seq_p, ROPE_DIM),
            tails[p_last][None],
            ckv_s.reshape(1, bs, seq_s, KV_LORA),
            kr_s.reshape(1, bs, seq_s, ROPE_DIM),
            tails[s_last][None])
```

```python
import functools

import jax
import jax.numpy as jnp
import numpy as np
from jax import lax
from jax.experimental import pallas as pl
from jax.experimental.pallas import tpu as pltpu

F32 = jnp.float32
BF16 = jnp.bfloat16

D_MODEL = 2048
N_HEADS = 8
QK_NOPE = 128
ROPE_DIM = 64
V_DIM = 128
Q_LORA = 512
KV_LORA = 512
ATTN_W = N_HEADS * V_DIM
CONV_CH = D_MODEL - ATTN_W
CONV_W = 3
CHUNK = 64
N_GROUPS = 4
EXPERTS_PER_GROUP = 8
N_EXPERTS = N_GROUPS * EXPERTS_PER_GROUP
D_FF = 512
ROPE_THETA = 10000.0
EPS = 1e-6
ATTN_SCALE = (QK_NOPE + ROPE_DIM) ** -0.5
EXP2_SCALE = ATTN_SCALE * 1.4426950408889634

LANES = 128
SUBLANES = 8
TM = 256
MOE_BLOCK = 256
TQ = 256
TK = 256
DMA_ISSUE_UNROLL = 8
NEG_BIG = -1e30
VMEM_LIMIT = 56 * 1024 * 1024


def _rms(v, g):
    return v * lax.rsqrt(jnp.mean(v * v, axis=-1, keepdims=True) + EPS) * g


def _lane_bcast(v, width):
    if width % LANES == 0:
        return jnp.concatenate([v] * (width // LANES), axis=1)
    assert width < LANES
    return v[:, :width]


def _const_spec(shape):
    nd = len(shape)
    return pl.BlockSpec(shape, lambda *_: (0,) * nd, pipeline_mode=pl.Buffered(1))


def _in_proj_kernel(xp_ref, xs_ref, gmix_ref, wt_ref, gq_ref, gkv_ref, gco_ref, convw_ref,
                    cos_ref, sin_ref, state_ref,
                    cqn_ref, ckvp_ref, krp_ref, ckvs_ref, krs_ref, convn_ref, utail_ref, ext_ref,
                    *, n_prompt_tiles, tiles_per_seq, n_prompt_seq, sample_seq_len):
    i = pl.program_id(0)

    def conv_block(u_sub, gate_sub, row0, length):
        ext_ref[SUBLANES:SUBLANES + length, :] = u_sub
        um1 = ext_ref[SUBLANES - 1:SUBLANES - 1 + length, :]
        um2 = ext_ref[SUBLANES - 2:SUBLANES - 2 + length, :]
        cw = convw_ref[...]
        conv = cw[0:1] * um2 + cw[1:2] * um1 + cw[2:3] * u_sub
        convn_ref[row0:row0 + length, :] = _rms(gate_sub * conv, gco_ref[...]).astype(BF16)

    def tile(x_ref, is_prompt):
        ckv_ref, krt_ref = (ckvp_ref, krp_ref) if is_prompt else (ckvs_ref, krs_ref)
        x = x_ref[...]
        xn = _rms(x, gmix_ref[...]).astype(BF16)
        lat_w = Q_LORA + KV_LORA
        conv0 = lat_w + ROPE_DIM
        nt = (((1,), (1,)), ((), ()))
        za = lax.dot_general(xn, wt_ref[:lat_w, :], nt, preferred_element_type=F32)
        cqn_ref[...] = _rms(za[:, :Q_LORA], gq_ref[...]).astype(BF16)
        ckv_ref[...] = _rms(za[:, Q_LORA:], gkv_ref[...])
        zk = lax.dot_general(xn, wt_ref[lat_w:conv0, :], nt, preferred_element_type=F32)
        zk_swapped = jnp.concatenate([zk[:, ROPE_DIM // 2:], zk[:, :ROPE_DIM // 2]], axis=1)
        k_rope = zk * cos_ref[...] + zk_swapped * sin_ref[...]
        if is_prompt:
            krt_ref[...] = k_rope.T
        else:
            for k in range(TM // sample_seq_len):
                krt_ref[k] = k_rope[k * sample_seq_len:(k + 1) * sample_seq_len, :].T

        zc = lax.dot_general(xn, wt_ref[conv0:, :], nt, preferred_element_type=F32)
        gate_b = zc[:, :CONV_CH]
        u = zc[:, CONV_CH:2 * CONV_CH] * zc[:, 2 * CONV_CH:]
        for j in range(TM // CHUNK):
            utail_ref[j] = u[CHUNK * (j + 1) - SUBLANES:CHUNK * (j + 1), :]

        if is_prompt:
            first = (i % tiles_per_seq) == 0

            @pl.when(first)
            def _():
                ext_ref[SUBLANES - 2:SUBLANES, :] = state_ref[i // tiles_per_seq]

            @pl.when(jnp.logical_not(first))
            def _():
                ext_ref[SUBLANES - 2:SUBLANES, :] = ext_ref[TM + SUBLANES - 2:TM + SUBLANES, :]

            conv_block(u, gate_b, 0, TM)
        else:
            n_sub = TM // sample_seq_len
            seq0 = n_prompt_seq + (i - n_prompt_tiles) * n_sub
            for k in range(n_sub):
                ext_ref[SUBLANES - 2:SUBLANES, :] = state_ref[seq0 + k]
                lo = k * sample_seq_len
                conv_block(u[lo:lo + sample_seq_len], gate_b[lo:lo + sample_seq_len], lo, sample_seq_len)

    @pl.when(i < n_prompt_tiles)
    def _():
        tile(xp_ref, True)

    @pl.when(i >= n_prompt_tiles)
    def _():
        tile(xs_ref, False)


def _in_proj(xp, xs, gmix, w_t, gq, gkv, gco, convw, cosk, sink, state, *, seq_p, seq_s):
    np_rows, ns_rows = xp.shape[0], xs.shape[0]
    m = np_rows + ns_rows
    npt, nst = np_rows // TM, ns_rows // TM
    tps = seq_p // TM
    n_prompt_seq = np_rows // seq_p
    last_p = npt - 1

    def tab_idx(i):
        return (jnp.where(i < npt, i % tps, tps), 0)

    row = lambda i: (i, 0)
    prow = lambda i: (jnp.minimum(i, last_p), 0)
    srow = lambda i: (jnp.maximum(i - npt, 0), 0)
    kern = functools.partial(_in_proj_kernel, n_prompt_tiles=npt, tiles_per_seq=tps,
                             n_prompt_seq=n_prompt_seq, sample_seq_len=seq_s)
    return pl.pallas_call(
        kern,
        grid=(npt + nst,),
        in_specs=[
            pl.BlockSpec((TM, D_MODEL), prow),
            pl.BlockSpec((TM, D_MODEL), srow),
            _const_spec((1, D_MODEL)),
            _const_spec(w_t.shape),
            _const_spec((1, Q_LORA)),
            _const_spec((1, KV_LORA)),
            _const_spec((1, CONV_CH)),
            _const_spec((CONV_W, CONV_CH)),
            pl.BlockSpec((TM, ROPE_DIM), tab_idx),
            pl.BlockSpec((TM, ROPE_DIM), tab_idx),
            _const_spec(state.shape),
        ],
        out_specs=[
            pl.BlockSpec((TM, Q_LORA), row),
            pl.BlockSpec((TM, KV_LORA), prow),
            pl.BlockSpec((None, ROPE_DIM, TM), lambda i: (jnp.minimum(i, last_p) // tps, 0,
                                                          jnp.minimum(i, last_p) % tps)),
            pl.BlockSpec((TM, KV_LORA), srow),
            pl.BlockSpec((TM // seq_s, ROPE_DIM, seq_s), lambda i: (jnp.maximum(i - npt, 0), 0, 0)),
            pl.BlockSpec((TM, CONV_CH), row),
            pl.BlockSpec((TM // CHUNK, SUBLANES, CONV_CH), lambda i: (i, 0, 0)),
        ],
        out_shape=[
            jax.ShapeDtypeStruct((m, Q_LORA), BF16),
            jax.ShapeDtypeStruct((np_rows, KV_LORA), F32),
            jax.ShapeDtypeStruct((n_prompt_seq, ROPE_DIM, seq_p), F32),
            jax.ShapeDtypeStruct((ns_rows, KV_LORA), F32),
            jax.ShapeDtypeStruct((ns_rows // seq_s, ROPE_DIM, seq_s), F32),
            jax.ShapeDtypeStruct((m, CONV_CH), BF16),
            jax.ShapeDtypeStruct((m // CHUNK, SUBLANES, CONV_CH), F32),
        ],
        scratch_shapes=[pltpu.VMEM((TM + SUBLANES, CONV_CH), F32)],
        compiler_params=pltpu.CompilerParams(dimension_semantics=("arbitrary",),
                                             vmem_limit_bytes=VMEM_LIMIT),
        name="in_proj",
    )(xp, xs, gmix, w_t, gq, gkv, gco, convw, cosk, sink, state)


def _attn_kernel(*refs, tq, n_past, causal):
    refs = list(refs)
    cqn_ref, wq_ref, wuk_ref, wuv_ref, cos_ref, sin_ref, gao_ref = refs[:7]
    refs = refs[7:]
    if n_past:
        pkv_ref, pkr_ref = refs[:2]
        refs = refs[2:]
    kv_ref, kr_ref, out_ref, qlat_ref, qr_ref, m_ref, l_ref, acc_ref, s_ref, klim_ref = refs

    qi = pl.program_id(1)
    rows = N_HEADS * tq

    q = jnp.dot(cqn_ref[...], wq_ref[...], preferred_element_type=F32)
    nope_w = N_HEADS * QK_NOPE
    rope_w = N_HEADS * ROPE_DIM
    qrope = q[:, nope_w:nope_w + rope_w] * cos_ref[...] + q[:, nope_w + rope_w:] * sin_ref[...]
    for h in range(N_HEADS):
        qn = q[:, h * QK_NOPE:(h + 1) * QK_NOPE].astype(BF16)
        ql = jnp.dot(qn, wuk_ref[h], preferred_element_type=F32)
        qlat_ref[h * tq:(h + 1) * tq, :] = ql.astype(BF16)
        qr_ref[h * tq:(h + 1) * tq, :] = qrope[:, h * ROPE_DIM:(h + 1) * ROPE_DIM].astype(BF16)

    m_ref[...] = jnp.full(m_ref.shape, NEG_BIG, F32)
    l_ref[...] = jnp.zeros(l_ref.shape, F32)
    acc_ref[...] = jnp.zeros(acc_ref.shape, F32)

    nt = (((1,), (1,)), ((), ()))

    def scores(kc_f32, krt_f32):
        s = lax.dot_general(qlat_ref[...], kc_f32.astype(BF16), nt, preferred_element_type=F32)
        return s + jnp.dot(qr_ref[...], krt_f32.astype(BF16), preferred_element_type=F32)

    def update(s, kc_f32, mask):
        if mask is not None:
            s = jnp.where(mask, s, NEG_BIG)
        m_prev = m_ref[...]
        m_new = jnp.maximum(m_prev, jnp.max(s, axis=-1, keepdims=True))
        alpha = jnp.exp2((m_prev - m_new) * EXP2_SCALE)
        p = jnp.exp2((s - _lane_bcast(m_new, s.shape[1])) * EXP2_SCALE)
        l_ref[...] = alpha * l_ref[...] + jnp.sum(p, axis=-1, keepdims=True)
        pv = jnp.dot(p.astype(BF16), kc_f32.astype(BF16), preferred_element_type=F32)
        acc_ref[...] = _lane_bcast(alpha, KV_LORA) * acc_ref[...] + pv
        m_ref[...] = m_new

    def pipelined(kv, kr, lo, hi, last, mask_fn):
        def body(j, c):
            k0 = pl.multiple_of(j * TK, TK)
            k1 = pl.multiple_of(jnp.minimum(j + 1, last) * TK, TK)
            s_cur = s_ref[j % 2]
            s_ref[(j + 1) % 2] = scores(kv[pl.ds(k1, TK), :], kr[:, pl.ds(k1, TK)])
            update(s_cur, kv[pl.ds(k0, TK), :], None if mask_fn is None else mask_fn(k0))
            return c
        lax.fori_loop(lo, hi, body, 0)

    def pipelined_pairs(kv, kr, n_pairs, last):
        def body(i, c):
            k0 = pl.multiple_of(2 * i * TK, TK)
            k1 = pl.multiple_of((2 * i + 1) * TK, TK)
            k2 = pl.multiple_of(jnp.minimum(2 * i + 2, last) * TK, TK)
            s_ref[1] = scores(kv[pl.ds(k1, TK), :], kr[:, pl.ds(k1, TK)])
            update(s_ref[0], kv[pl.ds(k0, TK), :], None)
            s_ref[0] = scores(kv[pl.ds(k2, TK), :], kr[:, pl.ds(k2, TK)])
            update(s_ref[1], kv[pl.ds(k1, TK), :], None)
            return c
        lax.fori_loop(0, n_pairs, body, 0)

    if n_past:
        n_pb = n_past // TK
        s_ref[0] = scores(pkv_ref[pl.ds(0, TK), :], pkr_ref[:, pl.ds(0, TK)])
        pipelined_pairs(pkv_ref, pkr_ref, n_pb // 2, n_pb - 1)
        if n_pb % 2:
            pipelined(pkv_ref, pkr_ref, n_pb - 1, n_pb, n_pb - 1, None)

    if causal:
        n_blocks = ((qi + 1) * tq + TK - 1) // TK
        n_full = jnp.minimum((qi * tq // CHUNK + 1) * CHUNK // TK, n_blocks)

        assert tq & (tq - 1) == 0 and CHUNK & (CHUNK - 1) == 0
        r = lax.broadcasted_iota(jnp.int32, (rows, LANES), 0)
        q_pos = qi * tq + (r & (tq - 1))
        klim_ref[...] = (q_pos & ~(CHUNK - 1)) + CHUNK

        def mask_fn(k0):
            cidx = lax.broadcasted_iota(jnp.int32, (rows, TK), 1)
            return cidx < _lane_bcast(klim_ref[...] - k0, TK)

        s_ref[0] = scores(kv_ref[pl.ds(0, TK), :], kr_ref[:, pl.ds(0, TK)])
        pipelined_pairs(kv_ref, kr_ref, n_full // 2, n_blocks - 1)
        pipelined(kv_ref, kr_ref, n_full // 2 * 2, n_full, n_blocks - 1, None)
        pipelined(kv_ref, kr_ref, n_full, n_blocks, n_blocks - 1, mask_fn)
    else:
        update(scores(kv_ref[...], kr_ref[...]), kv_ref[...], None)

    o = acc_ref[...] / _lane_bcast(l_ref[...], KV_LORA)
    parts = []
    for h in range(N_HEADS):
        oh = o[h * tq:(h + 1) * tq, :].astype(BF16)
        parts.append(jnp.dot(oh, wuv_ref[h], preferred_element_type=F32))
    attn = jnp.concatenate(parts, axis=-1)
    out_ref[...] = _rms(attn, gao_ref[...]).astype(BF16)


def _attention(cqn, w_q, w_ukt, w_uv, cosq, sinq, gao, ckv, krope, *, n_batch, seq, row0,
               past_kv=None, past_kr=None):
    causal = past_kv is None
    tq = TQ if causal else seq
    nq = seq // tq
    n_past = 0 if causal else past_kv.shape[1]
    if not causal:
        assert n_past % CHUNK == 0 and seq <= CHUNK and n_past % TK == 0
    blk0 = row0 // tq
    qrow = lambda b, q: (blk0 + b * nq + q, 0)
    in_specs = [
        pl.BlockSpec((tq, Q_LORA), qrow),
        _const_spec(w_q.shape),
        _const_spec(w_ukt.shape),
        _const_spec(w_uv.shape),
        pl.BlockSpec((tq, N_HEADS * ROPE_DIM), lambda b, q: (q, 0)),
        pl.BlockSpec((tq, N_HEADS * ROPE_DIM), lambda b, q: (q, 0)),
        _const_spec((1, ATTN_W)),
    ]
    args = [cqn, w_q, w_ukt, w_uv, cosq, sinq, gao]
    if n_past:
        in_specs += [pl.BlockSpec((None, n_past, KV_LORA), lambda b, q: (b, 0, 0)),
                     pl.BlockSpec((None, ROPE_DIM, n_past), lambda b, q: (b, 0, 0))]
        args += [past_kv, past_kr]
    in_specs += [pl.BlockSpec((seq, KV_LORA), lambda b, q: (b, 0)),
                 pl.BlockSpec((None, ROPE_DIM, seq), lambda b, q: (b, 0, 0))]
    args += [ckv, krope]
    rows = N_HEADS * tq
    kern = functools.partial(_attn_kernel, tq=tq, n_past=n_past, causal=causal)
    return pl.pallas_call(
        kern,
        grid=(n_batch, nq),
        in_specs=in_specs,
        out_specs=pl.BlockSpec((tq, ATTN_W), lambda b, q: (b * nq + q, 0)),
        out_shape=jax.ShapeDtypeStruct((n_batch * seq, ATTN_W), BF16),
        scratch_shapes=[
            pltpu.VMEM((rows, KV_LORA), BF16),
            pltpu.VMEM((rows, ROPE_DIM), BF16),
            pltpu.VMEM((rows, LANES), F32),
            pltpu.VMEM((rows, LANES), F32),
            pltpu.VMEM((rows, KV_LORA), F32),
            pltpu.VMEM((2, rows, TK), F32),
            pltpu.VMEM((rows, LANES), jnp.int32),
        ],
        compiler_params=pltpu.CompilerParams(dimension_semantics=("arbitrary", "arbitrary"),
                                             vmem_limit_bytes=VMEM_LIMIT),
        name="attn_prompt" if causal else "attn_sample",
    )(*args)


def _out_proj_kernel(attnp_ref, attns_ref, convn_ref, xp_ref, xs_ref, wo_ref, gffn_ref, wr_ref,
                     br_ref, h_ref, xpk_ref, mi_ref, mf_ref, cnt_ref, carry_ref, *, n_prompt_tiles):
    i = pl.program_id(0)

    @pl.when(i == 0)
    def _():
        carry_ref[...] = jnp.zeros(carry_ref.shape, F32)

    def tile(x_ref, attn_ref):
        y = jnp.dot(attn_ref[...], wo_ref[:ATTN_W, :], preferred_element_type=F32)
        y = y + jnp.dot(convn_ref[...], wo_ref[ATTN_W:, :], preferred_element_type=F32)
        h = x_ref[...] + y
        h_ref[...] = h
        xn = _rms(h, gffn_ref[...])

        half = D_MODEL // 2
        xh = xn.astype(BF16)
        xh32 = xh.astype(F32)
        lo = lax.bitcast_convert_type(xh32[:, :half], jnp.uint32)
        hi = lax.bitcast_convert_type(xh32[:, half:], jnp.uint32)
        xpk_ref[...] = (lo >> 16) | (hi & jnp.uint32(0xFFFF0000))

        xl = (xn - xh32).astype(BF16)
        hh_hl = jnp.dot(xh, wr_ref[...], preferred_element_type=F32)
        lh = jnp.dot(xl, wr_ref[:, :LANES], preferred_element_type=F32)
        logits = hh_hl[:, :LANES] + (lh + hh_hl[:, LANES:]) + br_ref[...]
        lane = lax.broadcasted_iota(jnp.int32, (TM, LANES), 1).astype(F32)
        ninf = -jnp.inf
        far = float(LANES)

        def first_argmax(v):
            vmax = jnp.max(v, axis=-1, keepdims=True)
            return vmax, jnp.min(jnp.where(v == vmax, lane, far), axis=-1, keepdims=True)

        gl = jnp.where(lane < N_GROUPS, logits, ninf)
        gmax, gidx = first_argmax(gl)
        g_p = 1.0 / jnp.sum(jnp.exp(gl - gmax), axis=-1, keepdims=True)
        e_lo = N_GROUPS + EXPERTS_PER_GROUP * gidx
        el = jnp.where((lane >= e_lo) & (lane < e_lo + EXPERTS_PER_GROUP), logits, ninf)
        e1max, i1 = first_argmax(el)
        z = jnp.sum(jnp.exp(el - e1max), axis=-1, keepdims=True)
        el2 = jnp.where(lane == i1, ninf, el)
        e2max, i2 = first_argmax(el2)
        p1 = 1.0 / z
        p2 = jnp.exp(e2max - e1max) / z
        den = p1 + p2
        g0 = g_p * p1 / den
        g1 = g_p * p2 / den
        e0 = i1 - N_GROUPS
        e1 = i2 - N_GROUPS

        oh0 = lane == e0
        oh1 = lane == e1
        oh = jnp.where(oh0 | oh1, 1.0, 0.0)
        r = lax.broadcasted_iota(jnp.int32, (TM, TM), 0)
        c = lax.broadcasted_iota(jnp.int32, (TM, TM), 1)
        ltri = jnp.where(r > c, 1.0, 0.0).astype(BF16)
        before = jnp.dot(ltri, oh.astype(BF16), preferred_element_type=F32) + carry_ref[...]
        rank0 = jnp.sum(jnp.where(oh0, before, 0.0), axis=-1, keepdims=True)
        rank1 = jnp.sum(jnp.where(oh1, before, 0.0), axis=-1, keepdims=True)
        total = carry_ref[...] + jnp.sum(oh, axis=0, keepdims=True)
        carry_ref[...] = total
        cnt_ref[...] = jnp.broadcast_to(total, cnt_ref.shape)

        mi = jnp.where(lane == 0, e0, jnp.where(lane == 1, e1, jnp.where(lane == 2, rank0, rank1)))
        mi_ref[...] = jnp.transpose(mi)[:SUBLANES, :].astype(jnp.int32)
        mf_ref[...] = jnp.where(lane == 0, g0, g1)

    @pl.when(i < n_prompt_tiles)
    def _():
        tile(xp_ref, attnp_ref)

    @pl.when(i >= n_prompt_tiles)
    def _():
        tile(xs_ref, attns_ref)


def _out_proj(attn_p, attn_s, conv_n, xp, xs, w_ob, gffn, w_r2, b_r):
    m = conv_n.shape[0]
    npt = xp.shape[0] // TM
    last_p = npt - 1
    row = lambda i: (i, 0)
    return pl.pallas_call(
        functools.partial(_out_proj_kernel, n_prompt_tiles=npt),
        grid=(m // TM,),
        in_specs=[
            pl.BlockSpec((TM, ATTN_W), lambda i: (jnp.minimum(i, last_p), 0)),
            pl.BlockSpec((TM, ATTN_W), lambda i: (jnp.maximum(i - npt, 0), 0)),
            pl.BlockSpec((TM, CONV_CH), row),
            pl.BlockSpec((TM, D_MODEL), lambda i: (jnp.minimum(i, last_p), 0)),
            pl.BlockSpec((TM, D_MODEL), lambda i: (jnp.maximum(i - npt, 0), 0)),
            _const_spec(w_ob.shape),
            _const_spec((1, D_MODEL)),
            _const_spec(w_r2.shape),
            _const_spec((1, LANES)),
        ],
        out_specs=[
            pl.BlockSpec((TM, D_MODEL), row),
            pl.BlockSpec((TM, D_MODEL // 2), row),
            pl.BlockSpec((SUBLANES, TM), lambda i: (0, i)),
            pl.BlockSpec((TM, LANES), row),
            pl.BlockSpec((SUBLANES, LANES), lambda i: (0, 0)),
        ],
        out_shape=[
            jax.ShapeDtypeStruct((m, D_MODEL), F32),
            jax.ShapeDtypeStruct((m, D_MODEL // 2), jnp.uint32),
            jax.ShapeDtypeStruct((SUBLANES, m), jnp.int32),
            jax.ShapeDtypeStruct((m, LANES), F32),
            jax.ShapeDtypeStruct((SUBLANES, LANES), F32),
        ],
        scratch_shapes=[pltpu.VMEM((1, LANES), F32)],
        compiler_params=pltpu.CompilerParams(dimension_semantics=("arbitrary",),
                                             vmem_limit_bytes=VMEM_LIMIT),
        name="out_proj",
    )(attn_p, attn_s, conv_n, xp, xs, w_ob, gffn, w_r2, b_r)


def _dispatch_kernel(d0_ref, d1_ref, zlo_ref, zn_ref, nu_ref, xpk_ref, xs_hbm, zeros_ref, sems, *, n_blocks):
    i = pl.program_id(0)
    sem = sems.at[0]
    zsem = sems.at[1]

    def row_copy(src_ref, src_row, dst_row):
        return pltpu.make_async_copy(src_ref.at[pl.ds(src_row, 1)], xs_hbm.at[pl.ds(dst_row, 1)], sem)

    def zero_fill(act):
        def per_expert(e, c):
            lo = zlo_ref[e]
            n = zn_ref[e]
            head = (-lo) & (SUBLANES - 1)
            for r in range(SUBLANES - 1):
                @pl.when(r < head)
                def _(r=r):
                    act(pltpu.make_async_copy(zeros_ref.at[pl.ds(0, 1)], xs_hbm.at[pl.ds(lo + r, 1)], zsem))
            off = lo + head
            rest = n - head
            size = MOE_BLOCK // 2
            while size >= SUBLANES:
                @pl.when((rest & size) != 0)
                def _(off=off, size=size):
                    dst = xs_hbm.at[pl.ds(pl.multiple_of(off, SUBLANES), size)]
                    act(pltpu.make_async_copy(zeros_ref.at[pl.ds(0, size)], dst, zsem))
                off = off + (rest & size)
                size //= 2
            return c

        def per_block(b, c):
            dst = xs_hbm.at[pl.ds(pl.multiple_of(b * MOE_BLOCK, MOE_BLOCK), MOE_BLOCK)]
            act(pltpu.make_async_copy(zeros_ref, dst, zsem))
            return c

        lax.fori_loop(0, N_EXPERTS, per_expert, 0)
        lax.fori_loop(nu_ref[0], n_blocks, per_block, 0)

    @pl.when(i == 0)
    def _():
        zeros_ref[...] = jnp.zeros(zeros_ref.shape, zeros_ref.dtype)
        zero_fill(lambda cp: cp.start())
        zero_fill(lambda cp: cp.wait())

    base = i * TM

    def start(r, c):
        row_copy(xpk_ref, r, d0_ref[base + r]).start()
        row_copy(xpk_ref, r, d1_ref[base + r]).start()
        return c

    lax.fori_loop(0, TM, start, 0, unroll=DMA_ISSUE_UNROLL)
    for _ in range(2):
        pltpu.make_async_copy(xpk_ref, xs_hbm.at[pl.ds(0, TM)], sem).wait()


def _dispatch(dest0, dest1, pad_lo, n_pad, n_used, xpk, n_blocks):
    m = xpk.shape[0]
    grid_spec = pltpu.PrefetchScalarGridSpec(
        num_scalar_prefetch=5,
        grid=(m // TM,),
        in_specs=[pl.BlockSpec((TM, D_MODEL // 2), lambda i, *_: (i, 0))],
        out_specs=pl.BlockSpec(memory_space=pl.ANY),
        scratch_shapes=[pltpu.VMEM((MOE_BLOCK, D_MODEL // 2), jnp.uint32),
                        pltpu.SemaphoreType.DMA((2,))],
    )
    return pl.pallas_call(
        functools.partial(_dispatch_kernel, n_blocks=n_blocks),
        grid_spec=grid_spec,
        out_shape=jax.ShapeDtypeStruct((n_blocks * MOE_BLOCK, D_MODEL // 2), jnp.uint32),
        compiler_params=pltpu.CompilerParams(dimension_semantics=("arbitrary",)),
        name="dispatch",
    )(dest0, dest1, pad_lo, n_pad, n_used, xpk)


def _experts_kernel(be_ref, nu_ref, nxt_ref, x_ref, wg_hbm, wu_hbm, wd_hbm, y_ref,
                    sg_ref, su_ref, sd_ref, wgb_ref, wub_ref, wdb_ref, sems):
    b = pl.program_id(0)
    active = b < nu_ref[0]
    new_expert = jnp.logical_or(b == 0, be_ref[b] != be_ref[jnp.maximum(b - 1, 0)])

    def weight_copies(e):
        return (pltpu.make_async_copy(wg_hbm.at[e], sg_ref, sems.at[0]),
                pltpu.make_async_copy(wu_hbm.at[e], su_ref, sems.at[1]),
                pltpu.make_async_copy(wd_hbm.at[e], sd_ref, sems.at[2]))

    @pl.when(b == 0)
    def _():
        for cp in weight_copies(be_ref[0]):
            cp.start()

    @pl.when(jnp.logical_and(active, new_expert))
    def _():
        for cp in weight_copies(be_ref[b]):
            cp.wait()
        wgb_ref[...] = sg_ref[...].astype(BF16)
        wub_ref[...] = su_ref[...].astype(BF16)
        wdb_ref[...] = sd_ref[...].astype(BF16)

        @pl.when(nxt_ref[b] >= 0)
        def _():
            for cp in weight_copies(nxt_ref[b]):
                cp.start()

    @pl.when(active)
    def _():
        half = D_MODEL // 2
        xw = x_ref[...]
        xa = lax.bitcast_convert_type(xw << 16, F32).astype(BF16)
        xb = lax.bitcast_convert_type(xw & jnp.uint32(0xFFFF0000), F32).astype(BF16)
        g = jnp.dot(xa, wgb_ref[:half, :], preferred_element_type=F32)
        g = g + jnp.dot(xb, wgb_ref[half:, :], preferred_element_type=F32)
        u = jnp.dot(xa, wub_ref[:half, :], preferred_element_type=F32)
        u = u + jnp.dot(xb, wub_ref[half:, :], preferred_element_type=F32)
        hmid = (g * jax.nn.sigmoid(g)) * u
        y_ref[...] = jnp.dot(hmid.astype(BF16), wdb_ref[...], preferred_element_type=F32)

    @pl.when(b >= nu_ref[0])
    def _():
        y_ref[...] = jnp.zeros(y_ref.shape, y_ref.dtype)


def _experts(block_e, n_used, next_e, x_sorted, w_gate, w_up, w_down):
    p = x_sorted.shape[0]
    nb = p // MOE_BLOCK

    def xrow(b, be, nu, nxt):
        return (jnp.maximum(jnp.minimum(b, nu[0] - 1), 0), 0)

    grid_spec = pltpu.PrefetchScalarGridSpec(
        num_scalar_prefetch=3,
        grid=(nb,),
        in_specs=[
            pl.BlockSpec((MOE_BLOCK, D_MODEL // 2), xrow),
            pl.BlockSpec(memory_space=pl.ANY),
            pl.BlockSpec(memory_space=pl.ANY),
            pl.BlockSpec(memory_space=pl.ANY),
        ],
        out_specs=pl.BlockSpec((MOE_BLOCK, D_MODEL), lambda b, be, nu, nxt: (b, 0)),
        scratch_shapes=[pltpu.VMEM((D_MODEL, D_FF), F32), pltpu.VMEM((D_MODEL, D_FF), F32),
                        pltpu.VMEM((D_FF, D_MODEL), F32),
                        pltpu.VMEM((D_MODEL, D_FF), BF16), pltpu.VMEM((D_MODEL, D_FF), BF16),
                        pltpu.VMEM((D_FF, D_MODEL), BF16),
                        pltpu.SemaphoreType.DMA((3,))],
    )
    return pl.pallas_call(
        _experts_kernel,
        grid_spec=grid_spec,
        out_shape=jax.ShapeDtypeStruct((p, D_MODEL), F32),
        compiler_params=pltpu.CompilerParams(dimension_semantics=("arbitrary",),
                                             vmem_limit_bytes=VMEM_LIMIT),
        name="experts",
    )(block_e, n_used, next_e, x_sorted, w_gate, w_up, w_down)


def _combine_kernel(d0_ref, d1_ref, h_ref, mf_ref, gfin_ref, y_hbm, outp_ref, outs_ref, y0_ref, y1_ref, sems,
                    *, n_tiles, n_prompt_tiles):
    i = pl.program_id(0)

    def gather(tile, slot, act):
        base = tile * TM

        def body(r, c):
            act(pltpu.make_async_copy(y_hbm.at[pl.ds(d0_ref[base + r], 1)], y0_ref.at[slot, pl.ds(r, 1)],
                                      sems.at[slot]))
            act(pltpu.make_async_copy(y_hbm.at[pl.ds(d1_ref[base + r], 1)], y1_ref.at[slot, pl.ds(r, 1)],
                                      sems.at[slot]))
            return c
        lax.fori_loop(0, TM, body, 0, unroll=DMA_ISSUE_UNROLL)

    @pl.when(i == 0)
    def _():
        gather(0, 0, lambda cp: cp.start())

    @pl.when(i + 1 < n_tiles)
    def _():
        gather(i + 1, (i + 1) % 2, lambda cp: cp.start())

    slot = i % 2
    pltpu.make_async_copy(y_hbm.at[pl.ds(0, TM)], y0_ref.at[slot], sems.at[slot]).wait()
    pltpu.make_async_copy(y_hbm.at[pl.ds(0, TM)], y1_ref.at[slot], sems.at[slot]).wait()
    mf = mf_ref[...]
    ffn = mf[:, 0:1] * y0_ref[slot] + mf[:, 1:2] * y1_ref[slot]
    out = _rms(h_ref[...] + ffn, gfin_ref[...])

    @pl.when(i < n_prompt_tiles)
    def _():
        outp_ref[...] = out

    @pl.when(i >= n_prompt_tiles)
    def _():
        outs_ref[...] = out


def _combine(dest0, dest1, h, mf, gfin, y_sorted, *, n_prompt_rows):
    m = h.shape[0]
    npt = n_prompt_rows // TM
    grid_spec = pltpu.PrefetchScalarGridSpec(
        num_scalar_prefetch=2,
        grid=(m // TM,),
        in_specs=[
            pl.BlockSpec((TM, D_MODEL), lambda i, *_: (i, 0)),
            pl.BlockSpec((TM, LANES), lambda i, *_: (i, 0)),
            pl.BlockSpec((1, D_MODEL), lambda i, *_: (0, 0)),
            pl.BlockSpec(memory_space=pl.ANY),
        ],
        out_specs=[pl.BlockSpec((TM, D_MODEL), lambda i, *_: (jnp.minimum(i, npt - 1), 0)),
                   pl.BlockSpec((TM, D_MODEL), lambda i, *_: (jnp.maximum(i - npt, 0), 0))],
        scratch_shapes=[pltpu.VMEM((2, TM, D_MODEL), F32), pltpu.VMEM((2, TM, D_MODEL), F32),
                        pltpu.SemaphoreType.DMA((2,))],
    )
    return pl.pallas_call(
        functools.partial(_combine_kernel, n_tiles=m // TM, n_prompt_tiles=npt),
        grid_spec=grid_spec,
        out_shape=[jax.ShapeDtypeStruct((n_prompt_rows, D_MODEL), F32),
                   jax.ShapeDtypeStruct((m - n_prompt_rows, D_MODEL), F32)],
        compiler_params=pltpu.CompilerParams(dimension_semantics=("arbitrary",),
                                             vmem_limit_bytes=VMEM_LIMIT),
        name="combine",
    )(dest0, dest1, h, mf, gfin, y_sorted)


def _rope_tables(pos):
    f32 = np.float32
    inv = np.power(f32(ROPE_THETA), -np.arange(0, ROPE_DIM, 2, dtype=f32) / f32(ROPE_DIM)).astype(f32)
    ang = (pos.astype(f32)[:, None] * inv[None, :]).astype(f32)
    cos, sin = np.cos(ang).astype(f32), np.sin(ang).astype(f32)
    return np.concatenate([cos, cos], axis=-1), np.concatenate([-sin, sin], axis=-1)


def _swap_halves(w):
    return jnp.concatenate([w[..., ROPE_DIM // 2:], w[..., :ROPE_DIM // 2]], axis=-1)


def kernel(x_prompt, x_sample, cache_kv_latent, cache_k_rope, state_conv, norm_mix, w_in, norm_q, w_uq,
           norm_kv, w_uk, w_uv, conv_w, norm_attn_out, norm_conv_out, w_o, norm_ffn, w_router_group,
           b_router_group, w_router_expert, b_router_expert, w_gate, w_up, w_down, norm_final):
    assert w_in.shape[0] == 1, "single-layer trunk"
    bp, seq_p, _ = x_prompt.shape
    bs, seq_s, _ = x_sample.shape
    past_len = cache_kv_latent.shape[2]
    np_rows, ns_rows = bp * seq_p, bs * seq_s
    m = np_rows + ns_rows
    assert seq_p % TM == 0 and TM % seq_s == 0 and ns_rows % TM == 0 and seq_s == CHUNK

    xp = x_prompt.reshape(np_rows, D_MODEL)
    xs = x_sample.reshape(ns_rows, D_MODEL)
    row_vec = lambda v: v.reshape(1, -1)

    assert w_in.shape[2] == Q_LORA + KV_LORA + ROPE_DIM + 3 * CONV_CH
    w_t = jnp.swapaxes(w_in[0], 0, 1).astype(BF16)
    wq4 = w_uq[0].reshape(Q_LORA, N_HEADS, QK_NOPE + ROPE_DIM)
    wq_rope = wq4[:, :, QK_NOPE:]
    w_q = jnp.concatenate([wq4[:, :, :QK_NOPE].reshape(Q_LORA, -1), wq_rope.reshape(Q_LORA, -1),
                           _swap_halves(wq_rope).reshape(Q_LORA, -1)], axis=1).astype(BF16)
    w_ukt = jnp.transpose(w_uk[0], (1, 2, 0)).astype(BF16)
    w_uvh = jnp.transpose(w_uv[0], (1, 0, 2)).astype(BF16)
    w_ob = w_o[0].astype(BF16)
    n_router = N_GROUPS + N_EXPERTS
    w_r = jnp.concatenate([w_router_group[0], w_router_expert[0].reshape(D_MODEL, N_EXPERTS)], axis=1)
    w_r = jnp.pad(w_r, ((0, 0), (0, LANES - n_router)))
    w_rh = w_r.astype(BF16)
    w_rl = (w_r - w_rh.astype(F32)).astype(BF16)
    w_r2 = jnp.concatenate([w_rh, w_rl], axis=1)
    b_r =jnp.pad(jnp.concatenate([b_router_group[0], b_router_expert[0].reshape(N_EXPERTS)]),
                  (0, LANES - n_router)).reshape(1, LANES)

    cos_p, sin_p = _rope_tables(np.arange(seq_p))
    cos_s, sin_s = _rope_tables(past_len + np.arange(seq_s))
    cosk = np.concatenate([cos_p, np.tile(cos_s, (TM // seq_s, 1))], axis=0)
    sink = np.concatenate([sin_p, np.tile(sin_s, (TM // seq_s, 1))], axis=0)
    state = jnp.concatenate([jnp.zeros((bp, CONV_W - 1, CONV_CH), F32), state_conv[0]], axis=0)

    cqn, ckv_p, kr_p, ckv_s, kr_s, conv_n, utail = _in_proj(
        xp, xs, row_vec(norm_mix[0]), w_t, row_vec(norm_q[0]), row_vec(norm_kv[0]),
        row_vec(norm_conv_out[0]), conv_w[0], cosk, sink, state, seq_p=seq_p, seq_s=seq_s)

    gao = row_vec(norm_attn_out[0])
    attn_p = _attention(cqn, w_q, w_ukt, w_uvh, np.tile(cos_p, (1, N_HEADS)), np.tile(sin_p, (1, N_HEADS)),
                        gao, ckv_p, kr_p, n_batch=bp, seq=seq_p, row0=0)
    attn_s = _attention(cqn, w_q, w_ukt, w_uvh, np.tile(cos_s, (1, N_HEADS)), np.tile(sin_s, (1, N_HEADS)),
                        gao, ckv_s, kr_s, n_batch=bs, seq=seq_s, row0=np_rows,
                        past_kv=cache_kv_latent[0], past_kr=jnp.swapaxes(cache_k_rope[0], 1, 2))

    h, xpk, mi, mf, cnt = _out_proj(attn_p, attn_s, conv_n, xp, xs, w_ob, row_vec(norm_ffn[0]),
                                    w_r2, b_r)

    counts = cnt[0, :N_EXPERTS].astype(jnp.int32)
    padded = (counts + MOE_BLOCK - 1) // MOE_BLOCK * MOE_BLOCK
    pad_end = jnp.cumsum(padded)
    pad_start = pad_end - padded
    n_blocks = -(-(m * 2) // MOE_BLOCK) + N_EXPERTS
    block_row0 = jnp.arange(n_blocks, dtype=jnp.int32) * MOE_BLOCK
    block_e = jnp.minimum(jnp.sum((pad_end[None, :] <= block_row0[:, None]).astype(jnp.int32), axis=1),
                          N_EXPERTS - 1)
    n_used = (pad_end[-1:] // MOE_BLOCK).astype(jnp.int32)
    expert_ids = jnp.arange(N_EXPERTS, dtype=jnp.int32)[:, None]

    def seg_start(e):
        return jnp.sum(jnp.where(expert_ids == e[None, :], pad_start[:, None], 0), axis=0)

    dest0 = seg_start(mi[0]) + mi[2]
    dest1 = seg_start(mi[1]) + mi[3]

    x_sorted = _dispatch(dest0, dest1, pad_start + counts, padded - counts, n_used, xpk, n_blocks)
    later = (expert_ids.T > block_e[:, None]) & (padded > 0)[None, :]
    next_e = jnp.min(jnp.where(later, expert_ids.T, N_EXPERTS), axis=1)
    next_e = jnp.where(next_e == N_EXPERTS, -1, next_e).astype(jnp.int32)
    y_sorted = _experts(block_e, n_used, next_e, x_sorted, w_gate[0], w_up[0], w_down[0])
    gfin = row_vec(norm_final)
    y_p, y_s = _combine(dest0, dest1, h, mf, gfin, y_sorted, n_prompt_rows=np_rows)

    ut = utail.reshape(m // CHUNK, SUBLANES, CONV_CH)
    tails = ut[:, SUBLANES - (CONV_W - 1):, :]
    p_last = (jnp.arange(bp) + 1) * (seq_p // CHUNK) - 1
    s_last = np_rows // CHUNK + (jnp.arange(bs) + 1) * (seq_s // CHUNK) - 1
    return (y_p.reshape(bp, seq_p, D_MODEL),
            y_s.reshape(bs, seq_s, D_MODEL),
            ckv_p.reshape(1, bp, seq_p, KV_LORA),
            jnp.swapaxes(kr_p, 1, 2)[None],
            tails[p_last][None],
            ckv_s.reshape(1, bs, seq_s, KV_LORA),
            jnp.swapaxes(kr_s, 1, 2)[None],
            tails[s_last][None])
```

```python
import functools

import jax
import jax.numpy as jnp
import numpy as np
from jax import lax
from jax.experimental import pallas as pl
from jax.experimental.pallas import tpu as pltpu

F32 = jnp.float32
BF16 = jnp.bfloat16

D_MODEL = 2048
N_HEADS = 8
QK_NOPE = 128
ROPE_DIM = 64
V_DIM = 128
Q_LORA = 512
KV_LORA = 512
ATTN_W = N_HEADS * V_DIM
CONV_CH = D_MODEL - ATTN_W
CONV_W = 3
CHUNK = 64
N_GROUPS = 4
EXPERTS_PER_GROUP = 8
N_EXPERTS = N_GROUPS * EXPERTS_PER_GROUP
D_FF = 512
ROPE_THETA = 10000.0
EPS = 1e-6
ATTN_SCALE = (QK_NOPE + ROPE_DIM) ** -0.5
EXP2_SCALE = ATTN_SCALE * 1.4426950408889634

LANES = 128
SUBLANES = 8
TM = 256
MOE_BLOCK = 256
TQ = 256
TK = 256
DMA_ISSUE_UNROLL = 8
NEG_BIG = -1e30
VMEM_LIMIT = 56 * 1024 * 1024


def _rms(v, g):
    return v * lax.rsqrt(jnp.mean(v * v, axis=-1, keepdims=True) + EPS) * g


def _lane_bcast(v, width):
    if width % LANES == 0:
        return jnp.concatenate([v] * (width // LANES), axis=1)
    assert width < LANES
    return v[:, :width]


def _const_spec(shape):
    nd = len(shape)
    return pl.BlockSpec(shape, lambda *_: (0,) * nd, pipeline_mode=pl.Buffered(1))


def _in_proj_kernel(xp_ref, xs_ref, gmix_ref, wt_ref, gq_ref, gkv_ref, gco_ref, convw_ref,
                    cos_ref, sin_ref, state_ref,
                    cqn_ref, ckvp_ref, krp_ref, ckvs_ref, krs_ref, convn_ref, utail_ref, ext_ref,
                    *, n_prompt_tiles, tiles_per_seq, n_prompt_seq, sample_seq_len):
    i = pl.program_id(0)

    def conv_block(u_sub, gate_sub, row0, length):
        ext_ref[SUBLANES:SUBLANES + length, :] = u_sub
        um1 = ext_ref[SUBLANES - 1:SUBLANES - 1 + length, :]
        um2 = ext_ref[SUBLANES - 2:SUBLANES - 2 + length, :]
        cw = convw_ref[...]
        conv = cw[0:1] * um2 + cw[1:2] * um1 + cw[2:3] * u_sub
        convn_ref[row0:row0 + length, :] = _rms(gate_sub * conv, gco_ref[...]).astype(BF16)

    def tile(x_ref, is_prompt):
        ckv_ref, krt_ref = (ckvp_ref, krp_ref) if is_prompt else (ckvs_ref, krs_ref)
        x = x_ref[...]
        xn = _rms(x, gmix_ref[...]).astype(BF16)
        lat_w = Q_LORA + KV_LORA
        conv0 = lat_w + ROPE_DIM
        nt = (((1,), (1,)), ((), ()))
        za = lax.dot_general(xn, wt_ref[:lat_w, :], nt, preferred_element_type=F32)
        cqn_ref[...] = _rms(za[:, :Q_LORA], gq_ref[...]).astype(BF16)
        ckv_ref[...] = _rms(za[:, Q_LORA:], gkv_ref[...])
        zk = lax.dot_general(xn, wt_ref[lat_w:conv0, :], nt, preferred_element_type=F32)
        zk_swapped = jnp.concatenate([zk[:, ROPE_DIM // 2:], zk[:, :ROPE_DIM // 2]], axis=1)
        k_rope = zk * cos_ref[...] + zk_swapped * sin_ref[...]
        if is_prompt:
            krt_ref[...] = k_rope.T
        else:
            for k in range(TM // sample_seq_len):
                krt_ref[k] = k_rope[k * sample_seq_len:(k + 1) * sample_seq_len, :].T

        zc = lax.dot_general(xn, wt_ref[conv0:, :], nt, preferred_element_type=F32)
        gate_b = zc[:, :CONV_CH]
        u = zc[:, CONV_CH:2 * CONV_CH] * zc[:, 2 * CONV_CH:]
        for j in range(TM // CHUNK):
            utail_ref[j] = u[CHUNK * (j + 1) - SUBLANES:CHUNK * (j + 1), :]

        if is_prompt:
            first = (i % tiles_per_seq) == 0

            @pl.when(first)
            def _():
                ext_ref[SUBLANES - 2:SUBLANES, :] = state_ref[i // tiles_per_seq]

            @pl.when(jnp.logical_not(first))
            def _():
                ext_ref[SUBLANES - 2:SUBLANES, :] = ext_ref[TM + SUBLANES - 2:TM + SUBLANES, :]

            conv_block(u, gate_b, 0, TM)
        else:
            n_sub = TM // sample_seq_len
            seq0 = n_prompt_seq + (i - n_prompt_tiles) * n_sub
            for k in range(n_sub):
                ext_ref[SUBLANES - 2:SUBLANES, :] = state_ref[seq0 + k]
                lo = k * sample_seq_len
                conv_block(u[lo:lo + sample_seq_len], gate_b[lo:lo + sample_seq_len], lo, sample_seq_len)

    @pl.when(i < n_prompt_tiles)
    def _():
        tile(xp_ref, True)

    @pl.when(i >= n_prompt_tiles)
    def _():
        tile(xs_ref, False)


def _in_proj(xp, xs, gmix, w_t, gq, gkv, gco, convw, cosk, sink, state, *, seq_p, seq_s):
    np_rows, ns_rows = xp.shape[0], xs.shape[0]
    m = np_rows + ns_rows
    npt, nst = np_rows // TM, ns_rows // TM
    tps = seq_p // TM
    n_prompt_seq = np_rows // seq_p
    last_p = npt - 1

    def tab_idx(i):
        return (jnp.where(i < npt, i % tps, tps), 0)

    row = lambda i: (i, 0)
    prow = lambda i: (jnp.minimum(i, last_p), 0)
    srow = lambda i: (jnp.maximum(i - npt, 0), 0)
    kern = functools.partial(_in_proj_kernel, n_prompt_tiles=npt, tiles_per_seq=tps,
                             n_prompt_seq=n_prompt_seq, sample_seq_len=seq_s)
    return pl.pallas_call(
        kern,
        grid=(npt + nst,),
        in_specs=[
            pl.BlockSpec((TM, D_MODEL), prow),
            pl.BlockSpec((TM, D_MODEL), srow),
            _const_spec((1, D_MODEL)),
            _const_spec(w_t.shape),
            _const_spec((1, Q_LORA)),
            _const_spec((1, KV_LORA)),
            _const_spec((1, CONV_CH)),
            _const_spec((CONV_W, CONV_CH)),
            pl.BlockSpec((TM, ROPE_DIM), tab_idx),
            pl.BlockSpec((TM, ROPE_DIM), tab_idx),
            _const_spec(state.shape),
        ],
        out_specs=[
            pl.BlockSpec((TM, Q_LORA), row),
            pl.BlockSpec((TM, KV_LORA), prow),
            pl.BlockSpec((None, ROPE_DIM, TM), lambda i: (jnp.minimum(i, last_p) // tps, 0,
                                                          jnp.minimum(i, last_p) % tps)),
            pl.BlockSpec((TM, KV_LORA), srow),
            pl.BlockSpec((TM // seq_s, ROPE_DIM, seq_s), lambda i: (jnp.maximum(i - npt, 0), 0, 0)),
            pl.BlockSpec((TM, CONV_CH), row),
            pl.BlockSpec((TM // CHUNK, SUBLANES, CONV_CH), lambda i: (i, 0, 0)),
        ],
        out_shape=[
            jax.ShapeDtypeStruct((m, Q_LORA), BF16),
            jax.ShapeDtypeStruct((np_rows, KV_LORA), F32),
            jax.ShapeDtypeStruct((n_prompt_seq, ROPE_DIM, seq_p), F32),
            jax.ShapeDtypeStruct((ns_rows, KV_LORA), F32),
            jax.ShapeDtypeStruct((ns_rows // seq_s, ROPE_DIM, seq_s), F32),
            jax.ShapeDtypeStruct((m, CONV_CH), BF16),
            jax.ShapeDtypeStruct((m // CHUNK, SUBLANES, CONV_CH), F32),
        ],
        scratch_shapes=[pltpu.VMEM((TM + SUBLANES, CONV_CH), F32)],
        compiler_params=pltpu.CompilerParams(dimension_semantics=("arbitrary",),
                                             vmem_limit_bytes=VMEM_LIMIT),
        name="in_proj",
    )(xp, xs, gmix, w_t, gq, gkv, gco, convw, cosk, sink, state)


def _attn_kernel(*refs, tq, n_past, causal):
    refs = list(refs)
    cqn_ref, wq_ref, wuk_ref, wuv_ref, cos_ref, sin_ref, gao_ref = refs[:7]
    refs = refs[7:]
    if n_past:
        pkv_ref, pkr_ref = refs[:2]
        refs = refs[2:]
    kv_ref, kr_ref, out_ref, qlat_ref, qr_ref, m_ref, l_ref, acc_ref, s_ref, klim_ref = refs

    qi = pl.program_id(1)
    rows = N_HEADS * tq

    q = jnp.dot(cqn_ref[...], wq_ref[...], preferred_element_type=F32)
    nope_w = N_HEADS * QK_NOPE
    rope_w = N_HEADS * ROPE_DIM
    qrope = q[:, nope_w:nope_w + rope_w] * cos_ref[...] + q[:, nope_w + rope_w:] * sin_ref[...]
    for h in range(N_HEADS):
        qn = q[:, h * QK_NOPE:(h + 1) * QK_NOPE].astype(BF16)
        ql = jnp.dot(qn, wuk_ref[h], preferred_element_type=F32)
        qlat_ref[h * tq:(h + 1) * tq, :] = ql.astype(BF16)
        qr_ref[h * tq:(h + 1) * tq, :] = qrope[:, h * ROPE_DIM:(h + 1) * ROPE_DIM].astype(BF16)

    m_ref[...] = jnp.full(m_ref.shape, NEG_BIG, F32)
    l_ref[...] = jnp.zeros(l_ref.shape, F32)
    acc_ref[...] = jnp.zeros(acc_ref.shape, F32)

    nt = (((1,), (1,)), ((), ()))

    def scores(kc_f32, krt_f32):
        s = lax.dot_general(qlat_ref[...], kc_f32.astype(BF16), nt, preferred_element_type=F32)
        return s + jnp.dot(qr_ref[...], krt_f32.astype(BF16), preferred_element_type=F32)

    def update(s, kc_f32, mask):
        if mask is not None:
            s = jnp.where(mask, s, NEG_BIG)
        m_prev = m_ref[...]
        m_new = jnp.maximum(m_prev, jnp.max(s, axis=-1, keepdims=True))
        alpha = jnp.exp2((m_prev - m_new) * EXP2_SCALE)
        p = jnp.exp2((s - _lane_bcast(m_new, s.shape[1])) * EXP2_SCALE)
        l_ref[...] = alpha * l_ref[...] + jnp.sum(p, axis=-1, keepdims=True)
        pv = jnp.dot(p.astype(BF16), kc_f32.astype(BF16), preferred_element_type=F32)
        acc_ref[...] = _lane_bcast(alpha, KV_LORA) * acc_ref[...] + pv
        m_ref[...] = m_new

    def pipelined(kv, kr, lo, hi, last, mask_fn):
        def body(j, c):
            k0 = pl.multiple_of(j * TK, TK)
            k1 = pl.multiple_of(jnp.minimum(j + 1, last) * TK, TK)
            s_cur = s_ref[j % 2]
            s_ref[(j + 1) % 2] = scores(kv[pl.ds(k1, TK), :], kr[:, pl.ds(k1, TK)])
            update(s_cur, kv[pl.ds(k0, TK), :], None if mask_fn is None else mask_fn(k0))
            return c
        lax.fori_loop(lo, hi, body, 0)

    def pipelined_pairs(kv, kr, n_pairs, last):
        def body(i, c):
            k0 = pl.multiple_of(2 * i * TK, TK)
            k1 = pl.multiple_of((2 * i + 1) * TK, TK)
            k2 = pl.multiple_of(jnp.minimum(2 * i + 2, last) * TK, TK)
            s_ref[1] = scores(kv[pl.ds(k1, TK), :], kr[:, pl.ds(k1, TK)])
            update(s_ref[0], kv[pl.ds(k0, TK), :], None)
            s_ref[0] = scores(kv[pl.ds(k2, TK), :], kr[:, pl.ds(k2, TK)])
            update(s_ref[1], kv[pl.ds(k1, TK), :], None)
            return c
        lax.fori_loop(0, n_pairs, body, 0)

    if n_past:
        n_pb = n_past // TK
        s_ref[0] = scores(pkv_ref[pl.ds(0, TK), :], pkr_ref[:, pl.ds(0, TK)])
        pipelined_pairs(pkv_ref, pkr_ref, n_pb // 2, n_pb - 1)
        if n_pb % 2:
            pipelined(pkv_ref, pkr_ref, n_pb - 1, n_pb, n_pb - 1, None)

    if causal:
        n_blocks = ((qi + 1) * tq + TK - 1) // TK
        n_full = jnp.minimum((qi * tq // CHUNK + 1) * CHUNK // TK, n_blocks)

        assert tq & (tq - 1) == 0 and CHUNK & (CHUNK - 1) == 0
        r = lax.broadcasted_iota(jnp.int32, (rows, LANES), 0)
        q_pos = qi * tq + (r & (tq - 1))
        klim_ref[...] = (q_pos & ~(CHUNK - 1)) + CHUNK

        def mask_fn(k0):
            cidx = lax.broadcasted_iota(jnp.int32, (rows, TK), 1)
            return cidx < _lane_bcast(klim_ref[...] - k0, TK)

        s_ref[0] = scores(kv_ref[pl.ds(0, TK), :], kr_ref[:, pl.ds(0, TK)])
        pipelined_pairs(kv_ref, kr_ref, n_full // 2, n_blocks - 1)
        pipelined(kv_ref, kr_ref, n_full // 2 * 2, n_full, n_blocks - 1, None)
        pipelined(kv_ref, kr_ref, n_full, n_blocks, n_blocks - 1, mask_fn)
    else:
        update(scores(kv_ref[...], kr_ref[...]), kv_ref[...], None)

    o = acc_ref[...] / _lane_bcast(l_ref[...], KV_LORA)
    parts = []
    for h in range(N_HEADS):
        oh = o[h * tq:(h + 1) * tq, :].astype(BF16)
        parts.append(jnp.dot(oh, wuv_ref[h], preferred_element_type=F32))
    attn = jnp.concatenate(parts, axis=-1)
    out_ref[...] = _rms(attn, gao_ref[...]).astype(BF16)


def _attention(cqn, w_q, w_ukt, w_uv, cosq, sinq, gao, ckv, krope, *, n_batch, seq, row0,
               past_kv=None, past_kr=None):
    causal = past_kv is None
    tq = TQ if causal else seq
    nq = seq // tq
    n_past = 0 if causal else past_kv.shape[1]
    if not causal:
        assert n_past % CHUNK == 0 and seq <= CHUNK and n_past % TK == 0
    blk0 = row0 // tq
    qrow = lambda b, q: (blk0 + b * nq + q, 0)
    in_specs = [
        pl.BlockSpec((tq, Q_LORA), qrow),
        _const_spec(w_q.shape),
        _const_spec(w_ukt.shape),
        _const_spec(w_uv.shape),
        pl.BlockSpec((tq, N_HEADS * ROPE_DIM), lambda b, q: (q, 0)),
        pl.BlockSpec((tq, N_HEADS * ROPE_DIM), lambda b, q: (q, 0)),
        _const_spec((1, ATTN_W)),
    ]
    args = [cqn, w_q, w_ukt, w_uv, cosq, sinq, gao]
    if n_past:
        in_specs += [pl.BlockSpec((None, n_past, KV_LORA), lambda b, q: (b, 0, 0)),
                     pl.BlockSpec((None, ROPE_DIM, n_past), lambda b, q: (b, 0, 0))]
        args += [past_kv, past_kr]
    in_specs += [pl.BlockSpec((seq, KV_LORA), lambda b, q: (b, 0)),
                 pl.BlockSpec((None, ROPE_DIM, seq), lambda b, q: (b, 0, 0))]
    args += [ckv, krope]
    rows = N_HEADS * tq
    kern = functools.partial(_attn_kernel, tq=tq, n_past=n_past, causal=causal)
    return pl.pallas_call(
        kern,
        grid=(n_batch, nq),
        in_specs=in_specs,
        out_specs=pl.BlockSpec((tq, ATTN_W), lambda b, q: (b * nq + q, 0)),
        out_shape=jax.ShapeDtypeStruct((n_batch * seq, ATTN_W), BF16),
        scratch_shapes=[
            pltpu.VMEM((rows, KV_LORA), BF16),
            pltpu.VMEM((rows, ROPE_DIM), BF16),
            pltpu.VMEM((rows, LANES), F32),
            pltpu.VMEM((rows, LANES), F32),
            pltpu.VMEM((rows, KV_LORA), F32),
            pltpu.VMEM((2, rows, TK), F32),
            pltpu.VMEM((rows, LANES), jnp.int32),
        ],
        compiler_params=pltpu.CompilerParams(dimension_semantics=("arbitrary", "arbitrary"),
                                             vmem_limit_bytes=VMEM_LIMIT),
        name="attn_prompt" if causal else "attn_sample",
    )(*args)


def _out_proj_kernel(attnp_ref, attns_ref, convn_ref, xp_ref, xs_ref, wo_ref, gffn_ref, wr_ref,
                     br_ref, h_ref, xpk_ref, mi_ref, mf_ref, cnt_ref, carry_ref, *, n_prompt_tiles):
    i = pl.program_id(0)

    @pl.when(i == 0)
    def _():
        carry_ref[...] = jnp.zeros(carry_ref.shape, F32)

    def tile(x_ref, attn_ref):
        y = jnp.dot(attn_ref[...], wo_ref[:ATTN_W, :], preferred_element_type=F32)
        y = y + jnp.dot(convn_ref[...], wo_ref[ATTN_W:, :], preferred_element_type=F32)
        h = x_ref[...] + y
        h_ref[...] = h
        xn = _rms(h, gffn_ref[...])

        half = D_MODEL // 2
        xh = xn.astype(BF16)
        xh32 = xh.astype(F32)
        lo = lax.bitcast_convert_type(xh32[:, :half], jnp.uint32)
        hi = lax.bitcast_convert_type(xh32[:, half:], jnp.uint32)
        xpk_ref[...] = (lo >> 16) | (hi & jnp.uint32(0xFFFF0000))

        xl = (xn - xh32).astype(BF16)
        hh_hl = jnp.dot(xh, wr_ref[...], preferred_element_type=F32)
        lh = jnp.dot(xl, wr_ref[:, :LANES], preferred_element_type=F32)
        logits = hh_hl[:, :LANES] + (lh + hh_hl[:, LANES:]) + br_ref[...]
        lane = lax.broadcasted_iota(jnp.int32, (TM, LANES), 1).astype(F32)
        ninf = -jnp.inf
        far = float(LANES)

        def first_argmax(v):
            vmax = jnp.max(v, axis=-1, keepdims=True)
            return vmax, jnp.min(jnp.where(v == vmax, lane, far), axis=-1, keepdims=True)

        gl = jnp.where(lane < N_GROUPS, logits, ninf)
        gmax, gidx = first_argmax(gl)
        g_p = 1.0 / jnp.sum(jnp.exp(gl - gmax), axis=-1, keepdims=True)
        e_lo = N_GROUPS + EXPERTS_PER_GROUP * gidx
        el = jnp.where((lane >= e_lo) & (lane < e_lo + EXPERTS_PER_GROUP), logits, ninf)
        e1max, i1 = first_argmax(el)
        z = jnp.sum(jnp.exp(el - e1max), axis=-1, keepdims=True)
        el2 = jnp.where(lane == i1, ninf, el)
        e2max, i2 = first_argmax(el2)
        p1 = 1.0 / z
        p2 = jnp.exp(e2max - e1max) / z
        den = p1 + p2
        g0 = g_p * p1 / den
        g1 = g_p * p2 / den
        e0 = i1 - N_GROUPS
        e1 = i2 - N_GROUPS

        oh0 = lane == e0
        oh1 = lane == e1
        oh = jnp.where(oh0 | oh1, 1.0, 0.0)
        r = lax.broadcasted_iota(jnp.int32, (TM, TM), 0)
        c = lax.broadcasted_iota(jnp.int32, (TM, TM), 1)
        ltri = jnp.where(r > c, 1.0, 0.0).astype(BF16)
        before = jnp.dot(ltri, oh.astype(BF16), preferred_element_type=F32) + carry_ref[...]
        rank0 = jnp.sum(jnp.where(oh0, before, 0.0), axis=-1, keepdims=True)
        rank1 = jnp.sum(jnp.where(oh1, before, 0.0), axis=-1, keepdims=True)
        total = carry_ref[...] + jnp.sum(oh, axis=0, keepdims=True)
        carry_ref[...] = total
        cnt_ref[...] = jnp.broadcast_to(total, cnt_ref.shape)

        mi = jnp.where(lane == 0, e0, jnp.where(lane == 1, e1, jnp.where(lane == 2, rank0, rank1)))
        mi_ref[...] = jnp.transpose(mi)[:SUBLANES, :].astype(jnp.int32)
        mf_ref[...] = jnp.where(lane == 0, g0, g1)

    @pl.when(i < n_prompt_tiles)
    def _():
        tile(xp_ref, attnp_ref)

    @pl.when(i >= n_prompt_tiles)
    def _():
        tile(xs_ref, attns_ref)


def _out_proj(attn_p, attn_s, conv_n, xp, xs, w_ob, gffn, w_r2, b_r):
    m = conv_n.shape[0]
    npt = xp.shape[0] // TM
    last_p = npt - 1
    row = lambda i: (i, 0)
    return pl.pallas_call(
        functools.partial(_out_proj_kernel, n_prompt_tiles=npt),
        grid=(m // TM,),
        in_specs=[
            pl.BlockSpec((TM, ATTN_W), lambda i: (jnp.minimum(i, last_p), 0)),
            pl.BlockSpec((TM, ATTN_W), lambda i: (jnp.maximum(i - npt, 0), 0)),
            pl.BlockSpec((TM, CONV_CH), row),
            pl.BlockSpec((TM, D_MODEL), lambda i: (jnp.minimum(i, last_p), 0)),
            pl.BlockSpec((TM, D_MODEL), lambda i: (jnp.maximum(i - npt, 0), 0)),
            _const_spec(w_ob.shape),
            _const_spec((1, D_MODEL)),
            _const_spec(w_r2.shape),
            _const_spec((1, LANES)),
        ],
        out_specs=[
            pl.BlockSpec((TM, D_MODEL), row),
            pl.BlockSpec((TM, D_MODEL // 2), row),
            pl.BlockSpec((SUBLANES, TM), lambda i: (0, i)),
            pl.BlockSpec((TM, LANES), row),
            pl.BlockSpec((SUBLANES, LANES), lambda i: (0, 0)),
        ],
        out_shape=[
            jax.ShapeDtypeStruct((m, D_MODEL), F32),
            jax.ShapeDtypeStruct((m, D_MODEL // 2), jnp.uint32),
            jax.ShapeDtypeStruct((SUBLANES, m), jnp.int32),
            jax.ShapeDtypeStruct((m, LANES), F32),
            jax.ShapeDtypeStruct((SUBLANES, LANES), F32),
        ],
        scratch_shapes=[pltpu.VMEM((1, LANES), F32)],
        compiler_params=pltpu.CompilerParams(dimension_semantics=("arbitrary",),
                                             vmem_limit_bytes=VMEM_LIMIT),
        name="out_proj",
    )(attn_p, attn_s, conv_n, xp, xs, w_ob, gffn, w_r2, b_r)


def _dispatch_kernel(d0_ref, d1_ref, zlo_ref, zn_ref, nu_ref, xpk_ref, xs_hbm, zeros_ref, sems, *, n_blocks):
    i = pl.program_id(0)
    sem = sems.at[0]
    zsem = sems.at[1]

    def zero_fill(act):
        def per_expert(e, c):
            lo = zlo_ref[e]
            n = zn_ref[e]
            head = (-lo) & (SUBLANES - 1)
            for r in range(SUBLANES - 1):
                @pl.when(r < head)
                def _(r=r):
                    act(pltpu.make_async_copy(zeros_ref.at[pl.ds(0, 1)], xs_hbm.at[pl.ds(lo + r, 1)], zsem))
            off = lo + head
            rest = n - head
            size = MOE_BLOCK // 2
            while size >= SUBLANES:
                @pl.when((rest & size) != 0)
                def _(off=off, size=size):
                    dst = xs_hbm.at[pl.ds(pl.multiple_of(off, SUBLANES), size)]
                    act(pltpu.make_async_copy(zeros_ref.at[pl.ds(0, size)], dst, zsem))
                off = off + (rest & size)
                size //= 2
            return c

        def per_block(b, c):
            dst = xs_hbm.at[pl.ds(pl.multiple_of(b * MOE_BLOCK, MOE_BLOCK), MOE_BLOCK)]
            act(pltpu.make_async_copy(zeros_ref, dst, zsem))
            return c

        lax.fori_loop(0, N_EXPERTS, per_expert, 0)
        lax.fori_loop(nu_ref[0], n_blocks, per_block, 0)

    @pl.when(i == 0)
    def _():
        zeros_ref[...] = jnp.zeros(zeros_ref.shape, zeros_ref.dtype)
        zero_fill(lambda cp: cp.start())
        zero_fill(lambda cp: cp.wait())

    base = i * TM

    def start(g, c):
        for u in range(SUBLANES):
            r = base + g * SUBLANES + u
            src = xpk_ref.at[g, pl.ds(u, 1)]
            pltpu.make_async_copy(src, xs_hbm.at[pl.ds(d0_ref[r], 1)], sem).start()
            pltpu.make_async_copy(src, xs_hbm.at[pl.ds(d1_ref[r], 1)], sem).start()
        return c

    lax.fori_loop(0, TM // SUBLANES, start, 0)
    for _ in range(2):
        pltpu.make_async_copy(xs_hbm.at[pl.ds(0, TM)], xs_hbm.at[pl.ds(0, TM)], sem).wait()


def _dispatch(dest0, dest1, pad_lo, n_pad, n_used, xpk, n_blocks):
    m = xpk.shape[0]
    grid_spec = pltpu.PrefetchScalarGridSpec(
        num_scalar_prefetch=5,
        grid=(m // TM,),
        in_specs=[pl.BlockSpec((TM // SUBLANES, SUBLANES, D_MODEL // 2), lambda i, *_: (i, 0, 0))],
        out_specs=pl.BlockSpec(memory_space=pl.ANY),
        scratch_shapes=[pltpu.VMEM((MOE_BLOCK, D_MODEL // 2), jnp.uint32),
                        pltpu.SemaphoreType.DMA((2,))],
    )
    return pl.pallas_call(
        functools.partial(_dispatch_kernel, n_blocks=n_blocks),
        grid_spec=grid_spec,
        out_shape=jax.ShapeDtypeStruct((n_blocks * MOE_BLOCK, D_MODEL // 2), jnp.uint32),
        compiler_params=pltpu.CompilerParams(dimension_semantics=("arbitrary",)),
        name="dispatch",
    )(dest0, dest1, pad_lo, n_pad, n_used, xpk.reshape(m // SUBLANES, SUBLANES, D_MODEL // 2))


def _experts_kernel(be_ref, nu_ref, nxt_ref, x_ref, wg_hbm, wu_hbm, wd_hbm, y_ref,
                    sg_ref, su_ref, sd_ref, wgb_ref, wub_ref, wdb_ref, sems):
    b = pl.program_id(0)
    active = b < nu_ref[0]
    new_expert = jnp.logical_or(b == 0, be_ref[b] != be_ref[jnp.maximum(b - 1, 0)])

    def weight_copies(e):
        return (pltpu.make_async_copy(wg_hbm.at[e], sg_ref, sems.at[0]),
                pltpu.make_async_copy(wu_hbm.at[e], su_ref, sems.at[1]),
                pltpu.make_async_copy(wd_hbm.at[e], sd_ref, sems.at[2]))

    @pl.when(b == 0)
    def _():
        for cp in weight_copies(be_ref[0]):
            cp.start()

    @pl.when(jnp.logical_and(active, new_expert))
    def _():
        for cp in weight_copies(be_ref[b]):
            cp.wait()
        wgb_ref[...] = sg_ref[...].astype(BF16)
        wub_ref[...] = su_ref[...].astype(BF16)
        wdb_ref[...] = sd_ref[...].astype(BF16)

        @pl.when(nxt_ref[b] >= 0)
        def _():
            for cp in weight_copies(nxt_ref[b]):
                cp.start()

    @pl.when(active)
    def _():
        half = D_MODEL // 2
        xw = x_ref[...]
        xa = lax.bitcast_convert_type(xw << 16, F32).astype(BF16)
        xb = lax.bitcast_convert_type(xw & jnp.uint32(0xFFFF0000), F32).astype(BF16)
        g = jnp.dot(xa, wgb_ref[:half, :], preferred_element_type=F32)
        g = g + jnp.dot(xb, wgb_ref[half:, :], preferred_element_type=F32)
        u = jnp.dot(xa, wub_ref[:half, :], preferred_element_type=F32)
        u = u + jnp.dot(xb, wub_ref[half:, :], preferred_element_type=F32)
        hmid = (g * jax.nn.sigmoid(g)) * u
        y_ref[...] = jnp.dot(hmid.astype(BF16), wdb_ref[...], preferred_element_type=F32)

    @pl.when(b >= nu_ref[0])
    def _():
        y_ref[...] = jnp.zeros(y_ref.shape, y_ref.dtype)


def _experts(block_e, n_used, next_e, x_sorted, w_gate, w_up, w_down):
    p = x_sorted.shape[0]
    nb = p // MOE_BLOCK

    def xrow(b, be, nu, nxt):
        return (jnp.maximum(jnp.minimum(b, nu[0] - 1), 0), 0)

    grid_spec = pltpu.PrefetchScalarGridSpec(
        num_scalar_prefetch=3,
        grid=(nb,),
        in_specs=[
            pl.BlockSpec((MOE_BLOCK, D_MODEL // 2), xrow),
            pl.BlockSpec(memory_space=pl.ANY),
            pl.BlockSpec(memory_space=pl.ANY),
            pl.BlockSpec(memory_space=pl.ANY),
        ],
        out_specs=pl.BlockSpec((MOE_BLOCK, D_MODEL), lambda b, be, nu, nxt: (b, 0)),
        scratch_shapes=[pltpu.VMEM((D_MODEL, D_FF), F32), pltpu.VMEM((D_MODEL, D_FF), F32),
                        pltpu.VMEM((D_FF, D_MODEL), F32),
                        pltpu.VMEM((D_MODEL, D_FF), BF16), pltpu.VMEM((D_MODEL, D_FF), BF16),
                        pltpu.VMEM((D_FF, D_MODEL), BF16),
                        pltpu.SemaphoreType.DMA((3,))],
    )
    return pl.pallas_call(
        _experts_kernel,
        grid_spec=grid_spec,
        out_shape=jax.ShapeDtypeStruct((p, D_MODEL), F32),
        compiler_params=pltpu.CompilerParams(dimension_semantics=("arbitrary",),
                                             vmem_limit_bytes=VMEM_LIMIT),
        name="experts",
    )(block_e, n_used, next_e, x_sorted, w_gate, w_up, w_down)


def _combine_kernel(d0_ref, d1_ref, h_ref, mf_ref, gfin_ref, y_hbm, outp_ref, outs_ref, y0_ref, y1_ref, sems,
                    *, n_tiles, n_prompt_tiles):
    i = pl.program_id(0)

    def gather(tile, slot, act):
        base = tile * TM

        def body(g, c):
            for u in range(SUBLANES):
                r = base + g * SUBLANES + u
                act(pltpu.make_async_copy(y_hbm.at[pl.ds(d0_ref[r], 1)], y0_ref.at[slot, g, pl.ds(u, 1)],
                                          sems.at[slot]))
                act(pltpu.make_async_copy(y_hbm.at[pl.ds(d1_ref[r], 1)], y1_ref.at[slot, g, pl.ds(u, 1)],
                                          sems.at[slot]))
            return c
        lax.fori_loop(0, TM // SUBLANES, body, 0)

    @pl.when(i == 0)
    def _():
        gather(0, 0, lambda cp: cp.start())

    @pl.when(i + 1 < n_tiles)
    def _():
        gather(i + 1, (i + 1) % 2, lambda cp: cp.start())

    slot = i % 2
    for _ in range(2):
        pltpu.make_async_copy(y_hbm.at[pl.ds(0, TM)], y_hbm.at[pl.ds(0, TM)], sems.at[slot]).wait()
    def finish(out_ref):
        mf = mf_ref[...]
        ffn = mf[:, :, 0:1] * y0_ref[slot] + mf[:, :, 1:2] * y1_ref[slot]
        out_ref[...] = _rms(h_ref[...] + ffn, gfin_ref[...])

    @pl.when(i < n_prompt_tiles)
    def _():
        finish(outp_ref)

    @pl.when(i >= n_prompt_tiles)
    def _():
        finish(outs_ref)


def _combine(dest0, dest1, h, mf, gfin, y_sorted, *, n_prompt_rows):
    m = h.shape[0]
    npt = n_prompt_rows // TM
    tg = TM // SUBLANES
    grouped = lambda a: a.reshape(a.shape[0] // SUBLANES, SUBLANES, a.shape[1])
    grid_spec = pltpu.PrefetchScalarGridSpec(
        num_scalar_prefetch=2,
        grid=(m // TM,),
        in_specs=[
            pl.BlockSpec((tg, SUBLANES, D_MODEL), lambda i, *_: (i, 0, 0)),
            pl.BlockSpec((tg, SUBLANES, LANES), lambda i, *_: (i, 0, 0)),
            pl.BlockSpec((1, 1, D_MODEL), lambda i, *_: (0, 0, 0)),
            pl.BlockSpec(memory_space=pl.ANY),
        ],
        out_specs=[pl.BlockSpec((tg, SUBLANES, D_MODEL), lambda i, *_: (jnp.minimum(i, npt - 1), 0, 0)),
                   pl.BlockSpec((tg, SUBLANES, D_MODEL), lambda i, *_: (jnp.maximum(i - npt, 0), 0, 0))],
        scratch_shapes=[pltpu.VMEM((2, TM // SUBLANES, SUBLANES, D_MODEL), F32),
                        pltpu.VMEM((2, TM // SUBLANES, SUBLANES, D_MODEL), F32),
                        pltpu.SemaphoreType.DMA((2,))],
    )
    y_p, y_s = pl.pallas_call(
        functools.partial(_combine_kernel, n_tiles=m // TM, n_prompt_tiles=npt),
        grid_spec=grid_spec,
        out_shape=[jax.ShapeDtypeStruct((n_prompt_rows // SUBLANES, SUBLANES, D_MODEL), F32),
                   jax.ShapeDtypeStruct(((m - n_prompt_rows) // SUBLANES, SUBLANES, D_MODEL), F32)],
        compiler_params=pltpu.CompilerParams(dimension_semantics=("arbitrary",),
                                             vmem_limit_bytes=VMEM_LIMIT),
        name="combine",
    )(dest0, dest1, grouped(h), grouped(mf), gfin.reshape(1, 1, D_MODEL), y_sorted)
    return y_p.reshape(n_prompt_rows, D_MODEL), y_s.reshape(m - n_prompt_rows, D_MODEL)


def _rope_tables(pos):
    f32 = np.float32
    inv = np.power(f32(ROPE_THETA), -np.arange(0, ROPE_DIM, 2, dtype=f32) / f32(ROPE_DIM)).astype(f32)
    ang = (pos.astype(f32)[:, None] * inv[None, :]).astype(f32)
    cos, sin = np.cos(ang).astype(f32), np.sin(ang).astype(f32)
    return np.concatenate([cos, cos], axis=-1), np.concatenate([-sin, sin], axis=-1)


def _swap_halves(w):
    return jnp.concatenate([w[..., ROPE_DIM // 2:], w[..., :ROPE_DIM // 2]], axis=-1)


def kernel(x_prompt, x_sample, cache_kv_latent, cache_k_rope, state_conv, norm_mix, w_in, norm_q, w_uq,
           norm_kv, w_uk, w_uv, conv_w, norm_attn_out, norm_conv_out, w_o, norm_ffn, w_router_group,
           b_router_group, w_router_expert, b_router_expert, w_gate, w_up, w_down, norm_final):
    assert w_in.shape[0] == 1, "single-layer trunk"
    bp, seq_p, _ = x_prompt.shape
    bs, seq_s, _ = x_sample.shape
    past_len = cache_kv_latent.shape[2]
    np_rows, ns_rows = bp * seq_p, bs * seq_s
    m = np_rows + ns_rows
    assert seq_p % TM == 0 and TM % seq_s == 0 and ns_rows % TM == 0 and seq_s == CHUNK

    xp = x_prompt.reshape(np_rows, D_MODEL)
    xs = x_sample.reshape(ns_rows, D_MODEL)
    row_vec = lambda v: v.reshape(1, -1)

    assert w_in.shape[2] == Q_LORA + KV_LORA + ROPE_DIM + 3 * CONV_CH
    w_t = jnp.swapaxes(w_in[0], 0, 1).astype(BF16)
    wq4 = w_uq[0].reshape(Q_LORA, N_HEADS, QK_NOPE + ROPE_DIM)
    wq_rope = wq4[:, :, QK_NOPE:]
    w_q = jnp.concatenate([wq4[:, :, :QK_NOPE].reshape(Q_LORA, -1), wq_rope.reshape(Q_LORA, -1),
                           _swap_halves(wq_rope).reshape(Q_LORA, -1)], axis=1).astype(BF16)
    w_ukt = jnp.transpose(w_uk[0], (1, 2, 0)).astype(BF16)
    w_uvh = jnp.transpose(w_uv[0], (1, 0, 2)).astype(BF16)
    w_ob = w_o[0].astype(BF16)
    n_router = N_GROUPS + N_EXPERTS
    w_r = jnp.concatenate([w_router_group[0], w_router_expert[0].reshape(D_MODEL, N_EXPERTS)], axis=1)
    w_r = jnp.pad(w_r, ((0, 0), (0, LANES - n_router)))
    w_rh = w_r.astype(BF16)
    w_rl = (w_r - w_rh.astype(F32)).astype(BF16)
    w_r2 = jnp.concatenate([w_rh, w_rl], axis=1)
    b_r =jnp.pad(jnp.concatenate([b_router_group[0], b_router_expert[0].reshape(N_EXPERTS)]),
                  (0, LANES - n_router)).reshape(1, LANES)

    cos_p, sin_p = _rope_tables(np.arange(seq_p))
    cos_s, sin_s = _rope_tables(past_len + np.arange(seq_s))
    cosk = np.concatenate([cos_p, np.tile(cos_s, (TM // seq_s, 1))], axis=0)
    sink = np.concatenate([sin_p, np.tile(sin_s, (TM // seq_s, 1))], axis=0)
    state = jnp.concatenate([jnp.zeros((bp, CONV_W - 1, CONV_CH), F32), state_conv[0]], axis=0)

    cqn, ckv_p, kr_p, ckv_s, kr_s, conv_n, utail = _in_proj(
        xp, xs, row_vec(norm_mix[0]), w_t, row_vec(norm_q[0]), row_vec(norm_kv[0]),
        row_vec(norm_conv_out[0]), conv_w[0], cosk, sink, state, seq_p=seq_p, seq_s=seq_s)

    gao = row_vec(norm_attn_out[0])
    attn_p = _attention(cqn, w_q, w_ukt, w_uvh, np.tile(cos_p, (1, N_HEADS)), np.tile(sin_p, (1, N_HEADS)),
                        gao, ckv_p, kr_p, n_batch=bp, seq=seq_p, row0=0)
    attn_s = _attention(cqn, w_q, w_ukt, w_uvh, np.tile(cos_s, (1, N_HEADS)), np.tile(sin_s, (1, N_HEADS)),
                        gao, ckv_s, kr_s, n_batch=bs, seq=seq_s, row0=np_rows,
                        past_kv=cache_kv_latent[0], past_kr=jnp.swapaxes(cache_k_rope[0], 1, 2))

    h, xpk, mi, mf, cnt = _out_proj(attn_p, attn_s, conv_n, xp, xs, w_ob, row_vec(norm_ffn[0]),
                                    w_r2, b_r)

    counts = cnt[0, :N_EXPERTS].astype(jnp.int32)
    padded = (counts + MOE_BLOCK - 1) // MOE_BLOCK * MOE_BLOCK
    pad_end = jnp.cumsum(padded)
    pad_start = pad_end - padded
    n_blocks = -(-(m * 2) // MOE_BLOCK) + N_EXPERTS
    block_row0 = jnp.arange(n_blocks, dtype=jnp.int32) * MOE_BLOCK
    block_e = jnp.minimum(jnp.sum((pad_end[None, :] <= block_row0[:, None]).astype(jnp.int32), axis=1),
                          N_EXPERTS - 1)
    n_used = (pad_end[-1:] // MOE_BLOCK).astype(jnp.int32)
    expert_ids = jnp.arange(N_EXPERTS, dtype=jnp.int32)[:, None]

    def seg_start(e):
        return jnp.sum(jnp.where(expert_ids == e[None, :], pad_start[:, None], 0), axis=0)

    dest0 = seg_start(mi[0]) + mi[2]
    dest1 = seg_start(mi[1]) + mi[3]

    x_sorted = _dispatch(dest0, dest1, pad_start + counts, padded - counts, n_used, xpk, n_blocks)
    later = (expert_ids.T > block_e[:, None]) & (padded > 0)[None, :]
    next_e = jnp.min(jnp.where(later, expert_ids.T, N_EXPERTS), axis=1)
    next_e = jnp.where(next_e == N_EXPERTS, -1, next_e).astype(jnp.int32)
    y_sorted = _experts(block_e, n_used, next_e, x_sorted, w_gate[0], w_up[0], w_down[0])
    gfin = row_vec(norm_final)
    y_p, y_s = _combine(dest0, dest1, h, mf, gfin, y_sorted, n_prompt_rows=np_rows)

    ut = utail.reshape(m // CHUNK, SUBLANES, CONV_CH)
    tails = ut[:, SUBLANES - (CONV_W - 1):, :]
    p_last = (jnp.arange(bp) + 1) * (seq_p // CHUNK) - 1
    s_last = np_rows // CHUNK + (jnp.arange(bs) + 1) * (seq_s // CHUNK) - 1
    return (y_p.reshape(bp, seq_p, D_MODEL),
            y_s.reshape(bs, seq_s, D_MODEL),
            ckv_p.reshape(1, bp, seq_p, KV_LORA),
            jnp.swapaxes(kr_p, 1, 2)[None],
            tails[p_last][None],
            ckv_s.reshape(1, bs, seq_s, KV_LORA),
            jnp.swapaxes(kr_s, 1, 2)[None],
            tails[s_last][None])
```

```python
import functools

import jax
import jax.numpy as jnp
import numpy as np
from jax import lax
from jax.experimental import pallas as pl
from jax.experimental.pallas import tpu as pltpu

F32 = jnp.float32
BF16 = jnp.bfloat16

D_MODEL = 2048
N_HEADS = 8
QK_NOPE = 128
ROPE_DIM = 64
V_DIM = 128
Q_LORA = 512
KV_LORA = 512
ATTN_W = N_HEADS * V_DIM
CONV_CH = D_MODEL - ATTN_W
CONV_W = 3
CHUNK = 64
N_GROUPS = 4
EXPERTS_PER_GROUP = 8
N_EXPERTS = N_GROUPS * EXPERTS_PER_GROUP
D_FF = 512
ROPE_THETA = 10000.0
EPS = 1e-6
ATTN_SCALE = (QK_NOPE + ROPE_DIM) ** -0.5
EXP2_SCALE = ATTN_SCALE * 1.4426950408889634

LANES = 128
SUBLANES = 8
TM = 256
MOE_BLOCK = 256
TQ = 256
TK = 256
CAST_CHUNK_ELEMS = 32 * SUBLANES * LANES
NEG_BIG = -1e30
VMEM_LIMIT = 56 * 1024 * 1024


def _rms(v, g):
    return v * lax.rsqrt(jnp.mean(v * v, axis=-1, keepdims=True) + EPS) * g


def _lane_bcast(v, width):
    if width % LANES == 0:
        return jnp.concatenate([v] * (width // LANES), axis=1)
    assert width < LANES
    return v[:, :width]


def _const_spec(shape):
    nd = len(shape)
    return pl.BlockSpec(shape, lambda *_: (0,) * nd, pipeline_mode=pl.Buffered(1))


def _in_proj_kernel(xp_ref, xs_ref, gmix_ref, wt_ref, gq_ref, gkv_ref, gco_ref, convw_ref,
                    cos_ref, sin_ref, state_ref,
                    cqn_ref, ckvp_ref, krp_ref, ckvs_ref, krs_ref, convn_ref, utail_ref, ext_ref,
                    *, n_prompt_tiles, tiles_per_seq, n_prompt_seq, sample_seq_len):
    i = pl.program_id(0)

    def conv_block(u_sub, gate_sub, row0, length):
        ext_ref[SUBLANES:SUBLANES + length, :] = u_sub
        um1 = ext_ref[SUBLANES - 1:SUBLANES - 1 + length, :]
        um2 = ext_ref[SUBLANES - 2:SUBLANES - 2 + length, :]
        cw = convw_ref[...]
        conv = cw[0:1] * um2 + cw[1:2] * um1 + cw[2:3] * u_sub
        convn_ref[row0:row0 + length, :] = _rms(gate_sub * conv, gco_ref[...]).astype(BF16)

    def tile(x_ref, is_prompt):
        ckv_ref, krt_ref = (ckvp_ref, krp_ref) if is_prompt else (ckvs_ref, krs_ref)
        x = x_ref[...]
        xn = _rms(x, gmix_ref[...]).astype(BF16)
        lat_w = Q_LORA + KV_LORA
        conv0 = lat_w + ROPE_DIM
        nt = (((1,), (1,)), ((), ()))
        za = lax.dot_general(xn, wt_ref[:lat_w, :], nt, preferred_element_type=F32)
        cqn_ref[...] = _rms(za[:, :Q_LORA], gq_ref[...]).astype(BF16)
        ckv_ref[...] = _rms(za[:, Q_LORA:], gkv_ref[...])
        zk = lax.dot_general(xn, wt_ref[lat_w:conv0, :], nt, preferred_element_type=F32)
        zk_swapped = jnp.concatenate([zk[:, ROPE_DIM // 2:], zk[:, :ROPE_DIM // 2]], axis=1)
        k_rope = zk * cos_ref[...] + zk_swapped * sin_ref[...]
        if is_prompt:
            krt_ref[...] = k_rope.T
        else:
            for k in range(TM // sample_seq_len):
                krt_ref[k] = k_rope[k * sample_seq_len:(k + 1) * sample_seq_len, :].T

        zc = lax.dot_general(xn, wt_ref[conv0:, :], nt, preferred_element_type=F32)
        gate_b = zc[:, :CONV_CH]
        u = zc[:, CONV_CH:2 * CONV_CH] * zc[:, 2 * CONV_CH:]
        for j in range(TM // CHUNK):
            utail_ref[j] = u[CHUNK * (j + 1) - SUBLANES:CHUNK * (j + 1), :]

        if is_prompt:
            first = (i % tiles_per_seq) == 0

            @pl.when(first)
            def _():
                ext_ref[SUBLANES - 2:SUBLANES, :] = state_ref[i // tiles_per_seq]

            @pl.when(jnp.logical_not(first))
            def _():
                ext_ref[SUBLANES - 2:SUBLANES, :] = ext_ref[TM + SUBLANES - 2:TM + SUBLANES, :]

            conv_block(u, gate_b, 0, TM)
        else:
            n_sub = TM // sample_seq_len
            seq0 = n_prompt_seq + (i - n_prompt_tiles) * n_sub
            for k in range(n_sub):
                ext_ref[SUBLANES - 2:SUBLANES, :] = state_ref[seq0 + k]
                lo = k * sample_seq_len
                conv_block(u[lo:lo + sample_seq_len], gate_b[lo:lo + sample_seq_len], lo, sample_seq_len)

    @pl.when(i < n_prompt_tiles)
    def _():
        tile(xp_ref, True)

    @pl.when(i >= n_prompt_tiles)
    def _():
        tile(xs_ref, False)


def _in_proj(xp, xs, gmix, w_t, gq, gkv, gco, convw, cosk, sink, state, *, seq_p, seq_s):
    np_rows, ns_rows = xp.shape[0], xs.shape[0]
    m = np_rows + ns_rows
    npt, nst = np_rows // TM, ns_rows // TM
    tps = seq_p // TM
    n_prompt_seq = np_rows // seq_p
    last_p = npt - 1

    def tab_idx(i):
        return (jnp.where(i < npt, i % tps, tps), 0)

    row = lambda i: (i, 0)
    prow = lambda i: (jnp.minimum(i, last_p), 0)
    srow = lambda i: (jnp.maximum(i - npt, 0), 0)
    kern = functools.partial(_in_proj_kernel, n_prompt_tiles=npt, tiles_per_seq=tps,
                             n_prompt_seq=n_prompt_seq, sample_seq_len=seq_s)
    return pl.pallas_call(
        kern,
        grid=(npt + nst,),
        in_specs=[
            pl.BlockSpec((TM, D_MODEL), prow),
            pl.BlockSpec((TM, D_MODEL), srow),
            _const_spec((1, D_MODEL)),
            _const_spec(w_t.shape),
            _const_spec((1, Q_LORA)),
            _const_spec((1, KV_LORA)),
            _const_spec((1, CONV_CH)),
            _const_spec((CONV_W, CONV_CH)),
            pl.BlockSpec((TM, ROPE_DIM), tab_idx),
            pl.BlockSpec((TM, ROPE_DIM), tab_idx),
            _const_spec(state.shape),
        ],
        out_specs=[
            pl.BlockSpec((TM, Q_LORA), row),
            pl.BlockSpec((TM, KV_LORA), prow),
            pl.BlockSpec((None, ROPE_DIM, TM), lambda i: (jnp.minimum(i, last_p) // tps, 0,
                                                          jnp.minimum(i, last_p) % tps)),
            pl.BlockSpec((TM, KV_LORA), srow),
            pl.BlockSpec((TM // seq_s, ROPE_DIM, seq_s), lambda i: (jnp.maximum(i - npt, 0), 0, 0)),
            pl.BlockSpec((TM, CONV_CH), row),
            pl.BlockSpec((TM // CHUNK, SUBLANES, CONV_CH), lambda i: (i, 0, 0)),
        ],
        out_shape=[
            jax.ShapeDtypeStruct((m, Q_LORA), BF16),
            jax.ShapeDtypeStruct((np_rows, KV_LORA), F32),
            jax.ShapeDtypeStruct((n_prompt_seq, ROPE_DIM, seq_p), F32),
            jax.ShapeDtypeStruct((ns_rows, KV_LORA), F32),
            jax.ShapeDtypeStruct((ns_rows // seq_s, ROPE_DIM, seq_s), F32),
            jax.ShapeDtypeStruct((m, CONV_CH), BF16),
            jax.ShapeDtypeStruct((m // CHUNK, SUBLANES, CONV_CH), F32),
        ],
        scratch_shapes=[pltpu.VMEM((TM + SUBLANES, CONV_CH), F32)],
        compiler_params=pltpu.CompilerParams(dimension_semantics=("arbitrary",),
                                             vmem_limit_bytes=VMEM_LIMIT),
        name="in_proj",
    )(xp, xs, gmix, w_t, gq, gkv, gco, convw, cosk, sink, state)


def _attn_kernel(*refs, tq, n_past, causal):
    refs = list(refs)
    cqn_ref, wq_ref, wuk_ref, wuv_ref, cos_ref, sin_ref, gao_ref = refs[:7]
    refs = refs[7:]
    if n_past:
        pkv_ref, pkr_ref = refs[:2]
        refs = refs[2:]
    kv_ref, kr_ref, out_ref, qlat_ref, qr_ref, m_ref, l_ref, acc_ref, s_ref, klim_ref = refs

    qi = pl.program_id(1)
    rows = N_HEADS * tq

    q = jnp.dot(cqn_ref[...], wq_ref[...], preferred_element_type=F32)
    nope_w = N_HEADS * QK_NOPE
    rope_w = N_HEADS * ROPE_DIM
    qrope = q[:, nope_w:nope_w + rope_w] * cos_ref[...] + q[:, nope_w + rope_w:] * sin_ref[...]
    for h in range(N_HEADS):
        qn = q[:, h * QK_NOPE:(h + 1) * QK_NOPE].astype(BF16)
        ql = jnp.dot(qn, wuk_ref[h], preferred_element_type=F32)
        qlat_ref[h * tq:(h + 1) * tq, :] = ql.astype(BF16)
        qr_ref[h * tq:(h + 1) * tq, :] = qrope[:, h * ROPE_DIM:(h + 1) * ROPE_DIM].astype(BF16)


    nt = (((1,), (1,)), ((), ()))

    def scores(kc_f32, krt_f32):
        s = lax.dot_general(qlat_ref[...], kc_f32.astype(BF16), nt, preferred_element_type=F32)
        return s + jnp.dot(qr_ref[...], krt_f32.astype(BF16), preferred_element_type=F32)

    def update(s, kc_f32, mask, first=False):
        if mask is not None:
            s = jnp.where(mask, s, NEG_BIG)
        m_cur = jnp.max(s, axis=-1, keepdims=True)
        if first:
            m_new = jnp.broadcast_to(m_cur, m_ref.shape)
        else:
            m_prev = m_ref[...]
            m_new = jnp.maximum(m_prev, m_cur)
            alpha = jnp.exp2((m_prev - m_new) * EXP2_SCALE)
        p = jnp.exp2((s - _lane_bcast(m_new, s.shape[1])) * EXP2_SCALE)
        l_cur = jnp.sum(p, axis=-1, keepdims=True)
        pv = jnp.dot(p.astype(BF16), kc_f32.astype(BF16), preferred_element_type=F32)
        if first:
            l_ref[...] = jnp.broadcast_to(l_cur, l_ref.shape)
            acc_ref[...] = pv
        else:
            l_ref[...] = alpha * l_ref[...] + l_cur
            acc_ref[...] = _lane_bcast(alpha, KV_LORA) * acc_ref[...] + pv
        m_ref[...] = m_new

    def pipelined(kv, kr, lo, hi, last, mask_fn):
        def body(j, c):
            k0 = pl.multiple_of(j * TK, TK)
            k1 = pl.multiple_of(jnp.minimum(j + 1, last) * TK, TK)
            s_cur = s_ref[j % 2]
            s_ref[(j + 1) % 2] = scores(kv[pl.ds(k1, TK), :], kr[:, pl.ds(k1, TK)])
            update(s_cur, kv[pl.ds(k0, TK), :], None if mask_fn is None else mask_fn(k0))
            return c
        lax.fori_loop(lo, hi, body, 0)

    def pipelined_pairs(kv, kr, n_pairs, last):
        def body(i, c):
            ka = pl.multiple_of((2 * i + 1) * TK, TK)
            kb = pl.multiple_of((2 * i + 2) * TK, TK)
            kc = pl.multiple_of(jnp.minimum(2 * i + 3, last) * TK, TK)
            s_ref[0] = scores(kv[pl.ds(kb, TK), :], kr[:, pl.ds(kb, TK)])
            update(s_ref[1], kv[pl.ds(ka, TK), :], None)
            s_ref[1] = scores(kv[pl.ds(kc, TK), :], kr[:, pl.ds(kc, TK)])
            update(s_ref[0], kv[pl.ds(kb, TK), :], None)
            return c
        lax.fori_loop(0, n_pairs, body, 0)

    def first_block(kv, kr, last, mask):
        k1 = pl.multiple_of(jnp.minimum(1, last) * TK, TK)
        s_ref[0] = scores(kv[pl.ds(0, TK), :], kr[:, pl.ds(0, TK)])
        s_ref[1] = scores(kv[pl.ds(k1, TK), :], kr[:, pl.ds(k1, TK)])
        update(s_ref[0], kv[pl.ds(0, TK), :], mask, first=True)

    if n_past:
        n_pb = n_past // TK
        first_block(pkv_ref, pkr_ref, n_pb - 1, None)
        n_pairs = (n_pb - 1) // 2
        pipelined_pairs(pkv_ref, pkr_ref, n_pairs, n_pb - 1)
        if 1 + 2 * n_pairs < n_pb:
            pipelined(pkv_ref, pkr_ref, 1 + 2 * n_pairs, n_pb, n_pb - 1, None)

    if causal:
        n_blocks = ((qi + 1) * tq + TK - 1) // TK
        n_full = jnp.minimum((qi * tq // CHUNK + 1) * CHUNK // TK, n_blocks)

        assert tq & (tq - 1) == 0 and CHUNK & (CHUNK - 1) == 0
        r = lax.broadcasted_iota(jnp.int32, (rows, LANES), 0)
        q_pos = qi * tq + (r & (tq - 1))
        klim_ref[...] = (q_pos & ~(CHUNK - 1)) + CHUNK

        def mask_fn(k0):
            cidx = lax.broadcasted_iota(jnp.int32, (rows, TK), 1)
            return cidx < _lane_bcast(klim_ref[...] - k0, TK)

        first_block(kv_ref, kr_ref, n_blocks - 1, mask_fn(0))
        n_pairs = jnp.maximum(n_full - 1, 0) // 2
        pipelined_pairs(kv_ref, kr_ref, n_pairs, n_blocks - 1)
        pipelined(kv_ref, kr_ref, 1 + 2 * n_pairs, n_full, n_blocks - 1, None)
        pipelined(kv_ref, kr_ref, jnp.maximum(n_full, 1), n_blocks, n_blocks - 1, mask_fn)
    else:
        update(scores(kv_ref[...], kr_ref[...]), kv_ref[...], None)

    o = acc_ref[...] / _lane_bcast(l_ref[...], KV_LORA)
    parts = []
    for h in range(N_HEADS):
        oh = o[h * tq:(h + 1) * tq, :].astype(BF16)
        parts.append(jnp.dot(oh, wuv_ref[h], preferred_element_type=F32))
    attn = jnp.concatenate(parts, axis=-1)
    out_ref[...] = _rms(attn, gao_ref[...]).astype(BF16)


def _attention(cqn, w_q, w_ukt, w_uv, cosq, sinq, gao, ckv, krope, *, n_batch, seq, row0,
               past_kv=None, past_kr=None):
    causal = past_kv is None
    tq = TQ if causal else seq
    nq = seq // tq
    n_past = 0 if causal else past_kv.shape[1]
    if not causal:
        assert n_past % CHUNK == 0 and seq <= CHUNK and n_past % TK == 0
    blk0 = row0 // tq
    qrow = lambda b, q: (blk0 + b * nq + q, 0)
    in_specs = [
        pl.BlockSpec((tq, Q_LORA), qrow),
        _const_spec(w_q.shape),
        _const_spec(w_ukt.shape),
        _const_spec(w_uv.shape),
        pl.BlockSpec((tq, N_HEADS * ROPE_DIM), lambda b, q: (q, 0)),
        pl.BlockSpec((tq, N_HEADS * ROPE_DIM), lambda b, q: (q, 0)),
        _const_spec((1, ATTN_W)),
    ]
    args = [cqn, w_q, w_ukt, w_uv, cosq, sinq, gao]
    if n_past:
        in_specs += [pl.BlockSpec((None, n_past, KV_LORA), lambda b, q: (b, 0, 0)),
                     pl.BlockSpec((None, ROPE_DIM, n_past), lambda b, q: (b, 0, 0))]
        args += [past_kv, past_kr]
    in_specs += [pl.BlockSpec((seq, KV_LORA), lambda b, q: (b, 0)),
                 pl.BlockSpec((None, ROPE_DIM, seq), lambda b, q: (b, 0, 0))]
    args += [ckv, krope]
    rows = N_HEADS * tq
    kern = functools.partial(_attn_kernel, tq=tq, n_past=n_past, causal=causal)
    return pl.pallas_call(
        kern,
        grid=(n_batch, nq),
        in_specs=in_specs,
        out_specs=pl.BlockSpec((tq, ATTN_W), lambda b, q: (b * nq + q, 0)),
        out_shape=jax.ShapeDtypeStruct((n_batch * seq, ATTN_W), BF16),
        scratch_shapes=[
            pltpu.VMEM((rows, KV_LORA), BF16),
            pltpu.VMEM((rows, ROPE_DIM), BF16),
            pltpu.VMEM((rows, LANES), F32),
            pltpu.VMEM((rows, LANES), F32),
            pltpu.VMEM((rows, KV_LORA), F32),
            pltpu.VMEM((2, rows, TK), F32),
            pltpu.VMEM((rows, LANES), jnp.int32),
        ],
        compiler_params=pltpu.CompilerParams(dimension_semantics=("arbitrary", "arbitrary"),
                                             vmem_limit_bytes=VMEM_LIMIT),
        name="attn_prompt" if causal else "attn_sample",
    )(*args)


def _out_proj_kernel(attnp_ref, attns_ref, convn_ref, xp_ref, xs_ref, wo_ref, gffn_ref, wr_ref,
                     br_ref, h_ref, xpk_ref, mi_ref, mf_ref, cnt_ref, carry_ref, *, n_prompt_tiles):
    i = pl.program_id(0)

    @pl.when(i == 0)
    def _():
        carry_ref[...] = jnp.zeros(carry_ref.shape, F32)

    def tile(x_ref, attn_ref):
        y = jnp.dot(attn_ref[...], wo_ref[:ATTN_W, :], preferred_element_type=F32)
        y = y + jnp.dot(convn_ref[...], wo_ref[ATTN_W:, :], preferred_element_type=F32)
        h = x_ref[...] + y
        h_ref[...] = h
        xn = _rms(h, gffn_ref[...])

        half = D_MODEL // 2
        xh = xn.astype(BF16)
        xh32 = xh.astype(F32)
        lo = lax.bitcast_convert_type(xh32[:, :half], jnp.uint32)
        hi = lax.bitcast_convert_type(xh32[:, half:], jnp.uint32)
        xpk_ref[...] = (lo >> 16) | (hi & jnp.uint32(0xFFFF0000))

        xl = (xn - xh32).astype(BF16)
        hh_hl = jnp.dot(xh, wr_ref[...], preferred_element_type=F32)
        lh = jnp.dot(xl, wr_ref[:, :LANES], preferred_element_type=F32)
        logits = hh_hl[:, :LANES] + (lh + hh_hl[:, LANES:]) + br_ref[...]
        lane = lax.broadcasted_iota(jnp.int32, (TM, LANES), 1).astype(F32)
        ninf = -jnp.inf
        far = float(LANES)

        def first_argmax(v):
            vmax = jnp.max(v, axis=-1, keepdims=True)
            return vmax, jnp.min(jnp.where(v == vmax, lane, far), axis=-1, keepdims=True)

        gl = jnp.where(lane < N_GROUPS, logits, ninf)
        gmax, gidx = first_argmax(gl)
        g_p = 1.0 / jnp.sum(jnp.exp(gl - gmax), axis=-1, keepdims=True)
        e_lo = N_GROUPS + EXPERTS_PER_GROUP * gidx
        el = jnp.where((lane >= e_lo) & (lane < e_lo + EXPERTS_PER_GROUP), logits, ninf)
        e1max, i1 = first_argmax(el)
        z = jnp.sum(jnp.exp(el - e1max), axis=-1, keepdims=True)
        el2 = jnp.where(lane == i1, ninf, el)
        e2max, i2 = first_argmax(el2)
        p1 = 1.0 / z
        p2 = jnp.exp(e2max - e1max) / z
        den = p1 + p2
        g0 = g_p * p1 / den
        g1 = g_p * p2 / den
        e0 = i1 - N_GROUPS
        e1 = i2 - N_GROUPS

        oh0 = lane == e0
        oh1 = lane == e1
        oh = jnp.where(oh0 | oh1, 1.0, 0.0)
        r = lax.broadcasted_iota(jnp.int32, (TM, TM), 0)
        c = lax.broadcasted_iota(jnp.int32, (TM, TM), 1)
        ltri = jnp.where(r > c, 1.0, 0.0).astype(BF16)
        before = jnp.dot(ltri, oh.astype(BF16), preferred_element_type=F32) + carry_ref[...]
        rank0 = jnp.sum(jnp.where(oh0, before, 0.0), axis=-1, keepdims=True)
        rank1 = jnp.sum(jnp.where(oh1, before, 0.0), axis=-1, keepdims=True)
        total = carry_ref[...] + jnp.sum(oh, axis=0, keepdims=True)
        carry_ref[...] = total
        cnt_ref[...] = jnp.broadcast_to(total, cnt_ref.shape)

        mi = jnp.where(lane == 0, e0, jnp.where(lane == 1, e1, jnp.where(lane == 2, rank0, rank1)))
        mi_ref[...] = jnp.transpose(mi)[:SUBLANES, :].astype(jnp.int32)
        mf_ref[...] = jnp.where(lane == 0, g0, g1)

    @pl.when(i < n_prompt_tiles)
    def _():
        tile(xp_ref, attnp_ref)

    @pl.when(i >= n_prompt_tiles)
    def _():
        tile(xs_ref, attns_ref)


def _out_proj(attn_p, attn_s, conv_n, xp, xs, w_ob, gffn, w_r2, b_r):
    m = conv_n.shape[0]
    npt = xp.shape[0] // TM
    last_p = npt - 1
    row = lambda i: (i, 0)
    return pl.pallas_call(
        functools.partial(_out_proj_kernel, n_prompt_tiles=npt),
        grid=(m // TM,),
        in_specs=[
            pl.BlockSpec((TM, ATTN_W), lambda i: (jnp.minimum(i, last_p), 0)),
            pl.BlockSpec((TM, ATTN_W), lambda i: (jnp.maximum(i - npt, 0), 0)),
            pl.BlockSpec((TM, CONV_CH), row),
            pl.BlockSpec((TM, D_MODEL), lambda i: (jnp.minimum(i, last_p), 0)),
            pl.BlockSpec((TM, D_MODEL), lambda i: (jnp.maximum(i - npt, 0), 0)),
            _const_spec(w_ob.shape),
            _const_spec((1, D_MODEL)),
            _const_spec(w_r2.shape),
            _const_spec((1, LANES)),
        ],
        out_specs=[
            pl.BlockSpec((TM, D_MODEL), row),
            pl.BlockSpec((TM, D_MODEL // 2), row),
            pl.BlockSpec((SUBLANES, TM), lambda i: (0, i)),
            pl.BlockSpec((TM, LANES), row),
            pl.BlockSpec((SUBLANES, LANES), lambda i: (0, 0)),
        ],
        out_shape=[
            jax.ShapeDtypeStruct((m, D_MODEL), F32),
            jax.ShapeDtypeStruct((m, D_MODEL // 2), jnp.uint32),
            jax.ShapeDtypeStruct((SUBLANES, m), jnp.int32),
            jax.ShapeDtypeStruct((m, LANES), F32),
            jax.ShapeDtypeStruct((SUBLANES, LANES), F32),
        ],
        scratch_shapes=[pltpu.VMEM((1, LANES), F32)],
        compiler_params=pltpu.CompilerParams(dimension_semantics=("arbitrary",),
                                             vmem_limit_bytes=VMEM_LIMIT),
        name="out_proj",
    )(attn_p, attn_s, conv_n, xp, xs, w_ob, gffn, w_r2, b_r)


def _dispatch_kernel(d0_ref, d1_ref, zlo_ref, zn_ref, nu_ref, xpk_ref, xs_hbm, zeros_ref, sems, *, n_blocks):
    i = pl.program_id(0)
    sem = sems.at[0]
    zsem = sems.at[1]

    def zero_fill(act):
        def per_expert(e, c):
            lo = zlo_ref[e]
            n = zn_ref[e]
            head = (-lo) & (SUBLANES - 1)
            for r in range(SUBLANES - 1):
                @pl.when(r < head)
                def _(r=r):
                    act(pltpu.make_async_copy(zeros_ref.at[pl.ds(0, 1)], xs_hbm.at[pl.ds(lo + r, 1)], zsem))
            off = lo + head
            rest = n - head
            size = MOE_BLOCK // 2
            while size >= SUBLANES:
                @pl.when((rest & size) != 0)
                def _(off=off, size=size):
                    dst = xs_hbm.at[pl.ds(pl.multiple_of(off, SUBLANES), size)]
                    act(pltpu.make_async_copy(zeros_ref.at[pl.ds(0, size)], dst, zsem))
                off = off + (rest & size)
                size //= 2
            return c

        def per_block(b, c):
            dst = xs_hbm.at[pl.ds(pl.multiple_of(b * MOE_BLOCK, MOE_BLOCK), MOE_BLOCK)]
            act(pltpu.make_async_copy(zeros_ref, dst, zsem))
            return c

        lax.fori_loop(0, N_EXPERTS, per_expert, 0)
        lax.fori_loop(nu_ref[0], n_blocks, per_block, 0)

    @pl.when(i == 0)
    def _():
        zeros_ref[...] = jnp.zeros(zeros_ref.shape, zeros_ref.dtype)
        zero_fill(lambda cp: cp.start())
        zero_fill(lambda cp: cp.wait())

    base = i * TM

    def start(g, c):
        for u in range(SUBLANES):
            r = base + g * SUBLANES + u
            src = xpk_ref.at[g, pl.ds(u, 1)]
            pltpu.make_async_copy(src, xs_hbm.at[pl.ds(d0_ref[r], 1)], sem).start()
            pltpu.make_async_copy(src, xs_hbm.at[pl.ds(d1_ref[r], 1)], sem).start()
        return c

    lax.fori_loop(0, TM // SUBLANES, start, 0)
    for _ in range(2):
        pltpu.make_async_copy(xs_hbm.at[pl.ds(0, TM)], xs_hbm.at[pl.ds(0, TM)], sem).wait()


def _dispatch(dest0, dest1, pad_lo, n_pad, n_used, xpk, n_blocks):
    m = xpk.shape[0]
    grid_spec = pltpu.PrefetchScalarGridSpec(
        num_scalar_prefetch=5,
        grid=(m // TM,),
        in_specs=[pl.BlockSpec((TM // SUBLANES, SUBLANES, D_MODEL // 2), lambda i, *_: (i, 0, 0))],
        out_specs=pl.BlockSpec(memory_space=pl.ANY),
        scratch_shapes=[pltpu.VMEM((MOE_BLOCK, D_MODEL // 2), jnp.uint32),
                        pltpu.SemaphoreType.DMA((2,))],
    )
    return pl.pallas_call(
        functools.partial(_dispatch_kernel, n_blocks=n_blocks),
        grid_spec=grid_spec,
        out_shape=jax.ShapeDtypeStruct((n_blocks * MOE_BLOCK, D_MODEL // 2), jnp.uint32),
        compiler_params=pltpu.CompilerParams(dimension_semantics=("arbitrary",)),
        name="dispatch",
    )(dest0, dest1, pad_lo, n_pad, n_used, xpk.reshape(m // SUBLANES, SUBLANES, D_MODEL // 2))


def _cast_rows(src_ref, dst_ref):
    rows, width = src_ref.shape
    chunk = max(CAST_CHUNK_ELEMS // width, 2 * SUBLANES)
    assert rows % chunk == 0

    def body(c, carry):
        r0 = pl.multiple_of(c * chunk, chunk)
        dst_ref[pl.ds(r0, chunk), :] = src_ref[pl.ds(r0, chunk), :].astype(dst_ref.dtype)
        return carry
    lax.fori_loop(0, rows // chunk, body, 0, unroll=2)


def _experts_kernel(be_ref, nu_ref, nxt_ref, x_ref, wg_hbm, wu_hbm, wd_hbm, y_ref,
                    sg_ref, su_ref, sd_ref, wgb_ref, wub_ref, wdb_ref, sems):
    b = pl.program_id(0)
    active = b < nu_ref[0]
    new_expert = jnp.logical_or(b == 0, be_ref[b] != be_ref[jnp.maximum(b - 1, 0)])

    def weight_copies(e):
        return (pltpu.make_async_copy(wg_hbm.at[e], sg_ref, sems.at[0]),
                pltpu.make_async_copy(wu_hbm.at[e], su_ref, sems.at[1]),
                pltpu.make_async_copy(wd_hbm.at[e], sd_ref, sems.at[2]))

    @pl.when(b == 0)
    def _():
        for cp in weight_copies(be_ref[0]):
            cp.start()

    @pl.when(jnp.logical_and(active, new_expert))
    def _():
        for cp in weight_copies(be_ref[b]):
            cp.wait()
        _cast_rows(sg_ref, wgb_ref)
        _cast_rows(su_ref, wub_ref)
        _cast_rows(sd_ref, wdb_ref)

        @pl.when(nxt_ref[b] >= 0)
        def _():
            for cp in weight_copies(nxt_ref[b]):
                cp.start()

    @pl.when(active)
    def _():
        half = D_MODEL // 2
        xw = x_ref[...]
        xa = lax.bitcast_convert_type(xw << 16, F32).astype(BF16)
        xb = lax.bitcast_convert_type(xw & jnp.uint32(0xFFFF0000), F32).astype(BF16)
        g = jnp.dot(xa, wgb_ref[:half, :], preferred_element_type=F32)
        g = g + jnp.dot(xb, wgb_ref[half:, :], preferred_element_type=F32)
        u = jnp.dot(xa, wub_ref[:half, :], preferred_element_type=F32)
        u = u + jnp.dot(xb, wub_ref[half:, :], preferred_element_type=F32)
        hmid = (g * jax.nn.sigmoid(g)) * u
        y_ref[...] = jnp.dot(hmid.astype(BF16), wdb_ref[...], preferred_element_type=F32)

    @pl.when(b >= nu_ref[0])
    def _():
        y_ref[...] = jnp.zeros(y_ref.shape, y_ref.dtype)


def _experts(block_e, n_used, next_e, x_sorted, w_gate, w_up, w_down):
    p = x_sorted.shape[0]
    nb = p // MOE_BLOCK

    def xrow(b, be, nu, nxt):
        return (jnp.maximum(jnp.minimum(b, nu[0] - 1), 0), 0)

    grid_spec = pltpu.PrefetchScalarGridSpec(
        num_scalar_prefetch=3,
        grid=(nb,),
        in_specs=[
            pl.BlockSpec((MOE_BLOCK, D_MODEL // 2), xrow),
            pl.BlockSpec(memory_space=pl.ANY),
            pl.BlockSpec(memory_space=pl.ANY),
            pl.BlockSpec(memory_space=pl.ANY),
        ],
        out_specs=pl.BlockSpec((MOE_BLOCK, D_MODEL), lambda b, be, nu, nxt: (b, 0)),
        scratch_shapes=[pltpu.VMEM((D_MODEL, D_FF), F32), pltpu.VMEM((D_MODEL, D_FF), F32),
                        pltpu.VMEM((D_FF, D_MODEL), F32),
                        pltpu.VMEM((D_MODEL, D_FF), BF16), pltpu.VMEM((D_MODEL, D_FF), BF16),
                        pltpu.VMEM((D_FF, D_MODEL), BF16),
                        pltpu.SemaphoreType.DMA((3,))],
    )
    return pl.pallas_call(
        _experts_kernel,
        grid_spec=grid_spec,
        out_shape=jax.ShapeDtypeStruct((p, D_MODEL), F32),
        compiler_params=pltpu.CompilerParams(dimension_semantics=("arbitrary",),
                                             vmem_limit_bytes=VMEM_LIMIT),
        name="experts",
    )(block_e, n_used, next_e, x_sorted, w_gate, w_up, w_down)


def _combine_kernel(d0_ref, d1_ref, h_ref, mf_ref, gfin_ref, y_hbm, outp_ref, outs_ref, y0_ref, y1_ref, sems,
                    *, n_tiles, n_prompt_tiles):
    i = pl.program_id(0)

    def gather(tile, slot, act):
        base = tile * TM

        def body(g, c):
            for u in range(SUBLANES):
                r = base + g * SUBLANES + u
                act(pltpu.make_async_copy(y_hbm.at[pl.ds(d0_ref[r], 1)], y0_ref.at[slot, g, pl.ds(u, 1)],
                                          sems.at[slot]))
                act(pltpu.make_async_copy(y_hbm.at[pl.ds(d1_ref[r], 1)], y1_ref.at[slot, g, pl.ds(u, 1)],
                                          sems.at[slot]))
            return c
        lax.fori_loop(0, TM // SUBLANES, body, 0)

    @pl.when(i == 0)
    def _():
        gather(0, 0, lambda cp: cp.start())

    @pl.when(i + 1 < n_tiles)
    def _():
        gather(i + 1, (i + 1) % 2, lambda cp: cp.start())

    slot = i % 2
    for _ in range(2):
        pltpu.make_async_copy(y_hbm.at[pl.ds(0, TM)], y_hbm.at[pl.ds(0, TM)], sems.at[slot]).wait()
    def finish(out_ref):
        mf = mf_ref[...]
        ffn = mf[:, :, 0:1] * y0_ref[slot] + mf[:, :, 1:2] * y1_ref[slot]
        out_ref[...] = _rms(h_ref[...] + ffn, gfin_ref[...])

    @pl.when(i < n_prompt_tiles)
    def _():
        finish(outp_ref)

    @pl.when(i >= n_prompt_tiles)
    def _():
        finish(outs_ref)


def _combine(dest0, dest1, h, mf, gfin, y_sorted, *, n_prompt_rows):
    m = h.shape[0]
    npt = n_prompt_rows // TM
    tg = TM // SUBLANES
    grouped = lambda a: a.reshape(a.shape[0] // SUBLANES, SUBLANES, a.shape[1])
    grid_spec = pltpu.PrefetchScalarGridSpec(
        num_scalar_prefetch=2,
        grid=(m // TM,),
        in_specs=[
            pl.BlockSpec((tg, SUBLANES, D_MODEL), lambda i, *_: (i, 0, 0)),
            pl.BlockSpec((tg, SUBLANES, LANES), lambda i, *_: (i, 0, 0)),
            pl.BlockSpec((1, 1, D_MODEL), lambda i, *_: (0, 0, 0)),
            pl.BlockSpec(memory_space=pl.ANY),
        ],
        out_specs=[pl.BlockSpec((tg, SUBLANES, D_MODEL), lambda i, *_: (jnp.minimum(i, npt - 1), 0, 0)),
                   pl.BlockSpec((tg, SUBLANES, D_MODEL), lambda i, *_: (jnp.maximum(i - npt, 0), 0, 0))],
        scratch_shapes=[pltpu.VMEM((2, TM // SUBLANES, SUBLANES, D_MODEL), F32),
                        pltpu.VMEM((2, TM // SUBLANES, SUBLANES, D_MODEL), F32),
                        pltpu.SemaphoreType.DMA((2,))],
    )
    y_p, y_s = pl.pallas_call(
        functools.partial(_combine_kernel, n_tiles=m // TM, n_prompt_tiles=npt),
        grid_spec=grid_spec,
        out_shape=[jax.ShapeDtypeStruct((n_prompt_rows // SUBLANES, SUBLANES, D_MODEL), F32),
                   jax.ShapeDtypeStruct(((m - n_prompt_rows) // SUBLANES, SUBLANES, D_MODEL), F32)],
        compiler_params=pltpu.CompilerParams(dimension_semantics=("arbitrary",),
                                             vmem_limit_bytes=VMEM_LIMIT),
        name="combine",
    )(dest0, dest1, grouped(h), grouped(mf), gfin.reshape(1, 1, D_MODEL), y_sorted)
    return y_p.reshape(n_prompt_rows, D_MODEL), y_s.reshape(m - n_prompt_rows, D_MODEL)


def _rope_tables(pos):
    f32 = np.float32
    inv = np.power(f32(ROPE_THETA), -np.arange(0, ROPE_DIM, 2, dtype=f32) / f32(ROPE_DIM)).astype(f32)
    ang = (pos.astype(f32)[:, None] * inv[None, :]).astype(f32)
    cos, sin = np.cos(ang).astype(f32), np.sin(ang).astype(f32)
    return np.concatenate([cos, cos], axis=-1), np.concatenate([-sin, sin], axis=-1)


def _swap_halves(w):
    return jnp.concatenate([w[..., ROPE_DIM // 2:], w[..., :ROPE_DIM // 2]], axis=-1)


def kernel(x_prompt, x_sample, cache_kv_latent, cache_k_rope, state_conv, norm_mix, w_in, norm_q, w_uq,
           norm_kv, w_uk, w_uv, conv_w, norm_attn_out, norm_conv_out, w_o, norm_ffn, w_router_group,
           b_router_group, w_router_expert, b_router_expert, w_gate, w_up, w_down, norm_final):
    assert w_in.shape[0] == 1, "single-layer trunk"
    bp, seq_p, _ = x_prompt.shape
    bs, seq_s, _ = x_sample.shape
    past_len = cache_kv_latent.shape[2]
    np_rows, ns_rows = bp * seq_p, bs * seq_s
    m = np_rows + ns_rows
    assert seq_p % TM == 0 and TM % seq_s == 0 and ns_rows % TM == 0 and seq_s == CHUNK

    xp = x_prompt.reshape(np_rows, D_MODEL)
    xs = x_sample.reshape(ns_rows, D_MODEL)
    row_vec = lambda v: v.reshape(1, -1)

    assert w_in.shape[2] == Q_LORA + KV_LORA + ROPE_DIM + 3 * CONV_CH
    w_t = jnp.swapaxes(w_in[0], 0, 1).astype(BF16)
    wq4 = w_uq[0].reshape(Q_LORA, N_HEADS, QK_NOPE + ROPE_DIM)
    wq_rope = wq4[:, :, QK_NOPE:]
    w_q = jnp.concatenate([wq4[:, :, :QK_NOPE].reshape(Q_LORA, -1), wq_rope.reshape(Q_LORA, -1),
                           _swap_halves(wq_rope).reshape(Q_LORA, -1)], axis=1).astype(BF16)
    w_ukt = jnp.transpose(w_uk[0], (1, 2, 0)).astype(BF16)
    w_uvh = jnp.transpose(w_uv[0], (1, 0, 2)).astype(BF16)
    w_ob = w_o[0].astype(BF16)
    n_router = N_GROUPS + N_EXPERTS
    w_r = jnp.concatenate([w_router_group[0], w_router_expert[0].reshape(D_MODEL, N_EXPERTS)], axis=1)
    w_r = jnp.pad(w_r, ((0, 0), (0, LANES - n_router)))
    w_rh = w_r.astype(BF16)
    w_rl = (w_r - w_rh.astype(F32)).astype(BF16)
    w_r2 = jnp.concatenate([w_rh, w_rl], axis=1)
    b_r =jnp.pad(jnp.concatenate([b_router_group[0], b_router_expert[0].reshape(N_EXPERTS)]),
                  (0, LANES - n_router)).reshape(1, LANES)

    cos_p, sin_p = _rope_tables(np.arange(seq_p))
    cos_s, sin_s = _rope_tables(past_len + np.arange(seq_s))
    cosk = np.concatenate([cos_p, np.tile(cos_s, (TM // seq_s, 1))], axis=0)
    sink = np.concatenate([sin_p, np.tile(sin_s, (TM // seq_s, 1))], axis=0)
    state = jnp.concatenate([jnp.zeros((bp, CONV_W - 1, CONV_CH), F32), state_conv[0]], axis=0)

    cqn, ckv_p, kr_p, ckv_s, kr_s, conv_n, utail = _in_proj(
        xp, xs, row_vec(norm_mix[0]), w_t, row_vec(norm_q[0]), row_vec(norm_kv[0]),
        row_vec(norm_conv_out[0]), conv_w[0], cosk, sink, state, seq_p=seq_p, seq_s=seq_s)

    gao = row_vec(norm_attn_out[0])
    attn_p = _attention(cqn, w_q, w_ukt, w_uvh, np.tile(cos_p, (1, N_HEADS)), np.tile(sin_p, (1, N_HEADS)),
                        gao, ckv_p, kr_p, n_batch=bp, seq=seq_p, row0=0)
    attn_s = _attention(cqn, w_q, w_ukt, w_uvh, np.tile(cos_s, (1, N_HEADS)), np.tile(sin_s, (1, N_HEADS)),
                        gao, ckv_s, kr_s, n_batch=bs, seq=seq_s, row0=np_rows,
                        past_kv=cache_kv_latent[0], past_kr=jnp.swapaxes(cache_k_rope[0], 1, 2))

    h, xpk, mi, mf, cnt = _out_proj(attn_p, attn_s, conv_n, xp, xs, w_ob, row_vec(norm_ffn[0]),
                                    w_r2, b_r)

    counts = cnt[0, :N_EXPERTS].astype(jnp.int32)
    padded = (counts + MOE_BLOCK - 1) // MOE_BLOCK * MOE_BLOCK
    pad_end = jnp.cumsum(padded)
    pad_start = pad_end - padded
    n_blocks = -(-(m * 2) // MOE_BLOCK) + N_EXPERTS
    block_row0 = jnp.arange(n_blocks, dtype=jnp.int32) * MOE_BLOCK
    block_e = jnp.minimum(jnp.sum((pad_end[None, :] <= block_row0[:, None]).astype(jnp.int32), axis=1),
                          N_EXPERTS - 1)
    n_used = (pad_end[-1:] // MOE_BLOCK).astype(jnp.int32)
    expert_ids = jnp.arange(N_EXPERTS, dtype=jnp.int32)[:, None]

    def seg_start(e):
        return jnp.sum(jnp.where(expert_ids == e[None, :], pad_start[:, None], 0), axis=0)

    dest0 = seg_start(mi[0]) + mi[2]
    dest1 = seg_start(mi[1]) + mi[3]

    x_sorted = _dispatch(dest0, dest1, pad_start + counts, padded - counts, n_used, xpk, n_blocks)
    later = (expert_ids.T > block_e[:, None]) & (padded > 0)[None, :]
    next_e = jnp.min(jnp.where(later, expert_ids.T, N_EXPERTS), axis=1)
    next_e = jnp.where(next_e == N_EXPERTS, -1, next_e).astype(jnp.int32)
    y_sorted = _experts(block_e, n_used, next_e, x_sorted, w_gate[0], w_up[0], w_down[0])
    gfin = row_vec(norm_final)
    y_p, y_s = _combine(dest0, dest1, h, mf, gfin, y_sorted, n_prompt_rows=np_rows)

    ut = utail.reshape(m // CHUNK, SUBLANES, CONV_CH)
    tails = ut[:, SUBLANES - (CONV_W - 1):, :]
    p_last = (jnp.arange(bp) + 1) * (seq_p // CHUNK) - 1
    s_last = np_rows // CHUNK + (jnp.arange(bs) + 1) * (seq_s // CHUNK) - 1
    return (y_p.reshape(bp, seq_p, D_MODEL),
            y_s.reshape(bs, seq_s, D_MODEL),
            ckv_p.reshape(1, bp, seq_p, KV_LORA),
            jnp.swapaxes(kr_p, 1, 2)[None],
            tails[p_last][None],
            ckv_s.reshape(1, bs, seq_s, KV_LORA),
            jnp.swapaxes(kr_s, 1, 2)[None],
            tails[s_last][None])
```

```python
import functools

import jax
import jax.numpy as jnp
import numpy as np
from jax import lax
from jax.experimental import pallas as pl
from jax.experimental.pallas import tpu as pltpu

F32 = jnp.float32
BF16 = jnp.bfloat16

D_MODEL = 2048
N_HEADS = 8
QK_NOPE = 128
ROPE_DIM = 64
V_DIM = 128
Q_LORA = 512
KV_LORA = 512
ATTN_W = N_HEADS * V_DIM
CONV_CH = D_MODEL - ATTN_W
CONV_W = 3
CHUNK = 64
N_GROUPS = 4
EXPERTS_PER_GROUP = 8
N_EXPERTS = N_GROUPS * EXPERTS_PER_GROUP
D_FF = 512
ROPE_THETA = 10000.0
EPS = 1e-6
ATTN_SCALE = (QK_NOPE + ROPE_DIM) ** -0.5
EXP2_SCALE = ATTN_SCALE * 1.4426950408889634

LANES = 128
SUBLANES = 8
TM = 256
MOE_BLOCK = 256
TQ = 256
TK = 256
NEG_BIG = -1e30
VMEM_LIMIT = 56 * 1024 * 1024


def _rms(v, g):
    return v * lax.rsqrt(jnp.mean(v * v, axis=-1, keepdims=True) + EPS) * g


def _lane_bcast(v, width):
    if width % LANES == 0:
        return jnp.concatenate([v] * (width // LANES), axis=1)
    assert width < LANES
    return v[:, :width]


def _const_spec(shape):
    nd = len(shape)
    return pl.BlockSpec(shape, lambda *_: (0,) * nd, pipeline_mode=pl.Buffered(1))


def _in_proj_kernel(xp_ref, xs_ref, gmix_ref, wt_ref, gq_ref, gkv_ref, gco_ref, convw_ref,
                    cos_ref, sin_ref, state_ref,
                    cqn_ref, ckvp_ref, krp_ref, ckvs_ref, krs_ref, convn_ref, utail_ref, ext_ref,
                    *, n_prompt_tiles, tiles_per_seq, n_prompt_seq, sample_seq_len):
    i = pl.program_id(0)

    def conv_block(u_sub, gate_sub, row0, length):
        ext_ref[SUBLANES:SUBLANES + length, :] = u_sub
        um1 = ext_ref[SUBLANES - 1:SUBLANES - 1 + length, :]
        um2 = ext_ref[SUBLANES - 2:SUBLANES - 2 + length, :]
        cw = convw_ref[...]
        conv = cw[0:1] * um2 + cw[1:2] * um1 + cw[2:3] * u_sub
        convn_ref[row0:row0 + length, :] = _rms(gate_sub * conv, gco_ref[...]).astype(BF16)

    def tile(x_ref, is_prompt):
        ckv_ref, krt_ref = (ckvp_ref, krp_ref) if is_prompt else (ckvs_ref, krs_ref)
        x = x_ref[...]
        xn = _rms(x, gmix_ref[...]).astype(BF16)
        lat_w = Q_LORA + KV_LORA
        conv0 = lat_w + ROPE_DIM
        nt = (((1,), (1,)), ((), ()))
        za = lax.dot_general(xn, wt_ref[:lat_w, :], nt, preferred_element_type=F32)
        cqn_ref[...] = _rms(za[:, :Q_LORA], gq_ref[...]).astype(BF16)
        ckv_ref[...] = _rms(za[:, Q_LORA:], gkv_ref[...])
        zk = lax.dot_general(xn, wt_ref[lat_w:conv0, :], nt, preferred_element_type=F32)
        zk_swapped = jnp.concatenate([zk[:, ROPE_DIM // 2:], zk[:, :ROPE_DIM // 2]], axis=1)
        k_rope = zk * cos_ref[...] + zk_swapped * sin_ref[...]
        if is_prompt:
            krt_ref[...] = k_rope.T
        else:
            for k in range(TM // sample_seq_len):
                krt_ref[k] = k_rope[k * sample_seq_len:(k + 1) * sample_seq_len, :].T

        zc = lax.dot_general(xn, wt_ref[conv0:, :], nt, preferred_element_type=F32)
        gate_b = zc[:, :CONV_CH]
        u = zc[:, CONV_CH:2 * CONV_CH] * zc[:, 2 * CONV_CH:]
        for j in range(TM // CHUNK):
            utail_ref[j] = u[CHUNK * (j + 1) - SUBLANES:CHUNK * (j + 1), :]

        if is_prompt:
            first = (i % tiles_per_seq) == 0

            @pl.when(first)
            def _():
                ext_ref[SUBLANES - 2:SUBLANES, :] = state_ref[i // tiles_per_seq]

            @pl.when(jnp.logical_not(first))
            def _():
                ext_ref[SUBLANES - 2:SUBLANES, :] = ext_ref[TM + SUBLANES - 2:TM + SUBLANES, :]

            conv_block(u, gate_b, 0, TM)
        else:
            n_sub = TM // sample_seq_len
            seq0 = n_prompt_seq + (i - n_prompt_tiles) * n_sub
            for k in range(n_sub):
                ext_ref[SUBLANES - 2:SUBLANES, :] = state_ref[seq0 + k]
                lo = k * sample_seq_len
                conv_block(u[lo:lo + sample_seq_len], gate_b[lo:lo + sample_seq_len], lo, sample_seq_len)

    @pl.when(i < n_prompt_tiles)
    def _():
        tile(xp_ref, True)

    @pl.when(i >= n_prompt_tiles)
    def _():
        tile(xs_ref, False)


def _in_proj(xp, xs, gmix, w_t, gq, gkv, gco, convw, cosk, sink, state, *, seq_p, seq_s):
    np_rows, ns_rows = xp.shape[0], xs.shape[0]
    m = np_rows + ns_rows
    npt, nst = np_rows // TM, ns_rows // TM
    tps = seq_p // TM
    n_prompt_seq = np_rows // seq_p
    last_p = npt - 1

    def tab_idx(i):
        return (jnp.where(i < npt, i % tps, tps), 0)

    row = lambda i: (i, 0)
    prow = lambda i: (jnp.minimum(i, last_p), 0)
    srow = lambda i: (jnp.maximum(i - npt, 0), 0)
    kern = functools.partial(_in_proj_kernel, n_prompt_tiles=npt, tiles_per_seq=tps,
                             n_prompt_seq=n_prompt_seq, sample_seq_len=seq_s)
    return pl.pallas_call(
        kern,
        grid=(npt + nst,),
        in_specs=[
            pl.BlockSpec((TM, D_MODEL), prow),
            pl.BlockSpec((TM, D_MODEL), srow),
            _const_spec((1, D_MODEL)),
            _const_spec(w_t.shape),
            _const_spec((1, Q_LORA)),
            _const_spec((1, KV_LORA)),
            _const_spec((1, CONV_CH)),
            _const_spec((CONV_W, CONV_CH)),
            pl.BlockSpec((TM, ROPE_DIM), tab_idx),
            pl.BlockSpec((TM, ROPE_DIM), tab_idx),
            _const_spec(state.shape),
        ],
        out_specs=[
            pl.BlockSpec((TM, Q_LORA), row),
            pl.BlockSpec((TM, KV_LORA), prow),
            pl.BlockSpec((None, ROPE_DIM, TM), lambda i: (jnp.minimum(i, last_p) // tps, 0,
                                                          jnp.minimum(i, last_p) % tps)),
            pl.BlockSpec((TM, KV_LORA), srow),
            pl.BlockSpec((TM // seq_s, ROPE_DIM, seq_s), lambda i: (jnp.maximum(i - npt, 0), 0, 0)),
            pl.BlockSpec((TM, CONV_CH), row),
            pl.BlockSpec((TM // CHUNK, SUBLANES, CONV_CH), lambda i: (i, 0, 0)),
        ],
        out_shape=[
            jax.ShapeDtypeStruct((m, Q_LORA), BF16),
            jax.ShapeDtypeStruct((np_rows, KV_LORA), F32),
            jax.ShapeDtypeStruct((n_prompt_seq, ROPE_DIM, seq_p), F32),
            jax.ShapeDtypeStruct((ns_rows, KV_LORA), F32),
            jax.ShapeDtypeStruct((ns_rows // seq_s, ROPE_DIM, seq_s), F32),
            jax.ShapeDtypeStruct((m, CONV_CH), BF16),
            jax.ShapeDtypeStruct((m // CHUNK, SUBLANES, CONV_CH), F32),
        ],
        scratch_shapes=[pltpu.VMEM((TM + SUBLANES, CONV_CH), F32)],
        compiler_params=pltpu.CompilerParams(dimension_semantics=("arbitrary",),
                                             vmem_limit_bytes=VMEM_LIMIT),
        name="in_proj",
    )(xp, xs, gmix, w_t, gq, gkv, gco, convw, cosk, sink, state)


def _attn_kernel(*refs, tq, n_past, causal):
    refs = list(refs)
    cqn_ref, wq_ref, wuk_ref, wuv_ref, cos_ref, sin_ref, gao_ref = refs[:7]
    refs = refs[7:]
    if n_past:
        pkv_ref, pkr_ref = refs[:2]
        refs = refs[2:]
    kv_ref, kr_ref, out_ref, qlat_ref, qr_ref, m_ref, l_ref, acc_ref, s_ref, klim_ref = refs

    qi = pl.program_id(1)
    rows = N_HEADS * tq

    q = jnp.dot(cqn_ref[...], wq_ref[...], preferred_element_type=F32)
    nope_w = N_HEADS * QK_NOPE
    rope_w = N_HEADS * ROPE_DIM
    qrope = q[:, nope_w:nope_w + rope_w] * cos_ref[...] + q[:, nope_w + rope_w:] * sin_ref[...]
    for h in range(N_HEADS):
        qn = q[:, h * QK_NOPE:(h + 1) * QK_NOPE].astype(BF16)
        ql = jnp.dot(qn, wuk_ref[h], preferred_element_type=F32)
        qlat_ref[h * tq:(h + 1) * tq, :] = ql.astype(BF16)
        qr_ref[h * tq:(h + 1) * tq, :] = qrope[:, h * ROPE_DIM:(h + 1) * ROPE_DIM].astype(BF16)


    nt = (((1,), (1,)), ((), ()))

    def scores(kc_f32, krt_f32):
        s = lax.dot_general(qlat_ref[...], kc_f32.astype(BF16), nt, preferred_element_type=F32)
        return s + jnp.dot(qr_ref[...], krt_f32.astype(BF16), preferred_element_type=F32)

    def update(s, kc_f32, mask, first=False):
        if mask is not None:
            s = jnp.where(mask, s, NEG_BIG)
        m_cur = jnp.max(s, axis=-1, keepdims=True)
        if first:
            m_new = jnp.broadcast_to(m_cur, m_ref.shape)
        else:
            m_prev = m_ref[...]
            m_new = jnp.maximum(m_prev, m_cur)
            alpha = jnp.exp2((m_prev - m_new) * EXP2_SCALE)
        p = jnp.exp2((s - _lane_bcast(m_new, s.shape[1])) * EXP2_SCALE)
        l_cur = jnp.sum(p, axis=-1, keepdims=True)
        pv = jnp.dot(p.astype(BF16), kc_f32.astype(BF16), preferred_element_type=F32)
        if first:
            l_ref[...] = jnp.broadcast_to(l_cur, l_ref.shape)
            acc_ref[...] = pv
        else:
            l_ref[...] = alpha * l_ref[...] + l_cur
            acc_ref[...] = _lane_bcast(alpha, KV_LORA) * acc_ref[...] + pv
        m_ref[...] = m_new

    def pipelined(kv, kr, lo, hi, last, mask_fn):
        def body(j, c):
            k0 = pl.multiple_of(j * TK, TK)
            k1 = pl.multiple_of(jnp.minimum(j + 1, last) * TK, TK)
            s_cur = s_ref[j % 2]
            s_ref[(j + 1) % 2] = scores(kv[pl.ds(k1, TK), :], kr[:, pl.ds(k1, TK)])
            update(s_cur, kv[pl.ds(k0, TK), :], None if mask_fn is None else mask_fn(k0))
            return c
        lax.fori_loop(lo, hi, body, 0)

    def pipelined_pairs(kv, kr, n_pairs, last):
        def body(i, c):
            ka = pl.multiple_of((2 * i + 1) * TK, TK)
            kb = pl.multiple_of((2 * i + 2) * TK, TK)
            kc = pl.multiple_of(jnp.minimum(2 * i + 3, last) * TK, TK)
            s_ref[0] = scores(kv[pl.ds(kb, TK), :], kr[:, pl.ds(kb, TK)])
            update(s_ref[1], kv[pl.ds(ka, TK), :], None)
            s_ref[1] = scores(kv[pl.ds(kc, TK), :], kr[:, pl.ds(kc, TK)])
            update(s_ref[0], kv[pl.ds(kb, TK), :], None)
            return c
        lax.fori_loop(0, n_pairs, body, 0)

    def first_block(kv, kr, last, mask):
        k1 = pl.multiple_of(jnp.minimum(1, last) * TK, TK)
        s_ref[0] = scores(kv[pl.ds(0, TK), :], kr[:, pl.ds(0, TK)])
        s_ref[1] = scores(kv[pl.ds(k1, TK), :], kr[:, pl.ds(k1, TK)])
        update(s_ref[0], kv[pl.ds(0, TK), :], mask, first=True)

    if n_past:
        n_pb = n_past // TK
        first_block(pkv_ref, pkr_ref, n_pb - 1, None)
        n_pairs = (n_pb - 1) // 2
        pipelined_pairs(pkv_ref, pkr_ref, n_pairs, n_pb - 1)
        if 1 + 2 * n_pairs < n_pb:
            pipelined(pkv_ref, pkr_ref, 1 + 2 * n_pairs, n_pb, n_pb - 1, None)

    if causal:
        n_blocks = ((qi + 1) * tq + TK - 1) // TK
        n_full = jnp.minimum((qi * tq // CHUNK + 1) * CHUNK // TK, n_blocks)

        assert tq & (tq - 1) == 0 and CHUNK & (CHUNK - 1) == 0
        r = lax.broadcasted_iota(jnp.int32, (rows, LANES), 0)
        q_pos = qi * tq + (r & (tq - 1))
        klim_ref[...] = (q_pos & ~(CHUNK - 1)) + CHUNK

        def mask_fn(k0):
            cidx = lax.broadcasted_iota(jnp.int32, (rows, TK), 1)
            return cidx < _lane_bcast(klim_ref[...] - k0, TK)

        first_block(kv_ref, kr_ref, n_blocks - 1, mask_fn(0))
        n_pairs = jnp.maximum(n_full - 1, 0) // 2
        pipelined_pairs(kv_ref, kr_ref, n_pairs, n_blocks - 1)
        pipelined(kv_ref, kr_ref, 1 + 2 * n_pairs, n_full, n_blocks - 1, None)
        pipelined(kv_ref, kr_ref, jnp.maximum(n_full, 1), n_blocks, n_blocks - 1, mask_fn)
    else:
        update(scores(kv_ref[...], kr_ref[...]), kv_ref[...], None)

    o = acc_ref[...] / _lane_bcast(l_ref[...], KV_LORA)
    parts = []
    for h in range(N_HEADS):
        oh = o[h * tq:(h + 1) * tq, :].astype(BF16)
        parts.append(jnp.dot(oh, wuv_ref[h], preferred_element_type=F32))
    attn = jnp.concatenate(parts, axis=-1)
    out_ref[...] = _rms(attn, gao_ref[...]).astype(BF16)


def _attention(cqn, w_q, w_ukt, w_uv, cosq, sinq, gao, ckv, krope, *, n_batch, seq, row0,
               past_kv=None, past_kr=None):
    causal = past_kv is None
    tq = TQ if causal else seq
    nq = seq // tq
    n_past = 0 if causal else past_kv.shape[1]
    if not causal:
        assert n_past % CHUNK == 0 and seq <= CHUNK and n_past % TK == 0
    blk0 = row0 // tq
    qrow = lambda b, q: (blk0 + b * nq + q, 0)
    in_specs = [
        pl.BlockSpec((tq, Q_LORA), qrow),
        _const_spec(w_q.shape),
        _const_spec(w_ukt.shape),
        _const_spec(w_uv.shape),
        pl.BlockSpec((tq, N_HEADS * ROPE_DIM), lambda b, q: (q, 0)),
        pl.BlockSpec((tq, N_HEADS * ROPE_DIM), lambda b, q: (q, 0)),
        _const_spec((1, ATTN_W)),
    ]
    args = [cqn, w_q, w_ukt, w_uv, cosq, sinq, gao]
    if n_past:
        in_specs += [pl.BlockSpec((None, n_past, KV_LORA), lambda b, q: (b, 0, 0)),
                     pl.BlockSpec((None, ROPE_DIM, n_past), lambda b, q: (b, 0, 0))]
        args += [past_kv, past_kr]
    in_specs += [pl.BlockSpec((seq, KV_LORA), lambda b, q: (b, 0)),
                 pl.BlockSpec((None, ROPE_DIM, seq), lambda b, q: (b, 0, 0))]
    args += [ckv, krope]
    rows = N_HEADS * tq
    kern = functools.partial(_attn_kernel, tq=tq, n_past=n_past, causal=causal)
    return pl.pallas_call(
        kern,
        grid=(n_batch, nq),
        in_specs=in_specs,
        out_specs=pl.BlockSpec((tq, ATTN_W), lambda b, q: (b * nq + q, 0)),
        out_shape=jax.ShapeDtypeStruct((n_batch * seq, ATTN_W), BF16),
        scratch_shapes=[
            pltpu.VMEM((rows, KV_LORA), BF16),
            pltpu.VMEM((rows, ROPE_DIM), BF16),
            pltpu.VMEM((rows, LANES), F32),
            pltpu.VMEM((rows, LANES), F32),
            pltpu.VMEM((rows, KV_LORA), F32),
            pltpu.VMEM((2, rows, TK), F32),
            pltpu.VMEM((rows, LANES), jnp.int32),
        ],
        compiler_params=pltpu.CompilerParams(dimension_semantics=("arbitrary", "arbitrary"),
                                             vmem_limit_bytes=VMEM_LIMIT),
        name="attn_prompt" if causal else "attn_sample",
    )(*args)


def _out_proj_kernel(attnp_ref, attns_ref, convn_ref, xp_ref, xs_ref, wo_ref, gffn_ref, wr_ref,
                     br_ref, h_ref, xpk_ref, mi_ref, mf_ref, cnt_ref, carry_ref, logit_ref, *, n_prompt_tiles):
    i = pl.program_id(0)

    @pl.when(i == 0)
    def _():
        carry_ref[...] = jnp.zeros(carry_ref.shape, F32)
        logit_ref[...] = jnp.zeros(logit_ref.shape, F32)

    def tile(x_ref, attn_ref):
        prev_logits = logit_ref[...]
        y = jnp.dot(attn_ref[...], wo_ref[:ATTN_W, :], preferred_element_type=F32)
        y = y + jnp.dot(convn_ref[...], wo_ref[ATTN_W:, :], preferred_element_type=F32)
        h = x_ref[...] + y
        h_ref[...] = h
        xn = _rms(h, gffn_ref[...])

        half = D_MODEL // 2
        xh = xn.astype(BF16)
        xh32 = xh.astype(F32)
        lo = lax.bitcast_convert_type(xh32[:, :half], jnp.uint32)
        hi = lax.bitcast_convert_type(xh32[:, half:], jnp.uint32)
        xpk_ref[...] = (lo >> 16) | (hi & jnp.uint32(0xFFFF0000))

        xl = (xn - xh32).astype(BF16)
        hh_hl = jnp.dot(xh, wr_ref[...], preferred_element_type=F32)
        lh = jnp.dot(xl, wr_ref[:, :LANES], preferred_element_type=F32)
        logit_ref[...] = hh_hl[:, :LANES] + (lh + hh_hl[:, LANES:]) + br_ref[...]

        logits = prev_logits
        counted = (i > 0).astype(F32)
        lane = lax.broadcasted_iota(jnp.int32, (TM, LANES), 1).astype(F32)
        ninf = -jnp.inf
        far = float(LANES)

        def first_argmax(v):
            vmax = jnp.max(v, axis=-1, keepdims=True)
            return vmax, jnp.min(jnp.where(v == vmax, lane, far), axis=-1, keepdims=True)

        gl = jnp.where(lane < N_GROUPS, logits, ninf)
        gmax, gidx = first_argmax(gl)
        g_p = 1.0 / jnp.sum(jnp.exp(gl - gmax), axis=-1, keepdims=True)
        e_lo = N_GROUPS + EXPERTS_PER_GROUP * gidx
        el = jnp.where((lane >= e_lo) & (lane < e_lo + EXPERTS_PER_GROUP), logits, ninf)
        e1max, i1 = first_argmax(el)
        z = jnp.sum(jnp.exp(el - e1max), axis=-1, keepdims=True)
        el2 = jnp.where(lane == i1, ninf, el)
        e2max, i2 = first_argmax(el2)
        p1 = 1.0 / z
        p2 = jnp.exp(e2max - e1max) / z
        den = p1 + p2
        g0 = g_p * p1 / den
        g1 = g_p * p2 / den
        e0 = i1 - N_GROUPS
        e1 = i2 - N_GROUPS

        oh0 = lane == e0
        oh1 = lane == e1
        oh = jnp.where(oh0 | oh1, 1.0, 0.0)
        r = lax.broadcasted_iota(jnp.int32, (TM, TM), 0)
        c = lax.broadcasted_iota(jnp.int32, (TM, TM), 1)
        ltri = jnp.where(r > c, 1.0, 0.0).astype(BF16)
        before = jnp.dot(ltri, oh.astype(BF16), preferred_element_type=F32) + carry_ref[...]
        rank0 = jnp.sum(jnp.where(oh0, before, 0.0), axis=-1, keepdims=True)
        rank1 = jnp.sum(jnp.where(oh1, before, 0.0), axis=-1, keepdims=True)
        total = carry_ref[...] + counted * jnp.sum(oh, axis=0, keepdims=True)
        carry_ref[...] = total
        cnt_ref[...] = jnp.broadcast_to(total, cnt_ref.shape)

        mi = jnp.where(lane == 0, e0, jnp.where(lane == 1, e1, jnp.where(lane == 2, rank0, rank1)))
        mi_ref[...] = jnp.transpose(mi)[:SUBLANES, :].astype(jnp.int32)
        mf_ref[...] = jnp.where(lane == 0, g0, g1)

    @pl.when(i < n_prompt_tiles)
    def _():
        tile(xp_ref, attnp_ref)

    @pl.when(i >= n_prompt_tiles)
    def _():
        tile(xs_ref, attns_ref)


def _out_proj(attn_p, attn_s, conv_n, xp, xs, w_ob, gffn, w_r2, b_r):
    m = conv_n.shape[0]
    npt = xp.shape[0] // TM
    n_tiles = m // TM
    last_p, last_s, last = npt - 1, n_tiles - npt - 1, n_tiles - 1
    row = lambda i: (jnp.minimum(i, last), 0)
    prow = lambda i: (jnp.minimum(i, last_p), 0)
    srow = lambda i: (jnp.clip(i - npt, 0, last_s), 0)
    lag = lambda i: jnp.maximum(i - 1, 0)
    return pl.pallas_call(
        functools.partial(_out_proj_kernel, n_prompt_tiles=npt),
        grid=(n_tiles + 1,),
        in_specs=[
            pl.BlockSpec((TM, ATTN_W), prow),
            pl.BlockSpec((TM, ATTN_W), srow),
            pl.BlockSpec((TM, CONV_CH), row),
            pl.BlockSpec((TM, D_MODEL), prow),
            pl.BlockSpec((TM, D_MODEL), srow),
            _const_spec(w_ob.shape),
            _const_spec((1, D_MODEL)),
            _const_spec(w_r2.shape),
            _const_spec((1, LANES)),
        ],
        out_specs=[
            pl.BlockSpec((TM, D_MODEL), row),
            pl.BlockSpec((TM, D_MODEL // 2), row),
            pl.BlockSpec((SUBLANES, TM), lambda i: (0, lag(i))),
            pl.BlockSpec((TM, LANES), lambda i: (lag(i), 0)),
            pl.BlockSpec((SUBLANES, LANES), lambda i: (0, 0)),
        ],
        out_shape=[
            jax.ShapeDtypeStruct((m, D_MODEL), F32),
            jax.ShapeDtypeStruct((m, D_MODEL // 2), jnp.uint32),
            jax.ShapeDtypeStruct((SUBLANES, m), jnp.int32),
            jax.ShapeDtypeStruct((m, LANES), F32),
            jax.ShapeDtypeStruct((SUBLANES, LANES), F32),
        ],
        scratch_shapes=[pltpu.VMEM((1, LANES), F32), pltpu.VMEM((TM, LANES), F32)],
        compiler_params=pltpu.CompilerParams(dimension_semantics=("arbitrary",),
                                             vmem_limit_bytes=VMEM_LIMIT),
        name="out_proj",
    )(attn_p, attn_s, conv_n, xp, xs, w_ob, gffn, w_r2, b_r)


def _dispatch_kernel(d0_ref, d1_ref, zlo_ref, zn_ref, nu_ref, xpk_ref, xs_hbm, zeros_ref, sems, *, n_blocks):
    i = pl.program_id(0)
    sem = sems.at[0]
    zsem = sems.at[1]

    def zero_fill(act):
        def per_expert(e, c):
            lo = zlo_ref[e]
            n = zn_ref[e]
            head = (-lo) & (SUBLANES - 1)
            for r in range(SUBLANES - 1):
                @pl.when(r < head)
                def _(r=r):
                    act(pltpu.make_async_copy(zeros_ref.at[pl.ds(0, 1)], xs_hbm.at[pl.ds(lo + r, 1)], zsem))
            off = lo + head
            rest = n - head
            size = MOE_BLOCK // 2
            while size >= SUBLANES:
                @pl.when((rest & size) != 0)
                def _(off=off, size=size):
                    dst = xs_hbm.at[pl.ds(pl.multiple_of(off, SUBLANES), size)]
                    act(pltpu.make_async_copy(zeros_ref.at[pl.ds(0, size)], dst, zsem))
                off = off + (rest & size)
                size //= 2
            return c

        def per_block(b, c):
            dst = xs_hbm.at[pl.ds(pl.multiple_of(b * MOE_BLOCK, MOE_BLOCK), MOE_BLOCK)]
            act(pltpu.make_async_copy(zeros_ref, dst, zsem))
            return c

        lax.fori_loop(0, N_EXPERTS, per_expert, 0)
        lax.fori_loop(nu_ref[0], n_blocks, per_block, 0)

    @pl.when(i == 0)
    def _():
        zeros_ref[...] = jnp.zeros(zeros_ref.shape, zeros_ref.dtype)
        zero_fill(lambda cp: cp.start())
        zero_fill(lambda cp: cp.wait())

    base = i * TM

    def start(g, c):
        for u in range(SUBLANES):
            r = base + g * SUBLANES + u
            src = xpk_ref.at[g, pl.ds(u, 1)]
            pltpu.make_async_copy(src, xs_hbm.at[pl.ds(d0_ref[r], 1)], sem).start()
            pltpu.make_async_copy(src, xs_hbm.at[pl.ds(d1_ref[r], 1)], sem).start()
        return c

    lax.fori_loop(0, TM // SUBLANES, start, 0)
    for _ in range(2):
        pltpu.make_async_copy(xs_hbm.at[pl.ds(0, TM)], xs_hbm.at[pl.ds(0, TM)], sem).wait()


def _dispatch(dest0, dest1, pad_lo, n_pad, n_used, xpk, n_blocks):
    m = xpk.shape[0]
    grid_spec = pltpu.PrefetchScalarGridSpec(
        num_scalar_prefetch=5,
        grid=(m // TM,),
        in_specs=[pl.BlockSpec((TM // SUBLANES, SUBLANES, D_MODEL // 2), lambda i, *_: (i, 0, 0))],
        out_specs=pl.BlockSpec(memory_space=pl.ANY),
        scratch_shapes=[pltpu.VMEM((MOE_BLOCK, D_MODEL // 2), jnp.uint32),
                        pltpu.SemaphoreType.DMA((2,))],
    )
    return pl.pallas_call(
        functools.partial(_dispatch_kernel, n_blocks=n_blocks),
        grid_spec=grid_spec,
        out_shape=jax.ShapeDtypeStruct((n_blocks * MOE_BLOCK, D_MODEL // 2), jnp.uint32),
        compiler_params=pltpu.CompilerParams(dimension_semantics=("arbitrary",)),
        name="dispatch",
    )(dest0, dest1, pad_lo, n_pad, n_used, xpk.reshape(m // SUBLANES, SUBLANES, D_MODEL // 2))


def _experts_kernel(be_ref, nu_ref, nxt_ref, x_ref, wg_hbm, wu_hbm, wd_hbm, y_ref,
                    sg_ref, su_ref, sd_ref, wgb_ref, wub_ref, wdb_ref, sems):
    b = pl.program_id(0)
    active = b < nu_ref[0]
    new_expert = jnp.logical_or(b == 0, be_ref[b] != be_ref[jnp.maximum(b - 1, 0)])

    def weight_copies(e):
        return (pltpu.make_async_copy(wg_hbm.at[e], sg_ref, sems.at[0]),
                pltpu.make_async_copy(wu_hbm.at[e], su_ref, sems.at[1]),
                pltpu.make_async_copy(wd_hbm.at[e], sd_ref, sems.at[2]))

    @pl.when(b == 0)
    def _():
        for cp in weight_copies(be_ref[0]):
            cp.start()

    @pl.when(jnp.logical_and(active, new_expert))
    def _():
        for cp in weight_copies(be_ref[b]):
            cp.wait()
        wgb_ref[...] = sg_ref[...].astype(BF16)
        wub_ref[...] = su_ref[...].astype(BF16)
        wdb_ref[...] = sd_ref[...].astype(BF16)

        @pl.when(nxt_ref[b] >= 0)
        def _():
            for cp in weight_copies(nxt_ref[b]):
                cp.start()

    @pl.when(active)
    def _():
        half = D_MODEL // 2
        xw = x_ref[...]
        xa = lax.bitcast_convert_type(xw << 16, F32).astype(BF16)
        xb = lax.bitcast_convert_type(xw & jnp.uint32(0xFFFF0000), F32).astype(BF16)
        g = jnp.dot(xa, wgb_ref[:half, :], preferred_element_type=F32)
        g = g + jnp.dot(xb, wgb_ref[half:, :], preferred_element_type=F32)
        u = jnp.dot(xa, wub_ref[:half, :], preferred_element_type=F32)
        u = u + jnp.dot(xb, wub_ref[half:, :], preferred_element_type=F32)
        hmid = (g * jax.nn.sigmoid(g)) * u
        y_ref[...] = jnp.dot(hmid.astype(BF16), wdb_ref[...], preferred_element_type=F32)

    @pl.when(b >= nu_ref[0])
    def _():
        y_ref[...] = jnp.zeros(y_ref.shape, y_ref.dtype)


def _experts(block_e, n_used, next_e, x_sorted, w_gate, w_up, w_down):
    p = x_sorted.shape[0]
    nb = p // MOE_BLOCK

    def xrow(b, be, nu, nxt):
        return (jnp.maximum(jnp.minimum(b, nu[0] - 1), 0), 0)

    grid_spec = pltpu.PrefetchScalarGridSpec(
        num_scalar_prefetch=3,
        grid=(nb,),
        in_specs=[
            pl.BlockSpec((MOE_BLOCK, D_MODEL // 2), xrow),
            pl.BlockSpec(memory_space=pl.ANY),
            pl.BlockSpec(memory_space=pl.ANY),
            pl.BlockSpec(memory_space=pl.ANY),
        ],
        out_specs=pl.BlockSpec((MOE_BLOCK, D_MODEL), lambda b, be, nu, nxt: (b, 0)),
        scratch_shapes=[pltpu.VMEM((D_MODEL, D_FF), F32), pltpu.VMEM((D_MODEL, D_FF), F32),
                        pltpu.VMEM((D_FF, D_MODEL), F32),
                        pltpu.VMEM((D_MODEL, D_FF), BF16), pltpu.VMEM((D_MODEL, D_FF), BF16),
                        pltpu.VMEM((D_FF, D_MODEL), BF16),
                        pltpu.SemaphoreType.DMA((3,))],
    )
    return pl.pallas_call(
        _experts_kernel,
        grid_spec=grid_spec,
        out_shape=jax.ShapeDtypeStruct((p, D_MODEL), F32),
        compiler_params=pltpu.CompilerParams(dimension_semantics=("arbitrary",),
                                             vmem_limit_bytes=VMEM_LIMIT),
        name="experts",
    )(block_e, n_used, next_e, x_sorted, w_gate, w_up, w_down)


def _combine_kernel(d0_ref, d1_ref, h_ref, mf_ref, gfin_ref, y_hbm, outp_ref, outs_ref, y0_ref, y1_ref, sems,
                    *, n_tiles, n_prompt_tiles):
    i = pl.program_id(0)

    def gather(tile, slot, act):
        base = tile * TM

        def body(g, c):
            for u in range(SUBLANES):
                r = base + g * SUBLANES + u
                act(pltpu.make_async_copy(y_hbm.at[pl.ds(d0_ref[r], 1)], y0_ref.at[slot, g, pl.ds(u, 1)],
                                          sems.at[slot]))
                act(pltpu.make_async_copy(y_hbm.at[pl.ds(d1_ref[r], 1)], y1_ref.at[slot, g, pl.ds(u, 1)],
                                          sems.at[slot]))
            return c
        lax.fori_loop(0, TM // SUBLANES, body, 0)

    @pl.when(i == 0)
    def _():
        gather(0, 0, lambda cp: cp.start())

    @pl.when(i + 1 < n_tiles)
    def _():
        gather(i + 1, (i + 1) % 2, lambda cp: cp.start())

    slot = i % 2
    for _ in range(2):
        pltpu.make_async_copy(y_hbm.at[pl.ds(0, TM)], y_hbm.at[pl.ds(0, TM)], sems.at[slot]).wait()
    def finish(out_ref):
        mf = mf_ref[...]
        ffn = mf[:, :, 0:1] * y0_ref[slot] + mf[:, :, 1:2] * y1_ref[slot]
        out_ref[...] = _rms(h_ref[...] + ffn, gfin_ref[...])

    @pl.when(i < n_prompt_tiles)
    def _():
        finish(outp_ref)

    @pl.when(i >= n_prompt_tiles)
    def _():
        finish(outs_ref)


def _combine(dest0, dest1, h, mf, gfin, y_sorted, *, n_prompt_rows):
    m = h.shape[0]
    npt = n_prompt_rows // TM
    tg = TM // SUBLANES
    grouped = lambda a: a.reshape(a.shape[0] // SUBLANES, SUBLANES, a.shape[1])
    grid_spec = pltpu.PrefetchScalarGridSpec(
        num_scalar_prefetch=2,
        grid=(m // TM,),
        in_specs=[
            pl.BlockSpec((tg, SUBLANES, D_MODEL), lambda i, *_: (i, 0, 0)),
            pl.BlockSpec((tg, SUBLANES, LANES), lambda i, *_: (i, 0, 0)),
            pl.BlockSpec((1, 1, D_MODEL), lambda i, *_: (0, 0, 0)),
            pl.BlockSpec(memory_space=pl.ANY),
        ],
        out_specs=[pl.BlockSpec((tg, SUBLANES, D_MODEL), lambda i, *_: (jnp.minimum(i, npt - 1), 0, 0)),
                   pl.BlockSpec((tg, SUBLANES, D_MODEL), lambda i, *_: (jnp.maximum(i - npt, 0), 0, 0))],
        scratch_shapes=[pltpu.VMEM((2, TM // SUBLANES, SUBLANES, D_MODEL), F32),
                        pltpu.VMEM((2, TM // SUBLANES, SUBLANES, D_MODEL), F32),
                        pltpu.SemaphoreType.DMA((2,))],
    )
    y_p, y_s = pl.pallas_call(
        functools.partial(_combine_kernel, n_tiles=m // TM, n_prompt_tiles=npt),
        grid_spec=grid_spec,
        out_shape=[jax.ShapeDtypeStruct((n_prompt_rows // SUBLANES, SUBLANES, D_MODEL), F32),
                   jax.ShapeDtypeStruct(((m - n_prompt_rows) // SUBLANES, SUBLANES, D_MODEL), F32)],
        compiler_params=pltpu.CompilerParams(dimension_semantics=("arbitrary",),
                                             vmem_limit_bytes=VMEM_LIMIT),
        name="combine",
    )(dest0, dest1, grouped(h), grouped(mf), gfin.reshape(1, 1, D_MODEL), y_sorted)
    return y_p.reshape(n_prompt_rows, D_MODEL), y_s.reshape(m - n_prompt_rows, D_MODEL)


def _rope_tables(pos):
    f32 = np.float32
    inv = np.power(f32(ROPE_THETA), -np.arange(0, ROPE_DIM, 2, dtype=f32) / f32(ROPE_DIM)).astype(f32)
    ang = (pos.astype(f32)[:, None] * inv[None, :]).astype(f32)
    cos, sin = np.cos(ang).astype(f32), np.sin(ang).astype(f32)
    return np.concatenate([cos, cos], axis=-1), np.concatenate([-sin, sin], axis=-1)


def _swap_halves(w):
    return jnp.concatenate([w[..., ROPE_DIM // 2:], w[..., :ROPE_DIM // 2]], axis=-1)


def kernel(x_prompt, x_sample, cache_kv_latent, cache_k_rope, state_conv, norm_mix, w_in, norm_q, w_uq,
           norm_kv, w_uk, w_uv, conv_w, norm_attn_out, norm_conv_out, w_o, norm_ffn, w_router_group,
           b_router_group, w_router_expert, b_router_expert, w_gate, w_up, w_down, norm_final):
    assert w_in.shape[0] == 1, "single-layer trunk"
    bp, seq_p, _ = x_prompt.shape
    bs, seq_s, _ = x_sample.shape
    past_len = cache_kv_latent.shape[2]
    np_rows, ns_rows = bp * seq_p, bs * seq_s
    m = np_rows + ns_rows
    assert seq_p % TM == 0 and TM % seq_s == 0 and ns_rows % TM == 0 and seq_s == CHUNK

    xp = x_prompt.reshape(np_rows, D_MODEL)
    xs = x_sample.reshape(ns_rows, D_MODEL)
    row_vec = lambda v: v.reshape(1, -1)

    assert w_in.shape[2] == Q_LORA + KV_LORA + ROPE_DIM + 3 * CONV_CH
    w_t = jnp.swapaxes(w_in[0], 0, 1).astype(BF16)
    wq4 = w_uq[0].reshape(Q_LORA, N_HEADS, QK_NOPE + ROPE_DIM)
    wq_rope = wq4[:, :, QK_NOPE:]
    w_q = jnp.concatenate([wq4[:, :, :QK_NOPE].reshape(Q_LORA, -1), wq_rope.reshape(Q_LORA, -1),
                           _swap_halves(wq_rope).reshape(Q_LORA, -1)], axis=1).astype(BF16)
    w_ukt = jnp.transpose(w_uk[0], (1, 2, 0)).astype(BF16)
    w_uvh = jnp.transpose(w_uv[0], (1, 0, 2)).astype(BF16)
    w_ob = w_o[0].astype(BF16)
    n_router = N_GROUPS + N_EXPERTS
    w_r = jnp.concatenate([w_router_group[0], w_router_expert[0].reshape(D_MODEL, N_EXPERTS)], axis=1)
    w_r = jnp.pad(w_r, ((0, 0), (0, LANES - n_router)))
    w_rh = w_r.astype(BF16)
    w_rl = (w_r - w_rh.astype(F32)).astype(BF16)
    w_r2 = jnp.concatenate([w_rh, w_rl], axis=1)
    b_r =jnp.pad(jnp.concatenate([b_router_group[0], b_router_expert[0].reshape(N_EXPERTS)]),
                  (0, LANES - n_router)).reshape(1, LANES)

    cos_p, sin_p = _rope_tables(np.arange(seq_p))
    cos_s, sin_s = _rope_tables(past_len + np.arange(seq_s))
    cosk = np.concatenate([cos_p, np.tile(cos_s, (TM // seq_s, 1))], axis=0)
    sink = np.concatenate([sin_p, np.tile(sin_s, (TM // seq_s, 1))], axis=0)
    state = jnp.concatenate([jnp.zeros((bp, CONV_W - 1, CONV_CH), F32), state_conv[0]], axis=0)

    cqn, ckv_p, kr_p, ckv_s, kr_s, conv_n, utail = _in_proj(
        xp, xs, row_vec(norm_mix[0]), w_t, row_vec(norm_q[0]), row_vec(norm_kv[0]),
        row_vec(norm_conv_out[0]), conv_w[0], cosk, sink, state, seq_p=seq_p, seq_s=seq_s)

    gao = row_vec(norm_attn_out[0])
    attn_p = _attention(cqn, w_q, w_ukt, w_uvh, np.tile(cos_p, (1, N_HEADS)), np.tile(sin_p, (1, N_HEADS)),
                        gao, ckv_p, kr_p, n_batch=bp, seq=seq_p, row0=0)
    attn_s = _attention(cqn, w_q, w_ukt, w_uvh, np.tile(cos_s, (1, N_HEADS)), np.tile(sin_s, (1, N_HEADS)),
                        gao, ckv_s, kr_s, n_batch=bs, seq=seq_s, row0=np_rows,
                        past_kv=cache_kv_latent[0], past_kr=jnp.swapaxes(cache_k_rope[0], 1, 2))

    h, xpk, mi, mf, cnt = _out_proj(attn_p, attn_s, conv_n, xp, xs, w_ob, row_vec(norm_ffn[0]),
                                    w_r2, b_r)

    counts = cnt[0, :N_EXPERTS].astype(jnp.int32)
    padded = (counts + MOE_BLOCK - 1) // MOE_BLOCK * MOE_BLOCK
    pad_end = jnp.cumsum(padded)
    pad_start = pad_end - padded
    n_blocks = -(-(m * 2) // MOE_BLOCK) + N_EXPERTS
    block_row0 = jnp.arange(n_blocks, dtype=jnp.int32) * MOE_BLOCK
    block_e = jnp.minimum(jnp.sum((pad_end[None, :] <= block_row0[:, None]).astype(jnp.int32), axis=1),
                          N_EXPERTS - 1)
    n_used = (pad_end[-1:] // MOE_BLOCK).astype(jnp.int32)
    expert_ids = jnp.arange(N_EXPERTS, dtype=jnp.int32)[:, None]

    def seg_start(e):
        return jnp.sum(jnp.where(expert_ids == e[None, :], pad_start[:, None], 0), axis=0)

    dest0 = seg_start(mi[0]) + mi[2]
    dest1 = seg_start(mi[1]) + mi[3]

    x_sorted = _dispatch(dest0, dest1, pad_start + counts, padded - counts, n_used, xpk, n_blocks)
    later = (expert_ids.T > block_e[:, None]) & (padded > 0)[None, :]
    next_e = jnp.min(jnp.where(later, expert_ids.T, N_EXPERTS), axis=1)
    next_e = jnp.where(next_e == N_EXPERTS, -1, next_e).astype(jnp.int32)
    y_sorted = _experts(block_e, n_used, next_e, x_sorted, w_gate[0], w_up[0], w_down[0])
    gfin = row_vec(norm_final)
    y_p, y_s = _combine(dest0, dest1, h, mf, gfin, y_sorted, n_prompt_rows=np_rows)

    ut = utail.reshape(m // CHUNK, SUBLANES, CONV_CH)
    tails = ut[:, SUBLANES - (CONV_W - 1):, :]
    p_last = (jnp.arange(bp) + 1) * (seq_p // CHUNK) - 1
    s_last = np_rows // CHUNK + (jnp.arange(bs) + 1) * (seq_s // CHUNK) - 1
    return (y_p.reshape(bp, seq_p, D_MODEL),
            y_s.reshape(bs, seq_s, D_MODEL),
            ckv_p.reshape(1, bp, seq_p, KV_LORA),
            jnp.swapaxes(kr_p, 1, 2)[None],
            tails[p_last][None],
            ckv_s.reshape(1, bs, seq_s, KV_LORA),
            jnp.swapaxes(kr_s, 1, 2)[None],
            tails[s_last][None])
```

```python
import functools

import jax
import jax.numpy as jnp
import numpy as np
from jax import lax
from jax.experimental import pallas as pl
from jax.experimental.pallas import tpu as pltpu

F32 = jnp.float32
BF16 = jnp.bfloat16

D_MODEL = 2048
N_HEADS = 8
QK_NOPE = 128
ROPE_DIM = 64
V_DIM = 128
Q_LORA = 512
KV_LORA = 512
ATTN_W = N_HEADS * V_DIM
CONV_CH = D_MODEL - ATTN_W
CONV_W = 3
CHUNK = 64
N_GROUPS = 4
EXPERTS_PER_GROUP = 8
N_EXPERTS = N_GROUPS * EXPERTS_PER_GROUP
D_FF = 512
ROPE_THETA = 10000.0
EPS = 1e-6
ATTN_SCALE = (QK_NOPE + ROPE_DIM) ** -0.5
EXP2_SCALE = ATTN_SCALE * 1.4426950408889634

LANES = 128
SUBLANES = 8
TM = 256
MOE_BLOCK = 256
TQ = 256
TK = 256
NEG_BIG = -1e30
VMEM_LIMIT = 56 * 1024 * 1024


def _rms(v, g):
    return v * lax.rsqrt(jnp.mean(v * v, axis=-1, keepdims=True) + EPS) * g


def _lane_bcast(v, width):
    if width % LANES == 0:
        return jnp.concatenate([v] * (width // LANES), axis=1)
    assert width < LANES
    return v[:, :width]


def _const_spec(shape):
    nd = len(shape)
    return pl.BlockSpec(shape, lambda *_: (0,) * nd, pipeline_mode=pl.Buffered(1))


def _in_proj_kernel(xp_ref, xs_ref, gmix_ref, wt_ref, gq_ref, gkv_ref, gco_ref, convw_ref,
                    cos_ref, sin_ref, state_ref,
                    cqn_ref, ckvp_ref, krp_ref, ckvs_ref, krs_ref, convn_ref, utail_ref, ext_ref,
                    *, n_prompt_tiles, tiles_per_seq, n_prompt_seq, sample_seq_len):
    i = pl.program_id(0)

    def conv_block(u_sub, gate_sub, row0, length):
        ext_ref[SUBLANES:SUBLANES + length, :] = u_sub
        um1 = ext_ref[SUBLANES - 1:SUBLANES - 1 + length, :]
        um2 = ext_ref[SUBLANES - 2:SUBLANES - 2 + length, :]
        cw = convw_ref[...]
        conv = cw[0:1] * um2 + cw[1:2] * um1 + cw[2:3] * u_sub
        convn_ref[row0:row0 + length, :] = _rms(gate_sub * conv, gco_ref[...]).astype(BF16)

    def tile(x_ref, is_prompt):
        ckv_ref, krt_ref = (ckvp_ref, krp_ref) if is_prompt else (ckvs_ref, krs_ref)
        x = x_ref[...]
        xn = _rms(x, gmix_ref[...]).astype(BF16)
        lat_w = Q_LORA + KV_LORA
        conv0 = lat_w + ROPE_DIM
        nt = (((1,), (1,)), ((), ()))
        za = lax.dot_general(xn, wt_ref[:lat_w, :], nt, preferred_element_type=F32)
        cqn_ref[...] = _rms(za[:, :Q_LORA], gq_ref[...]).astype(BF16)
        ckv_ref[...] = _rms(za[:, Q_LORA:], gkv_ref[...])
        zk = lax.dot_general(xn, wt_ref[lat_w:conv0, :], nt, preferred_element_type=F32)
        zk_swapped = jnp.concatenate([zk[:, ROPE_DIM // 2:], zk[:, :ROPE_DIM // 2]], axis=1)
        k_rope = zk * cos_ref[...] + zk_swapped * sin_ref[...]
        if is_prompt:
            krt_ref[...] = k_rope.T
        else:
            for k in range(TM // sample_seq_len):
                krt_ref[k] = k_rope[k * sample_seq_len:(k + 1) * sample_seq_len, :].T

        zc = lax.dot_general(xn, wt_ref[conv0:, :], nt, preferred_element_type=F32)
        gate_b = zc[:, :CONV_CH]
        u = zc[:, CONV_CH:2 * CONV_CH] * zc[:, 2 * CONV_CH:]
        for j in range(TM // CHUNK):
            utail_ref[j] = u[CHUNK * (j + 1) - SUBLANES:CHUNK * (j + 1), :]

        if is_prompt:
            first = (i % tiles_per_seq) == 0

            @pl.when(first)
            def _():
                ext_ref[SUBLANES - 2:SUBLANES, :] = state_ref[i // tiles_per_seq]

            @pl.when(jnp.logical_not(first))
            def _():
                ext_ref[SUBLANES - 2:SUBLANES, :] = ext_ref[TM + SUBLANES - 2:TM + SUBLANES, :]

            conv_block(u, gate_b, 0, TM)
        else:
            n_sub = TM // sample_seq_len
            seq0 = n_prompt_seq + (i - n_prompt_tiles) * n_sub
            for k in range(n_sub):
                ext_ref[SUBLANES - 2:SUBLANES, :] = state_ref[seq0 + k]
                lo = k * sample_seq_len
                conv_block(u[lo:lo + sample_seq_len], gate_b[lo:lo + sample_seq_len], lo, sample_seq_len)

    @pl.when(i < n_prompt_tiles)
    def _():
        tile(xp_ref, True)

    @pl.when(i >= n_prompt_tiles)
    def _():
        tile(xs_ref, False)


def _in_proj(xp, xs, gmix, w_t, gq, gkv, gco, convw, cosk, sink, state, *, seq_p, seq_s):
    np_rows, ns_rows = xp.shape[0], xs.shape[0]
    m = np_rows + ns_rows
    npt, nst = np_rows // TM, ns_rows // TM
    tps = seq_p // TM
    n_prompt_seq = np_rows // seq_p
    last_p = npt - 1

    def tab_idx(i):
        return (jnp.where(i < npt, i % tps, tps), 0)

    row = lambda i: (i, 0)
    prow = lambda i: (jnp.minimum(i, last_p), 0)
    srow = lambda i: (jnp.maximum(i - npt, 0), 0)
    kern = functools.partial(_in_proj_kernel, n_prompt_tiles=npt, tiles_per_seq=tps,
                             n_prompt_seq=n_prompt_seq, sample_seq_len=seq_s)
    return pl.pallas_call(
        kern,
        grid=(npt + nst,),
        in_specs=[
            pl.BlockSpec((TM, D_MODEL), prow),
            pl.BlockSpec((TM, D_MODEL), srow),
            _const_spec((1, D_MODEL)),
            _const_spec(w_t.shape),
            _const_spec((1, Q_LORA)),
            _const_spec((1, KV_LORA)),
            _const_spec((1, CONV_CH)),
            _const_spec((CONV_W, CONV_CH)),
            pl.BlockSpec((TM, ROPE_DIM), tab_idx),
            pl.BlockSpec((TM, ROPE_DIM), tab_idx),
            _const_spec(state.shape),
        ],
        out_specs=[
            pl.BlockSpec((TM, Q_LORA), row),
            pl.BlockSpec((TM, KV_LORA), prow),
            pl.BlockSpec((None, ROPE_DIM, TM), lambda i: (jnp.minimum(i, last_p) // tps, 0,
                                                          jnp.minimum(i, last_p) % tps)),
            pl.BlockSpec((TM, KV_LORA), srow),
            pl.BlockSpec((TM // seq_s, ROPE_DIM, seq_s), lambda i: (jnp.maximum(i - npt, 0), 0, 0)),
            pl.BlockSpec((TM, CONV_CH), row),
            pl.BlockSpec((TM // CHUNK, SUBLANES, CONV_CH), lambda i: (i, 0, 0)),
        ],
        out_shape=[
            jax.ShapeDtypeStruct((m, Q_LORA), BF16),
            jax.ShapeDtypeStruct((np_rows, KV_LORA), F32),
            jax.ShapeDtypeStruct((n_prompt_seq, ROPE_DIM, seq_p), F32),
            jax.ShapeDtypeStruct((ns_rows, KV_LORA), F32),
            jax.ShapeDtypeStruct((ns_rows // seq_s, ROPE_DIM, seq_s), F32),
            jax.ShapeDtypeStruct((m, CONV_CH), BF16),
            jax.ShapeDtypeStruct((m // CHUNK, SUBLANES, CONV_CH), F32),
        ],
        scratch_shapes=[pltpu.VMEM((TM + SUBLANES, CONV_CH), F32)],
        compiler_params=pltpu.CompilerParams(dimension_semantics=("arbitrary",),
                                             vmem_limit_bytes=VMEM_LIMIT),
        name="in_proj",
    )(xp, xs, gmix, w_t, gq, gkv, gco, convw, cosk, sink, state)


def _attn_kernel(*refs, tq, n_past, causal):
    refs = list(refs)
    cqn_ref, wq_ref, wuk_ref, wuv_ref, cos_ref, sin_ref, gao_ref = refs[:7]
    refs = refs[7:]
    if n_past:
        pkv_ref, pkr_ref = refs[:2]
        refs = refs[2:]
    kv_ref, kr_ref, out_ref, qlat_ref, qr_ref, m_ref, l_ref, acc_ref, s_ref, klim_ref = refs

    qi = pl.program_id(1)
    rows = N_HEADS * tq

    q = jnp.dot(cqn_ref[...], wq_ref[...], preferred_element_type=F32)
    nope_w = N_HEADS * QK_NOPE
    rope_w = N_HEADS * ROPE_DIM
    qrope = q[:, nope_w:nope_w + rope_w] * cos_ref[...] + q[:, nope_w + rope_w:] * sin_ref[...]
    for h in range(N_HEADS):
        qn = q[:, h * QK_NOPE:(h + 1) * QK_NOPE].astype(BF16)
        ql = jnp.dot(qn, wuk_ref[h], preferred_element_type=F32)
        qlat_ref[h * tq:(h + 1) * tq, :] = ql.astype(BF16)
        qr_ref[h * tq:(h + 1) * tq, :] = qrope[:, h * ROPE_DIM:(h + 1) * ROPE_DIM].astype(BF16)


    nt = (((1,), (1,)), ((), ()))

    def scores(kc_f32, krt_f32):
        s = lax.dot_general(qlat_ref[...], kc_f32.astype(BF16), nt, preferred_element_type=F32)
        return s + jnp.dot(qr_ref[...], krt_f32.astype(BF16), preferred_element_type=F32)

    def update(s, kc_f32, mask, first=False):
        if mask is not None:
            s = jnp.where(mask, s, NEG_BIG)
        m_cur = jnp.max(s, axis=-1, keepdims=True)
        if first:
            m_new = jnp.broadcast_to(m_cur, m_ref.shape)
        else:
            m_prev = m_ref[...]
            m_new = jnp.maximum(m_prev, m_cur)
            alpha = jnp.exp2((m_prev - m_new) * EXP2_SCALE)
        p = jnp.exp2((s - _lane_bcast(m_new, s.shape[1])) * EXP2_SCALE)
        l_cur = jnp.sum(p, axis=-1, keepdims=True)
        pv = jnp.dot(p.astype(BF16), kc_f32.astype(BF16), preferred_element_type=F32)
        if first:
            l_ref[...] = jnp.broadcast_to(l_cur, l_ref.shape)
            acc_ref[...] = pv
        else:
            l_ref[...] = alpha * l_ref[...] + l_cur
            acc_ref[...] = _lane_bcast(alpha, KV_LORA) * acc_ref[...] + pv
        m_ref[...] = m_new

    def pipelined(kv, kr, lo, hi, last, mask_fn):
        def body(j, c):
            k0 = pl.multiple_of(j * TK, TK)
            k1 = pl.multiple_of(jnp.minimum(j + 1, last) * TK, TK)
            s_cur = s_ref[j % 2]
            s_ref[(j + 1) % 2] = scores(kv[pl.ds(k1, TK), :], kr[:, pl.ds(k1, TK)])
            update(s_cur, kv[pl.ds(k0, TK), :], None if mask_fn is None else mask_fn(k0))
            return c
        lax.fori_loop(lo, hi, body, 0)

    def pipelined_pairs(kv, kr, n_pairs, last):
        def body(i, c):
            ka = pl.multiple_of((2 * i + 1) * TK, TK)
            kb = pl.multiple_of((2 * i + 2) * TK, TK)
            kc = pl.multiple_of(jnp.minimum(2 * i + 3, last) * TK, TK)
            s_ref[0] = scores(kv[pl.ds(kb, TK), :], kr[:, pl.ds(kb, TK)])
            update(s_ref[1], kv[pl.ds(ka, TK), :], None)
            s_ref[1] = scores(kv[pl.ds(kc, TK), :], kr[:, pl.ds(kc, TK)])
            update(s_ref[0], kv[pl.ds(kb, TK), :], None)
            return c
        lax.fori_loop(0, n_pairs, body, 0)

    def first_block(kv, kr, last, mask):
        k1 = pl.multiple_of(jnp.minimum(1, last) * TK, TK)
        s_ref[0] = scores(kv[pl.ds(0, TK), :], kr[:, pl.ds(0, TK)])
        s_ref[1] = scores(kv[pl.ds(k1, TK), :], kr[:, pl.ds(k1, TK)])
        update(s_ref[0], kv[pl.ds(0, TK), :], mask, first=True)

    if n_past:
        n_pb = n_past // TK
        first_block(pkv_ref, pkr_ref, n_pb - 1, None)
        n_pairs = (n_pb - 1) // 2
        pipelined_pairs(pkv_ref, pkr_ref, n_pairs, n_pb - 1)
        if 1 + 2 * n_pairs < n_pb:
            pipelined(pkv_ref, pkr_ref, 1 + 2 * n_pairs, n_pb, n_pb - 1, None)

    if causal:
        n_blocks = ((qi + 1) * tq + TK - 1) // TK
        n_full = jnp.minimum((qi * tq // CHUNK + 1) * CHUNK // TK, n_blocks)

        assert tq & (tq - 1) == 0 and CHUNK & (CHUNK - 1) == 0
        r = lax.broadcasted_iota(jnp.int32, (rows, LANES), 0)
        q_pos = qi * tq + (r & (tq - 1))
        klim_ref[...] = (q_pos & ~(CHUNK - 1)) + CHUNK

        def mask_fn(k0):
            cidx = lax.broadcasted_iota(jnp.int32, (rows, TK), 1)
            return cidx < _lane_bcast(klim_ref[...] - k0, TK)

        first_block(kv_ref, kr_ref, n_blocks - 1, mask_fn(0))
        n_pairs = jnp.maximum(n_full - 1, 0) // 2
        pipelined_pairs(kv_ref, kr_ref, n_pairs, n_blocks - 1)
        pipelined(kv_ref, kr_ref, 1 + 2 * n_pairs, n_full, n_blocks - 1, None)
        pipelined(kv_ref, kr_ref, jnp.maximum(n_full, 1), n_blocks, n_blocks - 1, mask_fn)
    else:
        update(scores(kv_ref[...], kr_ref[...]), kv_ref[...], None)

    o = acc_ref[...] / _lane_bcast(l_ref[...], KV_LORA)
    parts = []
    for h in range(N_HEADS):
        oh = o[h * tq:(h + 1) * tq, :].astype(BF16)
        parts.append(jnp.dot(oh, wuv_ref[h], preferred_element_type=F32))
    attn = jnp.concatenate(parts, axis=-1)
    out_ref[...] = _rms(attn, gao_ref[...]).astype(BF16)


def _attention(cqn, w_q, w_ukt, w_uv, cosq, sinq, gao, ckv, krope, *, n_batch, seq, row0,
               past_kv=None, past_kr=None):
    causal = past_kv is None
    tq = TQ if causal else seq
    nq = seq // tq
    n_past = 0 if causal else past_kv.shape[1]
    if not causal:
        assert n_past % CHUNK == 0 and seq <= CHUNK and n_past % TK == 0
    blk0 = row0 // tq
    qrow = lambda b, q: (blk0 + b * nq + q, 0)
    in_specs = [
        pl.BlockSpec((tq, Q_LORA), qrow),
        _const_spec(w_q.shape),
        _const_spec(w_ukt.shape),
        _const_spec(w_uv.shape),
        pl.BlockSpec((tq, N_HEADS * ROPE_DIM), lambda b, q: (q, 0)),
        pl.BlockSpec((tq, N_HEADS * ROPE_DIM), lambda b, q: (q, 0)),
        _const_spec((1, ATTN_W)),
    ]
    args = [cqn, w_q, w_ukt, w_uv, cosq, sinq, gao]
    if n_past:
        in_specs += [pl.BlockSpec((None, n_past, KV_LORA), lambda b, q: (b, 0, 0)),
                     pl.BlockSpec((None, ROPE_DIM, n_past), lambda b, q: (b, 0, 0))]
        args += [past_kv, past_kr]
    in_specs += [pl.BlockSpec((seq, KV_LORA), lambda b, q: (b, 0)),
                 pl.BlockSpec((None, ROPE_DIM, seq), lambda b, q: (b, 0, 0))]
    args += [ckv, krope]
    rows = N_HEADS * tq
    kern = functools.partial(_attn_kernel, tq=tq, n_past=n_past, causal=causal)
    return pl.pallas_call(
        kern,
        grid=(n_batch, nq),
        in_specs=in_specs,
        out_specs=pl.BlockSpec((tq, ATTN_W), lambda b, q: (b * nq + q, 0)),
        out_shape=jax.ShapeDtypeStruct((n_batch * seq, ATTN_W), BF16),
        scratch_shapes=[
            pltpu.VMEM((rows, KV_LORA), BF16),
            pltpu.VMEM((rows, ROPE_DIM), BF16),
            pltpu.VMEM((rows, LANES), F32),
            pltpu.VMEM((rows, LANES), F32),
            pltpu.VMEM((rows, KV_LORA), F32),
            pltpu.VMEM((2, rows, TK), F32),
            pltpu.VMEM((rows, LANES), jnp.int32),
        ],
        compiler_params=pltpu.CompilerParams(dimension_semantics=("arbitrary", "arbitrary"),
                                             vmem_limit_bytes=VMEM_LIMIT),
        name="attn_prompt" if causal else "attn_sample",
    )(*args)


def _out_proj_kernel(attnp_ref, attns_ref, convn_ref, xp_ref, xs_ref, wo_ref, gffn_ref, wr_ref,
                     br_ref, h_ref, xpk_ref, mi_ref, mf_ref, cnt_ref, carry_ref, logit_ref, *, n_prompt_tiles):
    i = pl.program_id(0)

    @pl.when(i == 0)
    def _():
        carry_ref[...] = jnp.zeros(carry_ref.shape, F32)
        logit_ref[...] = jnp.zeros(logit_ref.shape, F32)

    def tile(x_ref, attn_ref):
        prev_logits = logit_ref[...]
        y = jnp.dot(attn_ref[...], wo_ref[:ATTN_W, :], preferred_element_type=F32)
        y = y + jnp.dot(convn_ref[...], wo_ref[ATTN_W:, :], preferred_element_type=F32)
        h = x_ref[...] + y
        h_ref[...] = h
        xn = _rms(h, gffn_ref[...])

        half = D_MODEL // 2
        xh = xn.astype(BF16)
        xh32 = xh.astype(F32)
        lo = lax.bitcast_convert_type(xh32[:, :half], jnp.uint32)
        hi = lax.bitcast_convert_type(xh32[:, half:], jnp.uint32)
        xpk_ref[...] = (lo >> 16) | (hi & jnp.uint32(0xFFFF0000))

        xl = (xn - xh32).astype(BF16)
        hh_hl = jnp.dot(xh, wr_ref[...], preferred_element_type=F32)
        lh = jnp.dot(xl, wr_ref[:, :LANES], preferred_element_type=F32)
        logit_ref[...] = hh_hl[:, :LANES] + (lh + hh_hl[:, LANES:]) + br_ref[...]

        logits = prev_logits
        counted = (i > 0).astype(F32)
        lane = lax.broadcasted_iota(jnp.int32, (TM, LANES), 1).astype(F32)
        ninf = -jnp.inf
        far = float(LANES)

        def first_argmax(v):
            vmax = jnp.max(v, axis=-1, keepdims=True)
            return vmax, jnp.min(jnp.where(v == vmax, lane, far), axis=-1, keepdims=True)

        gl = jnp.where(lane < N_GROUPS, logits, ninf)
        gmax, gidx = first_argmax(gl)
        g_p = 1.0 / jnp.sum(jnp.exp(gl - gmax), axis=-1, keepdims=True)
        e_lo = N_GROUPS + EXPERTS_PER_GROUP * gidx
        el = jnp.where((lane >= e_lo) & (lane < e_lo + EXPERTS_PER_GROUP), logits, ninf)
        e1max, i1 = first_argmax(el)
        z = jnp.sum(jnp.exp(el - e1max), axis=-1, keepdims=True)
        el2 = jnp.where(lane == i1, ninf, el)
        e2max, i2 = first_argmax(el2)
        p1 = 1.0 / z
        p2 = jnp.exp(e2max - e1max) / z
        den = p1 + p2
        g0 = g_p * p1 / den
        g1 = g_p * p2 / den
        e0 = i1 - N_GROUPS
        e1 = i2 - N_GROUPS

        oh0 = lane == e0
        oh1 = lane == e1
        oh = jnp.where(oh0 | oh1, 1.0, 0.0)
        r = lax.broadcasted_iota(jnp.int32, (TM, TM), 0)
        c = lax.broadcasted_iota(jnp.int32, (TM, TM), 1)
        ltri = jnp.where(r > c, 1.0, 0.0).astype(BF16)
        before = jnp.dot(ltri, oh.astype(BF16), preferred_element_type=F32) + carry_ref[...]
        rank0 = jnp.sum(jnp.where(oh0, before, 0.0), axis=-1, keepdims=True)
        rank1 = jnp.sum(jnp.where(oh1, before, 0.0), axis=-1, keepdims=True)
        total = carry_ref[...] + counted * jnp.sum(oh, axis=0, keepdims=True)
        carry_ref[...] = total
        cnt_ref[...] = jnp.broadcast_to(total, cnt_ref.shape)

        mi = jnp.where(lane == 0, e0, jnp.where(lane == 1, e1, jnp.where(lane == 2, rank0, rank1)))
        mi_ref[...] = jnp.transpose(mi)[:SUBLANES, :].astype(jnp.int32)
        mf_ref[...] = jnp.where(lane == 0, g0, g1)

    @pl.when(i < n_prompt_tiles)
    def _():
        tile(xp_ref, attnp_ref)

    @pl.when(i >= n_prompt_tiles)
    def _():
        tile(xs_ref, attns_ref)


def _out_proj(attn_p, attn_s, conv_n, xp, xs, w_ob, gffn, w_r2, b_r):
    m = conv_n.shape[0]
    npt = xp.shape[0] // TM
    n_tiles = m // TM
    last_p, last_s, last = npt - 1, n_tiles - npt - 1, n_tiles - 1
    row = lambda i: (jnp.minimum(i, last), 0)
    prow = lambda i: (jnp.minimum(i, last_p), 0)
    srow = lambda i: (jnp.clip(i - npt, 0, last_s), 0)
    lag = lambda i: jnp.maximum(i - 1, 0)
    return pl.pallas_call(
        functools.partial(_out_proj_kernel, n_prompt_tiles=npt),
        grid=(n_tiles + 1,),
        in_specs=[
            pl.BlockSpec((TM, ATTN_W), prow),
            pl.BlockSpec((TM, ATTN_W), srow),
            pl.BlockSpec((TM, CONV_CH), row),
            pl.BlockSpec((TM, D_MODEL), prow),
            pl.BlockSpec((TM, D_MODEL), srow),
            _const_spec(w_ob.shape),
            _const_spec((1, D_MODEL)),
            _const_spec(w_r2.shape),
            _const_spec((1, LANES)),
        ],
        out_specs=[
            pl.BlockSpec((TM, D_MODEL), row),
            pl.BlockSpec((TM, D_MODEL // 2), row),
            pl.BlockSpec((SUBLANES, TM), lambda i: (0, lag(i))),
            pl.BlockSpec((TM, LANES), lambda i: (lag(i), 0)),
            pl.BlockSpec((SUBLANES, LANES), lambda i: (0, 0)),
        ],
        out_shape=[
            jax.ShapeDtypeStruct((m, D_MODEL), F32),
            jax.ShapeDtypeStruct((m, D_MODEL // 2), jnp.uint32),
            jax.ShapeDtypeStruct((SUBLANES, m), jnp.int32),
            jax.ShapeDtypeStruct((m, LANES), F32),
            jax.ShapeDtypeStruct((SUBLANES, LANES), F32),
        ],
        scratch_shapes=[pltpu.VMEM((1, LANES), F32), pltpu.VMEM((TM, LANES), F32)],
        compiler_params=pltpu.CompilerParams(dimension_semantics=("arbitrary",),
                                             vmem_limit_bytes=VMEM_LIMIT),
        name="out_proj",
    )(attn_p, attn_s, conv_n, xp, xs, w_ob, gffn, w_r2, b_r)


def _dispatch_kernel(d0_ref, d1_ref, zlo_ref, zn_ref, nu_ref, xpk_ref, xs_hbm, zeros_ref, sems, *, n_blocks):
    i = pl.program_id(0)
    sem = sems.at[0]
    zsem = sems.at[1]

    def zero_fill(act):
        def per_expert(e, c):
            lo = zlo_ref[e]
            n = zn_ref[e]
            head = (-lo) & (SUBLANES - 1)
            for r in range(SUBLANES - 1):
                @pl.when(r < head)
                def _(r=r):
                    act(pltpu.make_async_copy(zeros_ref.at[pl.ds(0, 1)], xs_hbm.at[pl.ds(lo + r, 1)], zsem))
            off = lo + head
            rest = n - head
            size = MOE_BLOCK // 2
            while size >= SUBLANES:
                @pl.when((rest & size) != 0)
                def _(off=off, size=size):
                    dst = xs_hbm.at[pl.ds(pl.multiple_of(off, SUBLANES), size)]
                    act(pltpu.make_async_copy(zeros_ref.at[pl.ds(0, size)], dst, zsem))
                off = off + (rest & size)
                size //= 2
            return c

        def per_block(b, c):
            dst = xs_hbm.at[pl.ds(pl.multiple_of(b * MOE_BLOCK, MOE_BLOCK), MOE_BLOCK)]
            act(pltpu.make_async_copy(zeros_ref, dst, zsem))
            return c

        lax.fori_loop(0, N_EXPERTS, per_expert, 0)
        lax.fori_loop(nu_ref[0], n_blocks, per_block, 0)

    @pl.when(i == 0)
    def _():
        zeros_ref[...] = jnp.zeros(zeros_ref.shape, zeros_ref.dtype)
        zero_fill(lambda cp: cp.start())
        zero_fill(lambda cp: cp.wait())

    base = i * TM

    def start(g, c):
        for u in range(SUBLANES):
            r = base + g * SUBLANES + u
            src = xpk_ref.at[g, pl.ds(u, 1)]
            pltpu.make_async_copy(src, xs_hbm.at[pl.ds(d0_ref[r], 1)], sem).start()
            pltpu.make_async_copy(src, xs_hbm.at[pl.ds(d1_ref[r], 1)], sem).start()
        return c

    lax.fori_loop(0, TM // SUBLANES, start, 0)
    for _ in range(2):
        pltpu.make_async_copy(xs_hbm.at[pl.ds(0, TM)], xs_hbm.at[pl.ds(0, TM)], sem).wait()


def _dispatch(dest0, dest1, pad_lo, n_pad, n_used, xpk, n_blocks):
    m = xpk.shape[0]
    grid_spec = pltpu.PrefetchScalarGridSpec(
        num_scalar_prefetch=5,
        grid=(m // TM,),
        in_specs=[pl.BlockSpec((TM // SUBLANES, SUBLANES, D_MODEL // 2), lambda i, *_: (i, 0, 0))],
        out_specs=pl.BlockSpec(memory_space=pl.ANY),
        scratch_shapes=[pltpu.VMEM((MOE_BLOCK, D_MODEL // 2), jnp.uint32),
                        pltpu.SemaphoreType.DMA((2,))],
    )
    return pl.pallas_call(
        functools.partial(_dispatch_kernel, n_blocks=n_blocks),
        grid_spec=grid_spec,
        out_shape=jax.ShapeDtypeStruct((n_blocks * MOE_BLOCK, D_MODEL // 2), jnp.uint32),
        compiler_params=pltpu.CompilerParams(dimension_semantics=("arbitrary",)),
        name="dispatch",
    )(dest0, dest1, pad_lo, n_pad, n_used, xpk.reshape(m // SUBLANES, SUBLANES, D_MODEL // 2))


def _experts_kernel(be_ref, nu_ref, nxt_ref, x_ref, wg_hbm, wu_hbm, wd_hbm, y_ref,
                    sg_ref, su_ref, sd_ref, wgb_ref, wub_ref, wdb_ref, sems):
    b = pl.program_id(0)
    active = b < nu_ref[0]
    new_expert = jnp.logical_or(b == 0, be_ref[b] != be_ref[jnp.maximum(b - 1, 0)])

    def weight_copies(e):
        return (pltpu.make_async_copy(wg_hbm.at[e], sg_ref, sems.at[0]),
                pltpu.make_async_copy(wu_hbm.at[e], su_ref, sems.at[1]),
                pltpu.make_async_copy(wd_hbm.at[e], sd_ref, sems.at[2]))

    @pl.when(b == 0)
    def _():
        for cp in weight_copies(be_ref[0]):
            cp.start()

    @pl.when(jnp.logical_and(active, new_expert))
    def _():
        for cp in weight_copies(be_ref[b]):
            cp.wait()
        wgb_ref[...] = sg_ref[...].astype(BF16)
        wub_ref[...] = su_ref[...].astype(BF16)
        wdb_ref[...] = sd_ref[...].astype(BF16)

        @pl.when(nxt_ref[b] >= 0)
        def _():
            for cp in weight_copies(nxt_ref[b]):
                cp.start()

    @pl.when(active)
    def _():
        half = D_MODEL // 2
        xw = x_ref[...]
        xa = lax.bitcast_convert_type(xw << 16, F32).astype(BF16)
        xb = lax.bitcast_convert_type(xw & jnp.uint32(0xFFFF0000), F32).astype(BF16)
        g = jnp.dot(xa, wgb_ref[:half, :], preferred_element_type=F32)
        g = g + jnp.dot(xb, wgb_ref[half:, :], preferred_element_type=F32)
        u = jnp.dot(xa, wub_ref[:half, :], preferred_element_type=F32)
        u = u + jnp.dot(xb, wub_ref[half:, :], preferred_element_type=F32)
        hmid = (g * jax.nn.sigmoid(g)) * u
        y_ref[...] = jnp.dot(hmid.astype(BF16), wdb_ref[...], preferred_element_type=F32)

    @pl.when(b >= nu_ref[0])
    def _():
        y_ref[...] = jnp.zeros(y_ref.shape, y_ref.dtype)


def _experts(block_e, n_used, next_e, x_sorted, w_gate, w_up, w_down):
    p = x_sorted.shape[0]
    nb = p // MOE_BLOCK

    def xrow(b, be, nu, nxt):
        return (jnp.maximum(jnp.minimum(b, nu[0] - 1), 0), 0)

    grid_spec = pltpu.PrefetchScalarGridSpec(
        num_scalar_prefetch=3,
        grid=(nb,),
        in_specs=[
            pl.BlockSpec((MOE_BLOCK, D_MODEL // 2), xrow),
            pl.BlockSpec(memory_space=pl.ANY),
            pl.BlockSpec(memory_space=pl.ANY),
            pl.BlockSpec(memory_space=pl.ANY),
        ],
        out_specs=pl.BlockSpec((MOE_BLOCK, D_MODEL), lambda b, be, nu, nxt: (b, 0)),
        scratch_shapes=[pltpu.VMEM((D_MODEL, D_FF), F32), pltpu.VMEM((D_MODEL, D_FF), F32),
                        pltpu.VMEM((D_FF, D_MODEL), F32),
                        pltpu.VMEM((D_MODEL, D_FF), BF16), pltpu.VMEM((D_MODEL, D_FF), BF16),
                        pltpu.VMEM((D_FF, D_MODEL), BF16),
                        pltpu.SemaphoreType.DMA((3,))],
    )
    return pl.pallas_call(
        _experts_kernel,
        grid_spec=grid_spec,
        out_shape=jax.ShapeDtypeStruct((p, D_MODEL), F32),
        compiler_params=pltpu.CompilerParams(dimension_semantics=("arbitrary",),
                                             vmem_limit_bytes=VMEM_LIMIT),
        name="experts",
    )(block_e, n_used, next_e, x_sorted, w_gate, w_up, w_down)


def _combine_kernel(d0_ref, d1_ref, h_ref, mf_ref, gfin_ref, y_hbm, outp_ref, outs_ref, y0_ref, y1_ref, sems,
                    *, n_tiles, n_prompt_tiles):
    i = pl.program_id(0)

    def start_group(tile, slot, g):
        for u in range(SUBLANES):
            r = tile * TM + g * SUBLANES + u
            pltpu.make_async_copy(y_hbm.at[pl.ds(d0_ref[r], 1)], y0_ref.at[slot, g, pl.ds(u, 1)],
                                  sems.at[slot]).start()
            pltpu.make_async_copy(y_hbm.at[pl.ds(d1_ref[r], 1)], y1_ref.at[slot, g, pl.ds(u, 1)],
                                  sems.at[slot]).start()

    @pl.when(i == 0)
    def _():
        def body(g, c):
            start_group(0, 0, g)
            return c
        lax.fori_loop(0, TM // SUBLANES, body, 0)

    slot = i % 2
    for _ in range(2):
        pltpu.make_async_copy(y_hbm.at[pl.ds(0, TM)], y_hbm.at[pl.ds(0, TM)], sems.at[slot]).wait()

    def finish(out_ref, prefetch):
        for g in range(TM // SUBLANES):
            if prefetch:
                start_group(i + 1, 1 - slot, g)
            mf = mf_ref[g]
            ffn = mf[:, 0:1] * y0_ref[slot, g] + mf[:, 1:2] * y1_ref[slot, g]
            out_ref[g] = _rms(h_ref[g] + ffn, gfin_ref[0])

    @pl.when(i < n_prompt_tiles)
    def _():
        finish(outp_ref, True)

    @pl.when(jnp.logical_and(i >= n_prompt_tiles, i + 1 < n_tiles))
    def _():
        finish(outs_ref, True)

    @pl.when(i + 1 == n_tiles)
    def _():
        finish(outs_ref, False)


def _combine(dest0, dest1, h, mf, gfin, y_sorted, *, n_prompt_rows):
    m = h.shape[0]
    npt = n_prompt_rows // TM
    tg = TM // SUBLANES
    grouped = lambda a: a.reshape(a.shape[0] // SUBLANES, SUBLANES, a.shape[1])
    grid_spec = pltpu.PrefetchScalarGridSpec(
        num_scalar_prefetch=2,
        grid=(m // TM,),
        in_specs=[
            pl.BlockSpec((tg, SUBLANES, D_MODEL), lambda i, *_: (i, 0, 0)),
            pl.BlockSpec((tg, SUBLANES, LANES), lambda i, *_: (i, 0, 0)),
            pl.BlockSpec((1, 1, D_MODEL), lambda i, *_: (0, 0, 0)),
            pl.BlockSpec(memory_space=pl.ANY),
        ],
        out_specs=[pl.BlockSpec((tg, SUBLANES, D_MODEL), lambda i, *_: (jnp.minimum(i, npt - 1), 0, 0)),
                   pl.BlockSpec((tg, SUBLANES, D_MODEL), lambda i, *_: (jnp.maximum(i - npt, 0), 0, 0))],
        scratch_shapes=[pltpu.VMEM((2, TM // SUBLANES, SUBLANES, D_MODEL), F32),
                        pltpu.VMEM((2, TM // SUBLANES, SUBLANES, D_MODEL), F32),
                        pltpu.SemaphoreType.DMA((2,))],
    )
    y_p, y_s = pl.pallas_call(
        functools.partial(_combine_kernel, n_tiles=m // TM, n_prompt_tiles=npt),
        grid_spec=grid_spec,
        out_shape=[jax.ShapeDtypeStruct((n_prompt_rows // SUBLANES, SUBLANES, D_MODEL), F32),
                   jax.ShapeDtypeStruct(((m - n_prompt_rows) // SUBLANES, SUBLANES, D_MODEL), F32)],
        compiler_params=pltpu.CompilerParams(dimension_semantics=("arbitrary",),
                                             vmem_limit_bytes=VMEM_LIMIT),
        name="combine",
    )(dest0, dest1, grouped(h), grouped(mf), gfin.reshape(1, 1, D_MODEL), y_sorted)
    return y_p.reshape(n_prompt_rows, D_MODEL), y_s.reshape(m - n_prompt_rows, D_MODEL)


def _rope_tables(pos):
    f32 = np.float32
    inv = np.power(f32(ROPE_THETA), -np.arange(0, ROPE_DIM, 2, dtype=f32) / f32(ROPE_DIM)).astype(f32)
    ang = (pos.astype(f32)[:, None] * inv[None, :]).astype(f32)
    cos, sin = np.cos(ang).astype(f32), np.sin(ang).astype(f32)
    return np.concatenate([cos, cos], axis=-1), np.concatenate([-sin, sin], axis=-1)


def _swap_halves(w):
    return jnp.concatenate([w[..., ROPE_DIM // 2:], w[..., :ROPE_DIM // 2]], axis=-1)


def kernel(x_prompt, x_sample, cache_kv_latent, cache_k_rope, state_conv, norm_mix, w_in, norm_q, w_uq,
           norm_kv, w_uk, w_uv, conv_w, norm_attn_out, norm_conv_out, w_o, norm_ffn, w_router_group,
           b_router_group, w_router_expert, b_router_expert, w_gate, w_up, w_down, norm_final):
    assert w_in.shape[0] == 1, "single-layer trunk"
    bp, seq_p, _ = x_prompt.shape
    bs, seq_s, _ = x_sample.shape
    past_len = cache_kv_latent.shape[2]
    np_rows, ns_rows = bp * seq_p, bs * seq_s
    m = np_rows + ns_rows
    assert seq_p % TM == 0 and TM % seq_s == 0 and ns_rows % TM == 0 and seq_s == CHUNK

    xp = x_prompt.reshape(np_rows, D_MODEL)
    xs = x_sample.reshape(ns_rows, D_MODEL)
    row_vec = lambda v: v.reshape(1, -1)

    assert w_in.shape[2] == Q_LORA + KV_LORA + ROPE_DIM + 3 * CONV_CH
    w_t = jnp.swapaxes(w_in[0], 0, 1).astype(BF16)
    wq4 = w_uq[0].reshape(Q_LORA, N_HEADS, QK_NOPE + ROPE_DIM)
    wq_rope = wq4[:, :, QK_NOPE:]
    w_q = jnp.concatenate([wq4[:, :, :QK_NOPE].reshape(Q_LORA, -1), wq_rope.reshape(Q_LORA, -1),
                           _swap_halves(wq_rope).reshape(Q_LORA, -1)], axis=1).astype(BF16)
    w_ukt = jnp.transpose(w_uk[0], (1, 2, 0)).astype(BF16)
    w_uvh = jnp.transpose(w_uv[0], (1, 0, 2)).astype(BF16)
    w_ob = w_o[0].astype(BF16)
    n_router = N_GROUPS + N_EXPERTS
    w_r = jnp.concatenate([w_router_group[0], w_router_expert[0].reshape(D_MODEL, N_EXPERTS)], axis=1)
    w_r = jnp.pad(w_r, ((0, 0), (0, LANES - n_router)))
    w_rh = w_r.astype(BF16)
    w_rl = (w_r - w_rh.astype(F32)).astype(BF16)
    w_r2 = jnp.concatenate([w_rh, w_rl], axis=1)
    b_r =jnp.pad(jnp.concatenate([b_router_group[0], b_router_expert[0].reshape(N_EXPERTS)]),
                  (0, LANES - n_router)).reshape(1, LANES)

    cos_p, sin_p = _rope_tables(np.arange(seq_p))
    cos_s, sin_s = _rope_tables(past_len + np.arange(seq_s))
    cosk = np.concatenate([cos_p, np.tile(cos_s, (TM // seq_s, 1))], axis=0)
    sink = np.concatenate([sin_p, np.tile(sin_s, (TM // seq_s, 1))], axis=0)
    state = jnp.concatenate([jnp.zeros((bp, CONV_W - 1, CONV_CH), F32), state_conv[0]], axis=0)

    cqn, ckv_p, kr_p, ckv_s, kr_s, conv_n, utail = _in_proj(
        xp, xs, row_vec(norm_mix[0]), w_t, row_vec(norm_q[0]), row_vec(norm_kv[0]),
        row_vec(norm_conv_out[0]), conv_w[0], cosk, sink, state, seq_p=seq_p, seq_s=seq_s)

    gao = row_vec(norm_attn_out[0])
    attn_p = _attention(cqn, w_q, w_ukt, w_uvh, np.tile(cos_p, (1, N_HEADS)), np.tile(sin_p, (1, N_HEADS)),
                        gao, ckv_p, kr_p, n_batch=bp, seq=seq_p, row0=0)
    attn_s = _attention(cqn, w_q, w_ukt, w_uvh, np.tile(cos_s, (1, N_HEADS)), np.tile(sin_s, (1, N_HEADS)),
                        gao, ckv_s, kr_s, n_batch=bs, seq=seq_s, row0=np_rows,
                        past_kv=cache_kv_latent[0], past_kr=jnp.swapaxes(cache_k_rope[0], 1, 2))

    h, xpk, mi, mf, cnt = _out_proj(attn_p, attn_s, conv_n, xp, xs, w_ob, row_vec(norm_ffn[0]),
                                    w_r2, b_r)

    counts = cnt[0, :N_EXPERTS].astype(jnp.int32)
    padded = (counts + MOE_BLOCK - 1) // MOE_BLOCK * MOE_BLOCK
    pad_end = jnp.cumsum(padded)
    pad_start = pad_end - padded
    n_blocks = -(-(m * 2) // MOE_BLOCK) + N_EXPERTS
    block_row0 = jnp.arange(n_blocks, dtype=jnp.int32) * MOE_BLOCK
    block_e = jnp.minimum(jnp.sum((pad_end[None, :] <= block_row0[:, None]).astype(jnp.int32), axis=1),
                          N_EXPERTS - 1)
    n_used = (pad_end[-1:] // MOE_BLOCK).astype(jnp.int32)
    expert_ids = jnp.arange(N_EXPERTS, dtype=jnp.int32)[:, None]

    def seg_start(e):
        return jnp.sum(jnp.where(expert_ids == e[None, :], pad_start[:, None], 0), axis=0)

    dest0 = seg_start(mi[0]) + mi[2]
    dest1 = seg_start(mi[1]) + mi[3]

    x_sorted = _dispatch(dest0, dest1, pad_start + counts, padded - counts, n_used, xpk, n_blocks)
    later = (expert_ids.T > block_e[:, None]) & (padded > 0)[None, :]
    next_e = jnp.min(jnp.where(later, expert_ids.T, N_EXPERTS), axis=1)
    next_e = jnp.where(next_e == N_EXPERTS, -1, next_e).astype(jnp.int32)
    y_sorted = _experts(block_e, n_used, next_e, x_sorted, w_gate[0], w_up[0], w_down[0])
    gfin = row_vec(norm_final)
    y_p, y_s = _combine(dest0, dest1, h, mf, gfin, y_sorted, n_prompt_rows=np_rows)

    ut = utail.reshape(m // CHUNK, SUBLANES, CONV_CH)
    tails = ut[:, SUBLANES - (CONV_W - 1):, :]
    p_last = (jnp.arange(bp) + 1) * (seq_p // CHUNK) - 1
    s_last = np_rows // CHUNK + (jnp.arange(bs) + 1) * (seq_s // CHUNK) - 1
    return (y_p.reshape(bp, seq_p, D_MODEL),
            y_s.reshape(bs, seq_s, D_MODEL),
            ckv_p.reshape(1, bp, seq_p, KV_LORA),
            jnp.swapaxes(kr_p, 1, 2)[None],
            tails[p_last][None],
            ckv_s.reshape(1, bs, seq_s, KV_LORA),
            jnp.swapaxes(kr_s, 1, 2)[None],
            tails[s_last][None])
```

```python
import functools

import jax
import jax.numpy as jnp
import numpy as np
from jax import lax
from jax.experimental import pallas as pl
from jax.experimental.pallas import tpu as pltpu

F32 = jnp.float32
BF16 = jnp.bfloat16

D_MODEL = 2048
N_HEADS = 8
QK_NOPE = 128
ROPE_DIM = 64
V_DIM = 128
Q_LORA = 512
KV_LORA = 512
ATTN_W = N_HEADS * V_DIM
CONV_CH = D_MODEL - ATTN_W
CONV_W = 3
CHUNK = 64
N_GROUPS = 4
EXPERTS_PER_GROUP = 8
N_EXPERTS = N_GROUPS * EXPERTS_PER_GROUP
D_FF = 512
ROPE_THETA = 10000.0
EPS = 1e-6
ATTN_SCALE = (QK_NOPE + ROPE_DIM) ** -0.5
EXP2_SCALE = ATTN_SCALE * 1.4426950408889634

LANES = 128
SUBLANES = 8
TM = 256
MOE_BLOCK = 256
TQ = 256
TK = 256
NEG_BIG = -1e30
VMEM_LIMIT = 56 * 1024 * 1024


def _rms(v, g):
    return v * lax.rsqrt(jnp.mean(v * v, axis=-1, keepdims=True) + EPS) * g


def _lane_bcast(v, width):
    if width % LANES == 0:
        return jnp.concatenate([v] * (width // LANES), axis=1)
    assert width < LANES
    return v[:, :width]


def _const_spec(shape):
    nd = len(shape)
    return pl.BlockSpec(shape, lambda *_: (0,) * nd, pipeline_mode=pl.Buffered(1))


def _in_proj_kernel(xp_ref, xs_ref, gmix_ref, wt_ref, gq_ref, gkv_ref, gco_ref, convw_ref,
                    cos_ref, sin_ref, state_ref,
                    cqn_ref, ckvp_ref, krp_ref, ckvs_ref, krs_ref, convn_ref, utail_ref, ext_ref,
                    *, n_prompt_tiles, tiles_per_seq, n_prompt_seq, sample_seq_len):
    i = pl.program_id(0)

    def conv_block(u_sub, gate_sub, row0, length):
        ext_ref[SUBLANES:SUBLANES + length, :] = u_sub
        um1 = ext_ref[SUBLANES - 1:SUBLANES - 1 + length, :]
        um2 = ext_ref[SUBLANES - 2:SUBLANES - 2 + length, :]
        cw = convw_ref[...]
        conv = cw[0:1] * um2 + cw[1:2] * um1 + cw[2:3] * u_sub
        convn_ref[row0:row0 + length, :] = _rms(gate_sub * conv, gco_ref[...]).astype(BF16)

    def tile(x_ref, is_prompt):
        ckv_ref, krt_ref = (ckvp_ref, krp_ref) if is_prompt else (ckvs_ref, krs_ref)
        x = x_ref[...]
        xn = _rms(x, gmix_ref[...]).astype(BF16)
        lat_w = Q_LORA + KV_LORA
        conv0 = lat_w + ROPE_DIM
        nt = (((1,), (1,)), ((), ()))
        za = lax.dot_general(xn, wt_ref[:lat_w, :], nt, preferred_element_type=F32)
        cqn_ref[...] = _rms(za[:, :Q_LORA], gq_ref[...]).astype(BF16)
        ckv_ref[...] = _rms(za[:, Q_LORA:], gkv_ref[...])
        zk = lax.dot_general(xn, wt_ref[lat_w:conv0, :], nt, preferred_element_type=F32)
        zk_swapped = jnp.concatenate([zk[:, ROPE_DIM // 2:], zk[:, :ROPE_DIM // 2]], axis=1)
        k_rope = zk * cos_ref[...] + zk_swapped * sin_ref[...]
        if is_prompt:
            krt_ref[...] = k_rope.T
        else:
            for k in range(TM // sample_seq_len):
                krt_ref[k] = k_rope[k * sample_seq_len:(k + 1) * sample_seq_len, :].T

        zc = lax.dot_general(xn, wt_ref[conv0:, :], nt, preferred_element_type=F32)
        gate_b = zc[:, :CONV_CH]
        u = zc[:, CONV_CH:2 * CONV_CH] * zc[:, 2 * CONV_CH:]
        for j in range(TM // CHUNK):
            utail_ref[j] = u[CHUNK * (j + 1) - SUBLANES:CHUNK * (j + 1), :]

        if is_prompt:
            first = (i % tiles_per_seq) == 0

            @pl.when(first)
            def _():
                ext_ref[SUBLANES - 2:SUBLANES, :] = state_ref[i // tiles_per_seq]

            @pl.when(jnp.logical_not(first))
            def _():
                ext_ref[SUBLANES - 2:SUBLANES, :] = ext_ref[TM + SUBLANES - 2:TM + SUBLANES, :]

            conv_block(u, gate_b, 0, TM)
        else:
            n_sub = TM // sample_seq_len
            seq0 = n_prompt_seq + (i - n_prompt_tiles) * n_sub
            for k in range(n_sub):
                ext_ref[SUBLANES - 2:SUBLANES, :] = state_ref[seq0 + k]
                lo = k * sample_seq_len
                conv_block(u[lo:lo + sample_seq_len], gate_b[lo:lo + sample_seq_len], lo, sample_seq_len)

    @pl.when(i < n_prompt_tiles)
    def _():
        tile(xp_ref, True)

    @pl.when(i >= n_prompt_tiles)
    def _():
        tile(xs_ref, False)


def _in_proj(xp, xs, gmix, w_t, gq, gkv, gco, convw, cosk, sink, state, *, seq_p, seq_s):
    np_rows, ns_rows = xp.shape[0], xs.shape[0]
    m = np_rows + ns_rows
    npt, nst = np_rows // TM, ns_rows // TM
    tps = seq_p // TM
    n_prompt_seq = np_rows // seq_p
    last_p = npt - 1

    def tab_idx(i):
        return (jnp.where(i < npt, i % tps, tps), 0)

    row = lambda i: (i, 0)
    prow = lambda i: (jnp.minimum(i, last_p), 0)
    srow = lambda i: (jnp.maximum(i - npt, 0), 0)
    kern = functools.partial(_in_proj_kernel, n_prompt_tiles=npt, tiles_per_seq=tps,
                             n_prompt_seq=n_prompt_seq, sample_seq_len=seq_s)
    return pl.pallas_call(
        kern,
        grid=(npt + nst,),
        in_specs=[
            pl.BlockSpec((TM, D_MODEL), prow),
            pl.BlockSpec((TM, D_MODEL), srow),
            _const_spec((1, D_MODEL)),
            _const_spec(w_t.shape),
            _const_spec((1, Q_LORA)),
            _const_spec((1, KV_LORA)),
            _const_spec((1, CONV_CH)),
            _const_spec((CONV_W, CONV_CH)),
            pl.BlockSpec((TM, ROPE_DIM), tab_idx),
            pl.BlockSpec((TM, ROPE_DIM), tab_idx),
            _const_spec(state.shape),
        ],
        out_specs=[
            pl.BlockSpec((TM, Q_LORA), row),
            pl.BlockSpec((TM, KV_LORA), prow),
            pl.BlockSpec((None, ROPE_DIM, TM), lambda i: (jnp.minimum(i, last_p) // tps, 0,
                                                          jnp.minimum(i, last_p) % tps)),
            pl.BlockSpec((TM, KV_LORA), srow),
            pl.BlockSpec((TM // seq_s, ROPE_DIM, seq_s), lambda i: (jnp.maximum(i - npt, 0), 0, 0)),
            pl.BlockSpec((TM, CONV_CH), row),
            pl.BlockSpec((TM // CHUNK, SUBLANES, CONV_CH), lambda i: (i, 0, 0)),
        ],
        out_shape=[
            jax.ShapeDtypeStruct((m, Q_LORA), BF16),
            jax.ShapeDtypeStruct((np_rows, KV_LORA), F32),
            jax.ShapeDtypeStruct((n_prompt_seq, ROPE_DIM, seq_p), F32),
            jax.ShapeDtypeStruct((ns_rows, KV_LORA), F32),
            jax.ShapeDtypeStruct((ns_rows // seq_s, ROPE_DIM, seq_s), F32),
            jax.ShapeDtypeStruct((m, CONV_CH), BF16),
            jax.ShapeDtypeStruct((m // CHUNK, SUBLANES, CONV_CH), F32),
        ],
        scratch_shapes=[pltpu.VMEM((TM + SUBLANES, CONV_CH), F32)],
        compiler_params=pltpu.CompilerParams(dimension_semantics=("arbitrary",),
                                             vmem_limit_bytes=VMEM_LIMIT),
        name="in_proj",
    )(xp, xs, gmix, w_t, gq, gkv, gco, convw, cosk, sink, state)


def _attn_kernel(*refs, tq, n_past, causal):
    refs = list(refs)
    cqn_ref, wq_ref, wuk_ref, wuv_ref, cos_ref, sin_ref, gao_ref = refs[:7]
    refs = refs[7:]
    if n_past:
        pkv_ref, pkr_ref = refs[:2]
        refs = refs[2:]
    kv_ref, kr_ref, out_ref, qlat_ref, qr_ref, m_ref, l_ref, acc_ref, s_ref, klim_ref = refs

    qi = pl.program_id(1)
    rows = N_HEADS * tq

    q = jnp.dot(cqn_ref[...], wq_ref[...], preferred_element_type=F32)
    nope_w = N_HEADS * QK_NOPE
    rope_w = N_HEADS * ROPE_DIM
    qrope = q[:, nope_w:nope_w + rope_w] * cos_ref[...] + q[:, nope_w + rope_w:] * sin_ref[...]
    for h in range(N_HEADS):
        qn = q[:, h * QK_NOPE:(h + 1) * QK_NOPE].astype(BF16)
        ql = jnp.dot(qn, wuk_ref[h], preferred_element_type=F32)
        qlat_ref[h * tq:(h + 1) * tq, :] = ql.astype(BF16)
        qr_ref[h * tq:(h + 1) * tq, :] = qrope[:, h * ROPE_DIM:(h + 1) * ROPE_DIM].astype(BF16)


    nt = (((1,), (1,)), ((), ()))

    def scores(kc_f32, krt_f32):
        s = lax.dot_general(qlat_ref[...], kc_f32.astype(BF16), nt, preferred_element_type=F32)
        return s + jnp.dot(qr_ref[...], krt_f32.astype(BF16), preferred_element_type=F32)

    def update(s, kc_f32, mask, first=False):
        if mask is not None:
            s = jnp.where(mask, s, NEG_BIG)
        m_cur = jnp.max(s, axis=-1, keepdims=True)
        if first:
            m_new = jnp.broadcast_to(m_cur, m_ref.shape)
        else:
            m_prev = m_ref[...]
            m_new = jnp.maximum(m_prev, m_cur)
            alpha = jnp.exp2((m_prev - m_new) * EXP2_SCALE)
        p = jnp.exp2((s - _lane_bcast(m_new, s.shape[1])) * EXP2_SCALE)
        l_cur = jnp.sum(p, axis=-1, keepdims=True)
        pv = jnp.dot(p.astype(BF16), kc_f32.astype(BF16), preferred_element_type=F32)
        if first:
            l_ref[...] = jnp.broadcast_to(l_cur, l_ref.shape)
            acc_ref[...] = pv
        else:
            l_ref[...] = alpha * l_ref[...] + l_cur
            acc_ref[...] = _lane_bcast(alpha, KV_LORA) * acc_ref[...] + pv
        m_ref[...] = m_new

    def pipelined(kv, kr, lo, hi, last, mask_fn):
        def body(j, c):
            k0 = pl.multiple_of(j * TK, TK)
            k1 = pl.multiple_of(jnp.minimum(j + 1, last) * TK, TK)
            s_cur = s_ref[j % 2]
            s_ref[(j + 1) % 2] = scores(kv[pl.ds(k1, TK), :], kr[:, pl.ds(k1, TK)])
            update(s_cur, kv[pl.ds(k0, TK), :], None if mask_fn is None else mask_fn(k0))
            return c
        lax.fori_loop(lo, hi, body, 0)

    def pipelined_pairs(kv, kr, n_pairs, last):
        def body(i, c):
            ka = pl.multiple_of((2 * i + 1) * TK, TK)
            kb = pl.multiple_of((2 * i + 2) * TK, TK)
            kc = pl.multiple_of(jnp.minimum(2 * i + 3, last) * TK, TK)
            s_ref[0] = scores(kv[pl.ds(kb, TK), :], kr[:, pl.ds(kb, TK)])
            update(s_ref[1], kv[pl.ds(ka, TK), :], None)
            s_ref[1] = scores(kv[pl.ds(kc, TK), :], kr[:, pl.ds(kc, TK)])
            update(s_ref[0], kv[pl.ds(kb, TK), :], None)
            return c
        lax.fori_loop(0, n_pairs, body, 0)

    def first_block(kv, kr, last, mask):
        k1 = pl.multiple_of(jnp.minimum(1, last) * TK, TK)
        s_ref[0] = scores(kv[pl.ds(0, TK), :], kr[:, pl.ds(0, TK)])
        s_ref[1] = scores(kv[pl.ds(k1, TK), :], kr[:, pl.ds(k1, TK)])
        update(s_ref[0], kv[pl.ds(0, TK), :], mask, first=True)

    if n_past:
        n_pb = n_past // TK
        first_block(pkv_ref, pkr_ref, n_pb - 1, None)
        n_pairs = (n_pb - 1) // 2
        pipelined_pairs(pkv_ref, pkr_ref, n_pairs, n_pb - 1)
        if 1 + 2 * n_pairs < n_pb:
            pipelined(pkv_ref, pkr_ref, 1 + 2 * n_pairs, n_pb, n_pb - 1, None)

    if causal:
        n_blocks = ((qi + 1) * tq + TK - 1) // TK
        n_full = jnp.minimum((qi * tq // CHUNK + 1) * CHUNK // TK, n_blocks)

        assert tq & (tq - 1) == 0 and CHUNK & (CHUNK - 1) == 0
        r = lax.broadcasted_iota(jnp.int32, (rows, LANES), 0)
        q_pos = qi * tq + (r & (tq - 1))
        klim_ref[...] = (q_pos & ~(CHUNK - 1)) + CHUNK

        def mask_fn(k0):
            cidx = lax.broadcasted_iota(jnp.int32, (rows, TK), 1)
            return cidx < _lane_bcast(klim_ref[...] - k0, TK)

        first_block(kv_ref, kr_ref, n_blocks - 1, mask_fn(0))
        n_pairs = jnp.maximum(n_full - 1, 0) // 2
        pipelined_pairs(kv_ref, kr_ref, n_pairs, n_blocks - 1)
        pipelined(kv_ref, kr_ref, 1 + 2 * n_pairs, n_full, n_blocks - 1, None)
        pipelined(kv_ref, kr_ref, jnp.maximum(n_full, 1), n_blocks, n_blocks - 1, mask_fn)
    else:
        update(scores(kv_ref[...], kr_ref[...]), kv_ref[...], None)

    o = acc_ref[...] / _lane_bcast(l_ref[...], KV_LORA)
    parts = []
    for h in range(N_HEADS):
        oh = o[h * tq:(h + 1) * tq, :].astype(BF16)
        parts.append(jnp.dot(oh, wuv_ref[h], preferred_element_type=F32))
    attn = jnp.concatenate(parts, axis=-1)
    out_ref[...] = _rms(attn, gao_ref[...]).astype(BF16)


def _attention(cqn, w_q, w_ukt, w_uv, cosq, sinq, gao, ckv, krope, *, n_batch, seq, row0,
               past_kv=None, past_kr=None):
    causal = past_kv is None
    tq = TQ if causal else seq
    nq = seq // tq
    n_past = 0 if causal else past_kv.shape[1]
    if not causal:
        assert n_past % CHUNK == 0 and seq <= CHUNK and n_past % TK == 0
    blk0 = row0 // tq
    qrow = lambda b, q: (blk0 + b * nq + q, 0)
    in_specs = [
        pl.BlockSpec((tq, Q_LORA), qrow),
        _const_spec(w_q.shape),
        _const_spec(w_ukt.shape),
        _const_spec(w_uv.shape),
        pl.BlockSpec((tq, N_HEADS * ROPE_DIM), lambda b, q: (q, 0)),
        pl.BlockSpec((tq, N_HEADS * ROPE_DIM), lambda b, q: (q, 0)),
        _const_spec((1, ATTN_W)),
    ]
    args = [cqn, w_q, w_ukt, w_uv, cosq, sinq, gao]
    if n_past:
        in_specs += [pl.BlockSpec((None, n_past, KV_LORA), lambda b, q: (b, 0, 0)),
                     pl.BlockSpec((None, ROPE_DIM, n_past), lambda b, q: (b, 0, 0))]
        args += [past_kv, past_kr]
    in_specs += [pl.BlockSpec((seq, KV_LORA), lambda b, q: (b, 0)),
                 pl.BlockSpec((None, ROPE_DIM, seq), lambda b, q: (b, 0, 0))]
    args += [ckv, krope]
    rows = N_HEADS * tq
    kern = functools.partial(_attn_kernel, tq=tq, n_past=n_past, causal=causal)
    return pl.pallas_call(
        kern,
        grid=(n_batch, nq),
        in_specs=in_specs,
        out_specs=pl.BlockSpec((tq, ATTN_W), lambda b, q: (b * nq + q, 0)),
        out_shape=jax.ShapeDtypeStruct((n_batch * seq, ATTN_W), BF16),
        scratch_shapes=[
            pltpu.VMEM((rows, KV_LORA), BF16),
            pltpu.VMEM((rows, ROPE_DIM), BF16),
            pltpu.VMEM((rows, LANES), F32),
            pltpu.VMEM((rows, LANES), F32),
            pltpu.VMEM((rows, KV_LORA), F32),
            pltpu.VMEM((2, rows, TK), F32),
            pltpu.VMEM((rows, LANES), jnp.int32),
        ],
        compiler_params=pltpu.CompilerParams(dimension_semantics=("arbitrary", "arbitrary"),
                                             vmem_limit_bytes=VMEM_LIMIT),
        name="attn_prompt" if causal else "attn_sample",
    )(*args)


def _out_proj_kernel(attnp_ref, attns_ref, convn_ref, xp_ref, xs_ref, wo_ref, gffn_ref, wr_ref,
                     br_ref, h_ref, xpk_ref, mi_ref, mf_ref, cnt_ref, carry_ref, logit_ref, *, n_prompt_tiles):
    i = pl.program_id(0)

    @pl.when(i == 0)
    def _():
        carry_ref[...] = jnp.zeros(carry_ref.shape, F32)
        logit_ref[...] = jnp.zeros(logit_ref.shape, F32)

    def tile(x_ref, attn_ref):
        prev_logits = logit_ref[...]
        y = jnp.dot(attn_ref[...], wo_ref[:ATTN_W, :], preferred_element_type=F32)
        y = y + jnp.dot(convn_ref[...], wo_ref[ATTN_W:, :], preferred_element_type=F32)
        h = x_ref[...] + y
        h_ref[...] = h
        xn = _rms(h, gffn_ref[...])

        half = D_MODEL // 2
        xh = xn.astype(BF16)
        xh32 = xh.astype(F32)
        lo = lax.bitcast_convert_type(xh32[:, :half], jnp.uint32)
        hi = lax.bitcast_convert_type(xh32[:, half:], jnp.uint32)
        xpk_ref[...] = (lo >> 16) | (hi & jnp.uint32(0xFFFF0000))

        xl = (xn - xh32).astype(BF16)
        hh_hl = jnp.dot(xh, wr_ref[...], preferred_element_type=F32)
        lh = jnp.dot(xl, wr_ref[:, :LANES], preferred_element_type=F32)
        logit_ref[...] = hh_hl[:, :LANES] + (lh + hh_hl[:, LANES:]) + br_ref[...]

        logits = prev_logits
        counted = (i > 0).astype(F32)
        lane = lax.broadcasted_iota(jnp.int32, (TM, LANES), 1).astype(F32)
        ninf = -jnp.inf
        far = float(LANES)

        def first_argmax(v):
            vmax = jnp.max(v, axis=-1, keepdims=True)
            return vmax, jnp.min(jnp.where(v == vmax, lane, far), axis=-1, keepdims=True)

        gl = jnp.where(lane < N_GROUPS, logits, ninf)
        gmax, gidx = first_argmax(gl)
        g_p = 1.0 / jnp.sum(jnp.exp(gl - gmax), axis=-1, keepdims=True)
        e_lo = N_GROUPS + EXPERTS_PER_GROUP * gidx
        el = jnp.where((lane >= e_lo) & (lane < e_lo + EXPERTS_PER_GROUP), logits, ninf)
        e1max, i1 = first_argmax(el)
        z = jnp.sum(jnp.exp(el - e1max), axis=-1, keepdims=True)
        el2 = jnp.where(lane == i1, ninf, el)
        e2max, i2 = first_argmax(el2)
        p1 = 1.0 / z
        p2 = jnp.exp(e2max - e1max) / z
        den = p1 + p2
        g0 = g_p * p1 / den
        g1 = g_p * p2 / den
        e0 = i1 - N_GROUPS
        e1 = i2 - N_GROUPS

        oh0 = lane == e0
        oh1 = lane == e1
        oh = jnp.where(oh0 | oh1, 1.0, 0.0)
        r = lax.broadcasted_iota(jnp.int32, (TM, TM), 0)
        c = lax.broadcasted_iota(jnp.int32, (TM, TM), 1)
        ltri = jnp.where(r > c, 1.0, 0.0).astype(BF16)
        before = jnp.dot(ltri, oh.astype(BF16), preferred_element_type=F32) + carry_ref[...]
        rank0 = jnp.sum(jnp.where(oh0, before, 0.0), axis=-1, keepdims=True)
        rank1 = jnp.sum(jnp.where(oh1, before, 0.0), axis=-1, keepdims=True)
        total = carry_ref[...] + counted * jnp.sum(oh, axis=0, keepdims=True)
        carry_ref[...] = total
        cnt_ref[...] = jnp.broadcast_to(total, cnt_ref.shape)

        mi = jnp.where(lane == 0, e0, jnp.where(lane == 1, e1, jnp.where(lane == 2, rank0, rank1)))
        mi_ref[...] = jnp.transpose(mi)[:SUBLANES, :].astype(jnp.int32)
        mf_ref[...] = jnp.where(lane == 0, g0, g1)

    @pl.when(i < n_prompt_tiles)
    def _():
        tile(xp_ref, attnp_ref)

    @pl.when(i >= n_prompt_tiles)
    def _():
        tile(xs_ref, attns_ref)


def _out_proj(attn_p, attn_s, conv_n, xp, xs, w_ob, gffn, w_r2, b_r):
    m = conv_n.shape[0]
    npt = xp.shape[0] // TM
    n_tiles = m // TM
    last_p, last_s, last = npt - 1, n_tiles - npt - 1, n_tiles - 1
    row = lambda i: (jnp.minimum(i, last), 0)
    prow = lambda i: (jnp.minimum(i, last_p), 0)
    srow = lambda i: (jnp.clip(i - npt, 0, last_s), 0)
    lag = lambda i: jnp.maximum(i - 1, 0)
    return pl.pallas_call(
        functools.partial(_out_proj_kernel, n_prompt_tiles=npt),
        grid=(n_tiles + 1,),
        in_specs=[
            pl.BlockSpec((TM, ATTN_W), prow),
            pl.BlockSpec((TM, ATTN_W), srow),
            pl.BlockSpec((TM, CONV_CH), row),
            pl.BlockSpec((TM, D_MODEL), prow),
            pl.BlockSpec((TM, D_MODEL), srow),
            _const_spec(w_ob.shape),
            _const_spec((1, D_MODEL)),
            _const_spec(w_r2.shape),
            _const_spec((1, LANES)),
        ],
        out_specs=[
            pl.BlockSpec((TM, D_MODEL), row),
            pl.BlockSpec((TM, D_MODEL // 2), row),
            pl.BlockSpec((SUBLANES, TM), lambda i: (0, lag(i))),
            pl.BlockSpec((TM, LANES), lambda i: (lag(i), 0)),
            pl.BlockSpec((SUBLANES, LANES), lambda i: (0, 0)),
        ],
        out_shape=[
            jax.ShapeDtypeStruct((m, D_MODEL), F32),
            jax.ShapeDtypeStruct((m, D_MODEL // 2), jnp.uint32),
            jax.ShapeDtypeStruct((SUBLANES, m), jnp.int32),
            jax.ShapeDtypeStruct((m, LANES), F32),
            jax.ShapeDtypeStruct((SUBLANES, LANES), F32),
        ],
        scratch_shapes=[pltpu.VMEM((1, LANES), F32), pltpu.VMEM((TM, LANES), F32)],
        compiler_params=pltpu.CompilerParams(dimension_semantics=("arbitrary",),
                                             vmem_limit_bytes=VMEM_LIMIT),
        name="out_proj",
    )(attn_p, attn_s, conv_n, xp, xs, w_ob, gffn, w_r2, b_r)


def _dispatch_kernel(d0_ref, d1_ref, zlo_ref, zn_ref, nu_ref, xpk_ref, xs_hbm, zeros_ref, sems, *, n_blocks):
    i = pl.program_id(0)
    sem = sems.at[0]
    zsem = sems.at[1]

    def zero_fill(act):
        def per_expert(e, c):
            lo = zlo_ref[e]
            n = zn_ref[e]
            head = (-lo) & (SUBLANES - 1)
            for r in range(SUBLANES - 1):
                @pl.when(r < head)
                def _(r=r):
                    act(pltpu.make_async_copy(zeros_ref.at[pl.ds(0, 1)], xs_hbm.at[pl.ds(lo + r, 1)], zsem))
            off = lo + head
            rest = n - head
            size = MOE_BLOCK // 2
            while size >= SUBLANES:
                @pl.when((rest & size) != 0)
                def _(off=off, size=size):
                    dst = xs_hbm.at[pl.ds(pl.multiple_of(off, SUBLANES), size)]
                    act(pltpu.make_async_copy(zeros_ref.at[pl.ds(0, size)], dst, zsem))
                off = off + (rest & size)
                size //= 2
            return c

        def per_block(b, c):
            dst = xs_hbm.at[pl.ds(pl.multiple_of(b * MOE_BLOCK, MOE_BLOCK), MOE_BLOCK)]
            act(pltpu.make_async_copy(zeros_ref, dst, zsem))
            return c

        lax.fori_loop(0, N_EXPERTS, per_expert, 0)
        lax.fori_loop(nu_ref[0], n_blocks, per_block, 0)

    @pl.when(i == 0)
    def _():
        zeros_ref[...] = jnp.zeros(zeros_ref.shape, zeros_ref.dtype)
        zero_fill(lambda cp: cp.start())
        zero_fill(lambda cp: cp.wait())

    base = i * TM

    def start(g, c):
        for u in range(SUBLANES):
            r = base + g * SUBLANES + u
            src = xpk_ref.at[g, pl.ds(u, 1)]
            pltpu.make_async_copy(src, xs_hbm.at[pl.ds(d0_ref[r], 1)], sem).start()
            pltpu.make_async_copy(src, xs_hbm.at[pl.ds(d1_ref[r], 1)], sem).start()
        return c

    lax.fori_loop(0, TM // SUBLANES, start, 0)
    for _ in range(2):
        pltpu.make_async_copy(xs_hbm.at[pl.ds(0, TM)], xs_hbm.at[pl.ds(0, TM)], sem).wait()


def _dispatch(dest0, dest1, pad_lo, n_pad, n_used, xpk, n_blocks):
    m = xpk.shape[0]
    grid_spec = pltpu.PrefetchScalarGridSpec(
        num_scalar_prefetch=5,
        grid=(m // TM,),
        in_specs=[pl.BlockSpec((TM // SUBLANES, SUBLANES, D_MODEL // 2), lambda i, *_: (i, 0, 0))],
        out_specs=pl.BlockSpec(memory_space=pl.ANY),
        scratch_shapes=[pltpu.VMEM((MOE_BLOCK, D_MODEL // 2), jnp.uint32),
                        pltpu.SemaphoreType.DMA((2,))],
    )
    return pl.pallas_call(
        functools.partial(_dispatch_kernel, n_blocks=n_blocks),
        grid_spec=grid_spec,
        out_shape=jax.ShapeDtypeStruct((n_blocks * MOE_BLOCK, D_MODEL // 2), jnp.uint32),
        compiler_params=pltpu.CompilerParams(dimension_semantics=("arbitrary",)),
        name="dispatch",
    )(dest0, dest1, pad_lo, n_pad, n_used, xpk.reshape(m // SUBLANES, SUBLANES, D_MODEL // 2))


def _experts_kernel(be_ref, nu_ref, nxt_ref, slot_ref, x_ref, wg_hbm, wu_hbm, wd_hbm, y_ref,
                    sg_ref, su_ref, sd_ref, wgb_ref, wub_ref, wdb_ref, sems):
    b = pl.program_id(0)
    active = b < nu_ref[0]
    new_expert = jnp.logical_or(b == 0, be_ref[b] != be_ref[jnp.maximum(b - 1, 0)])

    def weight_copies(e, slot):
        return (pltpu.make_async_copy(wg_hbm.at[e], sg_ref.at[slot], sems.at[slot, 0]),
                pltpu.make_async_copy(wu_hbm.at[e], su_ref.at[slot], sems.at[slot, 1]),
                pltpu.make_async_copy(wd_hbm.at[e], sd_ref.at[slot], sems.at[slot, 2]))

    @pl.when(b == 0)
    def _():
        for cp in weight_copies(be_ref[0], slot_ref[0]):
            cp.start()

    @pl.when(jnp.logical_and(active, new_expert))
    def _():
        slot = slot_ref[b]
        for cp in weight_copies(be_ref[b], slot):
            cp.wait()

        @pl.when(nxt_ref[b] >= 0)
        def _():
            for cp in weight_copies(nxt_ref[b], 1 - slot):
                cp.start()

        wgb_ref[...] = sg_ref[slot].astype(BF16)
        wub_ref[...] = su_ref[slot].astype(BF16)
        wdb_ref[...] = sd_ref[slot].astype(BF16)

    @pl.when(active)
    def _():
        half = D_MODEL // 2
        xw = x_ref[...]
        xa = lax.bitcast_convert_type(xw << 16, F32).astype(BF16)
        xb = lax.bitcast_convert_type(xw & jnp.uint32(0xFFFF0000), F32).astype(BF16)
        g = jnp.dot(xa, wgb_ref[:half, :], preferred_element_type=F32)
        g = g + jnp.dot(xb, wgb_ref[half:, :], preferred_element_type=F32)
        u = jnp.dot(xa, wub_ref[:half, :], preferred_element_type=F32)
        u = u + jnp.dot(xb, wub_ref[half:, :], preferred_element_type=F32)
        hmid = (g * jax.nn.sigmoid(g)) * u
        y_ref[...] = jnp.dot(hmid.astype(BF16), wdb_ref[...], preferred_element_type=F32)

    @pl.when(b >= nu_ref[0])
    def _():
        y_ref[...] = jnp.zeros(y_ref.shape, y_ref.dtype)


def _experts(block_e, n_used, next_e, stage_slot, x_sorted, w_gate, w_up, w_down):
    p = x_sorted.shape[0]
    nb = p // MOE_BLOCK

    def xrow(b, be, nu, *_):
        return (jnp.maximum(jnp.minimum(b, nu[0] - 1), 0), 0)

    grid_spec = pltpu.PrefetchScalarGridSpec(
        num_scalar_prefetch=4,
        grid=(nb,),
        in_specs=[
            pl.BlockSpec((MOE_BLOCK, D_MODEL // 2), xrow),
            pl.BlockSpec(memory_space=pl.ANY),
            pl.BlockSpec(memory_space=pl.ANY),
            pl.BlockSpec(memory_space=pl.ANY),
        ],
        out_specs=pl.BlockSpec((MOE_BLOCK, D_MODEL), lambda b, *_: (b, 0)),
        scratch_shapes=[pltpu.VMEM((2, D_MODEL, D_FF), F32), pltpu.VMEM((2, D_MODEL, D_FF), F32),
                        pltpu.VMEM((2, D_FF, D_MODEL), F32),
                        pltpu.VMEM((D_MODEL, D_FF), BF16), pltpu.VMEM((D_MODEL, D_FF), BF16),
                        pltpu.VMEM((D_FF, D_MODEL), BF16),
                        pltpu.SemaphoreType.DMA((2, 3))],
    )
    return pl.pallas_call(
        _experts_kernel,
        grid_spec=grid_spec,
        out_shape=jax.ShapeDtypeStruct((p, D_MODEL), F32),
        compiler_params=pltpu.CompilerParams(dimension_semantics=("arbitrary",),
                                             vmem_limit_bytes=VMEM_LIMIT),
        name="experts",
    )(block_e, n_used, next_e, stage_slot, x_sorted, w_gate, w_up, w_down)


def _combine_kernel(d0_ref, d1_ref, h_ref, mf_ref, gfin_ref, y_hbm, outp_ref, outs_ref, y0_ref, y1_ref, sems,
                    *, n_tiles, n_prompt_tiles):
    i = pl.program_id(0)

    def gather(tile, slot, act):
        base = tile * TM

        def body(g, c):
            for u in range(SUBLANES):
                r = base + g * SUBLANES + u
                act(pltpu.make_async_copy(y_hbm.at[pl.ds(d0_ref[r], 1)], y0_ref.at[slot, g, pl.ds(u, 1)],
                                          sems.at[slot]))
                act(pltpu.make_async_copy(y_hbm.at[pl.ds(d1_ref[r], 1)], y1_ref.at[slot, g, pl.ds(u, 1)],
                                          sems.at[slot]))
            return c
        lax.fori_loop(0, TM // SUBLANES, body, 0)

    @pl.when(i == 0)
    def _():
        gather(0, 0, lambda cp: cp.start())

    @pl.when(i + 1 < n_tiles)
    def _():
        gather(i + 1, (i + 1) % 2, lambda cp: cp.start())

    slot = i % 2
    for _ in range(2):
        pltpu.make_async_copy(y_hbm.at[pl.ds(0, TM)], y_hbm.at[pl.ds(0, TM)], sems.at[slot]).wait()

    def finish(out_ref):
        mf = mf_ref[...]
        ffn = mf[:, :, 0:1] * y0_ref[slot] + mf[:, :, 1:2] * y1_ref[slot]
        out_ref[...] = _rms(h_ref[...] + ffn, gfin_ref[...])

    @pl.when(i < n_prompt_tiles)
    def _():
        finish(outp_ref)

    @pl.when(i >= n_prompt_tiles)
    def _():
        finish(outs_ref)


def _combine(dest0, dest1, h, mf, gfin, y_sorted, *, n_prompt_rows):
    m = h.shape[0]
    npt = n_prompt_rows // TM
    tg = TM // SUBLANES
    grouped = lambda a: a.reshape(a.shape[0] // SUBLANES, SUBLANES, a.shape[1])
    grid_spec = pltpu.PrefetchScalarGridSpec(
        num_scalar_prefetch=2,
        grid=(m // TM,),
        in_specs=[
            pl.BlockSpec((tg, SUBLANES, D_MODEL), lambda i, *_: (i, 0, 0)),
            pl.BlockSpec((tg, SUBLANES, LANES), lambda i, *_: (i, 0, 0)),
            pl.BlockSpec((1, 1, D_MODEL), lambda i, *_: (0, 0, 0)),
            pl.BlockSpec(memory_space=pl.ANY),
        ],
        out_specs=[pl.BlockSpec((tg, SUBLANES, D_MODEL), lambda i, *_: (jnp.minimum(i, npt - 1), 0, 0)),
                   pl.BlockSpec((tg, SUBLANES, D_MODEL), lambda i, *_: (jnp.maximum(i - npt, 0), 0, 0))],
        scratch_shapes=[pltpu.VMEM((2, TM // SUBLANES, SUBLANES, D_MODEL), F32),
                        pltpu.VMEM((2, TM // SUBLANES, SUBLANES, D_MODEL), F32),
                        pltpu.SemaphoreType.DMA((2,))],
    )
    y_p, y_s = pl.pallas_call(
        functools.partial(_combine_kernel, n_tiles=m // TM, n_prompt_tiles=npt),
        grid_spec=grid_spec,
        out_shape=[jax.ShapeDtypeStruct((n_prompt_rows // SUBLANES, SUBLANES, D_MODEL), F32),
                   jax.ShapeDtypeStruct(((m - n_prompt_rows) // SUBLANES, SUBLANES, D_MODEL), F32)],
        compiler_params=pltpu.CompilerParams(dimension_semantics=("arbitrary",),
                                             vmem_limit_bytes=VMEM_LIMIT),
        name="combine",
    )(dest0, dest1, grouped(h), grouped(mf), gfin.reshape(1, 1, D_MODEL), y_sorted)
    return y_p.reshape(n_prompt_rows, D_MODEL), y_s.reshape(m - n_prompt_rows, D_MODEL)


def _rope_tables(pos):
    f32 = np.float32
    inv = np.power(f32(ROPE_THETA), -np.arange(0, ROPE_DIM, 2, dtype=f32) / f32(ROPE_DIM)).astype(f32)
    ang = (pos.astype(f32)[:, None] * inv[None, :]).astype(f32)
    cos, sin = np.cos(ang).astype(f32), np.sin(ang).astype(f32)
    return np.concatenate([cos, cos], axis=-1), np.concatenate([-sin, sin], axis=-1)


def _swap_halves(w):
    return jnp.concatenate([w[..., ROPE_DIM // 2:], w[..., :ROPE_DIM // 2]], axis=-1)


def kernel(x_prompt, x_sample, cache_kv_latent, cache_k_rope, state_conv, norm_mix, w_in, norm_q, w_uq,
           norm_kv, w_uk, w_uv, conv_w, norm_attn_out, norm_conv_out, w_o, norm_ffn, w_router_group,
           b_router_group, w_router_expert, b_router_expert, w_gate, w_up, w_down, norm_final):
    assert w_in.shape[0] == 1, "single-layer trunk"
    bp, seq_p, _ = x_prompt.shape
    bs, seq_s, _ = x_sample.shape
    past_len = cache_kv_latent.shape[2]
    np_rows, ns_rows = bp * seq_p, bs * seq_s
    m = np_rows + ns_rows
    assert seq_p % TM == 0 and TM % seq_s == 0 and ns_rows % TM == 0 and seq_s == CHUNK

    xp = x_prompt.reshape(np_rows, D_MODEL)
    xs = x_sample.reshape(ns_rows, D_MODEL)
    row_vec = lambda v: v.reshape(1, -1)

    assert w_in.shape[2] == Q_LORA + KV_LORA + ROPE_DIM + 3 * CONV_CH
    w_t = jnp.swapaxes(w_in[0], 0, 1).astype(BF16)
    wq4 = w_uq[0].reshape(Q_LORA, N_HEADS, QK_NOPE + ROPE_DIM)
    wq_rope = wq4[:, :, QK_NOPE:]
    w_q = jnp.concatenate([wq4[:, :, :QK_NOPE].reshape(Q_LORA, -1), wq_rope.reshape(Q_LORA, -1),
                           _swap_halves(wq_rope).reshape(Q_LORA, -1)], axis=1).astype(BF16)
    w_ukt = jnp.transpose(w_uk[0], (1, 2, 0)).astype(BF16)
    w_uvh = jnp.transpose(w_uv[0], (1, 0, 2)).astype(BF16)
    w_ob = w_o[0].astype(BF16)
    n_router = N_GROUPS + N_EXPERTS
    w_r = jnp.concatenate([w_router_group[0], w_router_expert[0].reshape(D_MODEL, N_EXPERTS)], axis=1)
    w_r = jnp.pad(w_r, ((0, 0), (0, LANES - n_router)))
    w_rh = w_r.astype(BF16)
    w_rl = (w_r - w_rh.astype(F32)).astype(BF16)
    w_r2 = jnp.concatenate([w_rh, w_rl], axis=1)
    b_r =jnp.pad(jnp.concatenate([b_router_group[0], b_router_expert[0].reshape(N_EXPERTS)]),
                  (0, LANES - n_router)).reshape(1, LANES)

    cos_p, sin_p = _rope_tables(np.arange(seq_p))
    cos_s, sin_s = _rope_tables(past_len + np.arange(seq_s))
    cosk = np.concatenate([cos_p, np.tile(cos_s, (TM // seq_s, 1))], axis=0)
    sink = np.concatenate([sin_p, np.tile(sin_s, (TM // seq_s, 1))], axis=0)
    state = jnp.concatenate([jnp.zeros((bp, CONV_W - 1, CONV_CH), F32), state_conv[0]], axis=0)

    cqn, ckv_p, kr_p, ckv_s, kr_s, conv_n, utail = _in_proj(
        xp, xs, row_vec(norm_mix[0]), w_t, row_vec(norm_q[0]), row_vec(norm_kv[0]),
        row_vec(norm_conv_out[0]), conv_w[0], cosk, sink, state, seq_p=seq_p, seq_s=seq_s)

    gao = row_vec(norm_attn_out[0])
    attn_p = _attention(cqn, w_q, w_ukt, w_uvh, np.tile(cos_p, (1, N_HEADS)), np.tile(sin_p, (1, N_HEADS)),
                        gao, ckv_p, kr_p, n_batch=bp, seq=seq_p, row0=0)
    attn_s = _attention(cqn, w_q, w_ukt, w_uvh, np.tile(cos_s, (1, N_HEADS)), np.tile(sin_s, (1, N_HEADS)),
                        gao, ckv_s, kr_s, n_batch=bs, seq=seq_s, row0=np_rows,
                        past_kv=cache_kv_latent[0], past_kr=jnp.swapaxes(cache_k_rope[0], 1, 2))

    h, xpk, mi, mf, cnt = _out_proj(attn_p, attn_s, conv_n, xp, xs, w_ob, row_vec(norm_ffn[0]),
                                    w_r2, b_r)

    counts = cnt[0, :N_EXPERTS].astype(jnp.int32)
    padded = (counts + MOE_BLOCK - 1) // MOE_BLOCK * MOE_BLOCK
    pad_end = jnp.cumsum(padded)
    pad_start = pad_end - padded
    n_blocks = -(-(m * 2) // MOE_BLOCK) + N_EXPERTS
    block_row0 = jnp.arange(n_blocks, dtype=jnp.int32) * MOE_BLOCK
    block_e = jnp.minimum(jnp.sum((pad_end[None, :] <= block_row0[:, None]).astype(jnp.int32), axis=1),
                          N_EXPERTS - 1)
    n_used = (pad_end[-1:] // MOE_BLOCK).astype(jnp.int32)
    expert_ids = jnp.arange(N_EXPERTS, dtype=jnp.int32)[:, None]

    def seg_start(e):
        return jnp.sum(jnp.where(expert_ids == e[None, :], pad_start[:, None], 0), axis=0)

    dest0 = seg_start(mi[0]) + mi[2]
    dest1 = seg_start(mi[1]) + mi[3]

    x_sorted = _dispatch(dest0, dest1, pad_start + counts, padded - counts, n_used, xpk, n_blocks)
    later = (expert_ids.T > block_e[:, None]) & (padded > 0)[None, :]
    next_e = jnp.min(jnp.where(later, expert_ids.T, N_EXPERTS), axis=1)
    next_e = jnp.where(next_e == N_EXPERTS, -1, next_e).astype(jnp.int32)
    owner_rank = jnp.cumsum((padded > 0).astype(jnp.int32)) - 1
    stage_slot = jnp.sum(jnp.where(expert_ids.T == block_e[:, None], owner_rank[None, :], 0), axis=1) % 2
    y_sorted = _experts(block_e, n_used, next_e, stage_slot.astype(jnp.int32), x_sorted,
                        w_gate[0], w_up[0], w_down[0])
    gfin = row_vec(norm_final)
    y_p, y_s = _combine(dest0, dest1, h, mf, gfin, y_sorted, n_prompt_rows=np_rows)

    ut = utail.reshape(m // CHUNK, SUBLANES, CONV_CH)
    tails = ut[:, SUBLANES - (CONV_W - 1):, :]
    p_last = (jnp.arange(bp) + 1) * (seq_p // CHUNK) - 1
    s_last = np_rows // CHUNK + (jnp.arange(bs) + 1) * (seq_s // CHUNK) - 1
    return (y_p.reshape(bp, seq_p, D_MODEL),
            y_s.reshape(bs, seq_s, D_MODEL),
            ckv_p.reshape(1, bp, seq_p, KV_LORA),
            jnp.swapaxes(kr_p, 1, 2)[None],
            tails[p_last][None],
            ckv_s.reshape(1, bs, seq_s, KV_LORA),
            jnp.swapaxes(kr_s, 1, 2)[None],
            tails[s_last][None])
```

```python
import functools

import jax
import jax.numpy as jnp
import numpy as np
from jax import lax
from jax.experimental import pallas as pl
from jax.experimental.pallas import tpu as pltpu

F32 = jnp.float32
BF16 = jnp.bfloat16

D_MODEL = 2048
N_HEADS = 8
QK_NOPE = 128
ROPE_DIM = 64
V_DIM = 128
Q_LORA = 512
KV_LORA = 512
ATTN_W = N_HEADS * V_DIM
CONV_CH = D_MODEL - ATTN_W
CONV_W = 3
CHUNK = 64
N_GROUPS = 4
EXPERTS_PER_GROUP = 8
N_EXPERTS = N_GROUPS * EXPERTS_PER_GROUP
D_FF = 512
ROPE_THETA = 10000.0
EPS = 1e-6
ATTN_SCALE = (QK_NOPE + ROPE_DIM) ** -0.5
EXP2_SCALE = ATTN_SCALE * 1.4426950408889634

LANES = 128
SUBLANES = 8
TM = 256
MOE_BLOCK = 256
TQ = 256
TK = 256
NEG_BIG = -1e30
VMEM_LIMIT = 56 * 1024 * 1024


def _rms(v, g):
    return v * lax.rsqrt(jnp.mean(v * v, axis=-1, keepdims=True) + EPS) * g


def _lane_bcast(v, width):
    if width % LANES == 0:
        return jnp.concatenate([v] * (width // LANES), axis=1)
    assert width < LANES
    return v[:, :width]


def _const_spec(shape):
    nd = len(shape)
    return pl.BlockSpec(shape, lambda *_: (0,) * nd, pipeline_mode=pl.Buffered(1))


def _in_proj_kernel(xp_ref, xs_ref, gmix_ref, wt_ref, gq_ref, gkv_ref, gco_ref, convw_ref,
                    cos_ref, sin_ref, state_ref,
                    cqn_ref, ckvp_ref, krp_ref, ckvs_ref, krs_ref, convn_ref, utail_ref, ext_ref,
                    *, n_prompt_tiles, tiles_per_seq, n_prompt_seq, sample_seq_len):
    i = pl.program_id(0)

    @pl.when(i == 0)
    def _():
        ext_ref[...] = jnp.zeros(ext_ref.shape, F32)

    def conv_block(u_sub, gate_sub, row0, length):
        ext_ref[SUBLANES:SUBLANES + length, :] = u_sub
        um1 = ext_ref[SUBLANES - 1:SUBLANES - 1 + length, :]
        um2 = ext_ref[SUBLANES - 2:SUBLANES - 2 + length, :]
        cw = convw_ref[...]
        conv = cw[0:1] * um2 + cw[1:2] * um1 + cw[2:3] * u_sub
        convn_ref[row0:row0 + length, :] = _rms(gate_sub * conv, gco_ref[...]).astype(BF16)

    def tile(x_ref, is_prompt):
        ckv_ref, krt_ref = (ckvp_ref, krp_ref) if is_prompt else (ckvs_ref, krs_ref)
        x = x_ref[...]
        xg = (x * gmix_ref[...]).astype(BF16)
        inv_rms = lax.rsqrt(jnp.mean(x * x, axis=-1, keepdims=True) + EPS)
        lat_w = Q_LORA + KV_LORA
        conv0 = lat_w + ROPE_DIM
        nt = (((1,), (1,)), ((), ()))

        def project(lo, hi):
            return inv_rms * lax.dot_general(xg, wt_ref[lo:hi, :], nt, preferred_element_type=F32)

        z_ch = project(conv0 + CONV_CH, conv0 + 3 * CONV_CH)
        u = z_ch[:, :CONV_CH] * z_ch[:, CONV_CH:]
        for j in range(TM // CHUNK):
            utail_ref[j] = u[CHUNK * (j + 1) - SUBLANES:CHUNK * (j + 1), :]
        gate_b = project(conv0, conv0 + CONV_CH)

        if is_prompt:
            first = (i % tiles_per_seq) == 0
            carried = ext_ref[TM + SUBLANES - 2:TM + SUBLANES, :]
            ext_ref[SUBLANES - 2:SUBLANES, :] = jnp.where(first, state_ref[i // tiles_per_seq], carried)
            conv_block(u, gate_b, 0, TM)
        else:
            n_sub = TM // sample_seq_len
            seq0 = n_prompt_seq + (i - n_prompt_tiles) * n_sub
            for k in range(n_sub):
                ext_ref[SUBLANES - 2:SUBLANES, :] = state_ref[seq0 + k]
                lo = k * sample_seq_len
                conv_block(u[lo:lo + sample_seq_len], gate_b[lo:lo + sample_seq_len], lo, sample_seq_len)

        zk = project(lat_w, conv0)
        zk_swapped = jnp.concatenate([zk[:, ROPE_DIM // 2:], zk[:, :ROPE_DIM // 2]], axis=1)
        k_rope = zk * cos_ref[...] + zk_swapped * sin_ref[...]
        if is_prompt:
            krt_ref[...] = k_rope.T
        else:
            for k in range(TM // sample_seq_len):
                krt_ref[k] = k_rope[k * sample_seq_len:(k + 1) * sample_seq_len, :].T
        ckv_ref[...] = _rms(project(Q_LORA, lat_w), gkv_ref[...])
        cqn_ref[...] = _rms(project(0, Q_LORA), gq_ref[...]).astype(BF16)

    @pl.when(i < n_prompt_tiles)
    def _():
        tile(xp_ref, True)

    @pl.when(i >= n_prompt_tiles)
    def _():
        tile(xs_ref, False)


def _in_proj(xp, xs, gmix, w_t, gq, gkv, gco, convw, cosk, sink, state, *, seq_p, seq_s):
    np_rows, ns_rows = xp.shape[0], xs.shape[0]
    m = np_rows + ns_rows
    npt, nst = np_rows // TM, ns_rows // TM
    tps = seq_p // TM
    n_prompt_seq = np_rows // seq_p
    last_p = npt - 1

    def tab_idx(i):
        return (jnp.where(i < npt, i % tps, tps), 0)

    row = lambda i: (i, 0)
    prow = lambda i: (jnp.minimum(i, last_p), 0)
    srow = lambda i: (jnp.maximum(i - npt, 0), 0)
    kern = functools.partial(_in_proj_kernel, n_prompt_tiles=npt, tiles_per_seq=tps,
                             n_prompt_seq=n_prompt_seq, sample_seq_len=seq_s)
    return pl.pallas_call(
        kern,
        grid=(npt + nst,),
        in_specs=[
            pl.BlockSpec((TM, D_MODEL), prow),
            pl.BlockSpec((TM, D_MODEL), srow),
            _const_spec((1, D_MODEL)),
            _const_spec(w_t.shape),
            _const_spec((1, Q_LORA)),
            _const_spec((1, KV_LORA)),
            _const_spec((1, CONV_CH)),
            _const_spec((CONV_W, CONV_CH)),
            pl.BlockSpec((TM, ROPE_DIM), tab_idx),
            pl.BlockSpec((TM, ROPE_DIM), tab_idx),
            _const_spec(state.shape),
        ],
        out_specs=[
            pl.BlockSpec((TM, Q_LORA), row),
            pl.BlockSpec((TM, KV_LORA), prow),
            pl.BlockSpec((None, ROPE_DIM, TM), lambda i: (jnp.minimum(i, last_p) // tps, 0,
                                                          jnp.minimum(i, last_p) % tps)),
            pl.BlockSpec((TM, KV_LORA), srow),
            pl.BlockSpec((TM // seq_s, ROPE_DIM, seq_s), lambda i: (jnp.maximum(i - npt, 0), 0, 0)),
            pl.BlockSpec((TM, CONV_CH), row),
            pl.BlockSpec((TM // CHUNK, SUBLANES, CONV_CH), lambda i: (i, 0, 0)),
        ],
        out_shape=[
            jax.ShapeDtypeStruct((m, Q_LORA), BF16),
            jax.ShapeDtypeStruct((np_rows, KV_LORA), F32),
            jax.ShapeDtypeStruct((n_prompt_seq, ROPE_DIM, seq_p), F32),
            jax.ShapeDtypeStruct((ns_rows, KV_LORA), F32),
            jax.ShapeDtypeStruct((ns_rows // seq_s, ROPE_DIM, seq_s), F32),
            jax.ShapeDtypeStruct((m, CONV_CH), BF16),
            jax.ShapeDtypeStruct((m // CHUNK, SUBLANES, CONV_CH), F32),
        ],
        scratch_shapes=[pltpu.VMEM((TM + SUBLANES, CONV_CH), F32)],
        compiler_params=pltpu.CompilerParams(dimension_semantics=("arbitrary",),
                                             vmem_limit_bytes=VMEM_LIMIT),
        name="in_proj",
    )(xp, xs, gmix, w_t, gq, gkv, gco, convw, cosk, sink, state)


def _attn_kernel(*refs, tq, n_past, causal):
    refs = list(refs)
    cqn_ref, wq_ref, wuk_ref, wuv_ref, cos_ref, sin_ref, gao_ref = refs[:7]
    refs = refs[7:]
    if n_past:
        pkv_ref, pkr_ref = refs[:2]
        refs = refs[2:]
    kv_ref, kr_ref, out_ref, qlat_ref, qr_ref, m_ref, l_ref, acc_ref, s_ref, klim_ref = refs

    qi = pl.program_id(1)
    rows = N_HEADS * tq

    q = jnp.dot(cqn_ref[...], wq_ref[...], preferred_element_type=F32)
    nope_w = N_HEADS * QK_NOPE
    rope_w = N_HEADS * ROPE_DIM
    qrope = q[:, nope_w:nope_w + rope_w] * cos_ref[...] + q[:, nope_w + rope_w:] * sin_ref[...]
    for h in range(N_HEADS):
        qn = q[:, h * QK_NOPE:(h + 1) * QK_NOPE].astype(BF16)
        ql = jnp.dot(qn, wuk_ref[h], preferred_element_type=F32)
        qlat_ref[h * tq:(h + 1) * tq, :] = ql.astype(BF16)
        qr_ref[h * tq:(h + 1) * tq, :] = qrope[:, h * ROPE_DIM:(h + 1) * ROPE_DIM].astype(BF16)


    nt = (((1,), (1,)), ((), ()))

    def scores(kc_f32, krt_f32):
        s = lax.dot_general(qlat_ref[...], kc_f32.astype(BF16), nt, preferred_element_type=F32)
        return s + jnp.dot(qr_ref[...], krt_f32.astype(BF16), preferred_element_type=F32)

    def update(s, kc_f32, mask, first=False):
        if mask is not None:
            s = jnp.where(mask, s, NEG_BIG)
        m_cur = jnp.max(s, axis=-1, keepdims=True)
        if first:
            m_new = jnp.broadcast_to(m_cur, m_ref.shape)
        else:
            m_prev = m_ref[...]
            m_new = jnp.maximum(m_prev, m_cur)
            alpha = jnp.exp2((m_prev - m_new) * EXP2_SCALE)
        p = jnp.exp2((s - _lane_bcast(m_new, s.shape[1])) * EXP2_SCALE)
        l_cur = jnp.sum(p, axis=-1, keepdims=True)
        pv = jnp.dot(p.astype(BF16), kc_f32.astype(BF16), preferred_element_type=F32)
        if first:
            l_ref[...] = jnp.broadcast_to(l_cur, l_ref.shape)
            acc_ref[...] = pv
        else:
            l_ref[...] = alpha * l_ref[...] + l_cur
            acc_ref[...] = _lane_bcast(alpha, KV_LORA) * acc_ref[...] + pv
        m_ref[...] = m_new

    def pipelined(kv, kr, lo, hi, last, mask_fn):
        def body(j, c):
            k0 = pl.multiple_of(j * TK, TK)
            k1 = pl.multiple_of(jnp.minimum(j + 1, last) * TK, TK)
            s_cur = s_ref[j % 2]
            s_ref[(j + 1) % 2] = scores(kv[pl.ds(k1, TK), :], kr[:, pl.ds(k1, TK)])
            update(s_cur, kv[pl.ds(k0, TK), :], None if mask_fn is None else mask_fn(k0))
            return c
        lax.fori_loop(lo, hi, body, 0)

    def pipelined_pairs(kv, kr, n_pairs, last):
        def body(i, c):
            ka = pl.multiple_of((2 * i + 1) * TK, TK)
            kb = pl.multiple_of((2 * i + 2) * TK, TK)
            kc = pl.multiple_of(jnp.minimum(2 * i + 3, last) * TK, TK)
            s_ref[0] = scores(kv[pl.ds(kb, TK), :], kr[:, pl.ds(kb, TK)])
            update(s_ref[1], kv[pl.ds(ka, TK), :], None)
            s_ref[1] = scores(kv[pl.ds(kc, TK), :], kr[:, pl.ds(kc, TK)])
            update(s_ref[0], kv[pl.ds(kb, TK), :], None)
            return c
        lax.fori_loop(0, n_pairs, body, 0)

    def first_block(kv, kr, last, mask):
        k1 = pl.multiple_of(jnp.minimum(1, last) * TK, TK)
        s_ref[0] = scores(kv[pl.ds(0, TK), :], kr[:, pl.ds(0, TK)])
        s_ref[1] = scores(kv[pl.ds(k1, TK), :], kr[:, pl.ds(k1, TK)])
        update(s_ref[0], kv[pl.ds(0, TK), :], mask, first=True)

    if n_past:
        n_pb = n_past // TK
        first_block(pkv_ref, pkr_ref, n_pb - 1, None)
        n_pairs = (n_pb - 1) // 2
        pipelined_pairs(pkv_ref, pkr_ref, n_pairs, n_pb - 1)
        if 1 + 2 * n_pairs < n_pb:
            pipelined(pkv_ref, pkr_ref, 1 + 2 * n_pairs, n_pb, n_pb - 1, None)

    if causal:
        n_blocks = ((qi + 1) * tq + TK - 1) // TK
        n_full = jnp.minimum((qi * tq // CHUNK + 1) * CHUNK // TK, n_blocks)

        assert tq & (tq - 1) == 0 and CHUNK & (CHUNK - 1) == 0
        r = lax.broadcasted_iota(jnp.int32, (rows, LANES), 0)
        q_pos = qi * tq + (r & (tq - 1))
        klim_ref[...] = (q_pos & ~(CHUNK - 1)) + CHUNK

        def mask_fn(k0):
            cidx = lax.broadcasted_iota(jnp.int32, (rows, TK), 1)
            return cidx < _lane_bcast(klim_ref[...] - k0, TK)

        first_block(kv_ref, kr_ref, n_blocks - 1, mask_fn(0))
        n_pairs = jnp.maximum(n_full - 1, 0) // 2
        pipelined_pairs(kv_ref, kr_ref, n_pairs, n_blocks - 1)
        pipelined(kv_ref, kr_ref, 1 + 2 * n_pairs, n_full, n_blocks - 1, None)
        pipelined(kv_ref, kr_ref, jnp.maximum(n_full, 1), n_blocks, n_blocks - 1, mask_fn)
    else:
        update(scores(kv_ref[...], kr_ref[...]), kv_ref[...], None)

    o = acc_ref[...] / _lane_bcast(l_ref[...], KV_LORA)
    parts = []
    for h in range(N_HEADS):
        oh = o[h * tq:(h + 1) * tq, :].astype(BF16)
        parts.append(jnp.dot(oh, wuv_ref[h], preferred_element_type=F32))
    attn = jnp.concatenate(parts, axis=-1)
    out_ref[...] = _rms(attn, gao_ref[...]).astype(BF16)


def _attention(cqn, w_q, w_ukt, w_uv, cosq, sinq, gao, ckv, krope, *, n_batch, seq, row0,
               past_kv=None, past_kr=None):
    causal = past_kv is None
    tq = TQ if causal else seq
    nq = seq // tq
    n_past = 0 if causal else past_kv.shape[1]
    if not causal:
        assert n_past % CHUNK == 0 and seq <= CHUNK and n_past % TK == 0
    blk0 = row0 // tq
    qrow = lambda b, q: (blk0 + b * nq + q, 0)
    in_specs = [
        pl.BlockSpec((tq, Q_LORA), qrow),
        _const_spec(w_q.shape),
        _const_spec(w_ukt.shape),
        _const_spec(w_uv.shape),
        pl.BlockSpec((tq, N_HEADS * ROPE_DIM), lambda b, q: (q, 0)),
        pl.BlockSpec((tq, N_HEADS * ROPE_DIM), lambda b, q: (q, 0)),
        _const_spec((1, ATTN_W)),
    ]
    args = [cqn, w_q, w_ukt, w_uv, cosq, sinq, gao]
    if n_past:
        in_specs += [pl.BlockSpec((None, n_past, KV_LORA), lambda b, q: (b, 0, 0)),
                     pl.BlockSpec((None, ROPE_DIM, n_past), lambda b, q: (b, 0, 0))]
        args += [past_kv, past_kr]
    in_specs += [pl.BlockSpec((seq, KV_LORA), lambda b, q: (b, 0)),
                 pl.BlockSpec((None, ROPE_DIM, seq), lambda b, q: (b, 0, 0))]
    args += [ckv, krope]
    rows = N_HEADS * tq
    kern = functools.partial(_attn_kernel, tq=tq, n_past=n_past, causal=causal)
    return pl.pallas_call(
        kern,
        grid=(n_batch, nq),
        in_specs=in_specs,
        out_specs=pl.BlockSpec((tq, ATTN_W), lambda b, q: (b * nq + q, 0)),
        out_shape=jax.ShapeDtypeStruct((n_batch * seq, ATTN_W), BF16),
        scratch_shapes=[
            pltpu.VMEM((rows, KV_LORA), BF16),
            pltpu.VMEM((rows, ROPE_DIM), BF16),
            pltpu.VMEM((rows, LANES), F32),
            pltpu.VMEM((rows, LANES), F32),
            pltpu.VMEM((rows, KV_LORA), F32),
            pltpu.VMEM((2, rows, TK), F32),
            pltpu.VMEM((rows, LANES), jnp.int32),
        ],
        compiler_params=pltpu.CompilerParams(dimension_semantics=("arbitrary", "arbitrary"),
                                             vmem_limit_bytes=VMEM_LIMIT),
        name="attn_prompt" if causal else "attn_sample",
    )(*args)


def _out_proj_kernel(attnp_ref, attns_ref, convn_ref, xp_ref, xs_ref, wo_ref, gffn_ref, wr_ref,
                     br_ref, h_ref, xpk_ref, mi_ref, mf_ref, cnt_ref, carry_ref, logit_ref, *, n_prompt_tiles):
    i = pl.program_id(0)

    @pl.when(i == 0)
    def _():
        carry_ref[...] = jnp.zeros(carry_ref.shape, F32)
        logit_ref[...] = jnp.zeros(logit_ref.shape, F32)

    def tile(x_ref, attn_ref):
        prev_logits = logit_ref[...]
        y = jnp.dot(attn_ref[...], wo_ref[:ATTN_W, :], preferred_element_type=F32)
        y = y + jnp.dot(convn_ref[...], wo_ref[ATTN_W:, :], preferred_element_type=F32)
        h = x_ref[...] + y
        h_ref[...] = h
        xn = _rms(h, gffn_ref[...])

        half = D_MODEL // 2
        xh = xn.astype(BF16)
        xh32 = xh.astype(F32)
        lo = lax.bitcast_convert_type(xh32[:, :half], jnp.uint32)
        hi = lax.bitcast_convert_type(xh32[:, half:], jnp.uint32)
        xpk_ref[...] = (lo >> 16) | (hi & jnp.uint32(0xFFFF0000))

        xl = (xn - xh32).astype(BF16)
        hh_hl = jnp.dot(xh, wr_ref[...], preferred_element_type=F32)
        lh = jnp.dot(xl, wr_ref[:, :LANES], preferred_element_type=F32)
        logit_ref[...] = hh_hl[:, :LANES] + (lh + hh_hl[:, LANES:]) + br_ref[...]

        logits = prev_logits
        counted = (i > 0).astype(F32)
        lane = lax.broadcasted_iota(jnp.int32, (TM, LANES), 1).astype(F32)
        ninf = -jnp.inf
        far = float(LANES)

        def first_argmax(v):
            vmax = jnp.max(v, axis=-1, keepdims=True)
            return vmax, jnp.min(jnp.where(v == vmax, lane, far), axis=-1, keepdims=True)

        gl = jnp.where(lane < N_GROUPS, logits, ninf)
        gmax, gidx = first_argmax(gl)
        g_p = 1.0 / jnp.sum(jnp.exp(gl - gmax), axis=-1, keepdims=True)
        e_lo = N_GROUPS + EXPERTS_PER_GROUP * gidx
        el = jnp.where((lane >= e_lo) & (lane < e_lo + EXPERTS_PER_GROUP), logits, ninf)
        e1max, i1 = first_argmax(el)
        z = jnp.sum(jnp.exp(el - e1max), axis=-1, keepdims=True)
        el2 = jnp.where(lane == i1, ninf, el)
        e2max, i2 = first_argmax(el2)
        p1 = 1.0 / z
        p2 = jnp.exp(e2max - e1max) / z
        den = p1 + p2
        g0 = g_p * p1 / den
        g1 = g_p * p2 / den
        e0 = i1 - N_GROUPS
        e1 = i2 - N_GROUPS

        oh0 = lane == e0
        oh1 = lane == e1
        oh = jnp.where(oh0 | oh1, 1.0, 0.0)
        r = lax.broadcasted_iota(jnp.int32, (TM, TM), 0)
        c = lax.broadcasted_iota(jnp.int32, (TM, TM), 1)
        ltri = jnp.where(r > c, 1.0, 0.0).astype(BF16)
        before = jnp.dot(ltri, oh.astype(BF16), preferred_element_type=F32) + carry_ref[...]
        rank0 = jnp.sum(jnp.where(oh0, before, 0.0), axis=-1, keepdims=True)
        rank1 = jnp.sum(jnp.where(oh1, before, 0.0), axis=-1, keepdims=True)
        total = carry_ref[...] + counted * jnp.sum(oh, axis=0, keepdims=True)
        carry_ref[...] = total
        cnt_ref[...] = jnp.broadcast_to(total, cnt_ref.shape)

        mi = jnp.where(lane == 0, e0, jnp.where(lane == 1, e1, jnp.where(lane == 2, rank0, rank1)))
        mi_ref[...] = jnp.transpose(mi)[:SUBLANES, :].astype(jnp.int32)
        mf_ref[...] = jnp.where(lane == 0, g0, g1)

    @pl.when(i < n_prompt_tiles)
    def _():
        tile(xp_ref, attnp_ref)

    @pl.when(i >= n_prompt_tiles)
    def _():
        tile(xs_ref, attns_ref)


def _out_proj(attn_p, attn_s, conv_n, xp, xs, w_ob, gffn, w_r2, b_r):
    m = conv_n.shape[0]
    npt = xp.shape[0] // TM
    n_tiles = m // TM
    last_p, last_s, last = npt - 1, n_tiles - npt - 1, n_tiles - 1
    row = lambda i: (jnp.minimum(i, last), 0)
    prow = lambda i: (jnp.minimum(i, last_p), 0)
    srow = lambda i: (jnp.clip(i - npt, 0, last_s), 0)
    lag = lambda i: jnp.maximum(i - 1, 0)
    return pl.pallas_call(
        functools.partial(_out_proj_kernel, n_prompt_tiles=npt),
        grid=(n_tiles + 1,),
        in_specs=[
            pl.BlockSpec((TM, ATTN_W), prow),
            pl.BlockSpec((TM, ATTN_W), srow),
            pl.BlockSpec((TM, CONV_CH), row),
            pl.BlockSpec((TM, D_MODEL), prow),
            pl.BlockSpec((TM, D_MODEL), srow),
            _const_spec(w_ob.shape),
            _const_spec((1, D_MODEL)),
            _const_spec(w_r2.shape),
            _const_spec((1, LANES)),
        ],
        out_specs=[
            pl.BlockSpec((TM, D_MODEL), row),
            pl.BlockSpec((TM, D_MODEL // 2), row),
            pl.BlockSpec((SUBLANES, TM), lambda i: (0, lag(i))),
            pl.BlockSpec((TM, LANES), lambda i: (lag(i), 0)),
            pl.BlockSpec((SUBLANES, LANES), lambda i: (0, 0)),
        ],
        out_shape=[
            jax.ShapeDtypeStruct((m, D_MODEL), F32),
            jax.ShapeDtypeStruct((m, D_MODEL // 2), jnp.uint32),
            jax.ShapeDtypeStruct((SUBLANES, m), jnp.int32),
            jax.ShapeDtypeStruct((m, LANES), F32),
            jax.ShapeDtypeStruct((SUBLANES, LANES), F32),
        ],
        scratch_shapes=[pltpu.VMEM((1, LANES), F32), pltpu.VMEM((TM, LANES), F32)],
        compiler_params=pltpu.CompilerParams(dimension_semantics=("arbitrary",),
                                             vmem_limit_bytes=VMEM_LIMIT),
        name="out_proj",
    )(attn_p, attn_s, conv_n, xp, xs, w_ob, gffn, w_r2, b_r)


def _dispatch_kernel(d0_ref, d1_ref, zlo_ref, zn_ref, nu_ref, xpk_ref, xs_hbm, zeros_ref, sems, *, n_blocks):
    i = pl.program_id(0)
    sem = sems.at[0]
    zsem = sems.at[1]

    def zero_fill(act):
        def per_expert(e, c):
            lo = zlo_ref[e]
            n = zn_ref[e]
            head = (-lo) & (SUBLANES - 1)
            for r in range(SUBLANES - 1):
                @pl.when(r < head)
                def _(r=r):
                    act(pltpu.make_async_copy(zeros_ref.at[pl.ds(0, 1)], xs_hbm.at[pl.ds(lo + r, 1)], zsem))
            off = lo + head
            rest = n - head
            size = MOE_BLOCK // 2
            while size >= SUBLANES:
                @pl.when((rest & size) != 0)
                def _(off=off, size=size):
                    dst = xs_hbm.at[pl.ds(pl.multiple_of(off, SUBLANES), size)]
                    act(pltpu.make_async_copy(zeros_ref.at[pl.ds(0, size)], dst, zsem))
                off = off + (rest & size)
                size //= 2
            return c

        def per_block(b, c):
            dst = xs_hbm.at[pl.ds(pl.multiple_of(b * MOE_BLOCK, MOE_BLOCK), MOE_BLOCK)]
            act(pltpu.make_async_copy(zeros_ref, dst, zsem))
            return c

        lax.fori_loop(0, N_EXPERTS, per_expert, 0)
        lax.fori_loop(nu_ref[0], n_blocks, per_block, 0)

    @pl.when(i == 0)
    def _():
        zeros_ref[...] = jnp.zeros(zeros_ref.shape, zeros_ref.dtype)
        zero_fill(lambda cp: cp.start())
        zero_fill(lambda cp: cp.wait())

    base = i * TM

    def start(g, c):
        for u in range(SUBLANES):
            r = base + g * SUBLANES + u
            src = xpk_ref.at[g, pl.ds(u, 1)]
            pltpu.make_async_copy(src, xs_hbm.at[pl.ds(d0_ref[r], 1)], sem).start()
            pltpu.make_async_copy(src, xs_hbm.at[pl.ds(d1_ref[r], 1)], sem).start()
        return c

    lax.fori_loop(0, TM // SUBLANES, start, 0)
    for _ in range(2):
        pltpu.make_async_copy(xs_hbm.at[pl.ds(0, TM)], xs_hbm.at[pl.ds(0, TM)], sem).wait()


def _dispatch(dest0, dest1, pad_lo, n_pad, n_used, xpk, n_blocks):
    m = xpk.shape[0]
    grid_spec = pltpu.PrefetchScalarGridSpec(
        num_scalar_prefetch=5,
        grid=(m // TM,),
        in_specs=[pl.BlockSpec((TM // SUBLANES, SUBLANES, D_MODEL // 2), lambda i, *_: (i, 0, 0))],
        out_specs=pl.BlockSpec(memory_space=pl.ANY),
        scratch_shapes=[pltpu.VMEM((MOE_BLOCK, D_MODEL // 2), jnp.uint32),
                        pltpu.SemaphoreType.DMA((2,))],
    )
    return pl.pallas_call(
        functools.partial(_dispatch_kernel, n_blocks=n_blocks),
        grid_spec=grid_spec,
        out_shape=jax.ShapeDtypeStruct((n_blocks * MOE_BLOCK, D_MODEL // 2), jnp.uint32),
        compiler_params=pltpu.CompilerParams(dimension_semantics=("arbitrary",)),
        name="dispatch",
    )(dest0, dest1, pad_lo, n_pad, n_used, xpk.reshape(m // SUBLANES, SUBLANES, D_MODEL // 2))


def _experts_kernel(be_ref, nu_ref, nxt_ref, x_ref, wg_hbm, wu_hbm, wd_hbm, y_ref,
                    sg_ref, su_ref, sd_ref, wgb_ref, wub_ref, wdb_ref, sems):
    b = pl.program_id(0)
    active = b < nu_ref[0]
    new_expert = jnp.logical_or(b == 0, be_ref[b] != be_ref[jnp.maximum(b - 1, 0)])

    def weight_copies(e):
        return (pltpu.make_async_copy(wg_hbm.at[e], sg_ref, sems.at[0]),
                pltpu.make_async_copy(wu_hbm.at[e], su_ref, sems.at[1]),
                pltpu.make_async_copy(wd_hbm.at[e], sd_ref, sems.at[2]))

    @pl.when(b == 0)
    def _():
        for cp in weight_copies(be_ref[0]):
            cp.start()

    @pl.when(jnp.logical_and(active, new_expert))
    def _():
        for cp in weight_copies(be_ref[b]):
            cp.wait()
        wgb_ref[...] = sg_ref[...].astype(BF16)
        wub_ref[...] = su_ref[...].astype(BF16)
        wdb_ref[...] = sd_ref[...].astype(BF16)

        @pl.when(nxt_ref[b] >= 0)
        def _():
            for cp in weight_copies(nxt_ref[b]):
                cp.start()

    @pl.when(active)
    def _():
        half = D_MODEL // 2
        xw = x_ref[...]
        xa = lax.bitcast_convert_type(xw << 16, F32).astype(BF16)
        xb = lax.bitcast_convert_type(xw & jnp.uint32(0xFFFF0000), F32).astype(BF16)
        g = jnp.dot(xa, wgb_ref[:half, :], preferred_element_type=F32)
        g = g + jnp.dot(xb, wgb_ref[half:, :], preferred_element_type=F32)
        u = jnp.dot(xa, wub_ref[:half, :], preferred_element_type=F32)
        u = u + jnp.dot(xb, wub_ref[half:, :], preferred_element_type=F32)
        hmid = (g * jax.nn.sigmoid(g)) * u
        y_ref[...] = jnp.dot(hmid.astype(BF16), wdb_ref[...], preferred_element_type=F32)

    @pl.when(b >= nu_ref[0])
    def _():
        y_ref[...] = jnp.zeros(y_ref.shape, y_ref.dtype)


def _experts(block_e, n_used, next_e, x_sorted, w_gate, w_up, w_down):
    p = x_sorted.shape[0]
    nb = p // MOE_BLOCK

    def xrow(b, be, nu, nxt):
        return (jnp.maximum(jnp.minimum(b, nu[0] - 1), 0), 0)

    grid_spec = pltpu.PrefetchScalarGridSpec(
        num_scalar_prefetch=3,
        grid=(nb,),
        in_specs=[
            pl.BlockSpec((MOE_BLOCK, D_MODEL // 2), xrow),
            pl.BlockSpec(memory_space=pl.ANY),
            pl.BlockSpec(memory_space=pl.ANY),
            pl.BlockSpec(memory_space=pl.ANY),
        ],
        out_specs=pl.BlockSpec((MOE_BLOCK, D_MODEL), lambda b, be, nu, nxt: (b, 0)),
        scratch_shapes=[pltpu.VMEM((D_MODEL, D_FF), F32), pltpu.VMEM((D_MODEL, D_FF), F32),
                        pltpu.VMEM((D_FF, D_MODEL), F32),
                        pltpu.VMEM((D_MODEL, D_FF), BF16), pltpu.VMEM((D_MODEL, D_FF), BF16),
                        pltpu.VMEM((D_FF, D_MODEL), BF16),
                        pltpu.SemaphoreType.DMA((3,))],
    )
    return pl.pallas_call(
        _experts_kernel,
        grid_spec=grid_spec,
        out_shape=jax.ShapeDtypeStruct((p, D_MODEL), F32),
        compiler_params=pltpu.CompilerParams(dimension_semantics=("arbitrary",),
                                             vmem_limit_bytes=VMEM_LIMIT),
        name="experts",
    )(block_e, n_used, next_e, x_sorted, w_gate, w_up, w_down)


def _combine_kernel(d0_ref, d1_ref, h_ref, mf_ref, gfin_ref, y_hbm, outp_ref, outs_ref, y0_ref, y1_ref, sems,
                    *, n_tiles, n_prompt_tiles):
    i = pl.program_id(0)

    def gather(tile, slot, act):
        base = tile * TM

        def body(g, c):
            for u in range(SUBLANES):
                r = base + g * SUBLANES + u
                act(pltpu.make_async_copy(y_hbm.at[pl.ds(d0_ref[r], 1)], y0_ref.at[slot, g, pl.ds(u, 1)],
                                          sems.at[slot]))
                act(pltpu.make_async_copy(y_hbm.at[pl.ds(d1_ref[r], 1)], y1_ref.at[slot, g, pl.ds(u, 1)],
                                          sems.at[slot]))
            return c
        lax.fori_loop(0, TM // SUBLANES, body, 0)

    @pl.when(i == 0)
    def _():
        gather(0, 0, lambda cp: cp.start())

    @pl.when(i + 1 < n_tiles)
    def _():
        gather(i + 1, (i + 1) % 2, lambda cp: cp.start())

    slot = i % 2
    for _ in range(2):
        pltpu.make_async_copy(y_hbm.at[pl.ds(0, TM)], y_hbm.at[pl.ds(0, TM)], sems.at[slot]).wait()

    def finish(out_ref):
        mf = mf_ref[...]
        ffn = mf[:, :, 0:1] * y0_ref[slot] + mf[:, :, 1:2] * y1_ref[slot]
        out_ref[...] = _rms(h_ref[...] + ffn, gfin_ref[...])

    @pl.when(i < n_prompt_tiles)
    def _():
        finish(outp_ref)

    @pl.when(i >= n_prompt_tiles)
    def _():
        finish(outs_ref)


def _combine(dest0, dest1, h, mf, gfin, y_sorted, *, n_prompt_rows):
    m = h.shape[0]
    npt = n_prompt_rows // TM
    tg = TM // SUBLANES
    grouped = lambda a: a.reshape(a.shape[0] // SUBLANES, SUBLANES, a.shape[1])
    grid_spec = pltpu.PrefetchScalarGridSpec(
        num_scalar_prefetch=2,
        grid=(m // TM,),
        in_specs=[
            pl.BlockSpec((tg, SUBLANES, D_MODEL), lambda i, *_: (i, 0, 0)),
            pl.BlockSpec((tg, SUBLANES, LANES), lambda i, *_: (i, 0, 0)),
            pl.BlockSpec((1, 1, D_MODEL), lambda i, *_: (0, 0, 0)),
            pl.BlockSpec(memory_space=pl.ANY),
        ],
        out_specs=[pl.BlockSpec((tg, SUBLANES, D_MODEL), lambda i, *_: (jnp.minimum(i, npt - 1), 0, 0)),
                   pl.BlockSpec((tg, SUBLANES, D_MODEL), lambda i, *_: (jnp.maximum(i - npt, 0), 0, 0))],
        scratch_shapes=[pltpu.VMEM((2, TM // SUBLANES, SUBLANES, D_MODEL), F32),
                        pltpu.VMEM((2, TM // SUBLANES, SUBLANES, D_MODEL), F32),
                        pltpu.SemaphoreType.DMA((2,))],
    )
    y_p, y_s = pl.pallas_call(
        functools.partial(_combine_kernel, n_tiles=m // TM, n_prompt_tiles=npt),
        grid_spec=grid_spec,
        out_shape=[jax.ShapeDtypeStruct((n_prompt_rows // SUBLANES, SUBLANES, D_MODEL), F32),
                   jax.ShapeDtypeStruct(((m - n_prompt_rows) // SUBLANES, SUBLANES, D_MODEL), F32)],
        compiler_params=pltpu.CompilerParams(dimension_semantics=("arbitrary",),
                                             vmem_limit_bytes=VMEM_LIMIT),
        name="combine",
    )(dest0, dest1, grouped(h), grouped(mf), gfin.reshape(1, 1, D_MODEL), y_sorted)
    return y_p.reshape(n_prompt_rows, D_MODEL), y_s.reshape(m - n_prompt_rows, D_MODEL)


def _rope_tables(pos):
    f32 = np.float32
    inv = np.power(f32(ROPE_THETA), -np.arange(0, ROPE_DIM, 2, dtype=f32) / f32(ROPE_DIM)).astype(f32)
    ang = (pos.astype(f32)[:, None] * inv[None, :]).astype(f32)
    cos, sin = np.cos(ang).astype(f32), np.sin(ang).astype(f32)
    return np.concatenate([cos, cos], axis=-1), np.concatenate([-sin, sin], axis=-1)


def _swap_halves(w):
    return jnp.concatenate([w[..., ROPE_DIM // 2:], w[..., :ROPE_DIM // 2]], axis=-1)


def kernel(x_prompt, x_sample, cache_kv_latent, cache_k_rope, state_conv, norm_mix, w_in, norm_q, w_uq,
           norm_kv, w_uk, w_uv, conv_w, norm_attn_out, norm_conv_out, w_o, norm_ffn, w_router_group,
           b_router_group, w_router_expert, b_router_expert, w_gate, w_up, w_down, norm_final):
    assert w_in.shape[0] == 1, "single-layer trunk"
    bp, seq_p, _ = x_prompt.shape
    bs, seq_s, _ = x_sample.shape
    past_len = cache_kv_latent.shape[2]
    np_rows, ns_rows = bp * seq_p, bs * seq_s
    m = np_rows + ns_rows
    assert seq_p % TM == 0 and TM % seq_s == 0 and ns_rows % TM == 0 and seq_s == CHUNK

    xp = x_prompt.reshape(np_rows, D_MODEL)
    xs = x_sample.reshape(ns_rows, D_MODEL)
    row_vec = lambda v: v.reshape(1, -1)

    assert w_in.shape[2] == Q_LORA + KV_LORA + ROPE_DIM + 3 * CONV_CH
    w_t = jnp.swapaxes(w_in[0], 0, 1).astype(BF16)
    wq4 = w_uq[0].reshape(Q_LORA, N_HEADS, QK_NOPE + ROPE_DIM)
    wq_rope = wq4[:, :, QK_NOPE:]
    w_q = jnp.concatenate([wq4[:, :, :QK_NOPE].reshape(Q_LORA, -1), wq_rope.reshape(Q_LORA, -1),
                           _swap_halves(wq_rope).reshape(Q_LORA, -1)], axis=1).astype(BF16)
    w_ukt = jnp.transpose(w_uk[0], (1, 2, 0)).astype(BF16)
    w_uvh = jnp.transpose(w_uv[0], (1, 0, 2)).astype(BF16)
    w_ob = w_o[0].astype(BF16)
    n_router = N_GROUPS + N_EXPERTS
    w_r = jnp.concatenate([w_router_group[0], w_router_expert[0].reshape(D_MODEL, N_EXPERTS)], axis=1)
    w_r = jnp.pad(w_r, ((0, 0), (0, LANES - n_router)))
    w_rh = w_r.astype(BF16)
    w_rl = (w_r - w_rh.astype(F32)).astype(BF16)
    w_r2 = jnp.concatenate([w_rh, w_rl], axis=1)
    b_r =jnp.pad(jnp.concatenate([b_router_group[0], b_router_expert[0].reshape(N_EXPERTS)]),
                  (0, LANES - n_router)).reshape(1, LANES)

    cos_p, sin_p = _rope_tables(np.arange(seq_p))
    cos_s, sin_s = _rope_tables(past_len + np.arange(seq_s))
    cosk = np.concatenate([cos_p, np.tile(cos_s, (TM // seq_s, 1))], axis=0)
    sink = np.concatenate([sin_p, np.tile(sin_s, (TM // seq_s, 1))], axis=0)
    state = jnp.concatenate([jnp.zeros((bp, CONV_W - 1, CONV_CH), F32), state_conv[0]], axis=0)

    cqn, ckv_p, kr_p, ckv_s, kr_s, conv_n, utail = _in_proj(
        xp, xs, row_vec(norm_mix[0]), w_t, row_vec(norm_q[0]), row_vec(norm_kv[0]),
        row_vec(norm_conv_out[0]), conv_w[0], cosk, sink, state, seq_p=seq_p, seq_s=seq_s)

    gao = row_vec(norm_attn_out[0])
    attn_p = _attention(cqn, w_q, w_ukt, w_uvh, np.tile(cos_p, (1, N_HEADS)), np.tile(sin_p, (1, N_HEADS)),
                        gao, ckv_p, kr_p, n_batch=bp, seq=seq_p, row0=0)
    attn_s = _attention(cqn, w_q, w_ukt, w_uvh, np.tile(cos_s, (1, N_HEADS)), np.tile(sin_s, (1, N_HEADS)),
                        gao, ckv_s, kr_s, n_batch=bs, seq=seq_s, row0=np_rows,
                        past_kv=cache_kv_latent[0], past_kr=jnp.swapaxes(cache_k_rope[0], 1, 2))

    h, xpk, mi, mf, cnt = _out_proj(attn_p, attn_s, conv_n, xp, xs, w_ob, row_vec(norm_ffn[0]),
                                    w_r2, b_r)

    counts = cnt[0, :N_EXPERTS].astype(jnp.int32)
    padded = (counts + MOE_BLOCK - 1) // MOE_BLOCK * MOE_BLOCK
    pad_end = jnp.cumsum(padded)
    pad_start = pad_end - padded
    n_blocks = -(-(m * 2) // MOE_BLOCK) + N_EXPERTS
    block_row0 = jnp.arange(n_blocks, dtype=jnp.int32) * MOE_BLOCK
    block_e = jnp.minimum(jnp.sum((pad_end[None, :] <= block_row0[:, None]).astype(jnp.int32), axis=1),
                          N_EXPERTS - 1)
    n_used = (pad_end[-1:] // MOE_BLOCK).astype(jnp.int32)
    expert_ids = jnp.arange(N_EXPERTS, dtype=jnp.int32)[:, None]

    def seg_start(e):
        return jnp.sum(jnp.where(expert_ids == e[None, :], pad_start[:, None], 0), axis=0)

    dest0 = seg_start(mi[0]) + mi[2]
    dest1 = seg_start(mi[1]) + mi[3]

    x_sorted = _dispatch(dest0, dest1, pad_start + counts, padded - counts, n_used, xpk, n_blocks)
    later = (expert_ids.T > block_e[:, None]) & (padded > 0)[None, :]
    next_e = jnp.min(jnp.where(later, expert_ids.T, N_EXPERTS), axis=1)
    next_e = jnp.where(next_e == N_EXPERTS, -1, next_e).astype(jnp.int32)
    y_sorted = _experts(block_e, n_used, next_e, x_sorted, w_gate[0], w_up[0], w_down[0])
    gfin = row_vec(norm_final)
    y_p, y_s = _combine(dest0, dest1, h, mf, gfin, y_sorted, n_prompt_rows=np_rows)

    ut = utail.reshape(m // CHUNK, SUBLANES, CONV_CH)
    tails = ut[:, SUBLANES - (CONV_W - 1):, :]
    p_last = (jnp.arange(bp) + 1) * (seq_p // CHUNK) - 1
    s_last = np_rows // CHUNK + (jnp.arange(bs) + 1) * (seq_s // CHUNK) - 1
    return (y_p.reshape(bp, seq_p, D_MODEL),
            y_s.reshape(bs, seq_s, D_MODEL),
            ckv_p.reshape(1, bp, seq_p, KV_LORA),
            jnp.swapaxes(kr_p, 1, 2)[None],
            tails[p_last][None],
            ckv_s.reshape(1, bs, seq_s, KV_LORA),
            jnp.swapaxes(kr_s, 1, 2)[None],
            tails[s_last][None])
```

```python
import functools

import jax
import jax.numpy as jnp
import numpy as np
from jax import lax
from jax.experimental import pallas as pl
from jax.experimental.pallas import tpu as pltpu

F32 = jnp.float32
BF16 = jnp.bfloat16

D_MODEL = 2048
N_HEADS = 8
QK_NOPE = 128
ROPE_DIM = 64
V_DIM = 128
Q_LORA = 512
KV_LORA = 512
ATTN_W = N_HEADS * V_DIM
CONV_CH = D_MODEL - ATTN_W
CONV_W = 3
CHUNK = 64
N_GROUPS = 4
EXPERTS_PER_GROUP = 8
N_EXPERTS = N_GROUPS * EXPERTS_PER_GROUP
D_FF = 512
ROPE_THETA = 10000.0
EPS = 1e-6
ATTN_SCALE = (QK_NOPE + ROPE_DIM) ** -0.5
EXP2_SCALE = ATTN_SCALE * 1.4426950408889634

LANES = 128
SUBLANES = 8
TM = 256
MOE_BLOCK = 256
TQ = 256
TK = 256
NEG_BIG = -1e30
VMEM_LIMIT = 56 * 1024 * 1024


def _rms(v, g):
    return v * lax.rsqrt(jnp.mean(v * v, axis=-1, keepdims=True) + EPS) * g


def _lane_bcast(v, width):
    if width % LANES == 0:
        return jnp.concatenate([v] * (width // LANES), axis=1)
    assert width < LANES
    return v[:, :width]


def _pack_bf16_pairs(v):
    half = v.shape[-1] // 2
    lo = lax.bitcast_convert_type(v[..., :half].astype(BF16).astype(F32), jnp.uint32)
    hi = lax.bitcast_convert_type(v[..., half:].astype(BF16).astype(F32), jnp.uint32)
    return (lo >> 16) | (hi & jnp.uint32(0xFFFF0000))


def _unpack_bf16_pairs(w):
    return (lax.bitcast_convert_type(w << 16, F32),
            lax.bitcast_convert_type(w & jnp.uint32(0xFFFF0000), F32))


def _const_spec(shape):
    nd = len(shape)
    return pl.BlockSpec(shape, lambda *_: (0,) * nd, pipeline_mode=pl.Buffered(1))


def _in_proj_kernel(xp_ref, xs_ref, gmix_ref, wt_ref, gq_ref, gkv_ref, gco_ref, convw_ref,
                    cos_ref, sin_ref, state_ref,
                    cqn_ref, ckvp_ref, krp_ref, ckvs_ref, krs_ref, convn_ref, utail_ref, ext_ref,
                    *, n_prompt_tiles, tiles_per_seq, n_prompt_seq, sample_seq_len):
    i = pl.program_id(0)

    @pl.when(i == 0)
    def _():
        ext_ref[...] = jnp.zeros(ext_ref.shape, F32)

    def conv_block(u_sub, gate_sub, row0, length):
        ext_ref[SUBLANES:SUBLANES + length, :] = u_sub
        um1 = ext_ref[SUBLANES - 1:SUBLANES - 1 + length, :]
        um2 = ext_ref[SUBLANES - 2:SUBLANES - 2 + length, :]
        cw = convw_ref[...]
        conv = cw[0:1] * um2 + cw[1:2] * um1 + cw[2:3] * u_sub
        convn_ref[row0:row0 + length, :] = _rms(gate_sub * conv, gco_ref[...]).astype(BF16)

    def tile(x_ref, is_prompt):
        ckv_ref, krt_ref = (ckvp_ref, krp_ref) if is_prompt else (ckvs_ref, krs_ref)
        x = x_ref[...]
        xg = (x * gmix_ref[...]).astype(BF16)
        inv_rms = lax.rsqrt(jnp.mean(x * x, axis=-1, keepdims=True) + EPS)
        lat_w = Q_LORA + KV_LORA
        conv0 = lat_w + ROPE_DIM
        nt = (((1,), (1,)), ((), ()))

        def project(lo, hi):
            return inv_rms * lax.dot_general(xg, wt_ref[lo:hi, :], nt, preferred_element_type=F32)

        z_ch = project(conv0 + CONV_CH, conv0 + 3 * CONV_CH)
        u = z_ch[:, :CONV_CH] * z_ch[:, CONV_CH:]
        for j in range(TM // CHUNK):
            utail_ref[j] = u[CHUNK * (j + 1) - SUBLANES:CHUNK * (j + 1), :]
        gate_b = project(conv0, conv0 + CONV_CH)

        if is_prompt:
            first = (i % tiles_per_seq) == 0
            carried = ext_ref[TM + SUBLANES - 2:TM + SUBLANES, :]
            ext_ref[SUBLANES - 2:SUBLANES, :] = jnp.where(first, state_ref[i // tiles_per_seq], carried)
            conv_block(u, gate_b, 0, TM)
        else:
            n_sub = TM // sample_seq_len
            seq0 = n_prompt_seq + (i - n_prompt_tiles) * n_sub
            for k in range(n_sub):
                ext_ref[SUBLANES - 2:SUBLANES, :] = state_ref[seq0 + k]
                lo = k * sample_seq_len
                conv_block(u[lo:lo + sample_seq_len], gate_b[lo:lo + sample_seq_len], lo, sample_seq_len)

        zk = project(lat_w, conv0)
        zk_swapped = jnp.concatenate([zk[:, ROPE_DIM // 2:], zk[:, :ROPE_DIM // 2]], axis=1)
        k_rope = zk * cos_ref[...] + zk_swapped * sin_ref[...]
        if is_prompt:
            krt_ref[...] = k_rope.T
        else:
            for k in range(TM // sample_seq_len):
                krt_ref[k] = k_rope[k * sample_seq_len:(k + 1) * sample_seq_len, :].T
        ckv_ref[...] = _rms(project(Q_LORA, lat_w), gkv_ref[...])
        cqn_ref[...] = _rms(project(0, Q_LORA), gq_ref[...]).astype(BF16)

    @pl.when(i < n_prompt_tiles)
    def _():
        tile(xp_ref, True)

    @pl.when(i >= n_prompt_tiles)
    def _():
        tile(xs_ref, False)


def _in_proj(xp, xs, gmix, w_t, gq, gkv, gco, convw, cosk, sink, state, *, seq_p, seq_s):
    np_rows, ns_rows = xp.shape[0], xs.shape[0]
    m = np_rows + ns_rows
    npt, nst = np_rows // TM, ns_rows // TM
    tps = seq_p // TM
    n_prompt_seq = np_rows // seq_p
    last_p = npt - 1

    def tab_idx(i):
        return (jnp.where(i < npt, i % tps, tps), 0)

    row = lambda i: (i, 0)
    prow = lambda i: (jnp.minimum(i, last_p), 0)
    srow = lambda i: (jnp.maximum(i - npt, 0), 0)
    kern = functools.partial(_in_proj_kernel, n_prompt_tiles=npt, tiles_per_seq=tps,
                             n_prompt_seq=n_prompt_seq, sample_seq_len=seq_s)
    return pl.pallas_call(
        kern,
        grid=(npt + nst,),
        in_specs=[
            pl.BlockSpec((TM, D_MODEL), prow),
            pl.BlockSpec((TM, D_MODEL), srow),
            _const_spec((1, D_MODEL)),
            _const_spec(w_t.shape),
            _const_spec((1, Q_LORA)),
            _const_spec((1, KV_LORA)),
            _const_spec((1, CONV_CH)),
            _const_spec((CONV_W, CONV_CH)),
            pl.BlockSpec((TM, ROPE_DIM), tab_idx),
            pl.BlockSpec((TM, ROPE_DIM), tab_idx),
            _const_spec(state.shape),
        ],
        out_specs=[
            pl.BlockSpec((TM, Q_LORA), row),
            pl.BlockSpec((TM, KV_LORA), prow),
            pl.BlockSpec((None, ROPE_DIM, TM), lambda i: (jnp.minimum(i, last_p) // tps, 0,
                                                          jnp.minimum(i, last_p) % tps)),
            pl.BlockSpec((TM, KV_LORA), srow),
            pl.BlockSpec((TM // seq_s, ROPE_DIM, seq_s), lambda i: (jnp.maximum(i - npt, 0), 0, 0)),
            pl.BlockSpec((TM, CONV_CH), row),
            pl.BlockSpec((TM // CHUNK, SUBLANES, CONV_CH), lambda i: (i, 0, 0)),
        ],
        out_shape=[
            jax.ShapeDtypeStruct((m, Q_LORA), BF16),
            jax.ShapeDtypeStruct((np_rows, KV_LORA), F32),
            jax.ShapeDtypeStruct((n_prompt_seq, ROPE_DIM, seq_p), F32),
            jax.ShapeDtypeStruct((ns_rows, KV_LORA), F32),
            jax.ShapeDtypeStruct((ns_rows // seq_s, ROPE_DIM, seq_s), F32),
            jax.ShapeDtypeStruct((m, CONV_CH), BF16),
            jax.ShapeDtypeStruct((m // CHUNK, SUBLANES, CONV_CH), F32),
        ],
        scratch_shapes=[pltpu.VMEM((TM + SUBLANES, CONV_CH), F32)],
        compiler_params=pltpu.CompilerParams(dimension_semantics=("arbitrary",),
                                             vmem_limit_bytes=VMEM_LIMIT),
        name="in_proj",
    )(xp, xs, gmix, w_t, gq, gkv, gco, convw, cosk, sink, state)


def _attn_kernel(*refs, tq, n_past, causal):
    refs = list(refs)
    cqn_ref, wq_ref, wuk_ref, wuv_ref, cos_ref, sin_ref, gao_ref = refs[:7]
    refs = refs[7:]
    if n_past:
        pkv_ref, pkr_ref = refs[:2]
        refs = refs[2:]
    kv_ref, kr_ref, out_ref, qlat_ref, qr_ref, m_ref, l_ref, acc_ref, s_ref, klim_ref = refs

    qi = pl.program_id(1)
    rows = N_HEADS * tq

    q = jnp.dot(cqn_ref[...], wq_ref[...], preferred_element_type=F32)
    nope_w = N_HEADS * QK_NOPE
    rope_w = N_HEADS * ROPE_DIM
    qrope = q[:, nope_w:nope_w + rope_w] * cos_ref[...] + q[:, nope_w + rope_w:] * sin_ref[...]
    for h in range(N_HEADS):
        qn = q[:, h * QK_NOPE:(h + 1) * QK_NOPE].astype(BF16)
        ql = jnp.dot(qn, wuk_ref[h], preferred_element_type=F32)
        qlat_ref[h * tq:(h + 1) * tq, :] = ql.astype(BF16)
        qr_ref[h * tq:(h + 1) * tq, :] = qrope[:, h * ROPE_DIM:(h + 1) * ROPE_DIM].astype(BF16)


    nt = (((1,), (1,)), ((), ()))

    def scores(kc_f32, krt_f32):
        s = lax.dot_general(qlat_ref[...], kc_f32.astype(BF16), nt, preferred_element_type=F32)
        return s + jnp.dot(qr_ref[...], krt_f32.astype(BF16), preferred_element_type=F32)

    def update(s, kc_f32, mask, first=False):
        if mask is not None:
            s = jnp.where(mask, s, NEG_BIG)
        m_cur = jnp.max(s, axis=-1, keepdims=True)
        if first:
            m_new = jnp.broadcast_to(m_cur, m_ref.shape)
        else:
            m_prev = m_ref[...]
            m_new = jnp.maximum(m_prev, m_cur)
            alpha = jnp.exp2((m_prev - m_new) * EXP2_SCALE)
        p = jnp.exp2((s - _lane_bcast(m_new, s.shape[1])) * EXP2_SCALE)
        l_cur = jnp.sum(p, axis=-1, keepdims=True)
        pv = jnp.dot(p.astype(BF16), kc_f32.astype(BF16), preferred_element_type=F32)
        if first:
            l_ref[...] = jnp.broadcast_to(l_cur, l_ref.shape)
            acc_ref[...] = pv
        else:
            l_ref[...] = alpha * l_ref[...] + l_cur
            acc_ref[...] = _lane_bcast(alpha, KV_LORA) * acc_ref[...] + pv
        m_ref[...] = m_new

    def pipelined(kv, kr, lo, hi, last, mask_fn):
        def body(j, c):
            k0 = pl.multiple_of(j * TK, TK)
            k1 = pl.multiple_of(jnp.minimum(j + 1, last) * TK, TK)
            s_cur = s_ref[j % 2]
            s_ref[(j + 1) % 2] = scores(kv[pl.ds(k1, TK), :], kr[:, pl.ds(k1, TK)])
            update(s_cur, kv[pl.ds(k0, TK), :], None if mask_fn is None else mask_fn(k0))
            return c
        lax.fori_loop(lo, hi, body, 0)

    def pipelined_pairs(kv, kr, n_pairs, last):
        def body(i, c):
            ka = pl.multiple_of((2 * i + 1) * TK, TK)
            kb = pl.multiple_of((2 * i + 2) * TK, TK)
            kc = pl.multiple_of(jnp.minimum(2 * i + 3, last) * TK, TK)
            s_ref[0] = scores(kv[pl.ds(kb, TK), :], kr[:, pl.ds(kb, TK)])
            update(s_ref[1], kv[pl.ds(ka, TK), :], None)
            s_ref[1] = scores(kv[pl.ds(kc, TK), :], kr[:, pl.ds(kc, TK)])
            update(s_ref[0], kv[pl.ds(kb, TK), :], None)
            return c
        lax.fori_loop(0, n_pairs, body, 0)

    def first_block(kv, kr, last, mask):
        k1 = pl.multiple_of(jnp.minimum(1, last) * TK, TK)
        s_ref[0] = scores(kv[pl.ds(0, TK), :], kr[:, pl.ds(0, TK)])
        s_ref[1] = scores(kv[pl.ds(k1, TK), :], kr[:, pl.ds(k1, TK)])
        update(s_ref[0], kv[pl.ds(0, TK), :], mask, first=True)

    if n_past:
        n_pb = n_past // TK
        first_block(pkv_ref, pkr_ref, n_pb - 1, None)
        n_pairs = (n_pb - 1) // 2
        pipelined_pairs(pkv_ref, pkr_ref, n_pairs, n_pb - 1)
        if 1 + 2 * n_pairs < n_pb:
            pipelined(pkv_ref, pkr_ref, 1 + 2 * n_pairs, n_pb, n_pb - 1, None)

    if causal:
        n_blocks = ((qi + 1) * tq + TK - 1) // TK
        n_full = jnp.minimum((qi * tq // CHUNK + 1) * CHUNK // TK, n_blocks)

        assert tq & (tq - 1) == 0 and CHUNK & (CHUNK - 1) == 0
        r = lax.broadcasted_iota(jnp.int32, (rows, LANES), 0)
        q_pos = qi * tq + (r & (tq - 1))
        klim_ref[...] = (q_pos & ~(CHUNK - 1)) + CHUNK

        def mask_fn(k0):
            cidx = lax.broadcasted_iota(jnp.int32, (rows, TK), 1)
            return cidx < _lane_bcast(klim_ref[...] - k0, TK)

        first_block(kv_ref, kr_ref, n_blocks - 1, mask_fn(0))
        n_pairs = jnp.maximum(n_full - 1, 0) // 2
        pipelined_pairs(kv_ref, kr_ref, n_pairs, n_blocks - 1)
        pipelined(kv_ref, kr_ref, 1 + 2 * n_pairs, n_full, n_blocks - 1, None)
        pipelined(kv_ref, kr_ref, jnp.maximum(n_full, 1), n_blocks, n_blocks - 1, mask_fn)
    else:
        update(scores(kv_ref[...], kr_ref[...]), kv_ref[...], None)

    o = acc_ref[...] / _lane_bcast(l_ref[...], KV_LORA)
    parts = []
    for h in range(N_HEADS):
        oh = o[h * tq:(h + 1) * tq, :].astype(BF16)
        parts.append(jnp.dot(oh, wuv_ref[h], preferred_element_type=F32))
    attn = jnp.concatenate(parts, axis=-1)
    out_ref[...] = _rms(attn, gao_ref[...]).astype(BF16)


def _attention(cqn, w_q, w_ukt, w_uv, cosq, sinq, gao, ckv, krope, *, n_batch, seq, row0,
               past_kv=None, past_kr=None):
    causal = past_kv is None
    tq = TQ if causal else seq
    nq = seq // tq
    n_past = 0 if causal else past_kv.shape[1]
    if not causal:
        assert n_past % CHUNK == 0 and seq <= CHUNK and n_past % TK == 0
    blk0 = row0 // tq
    qrow = lambda b, q: (blk0 + b * nq + q, 0)
    in_specs = [
        pl.BlockSpec((tq, Q_LORA), qrow),
        _const_spec(w_q.shape),
        _const_spec(w_ukt.shape),
        _const_spec(w_uv.shape),
        pl.BlockSpec((tq, N_HEADS * ROPE_DIM), lambda b, q: (q, 0)),
        pl.BlockSpec((tq, N_HEADS * ROPE_DIM), lambda b, q: (q, 0)),
        _const_spec((1, ATTN_W)),
    ]
    args = [cqn, w_q, w_ukt, w_uv, cosq, sinq, gao]
    if n_past:
        in_specs += [pl.BlockSpec((None, n_past, KV_LORA), lambda b, q: (b, 0, 0)),
                     pl.BlockSpec((None, ROPE_DIM, n_past), lambda b, q: (b, 0, 0))]
        args += [past_kv, past_kr]
    in_specs += [pl.BlockSpec((seq, KV_LORA), lambda b, q: (b, 0)),
                 pl.BlockSpec((None, ROPE_DIM, seq), lambda b, q: (b, 0, 0))]
    args += [ckv, krope]
    rows = N_HEADS * tq
    kern = functools.partial(_attn_kernel, tq=tq, n_past=n_past, causal=causal)
    return pl.pallas_call(
        kern,
        grid=(n_batch, nq),
        in_specs=in_specs,
        out_specs=pl.BlockSpec((tq, ATTN_W), lambda b, q: (b * nq + q, 0)),
        out_shape=jax.ShapeDtypeStruct((n_batch * seq, ATTN_W), BF16),
        scratch_shapes=[
            pltpu.VMEM((rows, KV_LORA), BF16),
            pltpu.VMEM((rows, ROPE_DIM), BF16),
            pltpu.VMEM((rows, LANES), F32),
            pltpu.VMEM((rows, LANES), F32),
            pltpu.VMEM((rows, KV_LORA), F32),
            pltpu.VMEM((2, rows, TK), F32),
            pltpu.VMEM((rows, LANES), jnp.int32),
        ],
        compiler_params=pltpu.CompilerParams(dimension_semantics=("arbitrary", "arbitrary"),
                                             vmem_limit_bytes=VMEM_LIMIT),
        name="attn_prompt" if causal else "attn_sample",
    )(*args)


def _out_proj_kernel(attnp_ref, attns_ref, convn_ref, xp_ref, xs_ref, wo_ref, gffn_ref, wr_ref,
                     br_ref, h_ref, xpk_ref, mi_ref, mf_ref, cnt_ref, carry_ref, logit_ref, *, n_prompt_tiles):
    i = pl.program_id(0)

    @pl.when(i == 0)
    def _():
        carry_ref[...] = jnp.zeros(carry_ref.shape, F32)
        logit_ref[...] = jnp.zeros(logit_ref.shape, F32)

    def tile(x_ref, attn_ref):
        prev_logits = logit_ref[...]
        y = jnp.dot(attn_ref[...], wo_ref[:ATTN_W, :], preferred_element_type=F32)
        y = y + jnp.dot(convn_ref[...], wo_ref[ATTN_W:, :], preferred_element_type=F32)
        h = x_ref[...] + y
        h_ref[...] = h
        xn = _rms(h, gffn_ref[...])

        half = D_MODEL // 2
        xh = xn.astype(BF16)
        xh32 = xh.astype(F32)
        lo = lax.bitcast_convert_type(xh32[:, :half], jnp.uint32)
        hi = lax.bitcast_convert_type(xh32[:, half:], jnp.uint32)
        xpk_ref[...] = (lo >> 16) | (hi & jnp.uint32(0xFFFF0000))

        xl = (xn - xh32).astype(BF16)
        hh_hl = jnp.dot(xh, wr_ref[...], preferred_element_type=F32)
        lh = jnp.dot(xl, wr_ref[:, :LANES], preferred_element_type=F32)
        logit_ref[...] = hh_hl[:, :LANES] + (lh + hh_hl[:, LANES:]) + br_ref[...]

        logits = prev_logits
        counted = (i > 0).astype(F32)
        lane = lax.broadcasted_iota(jnp.int32, (TM, LANES), 1).astype(F32)
        ninf = -jnp.inf
        far = float(LANES)

        def first_argmax(v):
            vmax = jnp.max(v, axis=-1, keepdims=True)
            return vmax, jnp.min(jnp.where(v == vmax, lane, far), axis=-1, keepdims=True)

        gl = jnp.where(lane < N_GROUPS, logits, ninf)
        gmax, gidx = first_argmax(gl)
        g_p = 1.0 / jnp.sum(jnp.exp(gl - gmax), axis=-1, keepdims=True)
        e_lo = N_GROUPS + EXPERTS_PER_GROUP * gidx
        el = jnp.where((lane >= e_lo) & (lane < e_lo + EXPERTS_PER_GROUP), logits, ninf)
        e1max, i1 = first_argmax(el)
        z = jnp.sum(jnp.exp(el - e1max), axis=-1, keepdims=True)
        el2 = jnp.where(lane == i1, ninf, el)
        e2max, i2 = first_argmax(el2)
        p1 = 1.0 / z
        p2 = jnp.exp(e2max - e1max) / z
        den = p1 + p2
        g0 = g_p * p1 / den
        g1 = g_p * p2 / den
        e0 = i1 - N_GROUPS
        e1 = i2 - N_GROUPS

        oh0 = lane == e0
        oh1 = lane == e1
        oh = jnp.where(oh0 | oh1, 1.0, 0.0)
        r = lax.broadcasted_iota(jnp.int32, (TM, TM), 0)
        c = lax.broadcasted_iota(jnp.int32, (TM, TM), 1)
        ltri = jnp.where(r > c, 1.0, 0.0).astype(BF16)
        before = jnp.dot(ltri, oh.astype(BF16), preferred_element_type=F32) + carry_ref[...]
        rank0 = jnp.sum(jnp.where(oh0, before, 0.0), axis=-1, keepdims=True)
        rank1 = jnp.sum(jnp.where(oh1, before, 0.0), axis=-1, keepdims=True)
        total = carry_ref[...] + counted * jnp.sum(oh, axis=0, keepdims=True)
        carry_ref[...] = total
        cnt_ref[...] = jnp.broadcast_to(total, cnt_ref.shape)

        mi = jnp.where(lane == 0, e0, jnp.where(lane == 1, e1, jnp.where(lane == 2, rank0, rank1)))
        mi_ref[...] = jnp.transpose(mi)[:SUBLANES, :].astype(jnp.int32)
        mf_ref[...] = jnp.where(lane == 0, g0, g1)

    @pl.when(i < n_prompt_tiles)
    def _():
        tile(xp_ref, attnp_ref)

    @pl.when(i >= n_prompt_tiles)
    def _():
        tile(xs_ref, attns_ref)


def _out_proj(attn_p, attn_s, conv_n, xp, xs, w_ob, gffn, w_r2, b_r):
    m = conv_n.shape[0]
    npt = xp.shape[0] // TM
    n_tiles = m // TM
    last_p, last_s, last = npt - 1, n_tiles - npt - 1, n_tiles - 1
    row = lambda i: (jnp.minimum(i, last), 0)
    prow = lambda i: (jnp.minimum(i, last_p), 0)
    srow = lambda i: (jnp.clip(i - npt, 0, last_s), 0)
    lag = lambda i: jnp.maximum(i - 1, 0)
    return pl.pallas_call(
        functools.partial(_out_proj_kernel, n_prompt_tiles=npt),
        grid=(n_tiles + 1,),
        in_specs=[
            pl.BlockSpec((TM, ATTN_W), prow),
            pl.BlockSpec((TM, ATTN_W), srow),
            pl.BlockSpec((TM, CONV_CH), row),
            pl.BlockSpec((TM, D_MODEL), prow),
            pl.BlockSpec((TM, D_MODEL), srow),
            _const_spec(w_ob.shape),
            _const_spec((1, D_MODEL)),
            _const_spec(w_r2.shape),
            _const_spec((1, LANES)),
        ],
        out_specs=[
            pl.BlockSpec((TM, D_MODEL), row),
            pl.BlockSpec((TM, D_MODEL // 2), row),
            pl.BlockSpec((SUBLANES, TM), lambda i: (0, lag(i))),
            pl.BlockSpec((TM, LANES), lambda i: (lag(i), 0)),
            pl.BlockSpec((SUBLANES, LANES), lambda i: (0, 0)),
        ],
        out_shape=[
            jax.ShapeDtypeStruct((m, D_MODEL), F32),
            jax.ShapeDtypeStruct((m, D_MODEL // 2), jnp.uint32),
            jax.ShapeDtypeStruct((SUBLANES, m), jnp.int32),
            jax.ShapeDtypeStruct((m, LANES), F32),
            jax.ShapeDtypeStruct((SUBLANES, LANES), F32),
        ],
        scratch_shapes=[pltpu.VMEM((1, LANES), F32), pltpu.VMEM((TM, LANES), F32)],
        compiler_params=pltpu.CompilerParams(dimension_semantics=("arbitrary",),
                                             vmem_limit_bytes=VMEM_LIMIT),
        name="out_proj",
    )(attn_p, attn_s, conv_n, xp, xs, w_ob, gffn, w_r2, b_r)


def _dispatch_kernel(d0_ref, d1_ref, zlo_ref, zn_ref, nu_ref, xpk_ref, xs_hbm, zeros_ref, sems, *, n_blocks):
    i = pl.program_id(0)
    sem = sems.at[0]
    zsem = sems.at[1]

    def zero_fill(act):
        def per_expert(e, c):
            lo = zlo_ref[e]
            n = zn_ref[e]
            head = (-lo) & (SUBLANES - 1)
            for r in range(SUBLANES - 1):
                @pl.when(r < head)
                def _(r=r):
                    act(pltpu.make_async_copy(zeros_ref.at[pl.ds(0, 1)], xs_hbm.at[pl.ds(lo + r, 1)], zsem))
            off = lo + head
            rest = n - head
            size = MOE_BLOCK // 2
            while size >= SUBLANES:
                @pl.when((rest & size) != 0)
                def _(off=off, size=size):
                    dst = xs_hbm.at[pl.ds(pl.multiple_of(off, SUBLANES), size)]
                    act(pltpu.make_async_copy(zeros_ref.at[pl.ds(0, size)], dst, zsem))
                off = off + (rest & size)
                size //= 2
            return c

        def per_block(b, c):
            dst = xs_hbm.at[pl.ds(pl.multiple_of(b * MOE_BLOCK, MOE_BLOCK), MOE_BLOCK)]
            act(pltpu.make_async_copy(zeros_ref, dst, zsem))
            return c

        lax.fori_loop(0, N_EXPERTS, per_expert, 0)
        lax.fori_loop(nu_ref[0], n_blocks, per_block, 0)

    @pl.when(i == 0)
    def _():
        zeros_ref[...] = jnp.zeros(zeros_ref.shape, zeros_ref.dtype)
        zero_fill(lambda cp: cp.start())

    @pl.when(i == pl.num_programs(0) - 1)
    def _():
        zero_fill(lambda cp: cp.wait())

    base = i * TM

    def start(g, c):
        for u in range(SUBLANES):
            r = base + g * SUBLANES + u
            src = xpk_ref.at[g, pl.ds(u, 1)]
            pltpu.make_async_copy(src, xs_hbm.at[pl.ds(d0_ref[r], 1)], sem).start()
            pltpu.make_async_copy(src, xs_hbm.at[pl.ds(d1_ref[r], 1)], sem).start()
        return c

    lax.fori_loop(0, TM // SUBLANES, start, 0)
    for _ in range(2):
        pltpu.make_async_copy(xs_hbm.at[pl.ds(0, TM)], xs_hbm.at[pl.ds(0, TM)], sem).wait()


def _dispatch(dest0, dest1, pad_lo, n_pad, n_used, xpk, n_blocks):
    m = xpk.shape[0]
    grid_spec = pltpu.PrefetchScalarGridSpec(
        num_scalar_prefetch=5,
        grid=(m // TM,),
        in_specs=[pl.BlockSpec((TM // SUBLANES, SUBLANES, D_MODEL // 2), lambda i, *_: (i, 0, 0))],
        out_specs=pl.BlockSpec(memory_space=pl.ANY),
        scratch_shapes=[pltpu.VMEM((MOE_BLOCK, D_MODEL // 2), jnp.uint32),
                        pltpu.SemaphoreType.DMA((2,))],
    )
    return pl.pallas_call(
        functools.partial(_dispatch_kernel, n_blocks=n_blocks),
        grid_spec=grid_spec,
        out_shape=jax.ShapeDtypeStruct((n_blocks * MOE_BLOCK, D_MODEL // 2), jnp.uint32),
        compiler_params=pltpu.CompilerParams(dimension_semantics=("arbitrary",)),
        name="dispatch",
    )(dest0, dest1, pad_lo, n_pad, n_used, xpk.reshape(m // SUBLANES, SUBLANES, D_MODEL // 2))


def _experts_kernel(be_ref, nu_ref, nxt_ref, x_ref, wg_hbm, wu_hbm, wd_hbm, y_ref,
                    sg_ref, su_ref, sd_ref, wgb_ref, wub_ref, wdb_ref, sems):
    b = pl.program_id(0)
    active = b < nu_ref[0]
    new_expert = jnp.logical_or(b == 0, be_ref[b] != be_ref[jnp.maximum(b - 1, 0)])

    def weight_copies(e):
        return (pltpu.make_async_copy(wg_hbm.at[e], sg_ref, sems.at[0]),
                pltpu.make_async_copy(wu_hbm.at[e], su_ref, sems.at[1]),
                pltpu.make_async_copy(wd_hbm.at[e], sd_ref, sems.at[2]))

    @pl.when(b == 0)
    def _():
        for cp in weight_copies(be_ref[0]):
            cp.start()

    @pl.when(jnp.logical_and(active, new_expert))
    def _():
        for cp in weight_copies(be_ref[b]):
            cp.wait()
        wgb_ref[...] = sg_ref[...].astype(BF16)
        wub_ref[...] = su_ref[...].astype(BF16)
        wdb_ref[...] = sd_ref[...].astype(BF16)

        @pl.when(nxt_ref[b] >= 0)
        def _():
            for cp in weight_copies(nxt_ref[b]):
                cp.start()

    @pl.when(active)
    def _():
        half = D_MODEL // 2
        xw = x_ref[...]
        xa = lax.bitcast_convert_type(xw << 16, F32).astype(BF16)
        xb = lax.bitcast_convert_type(xw & jnp.uint32(0xFFFF0000), F32).astype(BF16)
        g = jnp.dot(xa, wgb_ref[:half, :], preferred_element_type=F32)
        g = g + jnp.dot(xb, wgb_ref[half:, :], preferred_element_type=F32)
        u = jnp.dot(xa, wub_ref[:half, :], preferred_element_type=F32)
        u = u + jnp.dot(xb, wub_ref[half:, :], preferred_element_type=F32)
        hmid = (g * jax.nn.sigmoid(g)) * u
        y = jnp.dot(hmid.astype(BF16), wdb_ref[...], preferred_element_type=F32)
        y_ref[...] = _pack_bf16_pairs(y)

    @pl.when(b >= nu_ref[0])
    def _():
        y_ref[...] = jnp.zeros(y_ref.shape, y_ref.dtype)


def _experts(block_e, n_used, next_e, x_sorted, w_gate, w_up, w_down):
    p = x_sorted.shape[0]
    nb = p // MOE_BLOCK

    def xrow(b, be, nu, nxt):
        return (jnp.maximum(jnp.minimum(b, nu[0] - 1), 0), 0)

    grid_spec = pltpu.PrefetchScalarGridSpec(
        num_scalar_prefetch=3,
        grid=(nb,),
        in_specs=[
            pl.BlockSpec((MOE_BLOCK, D_MODEL // 2), xrow),
            pl.BlockSpec(memory_space=pl.ANY),
            pl.BlockSpec(memory_space=pl.ANY),
            pl.BlockSpec(memory_space=pl.ANY),
        ],
        out_specs=pl.BlockSpec((MOE_BLOCK, D_MODEL // 2), lambda b, be, nu, nxt: (b, 0)),
        scratch_shapes=[pltpu.VMEM((D_MODEL, D_FF), F32), pltpu.VMEM((D_MODEL, D_FF), F32),
                        pltpu.VMEM((D_FF, D_MODEL), F32),
                        pltpu.VMEM((D_MODEL, D_FF), BF16), pltpu.VMEM((D_MODEL, D_FF), BF16),
                        pltpu.VMEM((D_FF, D_MODEL), BF16),
                        pltpu.SemaphoreType.DMA((3,))],
    )
    return pl.pallas_call(
        _experts_kernel,
        grid_spec=grid_spec,
        out_shape=jax.ShapeDtypeStruct((p, D_MODEL // 2), jnp.uint32),
        compiler_params=pltpu.CompilerParams(dimension_semantics=("arbitrary",),
                                             vmem_limit_bytes=VMEM_LIMIT),
        name="experts",
    )(block_e, n_used, next_e, x_sorted, w_gate, w_up, w_down)


def _combine_kernel(d0_ref, d1_ref, h_ref, mf_ref, gfin_ref, y_hbm, outp_ref, outs_ref, y0_ref, y1_ref, sems,
                    *, n_tiles, n_prompt_tiles):
    i = pl.program_id(0)

    def gather(tile, slot, act):
        base = tile * TM

        def body(g, c):
            for u in range(SUBLANES):
                r = base + g * SUBLANES + u
                act(pltpu.make_async_copy(y_hbm.at[pl.ds(d0_ref[r], 1)], y0_ref.at[slot, g, pl.ds(u, 1)],
                                          sems.at[slot]))
                act(pltpu.make_async_copy(y_hbm.at[pl.ds(d1_ref[r], 1)], y1_ref.at[slot, g, pl.ds(u, 1)],
                                          sems.at[slot]))
            return c
        lax.fori_loop(0, TM // SUBLANES, body, 0)

    @pl.when(i == 0)
    def _():
        gather(0, 0, lambda cp: cp.start())

    @pl.when(i + 1 < n_tiles)
    def _():
        gather(i + 1, (i + 1) % 2, lambda cp: cp.start())

    slot = i % 2
    for _ in range(2):
        pltpu.make_async_copy(y_hbm.at[pl.ds(0, TM)], y_hbm.at[pl.ds(0, TM)], sems.at[slot]).wait()

    def finish(out_ref):
        mf = mf_ref[...]
        g0, g1 = mf[:, :, 0:1], mf[:, :, 1:2]
        half = D_MODEL // 2
        a0, b0 = _unpack_bf16_pairs(y0_ref[slot])
        a1, b1 = _unpack_bf16_pairs(y1_ref[slot])
        o_lo = h_ref[:, :, :half] + (g0 * a0 + g1 * a1)
        o_hi = h_ref[:, :, half:] + (g0 * b0 + g1 * b1)
        sumsq = jnp.sum(o_lo * o_lo, axis=-1, keepdims=True) + jnp.sum(o_hi * o_hi, axis=-1, keepdims=True)
        inv_rms = lax.rsqrt(sumsq / D_MODEL + EPS)
        out_ref[:, :, :half] = o_lo * inv_rms * gfin_ref[:, :, :half]
        out_ref[:, :, half:] = o_hi * inv_rms * gfin_ref[:, :, half:]

    @pl.when(i < n_prompt_tiles)
    def _():
        finish(outp_ref)

    @pl.when(i >= n_prompt_tiles)
    def _():
        finish(outs_ref)


def _combine(dest0, dest1, h, mf, gfin, y_sorted, *, n_prompt_rows):
    m = h.shape[0]
    npt = n_prompt_rows // TM
    tg = TM // SUBLANES
    grouped = lambda a: a.reshape(a.shape[0] // SUBLANES, SUBLANES, a.shape[1])
    grid_spec = pltpu.PrefetchScalarGridSpec(
        num_scalar_prefetch=2,
        grid=(m // TM,),
        in_specs=[
            pl.BlockSpec((tg, SUBLANES, D_MODEL), lambda i, *_: (i, 0, 0)),
            pl.BlockSpec((tg, SUBLANES, LANES), lambda i, *_: (i, 0, 0)),
            pl.BlockSpec((1, 1, D_MODEL), lambda i, *_: (0, 0, 0)),
            pl.BlockSpec(memory_space=pl.ANY),
        ],
        out_specs=[pl.BlockSpec((tg, SUBLANES, D_MODEL), lambda i, *_: (jnp.minimum(i, npt - 1), 0, 0)),
                   pl.BlockSpec((tg, SUBLANES, D_MODEL), lambda i, *_: (jnp.maximum(i - npt, 0), 0, 0))],
        scratch_shapes=[pltpu.VMEM((2, TM // SUBLANES, SUBLANES, D_MODEL // 2), jnp.uint32),
                        pltpu.VMEM((2, TM // SUBLANES, SUBLANES, D_MODEL // 2), jnp.uint32),
                        pltpu.SemaphoreType.DMA((2,))],
    )
    y_p, y_s = pl.pallas_call(
        functools.partial(_combine_kernel, n_tiles=m // TM, n_prompt_tiles=npt),
        grid_spec=grid_spec,
        out_shape=[jax.ShapeDtypeStruct((n_prompt_rows // SUBLANES, SUBLANES, D_MODEL), F32),
                   jax.ShapeDtypeStruct(((m - n_prompt_rows) // SUBLANES, SUBLANES, D_MODEL), F32)],
        compiler_params=pltpu.CompilerParams(dimension_semantics=("arbitrary",),
                                             vmem_limit_bytes=VMEM_LIMIT),
        name="combine",
    )(dest0, dest1, grouped(h), grouped(mf), gfin.reshape(1, 1, D_MODEL), y_sorted)
    return y_p.reshape(n_prompt_rows, D_MODEL), y_s.reshape(m - n_prompt_rows, D_MODEL)


def _rope_tables(pos):
    f32 = np.float32
    inv = np.power(f32(ROPE_THETA), -np.arange(0, ROPE_DIM, 2, dtype=f32) / f32(ROPE_DIM)).astype(f32)
    ang = (pos.astype(f32)[:, None] * inv[None, :]).astype(f32)
    cos, sin = np.cos(ang).astype(f32), np.sin(ang).astype(f32)
    return np.concatenate([cos, cos], axis=-1), np.concatenate([-sin, sin], axis=-1)


def _swap_halves(w):
    return jnp.concatenate([w[..., ROPE_DIM // 2:], w[..., :ROPE_DIM // 2]], axis=-1)


def kernel(x_prompt, x_sample, cache_kv_latent, cache_k_rope, state_conv, norm_mix, w_in, norm_q, w_uq,
           norm_kv, w_uk, w_uv, conv_w, norm_attn_out, norm_conv_out, w_o, norm_ffn, w_router_group,
           b_router_group, w_router_expert, b_router_expert, w_gate, w_up, w_down, norm_final):
    assert w_in.shape[0] == 1, "single-layer trunk"
    bp, seq_p, _ = x_prompt.shape
    bs, seq_s, _ = x_sample.shape
    past_len = cache_kv_latent.shape[2]
    np_rows, ns_rows = bp * seq_p, bs * seq_s
    m = np_rows + ns_rows
    assert seq_p % TM == 0 and TM % seq_s == 0 and ns_rows % TM == 0 and seq_s == CHUNK

    xp = x_prompt.reshape(np_rows, D_MODEL)
    xs = x_sample.reshape(ns_rows, D_MODEL)
    row_vec = lambda v: v.reshape(1, -1)

    assert w_in.shape[2] == Q_LORA + KV_LORA + ROPE_DIM + 3 * CONV_CH
    w_t = jnp.swapaxes(w_in[0], 0, 1).astype(BF16)
    wq4 = w_uq[0].reshape(Q_LORA, N_HEADS, QK_NOPE + ROPE_DIM)
    wq_rope = wq4[:, :, QK_NOPE:]
    w_q = jnp.concatenate([wq4[:, :, :QK_NOPE].reshape(Q_LORA, -1), wq_rope.reshape(Q_LORA, -1),
                           _swap_halves(wq_rope).reshape(Q_LORA, -1)], axis=1).astype(BF16)
    w_ukt = jnp.transpose(w_uk[0], (1, 2, 0)).astype(BF16)
    w_uvh = jnp.transpose(w_uv[0], (1, 0, 2)).astype(BF16)
    w_ob = w_o[0].astype(BF16)
    n_router = N_GROUPS + N_EXPERTS
    w_r = jnp.concatenate([w_router_group[0], w_router_expert[0].reshape(D_MODEL, N_EXPERTS)], axis=1)
    w_r = jnp.pad(w_r, ((0, 0), (0, LANES - n_router)))
    w_rh = w_r.astype(BF16)
    w_rl = (w_r - w_rh.astype(F32)).astype(BF16)
    w_r2 = jnp.concatenate([w_rh, w_rl], axis=1)
    b_r =jnp.pad(jnp.concatenate([b_router_group[0], b_router_expert[0].reshape(N_EXPERTS)]),
                  (0, LANES - n_router)).reshape(1, LANES)

    cos_p, sin_p = _rope_tables(np.arange(seq_p))
    cos_s, sin_s = _rope_tables(past_len + np.arange(seq_s))
    cosk = np.concatenate([cos_p, np.tile(cos_s, (TM // seq_s, 1))], axis=0)
    sink = np.concatenate([sin_p, np.tile(sin_s, (TM // seq_s, 1))], axis=0)
    state = jnp.concatenate([jnp.zeros((bp, CONV_W - 1, CONV_CH), F32), state_conv[0]], axis=0)

    cqn, ckv_p, kr_p, ckv_s, kr_s, conv_n, utail = _in_proj(
        xp, xs, row_vec(norm_mix[0]), w_t, row_vec(norm_q[0]), row_vec(norm_kv[0]),
        row_vec(norm_conv_out[0]), conv_w[0], cosk, sink, state, seq_p=seq_p, seq_s=seq_s)

    gao = row_vec(norm_attn_out[0])
    attn_p = _attention(cqn, w_q, w_ukt, w_uvh, np.tile(cos_p, (1, N_HEADS)), np.tile(sin_p, (1, N_HEADS)),
                        gao, ckv_p, kr_p, n_batch=bp, seq=seq_p, row0=0)
    attn_s = _attention(cqn, w_q, w_ukt, w_uvh, np.tile(cos_s, (1, N_HEADS)), np.tile(sin_s, (1, N_HEADS)),
                        gao, ckv_s, kr_s, n_batch=bs, seq=seq_s, row0=np_rows,
                        past_kv=cache_kv_latent[0], past_kr=jnp.swapaxes(cache_k_rope[0], 1, 2))

    h, xpk, mi, mf, cnt = _out_proj(attn_p, attn_s, conv_n, xp, xs, w_ob, row_vec(norm_ffn[0]),
                                    w_r2, b_r)

    counts = cnt[0, :N_EXPERTS].astype(jnp.int32)
    padded = (counts + MOE_BLOCK - 1) // MOE_BLOCK * MOE_BLOCK
    pad_end = jnp.cumsum(padded)
    pad_start = pad_end - padded
    n_blocks = -(-(m * 2) // MOE_BLOCK) + N_EXPERTS
    block_row0 = jnp.arange(n_blocks, dtype=jnp.int32) * MOE_BLOCK
    block_e = jnp.minimum(jnp.sum((pad_end[None, :] <= block_row0[:, None]).astype(jnp.int32), axis=1),
                          N_EXPERTS - 1)
    n_used = (pad_end[-1:] // MOE_BLOCK).astype(jnp.int32)
    expert_ids = jnp.arange(N_EXPERTS, dtype=jnp.int32)[:, None]

    def seg_start(e):
        return jnp.sum(jnp.where(expert_ids == e[None, :], pad_start[:, None], 0), axis=0)

    dest0 = seg_start(mi[0]) + mi[2]
    dest1 = seg_start(mi[1]) + mi[3]

    x_sorted = _dispatch(dest0, dest1, pad_start + counts, padded - counts, n_used, xpk, n_blocks)
    later = (expert_ids.T > block_e[:, None]) & (padded > 0)[None, :]
    next_e = jnp.min(jnp.where(later, expert_ids.T, N_EXPERTS), axis=1)
    next_e = jnp.where(next_e == N_EXPERTS, -1, next_e).astype(jnp.int32)
    y_sorted = _experts(block_e, n_used, next_e, x_sorted, w_gate[0], w_up[0], w_down[0])
    gfin = row_vec(norm_final)
    y_p, y_s = _combine(dest0, dest1, h, mf, gfin, y_sorted, n_prompt_rows=np_rows)

    ut = utail.reshape(m // CHUNK, SUBLANES, CONV_CH)
    tails = ut[:, SUBLANES - (CONV_W - 1):, :]
    p_last = (jnp.arange(bp) + 1) * (seq_p // CHUNK) - 1
    s_last = np_rows // CHUNK + (jnp.arange(bs) + 1) * (seq_s // CHUNK) - 1
    return (y_p.reshape(bp, seq_p, D_MODEL),
            y_s.reshape(bs, seq_s, D_MODEL),
            ckv_p.reshape(1, bp, seq_p, KV_LORA),
            jnp.swapaxes(kr_p, 1, 2)[None],
            tails[p_last][None],
            ckv_s.reshape(1, bs, seq_s, KV_LORA),
            jnp.swapaxes(kr_s, 1, 2)[None],
            tails[s_last][None])
```

```python
import functools

import jax
import jax.numpy as jnp
import numpy as np
from jax import lax
from jax.experimental import pallas as pl
from jax.experimental.pallas import tpu as pltpu

F32 = jnp.float32
BF16 = jnp.bfloat16

D_MODEL = 2048
N_HEADS = 8
QK_NOPE = 128
ROPE_DIM = 64
V_DIM = 128
Q_LORA = 512
KV_LORA = 512
ATTN_W = N_HEADS * V_DIM
CONV_CH = D_MODEL - ATTN_W
CONV_W = 3
CHUNK = 64
N_GROUPS = 4
EXPERTS_PER_GROUP = 8
N_EXPERTS = N_GROUPS * EXPERTS_PER_GROUP
D_FF = 512
ROPE_THETA = 10000.0
EPS = 1e-6
ATTN_SCALE = (QK_NOPE + ROPE_DIM) ** -0.5
EXP2_SCALE = ATTN_SCALE * 1.4426950408889634

LANES = 128
SUBLANES = 8
TM = 256
MOE_BLOCK = 256
TQ = 256
TK = 256
NEG_BIG = -1e30
V7X_VMEM_BYTES = 64 * 1024 * 1024
VMEM_LIMIT = V7X_VMEM_BYTES * 7 // 8


def _rms(v, g):
    return v * lax.rsqrt(jnp.mean(v * v, axis=-1, keepdims=True) + EPS) * g


def _lane_bcast(v, width):
    if width % LANES == 0:
        return jnp.concatenate([v] * (width // LANES), axis=1)
    assert width < LANES
    return v[:, :width]


def _pack_bf16_pairs(v):
    half = v.shape[-1] // 2
    lo = lax.bitcast_convert_type(v[..., :half].astype(BF16).astype(F32), jnp.uint32)
    hi = lax.bitcast_convert_type(v[..., half:].astype(BF16).astype(F32), jnp.uint32)
    return (lo >> 16) | (hi & jnp.uint32(0xFFFF0000))


def _unpack_bf16_pairs(w):
    return (lax.bitcast_convert_type(w << 16, F32),
            lax.bitcast_convert_type(w & jnp.uint32(0xFFFF0000), F32))


def _const_spec(shape):
    nd = len(shape)
    return pl.BlockSpec(shape, lambda *_: (0,) * nd, pipeline_mode=pl.Buffered(1))


def _in_proj_kernel(xp_ref, xs_ref, gmix_ref, wt_ref, gq_ref, gkv_ref, gco_ref, convw_ref,
                    cos_ref, sin_ref, state_ref,
                    cqn_ref, ckvp_ref, krp_ref, ckvs_ref, krs_ref, convn_ref, utail_ref, ext_ref,
                    *, n_prompt_tiles, tiles_per_seq, n_prompt_seq, sample_seq_len):
    i = pl.program_id(0)

    @pl.when(i == 0)
    def _():
        ext_ref[...] = jnp.zeros(ext_ref.shape, F32)

    def conv_block(u_sub, gate_sub, row0, length):
        ext_ref[SUBLANES:SUBLANES + length, :] = u_sub
        um1 = ext_ref[SUBLANES - 1:SUBLANES - 1 + length, :]
        um2 = ext_ref[SUBLANES - 2:SUBLANES - 2 + length, :]
        cw = convw_ref[...]
        conv = cw[0:1] * um2 + cw[1:2] * um1 + cw[2:3] * u_sub
        convn_ref[row0:row0 + length, :] = _rms(gate_sub * conv, gco_ref[...]).astype(BF16)

    def tile(x_ref, is_prompt):
        ckv_ref, krt_ref = (ckvp_ref, krp_ref) if is_prompt else (ckvs_ref, krs_ref)
        x = x_ref[...]
        xg = (x * gmix_ref[...]).astype(BF16)
        inv_rms = lax.rsqrt(jnp.mean(x * x, axis=-1, keepdims=True) + EPS)
        lat_w = Q_LORA + KV_LORA
        conv0 = lat_w + ROPE_DIM
        nt = (((1,), (1,)), ((), ()))

        def project(lo, hi):
            return inv_rms * lax.dot_general(xg, wt_ref[lo:hi, :], nt, preferred_element_type=F32)

        z_ch = project(conv0 + CONV_CH, conv0 + 3 * CONV_CH)
        u = z_ch[:, :CONV_CH] * z_ch[:, CONV_CH:]
        for j in range(TM // CHUNK):
            utail_ref[j] = u[CHUNK * (j + 1) - SUBLANES:CHUNK * (j + 1), :]
        gate_b = project(conv0, conv0 + CONV_CH)

        if is_prompt:
            first = (i % tiles_per_seq) == 0
            carried = ext_ref[TM + SUBLANES - 2:TM + SUBLANES, :]
            ext_ref[SUBLANES - 2:SUBLANES, :] = jnp.where(first, state_ref[i // tiles_per_seq], carried)
            conv_block(u, gate_b, 0, TM)
        else:
            n_sub = TM // sample_seq_len
            seq0 = n_prompt_seq + (i - n_prompt_tiles) * n_sub
            for k in range(n_sub):
                ext_ref[SUBLANES - 2:SUBLANES, :] = state_ref[seq0 + k]
                lo = k * sample_seq_len
                conv_block(u[lo:lo + sample_seq_len], gate_b[lo:lo + sample_seq_len], lo, sample_seq_len)

        zk = project(lat_w, conv0)
        zk_swapped = jnp.concatenate([zk[:, ROPE_DIM // 2:], zk[:, :ROPE_DIM // 2]], axis=1)
        k_rope = zk * cos_ref[...] + zk_swapped * sin_ref[...]
        if is_prompt:
            krt_ref[...] = k_rope.T
        else:
            for k in range(TM // sample_seq_len):
                krt_ref[k] = k_rope[k * sample_seq_len:(k + 1) * sample_seq_len, :].T
        ckv_ref[...] = _rms(project(Q_LORA, lat_w), gkv_ref[...])
        cqn_ref[...] = _rms(project(0, Q_LORA), gq_ref[...]).astype(BF16)

    @pl.when(i < n_prompt_tiles)
    def _():
        tile(xp_ref, True)

    @pl.when(i >= n_prompt_tiles)
    def _():
        tile(xs_ref, False)


def _in_proj(xp, xs, gmix, w_t, gq, gkv, gco, convw, cosk, sink, state, *, seq_p, seq_s):
    np_rows, ns_rows = xp.shape[0], xs.shape[0]
    m = np_rows + ns_rows
    npt, nst = np_rows // TM, ns_rows // TM
    tps = seq_p // TM
    n_prompt_seq = np_rows // seq_p
    last_p = npt - 1

    def tab_idx(i):
        return (jnp.where(i < npt, i % tps, tps), 0)

    row = lambda i: (i, 0)
    prow = lambda i: (jnp.minimum(i, last_p), 0)
    srow = lambda i: (jnp.maximum(i - npt, 0), 0)
    kern = functools.partial(_in_proj_kernel, n_prompt_tiles=npt, tiles_per_seq=tps,
                             n_prompt_seq=n_prompt_seq, sample_seq_len=seq_s)
    return pl.pallas_call(
        kern,
        grid=(npt + nst,),
        in_specs=[
            pl.BlockSpec((TM, D_MODEL), prow),
            pl.BlockSpec((TM, D_MODEL), srow),
            _const_spec((1, D_MODEL)),
            _const_spec(w_t.shape),
            _const_spec((1, Q_LORA)),
            _const_spec((1, KV_LORA)),
            _const_spec((1, CONV_CH)),
            _const_spec((CONV_W, CONV_CH)),
            pl.BlockSpec((TM, ROPE_DIM), tab_idx),
            pl.BlockSpec((TM, ROPE_DIM), tab_idx),
            _const_spec(state.shape),
        ],
        out_specs=[
            pl.BlockSpec((TM, Q_LORA), row),
            pl.BlockSpec((TM, KV_LORA), prow),
            pl.BlockSpec((None, ROPE_DIM, TM), lambda i: (jnp.minimum(i, last_p) // tps, 0,
                                                          jnp.minimum(i, last_p) % tps)),
            pl.BlockSpec((TM, KV_LORA), srow),
            pl.BlockSpec((TM // seq_s, ROPE_DIM, seq_s), lambda i: (jnp.maximum(i - npt, 0), 0, 0)),
            pl.BlockSpec((TM, CONV_CH), row),
            pl.BlockSpec((TM // CHUNK, SUBLANES, CONV_CH), lambda i: (i, 0, 0)),
        ],
        out_shape=[
            jax.ShapeDtypeStruct((m, Q_LORA), BF16),
            jax.ShapeDtypeStruct((np_rows, KV_LORA), F32),
            jax.ShapeDtypeStruct((n_prompt_seq, ROPE_DIM, seq_p), F32),
            jax.ShapeDtypeStruct((ns_rows, KV_LORA), F32),
            jax.ShapeDtypeStruct((ns_rows // seq_s, ROPE_DIM, seq_s), F32),
            jax.ShapeDtypeStruct((m, CONV_CH), BF16),
            jax.ShapeDtypeStruct((m // CHUNK, SUBLANES, CONV_CH), F32),
        ],
        scratch_shapes=[pltpu.VMEM((TM + SUBLANES, CONV_CH), F32)],
        compiler_params=pltpu.CompilerParams(dimension_semantics=("arbitrary",),
                                             vmem_limit_bytes=VMEM_LIMIT),
        name="in_proj",
    )(xp, xs, gmix, w_t, gq, gkv, gco, convw, cosk, sink, state)


def _attn_kernel(*refs, tq, n_past, causal):
    refs = list(refs)
    cqn_ref, wq_ref, wuk_ref, wuv_ref, cos_ref, sin_ref, gao_ref = refs[:7]
    refs = refs[7:]
    if n_past:
        pkv_ref, pkr_ref = refs[:2]
        refs = refs[2:]
    kv_ref, kr_ref, out_ref, qlat_ref, qr_ref, m_ref, l_ref, acc_ref, s_ref, klim_ref = refs

    qi = pl.program_id(1)
    rows = N_HEADS * tq

    q = jnp.dot(cqn_ref[...], wq_ref[...], preferred_element_type=F32)
    nope_w = N_HEADS * QK_NOPE
    rope_w = N_HEADS * ROPE_DIM
    qrope = q[:, nope_w:nope_w + rope_w] * cos_ref[...] + q[:, nope_w + rope_w:] * sin_ref[...]
    for h in range(N_HEADS):
        qn = q[:, h * QK_NOPE:(h + 1) * QK_NOPE].astype(BF16)
        ql = jnp.dot(qn, wuk_ref[h], preferred_element_type=F32)
        qlat_ref[h * tq:(h + 1) * tq, :] = ql.astype(BF16)
        qr_ref[h * tq:(h + 1) * tq, :] = qrope[:, h * ROPE_DIM:(h + 1) * ROPE_DIM].astype(BF16)


    nt = (((1,), (1,)), ((), ()))

    def scores(kc_f32, krt_f32):
        s = lax.dot_general(qlat_ref[...], kc_f32.astype(BF16), nt, preferred_element_type=F32)
        return s + jnp.dot(qr_ref[...], krt_f32.astype(BF16), preferred_element_type=F32)

    def update(s, kc_f32, mask, first=False):
        if mask is not None:
            s = jnp.where(mask, s, NEG_BIG)
        m_cur = jnp.max(s, axis=-1, keepdims=True)
        if first:
            m_new = jnp.broadcast_to(m_cur, m_ref.shape)
        else:
            m_prev = m_ref[...]
            m_new = jnp.maximum(m_prev, m_cur)
            alpha = jnp.exp2((m_prev - m_new) * EXP2_SCALE)
        p = jnp.exp2((s - _lane_bcast(m_new, s.shape[1])) * EXP2_SCALE)
        l_cur = jnp.sum(p, axis=-1, keepdims=True)
        pv = jnp.dot(p.astype(BF16), kc_f32.astype(BF16), preferred_element_type=F32)
        if first:
            l_ref[...] = jnp.broadcast_to(l_cur, l_ref.shape)
            acc_ref[...] = pv
        else:
            l_ref[...] = alpha * l_ref[...] + l_cur
            acc_ref[...] = _lane_bcast(alpha, KV_LORA) * acc_ref[...] + pv
        m_ref[...] = m_new

    def pipelined(kv, kr, lo, hi, last, mask_fn):
        def body(j, c):
            k0 = pl.multiple_of(j * TK, TK)
            k1 = pl.multiple_of(jnp.minimum(j + 1, last) * TK, TK)
            s_cur = s_ref[j % 2]
            s_ref[(j + 1) % 2] = scores(kv[pl.ds(k1, TK), :], kr[:, pl.ds(k1, TK)])
            update(s_cur, kv[pl.ds(k0, TK), :], None if mask_fn is None else mask_fn(k0))
            return c
        lax.fori_loop(lo, hi, body, 0)

    def pipelined_pairs(kv, kr, n_pairs, last):
        def body(i, c):
            ka = pl.multiple_of((2 * i + 1) * TK, TK)
            kb = pl.multiple_of((2 * i + 2) * TK, TK)
            kc = pl.multiple_of(jnp.minimum(2 * i + 3, last) * TK, TK)
            s_ref[0] = scores(kv[pl.ds(kb, TK), :], kr[:, pl.ds(kb, TK)])
            update(s_ref[1], kv[pl.ds(ka, TK), :], None)
            s_ref[1] = scores(kv[pl.ds(kc, TK), :], kr[:, pl.ds(kc, TK)])
            update(s_ref[0], kv[pl.ds(kb, TK), :], None)
            return c
        lax.fori_loop(0, n_pairs, body, 0)

    def first_block(kv, kr, last, mask):
        k1 = pl.multiple_of(jnp.minimum(1, last) * TK, TK)
        s_ref[0] = scores(kv[pl.ds(0, TK), :], kr[:, pl.ds(0, TK)])
        s_ref[1] = scores(kv[pl.ds(k1, TK), :], kr[:, pl.ds(k1, TK)])
        update(s_ref[0], kv[pl.ds(0, TK), :], mask, first=True)

    if n_past:
        n_pb = n_past // TK
        first_block(pkv_ref, pkr_ref, n_pb - 1, None)
        n_pairs = (n_pb - 1) // 2
        pipelined_pairs(pkv_ref, pkr_ref, n_pairs, n_pb - 1)
        if 1 + 2 * n_pairs < n_pb:
            pipelined(pkv_ref, pkr_ref, 1 + 2 * n_pairs, n_pb, n_pb - 1, None)

    if causal:
        n_blocks = ((qi + 1) * tq + TK - 1) // TK
        n_full = jnp.minimum((qi * tq // CHUNK + 1) * CHUNK // TK, n_blocks)

        assert tq & (tq - 1) == 0 and CHUNK & (CHUNK - 1) == 0
        r = lax.broadcasted_iota(jnp.int32, (rows, LANES), 0)
        q_pos = qi * tq + (r & (tq - 1))
        klim_ref[...] = (q_pos & ~(CHUNK - 1)) + CHUNK

        def mask_fn(k0):
            cidx = lax.broadcasted_iota(jnp.int32, (rows, TK), 1)
            return cidx < _lane_bcast(klim_ref[...] - k0, TK)

        first_block(kv_ref, kr_ref, n_blocks - 1, mask_fn(0))
        n_pairs = jnp.maximum(n_full - 1, 0) // 2
        pipelined_pairs(kv_ref, kr_ref, n_pairs, n_blocks - 1)
        pipelined(kv_ref, kr_ref, 1 + 2 * n_pairs, n_full, n_blocks - 1, None)
        pipelined(kv_ref, kr_ref, jnp.maximum(n_full, 1), n_blocks, n_blocks - 1, mask_fn)
    else:
        update(scores(kv_ref[...], kr_ref[...]), kv_ref[...], None)

    o = acc_ref[...] / _lane_bcast(l_ref[...], KV_LORA)
    parts = []
    for h in range(N_HEADS):
        oh = o[h * tq:(h + 1) * tq, :].astype(BF16)
        parts.append(jnp.dot(oh, wuv_ref[h], preferred_element_type=F32))
    attn = jnp.concatenate(parts, axis=-1)
    out_ref[...] = _rms(attn, gao_ref[...]).astype(BF16)


def _attention(cqn, w_q, w_ukt, w_uv, cosq, sinq, gao, ckv, krope, *, n_batch, seq, row0,
               past_kv=None, past_kr=None):
    causal = past_kv is None
    tq = TQ if causal else seq
    nq = seq // tq
    n_past = 0 if causal else past_kv.shape[1]
    if not causal:
        assert n_past % CHUNK == 0 and seq <= CHUNK and n_past % TK == 0
    blk0 = row0 // tq
    qrow = lambda b, q: (blk0 + b * nq + q, 0)
    in_specs = [
        pl.BlockSpec((tq, Q_LORA), qrow),
        _const_spec(w_q.shape),
        _const_spec(w_ukt.shape),
        _const_spec(w_uv.shape),
        pl.BlockSpec((tq, N_HEADS * ROPE_DIM), lambda b, q: (q, 0)),
        pl.BlockSpec((tq, N_HEADS * ROPE_DIM), lambda b, q: (q, 0)),
        _const_spec((1, ATTN_W)),
    ]
    args = [cqn, w_q, w_ukt, w_uv, cosq, sinq, gao]
    if n_past:
        in_specs += [pl.BlockSpec((None, n_past, KV_LORA), lambda b, q: (b, 0, 0)),
                     pl.BlockSpec((None, ROPE_DIM, n_past), lambda b, q: (b, 0, 0))]
        args += [past_kv, past_kr]
    in_specs += [pl.BlockSpec((seq, KV_LORA), lambda b, q: (b, 0)),
                 pl.BlockSpec((None, ROPE_DIM, seq), lambda b, q: (b, 0, 0))]
    args += [ckv, krope]
    rows = N_HEADS * tq
    kern = functools.partial(_attn_kernel, tq=tq, n_past=n_past, causal=causal)
    return pl.pallas_call(
        kern,
        grid=(n_batch, nq),
        in_specs=in_specs,
        out_specs=pl.BlockSpec((tq, ATTN_W), lambda b, q: (b * nq + q, 0)),
        out_shape=jax.ShapeDtypeStruct((n_batch * seq, ATTN_W), BF16),
        scratch_shapes=[
            pltpu.VMEM((rows, KV_LORA), BF16),
            pltpu.VMEM((rows, ROPE_DIM), BF16),
            pltpu.VMEM((rows, LANES), F32),
            pltpu.VMEM((rows, LANES), F32),
            pltpu.VMEM((rows, KV_LORA), F32),
            pltpu.VMEM((2, rows, TK), F32),
            pltpu.VMEM((rows, LANES), jnp.int32),
        ],
        compiler_params=pltpu.CompilerParams(dimension_semantics=("arbitrary", "arbitrary"),
                                             vmem_limit_bytes=VMEM_LIMIT),
        name="attn_prompt" if causal else "attn_sample",
    )(*args)


def _out_proj_kernel(attnp_ref, attns_ref, convn_ref, xp_ref, xs_ref, wo_ref, gffn_ref, wr_ref,
                     br_ref, h_ref, xpk_ref, mi_ref, mf_ref, cnt_ref, carry_ref, logit_ref, *, n_prompt_tiles):
    i = pl.program_id(0)

    @pl.when(i == 0)
    def _():
        carry_ref[...] = jnp.zeros(carry_ref.shape, F32)
        logit_ref[...] = jnp.zeros(logit_ref.shape, F32)

    def tile(x_ref, attn_ref):
        prev_logits = logit_ref[...]
        y = jnp.dot(attn_ref[...], wo_ref[:ATTN_W, :], preferred_element_type=F32)
        y = y + jnp.dot(convn_ref[...], wo_ref[ATTN_W:, :], preferred_element_type=F32)
        h = x_ref[...] + y
        h_ref[...] = h
        xn = _rms(h, gffn_ref[...])

        half = D_MODEL // 2
        xh = xn.astype(BF16)
        xh32 = xh.astype(F32)
        lo = lax.bitcast_convert_type(xh32[:, :half], jnp.uint32)
        hi = lax.bitcast_convert_type(xh32[:, half:], jnp.uint32)
        xpk_ref[...] = (lo >> 16) | (hi & jnp.uint32(0xFFFF0000))

        xl = (xn - xh32).astype(BF16)
        hh_hl = jnp.dot(xh, wr_ref[...], preferred_element_type=F32)
        lh = jnp.dot(xl, wr_ref[:, :LANES], preferred_element_type=F32)
        logit_ref[...] = hh_hl[:, :LANES] + (lh + hh_hl[:, LANES:]) + br_ref[...]

        logits = prev_logits
        counted = (i > 0).astype(F32)
        lane = lax.broadcasted_iota(jnp.int32, (TM, LANES), 1).astype(F32)
        ninf = -jnp.inf
        far = float(LANES)

        def first_argmax(v):
            vmax = jnp.max(v, axis=-1, keepdims=True)
            return vmax, jnp.min(jnp.where(v == vmax, lane, far), axis=-1, keepdims=True)

        gl = jnp.where(lane < N_GROUPS, logits, ninf)
        gmax, gidx = first_argmax(gl)
        g_p = 1.0 / jnp.sum(jnp.exp(gl - gmax), axis=-1, keepdims=True)
        e_lo = N_GROUPS + EXPERTS_PER_GROUP * gidx
        el = jnp.where((lane >= e_lo) & (lane < e_lo + EXPERTS_PER_GROUP), logits, ninf)
        e1max, i1 = first_argmax(el)
        z = jnp.sum(jnp.exp(el - e1max), axis=-1, keepdims=True)
        el2 = jnp.where(lane == i1, ninf, el)
        e2max, i2 = first_argmax(el2)
        p1 = 1.0 / z
        p2 = jnp.exp(e2max - e1max) / z
        den = p1 + p2
        g0 = g_p * p1 / den
        g1 = g_p * p2 / den
        e0 = i1 - N_GROUPS
        e1 = i2 - N_GROUPS

        oh0 = lane == e0
        oh1 = lane == e1
        oh = jnp.where(oh0 | oh1, 1.0, 0.0)
        r = lax.broadcasted_iota(jnp.int32, (TM, TM), 0)
        c = lax.broadcasted_iota(jnp.int32, (TM, TM), 1)
        ltri = jnp.where(r > c, 1.0, 0.0).astype(BF16)
        before = jnp.dot(ltri, oh.astype(BF16), preferred_element_type=F32) + carry_ref[...]
        rank0 = jnp.sum(jnp.where(oh0, before, 0.0), axis=-1, keepdims=True)
        rank1 = jnp.sum(jnp.where(oh1, before, 0.0), axis=-1, keepdims=True)
        total = carry_ref[...] + counted * jnp.sum(oh, axis=0, keepdims=True)
        carry_ref[...] = total
        cnt_ref[...] = jnp.broadcast_to(total, cnt_ref.shape)

        mi = jnp.where(lane == 0, e0, jnp.where(lane == 1, e1, jnp.where(lane == 2, rank0, rank1)))
        mi_ref[...] = jnp.transpose(mi)[:SUBLANES, :].astype(jnp.int32)
        mf_ref[...] = jnp.where(lane == 0, g0, g1)

    @pl.when(i < n_prompt_tiles)
    def _():
        tile(xp_ref, attnp_ref)

    @pl.when(i >= n_prompt_tiles)
    def _():
        tile(xs_ref, attns_ref)


def _out_proj(attn_p, attn_s, conv_n, xp, xs, w_ob, gffn, w_r2, b_r):
    m = conv_n.shape[0]
    npt = xp.shape[0] // TM
    n_tiles = m // TM
    last_p, last_s, last = npt - 1, n_tiles - npt - 1, n_tiles - 1
    row = lambda i: (jnp.minimum(i, last), 0)
    prow = lambda i: (jnp.minimum(i, last_p), 0)
    srow = lambda i: (jnp.clip(i - npt, 0, last_s), 0)
    lag = lambda i: jnp.maximum(i - 1, 0)
    return pl.pallas_call(
        functools.partial(_out_proj_kernel, n_prompt_tiles=npt),
        grid=(n_tiles + 1,),
        in_specs=[
            pl.BlockSpec((TM, ATTN_W), prow),
            pl.BlockSpec((TM, ATTN_W), srow),
            pl.BlockSpec((TM, CONV_CH), row),
            pl.BlockSpec((TM, D_MODEL), prow),
            pl.BlockSpec((TM, D_MODEL), srow),
            _const_spec(w_ob.shape),
            _const_spec((1, D_MODEL)),
            _const_spec(w_r2.shape),
            _const_spec((1, LANES)),
        ],
        out_specs=[
            pl.BlockSpec((TM, D_MODEL), row),
            pl.BlockSpec((TM, D_MODEL // 2), row),
            pl.BlockSpec((SUBLANES, TM), lambda i: (0, lag(i))),
            pl.BlockSpec((TM, LANES), lambda i: (lag(i), 0)),
            pl.BlockSpec((SUBLANES, LANES), lambda i: (0, 0)),
        ],
        out_shape=[
            jax.ShapeDtypeStruct((m, D_MODEL), F32),
            jax.ShapeDtypeStruct((m, D_MODEL // 2), jnp.uint32),
            jax.ShapeDtypeStruct((SUBLANES, m), jnp.int32),
            jax.ShapeDtypeStruct((m, LANES), F32),
            jax.ShapeDtypeStruct((SUBLANES, LANES), F32),
        ],
        scratch_shapes=[pltpu.VMEM((1, LANES), F32), pltpu.VMEM((TM, LANES), F32)],
        compiler_params=pltpu.CompilerParams(dimension_semantics=("arbitrary",),
                                             vmem_limit_bytes=VMEM_LIMIT),
        name="out_proj",
    )(attn_p, attn_s, conv_n, xp, xs, w_ob, gffn, w_r2, b_r)


def _dispatch_kernel(d0_ref, d1_ref, zlo_ref, zn_ref, nu_ref, xpk_ref, xs_hbm, zeros_ref, sems, *, n_blocks):
    i = pl.program_id(0)
    sem = sems.at[0]
    zsem = sems.at[1]

    def zero_fill(act):
        def per_expert(e, c):
            lo = zlo_ref[e]
            n = zn_ref[e]
            head = (-lo) & (SUBLANES - 1)
            for r in range(SUBLANES - 1):
                @pl.when(r < head)
                def _(r=r):
                    act(pltpu.make_async_copy(zeros_ref.at[pl.ds(0, 1)], xs_hbm.at[pl.ds(lo + r, 1)], zsem))
            off = lo + head
            rest = n - head
            size = MOE_BLOCK // 2
            while size >= SUBLANES:
                @pl.when((rest & size) != 0)
                def _(off=off, size=size):
                    dst = xs_hbm.at[pl.ds(pl.multiple_of(off, SUBLANES), size)]
                    act(pltpu.make_async_copy(zeros_ref.at[pl.ds(0, size)], dst, zsem))
                off = off + (rest & size)
                size //= 2
            return c

        def per_block(b, c):
            dst = xs_hbm.at[pl.ds(pl.multiple_of(b * MOE_BLOCK, MOE_BLOCK), MOE_BLOCK)]
            act(pltpu.make_async_copy(zeros_ref, dst, zsem))
            return c

        lax.fori_loop(0, N_EXPERTS, per_expert, 0)
        lax.fori_loop(nu_ref[0], n_blocks, per_block, 0)

    @pl.when(i == 0)
    def _():
        zeros_ref[...] = jnp.zeros(zeros_ref.shape, zeros_ref.dtype)
        zero_fill(lambda cp: cp.start())

    @pl.when(i == pl.num_programs(0) - 1)
    def _():
        zero_fill(lambda cp: cp.wait())

    base = i * TM

    def start(g, c):
        for u in range(SUBLANES):
            r = base + g * SUBLANES + u
            src = xpk_ref.at[g, pl.ds(u, 1)]
            pltpu.make_async_copy(src, xs_hbm.at[pl.ds(d0_ref[r], 1)], sem).start()
            pltpu.make_async_copy(src, xs_hbm.at[pl.ds(d1_ref[r], 1)], sem).start()
        return c

    lax.fori_loop(0, TM // SUBLANES, start, 0)
    for _ in range(2):
        pltpu.make_async_copy(xs_hbm.at[pl.ds(0, TM)], xs_hbm.at[pl.ds(0, TM)], sem).wait()


def _dispatch(dest0, dest1, pad_lo, n_pad, n_used, xpk, n_blocks):
    m = xpk.shape[0]
    grid_spec = pltpu.PrefetchScalarGridSpec(
        num_scalar_prefetch=5,
        grid=(m // TM,),
        in_specs=[pl.BlockSpec((TM // SUBLANES, SUBLANES, D_MODEL // 2), lambda i, *_: (i, 0, 0))],
        out_specs=pl.BlockSpec(memory_space=pl.ANY),
        scratch_shapes=[pltpu.VMEM((MOE_BLOCK, D_MODEL // 2), jnp.uint32),
                        pltpu.SemaphoreType.DMA((2,))],
    )
    return pl.pallas_call(
        functools.partial(_dispatch_kernel, n_blocks=n_blocks),
        grid_spec=grid_spec,
        out_shape=jax.ShapeDtypeStruct((n_blocks * MOE_BLOCK, D_MODEL // 2), jnp.uint32),
        compiler_params=pltpu.CompilerParams(dimension_semantics=("arbitrary",)),
        name="dispatch",
    )(dest0, dest1, pad_lo, n_pad, n_used, xpk.reshape(m // SUBLANES, SUBLANES, D_MODEL // 2))


def _experts_kernel(be_ref, nu_ref, nxt_ref, x_ref, wg_hbm, wu_hbm, wd_hbm, y_ref,
                    sg_ref, su_ref, sd_ref, wgb_ref, wub_ref, wdb_ref, sems):
    b = pl.program_id(0)
    active = b < nu_ref[0]
    new_expert = jnp.logical_or(b == 0, be_ref[b] != be_ref[jnp.maximum(b - 1, 0)])

    def weight_copies(e):
        return (pltpu.make_async_copy(wg_hbm.at[e], sg_ref, sems.at[0]),
                pltpu.make_async_copy(wu_hbm.at[e], su_ref, sems.at[1]),
                pltpu.make_async_copy(wd_hbm.at[e], sd_ref, sems.at[2]))

    @pl.when(b == 0)
    def _():
        for cp in weight_copies(be_ref[0]):
            cp.start()

    def block(first_of_expert):
        half = D_MODEL // 2
        if first_of_expert:
            for cp in weight_copies(be_ref[b]):
                cp.wait()
        xa, xb = (v.astype(BF16) for v in _unpack_bf16_pairs(x_ref[...]))
        if first_of_expert:
            wgb_ref[...] = sg_ref[...].astype(BF16)
        g = jnp.dot(xa, wgb_ref[:half, :], preferred_element_type=F32)
        g = g + jnp.dot(xb, wgb_ref[half:, :], preferred_element_type=F32)
        if first_of_expert:
            wub_ref[...] = su_ref[...].astype(BF16)
        u = jnp.dot(xa, wub_ref[:half, :], preferred_element_type=F32)
        u = u + jnp.dot(xb, wub_ref[half:, :], preferred_element_type=F32)
        hmid = (g * jax.nn.sigmoid(g)) * u
        if first_of_expert:
            wdb_ref[...] = sd_ref[...].astype(BF16)
        y = jnp.dot(hmid.astype(BF16), wdb_ref[...], preferred_element_type=F32)
        y_ref[...] = _pack_bf16_pairs(y)
        if first_of_expert:
            @pl.when(nxt_ref[b] >= 0)
            def _():
                for cp in weight_copies(nxt_ref[b]):
                    cp.start()

    @pl.when(jnp.logical_and(active, new_expert))
    def _():
        block(True)

    @pl.when(jnp.logical_and(active, jnp.logical_not(new_expert)))
    def _():
        block(False)

    @pl.when(b >= nu_ref[0])
    def _():
        y_ref[...] = jnp.zeros(y_ref.shape, y_ref.dtype)


def _experts(block_e, n_used, next_e, x_sorted, w_gate, w_up, w_down):
    p = x_sorted.shape[0]
    nb = p // MOE_BLOCK

    def xrow(b, be, nu, nxt):
        return (jnp.maximum(jnp.minimum(b, nu[0] - 1), 0), 0)

    grid_spec = pltpu.PrefetchScalarGridSpec(
        num_scalar_prefetch=3,
        grid=(nb,),
        in_specs=[
            pl.BlockSpec((MOE_BLOCK, D_MODEL // 2), xrow),
            pl.BlockSpec(memory_space=pl.ANY),
            pl.BlockSpec(memory_space=pl.ANY),
            pl.BlockSpec(memory_space=pl.ANY),
        ],
        out_specs=pl.BlockSpec((MOE_BLOCK, D_MODEL // 2), lambda b, be, nu, nxt: (b, 0)),
        scratch_shapes=[pltpu.VMEM((D_MODEL, D_FF), F32), pltpu.VMEM((D_MODEL, D_FF), F32),
                        pltpu.VMEM((D_FF, D_MODEL), F32),
                        pltpu.VMEM((D_MODEL, D_FF), BF16), pltpu.VMEM((D_MODEL, D_FF), BF16),
                        pltpu.VMEM((D_FF, D_MODEL), BF16),
                        pltpu.SemaphoreType.DMA((3,))],
    )
    return pl.pallas_call(
        _experts_kernel,
        grid_spec=grid_spec,
        out_shape=jax.ShapeDtypeStruct((p, D_MODEL // 2), jnp.uint32),
        compiler_params=pltpu.CompilerParams(dimension_semantics=("arbitrary",),
                                             vmem_limit_bytes=VMEM_LIMIT),
        name="experts",
    )(block_e, n_used, next_e, x_sorted, w_gate, w_up, w_down)


def _combine_kernel(d0_ref, d1_ref, h_ref, mf_ref, gfin_ref, y_hbm, outp_ref, outs_ref, y0_ref, y1_ref, sems,
                    *, n_tiles, n_prompt_tiles):
    i = pl.program_id(0)

    def gather(tile, slot, act):
        base = tile * TM

        def body(g, c):
            for u in range(SUBLANES):
                r = base + g * SUBLANES + u
                act(pltpu.make_async_copy(y_hbm.at[pl.ds(d0_ref[r], 1)], y0_ref.at[slot, g, pl.ds(u, 1)],
                                          sems.at[slot]))
                act(pltpu.make_async_copy(y_hbm.at[pl.ds(d1_ref[r], 1)], y1_ref.at[slot, g, pl.ds(u, 1)],
                                          sems.at[slot]))
            return c
        lax.fori_loop(0, TM // SUBLANES, body, 0)

    @pl.when(i == 0)
    def _():
        gather(0, 0, lambda cp: cp.start())

    @pl.when(i + 1 < n_tiles)
    def _():
        gather(i + 1, (i + 1) % 2, lambda cp: cp.start())

    slot = i % 2
    for _ in range(2):
        pltpu.make_async_copy(y_hbm.at[pl.ds(0, TM)], y_hbm.at[pl.ds(0, TM)], sems.at[slot]).wait()

    def finish(out_ref):
        mf = mf_ref[...]
        g0, g1 = mf[:, :, 0:1], mf[:, :, 1:2]
        half = D_MODEL // 2
        a0, b0 = _unpack_bf16_pairs(y0_ref[slot])
        a1, b1 = _unpack_bf16_pairs(y1_ref[slot])
        o_lo = h_ref[:, :, :half] + (g0 * a0 + g1 * a1)
        o_hi = h_ref[:, :, half:] + (g0 * b0 + g1 * b1)
        sumsq = jnp.sum(o_lo * o_lo, axis=-1, keepdims=True) + jnp.sum(o_hi * o_hi, axis=-1, keepdims=True)
        inv_rms = lax.rsqrt(sumsq / D_MODEL + EPS)
        out_ref[:, :, :half] = o_lo * inv_rms * gfin_ref[:, :, :half]
        out_ref[:, :, half:] = o_hi * inv_rms * gfin_ref[:, :, half:]

    @pl.when(i < n_prompt_tiles)
    def _():
        finish(outp_ref)

    @pl.when(i >= n_prompt_tiles)
    def _():
        finish(outs_ref)


def _combine(dest0, dest1, h, mf, gfin, y_sorted, *, n_prompt_rows):
    m = h.shape[0]
    npt = n_prompt_rows // TM
    tg = TM // SUBLANES
    grouped = lambda a: a.reshape(a.shape[0] // SUBLANES, SUBLANES, a.shape[1])
    grid_spec = pltpu.PrefetchScalarGridSpec(
        num_scalar_prefetch=2,
        grid=(m // TM,),
        in_specs=[
            pl.BlockSpec((tg, SUBLANES, D_MODEL), lambda i, *_: (i, 0, 0)),
            pl.BlockSpec((tg, SUBLANES, LANES), lambda i, *_: (i, 0, 0)),
            pl.BlockSpec((1, 1, D_MODEL), lambda i, *_: (0, 0, 0)),
            pl.BlockSpec(memory_space=pl.ANY),
        ],
        out_specs=[pl.BlockSpec((tg, SUBLANES, D_MODEL), lambda i, *_: (jnp.minimum(i, npt - 1), 0, 0)),
                   pl.BlockSpec((tg, SUBLANES, D_MODEL), lambda i, *_: (jnp.maximum(i - npt, 0), 0, 0))],
        scratch_shapes=[pltpu.VMEM((2, TM // SUBLANES, SUBLANES, D_MODEL // 2), jnp.uint32),
                        pltpu.VMEM((2, TM // SUBLANES, SUBLANES, D_MODEL // 2), jnp.uint32),
                        pltpu.SemaphoreType.DMA((2,))],
    )
    y_p, y_s = pl.pallas_call(
        functools.partial(_combine_kernel, n_tiles=m // TM, n_prompt_tiles=npt),
        grid_spec=grid_spec,
        out_shape=[jax.ShapeDtypeStruct((n_prompt_rows // SUBLANES, SUBLANES, D_MODEL), F32),
                   jax.ShapeDtypeStruct(((m - n_prompt_rows) // SUBLANES, SUBLANES, D_MODEL), F32)],
        compiler_params=pltpu.CompilerParams(dimension_semantics=("arbitrary",),
                                             vmem_limit_bytes=VMEM_LIMIT),
        name="combine",
    )(dest0, dest1, grouped(h), grouped(mf), gfin.reshape(1, 1, D_MODEL), y_sorted)
    return y_p.reshape(n_prompt_rows, D_MODEL), y_s.reshape(m - n_prompt_rows, D_MODEL)


def _rope_tables(pos):
    f32 = np.float32
    inv = np.power(f32(ROPE_THETA), -np.arange(0, ROPE_DIM, 2, dtype=f32) / f32(ROPE_DIM)).astype(f32)
    ang = (pos.astype(f32)[:, None] * inv[None, :]).astype(f32)
    cos, sin = np.cos(ang).astype(f32), np.sin(ang).astype(f32)
    return np.concatenate([cos, cos], axis=-1), np.concatenate([-sin, sin], axis=-1)


def _swap_halves(w):
    return jnp.concatenate([w[..., ROPE_DIM // 2:], w[..., :ROPE_DIM // 2]], axis=-1)


def kernel(x_prompt, x_sample, cache_kv_latent, cache_k_rope, state_conv, norm_mix, w_in, norm_q, w_uq,
           norm_kv, w_uk, w_uv, conv_w, norm_attn_out, norm_conv_out, w_o, norm_ffn, w_router_group,
           b_router_group, w_router_expert, b_router_expert, w_gate, w_up, w_down, norm_final):
    assert w_in.shape[0] == 1, "single-layer trunk"
    bp, seq_p, _ = x_prompt.shape
    bs, seq_s, _ = x_sample.shape
    past_len = cache_kv_latent.shape[2]
    np_rows, ns_rows = bp * seq_p, bs * seq_s
    m = np_rows + ns_rows
    assert seq_p % TM == 0 and TM % seq_s == 0 and ns_rows % TM == 0 and seq_s == CHUNK

    xp = x_prompt.reshape(np_rows, D_MODEL)
    xs = x_sample.reshape(ns_rows, D_MODEL)
    row_vec = lambda v: v.reshape(1, -1)

    assert w_in.shape[2] == Q_LORA + KV_LORA + ROPE_DIM + 3 * CONV_CH
    w_t = jnp.swapaxes(w_in[0], 0, 1).astype(BF16)
    wq4 = w_uq[0].reshape(Q_LORA, N_HEADS, QK_NOPE + ROPE_DIM)
    wq_rope = wq4[:, :, QK_NOPE:]
    w_q = jnp.concatenate([wq4[:, :, :QK_NOPE].reshape(Q_LORA, -1), wq_rope.reshape(Q_LORA, -1),
                           _swap_halves(wq_rope).reshape(Q_LORA, -1)], axis=1).astype(BF16)
    w_ukt = jnp.transpose(w_uk[0], (1, 2, 0)).astype(BF16)
    w_uvh = jnp.transpose(w_uv[0], (1, 0, 2)).astype(BF16)
    w_ob = w_o[0].astype(BF16)
    n_router = N_GROUPS + N_EXPERTS
    w_r = jnp.concatenate([w_router_group[0], w_router_expert[0].reshape(D_MODEL, N_EXPERTS)], axis=1)
    w_r = jnp.pad(w_r, ((0, 0), (0, LANES - n_router)))
    w_rh = w_r.astype(BF16)
    w_rl = (w_r - w_rh.astype(F32)).astype(BF16)
    w_r2 = jnp.concatenate([w_rh, w_rl], axis=1)
    b_r =jnp.pad(jnp.concatenate([b_router_group[0], b_router_expert[0].reshape(N_EXPERTS)]),
                  (0, LANES - n_router)).reshape(1, LANES)

    cos_p, sin_p = _rope_tables(np.arange(seq_p))
    cos_s, sin_s = _rope_tables(past_len + np.arange(seq_s))
    cosk = np.concatenate([cos_p, np.tile(cos_s, (TM // seq_s, 1))], axis=0)
    sink = np.concatenate([sin_p, np.tile(sin_s, (TM // seq_s, 1))], axis=0)
    state = jnp.concatenate([jnp.zeros((bp, CONV_W - 1, CONV_CH), F32), state_conv[0]], axis=0)

    cqn, ckv_p, kr_p, ckv_s, kr_s, conv_n, utail = _in_proj(
        xp, xs, row_vec(norm_mix[0]), w_t, row_vec(norm_q[0]), row_vec(norm_kv[0]),
        row_vec(norm_conv_out[0]), conv_w[0], cosk, sink, state, seq_p=seq_p, seq_s=seq_s)

    gao = row_vec(norm_attn_out[0])
    attn_p = _attention(cqn, w_q, w_ukt, w_uvh, np.tile(cos_p, (1, N_HEADS)), np.tile(sin_p, (1, N_HEADS)),
                        gao, ckv_p, kr_p, n_batch=bp, seq=seq_p, row0=0)
    attn_s = _attention(cqn, w_q, w_ukt, w_uvh, np.tile(cos_s, (1, N_HEADS)), np.tile(sin_s, (1, N_HEADS)),
                        gao, ckv_s, kr_s, n_batch=bs, seq=seq_s, row0=np_rows,
                        past_kv=cache_kv_latent[0], past_kr=jnp.swapaxes(cache_k_rope[0], 1, 2))

    h, xpk, mi, mf, cnt = _out_proj(attn_p, attn_s, conv_n, xp, xs, w_ob, row_vec(norm_ffn[0]),
                                    w_r2, b_r)

    counts = cnt[0, :N_EXPERTS].astype(jnp.int32)
    padded = (counts + MOE_BLOCK - 1) // MOE_BLOCK * MOE_BLOCK
    pad_end = jnp.cumsum(padded)
    pad_start = pad_end - padded
    n_blocks = -(-(m * 2) // MOE_BLOCK) + N_EXPERTS
    block_row0 = jnp.arange(n_blocks, dtype=jnp.int32) * MOE_BLOCK
    block_e = jnp.minimum(jnp.sum((pad_end[None, :] <= block_row0[:, None]).astype(jnp.int32), axis=1),
                          N_EXPERTS - 1)
    n_used = (pad_end[-1:] // MOE_BLOCK).astype(jnp.int32)
    expert_ids = jnp.arange(N_EXPERTS, dtype=jnp.int32)[:, None]

    def seg_start(e):
        return jnp.sum(jnp.where(expert_ids == e[None, :], pad_start[:, None], 0), axis=0)

    dest0 = seg_start(mi[0]) + mi[2]
    dest1 = seg_start(mi[1]) + mi[3]

    x_sorted = _dispatch(dest0, dest1, pad_start + counts, padded - counts, n_used, xpk, n_blocks)
    later = (expert_ids.T > block_e[:, None]) & (padded > 0)[None, :]
    next_e = jnp.min(jnp.where(later, expert_ids.T, N_EXPERTS), axis=1)
    next_e = jnp.where(next_e == N_EXPERTS, -1, next_e).astype(jnp.int32)
    y_sorted = _experts(block_e, n_used, next_e, x_sorted, w_gate[0], w_up[0], w_down[0])
    gfin = row_vec(norm_final)
    y_p, y_s = _combine(dest0, dest1, h, mf, gfin, y_sorted, n_prompt_rows=np_rows)

    ut = utail.reshape(m // CHUNK, SUBLANES, CONV_CH)
    tails = ut[:, SUBLANES - (CONV_W - 1):, :]
    p_last = (jnp.arange(bp) + 1) * (seq_p // CHUNK) - 1
    s_last = np_rows // CHUNK + (jnp.arange(bs) + 1) * (seq_s // CHUNK) - 1
    return (y_p.reshape(bp, seq_p, D_MODEL),
            y_s.reshape(bs, seq_s, D_MODEL),
            ckv_p.reshape(1, bp, seq_p, KV_LORA),
            jnp.swapaxes(kr_p, 1, 2)[None],
            tails[p_last][None],
            ckv_s.reshape(1, bs, seq_s, KV_LORA),
            jnp.swapaxes(kr_s, 1, 2)[None],
            tails[s_last][None])
```

```python
import functools

import jax
import jax.numpy as jnp
import numpy as np
from jax import lax
from jax.experimental import pallas as pl
from jax.experimental.pallas import tpu as pltpu

F32 = jnp.float32
BF16 = jnp.bfloat16

D_MODEL = 2048
N_HEADS = 8
QK_NOPE = 128
ROPE_DIM = 64
V_DIM = 128
Q_LORA = 512
KV_LORA = 512
ATTN_W = N_HEADS * V_DIM
CONV_CH = D_MODEL - ATTN_W
CONV_W = 3
CHUNK = 64
N_GROUPS = 4
EXPERTS_PER_GROUP = 8
N_EXPERTS = N_GROUPS * EXPERTS_PER_GROUP
D_FF = 512
ROPE_THETA = 10000.0
EPS = 1e-6
ATTN_SCALE = (QK_NOPE + ROPE_DIM) ** -0.5
EXP2_SCALE = ATTN_SCALE * 1.4426950408889634

LANES = 128
SUBLANES = 8
TM = 256
MOE_BLOCK = 256
TQ = 256
TK = 256
WEIGHT_DMA_PARTS = 4
NEG_BIG = -1e30
V7X_VMEM_BYTES = 64 * 1024 * 1024
VMEM_LIMIT = V7X_VMEM_BYTES * 7 // 8


def _rms(v, g):
    return v * lax.rsqrt(jnp.mean(v * v, axis=-1, keepdims=True) + EPS) * g


def _lane_bcast(v, width):
    if width % LANES == 0:
        return jnp.concatenate([v] * (width // LANES), axis=1)
    assert width < LANES
    return v[:, :width]


def _pack_bf16_pairs(v):
    half = v.shape[-1] // 2
    lo = lax.bitcast_convert_type(v[..., :half].astype(BF16).astype(F32), jnp.uint32)
    hi = lax.bitcast_convert_type(v[..., half:].astype(BF16).astype(F32), jnp.uint32)
    return (lo >> 16) | (hi & jnp.uint32(0xFFFF0000))


def _unpack_bf16_pairs(w):
    return (lax.bitcast_convert_type(w << 16, F32),
            lax.bitcast_convert_type(w & jnp.uint32(0xFFFF0000), F32))


def _const_spec(shape):
    nd = len(shape)
    return pl.BlockSpec(shape, lambda *_: (0,) * nd, pipeline_mode=pl.Buffered(1))


def _in_proj_kernel(xp_ref, xs_ref, gmix_ref, wt_ref, gq_ref, gkv_ref, gco_ref, convw_ref,
                    cos_ref, sin_ref, state_ref,
                    cqn_ref, ckvp_ref, krp_ref, ckvs_ref, krs_ref, convn_ref, utail_ref, ext_ref,
                    *, n_prompt_tiles, tiles_per_seq, n_prompt_seq, sample_seq_len):
    i = pl.program_id(0)

    @pl.when(i == 0)
    def _():
        ext_ref[...] = jnp.zeros(ext_ref.shape, F32)

    def conv_block(u_sub, gate_sub, row0, length):
        ext_ref[SUBLANES:SUBLANES + length, :] = u_sub
        um1 = ext_ref[SUBLANES - 1:SUBLANES - 1 + length, :]
        um2 = ext_ref[SUBLANES - 2:SUBLANES - 2 + length, :]
        cw = convw_ref[...]
        conv = cw[0:1] * um2 + cw[1:2] * um1 + cw[2:3] * u_sub
        convn_ref[row0:row0 + length, :] = _rms(gate_sub * conv, gco_ref[...]).astype(BF16)

    def tile(x_ref, is_prompt):
        ckv_ref, krt_ref = (ckvp_ref, krp_ref) if is_prompt else (ckvs_ref, krs_ref)
        x = x_ref[...]
        xg = (x * gmix_ref[...]).astype(BF16)
        inv_rms = lax.rsqrt(jnp.mean(x * x, axis=-1, keepdims=True) + EPS)
        lat_w = Q_LORA + KV_LORA
        conv0 = lat_w + ROPE_DIM
        nt = (((1,), (1,)), ((), ()))

        def project(lo, hi):
            return inv_rms * lax.dot_general(xg, wt_ref[lo:hi, :], nt, preferred_element_type=F32)

        z_ch = project(conv0 + CONV_CH, conv0 + 3 * CONV_CH)
        u = z_ch[:, :CONV_CH] * z_ch[:, CONV_CH:]
        for j in range(TM // CHUNK):
            utail_ref[j] = u[CHUNK * (j + 1) - SUBLANES:CHUNK * (j + 1), :]
        gate_b = project(conv0, conv0 + CONV_CH)

        if is_prompt:
            first = (i % tiles_per_seq) == 0
            carried = ext_ref[TM + SUBLANES - 2:TM + SUBLANES, :]
            ext_ref[SUBLANES - 2:SUBLANES, :] = jnp.where(first, state_ref[i // tiles_per_seq], carried)
            conv_block(u, gate_b, 0, TM)
        else:
            n_sub = TM // sample_seq_len
            seq0 = n_prompt_seq + (i - n_prompt_tiles) * n_sub
            for k in range(n_sub):
                ext_ref[SUBLANES - 2:SUBLANES, :] = state_ref[seq0 + k]
                lo = k * sample_seq_len
                conv_block(u[lo:lo + sample_seq_len], gate_b[lo:lo + sample_seq_len], lo, sample_seq_len)

        zk = project(lat_w, conv0)
        zk_swapped = jnp.concatenate([zk[:, ROPE_DIM // 2:], zk[:, :ROPE_DIM // 2]], axis=1)
        k_rope = zk * cos_ref[...] + zk_swapped * sin_ref[...]
        if is_prompt:
            krt_ref[...] = k_rope.T
        else:
            for k in range(TM // sample_seq_len):
                krt_ref[k] = k_rope[k * sample_seq_len:(k + 1) * sample_seq_len, :].T
        ckv_ref[...] = _rms(project(Q_LORA, lat_w), gkv_ref[...])
        cqn_ref[...] = _rms(project(0, Q_LORA), gq_ref[...]).astype(BF16)

    @pl.when(i < n_prompt_tiles)
    def _():
        tile(xp_ref, True)

    @pl.when(i >= n_prompt_tiles)
    def _():
        tile(xs_ref, False)


def _in_proj(xp, xs, gmix, w_t, gq, gkv, gco, convw, cosk, sink, state, *, seq_p, seq_s):
    np_rows, ns_rows = xp.shape[0], xs.shape[0]
    m = np_rows + ns_rows
    npt, nst = np_rows // TM, ns_rows // TM
    tps = seq_p // TM
    n_prompt_seq = np_rows // seq_p
    last_p = npt - 1

    def tab_idx(i):
        return (jnp.where(i < npt, i % tps, tps), 0)

    row = lambda i: (i, 0)
    prow = lambda i: (jnp.minimum(i, last_p), 0)
    srow = lambda i: (jnp.maximum(i - npt, 0), 0)
    kern = functools.partial(_in_proj_kernel, n_prompt_tiles=npt, tiles_per_seq=tps,
                             n_prompt_seq=n_prompt_seq, sample_seq_len=seq_s)
    return pl.pallas_call(
        kern,
        grid=(npt + nst,),
        in_specs=[
            pl.BlockSpec((TM, D_MODEL), prow),
            pl.BlockSpec((TM, D_MODEL), srow),
            _const_spec((1, D_MODEL)),
            _const_spec(w_t.shape),
            _const_spec((1, Q_LORA)),
            _const_spec((1, KV_LORA)),
            _const_spec((1, CONV_CH)),
            _const_spec((CONV_W, CONV_CH)),
            pl.BlockSpec((TM, ROPE_DIM), tab_idx),
            pl.BlockSpec((TM, ROPE_DIM), tab_idx),
            _const_spec(state.shape),
        ],
        out_specs=[
            pl.BlockSpec((TM, Q_LORA), row),
            pl.BlockSpec((TM, KV_LORA), prow),
            pl.BlockSpec((None, ROPE_DIM, TM), lambda i: (jnp.minimum(i, last_p) // tps, 0,
                                                          jnp.minimum(i, last_p) % tps)),
            pl.BlockSpec((TM, KV_LORA), srow),
            pl.BlockSpec((TM // seq_s, ROPE_DIM, seq_s), lambda i: (jnp.maximum(i - npt, 0), 0, 0)),
            pl.BlockSpec((TM, CONV_CH), row),
            pl.BlockSpec((TM // CHUNK, SUBLANES, CONV_CH), lambda i: (i, 0, 0)),
        ],
        out_shape=[
            jax.ShapeDtypeStruct((m, Q_LORA), BF16),
            jax.ShapeDtypeStruct((np_rows, KV_LORA), F32),
            jax.ShapeDtypeStruct((n_prompt_seq, ROPE_DIM, seq_p), F32),
            jax.ShapeDtypeStruct((ns_rows, KV_LORA), F32),
            jax.ShapeDtypeStruct((ns_rows // seq_s, ROPE_DIM, seq_s), F32),
            jax.ShapeDtypeStruct((m, CONV_CH), BF16),
            jax.ShapeDtypeStruct((m // CHUNK, SUBLANES, CONV_CH), F32),
        ],
        scratch_shapes=[pltpu.VMEM((TM + SUBLANES, CONV_CH), F32)],
        compiler_params=pltpu.CompilerParams(dimension_semantics=("arbitrary",),
                                             vmem_limit_bytes=VMEM_LIMIT),
        name="in_proj",
    )(xp, xs, gmix, w_t, gq, gkv, gco, convw, cosk, sink, state)


def _attn_kernel(*refs, tq, n_past, causal):
    refs = list(refs)
    cqn_ref, wq_ref, wuk_ref, wuv_ref, cos_ref, sin_ref, gao_ref = refs[:7]
    refs = refs[7:]
    if n_past:
        pkv_ref, pkr_ref = refs[:2]
        refs = refs[2:]
    kv_ref, kr_ref, out_ref, qlat_ref, qr_ref, m_ref, l_ref, acc_ref, s_ref, klim_ref = refs

    qi = pl.program_id(1)
    rows = N_HEADS * tq

    q = jnp.dot(cqn_ref[...], wq_ref[...], preferred_element_type=F32)
    nope_w = N_HEADS * QK_NOPE
    rope_w = N_HEADS * ROPE_DIM
    qrope = q[:, nope_w:nope_w + rope_w] * cos_ref[...] + q[:, nope_w + rope_w:] * sin_ref[...]
    for h in range(N_HEADS):
        qn = q[:, h * QK_NOPE:(h + 1) * QK_NOPE].astype(BF16)
        ql = jnp.dot(qn, wuk_ref[h], preferred_element_type=F32)
        qlat_ref[h * tq:(h + 1) * tq, :] = ql.astype(BF16)
        qr_ref[h * tq:(h + 1) * tq, :] = qrope[:, h * ROPE_DIM:(h + 1) * ROPE_DIM].astype(BF16)


    nt = (((1,), (1,)), ((), ()))

    def scores(kc_f32, krt_f32):
        s = lax.dot_general(qlat_ref[...], kc_f32.astype(BF16), nt, preferred_element_type=F32)
        return s + jnp.dot(qr_ref[...], krt_f32.astype(BF16), preferred_element_type=F32)

    def update(s, kc_f32, mask, first=False):
        if mask is not None:
            s = jnp.where(mask, s, NEG_BIG)
        m_cur = jnp.max(s, axis=-1, keepdims=True)
        if first:
            m_new = jnp.broadcast_to(m_cur, m_ref.shape)
        else:
            m_prev = m_ref[...]
            m_new = jnp.maximum(m_prev, m_cur)
            alpha = jnp.exp2((m_prev - m_new) * EXP2_SCALE)
        p = jnp.exp2((s - _lane_bcast(m_new, s.shape[1])) * EXP2_SCALE)
        l_cur = jnp.sum(p, axis=-1, keepdims=True)
        pv = jnp.dot(p.astype(BF16), kc_f32.astype(BF16), preferred_element_type=F32)
        if first:
            l_ref[...] = jnp.broadcast_to(l_cur, l_ref.shape)
            acc_ref[...] = pv
        else:
            l_ref[...] = alpha * l_ref[...] + l_cur
            acc_ref[...] = _lane_bcast(alpha, KV_LORA) * acc_ref[...] + pv
        m_ref[...] = m_new

    def pipelined(kv, kr, lo, hi, last, mask_fn):
        def body(j, c):
            k0 = pl.multiple_of(j * TK, TK)
            k1 = pl.multiple_of(jnp.minimum(j + 1, last) * TK, TK)
            s_cur = s_ref[j % 2]
            s_ref[(j + 1) % 2] = scores(kv[pl.ds(k1, TK), :], kr[:, pl.ds(k1, TK)])
            update(s_cur, kv[pl.ds(k0, TK), :], None if mask_fn is None else mask_fn(k0))
            return c
        lax.fori_loop(lo, hi, body, 0)

    def pipelined_pairs(kv, kr, n_pairs, last):
        def body(i, c):
            ka = pl.multiple_of((2 * i + 1) * TK, TK)
            kb = pl.multiple_of((2 * i + 2) * TK, TK)
            kc = pl.multiple_of(jnp.minimum(2 * i + 3, last) * TK, TK)
            s_ref[0] = scores(kv[pl.ds(kb, TK), :], kr[:, pl.ds(kb, TK)])
            update(s_ref[1], kv[pl.ds(ka, TK), :], None)
            s_ref[1] = scores(kv[pl.ds(kc, TK), :], kr[:, pl.ds(kc, TK)])
            update(s_ref[0], kv[pl.ds(kb, TK), :], None)
            return c
        lax.fori_loop(0, n_pairs, body, 0)

    def first_block(kv, kr, last, mask):
        k1 = pl.multiple_of(jnp.minimum(1, last) * TK, TK)
        s_ref[0] = scores(kv[pl.ds(0, TK), :], kr[:, pl.ds(0, TK)])
        s_ref[1] = scores(kv[pl.ds(k1, TK), :], kr[:, pl.ds(k1, TK)])
        update(s_ref[0], kv[pl.ds(0, TK), :], mask, first=True)

    if n_past:
        n_pb = n_past // TK
        first_block(pkv_ref, pkr_ref, n_pb - 1, None)
        n_pairs = (n_pb - 1) // 2
        pipelined_pairs(pkv_ref, pkr_ref, n_pairs, n_pb - 1)
        if 1 + 2 * n_pairs < n_pb:
            pipelined(pkv_ref, pkr_ref, 1 + 2 * n_pairs, n_pb, n_pb - 1, None)

    if causal:
        n_blocks = ((qi + 1) * tq + TK - 1) // TK
        n_full = jnp.minimum((qi * tq // CHUNK + 1) * CHUNK // TK, n_blocks)

        assert tq & (tq - 1) == 0 and CHUNK & (CHUNK - 1) == 0
        r = lax.broadcasted_iota(jnp.int32, (rows, LANES), 0)
        q_pos = qi * tq + (r & (tq - 1))
        klim_ref[...] = (q_pos & ~(CHUNK - 1)) + CHUNK

        def mask_fn(k0):
            cidx = lax.broadcasted_iota(jnp.int32, (rows, TK), 1)
            return cidx < _lane_bcast(klim_ref[...] - k0, TK)

        first_block(kv_ref, kr_ref, n_blocks - 1, mask_fn(0))
        n_pairs = jnp.maximum(n_full - 1, 0) // 2
        pipelined_pairs(kv_ref, kr_ref, n_pairs, n_blocks - 1)
        pipelined(kv_ref, kr_ref, 1 + 2 * n_pairs, n_full, n_blocks - 1, None)
        pipelined(kv_ref, kr_ref, jnp.maximum(n_full, 1), n_blocks, n_blocks - 1, mask_fn)
    else:
        update(scores(kv_ref[...], kr_ref[...]), kv_ref[...], None)

    o = acc_ref[...] / _lane_bcast(l_ref[...], KV_LORA)
    parts = []
    for h in range(N_HEADS):
        oh = o[h * tq:(h + 1) * tq, :].astype(BF16)
        parts.append(jnp.dot(oh, wuv_ref[h], preferred_element_type=F32))
    attn = jnp.concatenate(parts, axis=-1)
    out_ref[...] = _rms(attn, gao_ref[...]).astype(BF16)


def _attention(cqn, w_q, w_ukt, w_uv, cosq, sinq, gao, ckv, krope, *, n_batch, seq, row0,
               past_kv=None, past_kr=None):
    causal = past_kv is None
    tq = TQ if causal else seq
    nq = seq // tq
    n_past = 0 if causal else past_kv.shape[1]
    if not causal:
        assert n_past % CHUNK == 0 and seq <= CHUNK and n_past % TK == 0
    blk0 = row0 // tq
    qrow = lambda b, q: (blk0 + b * nq + q, 0)
    in_specs = [
        pl.BlockSpec((tq, Q_LORA), qrow),
        _const_spec(w_q.shape),
        _const_spec(w_ukt.shape),
        _const_spec(w_uv.shape),
        pl.BlockSpec((tq, N_HEADS * ROPE_DIM), lambda b, q: (q, 0)),
        pl.BlockSpec((tq, N_HEADS * ROPE_DIM), lambda b, q: (q, 0)),
        _const_spec((1, ATTN_W)),
    ]
    args = [cqn, w_q, w_ukt, w_uv, cosq, sinq, gao]
    if n_past:
        in_specs += [pl.BlockSpec((None, n_past, KV_LORA), lambda b, q: (b, 0, 0)),
                     pl.BlockSpec((None, ROPE_DIM, n_past), lambda b, q: (b, 0, 0))]
        args += [past_kv, past_kr]
    in_specs += [pl.BlockSpec((seq, KV_LORA), lambda b, q: (b, 0)),
                 pl.BlockSpec((None, ROPE_DIM, seq), lambda b, q: (b, 0, 0))]
    args += [ckv, krope]
    rows = N_HEADS * tq
    kern = functools.partial(_attn_kernel, tq=tq, n_past=n_past, causal=causal)
    return pl.pallas_call(
        kern,
        grid=(n_batch, nq),
        in_specs=in_specs,
        out_specs=pl.BlockSpec((tq, ATTN_W), lambda b, q: (b * nq + q, 0)),
        out_shape=jax.ShapeDtypeStruct((n_batch * seq, ATTN_W), BF16),
        scratch_shapes=[
            pltpu.VMEM((rows, KV_LORA), BF16),
            pltpu.VMEM((rows, ROPE_DIM), BF16),
            pltpu.VMEM((rows, LANES), F32),
            pltpu.VMEM((rows, LANES), F32),
            pltpu.VMEM((rows, KV_LORA), F32),
            pltpu.VMEM((2, rows, TK), F32),
            pltpu.VMEM((rows, LANES), jnp.int32),
        ],
        compiler_params=pltpu.CompilerParams(dimension_semantics=("arbitrary", "arbitrary"),
                                             vmem_limit_bytes=VMEM_LIMIT),
        name="attn_prompt" if causal else "attn_sample",
    )(*args)


def _out_proj_kernel(attnp_ref, attns_ref, convn_ref, xp_ref, xs_ref, wo_ref, gffn_ref, wr_ref,
                     br_ref, h_ref, xpk_ref, mi_ref, mf_ref, cnt_ref, carry_ref, logit_ref, *, n_prompt_tiles):
    i = pl.program_id(0)

    @pl.when(i == 0)
    def _():
        carry_ref[...] = jnp.zeros(carry_ref.shape, F32)
        logit_ref[...] = jnp.zeros(logit_ref.shape, F32)

    def tile(x_ref, attn_ref):
        prev_logits = logit_ref[...]
        y = jnp.dot(attn_ref[...], wo_ref[:ATTN_W, :], preferred_element_type=F32)
        y = y + jnp.dot(convn_ref[...], wo_ref[ATTN_W:, :], preferred_element_type=F32)
        h = x_ref[...] + y
        h_ref[...] = h
        xn = _rms(h, gffn_ref[...])

        half = D_MODEL // 2
        xh = xn.astype(BF16)
        xh32 = xh.astype(F32)
        lo = lax.bitcast_convert_type(xh32[:, :half], jnp.uint32)
        hi = lax.bitcast_convert_type(xh32[:, half:], jnp.uint32)
        xpk_ref[...] = (lo >> 16) | (hi & jnp.uint32(0xFFFF0000))

        xl = (xn - xh32).astype(BF16)
        hh_hl = jnp.dot(xh, wr_ref[...], preferred_element_type=F32)
        lh = jnp.dot(xl, wr_ref[:, :LANES], preferred_element_type=F32)
        logit_ref[...] = hh_hl[:, :LANES] + (lh + hh_hl[:, LANES:]) + br_ref[...]

        logits = prev_logits
        counted = (i > 0).astype(F32)
        lane = lax.broadcasted_iota(jnp.int32, (TM, LANES), 1).astype(F32)
        ninf = -jnp.inf
        far = float(LANES)

        def first_argmax(v):
            vmax = jnp.max(v, axis=-1, keepdims=True)
            return vmax, jnp.min(jnp.where(v == vmax, lane, far), axis=-1, keepdims=True)

        gl = jnp.where(lane < N_GROUPS, logits, ninf)
        gmax, gidx = first_argmax(gl)
        g_p = 1.0 / jnp.sum(jnp.exp(gl - gmax), axis=-1, keepdims=True)
        e_lo = N_GROUPS + EXPERTS_PER_GROUP * gidx
        el = jnp.where((lane >= e_lo) & (lane < e_lo + EXPERTS_PER_GROUP), logits, ninf)
        e1max, i1 = first_argmax(el)
        z = jnp.sum(jnp.exp(el - e1max), axis=-1, keepdims=True)
        el2 = jnp.where(lane == i1, ninf, el)
        e2max, i2 = first_argmax(el2)
        p1 = 1.0 / z
        p2 = jnp.exp(e2max - e1max) / z
        den = p1 + p2
        g0 = g_p * p1 / den
        g1 = g_p * p2 / den
        e0 = i1 - N_GROUPS
        e1 = i2 - N_GROUPS

        oh0 = lane == e0
        oh1 = lane == e1
        oh = jnp.where(oh0 | oh1, 1.0, 0.0)
        r = lax.broadcasted_iota(jnp.int32, (TM, TM), 0)
        c = lax.broadcasted_iota(jnp.int32, (TM, TM), 1)
        ltri = jnp.where(r > c, 1.0, 0.0).astype(BF16)
        before = jnp.dot(ltri, oh.astype(BF16), preferred_element_type=F32) + carry_ref[...]
        rank0 = jnp.sum(jnp.where(oh0, before, 0.0), axis=-1, keepdims=True)
        rank1 = jnp.sum(jnp.where(oh1, before, 0.0), axis=-1, keepdims=True)
        total = carry_ref[...] + counted * jnp.sum(oh, axis=0, keepdims=True)
        carry_ref[...] = total
        cnt_ref[...] = jnp.broadcast_to(total, cnt_ref.shape)

        mi = jnp.where(lane == 0, e0, jnp.where(lane == 1, e1, jnp.where(lane == 2, rank0, rank1)))
        mi_ref[...] = jnp.transpose(mi)[:SUBLANES, :].astype(jnp.int32)
        mf_ref[...] = jnp.where(lane == 0, g0, g1)

    @pl.when(i < n_prompt_tiles)
    def _():
        tile(xp_ref, attnp_ref)

    @pl.when(i >= n_prompt_tiles)
    def _():
        tile(xs_ref, attns_ref)


def _out_proj(attn_p, attn_s, conv_n, xp, xs, w_ob, gffn, w_r2, b_r):
    m = conv_n.shape[0]
    npt = xp.shape[0] // TM
    n_tiles = m // TM
    last_p, last_s, last = npt - 1, n_tiles - npt - 1, n_tiles - 1
    row = lambda i: (jnp.minimum(i, last), 0)
    prow = lambda i: (jnp.minimum(i, last_p), 0)
    srow = lambda i: (jnp.clip(i - npt, 0, last_s), 0)
    lag = lambda i: jnp.maximum(i - 1, 0)
    return pl.pallas_call(
        functools.partial(_out_proj_kernel, n_prompt_tiles=npt),
        grid=(n_tiles + 1,),
        in_specs=[
            pl.BlockSpec((TM, ATTN_W), prow),
            pl.BlockSpec((TM, ATTN_W), srow),
            pl.BlockSpec((TM, CONV_CH), row),
            pl.BlockSpec((TM, D_MODEL), prow),
            pl.BlockSpec((TM, D_MODEL), srow),
            _const_spec(w_ob.shape),
            _const_spec((1, D_MODEL)),
            _const_spec(w_r2.shape),
            _const_spec((1, LANES)),
        ],
        out_specs=[
            pl.BlockSpec((TM, D_MODEL), row),
            pl.BlockSpec((TM, D_MODEL // 2), row),
            pl.BlockSpec((SUBLANES, TM), lambda i: (0, lag(i))),
            pl.BlockSpec((TM, LANES), lambda i: (lag(i), 0)),
            pl.BlockSpec((SUBLANES, LANES), lambda i: (0, 0)),
        ],
        out_shape=[
            jax.ShapeDtypeStruct((m, D_MODEL), F32),
            jax.ShapeDtypeStruct((m, D_MODEL // 2), jnp.uint32),
            jax.ShapeDtypeStruct((SUBLANES, m), jnp.int32),
            jax.ShapeDtypeStruct((m, LANES), F32),
            jax.ShapeDtypeStruct((SUBLANES, LANES), F32),
        ],
        scratch_shapes=[pltpu.VMEM((1, LANES), F32), pltpu.VMEM((TM, LANES), F32)],
        compiler_params=pltpu.CompilerParams(dimension_semantics=("arbitrary",),
                                             vmem_limit_bytes=VMEM_LIMIT),
        name="out_proj",
    )(attn_p, attn_s, conv_n, xp, xs, w_ob, gffn, w_r2, b_r)


def _dispatch_kernel(d0_ref, d1_ref, zlo_ref, zn_ref, nu_ref, xpk_ref, xs_hbm, zeros_ref, sems, *, n_blocks):
    i = pl.program_id(0)
    sem = sems.at[0]
    zsem = sems.at[1]

    def zero_fill(act):
        def per_expert(e, c):
            lo = zlo_ref[e]
            n = zn_ref[e]
            head = (-lo) & (SUBLANES - 1)
            for r in range(SUBLANES - 1):
                @pl.when(r < head)
                def _(r=r):
                    act(pltpu.make_async_copy(zeros_ref.at[pl.ds(0, 1)], xs_hbm.at[pl.ds(lo + r, 1)], zsem))
            off = lo + head
            rest = n - head
            size = MOE_BLOCK // 2
            while size >= SUBLANES:
                @pl.when((rest & size) != 0)
                def _(off=off, size=size):
                    dst = xs_hbm.at[pl.ds(pl.multiple_of(off, SUBLANES), size)]
                    act(pltpu.make_async_copy(zeros_ref.at[pl.ds(0, size)], dst, zsem))
                off = off + (rest & size)
                size //= 2
            return c

        def per_block(b, c):
            dst = xs_hbm.at[pl.ds(pl.multiple_of(b * MOE_BLOCK, MOE_BLOCK), MOE_BLOCK)]
            act(pltpu.make_async_copy(zeros_ref, dst, zsem))
            return c

        lax.fori_loop(0, N_EXPERTS, per_expert, 0)
        lax.fori_loop(nu_ref[0], n_blocks, per_block, 0)

    @pl.when(i == 0)
    def _():
        zeros_ref[...] = jnp.zeros(zeros_ref.shape, zeros_ref.dtype)
        zero_fill(lambda cp: cp.start())

    @pl.when(i == pl.num_programs(0) - 1)
    def _():
        zero_fill(lambda cp: cp.wait())

    base = i * TM

    def start(g, c):
        for u in range(SUBLANES):
            r = base + g * SUBLANES + u
            src = xpk_ref.at[g, pl.ds(u, 1)]
            pltpu.make_async_copy(src, xs_hbm.at[pl.ds(d0_ref[r], 1)], sem).start()
            pltpu.make_async_copy(src, xs_hbm.at[pl.ds(d1_ref[r], 1)], sem).start()
        return c

    lax.fori_loop(0, TM // SUBLANES, start, 0)
    for _ in range(2):
        pltpu.make_async_copy(xs_hbm.at[pl.ds(0, TM)], xs_hbm.at[pl.ds(0, TM)], sem).wait()


def _dispatch(dest0, dest1, pad_lo, n_pad, n_used, xpk, n_blocks):
    m = xpk.shape[0]
    grid_spec = pltpu.PrefetchScalarGridSpec(
        num_scalar_prefetch=5,
        grid=(m // TM,),
        in_specs=[pl.BlockSpec((TM // SUBLANES, SUBLANES, D_MODEL // 2), lambda i, *_: (i, 0, 0))],
        out_specs=pl.BlockSpec(memory_space=pl.ANY),
        scratch_shapes=[pltpu.VMEM((MOE_BLOCK, D_MODEL // 2), jnp.uint32),
                        pltpu.SemaphoreType.DMA((2,))],
    )
    return pl.pallas_call(
        functools.partial(_dispatch_kernel, n_blocks=n_blocks),
        grid_spec=grid_spec,
        out_shape=jax.ShapeDtypeStruct((n_blocks * MOE_BLOCK, D_MODEL // 2), jnp.uint32),
        compiler_params=pltpu.CompilerParams(dimension_semantics=("arbitrary",)),
        name="dispatch",
    )(dest0, dest1, pad_lo, n_pad, n_used, xpk.reshape(m // SUBLANES, SUBLANES, D_MODEL // 2))


def _experts_kernel(be_ref, nu_ref, nxt_ref, x_ref, wg_hbm, wu_hbm, wd_hbm, y_ref,
                    sg_ref, su_ref, sd_ref, wgb_ref, wub_ref, wdb_ref, sems):
    b = pl.program_id(0)
    active = b < nu_ref[0]
    new_expert = jnp.logical_or(b == 0, be_ref[b] != be_ref[jnp.maximum(b - 1, 0)])

    def weight_copies(e):
        copies = []
        for k, (hbm, stage) in enumerate(((wg_hbm, sg_ref), (wu_hbm, su_ref), (wd_hbm, sd_ref))):
            rows_per = stage.shape[0] // WEIGHT_DMA_PARTS
            for part in range(WEIGHT_DMA_PARTS):
                rs = pl.ds(part * rows_per, rows_per)
                copies.append(pltpu.make_async_copy(hbm.at[e, rs], stage.at[rs], sems.at[k]))
        return copies

    @pl.when(b == 0)
    def _():
        for cp in weight_copies(be_ref[0]):
            cp.start()

    def block(first_of_expert):
        half = D_MODEL // 2
        if first_of_expert:
            for cp in weight_copies(be_ref[b]):
                cp.wait()
            refill = weight_copies(nxt_ref[b])

        def convert(k, stage_ref, dst_ref):
            if first_of_expert:
                dst_ref[...] = stage_ref[...].astype(BF16)
                for cp in refill[k * WEIGHT_DMA_PARTS:(k + 1) * WEIGHT_DMA_PARTS]:
                    cp.start()

        xa, xb = (v.astype(BF16) for v in _unpack_bf16_pairs(x_ref[...]))
        convert(0, sg_ref, wgb_ref)
        g = jnp.dot(xa, wgb_ref[:half, :], preferred_element_type=F32)
        g = g + jnp.dot(xb, wgb_ref[half:, :], preferred_element_type=F32)
        convert(1, su_ref, wub_ref)
        u = jnp.dot(xa, wub_ref[:half, :], preferred_element_type=F32)
        u = u + jnp.dot(xb, wub_ref[half:, :], preferred_element_type=F32)
        hmid = (g * jax.nn.sigmoid(g)) * u
        convert(2, sd_ref, wdb_ref)
        y = jnp.dot(hmid.astype(BF16), wdb_ref[...], preferred_element_type=F32)
        y_ref[...] = _pack_bf16_pairs(y)

    @pl.when(jnp.logical_and(active, new_expert))
    def _():
        block(True)

    @pl.when(jnp.logical_and(active, jnp.logical_not(new_expert)))
    def _():
        block(False)

    @pl.when(b >= nu_ref[0])
    def _():
        y_ref[...] = jnp.zeros(y_ref.shape, y_ref.dtype)

    @pl.when(b == pl.num_programs(0) - 1)
    def _():
        for cp in weight_copies(0):
            cp.wait()


def _experts(block_e, n_used, next_e, x_sorted, w_gate, w_up, w_down):
    p = x_sorted.shape[0]
    nb = p // MOE_BLOCK

    def xrow(b, be, nu, nxt):
        return (jnp.maximum(jnp.minimum(b, nu[0] - 1), 0), 0)

    grid_spec = pltpu.PrefetchScalarGridSpec(
        num_scalar_prefetch=3,
        grid=(nb,),
        in_specs=[
            pl.BlockSpec((MOE_BLOCK, D_MODEL // 2), xrow),
            pl.BlockSpec(memory_space=pl.ANY),
            pl.BlockSpec(memory_space=pl.ANY),
            pl.BlockSpec(memory_space=pl.ANY),
        ],
        out_specs=pl.BlockSpec((MOE_BLOCK, D_MODEL // 2), lambda b, be, nu, nxt: (b, 0)),
        scratch_shapes=[pltpu.VMEM((D_MODEL, D_FF), F32), pltpu.VMEM((D_MODEL, D_FF), F32),
                        pltpu.VMEM((D_FF, D_MODEL), F32),
                        pltpu.VMEM((D_MODEL, D_FF), BF16), pltpu.VMEM((D_MODEL, D_FF), BF16),
                        pltpu.VMEM((D_FF, D_MODEL), BF16),
                        pltpu.SemaphoreType.DMA((3,))],
    )
    return pl.pallas_call(
        _experts_kernel,
        grid_spec=grid_spec,
        out_shape=jax.ShapeDtypeStruct((p, D_MODEL // 2), jnp.uint32),
        compiler_params=pltpu.CompilerParams(dimension_semantics=("arbitrary",),
                                             vmem_limit_bytes=VMEM_LIMIT),
        name="experts",
    )(block_e, n_used, next_e, x_sorted, w_gate, w_up, w_down)


def _combine_kernel(d0_ref, d1_ref, h_ref, mf_ref, gfin_ref, y_hbm, outp_ref, outs_ref, y0_ref, y1_ref, sems,
                    *, n_tiles, n_prompt_tiles):
    i = pl.program_id(0)

    def gather(tile, slot, act):
        base = tile * TM

        def body(g, c):
            for u in range(SUBLANES):
                r = base + g * SUBLANES + u
                act(pltpu.make_async_copy(y_hbm.at[pl.ds(d0_ref[r], 1)], y0_ref.at[slot, g, pl.ds(u, 1)],
                                          sems.at[slot]))
                act(pltpu.make_async_copy(y_hbm.at[pl.ds(d1_ref[r], 1)], y1_ref.at[slot, g, pl.ds(u, 1)],
                                          sems.at[slot]))
            return c
        lax.fori_loop(0, TM // SUBLANES, body, 0)

    @pl.when(i == 0)
    def _():
        gather(0, 0, lambda cp: cp.start())

    @pl.when(i + 1 < n_tiles)
    def _():
        gather(i + 1, (i + 1) % 2, lambda cp: cp.start())

    slot = i % 2
    for _ in range(2):
        pltpu.make_async_copy(y_hbm.at[pl.ds(0, TM)], y_hbm.at[pl.ds(0, TM)], sems.at[slot]).wait()

    def finish(out_ref):
        mf = mf_ref[...]
        g0, g1 = mf[:, :, 0:1], mf[:, :, 1:2]
        half = D_MODEL // 2
        a0, b0 = _unpack_bf16_pairs(y0_ref[slot])
        a1, b1 = _unpack_bf16_pairs(y1_ref[slot])
        o_lo = h_ref[:, :, :half] + (g0 * a0 + g1 * a1)
        o_hi = h_ref[:, :, half:] + (g0 * b0 + g1 * b1)
        sumsq = jnp.sum(o_lo * o_lo, axis=-1, keepdims=True) + jnp.sum(o_hi * o_hi, axis=-1, keepdims=True)
        inv_rms = lax.rsqrt(sumsq / D_MODEL + EPS)
        out_ref[:, :, :half] = o_lo * inv_rms * gfin_ref[:, :, :half]
        out_ref[:, :, half:] = o_hi * inv_rms * gfin_ref[:, :, half:]

    @pl.when(i < n_prompt_tiles)
    def _():
        finish(outp_ref)

    @pl.when(i >= n_prompt_tiles)
    def _():
        finish(outs_ref)


def _combine(dest0, dest1, h, mf, gfin, y_sorted, *, n_prompt_rows):
    m = h.shape[0]
    npt = n_prompt_rows // TM
    tg = TM // SUBLANES
    grouped = lambda a: a.reshape(a.shape[0] // SUBLANES, SUBLANES, a.shape[1])
    grid_spec = pltpu.PrefetchScalarGridSpec(
        num_scalar_prefetch=2,
        grid=(m // TM,),
        in_specs=[
            pl.BlockSpec((tg, SUBLANES, D_MODEL), lambda i, *_: (i, 0, 0)),
            pl.BlockSpec((tg, SUBLANES, LANES), lambda i, *_: (i, 0, 0)),
            pl.BlockSpec((1, 1, D_MODEL), lambda i, *_: (0, 0, 0)),
            pl.BlockSpec(memory_space=pl.ANY),
        ],
        out_specs=[pl.BlockSpec((tg, SUBLANES, D_MODEL), lambda i, *_: (jnp.minimum(i, npt - 1), 0, 0)),
                   pl.BlockSpec((tg, SUBLANES, D_MODEL), lambda i, *_: (jnp.maximum(i - npt, 0), 0, 0))],
        scratch_shapes=[pltpu.VMEM((2, TM // SUBLANES, SUBLANES, D_MODEL // 2), jnp.uint32),
                        pltpu.VMEM((2, TM // SUBLANES, SUBLANES, D_MODEL // 2), jnp.uint32),
                        pltpu.SemaphoreType.DMA((2,))],
    )
    y_p, y_s = pl.pallas_call(
        functools.partial(_combine_kernel, n_tiles=m // TM, n_prompt_tiles=npt),
        grid_spec=grid_spec,
        out_shape=[jax.ShapeDtypeStruct((n_prompt_rows // SUBLANES, SUBLANES, D_MODEL), F32),
                   jax.ShapeDtypeStruct(((m - n_prompt_rows) // SUBLANES, SUBLANES, D_MODEL), F32)],
        compiler_params=pltpu.CompilerParams(dimension_semantics=("arbitrary",),
                                             vmem_limit_bytes=VMEM_LIMIT),
        name="combine",
    )(dest0, dest1, grouped(h), grouped(mf), gfin.reshape(1, 1, D_MODEL), y_sorted)
    return y_p.reshape(n_prompt_rows, D_MODEL), y_s.reshape(m - n_prompt_rows, D_MODEL)


def _rope_tables(pos):
    f32 = np.float32
    inv = np.power(f32(ROPE_THETA), -np.arange(0, ROPE_DIM, 2, dtype=f32) / f32(ROPE_DIM)).astype(f32)
    ang = (pos.astype(f32)[:, None] * inv[None, :]).astype(f32)
    cos, sin = np.cos(ang).astype(f32), np.sin(ang).astype(f32)
    return np.concatenate([cos, cos], axis=-1), np.concatenate([-sin, sin], axis=-1)


def _swap_halves(w):
    return jnp.concatenate([w[..., ROPE_DIM // 2:], w[..., :ROPE_DIM // 2]], axis=-1)


def kernel(x_prompt, x_sample, cache_kv_latent, cache_k_rope, state_conv, norm_mix, w_in, norm_q, w_uq,
           norm_kv, w_uk, w_uv, conv_w, norm_attn_out, norm_conv_out, w_o, norm_ffn, w_router_group,
           b_router_group, w_router_expert, b_router_expert, w_gate, w_up, w_down, norm_final):
    assert w_in.shape[0] == 1, "single-layer trunk"
    bp, seq_p, _ = x_prompt.shape
    bs, seq_s, _ = x_sample.shape
    past_len = cache_kv_latent.shape[2]
    np_rows, ns_rows = bp * seq_p, bs * seq_s
    m = np_rows + ns_rows
    assert seq_p % TM == 0 and TM % seq_s == 0 and ns_rows % TM == 0 and seq_s == CHUNK

    xp = x_prompt.reshape(np_rows, D_MODEL)
    xs = x_sample.reshape(ns_rows, D_MODEL)
    row_vec = lambda v: v.reshape(1, -1)

    assert w_in.shape[2] == Q_LORA + KV_LORA + ROPE_DIM + 3 * CONV_CH
    w_t = jnp.swapaxes(w_in[0], 0, 1).astype(BF16)
    wq4 = w_uq[0].reshape(Q_LORA, N_HEADS, QK_NOPE + ROPE_DIM)
    wq_rope = wq4[:, :, QK_NOPE:]
    w_q = jnp.concatenate([wq4[:, :, :QK_NOPE].reshape(Q_LORA, -1), wq_rope.reshape(Q_LORA, -1),
                           _swap_halves(wq_rope).reshape(Q_LORA, -1)], axis=1).astype(BF16)
    w_ukt = jnp.transpose(w_uk[0], (1, 2, 0)).astype(BF16)
    w_uvh = jnp.transpose(w_uv[0], (1, 0, 2)).astype(BF16)
    w_ob = w_o[0].astype(BF16)
    n_router = N_GROUPS + N_EXPERTS
    w_r = jnp.concatenate([w_router_group[0], w_router_expert[0].reshape(D_MODEL, N_EXPERTS)], axis=1)
    w_r = jnp.pad(w_r, ((0, 0), (0, LANES - n_router)))
    w_rh = w_r.astype(BF16)
    w_rl = (w_r - w_rh.astype(F32)).astype(BF16)
    w_r2 = jnp.concatenate([w_rh, w_rl], axis=1)
    b_r =jnp.pad(jnp.concatenate([b_router_group[0], b_router_expert[0].reshape(N_EXPERTS)]),
                  (0, LANES - n_router)).reshape(1, LANES)

    cos_p, sin_p = _rope_tables(np.arange(seq_p))
    cos_s, sin_s = _rope_tables(past_len + np.arange(seq_s))
    cosk = np.concatenate([cos_p, np.tile(cos_s, (TM // seq_s, 1))], axis=0)
    sink = np.concatenate([sin_p, np.tile(sin_s, (TM // seq_s, 1))], axis=0)
    state = jnp.concatenate([jnp.zeros((bp, CONV_W - 1, CONV_CH), F32), state_conv[0]], axis=0)

    cqn, ckv_p, kr_p, ckv_s, kr_s, conv_n, utail = _in_proj(
        xp, xs, row_vec(norm_mix[0]), w_t, row_vec(norm_q[0]), row_vec(norm_kv[0]),
        row_vec(norm_conv_out[0]), conv_w[0], cosk, sink, state, seq_p=seq_p, seq_s=seq_s)

    gao = row_vec(norm_attn_out[0])
    attn_p = _attention(cqn, w_q, w_ukt, w_uvh, np.tile(cos_p, (1, N_HEADS)), np.tile(sin_p, (1, N_HEADS)),
                        gao, ckv_p, kr_p, n_batch=bp, seq=seq_p, row0=0)
    attn_s = _attention(cqn, w_q, w_ukt, w_uvh, np.tile(cos_s, (1, N_HEADS)), np.tile(sin_s, (1, N_HEADS)),
                        gao, ckv_s, kr_s, n_batch=bs, seq=seq_s, row0=np_rows,
                        past_kv=cache_kv_latent[0], past_kr=jnp.swapaxes(cache_k_rope[0], 1, 2))

    h, xpk, mi, mf, cnt = _out_proj(attn_p, attn_s, conv_n, xp, xs, w_ob, row_vec(norm_ffn[0]),
                                    w_r2, b_r)

    counts = cnt[0, :N_EXPERTS].astype(jnp.int32)
    padded = (counts + MOE_BLOCK - 1) // MOE_BLOCK * MOE_BLOCK
    pad_end = jnp.cumsum(padded)
    pad_start = pad_end - padded
    n_blocks = -(-(m * 2) // MOE_BLOCK) + N_EXPERTS
    block_row0 = jnp.arange(n_blocks, dtype=jnp.int32) * MOE_BLOCK
    block_e = jnp.minimum(jnp.sum((pad_end[None, :] <= block_row0[:, None]).astype(jnp.int32), axis=1),
                          N_EXPERTS - 1)
    n_used = (pad_end[-1:] // MOE_BLOCK).astype(jnp.int32)
    expert_ids = jnp.arange(N_EXPERTS, dtype=jnp.int32)[:, None]

    def seg_start(e):
        return jnp.sum(jnp.where(expert_ids == e[None, :], pad_start[:, None], 0), axis=0)

    dest0 = seg_start(mi[0]) + mi[2]
    dest1 = seg_start(mi[1]) + mi[3]

    x_sorted = _dispatch(dest0, dest1, pad_start + counts, padded - counts, n_used, xpk, n_blocks)
    later = (expert_ids.T > block_e[:, None]) & (padded > 0)[None, :]
    next_e = jnp.min(jnp.where(later, expert_ids.T, N_EXPERTS), axis=1)
    next_e = jnp.where(next_e == N_EXPERTS, block_e, next_e).astype(jnp.int32)
    y_sorted = _experts(block_e, n_used, next_e, x_sorted, w_gate[0], w_up[0], w_down[0])
    gfin = row_vec(norm_final)
    y_p, y_s = _combine(dest0, dest1, h, mf, gfin, y_sorted, n_prompt_rows=np_rows)

    ut = utail.reshape(m // CHUNK, SUBLANES, CONV_CH)
    tails = ut[:, SUBLANES - (CONV_W - 1):, :]
    p_last = (jnp.arange(bp) + 1) * (seq_p // CHUNK) - 1
    s_last = np_rows // CHUNK + (jnp.arange(bs) + 1) * (seq_s // CHUNK) - 1
    return (y_p.reshape(bp, seq_p, D_MODEL),
            y_s.reshape(bs, seq_s, D_MODEL),
            ckv_p.reshape(1, bp, seq_p, KV_LORA),
            jnp.swapaxes(kr_p, 1, 2)[None],
            tails[p_last][None],
            ckv_s.reshape(1, bs, seq_s, KV_LORA),
            jnp.swapaxes(kr_s, 1, 2)[None],
            tails[s_last][None])
```

```python
import functools

import jax
import jax.numpy as jnp
import numpy as np
from jax import lax
from jax.experimental import pallas as pl
from jax.experimental.pallas import tpu as pltpu

F32 = jnp.float32
BF16 = jnp.bfloat16

D_MODEL = 2048
N_HEADS = 8
QK_NOPE = 128
ROPE_DIM = 64
V_DIM = 128
Q_LORA = 512
KV_LORA = 512
ATTN_W = N_HEADS * V_DIM
CONV_CH = D_MODEL - ATTN_W
CONV_W = 3
CHUNK = 64
N_GROUPS = 4
EXPERTS_PER_GROUP = 8
N_EXPERTS = N_GROUPS * EXPERTS_PER_GROUP
D_FF = 512
ROPE_THETA = 10000.0
EPS = 1e-6
ATTN_SCALE = (QK_NOPE + ROPE_DIM) ** -0.5
EXP2_SCALE = ATTN_SCALE * 1.4426950408889634

LANES = 128
SUBLANES = 8
TM = 256
MOE_BLOCK = 256
TQ = 256
TK = 256
NEG_BIG = -1e30
V7X_VMEM_BYTES = 64 * 1024 * 1024
VMEM_LIMIT = V7X_VMEM_BYTES * 7 // 8


def _rms(v, g):
    return v * lax.rsqrt(jnp.mean(v * v, axis=-1, keepdims=True) + EPS) * g


def _lane_bcast(v, width):
    if width % LANES == 0:
        return jnp.concatenate([v] * (width // LANES), axis=1)
    assert width < LANES
    return v[:, :width]


def _pack_bf16_pairs(v):
    half = v.shape[-1] // 2
    lo = lax.bitcast_convert_type(v[..., :half].astype(BF16).astype(F32), jnp.uint32)
    hi = lax.bitcast_convert_type(v[..., half:].astype(BF16).astype(F32), jnp.uint32)
    return (lo >> 16) | (hi & jnp.uint32(0xFFFF0000))


def _unpack_bf16_pairs(w):
    return (lax.bitcast_convert_type(w << 16, F32),
            lax.bitcast_convert_type(w & jnp.uint32(0xFFFF0000), F32))


def _const_spec(shape):
    nd = len(shape)
    return pl.BlockSpec(shape, lambda *_: (0,) * nd, pipeline_mode=pl.Buffered(1))


def _in_proj_kernel(xp_ref, xs_ref, gmix_ref, wt_ref, gq_ref, gkv_ref, gco_ref, convw_ref,
                    cos_ref, sin_ref, state_ref,
                    cqn_ref, ckvp_ref, krp_ref, ckvs_ref, krs_ref, convn_ref, utail_ref, ext_ref,
                    *, n_prompt_tiles, tiles_per_seq, n_prompt_seq, sample_seq_len):
    i = pl.program_id(0)

    @pl.when(i == 0)
    def _():
        ext_ref[...] = jnp.zeros(ext_ref.shape, F32)

    def conv_block(u_sub, gate_sub, row0, length):
        ext_ref[SUBLANES:SUBLANES + length, :] = u_sub
        um1 = ext_ref[SUBLANES - 1:SUBLANES - 1 + length, :]
        um2 = ext_ref[SUBLANES - 2:SUBLANES - 2 + length, :]
        cw = convw_ref[...]
        conv = cw[0:1] * um2 + cw[1:2] * um1 + cw[2:3] * u_sub
        convn_ref[row0:row0 + length, :] = _rms(gate_sub * conv, gco_ref[...]).astype(BF16)

    def tile(x_ref, is_prompt):
        ckv_ref, krt_ref = (ckvp_ref, krp_ref) if is_prompt else (ckvs_ref, krs_ref)
        x = x_ref[...]
        xg = (x * gmix_ref[...]).astype(BF16)
        inv_rms = lax.rsqrt(jnp.mean(x * x, axis=-1, keepdims=True) + EPS)
        lat_w = Q_LORA + KV_LORA
        conv0 = lat_w + ROPE_DIM
        nt = (((1,), (1,)), ((), ()))

        def project(lo, hi):
            return inv_rms * lax.dot_general(xg, wt_ref[lo:hi, :], nt, preferred_element_type=F32)

        z_ch = project(conv0 + CONV_CH, conv0 + 3 * CONV_CH)
        u = z_ch[:, :CONV_CH] * z_ch[:, CONV_CH:]
        for j in range(TM // CHUNK):
            utail_ref[j] = u[CHUNK * (j + 1) - SUBLANES:CHUNK * (j + 1), :]
        gate_b = project(conv0, conv0 + CONV_CH)

        if is_prompt:
            first = (i % tiles_per_seq) == 0
            carried = ext_ref[TM + SUBLANES - 2:TM + SUBLANES, :]
            ext_ref[SUBLANES - 2:SUBLANES, :] = jnp.where(first, state_ref[i // tiles_per_seq], carried)
            conv_block(u, gate_b, 0, TM)
        else:
            n_sub = TM // sample_seq_len
            seq0 = n_prompt_seq + (i - n_prompt_tiles) * n_sub
            for k in range(n_sub):
                ext_ref[SUBLANES - 2:SUBLANES, :] = state_ref[seq0 + k]
                lo = k * sample_seq_len
                conv_block(u[lo:lo + sample_seq_len], gate_b[lo:lo + sample_seq_len], lo, sample_seq_len)

        zk = project(lat_w, conv0)
        zk_swapped = jnp.concatenate([zk[:, ROPE_DIM // 2:], zk[:, :ROPE_DIM // 2]], axis=1)
        k_rope = zk * cos_ref[...] + zk_swapped * sin_ref[...]
        if is_prompt:
            krt_ref[...] = k_rope.T
        else:
            for k in range(TM // sample_seq_len):
                krt_ref[k] = k_rope[k * sample_seq_len:(k + 1) * sample_seq_len, :].T
        ckv_ref[...] = _rms(project(Q_LORA, lat_w), gkv_ref[...])
        cqn_ref[...] = _rms(project(0, Q_LORA), gq_ref[...]).astype(BF16)

    @pl.when(i < n_prompt_tiles)
    def _():
        tile(xp_ref, True)

    @pl.when(i >= n_prompt_tiles)
    def _():
        tile(xs_ref, False)


def _in_proj(xp, xs, gmix, w_t, gq, gkv, gco, convw, cosk, sink, state, *, seq_p, seq_s):
    np_rows, ns_rows = xp.shape[0], xs.shape[0]
    m = np_rows + ns_rows
    npt, nst = np_rows // TM, ns_rows // TM
    tps = seq_p // TM
    n_prompt_seq = np_rows // seq_p
    last_p = npt - 1

    def tab_idx(i):
        return (jnp.where(i < npt, i % tps, tps), 0)

    row = lambda i: (i, 0)
    prow = lambda i: (jnp.minimum(i, last_p), 0)
    srow = lambda i: (jnp.maximum(i - npt, 0), 0)
    kern = functools.partial(_in_proj_kernel, n_prompt_tiles=npt, tiles_per_seq=tps,
                             n_prompt_seq=n_prompt_seq, sample_seq_len=seq_s)
    return pl.pallas_call(
        kern,
        grid=(npt + nst,),
        in_specs=[
            pl.BlockSpec((TM, D_MODEL), prow),
            pl.BlockSpec((TM, D_MODEL), srow),
            _const_spec((1, D_MODEL)),
            _const_spec(w_t.shape),
            _const_spec((1, Q_LORA)),
            _const_spec((1, KV_LORA)),
            _const_spec((1, CONV_CH)),
            _const_spec((CONV_W, CONV_CH)),
            pl.BlockSpec((TM, ROPE_DIM), tab_idx),
            pl.BlockSpec((TM, ROPE_DIM), tab_idx),
            _const_spec(state.shape),
        ],
        out_specs=[
            pl.BlockSpec((TM, Q_LORA), row),
            pl.BlockSpec((TM, KV_LORA), prow),
            pl.BlockSpec((None, ROPE_DIM, TM), lambda i: (jnp.minimum(i, last_p) // tps, 0,
                                                          jnp.minimum(i, last_p) % tps)),
            pl.BlockSpec((TM, KV_LORA), srow),
            pl.BlockSpec((TM // seq_s, ROPE_DIM, seq_s), lambda i: (jnp.maximum(i - npt, 0), 0, 0)),
            pl.BlockSpec((TM, CONV_CH), row),
            pl.BlockSpec((TM // CHUNK, SUBLANES, CONV_CH), lambda i: (i, 0, 0)),
        ],
        out_shape=[
            jax.ShapeDtypeStruct((m, Q_LORA), BF16),
            jax.ShapeDtypeStruct((np_rows, KV_LORA), F32),
            jax.ShapeDtypeStruct((n_prompt_seq, ROPE_DIM, seq_p), F32),
            jax.ShapeDtypeStruct((ns_rows, KV_LORA), F32),
            jax.ShapeDtypeStruct((ns_rows // seq_s, ROPE_DIM, seq_s), F32),
            jax.ShapeDtypeStruct((m, CONV_CH), BF16),
            jax.ShapeDtypeStruct((m // CHUNK, SUBLANES, CONV_CH), F32),
        ],
        scratch_shapes=[pltpu.VMEM((TM + SUBLANES, CONV_CH), F32)],
        compiler_params=pltpu.CompilerParams(dimension_semantics=("arbitrary",),
                                             vmem_limit_bytes=VMEM_LIMIT),
        name="in_proj",
    )(xp, xs, gmix, w_t, gq, gkv, gco, convw, cosk, sink, state)


def _attn_kernel(*refs, tq, n_past, causal):
    refs = list(refs)
    cqn_ref, wq_ref, wuk_ref, wuv_ref, cos_ref, sin_ref, gao_ref = refs[:7]
    refs = refs[7:]
    if n_past:
        pkv_ref, pkr_ref = refs[:2]
        refs = refs[2:]
    kv_ref, kr_ref, out_ref, qlat_ref, qr_ref, m_ref, l_ref, acc_ref, s_ref, klim_ref = refs

    qi = pl.program_id(1)
    rows = N_HEADS * tq

    q = jnp.dot(cqn_ref[...], wq_ref[...], preferred_element_type=F32)
    nope_w = N_HEADS * QK_NOPE
    rope_w = N_HEADS * ROPE_DIM
    qrope = q[:, nope_w:nope_w + rope_w] * cos_ref[...] + q[:, nope_w + rope_w:] * sin_ref[...]
    for h in range(N_HEADS):
        qn = q[:, h * QK_NOPE:(h + 1) * QK_NOPE].astype(BF16)
        ql = jnp.dot(qn, wuk_ref[h], preferred_element_type=F32)
        qlat_ref[h * tq:(h + 1) * tq, :] = ql.astype(BF16)
        qr_ref[h * tq:(h + 1) * tq, :] = qrope[:, h * ROPE_DIM:(h + 1) * ROPE_DIM].astype(BF16)


    nt = (((1,), (1,)), ((), ()))

    def scores(kc_f32, krt_f32):
        s = lax.dot_general(qlat_ref[...], kc_f32.astype(BF16), nt, preferred_element_type=F32)
        return s + jnp.dot(qr_ref[...], krt_f32.astype(BF16), preferred_element_type=F32)

    def update(s, kc_f32, mask, first=False):
        if mask is not None:
            s = jnp.where(mask, s, NEG_BIG)
        m_cur = jnp.max(s, axis=-1, keepdims=True)
        if first:
            m_new = jnp.broadcast_to(m_cur, m_ref.shape)
        else:
            m_prev = m_ref[...]
            m_new = jnp.maximum(m_prev, m_cur)
            alpha = jnp.exp2((m_prev - m_new) * EXP2_SCALE)
        p = jnp.exp2((s - _lane_bcast(m_new, s.shape[1])) * EXP2_SCALE)
        l_cur = jnp.sum(p, axis=-1, keepdims=True)
        pv = jnp.dot(p.astype(BF16), kc_f32.astype(BF16), preferred_element_type=F32)
        if first:
            l_ref[...] = jnp.broadcast_to(l_cur, l_ref.shape)
            acc_ref[...] = pv
        else:
            l_ref[...] = alpha * l_ref[...] + l_cur
            acc_ref[...] = _lane_bcast(alpha, KV_LORA) * acc_ref[...] + pv
        m_ref[...] = m_new

    def pipelined(kv, kr, lo, hi, last, mask_fn):
        def body(j, c):
            k0 = pl.multiple_of(j * TK, TK)
            k1 = pl.multiple_of(jnp.minimum(j + 1, last) * TK, TK)
            s_cur = s_ref[j % 2]
            s_ref[(j + 1) % 2] = scores(kv[pl.ds(k1, TK), :], kr[:, pl.ds(k1, TK)])
            update(s_cur, kv[pl.ds(k0, TK), :], None if mask_fn is None else mask_fn(k0))
            return c
        lax.fori_loop(lo, hi, body, 0)

    def pipelined_pairs(kv, kr, n_pairs, last):
        def body(i, c):
            ka = pl.multiple_of((2 * i + 1) * TK, TK)
            kb = pl.multiple_of((2 * i + 2) * TK, TK)
            kc = pl.multiple_of(jnp.minimum(2 * i + 3, last) * TK, TK)
            s_ref[0] = scores(kv[pl.ds(kb, TK), :], kr[:, pl.ds(kb, TK)])
            update(s_ref[1], kv[pl.ds(ka, TK), :], None)
            s_ref[1] = scores(kv[pl.ds(kc, TK), :], kr[:, pl.ds(kc, TK)])
            update(s_ref[0], kv[pl.ds(kb, TK), :], None)
            return c
        lax.fori_loop(0, n_pairs, body, 0)

    def first_block(kv, kr, last, mask):
        k1 = pl.multiple_of(jnp.minimum(1, last) * TK, TK)
        s_ref[0] = scores(kv[pl.ds(0, TK), :], kr[:, pl.ds(0, TK)])
        s_ref[1] = scores(kv[pl.ds(k1, TK), :], kr[:, pl.ds(k1, TK)])
        update(s_ref[0], kv[pl.ds(0, TK), :], mask, first=True)

    if n_past:
        n_pb = n_past // TK
        first_block(pkv_ref, pkr_ref, n_pb - 1, None)
        n_pairs = (n_pb - 1) // 2
        pipelined_pairs(pkv_ref, pkr_ref, n_pairs, n_pb - 1)
        if 1 + 2 * n_pairs < n_pb:
            pipelined(pkv_ref, pkr_ref, 1 + 2 * n_pairs, n_pb, n_pb - 1, None)

    if causal:
        n_blocks = ((qi + 1) * tq + TK - 1) // TK
        n_full = jnp.minimum((qi * tq // CHUNK + 1) * CHUNK // TK, n_blocks)

        assert tq & (tq - 1) == 0 and CHUNK & (CHUNK - 1) == 0
        r = lax.broadcasted_iota(jnp.int32, (rows, LANES), 0)
        q_pos = qi * tq + (r & (tq - 1))
        klim_ref[...] = (q_pos & ~(CHUNK - 1)) + CHUNK

        def mask_fn(k0):
            cidx = lax.broadcasted_iota(jnp.int32, (rows, TK), 1)
            return cidx < _lane_bcast(klim_ref[...] - k0, TK)

        first_block(kv_ref, kr_ref, n_blocks - 1, mask_fn(0))
        n_pairs = jnp.maximum(n_full - 1, 0) // 2
        pipelined_pairs(kv_ref, kr_ref, n_pairs, n_blocks - 1)
        pipelined(kv_ref, kr_ref, 1 + 2 * n_pairs, n_full, n_blocks - 1, None)
        pipelined(kv_ref, kr_ref, jnp.maximum(n_full, 1), n_blocks, n_blocks - 1, mask_fn)
    else:
        update(scores(kv_ref[...], kr_ref[...]), kv_ref[...], None)

    o = acc_ref[...] / _lane_bcast(l_ref[...], KV_LORA)
    parts = []
    for h in range(N_HEADS):
        oh = o[h * tq:(h + 1) * tq, :].astype(BF16)
        parts.append(jnp.dot(oh, wuv_ref[h], preferred_element_type=F32))
    attn = jnp.concatenate(parts, axis=-1)
    out_ref[...] = _rms(attn, gao_ref[...]).astype(BF16)


def _attention(cqn, w_q, w_ukt, w_uv, cosq, sinq, gao, ckv, krope, *, n_batch, seq, row0,
               past_kv=None, past_kr=None):
    causal = past_kv is None
    tq = TQ if causal else seq
    nq = seq // tq
    n_past = 0 if causal else past_kv.shape[1]
    if not causal:
        assert n_past % CHUNK == 0 and seq <= CHUNK and n_past % TK == 0
    blk0 = row0 // tq
    qrow = lambda b, q: (blk0 + b * nq + q, 0)
    in_specs = [
        pl.BlockSpec((tq, Q_LORA), qrow),
        _const_spec(w_q.shape),
        _const_spec(w_ukt.shape),
        _const_spec(w_uv.shape),
        pl.BlockSpec((tq, N_HEADS * ROPE_DIM), lambda b, q: (q, 0)),
        pl.BlockSpec((tq, N_HEADS * ROPE_DIM), lambda b, q: (q, 0)),
        _const_spec((1, ATTN_W)),
    ]
    args = [cqn, w_q, w_ukt, w_uv, cosq, sinq, gao]
    if n_past:
        in_specs += [pl.BlockSpec((None, n_past, KV_LORA), lambda b, q: (b, 0, 0)),
                     pl.BlockSpec((None, ROPE_DIM, n_past), lambda b, q: (b, 0, 0))]
        args += [past_kv, past_kr]
    in_specs += [pl.BlockSpec((seq, KV_LORA), lambda b, q: (b, 0)),
                 pl.BlockSpec((None, ROPE_DIM, seq), lambda b, q: (b, 0, 0))]
    args += [ckv, krope]
    rows = N_HEADS * tq
    kern = functools.partial(_attn_kernel, tq=tq, n_past=n_past, causal=causal)
    return pl.pallas_call(
        kern,
        grid=(n_batch, nq),
        in_specs=in_specs,
        out_specs=pl.BlockSpec((tq, ATTN_W), lambda b, q: (b * nq + q, 0)),
        out_shape=jax.ShapeDtypeStruct((n_batch * seq, ATTN_W), BF16),
        scratch_shapes=[
            pltpu.VMEM((rows, KV_LORA), BF16),
            pltpu.VMEM((rows, ROPE_DIM), BF16),
            pltpu.VMEM((rows, LANES), F32),
            pltpu.VMEM((rows, LANES), F32),
            pltpu.VMEM((rows, KV_LORA), F32),
            pltpu.VMEM((2, rows, TK), F32),
            pltpu.VMEM((rows, LANES), jnp.int32),
        ],
        compiler_params=pltpu.CompilerParams(dimension_semantics=("arbitrary", "arbitrary"),
                                             vmem_limit_bytes=VMEM_LIMIT),
        name="attn_prompt" if causal else "attn_sample",
    )(*args)


def _out_proj_kernel(attnp_ref, attns_ref, convn_ref, xp_ref, xs_ref, wo_ref, gffn_ref, wr_ref,
                     br_ref, h_ref, xpk_ref, mi_ref, mf_ref, cnt_ref, carry_ref, logit_ref, *, n_prompt_tiles):
    i = pl.program_id(0)

    @pl.when(i == 0)
    def _():
        carry_ref[...] = jnp.zeros(carry_ref.shape, F32)
        logit_ref[...] = jnp.zeros(logit_ref.shape, F32)

    def tile(x_ref, attn_ref):
        prev_logits = logit_ref[...]
        y = jnp.dot(attn_ref[...], wo_ref[:ATTN_W, :], preferred_element_type=F32)
        y = y + jnp.dot(convn_ref[...], wo_ref[ATTN_W:, :], preferred_element_type=F32)
        h = x_ref[...] + y
        h_ref[...] = h
        xn = _rms(h, gffn_ref[...])

        half = D_MODEL // 2
        xh = xn.astype(BF16)
        xh32 = xh.astype(F32)
        lo = lax.bitcast_convert_type(xh32[:, :half], jnp.uint32)
        hi = lax.bitcast_convert_type(xh32[:, half:], jnp.uint32)
        xpk_ref[...] = (lo >> 16) | (hi & jnp.uint32(0xFFFF0000))

        xl = (xn - xh32).astype(BF16)
        hh_hl = jnp.dot(xh, wr_ref[...], preferred_element_type=F32)
        lh = jnp.dot(xl, wr_ref[:, :LANES], preferred_element_type=F32)
        logit_ref[...] = hh_hl[:, :LANES] + (lh + hh_hl[:, LANES:]) + br_ref[...]

        logits = prev_logits
        counted = (i > 0).astype(F32)
        lane = lax.broadcasted_iota(jnp.int32, (TM, LANES), 1).astype(F32)
        ninf = -jnp.inf
        far = float(LANES)

        def first_argmax(v):
            vmax = jnp.max(v, axis=-1, keepdims=True)
            return vmax, jnp.min(jnp.where(v == vmax, lane, far), axis=-1, keepdims=True)

        gl = jnp.where(lane < N_GROUPS, logits, ninf)
        gmax, gidx = first_argmax(gl)
        g_p = 1.0 / jnp.sum(jnp.exp(gl - gmax), axis=-1, keepdims=True)
        e_lo = N_GROUPS + EXPERTS_PER_GROUP * gidx
        el = jnp.where((lane >= e_lo) & (lane < e_lo + EXPERTS_PER_GROUP), logits, ninf)
        e1max, i1 = first_argmax(el)
        z = jnp.sum(jnp.exp(el - e1max), axis=-1, keepdims=True)
        el2 = jnp.where(lane == i1, ninf, el)
        e2max, i2 = first_argmax(el2)
        p1 = 1.0 / z
        p2 = jnp.exp(e2max - e1max) / z
        den = p1 + p2
        g0 = g_p * p1 / den
        g1 = g_p * p2 / den
        e0 = i1 - N_GROUPS
        e1 = i2 - N_GROUPS

        oh0 = lane == e0
        oh1 = lane == e1
        oh = jnp.where(oh0 | oh1, 1.0, 0.0)
        r = lax.broadcasted_iota(jnp.int32, (TM, TM), 0)
        c = lax.broadcasted_iota(jnp.int32, (TM, TM), 1)
        ltri = jnp.where(r > c, 1.0, 0.0).astype(BF16)
        before = jnp.dot(ltri, oh.astype(BF16), preferred_element_type=F32) + carry_ref[...]
        rank0 = jnp.sum(jnp.where(oh0, before, 0.0), axis=-1, keepdims=True)
        rank1 = jnp.sum(jnp.where(oh1, before, 0.0), axis=-1, keepdims=True)
        total = carry_ref[...] + counted * jnp.sum(oh, axis=0, keepdims=True)
        carry_ref[...] = total
        cnt_ref[...] = jnp.broadcast_to(total, cnt_ref.shape)

        mi = jnp.where(lane == 0, e0, jnp.where(lane == 1, e1, jnp.where(lane == 2, rank0, rank1)))
        mi_ref[...] = jnp.transpose(mi)[:SUBLANES, :].astype(jnp.int32)
        mf_ref[...] = jnp.where(lane == 0, g0, g1)

    @pl.when(i < n_prompt_tiles)
    def _():
        tile(xp_ref, attnp_ref)

    @pl.when(i >= n_prompt_tiles)
    def _():
        tile(xs_ref, attns_ref)


def _out_proj(attn_p, attn_s, conv_n, xp, xs, w_ob, gffn, w_r2, b_r):
    m = conv_n.shape[0]
    npt = xp.shape[0] // TM
    n_tiles = m // TM
    last_p, last_s, last = npt - 1, n_tiles - npt - 1, n_tiles - 1
    row = lambda i: (jnp.minimum(i, last), 0)
    prow = lambda i: (jnp.minimum(i, last_p), 0)
    srow = lambda i: (jnp.clip(i - npt, 0, last_s), 0)
    lag = lambda i: jnp.maximum(i - 1, 0)
    return pl.pallas_call(
        functools.partial(_out_proj_kernel, n_prompt_tiles=npt),
        grid=(n_tiles + 1,),
        in_specs=[
            pl.BlockSpec((TM, ATTN_W), prow),
            pl.BlockSpec((TM, ATTN_W), srow),
            pl.BlockSpec((TM, CONV_CH), row),
            pl.BlockSpec((TM, D_MODEL), prow),
            pl.BlockSpec((TM, D_MODEL), srow),
            _const_spec(w_ob.shape),
            _const_spec((1, D_MODEL)),
            _const_spec(w_r2.shape),
            _const_spec((1, LANES)),
        ],
        out_specs=[
            pl.BlockSpec((TM, D_MODEL), row),
            pl.BlockSpec((TM, D_MODEL // 2), row),
            pl.BlockSpec((SUBLANES, TM), lambda i: (0, lag(i))),
            pl.BlockSpec((TM, LANES), lambda i: (lag(i), 0)),
            pl.BlockSpec((SUBLANES, LANES), lambda i: (0, 0)),
        ],
        out_shape=[
            jax.ShapeDtypeStruct((m, D_MODEL), F32),
            jax.ShapeDtypeStruct((m, D_MODEL // 2), jnp.uint32),
            jax.ShapeDtypeStruct((SUBLANES, m), jnp.int32),
            jax.ShapeDtypeStruct((m, LANES), F32),
            jax.ShapeDtypeStruct((SUBLANES, LANES), F32),
        ],
        scratch_shapes=[pltpu.VMEM((1, LANES), F32), pltpu.VMEM((TM, LANES), F32)],
        compiler_params=pltpu.CompilerParams(dimension_semantics=("arbitrary",),
                                             vmem_limit_bytes=VMEM_LIMIT),
        name="out_proj",
    )(attn_p, attn_s, conv_n, xp, xs, w_ob, gffn, w_r2, b_r)


def _dispatch_kernel(d0_ref, d1_ref, zlo_ref, zn_ref, nu_ref, xpk_ref, xs_hbm, zeros_ref, sems, *, n_blocks):
    i = pl.program_id(0)
    sem = sems.at[0]
    zsem = sems.at[1]

    def zero_fill(act):
        def per_expert(e, c):
            lo = zlo_ref[e]
            n = zn_ref[e]
            head = (-lo) & (SUBLANES - 1)
            for r in range(SUBLANES - 1):
                @pl.when(r < head)
                def _(r=r):
                    act(pltpu.make_async_copy(zeros_ref.at[pl.ds(0, 1)], xs_hbm.at[pl.ds(lo + r, 1)], zsem))
            off = lo + head
            rest = n - head
            size = MOE_BLOCK // 2
            while size >= SUBLANES:
                @pl.when((rest & size) != 0)
                def _(off=off, size=size):
                    dst = xs_hbm.at[pl.ds(pl.multiple_of(off, SUBLANES), size)]
                    act(pltpu.make_async_copy(zeros_ref.at[pl.ds(0, size)], dst, zsem))
                off = off + (rest & size)
                size //= 2
            return c

        def per_block(b, c):
            dst = xs_hbm.at[pl.ds(pl.multiple_of(b * MOE_BLOCK, MOE_BLOCK), MOE_BLOCK)]
            act(pltpu.make_async_copy(zeros_ref, dst, zsem))
            return c

        lax.fori_loop(0, N_EXPERTS, per_expert, 0)
        lax.fori_loop(nu_ref[0], n_blocks, per_block, 0)

    @pl.when(i == 0)
    def _():
        zeros_ref[...] = jnp.zeros(zeros_ref.shape, zeros_ref.dtype)
        zero_fill(lambda cp: cp.start())

    @pl.when(i == pl.num_programs(0) - 1)
    def _():
        zero_fill(lambda cp: cp.wait())

    base = i * TM

    def start(g, c):
        for u in range(SUBLANES):
            r = base + g * SUBLANES + u
            src = xpk_ref.at[g, pl.ds(u, 1)]
            pltpu.make_async_copy(src, xs_hbm.at[pl.ds(d0_ref[r], 1)], sem).start()
            pltpu.make_async_copy(src, xs_hbm.at[pl.ds(d1_ref[r], 1)], sem).start()
        return c

    lax.fori_loop(0, TM // SUBLANES, start, 0)
    for _ in range(2):
        pltpu.make_async_copy(xs_hbm.at[pl.ds(0, TM)], xs_hbm.at[pl.ds(0, TM)], sem).wait()


def _dispatch(dest0, dest1, pad_lo, n_pad, n_used, xpk, n_blocks):
    m = xpk.shape[0]
    grid_spec = pltpu.PrefetchScalarGridSpec(
        num_scalar_prefetch=5,
        grid=(m // TM,),
        in_specs=[pl.BlockSpec((TM // SUBLANES, SUBLANES, D_MODEL // 2), lambda i, *_: (i, 0, 0))],
        out_specs=pl.BlockSpec(memory_space=pl.ANY),
        scratch_shapes=[pltpu.VMEM((MOE_BLOCK, D_MODEL // 2), jnp.uint32),
                        pltpu.SemaphoreType.DMA((2,))],
    )
    return pl.pallas_call(
        functools.partial(_dispatch_kernel, n_blocks=n_blocks),
        grid_spec=grid_spec,
        out_shape=jax.ShapeDtypeStruct((n_blocks * MOE_BLOCK, D_MODEL // 2), jnp.uint32),
        compiler_params=pltpu.CompilerParams(dimension_semantics=("arbitrary",)),
        name="dispatch",
    )(dest0, dest1, pad_lo, n_pad, n_used, xpk.reshape(m // SUBLANES, SUBLANES, D_MODEL // 2))


def _experts_kernel(be_ref, nu_ref, nxt_ref, x_ref, wg_hbm, wu_hbm, wd_hbm, y_ref,
                    sg_ref, su_ref, sd_ref, wgb_ref, wub_ref, wdb_ref, sems):
    b = pl.program_id(0)
    active = b < nu_ref[0]
    new_expert = jnp.logical_or(b == 0, be_ref[b] != be_ref[jnp.maximum(b - 1, 0)])

    def weight_copies(e):
        return (pltpu.make_async_copy(wg_hbm.at[e], sg_ref, sems.at[0]),
                pltpu.make_async_copy(wu_hbm.at[e], su_ref, sems.at[1]),
                pltpu.make_async_copy(wd_hbm.at[e], sd_ref, sems.at[2]))

    @pl.when(b == 0)
    def _():
        for cp in weight_copies(be_ref[0]):
            cp.start()

    @pl.when(jnp.logical_and(active, new_expert))
    def _():
        for cp in weight_copies(be_ref[b]):
            cp.wait()
        wgb_ref[...] = sg_ref[...].astype(BF16)
        wub_ref[...] = su_ref[...].astype(BF16)
        wdb_ref[...] = sd_ref[...].astype(BF16)

        @pl.when(nxt_ref[b] >= 0)
        def _():
            for cp in weight_copies(nxt_ref[b]):
                cp.start()

    @pl.when(active)
    def _():
        half = D_MODEL // 2
        xa, xb = (v.astype(BF16) for v in _unpack_bf16_pairs(x_ref[...]))
        g = jnp.dot(xa, wgb_ref[:half, :], preferred_element_type=F32)
        g = g + jnp.dot(xb, wgb_ref[half:, :], preferred_element_type=F32)
        u = jnp.dot(xa, wub_ref[:half, :], preferred_element_type=F32)
        u = u + jnp.dot(xb, wub_ref[half:, :], preferred_element_type=F32)
        hmid = (g * jax.nn.sigmoid(g)) * u
        y = jnp.dot(hmid.astype(BF16), wdb_ref[...], preferred_element_type=F32)
        y_ref[...] = _pack_bf16_pairs(y)

    @pl.when(b >= nu_ref[0])
    def _():
        y_ref[...] = jnp.zeros(y_ref.shape, y_ref.dtype)


def _experts(block_e, n_used, next_e, x_sorted, w_gate, w_up, w_down):
    p = x_sorted.shape[0]
    nb = p // MOE_BLOCK

    def xrow(b, be, nu, nxt):
        return (jnp.maximum(jnp.minimum(b, nu[0] - 1), 0), 0)

    grid_spec = pltpu.PrefetchScalarGridSpec(
        num_scalar_prefetch=3,
        grid=(nb,),
        in_specs=[
            pl.BlockSpec((MOE_BLOCK, D_MODEL // 2), xrow),
            pl.BlockSpec(memory_space=pl.ANY),
            pl.BlockSpec(memory_space=pl.ANY),
            pl.BlockSpec(memory_space=pl.ANY),
        ],
        out_specs=pl.BlockSpec((MOE_BLOCK, D_MODEL // 2), lambda b, be, nu, nxt: (b, 0)),
        scratch_shapes=[pltpu.VMEM((D_MODEL, D_FF), F32), pltpu.VMEM((D_MODEL, D_FF), F32),
                        pltpu.VMEM((D_FF, D_MODEL), F32),
                        pltpu.VMEM((D_MODEL, D_FF), BF16), pltpu.VMEM((D_MODEL, D_FF), BF16),
                        pltpu.VMEM((D_FF, D_MODEL), BF16),
                        pltpu.SemaphoreType.DMA((3,))],
    )
    return pl.pallas_call(
        _experts_kernel,
        grid_spec=grid_spec,
        out_shape=jax.ShapeDtypeStruct((p, D_MODEL // 2), jnp.uint32),
        compiler_params=pltpu.CompilerParams(dimension_semantics=("arbitrary",),
                                             vmem_limit_bytes=VMEM_LIMIT),
        name="experts",
    )(block_e, n_used, next_e, x_sorted, w_gate, w_up, w_down)


def _combine_kernel(d0_ref, d1_ref, h_ref, mf_ref, gfin_ref, y_hbm, outp_ref, outs_ref, y0_ref, y1_ref, sems,
                    *, n_tiles, n_prompt_tiles):
    i = pl.program_id(0)

    def gather(tile, slot, act):
        base = tile * TM

        def body(g, c):
            for u in range(SUBLANES):
                r = base + g * SUBLANES + u
                act(pltpu.make_async_copy(y_hbm.at[pl.ds(d0_ref[r], 1)], y0_ref.at[slot, g, pl.ds(u, 1)],
                                          sems.at[slot]))
                act(pltpu.make_async_copy(y_hbm.at[pl.ds(d1_ref[r], 1)], y1_ref.at[slot, g, pl.ds(u, 1)],
                                          sems.at[slot]))
            return c
        lax.fori_loop(0, TM // SUBLANES, body, 0)

    @pl.when(i == 0)
    def _():
        gather(0, 0, lambda cp: cp.start())

    @pl.when(i + 1 < n_tiles)
    def _():
        gather(i + 1, (i + 1) % 2, lambda cp: cp.start())

    slot = i % 2
    for _ in range(2):
        pltpu.make_async_copy(y_hbm.at[pl.ds(0, TM)], y_hbm.at[pl.ds(0, TM)], sems.at[slot]).wait()

    def finish(out_ref):
        mf = mf_ref[...]
        g0, g1 = mf[:, :, 0:1], mf[:, :, 1:2]
        half = D_MODEL // 2
        a0, b0 = _unpack_bf16_pairs(y0_ref[slot])
        a1, b1 = _unpack_bf16_pairs(y1_ref[slot])
        o_lo = h_ref[:, :, :half] + (g0 * a0 + g1 * a1)
        o_hi = h_ref[:, :, half:] + (g0 * b0 + g1 * b1)
        sumsq = jnp.sum(o_lo * o_lo, axis=-1, keepdims=True) + jnp.sum(o_hi * o_hi, axis=-1, keepdims=True)
        inv_rms = lax.rsqrt(sumsq / D_MODEL + EPS)
        out_ref[:, :, :half] = o_lo * inv_rms * gfin_ref[:, :, :half]
        out_ref[:, :, half:] = o_hi * inv_rms * gfin_ref[:, :, half:]

    @pl.when(i < n_prompt_tiles)
    def _():
        finish(outp_ref)

    @pl.when(i >= n_prompt_tiles)
    def _():
        finish(outs_ref)


def _combine(dest0, dest1, h, mf, gfin, y_sorted, *, n_prompt_rows):
    m = h.shape[0]
    npt = n_prompt_rows // TM
    tg = TM // SUBLANES
    grouped = lambda a: a.reshape(a.shape[0] // SUBLANES, SUBLANES, a.shape[1])
    grid_spec = pltpu.PrefetchScalarGridSpec(
        num_scalar_prefetch=2,
        grid=(m // TM,),
        in_specs=[
            pl.BlockSpec((tg, SUBLANES, D_MODEL), lambda i, *_: (i, 0, 0)),
            pl.BlockSpec((tg, SUBLANES, LANES), lambda i, *_: (i, 0, 0)),
            pl.BlockSpec((1, 1, D_MODEL), lambda i, *_: (0, 0, 0)),
            pl.BlockSpec(memory_space=pl.ANY),
        ],
        out_specs=[pl.BlockSpec((tg, SUBLANES, D_MODEL), lambda i, *_: (jnp.minimum(i, npt - 1), 0, 0)),
                   pl.BlockSpec((tg, SUBLANES, D_MODEL), lambda i, *_: (jnp.maximum(i - npt, 0), 0, 0))],
        scratch_shapes=[pltpu.VMEM((2, TM // SUBLANES, SUBLANES, D_MODEL // 2), jnp.uint32),
                        pltpu.VMEM((2, TM // SUBLANES, SUBLANES, D_MODEL // 2), jnp.uint32),
                        pltpu.SemaphoreType.DMA((2,))],
    )
    y_p, y_s = pl.pallas_call(
        functools.partial(_combine_kernel, n_tiles=m // TM, n_prompt_tiles=npt),
        grid_spec=grid_spec,
        out_shape=[jax.ShapeDtypeStruct((n_prompt_rows // SUBLANES, SUBLANES, D_MODEL), F32),
                   jax.ShapeDtypeStruct(((m - n_prompt_rows) // SUBLANES, SUBLANES, D_MODEL), F32)],
        compiler_params=pltpu.CompilerParams(dimension_semantics=("arbitrary",),
                                             vmem_limit_bytes=VMEM_LIMIT),
        name="combine",
    )(dest0, dest1, grouped(h), grouped(mf), gfin.reshape(1, 1, D_MODEL), y_sorted)
    return y_p.reshape(n_prompt_rows, D_MODEL), y_s.reshape(m - n_prompt_rows, D_MODEL)


def _rope_tables(pos):
    f32 = np.float32
    inv = np.power(f32(ROPE_THETA), -np.arange(0, ROPE_DIM, 2, dtype=f32) / f32(ROPE_DIM)).astype(f32)
    ang = (pos.astype(f32)[:, None] * inv[None, :]).astype(f32)
    cos, sin = np.cos(ang).astype(f32), np.sin(ang).astype(f32)
    return np.concatenate([cos, cos], axis=-1), np.concatenate([-sin, sin], axis=-1)


def _swap_halves(w):
    return jnp.concatenate([w[..., ROPE_DIM // 2:], w[..., :ROPE_DIM // 2]], axis=-1)


def kernel(x_prompt, x_sample, cache_kv_latent, cache_k_rope, state_conv, norm_mix, w_in, norm_q, w_uq,
           norm_kv, w_uk, w_uv, conv_w, norm_attn_out, norm_conv_out, w_o, norm_ffn, w_router_group,
           b_router_group, w_router_expert, b_router_expert, w_gate, w_up, w_down, norm_final):
    assert w_in.shape[0] == 1, "single-layer trunk"
    bp, seq_p, _ = x_prompt.shape
    bs, seq_s, _ = x_sample.shape
    past_len = cache_kv_latent.shape[2]
    np_rows, ns_rows = bp * seq_p, bs * seq_s
    m = np_rows + ns_rows
    assert seq_p % TM == 0 and TM % seq_s == 0 and ns_rows % TM == 0 and seq_s == CHUNK

    xp = x_prompt.reshape(np_rows, D_MODEL)
    xs = x_sample.reshape(ns_rows, D_MODEL)
    row_vec = lambda v: v.reshape(1, -1)

    assert w_in.shape[2] == Q_LORA + KV_LORA + ROPE_DIM + 3 * CONV_CH
    w_t = jnp.swapaxes(w_in[0], 0, 1).astype(BF16)
    wq4 = w_uq[0].reshape(Q_LORA, N_HEADS, QK_NOPE + ROPE_DIM)
    wq_rope = wq4[:, :, QK_NOPE:]
    w_q = jnp.concatenate([wq4[:, :, :QK_NOPE].reshape(Q_LORA, -1), wq_rope.reshape(Q_LORA, -1),
                           _swap_halves(wq_rope).reshape(Q_LORA, -1)], axis=1).astype(BF16)
    w_ukt = jnp.transpose(w_uk[0], (1, 2, 0)).astype(BF16)
    w_uvh = jnp.transpose(w_uv[0], (1, 0, 2)).astype(BF16)
    w_ob = w_o[0].astype(BF16)
    n_router = N_GROUPS + N_EXPERTS
    w_r = jnp.concatenate([w_router_group[0], w_router_expert[0].reshape(D_MODEL, N_EXPERTS)], axis=1)
    w_r = jnp.pad(w_r, ((0, 0), (0, LANES - n_router)))
    w_rh = w_r.astype(BF16)
    w_rl = (w_r - w_rh.astype(F32)).astype(BF16)
    w_r2 = jnp.concatenate([w_rh, w_rl], axis=1)
    b_r =jnp.pad(jnp.concatenate([b_router_group[0], b_router_expert[0].reshape(N_EXPERTS)]),
                  (0, LANES - n_router)).reshape(1, LANES)

    cos_p, sin_p = _rope_tables(np.arange(seq_p))
    cos_s, sin_s = _rope_tables(past_len + np.arange(seq_s))
    cosk = np.concatenate([cos_p, np.tile(cos_s, (TM // seq_s, 1))], axis=0)
    sink = np.concatenate([sin_p, np.tile(sin_s, (TM // seq_s, 1))], axis=0)
    state = jnp.concatenate([jnp.zeros((bp, CONV_W - 1, CONV_CH), F32), state_conv[0]], axis=0)

    cqn, ckv_p, kr_p, ckv_s, kr_s, conv_n, utail = _in_proj(
        xp, xs, row_vec(norm_mix[0]), w_t, row_vec(norm_q[0]), row_vec(norm_kv[0]),
        row_vec(norm_conv_out[0]), conv_w[0], cosk, sink, state, seq_p=seq_p, seq_s=seq_s)

    gao = row_vec(norm_attn_out[0])
    attn_p = _attention(cqn, w_q, w_ukt, w_uvh, np.tile(cos_p, (1, N_HEADS)), np.tile(sin_p, (1, N_HEADS)),
                        gao, ckv_p, kr_p, n_batch=bp, seq=seq_p, row0=0)
    attn_s = _attention(cqn, w_q, w_ukt, w_uvh, np.tile(cos_s, (1, N_HEADS)), np.tile(sin_s, (1, N_HEADS)),
                        gao, ckv_s, kr_s, n_batch=bs, seq=seq_s, row0=np_rows,
                        past_kv=cache_kv_latent[0], past_kr=jnp.swapaxes(cache_k_rope[0], 1, 2))

    h, xpk, mi, mf, cnt = _out_proj(attn_p, attn_s, conv_n, xp, xs, w_ob, row_vec(norm_ffn[0]),
                                    w_r2, b_r)

    counts = cnt[0, :N_EXPERTS].astype(jnp.int32)
    padded = (counts + MOE_BLOCK - 1) // MOE_BLOCK * MOE_BLOCK
    pad_end = jnp.cumsum(padded)
    pad_start = pad_end - padded
    n_blocks = -(-(m * 2) // MOE_BLOCK) + N_EXPERTS
    block_row0 = jnp.arange(n_blocks, dtype=jnp.int32) * MOE_BLOCK
    block_e = jnp.minimum(jnp.sum((pad_end[None, :] <= block_row0[:, None]).astype(jnp.int32), axis=1),
                          N_EXPERTS - 1)
    n_used = (pad_end[-1:] // MOE_BLOCK).astype(jnp.int32)
    expert_ids = jnp.arange(N_EXPERTS, dtype=jnp.int32)[:, None]

    def seg_start(e):
        return jnp.sum(jnp.where(expert_ids == e[None, :], pad_start[:, None], 0), axis=0)

    dest0 = seg_start(mi[0]) + mi[2]
    dest1 = seg_start(mi[1]) + mi[3]

    x_sorted = _dispatch(dest0, dest1, pad_start + counts, padded - counts, n_used, xpk, n_blocks)
    later = (expert_ids.T > block_e[:, None]) & (padded > 0)[None, :]
    next_e = jnp.min(jnp.where(later, expert_ids.T, N_EXPERTS), axis=1)
    next_e = jnp.where(next_e == N_EXPERTS, -1, next_e).astype(jnp.int32)
    y_sorted = _experts(block_e, n_used, next_e, x_sorted, w_gate[0], w_up[0], w_down[0])
    gfin = row_vec(norm_final)
    y_p, y_s = _combine(dest0, dest1, h, mf, gfin, y_sorted, n_prompt_rows=np_rows)

    ut = utail.reshape(m // CHUNK, SUBLANES, CONV_CH)
    tails = ut[:, SUBLANES - (CONV_W - 1):, :]
    p_last = (jnp.arange(bp) + 1) * (seq_p // CHUNK) - 1
    s_last = np_rows // CHUNK + (jnp.arange(bs) + 1) * (seq_s // CHUNK) - 1
    return (y_p.reshape(bp, seq_p, D_MODEL),
            y_s.reshape(bs, seq_s, D_MODEL),
            ckv_p.reshape(1, bp, seq_p, KV_LORA),
            jnp.swapaxes(kr_p, 1, 2)[None],
            tails[p_last][None],
            ckv_s.reshape(1, bs, seq_s, KV_LORA),
            jnp.swapaxes(kr_s, 1, 2)[None],
            tails[s_last][None])
```

```python
import functools

import jax
import jax.numpy as jnp
import numpy as np
from jax import lax
from jax.experimental import pallas as pl
from jax.experimental.pallas import tpu as pltpu

F32 = jnp.float32
BF16 = jnp.bfloat16

D_MODEL = 2048
N_HEADS = 8
QK_NOPE = 128
ROPE_DIM = 64
V_DIM = 128
Q_LORA = 512
KV_LORA = 512
ATTN_W = N_HEADS * V_DIM
CONV_CH = D_MODEL - ATTN_W
CONV_W = 3
CHUNK = 64
N_GROUPS = 4
EXPERTS_PER_GROUP = 8
N_EXPERTS = N_GROUPS * EXPERTS_PER_GROUP
D_FF = 512
ROPE_THETA = 10000.0
EPS = 1e-6
ATTN_SCALE = (QK_NOPE + ROPE_DIM) ** -0.5
EXP2_SCALE = ATTN_SCALE * 1.4426950408889634

LANES = 128
SUBLANES = 8
TM = 256
TD = 512
MOE_BLOCK = 256
TQ = 256
TK = 256
NEG_BIG = -1e30
V7X_VMEM_BYTES = 64 * 1024 * 1024
VMEM_LIMIT = V7X_VMEM_BYTES * 7 // 8


def _rms(v, g):
    return v * lax.rsqrt(jnp.mean(v * v, axis=-1, keepdims=True) + EPS) * g


def _lane_bcast(v, width):
    if width % LANES == 0:
        return jnp.concatenate([v] * (width // LANES), axis=1)
    assert width < LANES
    return v[:, :width]


def _pack_bf16_pairs(v):
    half = v.shape[-1] // 2
    lo = lax.bitcast_convert_type(v[..., :half].astype(BF16).astype(F32), jnp.uint32)
    hi = lax.bitcast_convert_type(v[..., half:].astype(BF16).astype(F32), jnp.uint32)
    return (lo >> 16) | (hi & jnp.uint32(0xFFFF0000))


def _unpack_bf16_pairs(w):
    return (lax.bitcast_convert_type(w << 16, F32),
            lax.bitcast_convert_type(w & jnp.uint32(0xFFFF0000), F32))


def _const_spec(shape):
    nd = len(shape)
    return pl.BlockSpec(shape, lambda *_: (0,) * nd, pipeline_mode=pl.Buffered(1))


def _in_proj_kernel(xp_ref, xs_ref, gmix_ref, wt_ref, gq_ref, gkv_ref, gco_ref, convw_ref,
                    cos_ref, sin_ref, state_ref,
                    cqn_ref, ckvp_ref, krp_ref, ckvs_ref, krs_ref, convn_ref, utail_ref, ext_ref,
                    *, n_prompt_tiles, tiles_per_seq, n_prompt_seq, sample_seq_len):
    i = pl.program_id(0)

    @pl.when(i == 0)
    def _():
        ext_ref[...] = jnp.zeros(ext_ref.shape, F32)

    def conv_block(u_sub, gate_sub, row0, length):
        ext_ref[SUBLANES:SUBLANES + length, :] = u_sub
        um1 = ext_ref[SUBLANES - 1:SUBLANES - 1 + length, :]
        um2 = ext_ref[SUBLANES - 2:SUBLANES - 2 + length, :]
        cw = convw_ref[...]
        conv = cw[0:1] * um2 + cw[1:2] * um1 + cw[2:3] * u_sub
        convn_ref[row0:row0 + length, :] = _rms(gate_sub * conv, gco_ref[...]).astype(BF16)

    def tile(x_ref, is_prompt):
        ckv_ref, krt_ref = (ckvp_ref, krp_ref) if is_prompt else (ckvs_ref, krs_ref)
        x = x_ref[...]
        xg = (x * gmix_ref[...]).astype(BF16)
        inv_rms = lax.rsqrt(jnp.mean(x * x, axis=-1, keepdims=True) + EPS)
        lat_w = Q_LORA + KV_LORA
        conv0 = lat_w + ROPE_DIM
        nt = (((1,), (1,)), ((), ()))

        def project(lo, hi):
            return inv_rms * lax.dot_general(xg, wt_ref[lo:hi, :], nt, preferred_element_type=F32)

        z_ch = project(conv0 + CONV_CH, conv0 + 3 * CONV_CH)
        u = z_ch[:, :CONV_CH] * z_ch[:, CONV_CH:]
        for j in range(TM // CHUNK):
            utail_ref[j] = u[CHUNK * (j + 1) - SUBLANES:CHUNK * (j + 1), :]
        gate_b = project(conv0, conv0 + CONV_CH)

        if is_prompt:
            first = (i % tiles_per_seq) == 0
            carried = ext_ref[TM + SUBLANES - 2:TM + SUBLANES, :]
            ext_ref[SUBLANES - 2:SUBLANES, :] = jnp.where(first, state_ref[i // tiles_per_seq], carried)
            conv_block(u, gate_b, 0, TM)
        else:
            n_sub = TM // sample_seq_len
            seq0 = n_prompt_seq + (i - n_prompt_tiles) * n_sub
            for k in range(n_sub):
                ext_ref[SUBLANES - 2:SUBLANES, :] = state_ref[seq0 + k]
                lo = k * sample_seq_len
                conv_block(u[lo:lo + sample_seq_len], gate_b[lo:lo + sample_seq_len], lo, sample_seq_len)

        zk = project(lat_w, conv0)
        zk_swapped = jnp.concatenate([zk[:, ROPE_DIM // 2:], zk[:, :ROPE_DIM // 2]], axis=1)
        k_rope = zk * cos_ref[...] + zk_swapped * sin_ref[...]
        if is_prompt:
            krt_ref[...] = k_rope.T
        else:
            for k in range(TM // sample_seq_len):
                krt_ref[k] = k_rope[k * sample_seq_len:(k + 1) * sample_seq_len, :].T
        ckv_ref[...] = _rms(project(Q_LORA, lat_w), gkv_ref[...])
        cqn_ref[...] = _rms(project(0, Q_LORA), gq_ref[...]).astype(BF16)

    @pl.when(i < n_prompt_tiles)
    def _():
        tile(xp_ref, True)

    @pl.when(i >= n_prompt_tiles)
    def _():
        tile(xs_ref, False)


def _in_proj(xp, xs, gmix, w_t, gq, gkv, gco, convw, cosk, sink, state, *, seq_p, seq_s):
    np_rows, ns_rows = xp.shape[0], xs.shape[0]
    m = np_rows + ns_rows
    npt, nst = np_rows // TM, ns_rows // TM
    tps = seq_p // TM
    n_prompt_seq = np_rows // seq_p
    last_p = npt - 1

    def tab_idx(i):
        return (jnp.where(i < npt, i % tps, tps), 0)

    row = lambda i: (i, 0)
    prow = lambda i: (jnp.minimum(i, last_p), 0)
    srow = lambda i: (jnp.maximum(i - npt, 0), 0)
    kern = functools.partial(_in_proj_kernel, n_prompt_tiles=npt, tiles_per_seq=tps,
                             n_prompt_seq=n_prompt_seq, sample_seq_len=seq_s)
    return pl.pallas_call(
        kern,
        grid=(npt + nst,),
        in_specs=[
            pl.BlockSpec((TM, D_MODEL), prow),
            pl.BlockSpec((TM, D_MODEL), srow),
            _const_spec((1, D_MODEL)),
            _const_spec(w_t.shape),
            _const_spec((1, Q_LORA)),
            _const_spec((1, KV_LORA)),
            _const_spec((1, CONV_CH)),
            _const_spec((CONV_W, CONV_CH)),
            pl.BlockSpec((TM, ROPE_DIM), tab_idx),
            pl.BlockSpec((TM, ROPE_DIM), tab_idx),
            _const_spec(state.shape),
        ],
        out_specs=[
            pl.BlockSpec((TM, Q_LORA), row),
            pl.BlockSpec((TM, KV_LORA), prow),
            pl.BlockSpec((None, ROPE_DIM, TM), lambda i: (jnp.minimum(i, last_p) // tps, 0,
                                                          jnp.minimum(i, last_p) % tps)),
            pl.BlockSpec((TM, KV_LORA), srow),
            pl.BlockSpec((TM // seq_s, ROPE_DIM, seq_s), lambda i: (jnp.maximum(i - npt, 0), 0, 0)),
            pl.BlockSpec((TM, CONV_CH), row),
            pl.BlockSpec((TM // CHUNK, SUBLANES, CONV_CH), lambda i: (i, 0, 0)),
        ],
        out_shape=[
            jax.ShapeDtypeStruct((m, Q_LORA), BF16),
            jax.ShapeDtypeStruct((np_rows, KV_LORA), F32),
            jax.ShapeDtypeStruct((n_prompt_seq, ROPE_DIM, seq_p), F32),
            jax.ShapeDtypeStruct((ns_rows, KV_LORA), F32),
            jax.ShapeDtypeStruct((ns_rows // seq_s, ROPE_DIM, seq_s), F32),
            jax.ShapeDtypeStruct((m, CONV_CH), BF16),
            jax.ShapeDtypeStruct((m // CHUNK, SUBLANES, CONV_CH), F32),
        ],
        scratch_shapes=[pltpu.VMEM((TM + SUBLANES, CONV_CH), F32)],
        compiler_params=pltpu.CompilerParams(dimension_semantics=("arbitrary",),
                                             vmem_limit_bytes=VMEM_LIMIT),
        name="in_proj",
    )(xp, xs, gmix, w_t, gq, gkv, gco, convw, cosk, sink, state)


def _attn_kernel(*refs, tq, n_past, causal):
    refs = list(refs)
    cqn_ref, wq_ref, wuk_ref, wuv_ref, cos_ref, sin_ref, gao_ref = refs[:7]
    refs = refs[7:]
    if n_past:
        pkv_ref, pkr_ref = refs[:2]
        refs = refs[2:]
    kv_ref, kr_ref, out_ref, qlat_ref, qr_ref, m_ref, l_ref, acc_ref, s_ref, klim_ref = refs

    qi = pl.program_id(1)
    rows = N_HEADS * tq

    q = jnp.dot(cqn_ref[...], wq_ref[...], preferred_element_type=F32)
    nope_w = N_HEADS * QK_NOPE
    rope_w = N_HEADS * ROPE_DIM
    qrope = q[:, nope_w:nope_w + rope_w] * cos_ref[...] + q[:, nope_w + rope_w:] * sin_ref[...]
    for h in range(N_HEADS):
        qn = q[:, h * QK_NOPE:(h + 1) * QK_NOPE].astype(BF16)
        ql = jnp.dot(qn, wuk_ref[h], preferred_element_type=F32)
        qlat_ref[h * tq:(h + 1) * tq, :] = ql.astype(BF16)
        qr_ref[h * tq:(h + 1) * tq, :] = qrope[:, h * ROPE_DIM:(h + 1) * ROPE_DIM].astype(BF16)


    nt = (((1,), (1,)), ((), ()))

    def scores(kc_f32, krt_f32):
        s = lax.dot_general(qlat_ref[...], kc_f32.astype(BF16), nt, preferred_element_type=F32)
        return s + jnp.dot(qr_ref[...], krt_f32.astype(BF16), preferred_element_type=F32)

    def update(s, kc_f32, mask, first=False):
        if mask is not None:
            s = jnp.where(mask, s, NEG_BIG)
        m_cur = jnp.max(s, axis=-1, keepdims=True)
        if first:
            m_new = jnp.broadcast_to(m_cur, m_ref.shape)
        else:
            m_prev = m_ref[...]
            m_new = jnp.maximum(m_prev, m_cur)
            alpha = jnp.exp2((m_prev - m_new) * EXP2_SCALE)
        p = jnp.exp2((s - _lane_bcast(m_new, s.shape[1])) * EXP2_SCALE)
        l_cur = jnp.sum(p, axis=-1, keepdims=True)
        pv = jnp.dot(p.astype(BF16), kc_f32.astype(BF16), preferred_element_type=F32)
        if first:
            l_ref[...] = jnp.broadcast_to(l_cur, l_ref.shape)
            acc_ref[...] = pv
        else:
            l_ref[...] = alpha * l_ref[...] + l_cur
            acc_ref[...] = _lane_bcast(alpha, KV_LORA) * acc_ref[...] + pv
        m_ref[...] = m_new

    def pipelined(kv, kr, lo, hi, last, mask_fn):
        def body(j, c):
            k0 = pl.multiple_of(j * TK, TK)
            k1 = pl.multiple_of(jnp.minimum(j + 1, last) * TK, TK)
            s_cur = s_ref[j % 2]
            s_ref[(j + 1) % 2] = scores(kv[pl.ds(k1, TK), :], kr[:, pl.ds(k1, TK)])
            update(s_cur, kv[pl.ds(k0, TK), :], None if mask_fn is None else mask_fn(k0))
            return c
        lax.fori_loop(lo, hi, body, 0)

    def pipelined_pairs(kv, kr, n_pairs, last):
        def body(i, c):
            ka = pl.multiple_of((2 * i + 1) * TK, TK)
            kb = pl.multiple_of((2 * i + 2) * TK, TK)
            kc = pl.multiple_of(jnp.minimum(2 * i + 3, last) * TK, TK)
            s_ref[0] = scores(kv[pl.ds(kb, TK), :], kr[:, pl.ds(kb, TK)])
            update(s_ref[1], kv[pl.ds(ka, TK), :], None)
            s_ref[1] = scores(kv[pl.ds(kc, TK), :], kr[:, pl.ds(kc, TK)])
            update(s_ref[0], kv[pl.ds(kb, TK), :], None)
            return c
        lax.fori_loop(0, n_pairs, body, 0)

    def first_block(kv, kr, last, mask):
        k1 = pl.multiple_of(jnp.minimum(1, last) * TK, TK)
        s_ref[0] = scores(kv[pl.ds(0, TK), :], kr[:, pl.ds(0, TK)])
        s_ref[1] = scores(kv[pl.ds(k1, TK), :], kr[:, pl.ds(k1, TK)])
        update(s_ref[0], kv[pl.ds(0, TK), :], mask, first=True)

    if n_past:
        n_pb = n_past // TK
        first_block(pkv_ref, pkr_ref, n_pb - 1, None)
        n_pairs = (n_pb - 1) // 2
        pipelined_pairs(pkv_ref, pkr_ref, n_pairs, n_pb - 1)
        if 1 + 2 * n_pairs < n_pb:
            pipelined(pkv_ref, pkr_ref, 1 + 2 * n_pairs, n_pb, n_pb - 1, None)

    if causal:
        n_blocks = ((qi + 1) * tq + TK - 1) // TK
        n_full = jnp.minimum((qi * tq // CHUNK + 1) * CHUNK // TK, n_blocks)

        assert tq & (tq - 1) == 0 and CHUNK & (CHUNK - 1) == 0
        r = lax.broadcasted_iota(jnp.int32, (rows, LANES), 0)
        q_pos = qi * tq + (r & (tq - 1))
        klim_ref[...] = (q_pos & ~(CHUNK - 1)) + CHUNK

        def mask_fn(k0):
            cidx = lax.broadcasted_iota(jnp.int32, (rows, TK), 1)
            return cidx < _lane_bcast(klim_ref[...] - k0, TK)

        first_block(kv_ref, kr_ref, n_blocks - 1, mask_fn(0))
        n_pairs = jnp.maximum(n_full - 1, 0) // 2
        pipelined_pairs(kv_ref, kr_ref, n_pairs, n_blocks - 1)
        pipelined(kv_ref, kr_ref, 1 + 2 * n_pairs, n_full, n_blocks - 1, None)
        pipelined(kv_ref, kr_ref, jnp.maximum(n_full, 1), n_blocks, n_blocks - 1, mask_fn)
    else:
        update(scores(kv_ref[...], kr_ref[...]), kv_ref[...], None)

    o = acc_ref[...] / _lane_bcast(l_ref[...], KV_LORA)
    parts = []
    for h in range(N_HEADS):
        oh = o[h * tq:(h + 1) * tq, :].astype(BF16)
        parts.append(jnp.dot(oh, wuv_ref[h], preferred_element_type=F32))
    attn = jnp.concatenate(parts, axis=-1)
    out_ref[...] = _rms(attn, gao_ref[...]).astype(BF16)


def _attention(cqn, w_q, w_ukt, w_uv, cosq, sinq, gao, ckv, krope, *, n_batch, seq, row0,
               past_kv=None, past_kr=None):
    causal = past_kv is None
    tq = TQ if causal else seq
    nq = seq // tq
    n_past = 0 if causal else past_kv.shape[1]
    if not causal:
        assert n_past % CHUNK == 0 and seq <= CHUNK and n_past % TK == 0
    blk0 = row0 // tq
    qrow = lambda b, q: (blk0 + b * nq + q, 0)
    in_specs = [
        pl.BlockSpec((tq, Q_LORA), qrow),
        _const_spec(w_q.shape),
        _const_spec(w_ukt.shape),
        _const_spec(w_uv.shape),
        pl.BlockSpec((tq, N_HEADS * ROPE_DIM), lambda b, q: (q, 0)),
        pl.BlockSpec((tq, N_HEADS * ROPE_DIM), lambda b, q: (q, 0)),
        _const_spec((1, ATTN_W)),
    ]
    args = [cqn, w_q, w_ukt, w_uv, cosq, sinq, gao]
    if n_past:
        in_specs += [pl.BlockSpec((None, n_past, KV_LORA), lambda b, q: (b, 0, 0)),
                     pl.BlockSpec((None, ROPE_DIM, n_past), lambda b, q: (b, 0, 0))]
        args += [past_kv, past_kr]
    in_specs += [pl.BlockSpec((seq, KV_LORA), lambda b, q: (b, 0)),
                 pl.BlockSpec((None, ROPE_DIM, seq), lambda b, q: (b, 0, 0))]
    args += [ckv, krope]
    rows = N_HEADS * tq
    kern = functools.partial(_attn_kernel, tq=tq, n_past=n_past, causal=causal)
    return pl.pallas_call(
        kern,
        grid=(n_batch, nq),
        in_specs=in_specs,
        out_specs=pl.BlockSpec((tq, ATTN_W), lambda b, q: (b * nq + q, 0)),
        out_shape=jax.ShapeDtypeStruct((n_batch * seq, ATTN_W), BF16),
        scratch_shapes=[
            pltpu.VMEM((rows, KV_LORA), BF16),
            pltpu.VMEM((rows, ROPE_DIM), BF16),
            pltpu.VMEM((rows, LANES), F32),
            pltpu.VMEM((rows, LANES), F32),
            pltpu.VMEM((rows, KV_LORA), F32),
            pltpu.VMEM((2, rows, TK), F32),
            pltpu.VMEM((rows, LANES), jnp.int32),
        ],
        compiler_params=pltpu.CompilerParams(dimension_semantics=("arbitrary", "arbitrary"),
                                             vmem_limit_bytes=VMEM_LIMIT),
        name="attn_prompt" if causal else "attn_sample",
    )(*args)


def _out_proj_kernel(attnp_ref, attns_ref, convn_ref, xp_ref, xs_ref, wo_ref, gffn_ref, wr_ref,
                     br_ref, h_ref, xpk_ref, mi_ref, mf_ref, cnt_ref, carry_ref, logit_ref, *, n_prompt_tiles):
    i = pl.program_id(0)

    @pl.when(i == 0)
    def _():
        carry_ref[...] = jnp.zeros(carry_ref.shape, F32)
        logit_ref[...] = jnp.zeros(logit_ref.shape, F32)

    def tile(x_ref, attn_ref):
        prev_logits = logit_ref[...]
        y = jnp.dot(attn_ref[...], wo_ref[:ATTN_W, :], preferred_element_type=F32)
        y = y + jnp.dot(convn_ref[...], wo_ref[ATTN_W:, :], preferred_element_type=F32)
        h = x_ref[...] + y
        h_ref[...] = h
        xn = _rms(h, gffn_ref[...])

        half = D_MODEL // 2
        xh = xn.astype(BF16)
        xh32 = xh.astype(F32)
        lo = lax.bitcast_convert_type(xh32[:, :half], jnp.uint32)
        hi = lax.bitcast_convert_type(xh32[:, half:], jnp.uint32)
        xpk_ref[...] = (lo >> 16) | (hi & jnp.uint32(0xFFFF0000))

        xl = (xn - xh32).astype(BF16)
        hh_hl = jnp.dot(xh, wr_ref[...], preferred_element_type=F32)
        lh = jnp.dot(xl, wr_ref[:, :LANES], preferred_element_type=F32)
        logit_ref[...] = hh_hl[:, :LANES] + (lh + hh_hl[:, LANES:]) + br_ref[...]

        logits = prev_logits
        counted = (i > 0).astype(F32)
        lane = lax.broadcasted_iota(jnp.int32, (TM, LANES), 1).astype(F32)
        ninf = -jnp.inf
        far = float(LANES)

        def first_argmax(v):
            vmax = jnp.max(v, axis=-1, keepdims=True)
            return vmax, jnp.min(jnp.where(v == vmax, lane, far), axis=-1, keepdims=True)

        gl = jnp.where(lane < N_GROUPS, logits, ninf)
        gmax, gidx = first_argmax(gl)
        g_p = 1.0 / jnp.sum(jnp.exp(gl - gmax), axis=-1, keepdims=True)
        e_lo = N_GROUPS + EXPERTS_PER_GROUP * gidx
        el = jnp.where((lane >= e_lo) & (lane < e_lo + EXPERTS_PER_GROUP), logits, ninf)
        e1max, i1 = first_argmax(el)
        z = jnp.sum(jnp.exp(el - e1max), axis=-1, keepdims=True)
        el2 = jnp.where(lane == i1, ninf, el)
        e2max, i2 = first_argmax(el2)
        p1 = 1.0 / z
        p2 = jnp.exp(e2max - e1max) / z
        den = p1 + p2
        g0 = g_p * p1 / den
        g1 = g_p * p2 / den
        e0 = i1 - N_GROUPS
        e1 = i2 - N_GROUPS

        oh0 = lane == e0
        oh1 = lane == e1
        oh = jnp.where(oh0 | oh1, 1.0, 0.0)
        r = lax.broadcasted_iota(jnp.int32, (TM, TM), 0)
        c = lax.broadcasted_iota(jnp.int32, (TM, TM), 1)
        ltri = jnp.where(r > c, 1.0, 0.0).astype(BF16)
        before = jnp.dot(ltri, oh.astype(BF16), preferred_element_type=F32) + carry_ref[...]
        rank0 = jnp.sum(jnp.where(oh0, before, 0.0), axis=-1, keepdims=True)
        rank1 = jnp.sum(jnp.where(oh1, before, 0.0), axis=-1, keepdims=True)
        total = carry_ref[...] + counted * jnp.sum(oh, axis=0, keepdims=True)
        carry_ref[...] = total
        cnt_ref[...] = jnp.broadcast_to(total, cnt_ref.shape)

        mi = jnp.where(lane == 0, e0, jnp.where(lane == 1, e1, jnp.where(lane == 2, rank0, rank1)))
        mi_ref[...] = jnp.transpose(mi)[:SUBLANES, :].astype(jnp.int32)
        mf_ref[...] = jnp.where(lane == 0, g0, g1)

    @pl.when(i < n_prompt_tiles)
    def _():
        tile(xp_ref, attnp_ref)

    @pl.when(i >= n_prompt_tiles)
    def _():
        tile(xs_ref, attns_ref)


def _out_proj(attn_p, attn_s, conv_n, xp, xs, w_ob, gffn, w_r2, b_r):
    m = conv_n.shape[0]
    npt = xp.shape[0] // TM
    n_tiles = m // TM
    last_p, last_s, last = npt - 1, n_tiles - npt - 1, n_tiles - 1
    row = lambda i: (jnp.minimum(i, last), 0)
    prow = lambda i: (jnp.minimum(i, last_p), 0)
    srow = lambda i: (jnp.clip(i - npt, 0, last_s), 0)
    lag = lambda i: jnp.maximum(i - 1, 0)
    return pl.pallas_call(
        functools.partial(_out_proj_kernel, n_prompt_tiles=npt),
        grid=(n_tiles + 1,),
        in_specs=[
            pl.BlockSpec((TM, ATTN_W), prow),
            pl.BlockSpec((TM, ATTN_W), srow),
            pl.BlockSpec((TM, CONV_CH), row),
            pl.BlockSpec((TM, D_MODEL), prow),
            pl.BlockSpec((TM, D_MODEL), srow),
            _const_spec(w_ob.shape),
            _const_spec((1, D_MODEL)),
            _const_spec(w_r2.shape),
            _const_spec((1, LANES)),
        ],
        out_specs=[
            pl.BlockSpec((TM, D_MODEL), row),
            pl.BlockSpec((TM, D_MODEL // 2), row),
            pl.BlockSpec((SUBLANES, TM), lambda i: (0, lag(i))),
            pl.BlockSpec((TM, LANES), lambda i: (lag(i), 0)),
            pl.BlockSpec((SUBLANES, LANES), lambda i: (0, 0)),
        ],
        out_shape=[
            jax.ShapeDtypeStruct((m, D_MODEL), F32),
            jax.ShapeDtypeStruct((m, D_MODEL // 2), jnp.uint32),
            jax.ShapeDtypeStruct((SUBLANES, m), jnp.int32),
            jax.ShapeDtypeStruct((m, LANES), F32),
            jax.ShapeDtypeStruct((SUBLANES, LANES), F32),
        ],
        scratch_shapes=[pltpu.VMEM((1, LANES), F32), pltpu.VMEM((TM, LANES), F32)],
        compiler_params=pltpu.CompilerParams(dimension_semantics=("arbitrary",),
                                             vmem_limit_bytes=VMEM_LIMIT),
        name="out_proj",
    )(attn_p, attn_s, conv_n, xp, xs, w_ob, gffn, w_r2, b_r)


def _dispatch_kernel(d0_ref, d1_ref, zlo_ref, zn_ref, nu_ref, xpk_ref, xs_hbm, zeros_ref, sems, *, n_blocks):
    i = pl.program_id(0)
    sem = sems.at[0]
    zsem = sems.at[1]

    def zero_fill(act):
        def per_expert(e, c):
            lo = zlo_ref[e]
            n = zn_ref[e]
            head = (-lo) & (SUBLANES - 1)
            for r in range(SUBLANES - 1):
                @pl.when(r < head)
                def _(r=r):
                    act(pltpu.make_async_copy(zeros_ref.at[pl.ds(0, 1)], xs_hbm.at[pl.ds(lo + r, 1)], zsem))
            off = lo + head
            rest = n - head
            size = MOE_BLOCK // 2
            while size >= SUBLANES:
                @pl.when((rest & size) != 0)
                def _(off=off, size=size):
                    dst = xs_hbm.at[pl.ds(pl.multiple_of(off, SUBLANES), size)]
                    act(pltpu.make_async_copy(zeros_ref.at[pl.ds(0, size)], dst, zsem))
                off = off + (rest & size)
                size //= 2
            return c

        def per_block(b, c):
            dst = xs_hbm.at[pl.ds(pl.multiple_of(b * MOE_BLOCK, MOE_BLOCK), MOE_BLOCK)]
            act(pltpu.make_async_copy(zeros_ref, dst, zsem))
            return c

        lax.fori_loop(0, N_EXPERTS, per_expert, 0)
        lax.fori_loop(nu_ref[0], n_blocks, per_block, 0)

    @pl.when(i == 0)
    def _():
        zeros_ref[...] = jnp.zeros(zeros_ref.shape, zeros_ref.dtype)
        zero_fill(lambda cp: cp.start())

    @pl.when(i == pl.num_programs(0) - 1)
    def _():
        zero_fill(lambda cp: cp.wait())

    base = i * TD

    def start(g, c):
        for u in range(SUBLANES):
            r = base + g * SUBLANES + u
            src = xpk_ref.at[g, pl.ds(u, 1)]
            pltpu.make_async_copy(src, xs_hbm.at[pl.ds(d0_ref[r], 1)], sem).start()
            pltpu.make_async_copy(src, xs_hbm.at[pl.ds(d1_ref[r], 1)], sem).start()
        return c

    lax.fori_loop(0, TD // SUBLANES, start, 0)
    for _ in range(2):
        pltpu.make_async_copy(xs_hbm.at[pl.ds(0, TD)], xs_hbm.at[pl.ds(0, TD)], sem).wait()


def _dispatch(dest0, dest1, pad_lo, n_pad, n_used, xpk, n_blocks):
    m = xpk.shape[0]
    grid_spec = pltpu.PrefetchScalarGridSpec(
        num_scalar_prefetch=5,
        grid=(m // TD,),
        in_specs=[pl.BlockSpec((TD // SUBLANES, SUBLANES, D_MODEL // 2), lambda i, *_: (i, 0, 0))],
        out_specs=pl.BlockSpec(memory_space=pl.ANY),
        scratch_shapes=[pltpu.VMEM((MOE_BLOCK, D_MODEL // 2), jnp.uint32),
                        pltpu.SemaphoreType.DMA((2,))],
    )
    return pl.pallas_call(
        functools.partial(_dispatch_kernel, n_blocks=n_blocks),
        grid_spec=grid_spec,
        out_shape=jax.ShapeDtypeStruct((n_blocks * MOE_BLOCK, D_MODEL // 2), jnp.uint32),
        compiler_params=pltpu.CompilerParams(dimension_semantics=("arbitrary",)),
        name="dispatch",
    )(dest0, dest1, pad_lo, n_pad, n_used, xpk.reshape(m // SUBLANES, SUBLANES, D_MODEL // 2))


def _experts_kernel(be_ref, nu_ref, nxt_ref, x_ref, wg_hbm, wu_hbm, wd_hbm, y_ref,
                    sg_ref, su_ref, sd_ref, wgb_ref, wub_ref, wdb_ref, sems):
    b = pl.program_id(0)
    active = b < nu_ref[0]
    new_expert = jnp.logical_or(b == 0, be_ref[b] != be_ref[jnp.maximum(b - 1, 0)])

    def weight_copies(e):
        return (pltpu.make_async_copy(wg_hbm.at[e], sg_ref, sems.at[0]),
                pltpu.make_async_copy(wu_hbm.at[e], su_ref, sems.at[1]),
                pltpu.make_async_copy(wd_hbm.at[e], sd_ref, sems.at[2]))

    @pl.when(b == 0)
    def _():
        for cp in weight_copies(be_ref[0]):
            cp.start()

    @pl.when(jnp.logical_and(active, new_expert))
    def _():
        for cp in weight_copies(be_ref[b]):
            cp.wait()
        wgb_ref[...] = sg_ref[...].astype(BF16)
        wub_ref[...] = su_ref[...].astype(BF16)
        wdb_ref[...] = sd_ref[...].astype(BF16)

        @pl.when(nxt_ref[b] >= 0)
        def _():
            for cp in weight_copies(nxt_ref[b]):
                cp.start()

    @pl.when(active)
    def _():
        half = D_MODEL // 2
        xa, xb = (v.astype(BF16) for v in _unpack_bf16_pairs(x_ref[...]))
        g = jnp.dot(xa, wgb_ref[:half, :], preferred_element_type=F32)
        g = g + jnp.dot(xb, wgb_ref[half:, :], preferred_element_type=F32)
        u = jnp.dot(xa, wub_ref[:half, :], preferred_element_type=F32)
        u = u + jnp.dot(xb, wub_ref[half:, :], preferred_element_type=F32)
        hmid = (g * jax.nn.sigmoid(g)) * u
        y = jnp.dot(hmid.astype(BF16), wdb_ref[...], preferred_element_type=F32)
        y_ref[...] = _pack_bf16_pairs(y)

    @pl.when(b >= nu_ref[0])
    def _():
        y_ref[...] = jnp.zeros(y_ref.shape, y_ref.dtype)


def _experts(block_e, n_used, next_e, x_sorted, w_gate, w_up, w_down):
    p = x_sorted.shape[0]
    nb = p // MOE_BLOCK

    def xrow(b, be, nu, nxt):
        return (jnp.maximum(jnp.minimum(b, nu[0] - 1), 0), 0)

    grid_spec = pltpu.PrefetchScalarGridSpec(
        num_scalar_prefetch=3,
        grid=(nb,),
        in_specs=[
            pl.BlockSpec((MOE_BLOCK, D_MODEL // 2), xrow),
            pl.BlockSpec(memory_space=pl.ANY),
            pl.BlockSpec(memory_space=pl.ANY),
            pl.BlockSpec(memory_space=pl.ANY),
        ],
        out_specs=pl.BlockSpec((MOE_BLOCK, D_MODEL // 2), lambda b, be, nu, nxt: (b, 0)),
        scratch_shapes=[pltpu.VMEM((D_MODEL, D_FF), F32), pltpu.VMEM((D_MODEL, D_FF), F32),
                        pltpu.VMEM((D_FF, D_MODEL), F32),
                        pltpu.VMEM((D_MODEL, D_FF), BF16), pltpu.VMEM((D_MODEL, D_FF), BF16),
                        pltpu.VMEM((D_FF, D_MODEL), BF16),
                        pltpu.SemaphoreType.DMA((3,))],
    )
    return pl.pallas_call(
        _experts_kernel,
        grid_spec=grid_spec,
        out_shape=jax.ShapeDtypeStruct((p, D_MODEL // 2), jnp.uint32),
        compiler_params=pltpu.CompilerParams(dimension_semantics=("arbitrary",),
                                             vmem_limit_bytes=VMEM_LIMIT),
        name="experts",
    )(block_e, n_used, next_e, x_sorted, w_gate, w_up, w_down)


def _combine_kernel(d0_ref, d1_ref, h_ref, mf_ref, gfin_ref, y_hbm, outp_ref, outs_ref, y0_ref, y1_ref, sems,
                    *, n_tiles, n_prompt_tiles):
    i = pl.program_id(0)

    def gather(tile, slot, act):
        base = tile * TD

        def body(g, c):
            for u in range(SUBLANES):
                r = base + g * SUBLANES + u
                act(pltpu.make_async_copy(y_hbm.at[pl.ds(d0_ref[r], 1)], y0_ref.at[slot, g, pl.ds(u, 1)],
                                          sems.at[slot]))
                act(pltpu.make_async_copy(y_hbm.at[pl.ds(d1_ref[r], 1)], y1_ref.at[slot, g, pl.ds(u, 1)],
                                          sems.at[slot]))
            return c
        lax.fori_loop(0, TD // SUBLANES, body, 0)

    @pl.when(i == 0)
    def _():
        gather(0, 0, lambda cp: cp.start())

    @pl.when(i + 1 < n_tiles)
    def _():
        gather(i + 1, (i + 1) % 2, lambda cp: cp.start())

    slot = i % 2
    for _ in range(2):
        pltpu.make_async_copy(y_hbm.at[pl.ds(0, TD)], y_hbm.at[pl.ds(0, TD)], sems.at[slot]).wait()

    def finish(out_ref):
        mf = mf_ref[...]
        g0, g1 = mf[:, :, 0:1], mf[:, :, 1:2]
        half = D_MODEL // 2
        a0, b0 = _unpack_bf16_pairs(y0_ref[slot])
        a1, b1 = _unpack_bf16_pairs(y1_ref[slot])
        o_lo = h_ref[:, :, :half] + (g0 * a0 + g1 * a1)
        o_hi = h_ref[:, :, half:] + (g0 * b0 + g1 * b1)
        sumsq = jnp.sum(o_lo * o_lo, axis=-1, keepdims=True) + jnp.sum(o_hi * o_hi, axis=-1, keepdims=True)
        inv_rms = lax.rsqrt(sumsq / D_MODEL + EPS)
        out_ref[:, :, :half] = o_lo * inv_rms * gfin_ref[:, :, :half]
        out_ref[:, :, half:] = o_hi * inv_rms * gfin_ref[:, :, half:]

    @pl.when(i < n_prompt_tiles)
    def _():
        finish(outp_ref)

    @pl.when(i >= n_prompt_tiles)
    def _():
        finish(outs_ref)


def _combine(dest0, dest1, h, mf, gfin, y_sorted, *, n_prompt_rows):
    m = h.shape[0]
    npt = n_prompt_rows // TD
    tg = TD // SUBLANES
    grouped = lambda a: a.reshape(a.shape[0] // SUBLANES, SUBLANES, a.shape[1])
    grid_spec = pltpu.PrefetchScalarGridSpec(
        num_scalar_prefetch=2,
        grid=(m // TD,),
        in_specs=[
            pl.BlockSpec((tg, SUBLANES, D_MODEL), lambda i, *_: (i, 0, 0)),
            pl.BlockSpec((tg, SUBLANES, LANES), lambda i, *_: (i, 0, 0)),
            pl.BlockSpec((1, 1, D_MODEL), lambda i, *_: (0, 0, 0)),
            pl.BlockSpec(memory_space=pl.ANY),
        ],
        out_specs=[pl.BlockSpec((tg, SUBLANES, D_MODEL), lambda i, *_: (jnp.minimum(i, npt - 1), 0, 0)),
                   pl.BlockSpec((tg, SUBLANES, D_MODEL), lambda i, *_: (jnp.maximum(i - npt, 0), 0, 0))],
        scratch_shapes=[pltpu.VMEM((2, tg, SUBLANES, D_MODEL // 2), jnp.uint32),
                        pltpu.VMEM((2, tg, SUBLANES, D_MODEL // 2), jnp.uint32),
                        pltpu.SemaphoreType.DMA((2,))],
    )
    y_p, y_s = pl.pallas_call(
        functools.partial(_combine_kernel, n_tiles=m // TD, n_prompt_tiles=npt),
        grid_spec=grid_spec,
        out_shape=[jax.ShapeDtypeStruct((n_prompt_rows // SUBLANES, SUBLANES, D_MODEL), F32),
                   jax.ShapeDtypeStruct(((m - n_prompt_rows) // SUBLANES, SUBLANES, D_MODEL), F32)],
        compiler_params=pltpu.CompilerParams(dimension_semantics=("arbitrary",),
                                             vmem_limit_bytes=VMEM_LIMIT),
        name="combine",
    )(dest0, dest1, grouped(h), grouped(mf), gfin.reshape(1, 1, D_MODEL), y_sorted)
    return y_p.reshape(n_prompt_rows, D_MODEL), y_s.reshape(m - n_prompt_rows, D_MODEL)


def _rope_tables(pos):
    f32 = np.float32
    inv = np.power(f32(ROPE_THETA), -np.arange(0, ROPE_DIM, 2, dtype=f32) / f32(ROPE_DIM)).astype(f32)
    ang = (pos.astype(f32)[:, None] * inv[None, :]).astype(f32)
    cos, sin = np.cos(ang).astype(f32), np.sin(ang).astype(f32)
    return np.concatenate([cos, cos], axis=-1), np.concatenate([-sin, sin], axis=-1)


def _swap_halves(w):
    return jnp.concatenate([w[..., ROPE_DIM // 2:], w[..., :ROPE_DIM // 2]], axis=-1)


def kernel(x_prompt, x_sample, cache_kv_latent, cache_k_rope, state_conv, norm_mix, w_in, norm_q, w_uq,
           norm_kv, w_uk, w_uv, conv_w, norm_attn_out, norm_conv_out, w_o, norm_ffn, w_router_group,
           b_router_group, w_router_expert, b_router_expert, w_gate, w_up, w_down, norm_final):
    assert w_in.shape[0] == 1, "single-layer trunk"
    bp, seq_p, _ = x_prompt.shape
    bs, seq_s, _ = x_sample.shape
    past_len = cache_kv_latent.shape[2]
    np_rows, ns_rows = bp * seq_p, bs * seq_s
    m = np_rows + ns_rows
    assert seq_p % TM == 0 and TM % seq_s == 0 and ns_rows % TM == 0 and seq_s == CHUNK
    assert np_rows % TD == 0 and ns_rows % TD == 0

    xp = x_prompt.reshape(np_rows, D_MODEL)
    xs = x_sample.reshape(ns_rows, D_MODEL)
    row_vec = lambda v: v.reshape(1, -1)

    assert w_in.shape[2] == Q_LORA + KV_LORA + ROPE_DIM + 3 * CONV_CH
    w_t = jnp.swapaxes(w_in[0], 0, 1).astype(BF16)
    wq4 = w_uq[0].reshape(Q_LORA, N_HEADS, QK_NOPE + ROPE_DIM)
    wq_rope = wq4[:, :, QK_NOPE:]
    w_q = jnp.concatenate([wq4[:, :, :QK_NOPE].reshape(Q_LORA, -1), wq_rope.reshape(Q_LORA, -1),
                           _swap_halves(wq_rope).reshape(Q_LORA, -1)], axis=1).astype(BF16)
    w_ukt = jnp.transpose(w_uk[0], (1, 2, 0)).astype(BF16)
    w_uvh = jnp.transpose(w_uv[0], (1, 0, 2)).astype(BF16)
    w_ob = w_o[0].astype(BF16)
    n_router = N_GROUPS + N_EXPERTS
    w_r = jnp.concatenate([w_router_group[0], w_router_expert[0].reshape(D_MODEL, N_EXPERTS)], axis=1)
    w_r = jnp.pad(w_r, ((0, 0), (0, LANES - n_router)))
    w_rh = w_r.astype(BF16)
    w_rl = (w_r - w_rh.astype(F32)).astype(BF16)
    w_r2 = jnp.concatenate([w_rh, w_rl], axis=1)
    b_r =jnp.pad(jnp.concatenate([b_router_group[0], b_router_expert[0].reshape(N_EXPERTS)]),
                  (0, LANES - n_router)).reshape(1, LANES)

    cos_p, sin_p = _rope_tables(np.arange(seq_p))
    cos_s, sin_s = _rope_tables(past_len + np.arange(seq_s))
    cosk = np.concatenate([cos_p, np.tile(cos_s, (TM // seq_s, 1))], axis=0)
    sink = np.concatenate([sin_p, np.tile(sin_s, (TM // seq_s, 1))], axis=0)
    state = jnp.concatenate([jnp.zeros((bp, CONV_W - 1, CONV_CH), F32), state_conv[0]], axis=0)

    cqn, ckv_p, kr_p, ckv_s, kr_s, conv_n, utail = _in_proj(
        xp, xs, row_vec(norm_mix[0]), w_t, row_vec(norm_q[0]), row_vec(norm_kv[0]),
        row_vec(norm_conv_out[0]), conv_w[0], cosk, sink, state, seq_p=seq_p, seq_s=seq_s)

    gao = row_vec(norm_attn_out[0])
    attn_p = _attention(cqn, w_q, w_ukt, w_uvh, np.tile(cos_p, (1, N_HEADS)), np.tile(sin_p, (1, N_HEADS)),
                        gao, ckv_p, kr_p, n_batch=bp, seq=seq_p, row0=0)
    attn_s = _attention(cqn, w_q, w_ukt, w_uvh, np.tile(cos_s, (1, N_HEADS)), np.tile(sin_s, (1, N_HEADS)),
                        gao, ckv_s, kr_s, n_batch=bs, seq=seq_s, row0=np_rows,
                        past_kv=cache_kv_latent[0], past_kr=jnp.swapaxes(cache_k_rope[0], 1, 2))

    h, xpk, mi, mf, cnt = _out_proj(attn_p, attn_s, conv_n, xp, xs, w_ob, row_vec(norm_ffn[0]),
                                    w_r2, b_r)

    counts = cnt[0, :N_EXPERTS].astype(jnp.int32)
    padded = (counts + MOE_BLOCK - 1) // MOE_BLOCK * MOE_BLOCK
    pad_end = jnp.cumsum(padded)
    pad_start = pad_end - padded
    n_blocks = -(-(m * 2) // MOE_BLOCK) + N_EXPERTS
    block_row0 = jnp.arange(n_blocks, dtype=jnp.int32) * MOE_BLOCK
    block_e = jnp.minimum(jnp.sum((pad_end[None, :] <= block_row0[:, None]).astype(jnp.int32), axis=1),
                          N_EXPERTS - 1)
    n_used = (pad_end[-1:] // MOE_BLOCK).astype(jnp.int32)
    expert_ids = jnp.arange(N_EXPERTS, dtype=jnp.int32)[:, None]

    def seg_start(e):
        return jnp.sum(jnp.where(expert_ids == e[None, :], pad_start[:, None], 0), axis=0)

    dest0 = seg_start(mi[0]) + mi[2]
    dest1 = seg_start(mi[1]) + mi[3]

    x_sorted = _dispatch(dest0, dest1, pad_start + counts, padded - counts, n_used, xpk, n_blocks)
    later = (expert_ids.T > block_e[:, None]) & (padded > 0)[None, :]
    next_e = jnp.min(jnp.where(later, expert_ids.T, N_EXPERTS), axis=1)
    next_e = jnp.where(next_e == N_EXPERTS, -1, next_e).astype(jnp.int32)
    y_sorted = _experts(block_e, n_used, next_e, x_sorted, w_gate[0], w_up[0], w_down[0])
    gfin = row_vec(norm_final)
    y_p, y_s = _combine(dest0, dest1, h, mf, gfin, y_sorted, n_prompt_rows=np_rows)

    ut = utail.reshape(m // CHUNK, SUBLANES, CONV_CH)
    tails = ut[:, SUBLANES - (CONV_W - 1):, :]
    p_last = (jnp.arange(bp) + 1) * (seq_p // CHUNK) - 1
    s_last = np_rows // CHUNK + (jnp.arange(bs) + 1) * (seq_s // CHUNK) - 1
    return (y_p.reshape(bp, seq_p, D_MODEL),
            y_s.reshape(bs, seq_s, D_MODEL),
            ckv_p.reshape(1, bp, seq_p, KV_LORA),
            jnp.swapaxes(kr_p, 1, 2)[None],
            tails[p_last][None],
            ckv_s.reshape(1, bs, seq_s, KV_LORA),
            jnp.swapaxes(kr_s, 1, 2)[None],
            tails[s_last][None])
```

```python
import functools

import jax
import jax.numpy as jnp
import numpy as np
from jax import lax
from jax.experimental import pallas as pl
from jax.experimental.pallas import tpu as pltpu

F32 = jnp.float32
BF16 = jnp.bfloat16

D_MODEL = 2048
N_HEADS = 8
QK_NOPE = 128
ROPE_DIM = 64
V_DIM = 128
Q_LORA = 512
KV_LORA = 512
ATTN_W = N_HEADS * V_DIM
CONV_CH = D_MODEL - ATTN_W
CONV_W = 3
CHUNK = 64
N_GROUPS = 4
EXPERTS_PER_GROUP = 8
N_EXPERTS = N_GROUPS * EXPERTS_PER_GROUP
D_FF = 512
ROPE_THETA = 10000.0
EPS = 1e-6
ATTN_SCALE = (QK_NOPE + ROPE_DIM) ** -0.5
EXP2_SCALE = ATTN_SCALE * 1.4426950408889634

LANES = 128
SUBLANES = 8
TM = 256
TD = 1024
MOE_BLOCK = 256
TQ = 256
TK = 256
NEG_BIG = -1e30
V7X_VMEM_BYTES = 64 * 1024 * 1024
VMEM_LIMIT = V7X_VMEM_BYTES * 7 // 8


def _rms(v, g):
    return v * lax.rsqrt(jnp.mean(v * v, axis=-1, keepdims=True) + EPS) * g


def _lane_bcast(v, width):
    if width % LANES == 0:
        return jnp.concatenate([v] * (width // LANES), axis=1)
    assert width < LANES
    return v[:, :width]


def _pack_bf16_pairs(v):
    half = v.shape[-1] // 2
    lo = lax.bitcast_convert_type(v[..., :half].astype(BF16).astype(F32), jnp.uint32)
    hi = lax.bitcast_convert_type(v[..., half:].astype(BF16).astype(F32), jnp.uint32)
    return (lo >> 16) | (hi & jnp.uint32(0xFFFF0000))


def _unpack_bf16_pairs(w):
    return (lax.bitcast_convert_type(w << 16, F32),
            lax.bitcast_convert_type(w & jnp.uint32(0xFFFF0000), F32))


def _const_spec(shape):
    nd = len(shape)
    return pl.BlockSpec(shape, lambda *_: (0,) * nd, pipeline_mode=pl.Buffered(1))


def _in_proj_kernel(xp_ref, xs_ref, gmix_ref, wt_ref, gq_ref, gkv_ref, gco_ref, convw_ref,
                    cos_ref, sin_ref, state_ref,
                    cqn_ref, ckvp_ref, krp_ref, ckvs_ref, krs_ref, convn_ref, utail_ref, ext_ref,
                    *, n_prompt_tiles, tiles_per_seq, n_prompt_seq, sample_seq_len):
    i = pl.program_id(0)

    @pl.when(i == 0)
    def _():
        ext_ref[...] = jnp.zeros(ext_ref.shape, F32)

    def conv_block(u_sub, gate_sub, row0, length):
        ext_ref[SUBLANES:SUBLANES + length, :] = u_sub
        um1 = ext_ref[SUBLANES - 1:SUBLANES - 1 + length, :]
        um2 = ext_ref[SUBLANES - 2:SUBLANES - 2 + length, :]
        cw = convw_ref[...]
        conv = cw[0:1] * um2 + cw[1:2] * um1 + cw[2:3] * u_sub
        convn_ref[row0:row0 + length, :] = _rms(gate_sub * conv, gco_ref[...]).astype(BF16)

    def tile(x_ref, is_prompt):
        ckv_ref, krt_ref = (ckvp_ref, krp_ref) if is_prompt else (ckvs_ref, krs_ref)
        x = x_ref[...]
        xg = (x * gmix_ref[...]).astype(BF16)
        inv_rms = lax.rsqrt(jnp.mean(x * x, axis=-1, keepdims=True) + EPS)
        lat_w = Q_LORA + KV_LORA
        conv0 = lat_w + ROPE_DIM
        nt = (((1,), (1,)), ((), ()))

        def project(lo, hi):
            return inv_rms * lax.dot_general(xg, wt_ref[lo:hi, :], nt, preferred_element_type=F32)

        z_ch = project(conv0 + CONV_CH, conv0 + 3 * CONV_CH)
        u = z_ch[:, :CONV_CH] * z_ch[:, CONV_CH:]
        for j in range(TM // CHUNK):
            utail_ref[j] = u[CHUNK * (j + 1) - SUBLANES:CHUNK * (j + 1), :]
        gate_b = project(conv0, conv0 + CONV_CH)

        if is_prompt:
            first = (i % tiles_per_seq) == 0
            carried = ext_ref[TM + SUBLANES - 2:TM + SUBLANES, :]
            ext_ref[SUBLANES - 2:SUBLANES, :] = jnp.where(first, state_ref[i // tiles_per_seq], carried)
            conv_block(u, gate_b, 0, TM)
        else:
            n_sub = TM // sample_seq_len
            seq0 = n_prompt_seq + (i - n_prompt_tiles) * n_sub
            for k in range(n_sub):
                ext_ref[SUBLANES - 2:SUBLANES, :] = state_ref[seq0 + k]
                lo = k * sample_seq_len
                conv_block(u[lo:lo + sample_seq_len], gate_b[lo:lo + sample_seq_len], lo, sample_seq_len)

        zk = project(lat_w, conv0)
        zk_swapped = jnp.concatenate([zk[:, ROPE_DIM // 2:], zk[:, :ROPE_DIM // 2]], axis=1)
        k_rope = zk * cos_ref[...] + zk_swapped * sin_ref[...]
        if is_prompt:
            krt_ref[...] = k_rope.T
        else:
            for k in range(TM // sample_seq_len):
                krt_ref[k] = k_rope[k * sample_seq_len:(k + 1) * sample_seq_len, :].T
        ckv_ref[...] = _rms(project(Q_LORA, lat_w), gkv_ref[...])
        cqn_ref[...] = _rms(project(0, Q_LORA), gq_ref[...]).astype(BF16)

    @pl.when(i < n_prompt_tiles)
    def _():
        tile(xp_ref, True)

    @pl.when(i >= n_prompt_tiles)
    def _():
        tile(xs_ref, False)


def _in_proj(xp, xs, gmix, w_t, gq, gkv, gco, convw, cosk, sink, state, *, seq_p, seq_s):
    np_rows, ns_rows = xp.shape[0], xs.shape[0]
    m = np_rows + ns_rows
    npt, nst = np_rows // TM, ns_rows // TM
    tps = seq_p // TM
    n_prompt_seq = np_rows // seq_p
    last_p = npt - 1

    def tab_idx(i):
        return (jnp.where(i < npt, i % tps, tps), 0)

    row = lambda i: (i, 0)
    prow = lambda i: (jnp.minimum(i, last_p), 0)
    srow = lambda i: (jnp.maximum(i - npt, 0), 0)
    kern = functools.partial(_in_proj_kernel, n_prompt_tiles=npt, tiles_per_seq=tps,
                             n_prompt_seq=n_prompt_seq, sample_seq_len=seq_s)
    return pl.pallas_call(
        kern,
        grid=(npt + nst,),
        in_specs=[
            pl.BlockSpec((TM, D_MODEL), prow),
            pl.BlockSpec((TM, D_MODEL), srow),
            _const_spec((1, D_MODEL)),
            _const_spec(w_t.shape),
            _const_spec((1, Q_LORA)),
            _const_spec((1, KV_LORA)),
            _const_spec((1, CONV_CH)),
            _const_spec((CONV_W, CONV_CH)),
            pl.BlockSpec((TM, ROPE_DIM), tab_idx),
            pl.BlockSpec((TM, ROPE_DIM), tab_idx),
            _const_spec(state.shape),
        ],
        out_specs=[
            pl.BlockSpec((TM, Q_LORA), row),
            pl.BlockSpec((TM, KV_LORA), prow),
            pl.BlockSpec((None, ROPE_DIM, TM), lambda i: (jnp.minimum(i, last_p) // tps, 0,
                                                          jnp.minimum(i, last_p) % tps)),
            pl.BlockSpec((TM, KV_LORA), srow),
            pl.BlockSpec((TM // seq_s, ROPE_DIM, seq_s), lambda i: (jnp.maximum(i - npt, 0), 0, 0)),
            pl.BlockSpec((TM, CONV_CH), row),
            pl.BlockSpec((TM // CHUNK, SUBLANES, CONV_CH), lambda i: (i, 0, 0)),
        ],
        out_shape=[
            jax.ShapeDtypeStruct((m, Q_LORA), BF16),
            jax.ShapeDtypeStruct((np_rows, KV_LORA), F32),
            jax.ShapeDtypeStruct((n_prompt_seq, ROPE_DIM, seq_p), F32),
            jax.ShapeDtypeStruct((ns_rows, KV_LORA), F32),
            jax.ShapeDtypeStruct((ns_rows // seq_s, ROPE_DIM, seq_s), F32),
            jax.ShapeDtypeStruct((m, CONV_CH), BF16),
            jax.ShapeDtypeStruct((m // CHUNK, SUBLANES, CONV_CH), F32),
        ],
        scratch_shapes=[pltpu.VMEM((TM + SUBLANES, CONV_CH), F32)],
        compiler_params=pltpu.CompilerParams(dimension_semantics=("arbitrary",),
                                             vmem_limit_bytes=VMEM_LIMIT),
        name="in_proj",
    )(xp, xs, gmix, w_t, gq, gkv, gco, convw, cosk, sink, state)


def _attn_kernel(*refs, tq, n_past, causal):
    refs = list(refs)
    cqn_ref, wq_ref, wuk_ref, wuv_ref, cos_ref, sin_ref, gao_ref = refs[:7]
    refs = refs[7:]
    if n_past:
        pkv_ref, pkr_ref = refs[:2]
        refs = refs[2:]
    kv_ref, kr_ref, out_ref, qlat_ref, qr_ref, m_ref, l_ref, acc_ref, s_ref, klim_ref = refs

    qi = pl.program_id(1)
    rows = N_HEADS * tq

    q = jnp.dot(cqn_ref[...], wq_ref[...], preferred_element_type=F32)
    nope_w = N_HEADS * QK_NOPE
    rope_w = N_HEADS * ROPE_DIM
    qrope = q[:, nope_w:nope_w + rope_w] * cos_ref[...] + q[:, nope_w + rope_w:] * sin_ref[...]
    for h in range(N_HEADS):
        qn = q[:, h * QK_NOPE:(h + 1) * QK_NOPE].astype(BF16)
        ql = jnp.dot(qn, wuk_ref[h], preferred_element_type=F32)
        qlat_ref[h * tq:(h + 1) * tq, :] = ql.astype(BF16)
        qr_ref[h * tq:(h + 1) * tq, :] = qrope[:, h * ROPE_DIM:(h + 1) * ROPE_DIM].astype(BF16)


    nt = (((1,), (1,)), ((), ()))

    def scores(kc_f32, krt_f32):
        s = lax.dot_general(qlat_ref[...], kc_f32.astype(BF16), nt, preferred_element_type=F32)
        return s + jnp.dot(qr_ref[...], krt_f32.astype(BF16), preferred_element_type=F32)

    def update(s, kc_f32, mask, first=False):
        if mask is not None:
            s = jnp.where(mask, s, NEG_BIG)
        m_cur = jnp.max(s, axis=-1, keepdims=True)
        if first:
            m_new = jnp.broadcast_to(m_cur, m_ref.shape)
        else:
            m_prev = m_ref[...]
            m_new = jnp.maximum(m_prev, m_cur)
            alpha = jnp.exp2((m_prev - m_new) * EXP2_SCALE)
        p = jnp.exp2((s - _lane_bcast(m_new, s.shape[1])) * EXP2_SCALE)
        l_cur = jnp.sum(p, axis=-1, keepdims=True)
        pv = jnp.dot(p.astype(BF16), kc_f32.astype(BF16), preferred_element_type=F32)
        if first:
            l_ref[...] = jnp.broadcast_to(l_cur, l_ref.shape)
            acc_ref[...] = pv
        else:
            l_ref[...] = alpha * l_ref[...] + l_cur
            acc_ref[...] = _lane_bcast(alpha, KV_LORA) * acc_ref[...] + pv
        m_ref[...] = m_new

    def pipelined(kv, kr, lo, hi, last, mask_fn):
        def body(j, c):
            k0 = pl.multiple_of(j * TK, TK)
            k1 = pl.multiple_of(jnp.minimum(j + 1, last) * TK, TK)
            s_cur = s_ref[j % 2]
            s_ref[(j + 1) % 2] = scores(kv[pl.ds(k1, TK), :], kr[:, pl.ds(k1, TK)])
            update(s_cur, kv[pl.ds(k0, TK), :], None if mask_fn is None else mask_fn(k0))
            return c
        lax.fori_loop(lo, hi, body, 0)

    def pipelined_pairs(kv, kr, n_pairs, last):
        def body(i, c):
            ka = pl.multiple_of((2 * i + 1) * TK, TK)
            kb = pl.multiple_of((2 * i + 2) * TK, TK)
            kc = pl.multiple_of(jnp.minimum(2 * i + 3, last) * TK, TK)
            s_ref[0] = scores(kv[pl.ds(kb, TK), :], kr[:, pl.ds(kb, TK)])
            update(s_ref[1], kv[pl.ds(ka, TK), :], None)
            s_ref[1] = scores(kv[pl.ds(kc, TK), :], kr[:, pl.ds(kc, TK)])
            update(s_ref[0], kv[pl.ds(kb, TK), :], None)
            return c
        lax.fori_loop(0, n_pairs, body, 0)

    def first_block(kv, kr, last, mask):
        k1 = pl.multiple_of(jnp.minimum(1, last) * TK, TK)
        s_ref[0] = scores(kv[pl.ds(0, TK), :], kr[:, pl.ds(0, TK)])
        s_ref[1] = scores(kv[pl.ds(k1, TK), :], kr[:, pl.ds(k1, TK)])
        update(s_ref[0], kv[pl.ds(0, TK), :], mask, first=True)

    if n_past:
        n_pb = n_past // TK
        first_block(pkv_ref, pkr_ref, n_pb - 1, None)
        n_pairs = (n_pb - 1) // 2
        pipelined_pairs(pkv_ref, pkr_ref, n_pairs, n_pb - 1)
        if 1 + 2 * n_pairs < n_pb:
            pipelined(pkv_ref, pkr_ref, 1 + 2 * n_pairs, n_pb, n_pb - 1, None)

    if causal:
        n_blocks = ((qi + 1) * tq + TK - 1) // TK
        n_full = jnp.minimum((qi * tq // CHUNK + 1) * CHUNK // TK, n_blocks)

        assert tq & (tq - 1) == 0 and CHUNK & (CHUNK - 1) == 0
        r = lax.broadcasted_iota(jnp.int32, (rows, LANES), 0)
        q_pos = qi * tq + (r & (tq - 1))
        klim_ref[...] = (q_pos & ~(CHUNK - 1)) + CHUNK

        def mask_fn(k0):
            cidx = lax.broadcasted_iota(jnp.int32, (rows, TK), 1)
            return cidx < _lane_bcast(klim_ref[...] - k0, TK)

        first_block(kv_ref, kr_ref, n_blocks - 1, mask_fn(0))
        n_pairs = jnp.maximum(n_full - 1, 0) // 2
        pipelined_pairs(kv_ref, kr_ref, n_pairs, n_blocks - 1)
        pipelined(kv_ref, kr_ref, 1 + 2 * n_pairs, n_full, n_blocks - 1, None)
        pipelined(kv_ref, kr_ref, jnp.maximum(n_full, 1), n_blocks, n_blocks - 1, mask_fn)
    else:
        update(scores(kv_ref[...], kr_ref[...]), kv_ref[...], None)

    o = acc_ref[...] / _lane_bcast(l_ref[...], KV_LORA)
    parts = []
    for h in range(N_HEADS):
        oh = o[h * tq:(h + 1) * tq, :].astype(BF16)
        parts.append(jnp.dot(oh, wuv_ref[h], preferred_element_type=F32))
    attn = jnp.concatenate(parts, axis=-1)
    out_ref[...] = _rms(attn, gao_ref[...]).astype(BF16)


def _attention(cqn, w_q, w_ukt, w_uv, cosq, sinq, gao, ckv, krope, *, n_batch, seq, row0,
               past_kv=None, past_kr=None):
    causal = past_kv is None
    tq = TQ if causal else seq
    nq = seq // tq
    n_past = 0 if causal else past_kv.shape[1]
    if not causal:
        assert n_past % CHUNK == 0 and seq <= CHUNK and n_past % TK == 0
    blk0 = row0 // tq
    qrow = lambda b, q: (blk0 + b * nq + q, 0)
    in_specs = [
        pl.BlockSpec((tq, Q_LORA), qrow),
        _const_spec(w_q.shape),
        _const_spec(w_ukt.shape),
        _const_spec(w_uv.shape),
        pl.BlockSpec((tq, N_HEADS * ROPE_DIM), lambda b, q: (q, 0)),
        pl.BlockSpec((tq, N_HEADS * ROPE_DIM), lambda b, q: (q, 0)),
        _const_spec((1, ATTN_W)),
    ]
    args = [cqn, w_q, w_ukt, w_uv, cosq, sinq, gao]
    if n_past:
        in_specs += [pl.BlockSpec((None, n_past, KV_LORA), lambda b, q: (b, 0, 0)),
                     pl.BlockSpec((None, ROPE_DIM, n_past), lambda b, q: (b, 0, 0))]
        args += [past_kv, past_kr]
    in_specs += [pl.BlockSpec((seq, KV_LORA), lambda b, q: (b, 0)),
                 pl.BlockSpec((None, ROPE_DIM, seq), lambda b, q: (b, 0, 0))]
    args += [ckv, krope]
    rows = N_HEADS * tq
    kern = functools.partial(_attn_kernel, tq=tq, n_past=n_past, causal=causal)
    return pl.pallas_call(
        kern,
        grid=(n_batch, nq),
        in_specs=in_specs,
        out_specs=pl.BlockSpec((tq, ATTN_W), lambda b, q: (b * nq + q, 0)),
        out_shape=jax.ShapeDtypeStruct((n_batch * seq, ATTN_W), BF16),
        scratch_shapes=[
            pltpu.VMEM((rows, KV_LORA), BF16),
            pltpu.VMEM((rows, ROPE_DIM), BF16),
            pltpu.VMEM((rows, LANES), F32),
            pltpu.VMEM((rows, LANES), F32),
            pltpu.VMEM((rows, KV_LORA), F32),
            pltpu.VMEM((2, rows, TK), F32),
            pltpu.VMEM((rows, LANES), jnp.int32),
        ],
        compiler_params=pltpu.CompilerParams(dimension_semantics=("arbitrary", "arbitrary"),
                                             vmem_limit_bytes=VMEM_LIMIT),
        name="attn_prompt" if causal else "attn_sample",
    )(*args)


def _out_proj_kernel(attnp_ref, attns_ref, convn_ref, xp_ref, xs_ref, wo_ref, gffn_ref, wr_ref,
                     br_ref, h_ref, xpk_ref, mi_ref, mf_ref, cnt_ref, carry_ref, logit_ref, *, n_prompt_tiles):
    i = pl.program_id(0)

    @pl.when(i == 0)
    def _():
        carry_ref[...] = jnp.zeros(carry_ref.shape, F32)
        logit_ref[...] = jnp.zeros(logit_ref.shape, F32)

    def tile(x_ref, attn_ref):
        prev_logits = logit_ref[...]
        y = jnp.dot(attn_ref[...], wo_ref[:ATTN_W, :], preferred_element_type=F32)
        y = y + jnp.dot(convn_ref[...], wo_ref[ATTN_W:, :], preferred_element_type=F32)
        h = x_ref[...] + y
        h_ref[...] = h
        xn = _rms(h, gffn_ref[...])

        half = D_MODEL // 2
        xh = xn.astype(BF16)
        xh32 = xh.astype(F32)
        lo = lax.bitcast_convert_type(xh32[:, :half], jnp.uint32)
        hi = lax.bitcast_convert_type(xh32[:, half:], jnp.uint32)
        xpk_ref[...] = (lo >> 16) | (hi & jnp.uint32(0xFFFF0000))

        xl = (xn - xh32).astype(BF16)
        hh_hl = jnp.dot(xh, wr_ref[...], preferred_element_type=F32)
        lh = jnp.dot(xl, wr_ref[:, :LANES], preferred_element_type=F32)
        logit_ref[...] = hh_hl[:, :LANES] + (lh + hh_hl[:, LANES:]) + br_ref[...]

        logits = prev_logits
        counted = (i > 0).astype(F32)
        lane = lax.broadcasted_iota(jnp.int32, (TM, LANES), 1).astype(F32)
        ninf = -jnp.inf
        far = float(LANES)

        def first_argmax(v):
            vmax = jnp.max(v, axis=-1, keepdims=True)
            return vmax, jnp.min(jnp.where(v == vmax, lane, far), axis=-1, keepdims=True)

        gl = jnp.where(lane < N_GROUPS, logits, ninf)
        gmax, gidx = first_argmax(gl)
        g_p = 1.0 / jnp.sum(jnp.exp(gl - gmax), axis=-1, keepdims=True)
        e_lo = N_GROUPS + EXPERTS_PER_GROUP * gidx
        el = jnp.where((lane >= e_lo) & (lane < e_lo + EXPERTS_PER_GROUP), logits, ninf)
        e1max, i1 = first_argmax(el)
        z = jnp.sum(jnp.exp(el - e1max), axis=-1, keepdims=True)
        el2 = jnp.where(lane == i1, ninf, el)
        e2max, i2 = first_argmax(el2)
        p1 = 1.0 / z
        p2 = jnp.exp(e2max - e1max) / z
        den = p1 + p2
        g0 = g_p * p1 / den
        g1 = g_p * p2 / den
        e0 = i1 - N_GROUPS
        e1 = i2 - N_GROUPS

        oh0 = lane == e0
        oh1 = lane == e1
        oh = jnp.where(oh0 | oh1, 1.0, 0.0)
        r = lax.broadcasted_iota(jnp.int32, (TM, TM), 0)
        c = lax.broadcasted_iota(jnp.int32, (TM, TM), 1)
        ltri = jnp.where(r > c, 1.0, 0.0).astype(BF16)
        before = jnp.dot(ltri, oh.astype(BF16), preferred_element_type=F32) + carry_ref[...]
        rank0 = jnp.sum(jnp.where(oh0, before, 0.0), axis=-1, keepdims=True)
        rank1 = jnp.sum(jnp.where(oh1, before, 0.0), axis=-1, keepdims=True)
        total = carry_ref[...] + counted * jnp.sum(oh, axis=0, keepdims=True)
        carry_ref[...] = total
        cnt_ref[...] = jnp.broadcast_to(total, cnt_ref.shape)

        mi = jnp.where(lane == 0, e0, jnp.where(lane == 1, e1, jnp.where(lane == 2, rank0, rank1)))
        mi_ref[...] = jnp.transpose(mi)[:SUBLANES, :].astype(jnp.int32)
        mf_ref[...] = jnp.where(lane == 0, g0, g1)

    @pl.when(i < n_prompt_tiles)
    def _():
        tile(xp_ref, attnp_ref)

    @pl.when(i >= n_prompt_tiles)
    def _():
        tile(xs_ref, attns_ref)


def _out_proj(attn_p, attn_s, conv_n, xp, xs, w_ob, gffn, w_r2, b_r):
    m = conv_n.shape[0]
    npt = xp.shape[0] // TM
    n_tiles = m // TM
    last_p, last_s, last = npt - 1, n_tiles - npt - 1, n_tiles - 1
    row = lambda i: (jnp.minimum(i, last), 0)
    prow = lambda i: (jnp.minimum(i, last_p), 0)
    srow = lambda i: (jnp.clip(i - npt, 0, last_s), 0)
    lag = lambda i: jnp.maximum(i - 1, 0)
    return pl.pallas_call(
        functools.partial(_out_proj_kernel, n_prompt_tiles=npt),
        grid=(n_tiles + 1,),
        in_specs=[
            pl.BlockSpec((TM, ATTN_W), prow),
            pl.BlockSpec((TM, ATTN_W), srow),
            pl.BlockSpec((TM, CONV_CH), row),
            pl.BlockSpec((TM, D_MODEL), prow),
            pl.BlockSpec((TM, D_MODEL), srow),
            _const_spec(w_ob.shape),
            _const_spec((1, D_MODEL)),
            _const_spec(w_r2.shape),
            _const_spec((1, LANES)),
        ],
        out_specs=[
            pl.BlockSpec((TM, D_MODEL), row),
            pl.BlockSpec((TM, D_MODEL // 2), row),
            pl.BlockSpec((SUBLANES, TM), lambda i: (0, lag(i))),
            pl.BlockSpec((TM, LANES), lambda i: (lag(i), 0)),
            pl.BlockSpec((SUBLANES, LANES), lambda i: (0, 0)),
        ],
        out_shape=[
            jax.ShapeDtypeStruct((m, D_MODEL), F32),
            jax.ShapeDtypeStruct((m, D_MODEL // 2), jnp.uint32),
            jax.ShapeDtypeStruct((SUBLANES, m), jnp.int32),
            jax.ShapeDtypeStruct((m, LANES), F32),
            jax.ShapeDtypeStruct((SUBLANES, LANES), F32),
        ],
        scratch_shapes=[pltpu.VMEM((1, LANES), F32), pltpu.VMEM((TM, LANES), F32)],
        compiler_params=pltpu.CompilerParams(dimension_semantics=("arbitrary",),
                                             vmem_limit_bytes=VMEM_LIMIT),
        name="out_proj",
    )(attn_p, attn_s, conv_n, xp, xs, w_ob, gffn, w_r2, b_r)


def _dispatch_kernel(d0_ref, d1_ref, zlo_ref, zn_ref, nu_ref, xpk_ref, xs_hbm, zeros_ref, sems, *, n_blocks):
    i = pl.program_id(0)
    sem = sems.at[0]
    zsem = sems.at[1]

    def zero_fill(act):
        def per_expert(e, c):
            lo = zlo_ref[e]
            n = zn_ref[e]
            head = (-lo) & (SUBLANES - 1)
            for r in range(SUBLANES - 1):
                @pl.when(r < head)
                def _(r=r):
                    act(pltpu.make_async_copy(zeros_ref.at[pl.ds(0, 1)], xs_hbm.at[pl.ds(lo + r, 1)], zsem))
            off = lo + head
            rest = n - head
            size = MOE_BLOCK // 2
            while size >= SUBLANES:
                @pl.when((rest & size) != 0)
                def _(off=off, size=size):
                    dst = xs_hbm.at[pl.ds(pl.multiple_of(off, SUBLANES), size)]
                    act(pltpu.make_async_copy(zeros_ref.at[pl.ds(0, size)], dst, zsem))
                off = off + (rest & size)
                size //= 2
            return c

        def per_block(b, c):
            dst = xs_hbm.at[pl.ds(pl.multiple_of(b * MOE_BLOCK, MOE_BLOCK), MOE_BLOCK)]
            act(pltpu.make_async_copy(zeros_ref, dst, zsem))
            return c

        lax.fori_loop(0, N_EXPERTS, per_expert, 0)
        lax.fori_loop(nu_ref[0], n_blocks, per_block, 0)

    @pl.when(i == 0)
    def _():
        zeros_ref[...] = jnp.zeros(zeros_ref.shape, zeros_ref.dtype)
        zero_fill(lambda cp: cp.start())

    @pl.when(i == pl.num_programs(0) - 1)
    def _():
        zero_fill(lambda cp: cp.wait())

    base = i * TD

    def start(g, c):
        for u in range(SUBLANES):
            r = base + g * SUBLANES + u
            src = xpk_ref.at[g, pl.ds(u, 1)]
            pltpu.make_async_copy(src, xs_hbm.at[pl.ds(d0_ref[r], 1)], sem).start()
            pltpu.make_async_copy(src, xs_hbm.at[pl.ds(d1_ref[r], 1)], sem).start()
        return c

    lax.fori_loop(0, TD // SUBLANES, start, 0)
    for _ in range(2):
        pltpu.make_async_copy(xs_hbm.at[pl.ds(0, TD)], xs_hbm.at[pl.ds(0, TD)], sem).wait()


def _dispatch(dest0, dest1, pad_lo, n_pad, n_used, xpk, n_blocks):
    m = xpk.shape[0]
    grid_spec = pltpu.PrefetchScalarGridSpec(
        num_scalar_prefetch=5,
        grid=(m // TD,),
        in_specs=[pl.BlockSpec((TD // SUBLANES, SUBLANES, D_MODEL // 2), lambda i, *_: (i, 0, 0))],
        out_specs=pl.BlockSpec(memory_space=pl.ANY),
        scratch_shapes=[pltpu.VMEM((MOE_BLOCK, D_MODEL // 2), jnp.uint32),
                        pltpu.SemaphoreType.DMA((2,))],
    )
    return pl.pallas_call(
        functools.partial(_dispatch_kernel, n_blocks=n_blocks),
        grid_spec=grid_spec,
        out_shape=jax.ShapeDtypeStruct((n_blocks * MOE_BLOCK, D_MODEL // 2), jnp.uint32),
        compiler_params=pltpu.CompilerParams(dimension_semantics=("arbitrary",)),
        name="dispatch",
    )(dest0, dest1, pad_lo, n_pad, n_used, xpk.reshape(m // SUBLANES, SUBLANES, D_MODEL // 2))


def _experts_kernel(be_ref, nu_ref, nxt_ref, x_ref, wg_hbm, wu_hbm, wd_hbm, y_ref,
                    sg_ref, su_ref, sd_ref, wgb_ref, wub_ref, wdb_ref, sems):
    b = pl.program_id(0)
    active = b < nu_ref[0]
    new_expert = jnp.logical_or(b == 0, be_ref[b] != be_ref[jnp.maximum(b - 1, 0)])

    def weight_copies(e):
        return (pltpu.make_async_copy(wg_hbm.at[e], sg_ref, sems.at[0]),
                pltpu.make_async_copy(wu_hbm.at[e], su_ref, sems.at[1]),
                pltpu.make_async_copy(wd_hbm.at[e], sd_ref, sems.at[2]))

    @pl.when(b == 0)
    def _():
        for cp in weight_copies(be_ref[0]):
            cp.start()

    @pl.when(jnp.logical_and(active, new_expert))
    def _():
        for cp in weight_copies(be_ref[b]):
            cp.wait()
        wgb_ref[...] = sg_ref[...].astype(BF16)
        wub_ref[...] = su_ref[...].astype(BF16)
        wdb_ref[...] = sd_ref[...].astype(BF16)

        @pl.when(nxt_ref[b] >= 0)
        def _():
            for cp in weight_copies(nxt_ref[b]):
                cp.start()

    @pl.when(active)
    def _():
        half = D_MODEL // 2
        xa, xb = (v.astype(BF16) for v in _unpack_bf16_pairs(x_ref[...]))
        g = jnp.dot(xa, wgb_ref[:half, :], preferred_element_type=F32)
        g = g + jnp.dot(xb, wgb_ref[half:, :], preferred_element_type=F32)
        u = jnp.dot(xa, wub_ref[:half, :], preferred_element_type=F32)
        u = u + jnp.dot(xb, wub_ref[half:, :], preferred_element_type=F32)
        hmid = (g * jax.nn.sigmoid(g)) * u
        y = jnp.dot(hmid.astype(BF16), wdb_ref[...], preferred_element_type=F32)
        y_ref[...] = _pack_bf16_pairs(y)

    @pl.when(b >= nu_ref[0])
    def _():
        y_ref[...] = jnp.zeros(y_ref.shape, y_ref.dtype)


def _experts(block_e, n_used, next_e, x_sorted, w_gate, w_up, w_down):
    p = x_sorted.shape[0]
    nb = p // MOE_BLOCK

    def xrow(b, be, nu, nxt):
        return (jnp.maximum(jnp.minimum(b, nu[0] - 1), 0), 0)

    grid_spec = pltpu.PrefetchScalarGridSpec(
        num_scalar_prefetch=3,
        grid=(nb,),
        in_specs=[
            pl.BlockSpec((MOE_BLOCK, D_MODEL // 2), xrow),
            pl.BlockSpec(memory_space=pl.ANY),
            pl.BlockSpec(memory_space=pl.ANY),
            pl.BlockSpec(memory_space=pl.ANY),
        ],
        out_specs=pl.BlockSpec((MOE_BLOCK, D_MODEL // 2), lambda b, be, nu, nxt: (b, 0)),
        scratch_shapes=[pltpu.VMEM((D_MODEL, D_FF), F32), pltpu.VMEM((D_MODEL, D_FF), F32),
                        pltpu.VMEM((D_FF, D_MODEL), F32),
                        pltpu.VMEM((D_MODEL, D_FF), BF16), pltpu.VMEM((D_MODEL, D_FF), BF16),
                        pltpu.VMEM((D_FF, D_MODEL), BF16),
                        pltpu.SemaphoreType.DMA((3,))],
    )
    return pl.pallas_call(
        _experts_kernel,
        grid_spec=grid_spec,
        out_shape=jax.ShapeDtypeStruct((p, D_MODEL // 2), jnp.uint32),
        compiler_params=pltpu.CompilerParams(dimension_semantics=("arbitrary",),
                                             vmem_limit_bytes=VMEM_LIMIT),
        name="experts",
    )(block_e, n_used, next_e, x_sorted, w_gate, w_up, w_down)


def _combine_kernel(d0_ref, d1_ref, h_ref, mf_ref, gfin_ref, y_hbm, outp_ref, outs_ref, y0_ref, y1_ref, sems,
                    *, n_tiles, n_prompt_tiles):
    i = pl.program_id(0)

    def gather(tile, slot, act):
        base = tile * TM

        def body(g, c):
            for u in range(SUBLANES):
                r = base + g * SUBLANES + u
                act(pltpu.make_async_copy(y_hbm.at[pl.ds(d0_ref[r], 1)], y0_ref.at[slot, g, pl.ds(u, 1)],
                                          sems.at[slot]))
                act(pltpu.make_async_copy(y_hbm.at[pl.ds(d1_ref[r], 1)], y1_ref.at[slot, g, pl.ds(u, 1)],
                                          sems.at[slot]))
            return c
        lax.fori_loop(0, TM // SUBLANES, body, 0)

    @pl.when(i == 0)
    def _():
        gather(0, 0, lambda cp: cp.start())

    @pl.when(i + 1 < n_tiles)
    def _():
        gather(i + 1, (i + 1) % 2, lambda cp: cp.start())

    slot = i % 2
    for _ in range(2):
        pltpu.make_async_copy(y_hbm.at[pl.ds(0, TM)], y_hbm.at[pl.ds(0, TM)], sems.at[slot]).wait()

    def finish(out_ref):
        mf = mf_ref[...]
        g0, g1 = mf[:, :, 0:1], mf[:, :, 1:2]
        half = D_MODEL // 2
        a0, b0 = _unpack_bf16_pairs(y0_ref[slot])
        a1, b1 = _unpack_bf16_pairs(y1_ref[slot])
        o_lo = h_ref[:, :, :half] + (g0 * a0 + g1 * a1)
        o_hi = h_ref[:, :, half:] + (g0 * b0 + g1 * b1)
        sumsq = jnp.sum(o_lo * o_lo, axis=-1, keepdims=True) + jnp.sum(o_hi * o_hi, axis=-1, keepdims=True)
        inv_rms = lax.rsqrt(sumsq / D_MODEL + EPS)
        out_ref[:, :, :half] = o_lo * inv_rms * gfin_ref[:, :, :half]
        out_ref[:, :, half:] = o_hi * inv_rms * gfin_ref[:, :, half:]

    @pl.when(i < n_prompt_tiles)
    def _():
        finish(outp_ref)

    @pl.when(i >= n_prompt_tiles)
    def _():
        finish(outs_ref)


def _combine(dest0, dest1, h, mf, gfin, y_sorted, *, n_prompt_rows):
    m = h.shape[0]
    npt = n_prompt_rows // TM
    tg = TM // SUBLANES
    grouped = lambda a: a.reshape(a.shape[0] // SUBLANES, SUBLANES, a.shape[1])
    grid_spec = pltpu.PrefetchScalarGridSpec(
        num_scalar_prefetch=2,
        grid=(m // TM,),
        in_specs=[
            pl.BlockSpec((tg, SUBLANES, D_MODEL), lambda i, *_: (i, 0, 0)),
            pl.BlockSpec((tg, SUBLANES, LANES), lambda i, *_: (i, 0, 0)),
            pl.BlockSpec((1, 1, D_MODEL), lambda i, *_: (0, 0, 0)),
            pl.BlockSpec(memory_space=pl.ANY),
        ],
        out_specs=[pl.BlockSpec((tg, SUBLANES, D_MODEL), lambda i, *_: (jnp.minimum(i, npt - 1), 0, 0)),
                   pl.BlockSpec((tg, SUBLANES, D_MODEL), lambda i, *_: (jnp.maximum(i - npt, 0), 0, 0))],
        scratch_shapes=[pltpu.VMEM((2, tg, SUBLANES, D_MODEL // 2), jnp.uint32),
                        pltpu.VMEM((2, tg, SUBLANES, D_MODEL // 2), jnp.uint32),
                        pltpu.SemaphoreType.DMA((2,))],
    )
    y_p, y_s = pl.pallas_call(
        functools.partial(_combine_kernel, n_tiles=m // TM, n_prompt_tiles=npt),
        grid_spec=grid_spec,
        out_shape=[jax.ShapeDtypeStruct((n_prompt_rows // SUBLANES, SUBLANES, D_MODEL), F32),
                   jax.ShapeDtypeStruct(((m - n_prompt_rows) // SUBLANES, SUBLANES, D_MODEL), F32)],
        compiler_params=pltpu.CompilerParams(dimension_semantics=("arbitrary",),
                                             vmem_limit_bytes=VMEM_LIMIT),
        name="combine",
    )(dest0, dest1, grouped(h), grouped(mf), gfin.reshape(1, 1, D_MODEL), y_sorted)
    return y_p.reshape(n_prompt_rows, D_MODEL), y_s.reshape(m - n_prompt_rows, D_MODEL)


def _rope_tables(pos):
    f32 = np.float32
    inv = np.power(f32(ROPE_THETA), -np.arange(0, ROPE_DIM, 2, dtype=f32) / f32(ROPE_DIM)).astype(f32)
    ang = (pos.astype(f32)[:, None] * inv[None, :]).astype(f32)
    cos, sin = np.cos(ang).astype(f32), np.sin(ang).astype(f32)
    return np.concatenate([cos, cos], axis=-1), np.concatenate([-sin, sin], axis=-1)


def _swap_halves(w):
    return jnp.concatenate([w[..., ROPE_DIM // 2:], w[..., :ROPE_DIM // 2]], axis=-1)


def kernel(x_prompt, x_sample, cache_kv_latent, cache_k_rope, state_conv, norm_mix, w_in, norm_q, w_uq,
           norm_kv, w_uk, w_uv, conv_w, norm_attn_out, norm_conv_out, w_o, norm_ffn, w_router_group,
           b_router_group, w_router_expert, b_router_expert, w_gate, w_up, w_down, norm_final):
    assert w_in.shape[0] == 1, "single-layer trunk"
    bp, seq_p, _ = x_prompt.shape
    bs, seq_s, _ = x_sample.shape
    past_len = cache_kv_latent.shape[2]
    np_rows, ns_rows = bp * seq_p, bs * seq_s
    m = np_rows + ns_rows
    assert seq_p % TM == 0 and TM % seq_s == 0 and ns_rows % TM == 0 and seq_s == CHUNK
    assert np_rows % TD == 0 and ns_rows % TD == 0

    xp = x_prompt.reshape(np_rows, D_MODEL)
    xs = x_sample.reshape(ns_rows, D_MODEL)
    row_vec = lambda v: v.reshape(1, -1)

    assert w_in.shape[2] == Q_LORA + KV_LORA + ROPE_DIM + 3 * CONV_CH
    w_t = jnp.swapaxes(w_in[0], 0, 1).astype(BF16)
    wq4 = w_uq[0].reshape(Q_LORA, N_HEADS, QK_NOPE + ROPE_DIM)
    wq_rope = wq4[:, :, QK_NOPE:]
    w_q = jnp.concatenate([wq4[:, :, :QK_NOPE].reshape(Q_LORA, -1), wq_rope.reshape(Q_LORA, -1),
                           _swap_halves(wq_rope).reshape(Q_LORA, -1)], axis=1).astype(BF16)
    w_ukt = jnp.transpose(w_uk[0], (1, 2, 0)).astype(BF16)
    w_uvh = jnp.transpose(w_uv[0], (1, 0, 2)).astype(BF16)
    w_ob = w_o[0].astype(BF16)
    n_router = N_GROUPS + N_EXPERTS
    w_r = jnp.concatenate([w_router_group[0], w_router_expert[0].reshape(D_MODEL, N_EXPERTS)], axis=1)
    w_r = jnp.pad(w_r, ((0, 0), (0, LANES - n_router)))
    w_rh = w_r.astype(BF16)
    w_rl = (w_r - w_rh.astype(F32)).astype(BF16)
    w_r2 = jnp.concatenate([w_rh, w_rl], axis=1)
    b_r =jnp.pad(jnp.concatenate([b_router_group[0], b_router_expert[0].reshape(N_EXPERTS)]),
                  (0, LANES - n_router)).reshape(1, LANES)

    cos_p, sin_p = _rope_tables(np.arange(seq_p))
    cos_s, sin_s = _rope_tables(past_len + np.arange(seq_s))
    cosk = np.concatenate([cos_p, np.tile(cos_s, (TM // seq_s, 1))], axis=0)
    sink = np.concatenate([sin_p, np.tile(sin_s, (TM // seq_s, 1))], axis=0)
    state = jnp.concatenate([jnp.zeros((bp, CONV_W - 1, CONV_CH), F32), state_conv[0]], axis=0)

    cqn, ckv_p, kr_p, ckv_s, kr_s, conv_n, utail = _in_proj(
        xp, xs, row_vec(norm_mix[0]), w_t, row_vec(norm_q[0]), row_vec(norm_kv[0]),
        row_vec(norm_conv_out[0]), conv_w[0], cosk, sink, state, seq_p=seq_p, seq_s=seq_s)

    gao = row_vec(norm_attn_out[0])
    attn_p = _attention(cqn, w_q, w_ukt, w_uvh, np.tile(cos_p, (1, N_HEADS)), np.tile(sin_p, (1, N_HEADS)),
                        gao, ckv_p, kr_p, n_batch=bp, seq=seq_p, row0=0)
    attn_s = _attention(cqn, w_q, w_ukt, w_uvh, np.tile(cos_s, (1, N_HEADS)), np.tile(sin_s, (1, N_HEADS)),
                        gao, ckv_s, kr_s, n_batch=bs, seq=seq_s, row0=np_rows,
                        past_kv=cache_kv_latent[0], past_kr=jnp.swapaxes(cache_k_rope[0], 1, 2))

    h, xpk, mi, mf, cnt = _out_proj(attn_p, attn_s, conv_n, xp, xs, w_ob, row_vec(norm_ffn[0]),
                                    w_r2, b_r)

    counts = cnt[0, :N_EXPERTS].astype(jnp.int32)
    padded = (counts + MOE_BLOCK - 1) // MOE_BLOCK * MOE_BLOCK
    pad_end = jnp.cumsum(padded)
    pad_start = pad_end - padded
    n_blocks = -(-(m * 2) // MOE_BLOCK) + N_EXPERTS
    block_row0 = jnp.arange(n_blocks, dtype=jnp.int32) * MOE_BLOCK
    block_e = jnp.minimum(jnp.sum((pad_end[None, :] <= block_row0[:, None]).astype(jnp.int32), axis=1),
                          N_EXPERTS - 1)
    n_used = (pad_end[-1:] // MOE_BLOCK).astype(jnp.int32)
    expert_ids = jnp.arange(N_EXPERTS, dtype=jnp.int32)[:, None]

    def seg_start(e):
        return jnp.sum(jnp.where(expert_ids == e[None, :], pad_start[:, None], 0), axis=0)

    dest0 = seg_start(mi[0]) + mi[2]
    dest1 = seg_start(mi[1]) + mi[3]

    x_sorted = _dispatch(dest0, dest1, pad_start + counts, padded - counts, n_used, xpk, n_blocks)
    later = (expert_ids.T > block_e[:, None]) & (padded > 0)[None, :]
    next_e = jnp.min(jnp.where(later, expert_ids.T, N_EXPERTS), axis=1)
    next_e = jnp.where(next_e == N_EXPERTS, -1, next_e).astype(jnp.int32)
    y_sorted = _experts(block_e, n_used, next_e, x_sorted, w_gate[0], w_up[0], w_down[0])
    gfin = row_vec(norm_final)
    y_p, y_s = _combine(dest0, dest1, h, mf, gfin, y_sorted, n_prompt_rows=np_rows)

    ut = utail.reshape(m // CHUNK, SUBLANES, CONV_CH)
    tails = ut[:, SUBLANES - (CONV_W - 1):, :]
    p_last = (jnp.arange(bp) + 1) * (seq_p // CHUNK) - 1
    s_last = np_rows // CHUNK + (jnp.arange(bs) + 1) * (seq_s // CHUNK) - 1
    return (y_p.reshape(bp, seq_p, D_MODEL),
            y_s.reshape(bs, seq_s, D_MODEL),
            ckv_p.reshape(1, bp, seq_p, KV_LORA),
            jnp.swapaxes(kr_p, 1, 2)[None],
            tails[p_last][None],
            ckv_s.reshape(1, bs, seq_s, KV_LORA),
            jnp.swapaxes(kr_s, 1, 2)[None],
            tails[s_last][None])
```

```python
import functools

import jax
import jax.numpy as jnp
import numpy as np
from jax import lax
from jax.experimental import pallas as pl
from jax.experimental.pallas import tpu as pltpu

F32 = jnp.float32
BF16 = jnp.bfloat16

D_MODEL = 2048
N_HEADS = 8
QK_NOPE = 128
ROPE_DIM = 64
V_DIM = 128
Q_LORA = 512
KV_LORA = 512
ATTN_W = N_HEADS * V_DIM
CONV_CH = D_MODEL - ATTN_W
CONV_W = 3
CHUNK = 64
N_GROUPS = 4
EXPERTS_PER_GROUP = 8
N_EXPERTS = N_GROUPS * EXPERTS_PER_GROUP
D_FF = 512
ROPE_THETA = 10000.0
EPS = 1e-6
ATTN_SCALE = (QK_NOPE + ROPE_DIM) ** -0.5
EXP2_SCALE = ATTN_SCALE * 1.4426950408889634

LANES = 128
SUBLANES = 8
TM = 256
TD = 2304
MOE_BLOCK = 256
TQ = 256
TK = 256
NEG_BIG = -1e30
V7X_VMEM_BYTES = 64 * 1024 * 1024
VMEM_LIMIT = V7X_VMEM_BYTES * 7 // 8


def _rms(v, g):
    return v * lax.rsqrt(jnp.mean(v * v, axis=-1, keepdims=True) + EPS) * g


def _lane_bcast(v, width):
    if width % LANES == 0:
        return jnp.concatenate([v] * (width // LANES), axis=1)
    assert width < LANES
    return v[:, :width]


def _pack_bf16_pairs(v):
    half = v.shape[-1] // 2
    lo = lax.bitcast_convert_type(v[..., :half].astype(BF16).astype(F32), jnp.uint32)
    hi = lax.bitcast_convert_type(v[..., half:].astype(BF16).astype(F32), jnp.uint32)
    return (lo >> 16) | (hi & jnp.uint32(0xFFFF0000))


def _unpack_bf16_pairs(w):
    return (lax.bitcast_convert_type(w << 16, F32),
            lax.bitcast_convert_type(w & jnp.uint32(0xFFFF0000), F32))


def _const_spec(shape):
    nd = len(shape)
    return pl.BlockSpec(shape, lambda *_: (0,) * nd, pipeline_mode=pl.Buffered(1))


def _in_proj_kernel(xp_ref, xs_ref, gmix_ref, wt_ref, gq_ref, gkv_ref, gco_ref, convw_ref,
                    cos_ref, sin_ref, state_ref,
                    cqn_ref, ckvp_ref, krp_ref, ckvs_ref, krs_ref, convn_ref, utail_ref, ext_ref,
                    *, n_prompt_tiles, tiles_per_seq, n_prompt_seq, sample_seq_len):
    i = pl.program_id(0)

    @pl.when(i == 0)
    def _():
        ext_ref[...] = jnp.zeros(ext_ref.shape, F32)

    def conv_block(u_sub, gate_sub, row0, length):
        ext_ref[SUBLANES:SUBLANES + length, :] = u_sub
        um1 = ext_ref[SUBLANES - 1:SUBLANES - 1 + length, :]
        um2 = ext_ref[SUBLANES - 2:SUBLANES - 2 + length, :]
        cw = convw_ref[...]
        conv = cw[0:1] * um2 + cw[1:2] * um1 + cw[2:3] * u_sub
        convn_ref[row0:row0 + length, :] = _rms(gate_sub * conv, gco_ref[...]).astype(BF16)

    def tile(x_ref, is_prompt):
        ckv_ref, krt_ref = (ckvp_ref, krp_ref) if is_prompt else (ckvs_ref, krs_ref)
        x = x_ref[...]
        xg = (x * gmix_ref[...]).astype(BF16)
        inv_rms = lax.rsqrt(jnp.mean(x * x, axis=-1, keepdims=True) + EPS)
        lat_w = Q_LORA + KV_LORA
        conv0 = lat_w + ROPE_DIM
        nt = (((1,), (1,)), ((), ()))

        def project(lo, hi):
            return inv_rms * lax.dot_general(xg, wt_ref[lo:hi, :], nt, preferred_element_type=F32)

        z_ch = project(conv0 + CONV_CH, conv0 + 3 * CONV_CH)
        u = z_ch[:, :CONV_CH] * z_ch[:, CONV_CH:]
        for j in range(TM // CHUNK):
            utail_ref[j] = u[CHUNK * (j + 1) - SUBLANES:CHUNK * (j + 1), :]
        gate_b = project(conv0, conv0 + CONV_CH)

        if is_prompt:
            first = (i % tiles_per_seq) == 0
            carried = ext_ref[TM + SUBLANES - 2:TM + SUBLANES, :]
            ext_ref[SUBLANES - 2:SUBLANES, :] = jnp.where(first, state_ref[i // tiles_per_seq], carried)
            conv_block(u, gate_b, 0, TM)
        else:
            n_sub = TM // sample_seq_len
            seq0 = n_prompt_seq + (i - n_prompt_tiles) * n_sub
            for k in range(n_sub):
                ext_ref[SUBLANES - 2:SUBLANES, :] = state_ref[seq0 + k]
                lo = k * sample_seq_len
                conv_block(u[lo:lo + sample_seq_len], gate_b[lo:lo + sample_seq_len], lo, sample_seq_len)

        zk = project(lat_w, conv0)
        zk_swapped = jnp.concatenate([zk[:, ROPE_DIM // 2:], zk[:, :ROPE_DIM // 2]], axis=1)
        k_rope = zk * cos_ref[...] + zk_swapped * sin_ref[...]
        if is_prompt:
            krt_ref[...] = k_rope.T
        else:
            for k in range(TM // sample_seq_len):
                krt_ref[k] = k_rope[k * sample_seq_len:(k + 1) * sample_seq_len, :].T
        ckv_ref[...] = _rms(project(Q_LORA, lat_w), gkv_ref[...])
        cqn_ref[...] = _rms(project(0, Q_LORA), gq_ref[...]).astype(BF16)

    @pl.when(i < n_prompt_tiles)
    def _():
        tile(xp_ref, True)

    @pl.when(i >= n_prompt_tiles)
    def _():
        tile(xs_ref, False)


def _in_proj(xp, xs, gmix, w_t, gq, gkv, gco, convw, cosk, sink, state, *, seq_p, seq_s):
    np_rows, ns_rows = xp.shape[0], xs.shape[0]
    m = np_rows + ns_rows
    npt, nst = np_rows // TM, ns_rows // TM
    tps = seq_p // TM
    n_prompt_seq = np_rows // seq_p
    last_p = npt - 1

    def tab_idx(i):
        return (jnp.where(i < npt, i % tps, tps), 0)

    row = lambda i: (i, 0)
    prow = lambda i: (jnp.minimum(i, last_p), 0)
    srow = lambda i: (jnp.maximum(i - npt, 0), 0)
    kern = functools.partial(_in_proj_kernel, n_prompt_tiles=npt, tiles_per_seq=tps,
                             n_prompt_seq=n_prompt_seq, sample_seq_len=seq_s)
    return pl.pallas_call(
        kern,
        grid=(npt + nst,),
        in_specs=[
            pl.BlockSpec((TM, D_MODEL), prow),
            pl.BlockSpec((TM, D_MODEL), srow),
            _const_spec((1, D_MODEL)),
            _const_spec(w_t.shape),
            _const_spec((1, Q_LORA)),
            _const_spec((1, KV_LORA)),
            _const_spec((1, CONV_CH)),
            _const_spec((CONV_W, CONV_CH)),
            pl.BlockSpec((TM, ROPE_DIM), tab_idx),
            pl.BlockSpec((TM, ROPE_DIM), tab_idx),
            _const_spec(state.shape),
        ],
        out_specs=[
            pl.BlockSpec((TM, Q_LORA), row),
            pl.BlockSpec((TM, KV_LORA), prow),
            pl.BlockSpec((None, ROPE_DIM, TM), lambda i: (jnp.minimum(i, last_p) // tps, 0,
                                                          jnp.minimum(i, last_p) % tps)),
            pl.BlockSpec((TM, KV_LORA), srow),
            pl.BlockSpec((TM // seq_s, ROPE_DIM, seq_s), lambda i: (jnp.maximum(i - npt, 0), 0, 0)),
            pl.BlockSpec((TM, CONV_CH), row),
            pl.BlockSpec((TM // CHUNK, SUBLANES, CONV_CH), lambda i: (i, 0, 0)),
        ],
        out_shape=[
            jax.ShapeDtypeStruct((m, Q_LORA), BF16),
            jax.ShapeDtypeStruct((np_rows, KV_LORA), F32),
            jax.ShapeDtypeStruct((n_prompt_seq, ROPE_DIM, seq_p), F32),
            jax.ShapeDtypeStruct((ns_rows, KV_LORA), F32),
            jax.ShapeDtypeStruct((ns_rows // seq_s, ROPE_DIM, seq_s), F32),
            jax.ShapeDtypeStruct((m, CONV_CH), BF16),
            jax.ShapeDtypeStruct((m // CHUNK, SUBLANES, CONV_CH), F32),
        ],
        scratch_shapes=[pltpu.VMEM((TM + SUBLANES, CONV_CH), F32)],
        compiler_params=pltpu.CompilerParams(dimension_semantics=("arbitrary",),
                                             vmem_limit_bytes=VMEM_LIMIT),
        name="in_proj",
    )(xp, xs, gmix, w_t, gq, gkv, gco, convw, cosk, sink, state)


def _attn_kernel(*refs, tq, n_past, causal):
    refs = list(refs)
    cqn_ref, wq_ref, wuk_ref, wuv_ref, cos_ref, sin_ref, gao_ref = refs[:7]
    refs = refs[7:]
    if n_past:
        pkv_ref, pkr_ref = refs[:2]
        refs = refs[2:]
    kv_ref, kr_ref, out_ref, qlat_ref, qr_ref, m_ref, l_ref, acc_ref, s_ref, klim_ref = refs

    qi = pl.program_id(1)
    rows = N_HEADS * tq

    q = jnp.dot(cqn_ref[...], wq_ref[...], preferred_element_type=F32)
    nope_w = N_HEADS * QK_NOPE
    rope_w = N_HEADS * ROPE_DIM
    qrope = q[:, nope_w:nope_w + rope_w] * cos_ref[...] + q[:, nope_w + rope_w:] * sin_ref[...]
    for h in range(N_HEADS):
        qn = q[:, h * QK_NOPE:(h + 1) * QK_NOPE].astype(BF16)
        ql = jnp.dot(qn, wuk_ref[h], preferred_element_type=F32)
        qlat_ref[h * tq:(h + 1) * tq, :] = ql.astype(BF16)
        qr_ref[h * tq:(h + 1) * tq, :] = qrope[:, h * ROPE_DIM:(h + 1) * ROPE_DIM].astype(BF16)


    nt = (((1,), (1,)), ((), ()))

    def scores(kc_f32, krt_f32):
        s = lax.dot_general(qlat_ref[...], kc_f32.astype(BF16), nt, preferred_element_type=F32)
        return s + jnp.dot(qr_ref[...], krt_f32.astype(BF16), preferred_element_type=F32)

    def update(s, kc_f32, mask, first=False):
        if mask is not None:
            s = jnp.where(mask, s, NEG_BIG)
        m_cur = jnp.max(s, axis=-1, keepdims=True)
        if first:
            m_new = jnp.broadcast_to(m_cur, m_ref.shape)
        else:
            m_prev = m_ref[...]
            m_new = jnp.maximum(m_prev, m_cur)
            alpha = jnp.exp2((m_prev - m_new) * EXP2_SCALE)
        p = jnp.exp2((s - _lane_bcast(m_new, s.shape[1])) * EXP2_SCALE)
        l_cur = jnp.sum(p, axis=-1, keepdims=True)
        pv = jnp.dot(p.astype(BF16), kc_f32.astype(BF16), preferred_element_type=F32)
        if first:
            l_ref[...] = jnp.broadcast_to(l_cur, l_ref.shape)
            acc_ref[...] = pv
        else:
            l_ref[...] = alpha * l_ref[...] + l_cur
            acc_ref[...] = _lane_bcast(alpha, KV_LORA) * acc_ref[...] + pv
        m_ref[...] = m_new

    def pipelined(kv, kr, lo, hi, last, mask_fn):
        def body(j, c):
            k0 = pl.multiple_of(j * TK, TK)
            k1 = pl.multiple_of(jnp.minimum(j + 1, last) * TK, TK)
            s_cur = s_ref[j % 2]
            s_ref[(j + 1) % 2] = scores(kv[pl.ds(k1, TK), :], kr[:, pl.ds(k1, TK)])
            update(s_cur, kv[pl.ds(k0, TK), :], None if mask_fn is None else mask_fn(k0))
            return c
        lax.fori_loop(lo, hi, body, 0)

    def pipelined_pairs(kv, kr, n_pairs, last):
        def body(i, c):
            ka = pl.multiple_of((2 * i + 1) * TK, TK)
            kb = pl.multiple_of((2 * i + 2) * TK, TK)
            kc = pl.multiple_of(jnp.minimum(2 * i + 3, last) * TK, TK)
            s_ref[0] = scores(kv[pl.ds(kb, TK), :], kr[:, pl.ds(kb, TK)])
            update(s_ref[1], kv[pl.ds(ka, TK), :], None)
            s_ref[1] = scores(kv[pl.ds(kc, TK), :], kr[:, pl.ds(kc, TK)])
            update(s_ref[0], kv[pl.ds(kb, TK), :], None)
            return c
        lax.fori_loop(0, n_pairs, body, 0)

    def first_block(kv, kr, last, mask):
        k1 = pl.multiple_of(jnp.minimum(1, last) * TK, TK)
        s_ref[0] = scores(kv[pl.ds(0, TK), :], kr[:, pl.ds(0, TK)])
        s_ref[1] = scores(kv[pl.ds(k1, TK), :], kr[:, pl.ds(k1, TK)])
        update(s_ref[0], kv[pl.ds(0, TK), :], mask, first=True)

    if n_past:
        n_pb = n_past // TK
        first_block(pkv_ref, pkr_ref, n_pb - 1, None)
        n_pairs = (n_pb - 1) // 2
        pipelined_pairs(pkv_ref, pkr_ref, n_pairs, n_pb - 1)
        if 1 + 2 * n_pairs < n_pb:
            pipelined(pkv_ref, pkr_ref, 1 + 2 * n_pairs, n_pb, n_pb - 1, None)

    if causal:
        n_blocks = ((qi + 1) * tq + TK - 1) // TK
        n_full = jnp.minimum((qi * tq // CHUNK + 1) * CHUNK // TK, n_blocks)

        assert tq & (tq - 1) == 0 and CHUNK & (CHUNK - 1) == 0
        r = lax.broadcasted_iota(jnp.int32, (rows, LANES), 0)
        q_pos = qi * tq + (r & (tq - 1))
        klim_ref[...] = (q_pos & ~(CHUNK - 1)) + CHUNK

        def mask_fn(k0):
            cidx = lax.broadcasted_iota(jnp.int32, (rows, TK), 1)
            return cidx < _lane_bcast(klim_ref[...] - k0, TK)

        first_block(kv_ref, kr_ref, n_blocks - 1, mask_fn(0))
        n_pairs = jnp.maximum(n_full - 1, 0) // 2
        pipelined_pairs(kv_ref, kr_ref, n_pairs, n_blocks - 1)
        pipelined(kv_ref, kr_ref, 1 + 2 * n_pairs, n_full, n_blocks - 1, None)
        pipelined(kv_ref, kr_ref, jnp.maximum(n_full, 1), n_blocks, n_blocks - 1, mask_fn)
    else:
        update(scores(kv_ref[...], kr_ref[...]), kv_ref[...], None)

    o = acc_ref[...] / _lane_bcast(l_ref[...], KV_LORA)
    parts = []
    for h in range(N_HEADS):
        oh = o[h * tq:(h + 1) * tq, :].astype(BF16)
        parts.append(jnp.dot(oh, wuv_ref[h], preferred_element_type=F32))
    attn = jnp.concatenate(parts, axis=-1)
    out_ref[...] = _rms(attn, gao_ref[...]).astype(BF16)


def _attention(cqn, w_q, w_ukt, w_uv, cosq, sinq, gao, ckv, krope, *, n_batch, seq, row0,
               past_kv=None, past_kr=None):
    causal = past_kv is None
    tq = TQ if causal else seq
    nq = seq // tq
    n_past = 0 if causal else past_kv.shape[1]
    if not causal:
        assert n_past % CHUNK == 0 and seq <= CHUNK and n_past % TK == 0
    blk0 = row0 // tq
    qrow = lambda b, q: (blk0 + b * nq + q, 0)
    in_specs = [
        pl.BlockSpec((tq, Q_LORA), qrow),
        _const_spec(w_q.shape),
        _const_spec(w_ukt.shape),
        _const_spec(w_uv.shape),
        pl.BlockSpec((tq, N_HEADS * ROPE_DIM), lambda b, q: (q, 0)),
        pl.BlockSpec((tq, N_HEADS * ROPE_DIM), lambda b, q: (q, 0)),
        _const_spec((1, ATTN_W)),
    ]
    args = [cqn, w_q, w_ukt, w_uv, cosq, sinq, gao]
    if n_past:
        in_specs += [pl.BlockSpec((None, n_past, KV_LORA), lambda b, q: (b, 0, 0)),
                     pl.BlockSpec((None, ROPE_DIM, n_past), lambda b, q: (b, 0, 0))]
        args += [past_kv, past_kr]
    in_specs += [pl.BlockSpec((seq, KV_LORA), lambda b, q: (b, 0)),
                 pl.BlockSpec((None, ROPE_DIM, seq), lambda b, q: (b, 0, 0))]
    args += [ckv, krope]
    rows = N_HEADS * tq
    kern = functools.partial(_attn_kernel, tq=tq, n_past=n_past, causal=causal)
    return pl.pallas_call(
        kern,
        grid=(n_batch, nq),
        in_specs=in_specs,
        out_specs=pl.BlockSpec((tq, ATTN_W), lambda b, q: (b * nq + q, 0)),
        out_shape=jax.ShapeDtypeStruct((n_batch * seq, ATTN_W), BF16),
        scratch_shapes=[
            pltpu.VMEM((rows, KV_LORA), BF16),
            pltpu.VMEM((rows, ROPE_DIM), BF16),
            pltpu.VMEM((rows, LANES), F32),
            pltpu.VMEM((rows, LANES), F32),
            pltpu.VMEM((rows, KV_LORA), F32),
            pltpu.VMEM((2, rows, TK), F32),
            pltpu.VMEM((rows, LANES), jnp.int32),
        ],
        compiler_params=pltpu.CompilerParams(dimension_semantics=("arbitrary", "arbitrary"),
                                             vmem_limit_bytes=VMEM_LIMIT),
        name="attn_prompt" if causal else "attn_sample",
    )(*args)


def _out_proj_kernel(attnp_ref, attns_ref, convn_ref, xp_ref, xs_ref, wo_ref, gffn_ref, wr_ref,
                     br_ref, h_ref, xpk_ref, mi_ref, mf_ref, cnt_ref, carry_ref, logit_ref, *, n_prompt_tiles):
    i = pl.program_id(0)

    @pl.when(i == 0)
    def _():
        carry_ref[...] = jnp.zeros(carry_ref.shape, F32)
        logit_ref[...] = jnp.zeros(logit_ref.shape, F32)

    def tile(x_ref, attn_ref):
        prev_logits = logit_ref[...]
        y = jnp.dot(attn_ref[...], wo_ref[:ATTN_W, :], preferred_element_type=F32)
        y = y + jnp.dot(convn_ref[...], wo_ref[ATTN_W:, :], preferred_element_type=F32)
        h = x_ref[...] + y
        h_ref[...] = h
        xn = _rms(h, gffn_ref[...])

        half = D_MODEL // 2
        xh = xn.astype(BF16)
        xh32 = xh.astype(F32)
        lo = lax.bitcast_convert_type(xh32[:, :half], jnp.uint32)
        hi = lax.bitcast_convert_type(xh32[:, half:], jnp.uint32)
        xpk_ref[...] = (lo >> 16) | (hi & jnp.uint32(0xFFFF0000))

        xl = (xn - xh32).astype(BF16)
        hh_hl = jnp.dot(xh, wr_ref[...], preferred_element_type=F32)
        lh = jnp.dot(xl, wr_ref[:, :LANES], preferred_element_type=F32)
        logit_ref[...] = hh_hl[:, :LANES] + (lh + hh_hl[:, LANES:]) + br_ref[...]

        logits = prev_logits
        counted = (i > 0).astype(F32)
        lane = lax.broadcasted_iota(jnp.int32, (TM, LANES), 1).astype(F32)
        ninf = -jnp.inf
        far = float(LANES)

        def first_argmax(v):
            vmax = jnp.max(v, axis=-1, keepdims=True)
            return vmax, jnp.min(jnp.where(v == vmax, lane, far), axis=-1, keepdims=True)

        gl = jnp.where(lane < N_GROUPS, logits, ninf)
        gmax, gidx = first_argmax(gl)
        g_p = 1.0 / jnp.sum(jnp.exp(gl - gmax), axis=-1, keepdims=True)
        e_lo = N_GROUPS + EXPERTS_PER_GROUP * gidx
        el = jnp.where((lane >= e_lo) & (lane < e_lo + EXPERTS_PER_GROUP), logits, ninf)
        e1max, i1 = first_argmax(el)
        z = jnp.sum(jnp.exp(el - e1max), axis=-1, keepdims=True)
        el2 = jnp.where(lane == i1, ninf, el)
        e2max, i2 = first_argmax(el2)
        p1 = 1.0 / z
        p2 = jnp.exp(e2max - e1max) / z
        den = p1 + p2
        g0 = g_p * p1 / den
        g1 = g_p * p2 / den
        e0 = i1 - N_GROUPS
        e1 = i2 - N_GROUPS

        oh0 = lane == e0
        oh1 = lane == e1
        oh = jnp.where(oh0 | oh1, 1.0, 0.0)
        r = lax.broadcasted_iota(jnp.int32, (TM, TM), 0)
        c = lax.broadcasted_iota(jnp.int32, (TM, TM), 1)
        ltri = jnp.where(r > c, 1.0, 0.0).astype(BF16)
        before = jnp.dot(ltri, oh.astype(BF16), preferred_element_type=F32) + carry_ref[...]
        rank0 = jnp.sum(jnp.where(oh0, before, 0.0), axis=-1, keepdims=True)
        rank1 = jnp.sum(jnp.where(oh1, before, 0.0), axis=-1, keepdims=True)
        total = carry_ref[...] + counted * jnp.sum(oh, axis=0, keepdims=True)
        carry_ref[...] = total
        cnt_ref[...] = jnp.broadcast_to(total, cnt_ref.shape)

        mi = jnp.where(lane == 0, e0, jnp.where(lane == 1, e1, jnp.where(lane == 2, rank0, rank1)))
        mi_ref[...] = jnp.transpose(mi)[:SUBLANES, :].astype(jnp.int32)
        mf_ref[...] = jnp.where(lane == 0, g0, g1)

    @pl.when(i < n_prompt_tiles)
    def _():
        tile(xp_ref, attnp_ref)

    @pl.when(i >= n_prompt_tiles)
    def _():
        tile(xs_ref, attns_ref)


def _out_proj(attn_p, attn_s, conv_n, xp, xs, w_ob, gffn, w_r2, b_r):
    m = conv_n.shape[0]
    npt = xp.shape[0] // TM
    n_tiles = m // TM
    last_p, last_s, last = npt - 1, n_tiles - npt - 1, n_tiles - 1
    row = lambda i: (jnp.minimum(i, last), 0)
    prow = lambda i: (jnp.minimum(i, last_p), 0)
    srow = lambda i: (jnp.clip(i - npt, 0, last_s), 0)
    lag = lambda i: jnp.maximum(i - 1, 0)
    return pl.pallas_call(
        functools.partial(_out_proj_kernel, n_prompt_tiles=npt),
        grid=(n_tiles + 1,),
        in_specs=[
            pl.BlockSpec((TM, ATTN_W), prow),
            pl.BlockSpec((TM, ATTN_W), srow),
            pl.BlockSpec((TM, CONV_CH), row),
            pl.BlockSpec((TM, D_MODEL), prow),
            pl.BlockSpec((TM, D_MODEL), srow),
            _const_spec(w_ob.shape),
            _const_spec((1, D_MODEL)),
            _const_spec(w_r2.shape),
            _const_spec((1, LANES)),
        ],
        out_specs=[
            pl.BlockSpec((TM, D_MODEL), row),
            pl.BlockSpec((TM, D_MODEL // 2), row),
            pl.BlockSpec((SUBLANES, TM), lambda i: (0, lag(i))),
            pl.BlockSpec((TM, LANES), lambda i: (lag(i), 0)),
            pl.BlockSpec((SUBLANES, LANES), lambda i: (0, 0)),
        ],
        out_shape=[
            jax.ShapeDtypeStruct((m, D_MODEL), F32),
            jax.ShapeDtypeStruct((m, D_MODEL // 2), jnp.uint32),
            jax.ShapeDtypeStruct((SUBLANES, m), jnp.int32),
            jax.ShapeDtypeStruct((m, LANES), F32),
            jax.ShapeDtypeStruct((SUBLANES, LANES), F32),
        ],
        scratch_shapes=[pltpu.VMEM((1, LANES), F32), pltpu.VMEM((TM, LANES), F32)],
        compiler_params=pltpu.CompilerParams(dimension_semantics=("arbitrary",),
                                             vmem_limit_bytes=VMEM_LIMIT),
        name="out_proj",
    )(attn_p, attn_s, conv_n, xp, xs, w_ob, gffn, w_r2, b_r)


def _dispatch_kernel(d0_ref, d1_ref, zlo_ref, zn_ref, nu_ref, xpk_ref, xs_hbm, zeros_ref, sems, *, n_blocks):
    i = pl.program_id(0)
    sem = sems.at[0]
    zsem = sems.at[1]

    def zero_fill(act):
        def per_expert(e, c):
            lo = zlo_ref[e]
            n = zn_ref[e]
            head = (-lo) & (SUBLANES - 1)
            for r in range(SUBLANES - 1):
                @pl.when(r < head)
                def _(r=r):
                    act(pltpu.make_async_copy(zeros_ref.at[pl.ds(0, 1)], xs_hbm.at[pl.ds(lo + r, 1)], zsem))
            off = lo + head
            rest = n - head
            size = MOE_BLOCK // 2
            while size >= SUBLANES:
                @pl.when((rest & size) != 0)
                def _(off=off, size=size):
                    dst = xs_hbm.at[pl.ds(pl.multiple_of(off, SUBLANES), size)]
                    act(pltpu.make_async_copy(zeros_ref.at[pl.ds(0, size)], dst, zsem))
                off = off + (rest & size)
                size //= 2
            return c

        def per_block(b, c):
            dst = xs_hbm.at[pl.ds(pl.multiple_of(b * MOE_BLOCK, MOE_BLOCK), MOE_BLOCK)]
            act(pltpu.make_async_copy(zeros_ref, dst, zsem))
            return c

        lax.fori_loop(0, N_EXPERTS, per_expert, 0)
        lax.fori_loop(nu_ref[0], n_blocks, per_block, 0)

    @pl.when(i == 0)
    def _():
        zeros_ref[...] = jnp.zeros(zeros_ref.shape, zeros_ref.dtype)
        zero_fill(lambda cp: cp.start())

    @pl.when(i == pl.num_programs(0) - 1)
    def _():
        zero_fill(lambda cp: cp.wait())

    base = i * TD

    def start(g, c):
        for u in range(SUBLANES):
            r = base + g * SUBLANES + u
            src = xpk_ref.at[g, pl.ds(u, 1)]
            pltpu.make_async_copy(src, xs_hbm.at[pl.ds(d0_ref[r], 1)], sem).start()
            pltpu.make_async_copy(src, xs_hbm.at[pl.ds(d1_ref[r], 1)], sem).start()
        return c

    lax.fori_loop(0, TD // SUBLANES, start, 0)
    for _ in range(2):
        pltpu.make_async_copy(xs_hbm.at[pl.ds(0, TD)], xs_hbm.at[pl.ds(0, TD)], sem).wait()


def _dispatch(dest0, dest1, pad_lo, n_pad, n_used, xpk, n_blocks):
    m = xpk.shape[0]
    grid_spec = pltpu.PrefetchScalarGridSpec(
        num_scalar_prefetch=5,
        grid=(m // TD,),
        in_specs=[pl.BlockSpec((TD // SUBLANES, SUBLANES, D_MODEL // 2), lambda i, *_: (i, 0, 0))],
        out_specs=pl.BlockSpec(memory_space=pl.ANY),
        scratch_shapes=[pltpu.VMEM((MOE_BLOCK, D_MODEL // 2), jnp.uint32),
                        pltpu.SemaphoreType.DMA((2,))],
    )
    return pl.pallas_call(
        functools.partial(_dispatch_kernel, n_blocks=n_blocks),
        grid_spec=grid_spec,
        out_shape=jax.ShapeDtypeStruct((n_blocks * MOE_BLOCK, D_MODEL // 2), jnp.uint32),
        compiler_params=pltpu.CompilerParams(dimension_semantics=("arbitrary",)),
        name="dispatch",
    )(dest0, dest1, pad_lo, n_pad, n_used, xpk.reshape(m // SUBLANES, SUBLANES, D_MODEL // 2))


def _experts_kernel(be_ref, nu_ref, nxt_ref, x_ref, wg_hbm, wu_hbm, wd_hbm, y_ref,
                    sg_ref, su_ref, sd_ref, wgb_ref, wub_ref, wdb_ref, sems):
    b = pl.program_id(0)
    active = b < nu_ref[0]
    new_expert = jnp.logical_or(b == 0, be_ref[b] != be_ref[jnp.maximum(b - 1, 0)])

    def weight_copies(e):
        return (pltpu.make_async_copy(wg_hbm.at[e], sg_ref, sems.at[0]),
                pltpu.make_async_copy(wu_hbm.at[e], su_ref, sems.at[1]),
                pltpu.make_async_copy(wd_hbm.at[e], sd_ref, sems.at[2]))

    @pl.when(b == 0)
    def _():
        for cp in weight_copies(be_ref[0]):
            cp.start()

    @pl.when(jnp.logical_and(active, new_expert))
    def _():
        for cp in weight_copies(be_ref[b]):
            cp.wait()
        wgb_ref[...] = sg_ref[...].astype(BF16)
        wub_ref[...] = su_ref[...].astype(BF16)
        wdb_ref[...] = sd_ref[...].astype(BF16)

        @pl.when(nxt_ref[b] >= 0)
        def _():
            for cp in weight_copies(nxt_ref[b]):
                cp.start()

    @pl.when(active)
    def _():
        half = D_MODEL // 2
        xa, xb = (v.astype(BF16) for v in _unpack_bf16_pairs(x_ref[...]))
        g = jnp.dot(xa, wgb_ref[:half, :], preferred_element_type=F32)
        g = g + jnp.dot(xb, wgb_ref[half:, :], preferred_element_type=F32)
        u = jnp.dot(xa, wub_ref[:half, :], preferred_element_type=F32)
        u = u + jnp.dot(xb, wub_ref[half:, :], preferred_element_type=F32)
        hmid = (g * jax.nn.sigmoid(g)) * u
        y = jnp.dot(hmid.astype(BF16), wdb_ref[...], preferred_element_type=F32)
        y_ref[...] = _pack_bf16_pairs(y)

    @pl.when(b >= nu_ref[0])
    def _():
        y_ref[...] = jnp.zeros(y_ref.shape, y_ref.dtype)


def _experts(block_e, n_used, next_e, x_sorted, w_gate, w_up, w_down):
    p = x_sorted.shape[0]
    nb = p // MOE_BLOCK

    def xrow(b, be, nu, nxt):
        return (jnp.maximum(jnp.minimum(b, nu[0] - 1), 0), 0)

    grid_spec = pltpu.PrefetchScalarGridSpec(
        num_scalar_prefetch=3,
        grid=(nb,),
        in_specs=[
            pl.BlockSpec((MOE_BLOCK, D_MODEL // 2), xrow),
            pl.BlockSpec(memory_space=pl.ANY),
            pl.BlockSpec(memory_space=pl.ANY),
            pl.BlockSpec(memory_space=pl.ANY),
        ],
        out_specs=pl.BlockSpec((MOE_BLOCK, D_MODEL // 2), lambda b, be, nu, nxt: (b, 0)),
        scratch_shapes=[pltpu.VMEM((D_MODEL, D_FF), F32), pltpu.VMEM((D_MODEL, D_FF), F32),
                        pltpu.VMEM((D_FF, D_MODEL), F32),
                        pltpu.VMEM((D_MODEL, D_FF), BF16), pltpu.VMEM((D_MODEL, D_FF), BF16),
                        pltpu.VMEM((D_FF, D_MODEL), BF16),
                        pltpu.SemaphoreType.DMA((3,))],
    )
    return pl.pallas_call(
        _experts_kernel,
        grid_spec=grid_spec,
        out_shape=jax.ShapeDtypeStruct((p, D_MODEL // 2), jnp.uint32),
        compiler_params=pltpu.CompilerParams(dimension_semantics=("arbitrary",),
                                             vmem_limit_bytes=VMEM_LIMIT),
        name="experts",
    )(block_e, n_used, next_e, x_sorted, w_gate, w_up, w_down)


def _combine_kernel(d0_ref, d1_ref, h_ref, mf_ref, gfin_ref, y_hbm, outp_ref, outs_ref, y0_ref, y1_ref, sems,
                    *, n_tiles, n_prompt_tiles):
    i = pl.program_id(0)

    def gather(tile, slot, act):
        base = tile * TM

        def body(g, c):
            for u in range(SUBLANES):
                r = base + g * SUBLANES + u
                act(pltpu.make_async_copy(y_hbm.at[pl.ds(d0_ref[r], 1)], y0_ref.at[slot, g, pl.ds(u, 1)],
                                          sems.at[slot]))
                act(pltpu.make_async_copy(y_hbm.at[pl.ds(d1_ref[r], 1)], y1_ref.at[slot, g, pl.ds(u, 1)],
                                          sems.at[slot]))
            return c
        lax.fori_loop(0, TM // SUBLANES, body, 0)

    @pl.when(i == 0)
    def _():
        gather(0, 0, lambda cp: cp.start())

    @pl.when(i + 1 < n_tiles)
    def _():
        gather(i + 1, (i + 1) % 2, lambda cp: cp.start())

    slot = i % 2
    for _ in range(2):
        pltpu.make_async_copy(y_hbm.at[pl.ds(0, TM)], y_hbm.at[pl.ds(0, TM)], sems.at[slot]).wait()

    def finish(out_ref):
        mf = mf_ref[...]
        g0, g1 = mf[:, :, 0:1], mf[:, :, 1:2]
        half = D_MODEL // 2
        a0, b0 = _unpack_bf16_pairs(y0_ref[slot])
        a1, b1 = _unpack_bf16_pairs(y1_ref[slot])
        o_lo = h_ref[:, :, :half] + (g0 * a0 + g1 * a1)
        o_hi = h_ref[:, :, half:] + (g0 * b0 + g1 * b1)
        sumsq = jnp.sum(o_lo * o_lo, axis=-1, keepdims=True) + jnp.sum(o_hi * o_hi, axis=-1, keepdims=True)
        inv_rms = lax.rsqrt(sumsq / D_MODEL + EPS)
        out_ref[:, :, :half] = o_lo * inv_rms * gfin_ref[:, :, :half]
        out_ref[:, :, half:] = o_hi * inv_rms * gfin_ref[:, :, half:]

    @pl.when(i < n_prompt_tiles)
    def _():
        finish(outp_ref)

    @pl.when(i >= n_prompt_tiles)
    def _():
        finish(outs_ref)


def _combine(dest0, dest1, h, mf, gfin, y_sorted, *, n_prompt_rows):
    m = h.shape[0]
    npt = n_prompt_rows // TM
    tg = TM // SUBLANES
    grouped = lambda a: a.reshape(a.shape[0] // SUBLANES, SUBLANES, a.shape[1])
    grid_spec = pltpu.PrefetchScalarGridSpec(
        num_scalar_prefetch=2,
        grid=(m // TM,),
        in_specs=[
            pl.BlockSpec((tg, SUBLANES, D_MODEL), lambda i, *_: (i, 0, 0)),
            pl.BlockSpec((tg, SUBLANES, LANES), lambda i, *_: (i, 0, 0)),
            pl.BlockSpec((1, 1, D_MODEL), lambda i, *_: (0, 0, 0)),
            pl.BlockSpec(memory_space=pl.ANY),
        ],
        out_specs=[pl.BlockSpec((tg, SUBLANES, D_MODEL), lambda i, *_: (jnp.minimum(i, npt - 1), 0, 0)),
                   pl.BlockSpec((tg, SUBLANES, D_MODEL), lambda i, *_: (jnp.maximum(i - npt, 0), 0, 0))],
        scratch_shapes=[pltpu.VMEM((2, tg, SUBLANES, D_MODEL // 2), jnp.uint32),
                        pltpu.VMEM((2, tg, SUBLANES, D_MODEL // 2), jnp.uint32),
                        pltpu.SemaphoreType.DMA((2,))],
    )
    y_p, y_s = pl.pallas_call(
        functools.partial(_combine_kernel, n_tiles=m // TM, n_prompt_tiles=npt),
        grid_spec=grid_spec,
        out_shape=[jax.ShapeDtypeStruct((n_prompt_rows // SUBLANES, SUBLANES, D_MODEL), F32),
                   jax.ShapeDtypeStruct(((m - n_prompt_rows) // SUBLANES, SUBLANES, D_MODEL), F32)],
        compiler_params=pltpu.CompilerParams(dimension_semantics=("arbitrary",),
                                             vmem_limit_bytes=VMEM_LIMIT),
        name="combine",
    )(dest0, dest1, grouped(h), grouped(mf), gfin.reshape(1, 1, D_MODEL), y_sorted)
    return y_p.reshape(n_prompt_rows, D_MODEL), y_s.reshape(m - n_prompt_rows, D_MODEL)


def _rope_tables(pos):
    f32 = np.float32
    inv = np.power(f32(ROPE_THETA), -np.arange(0, ROPE_DIM, 2, dtype=f32) / f32(ROPE_DIM)).astype(f32)
    ang = (pos.astype(f32)[:, None] * inv[None, :]).astype(f32)
    cos, sin = np.cos(ang).astype(f32), np.sin(ang).astype(f32)
    return np.concatenate([cos, cos], axis=-1), np.concatenate([-sin, sin], axis=-1)


def _swap_halves(w):
    return jnp.concatenate([w[..., ROPE_DIM // 2:], w[..., :ROPE_DIM // 2]], axis=-1)


def kernel(x_prompt, x_sample, cache_kv_latent, cache_k_rope, state_conv, norm_mix, w_in, norm_q, w_uq,
           norm_kv, w_uk, w_uv, conv_w, norm_attn_out, norm_conv_out, w_o, norm_ffn, w_router_group,
           b_router_group, w_router_expert, b_router_expert, w_gate, w_up, w_down, norm_final):
    assert w_in.shape[0] == 1, "single-layer trunk"
    bp, seq_p, _ = x_prompt.shape
    bs, seq_s, _ = x_sample.shape
    past_len = cache_kv_latent.shape[2]
    np_rows, ns_rows = bp * seq_p, bs * seq_s
    m = np_rows + ns_rows
    assert seq_p % TM == 0 and TM % seq_s == 0 and ns_rows % TM == 0 and seq_s == CHUNK
    assert m % TD == 0

    xp = x_prompt.reshape(np_rows, D_MODEL)
    xs = x_sample.reshape(ns_rows, D_MODEL)
    row_vec = lambda v: v.reshape(1, -1)

    assert w_in.shape[2] == Q_LORA + KV_LORA + ROPE_DIM + 3 * CONV_CH
    w_t = jnp.swapaxes(w_in[0], 0, 1).astype(BF16)
    wq4 = w_uq[0].reshape(Q_LORA, N_HEADS, QK_NOPE + ROPE_DIM)
    wq_rope = wq4[:, :, QK_NOPE:]
    w_q = jnp.concatenate([wq4[:, :, :QK_NOPE].reshape(Q_LORA, -1), wq_rope.reshape(Q_LORA, -1),
                           _swap_halves(wq_rope).reshape(Q_LORA, -1)], axis=1).astype(BF16)
    w_ukt = jnp.transpose(w_uk[0], (1, 2, 0)).astype(BF16)
    w_uvh = jnp.transpose(w_uv[0], (1, 0, 2)).astype(BF16)
    w_ob = w_o[0].astype(BF16)
    n_router = N_GROUPS + N_EXPERTS
    w_r = jnp.concatenate([w_router_group[0], w_router_expert[0].reshape(D_MODEL, N_EXPERTS)], axis=1)
    w_r = jnp.pad(w_r, ((0, 0), (0, LANES - n_router)))
    w_rh = w_r.astype(BF16)
    w_rl = (w_r - w_rh.astype(F32)).astype(BF16)
    w_r2 = jnp.concatenate([w_rh, w_rl], axis=1)
    b_r =jnp.pad(jnp.concatenate([b_router_group[0], b_router_expert[0].reshape(N_EXPERTS)]),
                  (0, LANES - n_router)).reshape(1, LANES)

    cos_p, sin_p = _rope_tables(np.arange(seq_p))
    cos_s, sin_s = _rope_tables(past_len + np.arange(seq_s))
    cosk = np.concatenate([cos_p, np.tile(cos_s, (TM // seq_s, 1))], axis=0)
    sink = np.concatenate([sin_p, np.tile(sin_s, (TM // seq_s, 1))], axis=0)
    state = jnp.concatenate([jnp.zeros((bp, CONV_W - 1, CONV_CH), F32), state_conv[0]], axis=0)

    cqn, ckv_p, kr_p, ckv_s, kr_s, conv_n, utail = _in_proj(
        xp, xs, row_vec(norm_mix[0]), w_t, row_vec(norm_q[0]), row_vec(norm_kv[0]),
        row_vec(norm_conv_out[0]), conv_w[0], cosk, sink, state, seq_p=seq_p, seq_s=seq_s)

    gao = row_vec(norm_attn_out[0])
    attn_p = _attention(cqn, w_q, w_ukt, w_uvh, np.tile(cos_p, (1, N_HEADS)), np.tile(sin_p, (1, N_HEADS)),
                        gao, ckv_p, kr_p, n_batch=bp, seq=seq_p, row0=0)
    attn_s = _attention(cqn, w_q, w_ukt, w_uvh, np.tile(cos_s, (1, N_HEADS)), np.tile(sin_s, (1, N_HEADS)),
                        gao, ckv_s, kr_s, n_batch=bs, seq=seq_s, row0=np_rows,
                        past_kv=cache_kv_latent[0], past_kr=jnp.swapaxes(cache_k_rope[0], 1, 2))

    h, xpk, mi, mf, cnt = _out_proj(attn_p, attn_s, conv_n, xp, xs, w_ob, row_vec(norm_ffn[0]),
                                    w_r2, b_r)

    counts = cnt[0, :N_EXPERTS].astype(jnp.int32)
    padded = (counts + MOE_BLOCK - 1) // MOE_BLOCK * MOE_BLOCK
    pad_end = jnp.cumsum(padded)
    pad_start = pad_end - padded
    n_blocks = -(-(m * 2) // MOE_BLOCK) + N_EXPERTS
    block_row0 = jnp.arange(n_blocks, dtype=jnp.int32) * MOE_BLOCK
    block_e = jnp.minimum(jnp.sum((pad_end[None, :] <= block_row0[:, None]).astype(jnp.int32), axis=1),
                          N_EXPERTS - 1)
    n_used = (pad_end[-1:] // MOE_BLOCK).astype(jnp.int32)
    expert_ids = jnp.arange(N_EXPERTS, dtype=jnp.int32)[:, None]

    def seg_start(e):
        return jnp.sum(jnp.where(expert_ids == e[None, :], pad_start[:, None], 0), axis=0)

    dest0 = seg_start(mi[0]) + mi[2]
    dest1 = seg_start(mi[1]) + mi[3]

    x_sorted = _dispatch(dest0, dest1, pad_start + counts, padded - counts, n_used, xpk, n_blocks)
    later = (expert_ids.T > block_e[:, None]) & (padded > 0)[None, :]
    next_e = jnp.min(jnp.where(later, expert_ids.T, N_EXPERTS), axis=1)
    next_e = jnp.where(next_e == N_EXPERTS, -1, next_e).astype(jnp.int32)
    y_sorted = _experts(block_e, n_used, next_e, x_sorted, w_gate[0], w_up[0], w_down[0])
    gfin = row_vec(norm_final)
    y_p, y_s = _combine(dest0, dest1, h, mf, gfin, y_sorted, n_prompt_rows=np_rows)

    ut = utail.reshape(m // CHUNK, SUBLANES, CONV_CH)
    tails = ut[:, SUBLANES - (CONV_W - 1):, :]
    p_last = (jnp.arange(bp) + 1) * (seq_p // CHUNK) - 1
    s_last = np_rows // CHUNK + (jnp.arange(bs) + 1) * (seq_s // CHUNK) - 1
    return (y_p.reshape(bp, seq_p, D_MODEL),
            y_s.reshape(bs, seq_s, D_MODEL),
            ckv_p.reshape(1, bp, seq_p, KV_LORA),
            jnp.swapaxes(kr_p, 1, 2)[None],
            tails[p_last][None],
            ckv_s.reshape(1, bs, seq_s, KV_LORA),
            jnp.swapaxes(kr_s, 1, 2)[None],
            tails[s_last][None])
```

```python
import functools

import jax
import jax.numpy as jnp
import numpy as np
from jax import lax
from jax.experimental import pallas as pl
from jax.experimental.pallas import tpu as pltpu

F32 = jnp.float32
BF16 = jnp.bfloat16

D_MODEL = 2048
N_HEADS = 8
QK_NOPE = 128
ROPE_DIM = 64
V_DIM = 128
Q_LORA = 512
KV_LORA = 512
ATTN_W = N_HEADS * V_DIM
CONV_CH = D_MODEL - ATTN_W
CONV_W = 3
CHUNK = 64
N_GROUPS = 4
EXPERTS_PER_GROUP = 8
N_EXPERTS = N_GROUPS * EXPERTS_PER_GROUP
D_FF = 512
ROPE_THETA = 10000.0
EPS = 1e-6
ATTN_SCALE = (QK_NOPE + ROPE_DIM) ** -0.5
EXP2_SCALE = ATTN_SCALE * 1.4426950408889634

LANES = 128
SUBLANES = 8
TM = 256
TD = 2304
MOE_BLOCK = 256
TQ = 256
TK = 256
NEG_BIG = -1e30
V7X_VMEM_BYTES = 64 * 1024 * 1024
VMEM_LIMIT = V7X_VMEM_BYTES * 7 // 8


def _rms(v, g):
    return v * lax.rsqrt(jnp.mean(v * v, axis=-1, keepdims=True) + EPS) * g


def _lane_bcast(v, width):
    if width % LANES == 0:
        return jnp.concatenate([v] * (width // LANES), axis=1)
    assert width < LANES
    return v[:, :width]


def _pack_bf16_pairs(v):
    half = v.shape[-1] // 2
    lo = lax.bitcast_convert_type(v[..., :half].astype(BF16).astype(F32), jnp.uint32)
    hi = lax.bitcast_convert_type(v[..., half:].astype(BF16).astype(F32), jnp.uint32)
    return (lo >> 16) | (hi & jnp.uint32(0xFFFF0000))


def _unpack_bf16_pairs(w):
    return (lax.bitcast_convert_type(w << 16, F32),
            lax.bitcast_convert_type(w & jnp.uint32(0xFFFF0000), F32))


def _const_spec(shape):
    nd = len(shape)
    return pl.BlockSpec(shape, lambda *_: (0,) * nd, pipeline_mode=pl.Buffered(1))


def _in_proj_kernel(xp_ref, xs_ref, gmix_ref, wt_ref, gq_ref, gkv_ref, gco_ref, convw_ref,
                    cos_ref, sin_ref, state_ref,
                    cqn_ref, ckvp_ref, krp_ref, ckvs_ref, krs_ref, convn_ref, utail_ref, ext_ref,
                    *, n_prompt_tiles, tiles_per_seq, n_prompt_seq, sample_seq_len):
    i = pl.program_id(0)

    @pl.when(i == 0)
    def _():
        ext_ref[...] = jnp.zeros(ext_ref.shape, F32)

    def conv_block(u_sub, gate_sub, row0, length):
        ext_ref[SUBLANES:SUBLANES + length, :] = u_sub
        um1 = ext_ref[SUBLANES - 1:SUBLANES - 1 + length, :]
        um2 = ext_ref[SUBLANES - 2:SUBLANES - 2 + length, :]
        cw = convw_ref[...]
        conv = cw[0:1] * um2 + cw[1:2] * um1 + cw[2:3] * u_sub
        convn_ref[row0:row0 + length, :] = _rms(gate_sub * conv, gco_ref[...]).astype(BF16)

    def tile(x_ref, is_prompt):
        ckv_ref, krt_ref = (ckvp_ref, krp_ref) if is_prompt else (ckvs_ref, krs_ref)
        x = x_ref[...]
        xg = (x * gmix_ref[...]).astype(BF16)
        inv_rms = lax.rsqrt(jnp.mean(x * x, axis=-1, keepdims=True) + EPS)
        lat_w = Q_LORA + KV_LORA
        conv0 = lat_w + ROPE_DIM
        nt = (((1,), (1,)), ((), ()))

        def project(lo, hi):
            return inv_rms * lax.dot_general(xg, wt_ref[lo:hi, :], nt, preferred_element_type=F32)

        z_ch = project(conv0 + CONV_CH, conv0 + 3 * CONV_CH)
        u = z_ch[:, :CONV_CH] * z_ch[:, CONV_CH:]
        for j in range(TM // CHUNK):
            utail_ref[j] = u[CHUNK * (j + 1) - SUBLANES:CHUNK * (j + 1), :]
        gate_b = project(conv0, conv0 + CONV_CH)

        if is_prompt:
            first = (i % tiles_per_seq) == 0
            carried = ext_ref[TM + SUBLANES - 2:TM + SUBLANES, :]
            ext_ref[SUBLANES - 2:SUBLANES, :] = jnp.where(first, state_ref[i // tiles_per_seq], carried)
            conv_block(u, gate_b, 0, TM)
        else:
            n_sub = TM // sample_seq_len
            seq0 = n_prompt_seq + (i - n_prompt_tiles) * n_sub
            for k in range(n_sub):
                ext_ref[SUBLANES - 2:SUBLANES, :] = state_ref[seq0 + k]
                lo = k * sample_seq_len
                conv_block(u[lo:lo + sample_seq_len], gate_b[lo:lo + sample_seq_len], lo, sample_seq_len)

        zk = project(lat_w, conv0)
        zk_swapped = jnp.concatenate([zk[:, ROPE_DIM // 2:], zk[:, :ROPE_DIM // 2]], axis=1)
        k_rope = zk * cos_ref[...] + zk_swapped * sin_ref[...]
        if is_prompt:
            krt_ref[...] = k_rope.T
        else:
            for k in range(TM // sample_seq_len):
                krt_ref[k] = k_rope[k * sample_seq_len:(k + 1) * sample_seq_len, :].T
        ckv_ref[...] = _rms(project(Q_LORA, lat_w), gkv_ref[...])
        cqn_ref[...] = _rms(project(0, Q_LORA), gq_ref[...]).astype(BF16)

    @pl.when(i < n_prompt_tiles)
    def _():
        tile(xp_ref, True)

    @pl.when(i >= n_prompt_tiles)
    def _():
        tile(xs_ref, False)


def _in_proj(xp, xs, gmix, w_t, gq, gkv, gco, convw, cosk, sink, state, *, seq_p, seq_s):
    np_rows, ns_rows = xp.shape[0], xs.shape[0]
    m = np_rows + ns_rows
    npt, nst = np_rows // TM, ns_rows // TM
    tps = seq_p // TM
    n_prompt_seq = np_rows // seq_p
    last_p = npt - 1

    def tab_idx(i):
        return (jnp.where(i < npt, i % tps, tps), 0)

    row = lambda i: (i, 0)
    prow = lambda i: (jnp.minimum(i, last_p), 0)
    srow = lambda i: (jnp.maximum(i - npt, 0), 0)
    kern = functools.partial(_in_proj_kernel, n_prompt_tiles=npt, tiles_per_seq=tps,
                             n_prompt_seq=n_prompt_seq, sample_seq_len=seq_s)
    return pl.pallas_call(
        kern,
        grid=(npt + nst,),
        in_specs=[
            pl.BlockSpec((TM, D_MODEL), prow),
            pl.BlockSpec((TM, D_MODEL), srow),
            _const_spec((1, D_MODEL)),
            _const_spec(w_t.shape),
            _const_spec((1, Q_LORA)),
            _const_spec((1, KV_LORA)),
            _const_spec((1, CONV_CH)),
            _const_spec((CONV_W, CONV_CH)),
            pl.BlockSpec((TM, ROPE_DIM), tab_idx),
            pl.BlockSpec((TM, ROPE_DIM), tab_idx),
            _const_spec(state.shape),
        ],
        out_specs=[
            pl.BlockSpec((TM, Q_LORA), row),
            pl.BlockSpec((TM, KV_LORA), prow),
            pl.BlockSpec((None, ROPE_DIM, TM), lambda i: (jnp.minimum(i, last_p) // tps, 0,
                                                          jnp.minimum(i, last_p) % tps)),
            pl.BlockSpec((TM, KV_LORA), srow),
            pl.BlockSpec((TM // seq_s, ROPE_DIM, seq_s), lambda i: (jnp.maximum(i - npt, 0), 0, 0)),
            pl.BlockSpec((TM, CONV_CH), row),
            pl.BlockSpec((TM // CHUNK, SUBLANES, CONV_CH), lambda i: (i, 0, 0)),
        ],
        out_shape=[
            jax.ShapeDtypeStruct((m, Q_LORA), BF16),
            jax.ShapeDtypeStruct((np_rows, KV_LORA), F32),
            jax.ShapeDtypeStruct((n_prompt_seq, ROPE_DIM, seq_p), F32),
            jax.ShapeDtypeStruct((ns_rows, KV_LORA), F32),
            jax.ShapeDtypeStruct((ns_rows // seq_s, ROPE_DIM, seq_s), F32),
            jax.ShapeDtypeStruct((m, CONV_CH), BF16),
            jax.ShapeDtypeStruct((m // CHUNK, SUBLANES, CONV_CH), F32),
        ],
        scratch_shapes=[pltpu.VMEM((TM + SUBLANES, CONV_CH), F32)],
        compiler_params=pltpu.CompilerParams(dimension_semantics=("arbitrary",),
                                             vmem_limit_bytes=VMEM_LIMIT),
        name="in_proj",
    )(xp, xs, gmix, w_t, gq, gkv, gco, convw, cosk, sink, state)


def _attn_kernel(*refs, tq, n_past, causal):
    refs = list(refs)
    cqn_ref, wq_ref, wuk_ref, wuv_ref, cos_ref, sin_ref, gao_ref = refs[:7]
    refs = refs[7:]
    if n_past:
        pkv_ref, pkr_ref = refs[:2]
        refs = refs[2:]
    kv_ref, kr_ref, out_ref, qlat_ref, qr_ref, m_ref, l_ref, acc_ref, s_ref, klim_ref = refs

    qi = pl.program_id(1)
    rows = N_HEADS * tq

    q = jnp.dot(cqn_ref[...], wq_ref[...], preferred_element_type=F32)
    nope_w = N_HEADS * QK_NOPE
    rope_w = N_HEADS * ROPE_DIM
    qrope = q[:, nope_w:nope_w + rope_w] * cos_ref[...] + q[:, nope_w + rope_w:] * sin_ref[...]
    for h in range(N_HEADS):
        qn = q[:, h * QK_NOPE:(h + 1) * QK_NOPE].astype(BF16)
        ql = jnp.dot(qn, wuk_ref[h], preferred_element_type=F32)
        qlat_ref[h * tq:(h + 1) * tq, :] = ql.astype(BF16)
        qr_ref[h * tq:(h + 1) * tq, :] = qrope[:, h * ROPE_DIM:(h + 1) * ROPE_DIM].astype(BF16)


    nt = (((1,), (1,)), ((), ()))

    def scores(kc_f32, krt_f32):
        s = lax.dot_general(qlat_ref[...], kc_f32.astype(BF16), nt, preferred_element_type=F32)
        return s + jnp.dot(qr_ref[...], krt_f32.astype(BF16), preferred_element_type=F32)

    def update(s, kc_f32, mask, first=False):
        if mask is not None:
            s = jnp.where(mask, s, NEG_BIG)
        m_cur = jnp.max(s, axis=-1, keepdims=True)
        if first:
            m_new = jnp.broadcast_to(m_cur, m_ref.shape)
        else:
            m_prev = m_ref[...]
            m_new = jnp.maximum(m_prev, m_cur)
            alpha = jnp.exp2((m_prev - m_new) * EXP2_SCALE)
        p = jnp.exp2((s - _lane_bcast(m_new, s.shape[1])) * EXP2_SCALE)
        l_cur = jnp.sum(p, axis=-1, keepdims=True)
        pv = jnp.dot(p.astype(BF16), kc_f32.astype(BF16), preferred_element_type=F32)
        if first:
            l_ref[...] = jnp.broadcast_to(l_cur, l_ref.shape)
            acc_ref[...] = pv
        else:
            l_ref[...] = alpha * l_ref[...] + l_cur
            acc_ref[...] = _lane_bcast(alpha, KV_LORA) * acc_ref[...] + pv
        m_ref[...] = m_new

    def pipelined(kv, kr, lo, hi, last, mask_fn):
        def body(j, c):
            k0 = pl.multiple_of(j * TK, TK)
            k1 = pl.multiple_of(jnp.minimum(j + 1, last) * TK, TK)
            s_cur = s_ref[j % 2]
            s_ref[(j + 1) % 2] = scores(kv[pl.ds(k1, TK), :], kr[:, pl.ds(k1, TK)])
            update(s_cur, kv[pl.ds(k0, TK), :], None if mask_fn is None else mask_fn(k0))
            return c
        lax.fori_loop(lo, hi, body, 0)

    def pipelined_pairs(kv, kr, n_pairs, last):
        def body(i, c):
            ka = pl.multiple_of((2 * i + 1) * TK, TK)
            kb = pl.multiple_of((2 * i + 2) * TK, TK)
            kc = pl.multiple_of(jnp.minimum(2 * i + 3, last) * TK, TK)
            s_ref[0] = scores(kv[pl.ds(kb, TK), :], kr[:, pl.ds(kb, TK)])
            update(s_ref[1], kv[pl.ds(ka, TK), :], None)
            s_ref[1] = scores(kv[pl.ds(kc, TK), :], kr[:, pl.ds(kc, TK)])
            update(s_ref[0], kv[pl.ds(kb, TK), :], None)
            return c
        lax.fori_loop(0, n_pairs, body, 0)

    def first_block(kv, kr, last, mask):
        k1 = pl.multiple_of(jnp.minimum(1, last) * TK, TK)
        s_ref[0] = scores(kv[pl.ds(0, TK), :], kr[:, pl.ds(0, TK)])
        s_ref[1] = scores(kv[pl.ds(k1, TK), :], kr[:, pl.ds(k1, TK)])
        update(s_ref[0], kv[pl.ds(0, TK), :], mask, first=True)

    if n_past:
        n_pb = n_past // TK
        first_block(pkv_ref, pkr_ref, n_pb - 1, None)
        n_pairs = (n_pb - 1) // 2
        pipelined_pairs(pkv_ref, pkr_ref, n_pairs, n_pb - 1)
        if 1 + 2 * n_pairs < n_pb:
            pipelined(pkv_ref, pkr_ref, 1 + 2 * n_pairs, n_pb, n_pb - 1, None)

    if causal:
        n_blocks = ((qi + 1) * tq + TK - 1) // TK
        n_full = jnp.minimum((qi * tq // CHUNK + 1) * CHUNK // TK, n_blocks)

        assert tq & (tq - 1) == 0 and CHUNK & (CHUNK - 1) == 0
        r = lax.broadcasted_iota(jnp.int32, (rows, LANES), 0)
        q_pos = qi * tq + (r & (tq - 1))
        klim_ref[...] = (q_pos & ~(CHUNK - 1)) + CHUNK

        def mask_fn(k0):
            cidx = lax.broadcasted_iota(jnp.int32, (rows, TK), 1)
            return cidx < _lane_bcast(klim_ref[...] - k0, TK)

        first_block(kv_ref, kr_ref, n_blocks - 1, mask_fn(0))
        n_pairs = jnp.maximum(n_full - 1, 0) // 2
        pipelined_pairs(kv_ref, kr_ref, n_pairs, n_blocks - 1)
        pipelined(kv_ref, kr_ref, 1 + 2 * n_pairs, n_full, n_blocks - 1, None)
        pipelined(kv_ref, kr_ref, jnp.maximum(n_full, 1), n_blocks, n_blocks - 1, mask_fn)
    else:
        update(scores(kv_ref[...], kr_ref[...]), kv_ref[...], None)

    o = acc_ref[...] / _lane_bcast(l_ref[...], KV_LORA)
    parts = []
    for h in range(N_HEADS):
        oh = o[h * tq:(h + 1) * tq, :].astype(BF16)
        parts.append(jnp.dot(oh, wuv_ref[h], preferred_element_type=F32))
    attn = jnp.concatenate(parts, axis=-1)
    out_ref[...] = _rms(attn, gao_ref[...]).astype(BF16)


def _attention(cqn, w_q, w_ukt, w_uv, cosq, sinq, gao, ckv, krope, *, n_batch, seq, row0,
               past_kv=None, past_kr=None):
    causal = past_kv is None
    tq = TQ if causal else seq
    nq = seq // tq
    n_past = 0 if causal else past_kv.shape[1]
    if not causal:
        assert n_past % CHUNK == 0 and seq <= CHUNK and n_past % TK == 0
    blk0 = row0 // tq
    qrow = lambda b, q: (blk0 + b * nq + q, 0)
    in_specs = [
        pl.BlockSpec((tq, Q_LORA), qrow),
        _const_spec(w_q.shape),
        _const_spec(w_ukt.shape),
        _const_spec(w_uv.shape),
        pl.BlockSpec((tq, N_HEADS * ROPE_DIM), lambda b, q: (q, 0)),
        pl.BlockSpec((tq, N_HEADS * ROPE_DIM), lambda b, q: (q, 0)),
        _const_spec((1, ATTN_W)),
    ]
    args = [cqn, w_q, w_ukt, w_uv, cosq, sinq, gao]
    if n_past:
        in_specs += [pl.BlockSpec((None, n_past, KV_LORA), lambda b, q: (b, 0, 0)),
                     pl.BlockSpec((None, ROPE_DIM, n_past), lambda b, q: (b, 0, 0))]
        args += [past_kv, past_kr]
    in_specs += [pl.BlockSpec((seq, KV_LORA), lambda b, q: (b, 0)),
                 pl.BlockSpec((None, ROPE_DIM, seq), lambda b, q: (b, 0, 0))]
    args += [ckv, krope]
    rows = N_HEADS * tq
    kern = functools.partial(_attn_kernel, tq=tq, n_past=n_past, causal=causal)
    return pl.pallas_call(
        kern,
        grid=(n_batch, nq),
        in_specs=in_specs,
        out_specs=pl.BlockSpec((tq, ATTN_W), lambda b, q: (b * nq + q, 0)),
        out_shape=jax.ShapeDtypeStruct((n_batch * seq, ATTN_W), BF16),
        scratch_shapes=[
            pltpu.VMEM((rows, KV_LORA), BF16),
            pltpu.VMEM((rows, ROPE_DIM), BF16),
            pltpu.VMEM((rows, LANES), F32),
            pltpu.VMEM((rows, LANES), F32),
            pltpu.VMEM((rows, KV_LORA), F32),
            pltpu.VMEM((2, rows, TK), F32),
            pltpu.VMEM((rows, LANES), jnp.int32),
        ],
        compiler_params=pltpu.CompilerParams(dimension_semantics=("arbitrary", "arbitrary"),
                                             vmem_limit_bytes=VMEM_LIMIT),
        name="attn_prompt" if causal else "attn_sample",
    )(*args)


def _attn_heads_kernel(cqn_ref, wq_ref, wkv_ref, cos_ref, sin_ref, gao_ref, kv_ref, krt_ref, out_ref,
                       kcat_ref, vh_ref, qcat_ref, m_ref, l_ref, acc_ref, klim_ref, *, tq, seq):
    qi = pl.program_id(1)
    nt = (((1,), (1,)), ((), ()))
    kw = QK_NOPE + ROPE_DIM
    kpad = kcat_ref.shape[-1]

    @pl.when(qi == 0)
    def _():
        def expand(j, c):
            k0 = pl.multiple_of(j * TK, TK)
            latent = kv_ref[pl.ds(k0, TK), :].astype(BF16)
            kvh = jnp.dot(latent, wkv_ref[...], preferred_element_type=F32)
            k_rope = krt_ref[:, pl.ds(k0, TK)].T.astype(BF16)
            for h in range(N_HEADS):
                kcat_ref[h, pl.ds(k0, TK), :QK_NOPE] = kvh[:, h * QK_NOPE:(h + 1) * QK_NOPE].astype(BF16)
                kcat_ref[h, pl.ds(k0, TK), QK_NOPE:kw] = k_rope
                kcat_ref[h, pl.ds(k0, TK), kw:] = jnp.zeros((TK, kpad - kw), BF16)
                v0 = N_HEADS * QK_NOPE + h * V_DIM
                vh_ref[h, pl.ds(k0, TK), :] = kvh[:, v0:v0 + V_DIM].astype(BF16)
            return c
        lax.fori_loop(0, seq // TK, expand, 0)

    q = jnp.dot(cqn_ref[...], wq_ref[...], preferred_element_type=F32)
    nope_w = N_HEADS * QK_NOPE
    rope_w = N_HEADS * ROPE_DIM
    qrope = q[:, nope_w:nope_w + rope_w] * cos_ref[...] + q[:, nope_w + rope_w:] * sin_ref[...]
    for h in range(N_HEADS):
        qcat_ref[h, :, :QK_NOPE] = q[:, h * QK_NOPE:(h + 1) * QK_NOPE].astype(BF16)
        qcat_ref[h, :, QK_NOPE:kw] = qrope[:, h * ROPE_DIM:(h + 1) * ROPE_DIM].astype(BF16)
        qcat_ref[h, :, kw:] = jnp.zeros((tq, kpad - kw), BF16)

    r = lax.broadcasted_iota(jnp.int32, (tq, LANES), 0)
    klim_ref[...] = ((qi * tq + r) & ~(CHUNK - 1)) + CHUNK

    def block(k0, masked, first):
        if masked:
            cidx = lax.broadcasted_iota(jnp.int32, (tq, TK), 1)
            mask = cidx < _lane_bcast(klim_ref[...] - k0, TK)
        for h in range(N_HEADS):
            s = lax.dot_general(qcat_ref[h], kcat_ref[h, pl.ds(k0, TK), :], nt, preferred_element_type=F32)
            if masked:
                s = jnp.where(mask, s, NEG_BIG)
            m_cur = jnp.max(s, axis=-1, keepdims=True)
            if first:
                m_new = jnp.broadcast_to(m_cur, (tq, LANES))
            else:
                m_prev = m_ref[h]
                m_new = jnp.maximum(m_prev, m_cur)
                alpha = jnp.exp2((m_prev - m_new) * EXP2_SCALE)
            p = jnp.exp2((s - _lane_bcast(m_new, TK)) * EXP2_SCALE)
            l_cur = jnp.sum(p, axis=-1, keepdims=True)
            pv = jnp.dot(p.astype(BF16), vh_ref[h, pl.ds(k0, TK), :], preferred_element_type=F32)
            if first:
                l_ref[h] = jnp.broadcast_to(l_cur, (tq, LANES))
                acc_ref[h] = pv
            else:
                l_ref[h] = alpha * l_ref[h] + l_cur
                acc_ref[h] = _lane_bcast(alpha, V_DIM) * acc_ref[h] + pv
            m_ref[h] = m_new

    n_blocks = ((qi + 1) * tq + TK - 1) // TK
    n_full = jnp.minimum((qi * tq // CHUNK + 1) * CHUNK // TK, n_blocks)

    def loop(lo, hi, masked):
        def body(j, c):
            block(pl.multiple_of(j * TK, TK), masked, False)
            return c
        lax.fori_loop(lo, hi, body, 0)

    block(0, True, True)
    loop(1, n_full, False)
    loop(jnp.maximum(n_full, 1), n_blocks, True)

    attn = jnp.concatenate([acc_ref[h] / _lane_bcast(l_ref[h], V_DIM) for h in range(N_HEADS)], axis=-1)
    out_ref[...] = _rms(attn, gao_ref[...]).astype(BF16)


def _attention_heads(cqn, w_q, w_kv, cosq, sinq, gao, ckv, krope_t, *, n_batch, seq):
    nq = seq // TQ
    kpad = 2 * LANES
    assert QK_NOPE + ROPE_DIM <= kpad and V_DIM == LANES
    return pl.pallas_call(
        functools.partial(_attn_heads_kernel, tq=TQ, seq=seq),
        grid=(n_batch, nq),
        in_specs=[
            pl.BlockSpec((TQ, Q_LORA), lambda b, q: (b * nq + q, 0)),
            _const_spec(w_q.shape),
            _const_spec(w_kv.shape),
            pl.BlockSpec((TQ, N_HEADS * ROPE_DIM), lambda b, q: (q, 0)),
            pl.BlockSpec((TQ, N_HEADS * ROPE_DIM), lambda b, q: (q, 0)),
            _const_spec((1, ATTN_W)),
            pl.BlockSpec((seq, KV_LORA), lambda b, q: (b, 0)),
            pl.BlockSpec((None, ROPE_DIM, seq), lambda b, q: (b, 0, 0)),
        ],
        out_specs=pl.BlockSpec((TQ, ATTN_W), lambda b, q: (b * nq + q, 0)),
        out_shape=jax.ShapeDtypeStruct((n_batch * seq, ATTN_W), BF16),
        scratch_shapes=[
            pltpu.VMEM((N_HEADS, seq, kpad), BF16),
            pltpu.VMEM((N_HEADS, seq, V_DIM), BF16),
            pltpu.VMEM((N_HEADS, TQ, kpad), BF16),
            pltpu.VMEM((N_HEADS, TQ, LANES), F32),
            pltpu.VMEM((N_HEADS, TQ, LANES), F32),
            pltpu.VMEM((N_HEADS, TQ, V_DIM), F32),
            pltpu.VMEM((TQ, LANES), jnp.int32),
        ],
        compiler_params=pltpu.CompilerParams(dimension_semantics=("arbitrary", "arbitrary"),
                                             vmem_limit_bytes=VMEM_LIMIT),
        name="attn_prompt",
    )(cqn, w_q, w_kv, cosq, sinq, gao, ckv, krope_t)


def _out_proj_kernel(attnp_ref, attns_ref, convn_ref, xp_ref, xs_ref, wo_ref, gffn_ref, wr_ref,
                     br_ref, h_ref, xpk_ref, mi_ref, mf_ref, cnt_ref, carry_ref, logit_ref, *, n_prompt_tiles):
    i = pl.program_id(0)

    @pl.when(i == 0)
    def _():
        carry_ref[...] = jnp.zeros(carry_ref.shape, F32)
        logit_ref[...] = jnp.zeros(logit_ref.shape, F32)

    def tile(x_ref, attn_ref):
        prev_logits = logit_ref[...]
        y = jnp.dot(attn_ref[...], wo_ref[:ATTN_W, :], preferred_element_type=F32)
        y = y + jnp.dot(convn_ref[...], wo_ref[ATTN_W:, :], preferred_element_type=F32)
        h = x_ref[...] + y
        h_ref[...] = h
        xn = _rms(h, gffn_ref[...])

        half = D_MODEL // 2
        xh = xn.astype(BF16)
        xh32 = xh.astype(F32)
        lo = lax.bitcast_convert_type(xh32[:, :half], jnp.uint32)
        hi = lax.bitcast_convert_type(xh32[:, half:], jnp.uint32)
        xpk_ref[...] = (lo >> 16) | (hi & jnp.uint32(0xFFFF0000))

        xl = (xn - xh32).astype(BF16)
        hh_hl = jnp.dot(xh, wr_ref[...], preferred_element_type=F32)
        lh = jnp.dot(xl, wr_ref[:, :LANES], preferred_element_type=F32)
        logit_ref[...] = hh_hl[:, :LANES] + (lh + hh_hl[:, LANES:]) + br_ref[...]

        logits = prev_logits
        counted = (i > 0).astype(F32)
        lane = lax.broadcasted_iota(jnp.int32, (TM, LANES), 1).astype(F32)
        ninf = -jnp.inf
        far = float(LANES)

        def first_argmax(v):
            vmax = jnp.max(v, axis=-1, keepdims=True)
            return vmax, jnp.min(jnp.where(v == vmax, lane, far), axis=-1, keepdims=True)

        gl = jnp.where(lane < N_GROUPS, logits, ninf)
        gmax, gidx = first_argmax(gl)
        g_p = 1.0 / jnp.sum(jnp.exp(gl - gmax), axis=-1, keepdims=True)
        e_lo = N_GROUPS + EXPERTS_PER_GROUP * gidx
        el = jnp.where((lane >= e_lo) & (lane < e_lo + EXPERTS_PER_GROUP), logits, ninf)
        e1max, i1 = first_argmax(el)
        z = jnp.sum(jnp.exp(el - e1max), axis=-1, keepdims=True)
        el2 = jnp.where(lane == i1, ninf, el)
        e2max, i2 = first_argmax(el2)
        p1 = 1.0 / z
        p2 = jnp.exp(e2max - e1max) / z
        den = p1 + p2
        g0 = g_p * p1 / den
        g1 = g_p * p2 / den
        e0 = i1 - N_GROUPS
        e1 = i2 - N_GROUPS

        oh0 = lane == e0
        oh1 = lane == e1
        oh = jnp.where(oh0 | oh1, 1.0, 0.0)
        r = lax.broadcasted_iota(jnp.int32, (TM, TM), 0)
        c = lax.broadcasted_iota(jnp.int32, (TM, TM), 1)
        ltri = jnp.where(r > c, 1.0, 0.0).astype(BF16)
        before = jnp.dot(ltri, oh.astype(BF16), preferred_element_type=F32) + carry_ref[...]
        rank0 = jnp.sum(jnp.where(oh0, before, 0.0), axis=-1, keepdims=True)
        rank1 = jnp.sum(jnp.where(oh1, before, 0.0), axis=-1, keepdims=True)
        total = carry_ref[...] + counted * jnp.sum(oh, axis=0, keepdims=True)
        carry_ref[...] = total
        cnt_ref[...] = jnp.broadcast_to(total, cnt_ref.shape)

        mi = jnp.where(lane == 0, e0, jnp.where(lane == 1, e1, jnp.where(lane == 2, rank0, rank1)))
        mi_ref[...] = jnp.transpose(mi)[:SUBLANES, :].astype(jnp.int32)
        mf_ref[...] = jnp.where(lane == 0, g0, g1)

    @pl.when(i < n_prompt_tiles)
    def _():
        tile(xp_ref, attnp_ref)

    @pl.when(i >= n_prompt_tiles)
    def _():
        tile(xs_ref, attns_ref)


def _out_proj(attn_p, attn_s, conv_n, xp, xs, w_ob, gffn, w_r2, b_r):
    m = conv_n.shape[0]
    npt = xp.shape[0] // TM
    n_tiles = m // TM
    last_p, last_s, last = npt - 1, n_tiles - npt - 1, n_tiles - 1
    row = lambda i: (jnp.minimum(i, last), 0)
    prow = lambda i: (jnp.minimum(i, last_p), 0)
    srow = lambda i: (jnp.clip(i - npt, 0, last_s), 0)
    lag = lambda i: jnp.maximum(i - 1, 0)
    return pl.pallas_call(
        functools.partial(_out_proj_kernel, n_prompt_tiles=npt),
        grid=(n_tiles + 1,),
        in_specs=[
            pl.BlockSpec((TM, ATTN_W), prow),
            pl.BlockSpec((TM, ATTN_W), srow),
            pl.BlockSpec((TM, CONV_CH), row),
            pl.BlockSpec((TM, D_MODEL), prow),
            pl.BlockSpec((TM, D_MODEL), srow),
            _const_spec(w_ob.shape),
            _const_spec((1, D_MODEL)),
            _const_spec(w_r2.shape),
            _const_spec((1, LANES)),
        ],
        out_specs=[
            pl.BlockSpec((TM, D_MODEL), row),
            pl.BlockSpec((TM, D_MODEL // 2), row),
            pl.BlockSpec((SUBLANES, TM), lambda i: (0, lag(i))),
            pl.BlockSpec((TM, LANES), lambda i: (lag(i), 0)),
            pl.BlockSpec((SUBLANES, LANES), lambda i: (0, 0)),
        ],
        out_shape=[
            jax.ShapeDtypeStruct((m, D_MODEL), F32),
            jax.ShapeDtypeStruct((m, D_MODEL // 2), jnp.uint32),
            jax.ShapeDtypeStruct((SUBLANES, m), jnp.int32),
            jax.ShapeDtypeStruct((m, LANES), F32),
            jax.ShapeDtypeStruct((SUBLANES, LANES), F32),
        ],
        scratch_shapes=[pltpu.VMEM((1, LANES), F32), pltpu.VMEM((TM, LANES), F32)],
        compiler_params=pltpu.CompilerParams(dimension_semantics=("arbitrary",),
                                             vmem_limit_bytes=VMEM_LIMIT),
        name="out_proj",
    )(attn_p, attn_s, conv_n, xp, xs, w_ob, gffn, w_r2, b_r)


def _dispatch_kernel(d0_ref, d1_ref, zlo_ref, zn_ref, nu_ref, xpk_ref, xs_hbm, zeros_ref, sems, *, n_blocks):
    i = pl.program_id(0)
    sem = sems.at[0]
    zsem = sems.at[1]

    def zero_fill(act):
        def per_expert(e, c):
            lo = zlo_ref[e]
            n = zn_ref[e]
            head = (-lo) & (SUBLANES - 1)
            for r in range(SUBLANES - 1):
                @pl.when(r < head)
                def _(r=r):
                    act(pltpu.make_async_copy(zeros_ref.at[pl.ds(0, 1)], xs_hbm.at[pl.ds(lo + r, 1)], zsem))
            off = lo + head
            rest = n - head
            size = MOE_BLOCK // 2
            while size >= SUBLANES:
                @pl.when((rest & size) != 0)
                def _(off=off, size=size):
                    dst = xs_hbm.at[pl.ds(pl.multiple_of(off, SUBLANES), size)]
                    act(pltpu.make_async_copy(zeros_ref.at[pl.ds(0, size)], dst, zsem))
                off = off + (rest & size)
                size //= 2
            return c

        def per_block(b, c):
            dst = xs_hbm.at[pl.ds(pl.multiple_of(b * MOE_BLOCK, MOE_BLOCK), MOE_BLOCK)]
            act(pltpu.make_async_copy(zeros_ref, dst, zsem))
            return c

        lax.fori_loop(0, N_EXPERTS, per_expert, 0)
        lax.fori_loop(nu_ref[0], n_blocks, per_block, 0)

    @pl.when(i == 0)
    def _():
        zeros_ref[...] = jnp.zeros(zeros_ref.shape, zeros_ref.dtype)
        zero_fill(lambda cp: cp.start())

    @pl.when(i == pl.num_programs(0) - 1)
    def _():
        zero_fill(lambda cp: cp.wait())

    base = i * TD

    def start(g, c):
        for u in range(SUBLANES):
            r = base + g * SUBLANES + u
            src = xpk_ref.at[g, pl.ds(u, 1)]
            pltpu.make_async_copy(src, xs_hbm.at[pl.ds(d0_ref[r], 1)], sem).start()
            pltpu.make_async_copy(src, xs_hbm.at[pl.ds(d1_ref[r], 1)], sem).start()
        return c

    lax.fori_loop(0, TD // SUBLANES, start, 0)
    for _ in range(2):
        pltpu.make_async_copy(xs_hbm.at[pl.ds(0, TD)], xs_hbm.at[pl.ds(0, TD)], sem).wait()


def _dispatch(dest0, dest1, pad_lo, n_pad, n_used, xpk, n_blocks):
    m = xpk.shape[0]
    grid_spec = pltpu.PrefetchScalarGridSpec(
        num_scalar_prefetch=5,
        grid=(m // TD,),
        in_specs=[pl.BlockSpec((TD // SUBLANES, SUBLANES, D_MODEL // 2), lambda i, *_: (i, 0, 0))],
        out_specs=pl.BlockSpec(memory_space=pl.ANY),
        scratch_shapes=[pltpu.VMEM((MOE_BLOCK, D_MODEL // 2), jnp.uint32),
                        pltpu.SemaphoreType.DMA((2,))],
    )
    return pl.pallas_call(
        functools.partial(_dispatch_kernel, n_blocks=n_blocks),
        grid_spec=grid_spec,
        out_shape=jax.ShapeDtypeStruct((n_blocks * MOE_BLOCK, D_MODEL // 2), jnp.uint32),
        compiler_params=pltpu.CompilerParams(dimension_semantics=("arbitrary",)),
        name="dispatch",
    )(dest0, dest1, pad_lo, n_pad, n_used, xpk.reshape(m // SUBLANES, SUBLANES, D_MODEL // 2))


def _experts_kernel(be_ref, nu_ref, nxt_ref, x_ref, wg_hbm, wu_hbm, wd_hbm, y_ref,
                    sg_ref, su_ref, sd_ref, wgb_ref, wub_ref, wdb_ref, sems):
    b = pl.program_id(0)
    active = b < nu_ref[0]
    new_expert = jnp.logical_or(b == 0, be_ref[b] != be_ref[jnp.maximum(b - 1, 0)])

    def weight_copies(e):
        return (pltpu.make_async_copy(wg_hbm.at[e], sg_ref, sems.at[0]),
                pltpu.make_async_copy(wu_hbm.at[e], su_ref, sems.at[1]),
                pltpu.make_async_copy(wd_hbm.at[e], sd_ref, sems.at[2]))

    @pl.when(b == 0)
    def _():
        for cp in weight_copies(be_ref[0]):
            cp.start()

    @pl.when(jnp.logical_and(active, new_expert))
    def _():
        for cp in weight_copies(be_ref[b]):
            cp.wait()
        wgb_ref[...] = sg_ref[...].astype(BF16)
        wub_ref[...] = su_ref[...].astype(BF16)
        wdb_ref[...] = sd_ref[...].astype(BF16)

        @pl.when(nxt_ref[b] >= 0)
        def _():
            for cp in weight_copies(nxt_ref[b]):
                cp.start()

    @pl.when(active)
    def _():
        half = D_MODEL // 2
        xa, xb = (v.astype(BF16) for v in _unpack_bf16_pairs(x_ref[...]))
        g = jnp.dot(xa, wgb_ref[:half, :], preferred_element_type=F32)
        g = g + jnp.dot(xb, wgb_ref[half:, :], preferred_element_type=F32)
        u = jnp.dot(xa, wub_ref[:half, :], preferred_element_type=F32)
        u = u + jnp.dot(xb, wub_ref[half:, :], preferred_element_type=F32)
        hmid = (g * jax.nn.sigmoid(g)) * u
        y = jnp.dot(hmid.astype(BF16), wdb_ref[...], preferred_element_type=F32)
        y_ref[...] = _pack_bf16_pairs(y)

    @pl.when(b >= nu_ref[0])
    def _():
        y_ref[...] = jnp.zeros(y_ref.shape, y_ref.dtype)


def _experts(block_e, n_used, next_e, x_sorted, w_gate, w_up, w_down):
    p = x_sorted.shape[0]
    nb = p // MOE_BLOCK

    def xrow(b, be, nu, nxt):
        return (jnp.maximum(jnp.minimum(b, nu[0] - 1), 0), 0)

    grid_spec = pltpu.PrefetchScalarGridSpec(
        num_scalar_prefetch=3,
        grid=(nb,),
        in_specs=[
            pl.BlockSpec((MOE_BLOCK, D_MODEL // 2), xrow),
            pl.BlockSpec(memory_space=pl.ANY),
            pl.BlockSpec(memory_space=pl.ANY),
            pl.BlockSpec(memory_space=pl.ANY),
        ],
        out_specs=pl.BlockSpec((MOE_BLOCK, D_MODEL // 2), lambda b, be, nu, nxt: (b, 0)),
        scratch_shapes=[pltpu.VMEM((D_MODEL, D_FF), F32), pltpu.VMEM((D_MODEL, D_FF), F32),
                        pltpu.VMEM((D_FF, D_MODEL), F32),
                        pltpu.VMEM((D_MODEL, D_FF), BF16), pltpu.VMEM((D_MODEL, D_FF), BF16),
                        pltpu.VMEM((D_FF, D_MODEL), BF16),
                        pltpu.SemaphoreType.DMA((3,))],
    )
    return pl.pallas_call(
        _experts_kernel,
        grid_spec=grid_spec,
        out_shape=jax.ShapeDtypeStruct((p, D_MODEL // 2), jnp.uint32),
        compiler_params=pltpu.CompilerParams(dimension_semantics=("arbitrary",),
                                             vmem_limit_bytes=VMEM_LIMIT),
        name="experts",
    )(block_e, n_used, next_e, x_sorted, w_gate, w_up, w_down)


def _combine_kernel(d0_ref, d1_ref, h_ref, mf_ref, gfin_ref, y_hbm, outp_ref, outs_ref, y0_ref, y1_ref, sems,
                    *, n_tiles, n_prompt_tiles):
    i = pl.program_id(0)

    def gather(tile, slot, act):
        base = tile * TM

        def body(g, c):
            for u in range(SUBLANES):
                r = base + g * SUBLANES + u
                act(pltpu.make_async_copy(y_hbm.at[pl.ds(d0_ref[r], 1)], y0_ref.at[slot, g, pl.ds(u, 1)],
                                          sems.at[slot]))
                act(pltpu.make_async_copy(y_hbm.at[pl.ds(d1_ref[r], 1)], y1_ref.at[slot, g, pl.ds(u, 1)],
                                          sems.at[slot]))
            return c
        lax.fori_loop(0, TM // SUBLANES, body, 0)

    @pl.when(i == 0)
    def _():
        gather(0, 0, lambda cp: cp.start())

    @pl.when(i + 1 < n_tiles)
    def _():
        gather(i + 1, (i + 1) % 2, lambda cp: cp.start())

    slot = i % 2
    for _ in range(2):
        pltpu.make_async_copy(y_hbm.at[pl.ds(0, TM)], y_hbm.at[pl.ds(0, TM)], sems.at[slot]).wait()

    def finish(out_ref):
        mf = mf_ref[...]
        g0, g1 = mf[:, :, 0:1], mf[:, :, 1:2]
        half = D_MODEL // 2
        a0, b0 = _unpack_bf16_pairs(y0_ref[slot])
        a1, b1 = _unpack_bf16_pairs(y1_ref[slot])
        o_lo = h_ref[:, :, :half] + (g0 * a0 + g1 * a1)
        o_hi = h_ref[:, :, half:] + (g0 * b0 + g1 * b1)
        sumsq = jnp.sum(o_lo * o_lo, axis=-1, keepdims=True) + jnp.sum(o_hi * o_hi, axis=-1, keepdims=True)
        inv_rms = lax.rsqrt(sumsq / D_MODEL + EPS)
        out_ref[:, :, :half] = o_lo * inv_rms * gfin_ref[:, :, :half]
        out_ref[:, :, half:] = o_hi * inv_rms * gfin_ref[:, :, half:]

    @pl.when(i < n_prompt_tiles)
    def _():
        finish(outp_ref)

    @pl.when(i >= n_prompt_tiles)
    def _():
        finish(outs_ref)


def _combine(dest0, dest1, h, mf, gfin, y_sorted, *, n_prompt_rows):
    m = h.shape[0]
    npt = n_prompt_rows // TM
    tg = TM // SUBLANES
    grouped = lambda a: a.reshape(a.shape[0] // SUBLANES, SUBLANES, a.shape[1])
    grid_spec = pltpu.PrefetchScalarGridSpec(
        num_scalar_prefetch=2,
        grid=(m // TM,),
        in_specs=[
            pl.BlockSpec((tg, SUBLANES, D_MODEL), lambda i, *_: (i, 0, 0)),
            pl.BlockSpec((tg, SUBLANES, LANES), lambda i, *_: (i, 0, 0)),
            pl.BlockSpec((1, 1, D_MODEL), lambda i, *_: (0, 0, 0)),
            pl.BlockSpec(memory_space=pl.ANY),
        ],
        out_specs=[pl.BlockSpec((tg, SUBLANES, D_MODEL), lambda i, *_: (jnp.minimum(i, npt - 1), 0, 0)),
                   pl.BlockSpec((tg, SUBLANES, D_MODEL), lambda i, *_: (jnp.maximum(i - npt, 0), 0, 0))],
        scratch_shapes=[pltpu.VMEM((2, tg, SUBLANES, D_MODEL // 2), jnp.uint32),
                        pltpu.VMEM((2, tg, SUBLANES, D_MODEL // 2), jnp.uint32),
                        pltpu.SemaphoreType.DMA((2,))],
    )
    y_p, y_s = pl.pallas_call(
        functools.partial(_combine_kernel, n_tiles=m // TM, n_prompt_tiles=npt),
        grid_spec=grid_spec,
        out_shape=[jax.ShapeDtypeStruct((n_prompt_rows // SUBLANES, SUBLANES, D_MODEL), F32),
                   jax.ShapeDtypeStruct(((m - n_prompt_rows) // SUBLANES, SUBLANES, D_MODEL), F32)],
        compiler_params=pltpu.CompilerParams(dimension_semantics=("arbitrary",),
                                             vmem_limit_bytes=VMEM_LIMIT),
        name="combine",
    )(dest0, dest1, grouped(h), grouped(mf), gfin.reshape(1, 1, D_MODEL), y_sorted)
    return y_p.reshape(n_prompt_rows, D_MODEL), y_s.reshape(m - n_prompt_rows, D_MODEL)


def _rope_tables(pos):
    f32 = np.float32
    inv = np.power(f32(ROPE_THETA), -np.arange(0, ROPE_DIM, 2, dtype=f32) / f32(ROPE_DIM)).astype(f32)
    ang = (pos.astype(f32)[:, None] * inv[None, :]).astype(f32)
    cos, sin = np.cos(ang).astype(f32), np.sin(ang).astype(f32)
    return np.concatenate([cos, cos], axis=-1), np.concatenate([-sin, sin], axis=-1)


def _swap_halves(w):
    return jnp.concatenate([w[..., ROPE_DIM // 2:], w[..., :ROPE_DIM // 2]], axis=-1)


def kernel(x_prompt, x_sample, cache_kv_latent, cache_k_rope, state_conv, norm_mix, w_in, norm_q, w_uq,
           norm_kv, w_uk, w_uv, conv_w, norm_attn_out, norm_conv_out, w_o, norm_ffn, w_router_group,
           b_router_group, w_router_expert, b_router_expert, w_gate, w_up, w_down, norm_final):
    assert w_in.shape[0] == 1, "single-layer trunk"
    bp, seq_p, _ = x_prompt.shape
    bs, seq_s, _ = x_sample.shape
    past_len = cache_kv_latent.shape[2]
    np_rows, ns_rows = bp * seq_p, bs * seq_s
    m = np_rows + ns_rows
    assert seq_p % TM == 0 and TM % seq_s == 0 and ns_rows % TM == 0 and seq_s == CHUNK
    assert m % TD == 0

    xp = x_prompt.reshape(np_rows, D_MODEL)
    xs = x_sample.reshape(ns_rows, D_MODEL)
    row_vec = lambda v: v.reshape(1, -1)

    assert w_in.shape[2] == Q_LORA + KV_LORA + ROPE_DIM + 3 * CONV_CH
    w_t = jnp.swapaxes(w_in[0], 0, 1).astype(BF16)
    wq4 = w_uq[0].reshape(Q_LORA, N_HEADS, QK_NOPE + ROPE_DIM)
    wq_rope = wq4[:, :, QK_NOPE:]
    w_q = jnp.concatenate([wq4[:, :, :QK_NOPE].reshape(Q_LORA, -1), wq_rope.reshape(Q_LORA, -1),
                           _swap_halves(wq_rope).reshape(Q_LORA, -1)], axis=1).astype(BF16)
    w_ukt = jnp.transpose(w_uk[0], (1, 2, 0)).astype(BF16)
    w_uvh = jnp.transpose(w_uv[0], (1, 0, 2)).astype(BF16)
    w_ob = w_o[0].astype(BF16)
    n_router = N_GROUPS + N_EXPERTS
    w_r = jnp.concatenate([w_router_group[0], w_router_expert[0].reshape(D_MODEL, N_EXPERTS)], axis=1)
    w_r = jnp.pad(w_r, ((0, 0), (0, LANES - n_router)))
    w_rh = w_r.astype(BF16)
    w_rl = (w_r - w_rh.astype(F32)).astype(BF16)
    w_r2 = jnp.concatenate([w_rh, w_rl], axis=1)
    b_r =jnp.pad(jnp.concatenate([b_router_group[0], b_router_expert[0].reshape(N_EXPERTS)]),
                  (0, LANES - n_router)).reshape(1, LANES)

    cos_p, sin_p = _rope_tables(np.arange(seq_p))
    cos_s, sin_s = _rope_tables(past_len + np.arange(seq_s))
    cosk = np.concatenate([cos_p, np.tile(cos_s, (TM // seq_s, 1))], axis=0)
    sink = np.concatenate([sin_p, np.tile(sin_s, (TM // seq_s, 1))], axis=0)
    state = jnp.concatenate([jnp.zeros((bp, CONV_W - 1, CONV_CH), F32), state_conv[0]], axis=0)

    cqn, ckv_p, kr_p, ckv_s, kr_s, conv_n, utail = _in_proj(
        xp, xs, row_vec(norm_mix[0]), w_t, row_vec(norm_q[0]), row_vec(norm_kv[0]),
        row_vec(norm_conv_out[0]), conv_w[0], cosk, sink, state, seq_p=seq_p, seq_s=seq_s)

    gao = row_vec(norm_attn_out[0])
    w_kv = jnp.concatenate([w_uk[0].reshape(KV_LORA, N_HEADS * QK_NOPE),
                            w_uv[0].reshape(KV_LORA, N_HEADS * V_DIM)], axis=1).astype(BF16)
    attn_p = _attention_heads(cqn, w_q, w_kv, np.tile(cos_p, (1, N_HEADS)), np.tile(sin_p, (1, N_HEADS)),
                              gao, ckv_p, kr_p, n_batch=bp, seq=seq_p)
    attn_s = _attention(cqn, w_q, w_ukt, w_uvh, np.tile(cos_s, (1, N_HEADS)), np.tile(sin_s, (1, N_HEADS)),
                        gao, ckv_s, kr_s, n_batch=bs, seq=seq_s, row0=np_rows,
                        past_kv=cache_kv_latent[0], past_kr=jnp.swapaxes(cache_k_rope[0], 1, 2))

    h, xpk, mi, mf, cnt = _out_proj(attn_p, attn_s, conv_n, xp, xs, w_ob, row_vec(norm_ffn[0]),
                                    w_r2, b_r)

    counts = cnt[0, :N_EXPERTS].astype(jnp.int32)
    padded = (counts + MOE_BLOCK - 1) // MOE_BLOCK * MOE_BLOCK
    pad_end = jnp.cumsum(padded)
    pad_start = pad_end - padded
    n_blocks = -(-(m * 2) // MOE_BLOCK) + N_EXPERTS
    block_row0 = jnp.arange(n_blocks, dtype=jnp.int32) * MOE_BLOCK
    block_e = jnp.minimum(jnp.sum((pad_end[None, :] <= block_row0[:, None]).astype(jnp.int32), axis=1),
                          N_EXPERTS - 1)
    n_used = (pad_end[-1:] // MOE_BLOCK).astype(jnp.int32)
    expert_ids = jnp.arange(N_EXPERTS, dtype=jnp.int32)[:, None]

    def seg_start(e):
        return jnp.sum(jnp.where(expert_ids == e[None, :], pad_start[:, None], 0), axis=0)

    dest0 = seg_start(mi[0]) + mi[2]
    dest1 = seg_start(mi[1]) + mi[3]

    x_sorted = _dispatch(dest0, dest1, pad_start + counts, padded - counts, n_used, xpk, n_blocks)
    later = (expert_ids.T > block_e[:, None]) & (padded > 0)[None, :]
    next_e = jnp.min(jnp.where(later, expert_ids.T, N_EXPERTS), axis=1)
    next_e = jnp.where(next_e == N_EXPERTS, -1, next_e).astype(jnp.int32)
    y_sorted = _experts(block_e, n_used, next_e, x_sorted, w_gate[0], w_up[0], w_down[0])
    gfin = row_vec(norm_final)
    y_p, y_s = _combine(dest0, dest1, h, mf, gfin, y_sorted, n_prompt_rows=np_rows)

    ut = utail.reshape(m // CHUNK, SUBLANES, CONV_CH)
    tails = ut[:, SUBLANES - (CONV_W - 1):, :]
    p_last = (jnp.arange(bp) + 1) * (seq_p // CHUNK) - 1
    s_last = np_rows // CHUNK + (jnp.arange(bs) + 1) * (seq_s // CHUNK) - 1
    return (y_p.reshape(bp, seq_p, D_MODEL),
            y_s.reshape(bs, seq_s, D_MODEL),
            ckv_p.reshape(1, bp, seq_p, KV_LORA),
            jnp.swapaxes(kr_p, 1, 2)[None],
            tails[p_last][None],
            ckv_s.reshape(1, bs, seq_s, KV_LORA),
            jnp.swapaxes(kr_s, 1, 2)[None],
            tails[s_last][None])
```

```python
import functools

import jax
import jax.numpy as jnp
import numpy as np
from jax import lax
from jax.experimental import pallas as pl
from jax.experimental.pallas import tpu as pltpu

F32 = jnp.float32
BF16 = jnp.bfloat16

D_MODEL = 2048
N_HEADS = 8
QK_NOPE = 128
ROPE_DIM = 64
V_DIM = 128
Q_LORA = 512
KV_LORA = 512
ATTN_W = N_HEADS * V_DIM
CONV_CH = D_MODEL - ATTN_W
CONV_W = 3
CHUNK = 64
N_GROUPS = 4
EXPERTS_PER_GROUP = 8
N_EXPERTS = N_GROUPS * EXPERTS_PER_GROUP
D_FF = 512
ROPE_THETA = 10000.0
EPS = 1e-6
ATTN_SCALE = (QK_NOPE + ROPE_DIM) ** -0.5
EXP2_SCALE = ATTN_SCALE * 1.4426950408889634

LANES = 128
SUBLANES = 8
TM = 256
TD = 2304
MOE_BLOCK = 256
TQ = 256
TK = 256
NEG_BIG = -1e30
V7X_VMEM_BYTES = 64 * 1024 * 1024
VMEM_LIMIT = V7X_VMEM_BYTES * 7 // 8


def _rms(v, g):
    return v * lax.rsqrt(jnp.mean(v * v, axis=-1, keepdims=True) + EPS) * g


def _lane_bcast(v, width):
    if width % LANES == 0:
        return jnp.concatenate([v] * (width // LANES), axis=1)
    assert width < LANES
    return v[:, :width]


def _pack_bf16_pairs(v):
    half = v.shape[-1] // 2
    lo = lax.bitcast_convert_type(v[..., :half].astype(BF16).astype(F32), jnp.uint32)
    hi = lax.bitcast_convert_type(v[..., half:].astype(BF16).astype(F32), jnp.uint32)
    return (lo >> 16) | (hi & jnp.uint32(0xFFFF0000))


def _unpack_bf16_pairs(w):
    return (lax.bitcast_convert_type(w << 16, F32),
            lax.bitcast_convert_type(w & jnp.uint32(0xFFFF0000), F32))


def _const_spec(shape):
    nd = len(shape)
    return pl.BlockSpec(shape, lambda *_: (0,) * nd, pipeline_mode=pl.Buffered(1))


def _in_proj_kernel(xp_ref, xs_ref, gmix_ref, wt_ref, gq_ref, gkv_ref, gco_ref, convw_ref,
                    cos_ref, sin_ref, state_ref,
                    cqn_ref, ckvp_ref, krp_ref, ckvs_ref, krs_ref, convn_ref, utail_ref, ext_ref,
                    *, n_prompt_tiles, tiles_per_seq, n_prompt_seq, sample_seq_len):
    i = pl.program_id(0)

    @pl.when(i == 0)
    def _():
        ext_ref[...] = jnp.zeros(ext_ref.shape, F32)

    def conv_block(u_sub, gate_sub, row0, length):
        ext_ref[SUBLANES:SUBLANES + length, :] = u_sub
        um1 = ext_ref[SUBLANES - 1:SUBLANES - 1 + length, :]
        um2 = ext_ref[SUBLANES - 2:SUBLANES - 2 + length, :]
        cw = convw_ref[...]
        conv = cw[0:1] * um2 + cw[1:2] * um1 + cw[2:3] * u_sub
        convn_ref[row0:row0 + length, :] = _rms(gate_sub * conv, gco_ref[...]).astype(BF16)

    def tile(x_ref, is_prompt):
        ckv_ref, krt_ref = (ckvp_ref, krp_ref) if is_prompt else (ckvs_ref, krs_ref)
        x = x_ref[...]
        xg = (x * gmix_ref[...]).astype(BF16)
        inv_rms = lax.rsqrt(jnp.mean(x * x, axis=-1, keepdims=True) + EPS)
        lat_w = Q_LORA + KV_LORA
        conv0 = lat_w + ROPE_DIM
        nt = (((1,), (1,)), ((), ()))

        def project(lo, hi):
            return inv_rms * lax.dot_general(xg, wt_ref[lo:hi, :], nt, preferred_element_type=F32)

        z_ch = project(conv0 + CONV_CH, conv0 + 3 * CONV_CH)
        u = z_ch[:, :CONV_CH] * z_ch[:, CONV_CH:]
        for j in range(TM // CHUNK):
            utail_ref[j] = u[CHUNK * (j + 1) - SUBLANES:CHUNK * (j + 1), :]
        gate_b = project(conv0, conv0 + CONV_CH)

        if is_prompt:
            first = (i % tiles_per_seq) == 0
            carried = ext_ref[TM + SUBLANES - 2:TM + SUBLANES, :]
            ext_ref[SUBLANES - 2:SUBLANES, :] = jnp.where(first, state_ref[i // tiles_per_seq], carried)
            conv_block(u, gate_b, 0, TM)
        else:
            n_sub = TM // sample_seq_len
            seq0 = n_prompt_seq + (i - n_prompt_tiles) * n_sub
            for k in range(n_sub):
                ext_ref[SUBLANES - 2:SUBLANES, :] = state_ref[seq0 + k]
                lo = k * sample_seq_len
                conv_block(u[lo:lo + sample_seq_len], gate_b[lo:lo + sample_seq_len], lo, sample_seq_len)

        zk = project(lat_w, conv0)
        zk_swapped = jnp.concatenate([zk[:, ROPE_DIM // 2:], zk[:, :ROPE_DIM // 2]], axis=1)
        k_rope = zk * cos_ref[...] + zk_swapped * sin_ref[...]
        if is_prompt:
            krt_ref[...] = k_rope.T
        else:
            for k in range(TM // sample_seq_len):
                krt_ref[k] = k_rope[k * sample_seq_len:(k + 1) * sample_seq_len, :].T
        ckv_ref[...] = _rms(project(Q_LORA, lat_w), gkv_ref[...])
        cqn_ref[...] = _rms(project(0, Q_LORA), gq_ref[...]).astype(BF16)

    @pl.when(i < n_prompt_tiles)
    def _():
        tile(xp_ref, True)

    @pl.when(i >= n_prompt_tiles)
    def _():
        tile(xs_ref, False)


def _in_proj(xp, xs, gmix, w_t, gq, gkv, gco, convw, cosk, sink, state, *, seq_p, seq_s):
    np_rows, ns_rows = xp.shape[0], xs.shape[0]
    m = np_rows + ns_rows
    npt, nst = np_rows // TM, ns_rows // TM
    tps = seq_p // TM
    n_prompt_seq = np_rows // seq_p
    last_p = npt - 1

    def tab_idx(i):
        return (jnp.where(i < npt, i % tps, tps), 0)

    row = lambda i: (i, 0)
    prow = lambda i: (jnp.minimum(i, last_p), 0)
    srow = lambda i: (jnp.maximum(i - npt, 0), 0)
    kern = functools.partial(_in_proj_kernel, n_prompt_tiles=npt, tiles_per_seq=tps,
                             n_prompt_seq=n_prompt_seq, sample_seq_len=seq_s)
    return pl.pallas_call(
        kern,
        grid=(npt + nst,),
        in_specs=[
            pl.BlockSpec((TM, D_MODEL), prow),
            pl.BlockSpec((TM, D_MODEL), srow),
            _const_spec((1, D_MODEL)),
            _const_spec(w_t.shape),
            _const_spec((1, Q_LORA)),
            _const_spec((1, KV_LORA)),
            _const_spec((1, CONV_CH)),
            _const_spec((CONV_W, CONV_CH)),
            pl.BlockSpec((TM, ROPE_DIM), tab_idx),
            pl.BlockSpec((TM, ROPE_DIM), tab_idx),
            _const_spec(state.shape),
        ],
        out_specs=[
            pl.BlockSpec((TM, Q_LORA), row),
            pl.BlockSpec((TM, KV_LORA), prow),
            pl.BlockSpec((None, ROPE_DIM, TM), lambda i: (jnp.minimum(i, last_p) // tps, 0,
                                                          jnp.minimum(i, last_p) % tps)),
            pl.BlockSpec((TM, KV_LORA), srow),
            pl.BlockSpec((TM // seq_s, ROPE_DIM, seq_s), lambda i: (jnp.maximum(i - npt, 0), 0, 0)),
            pl.BlockSpec((TM, CONV_CH), row),
            pl.BlockSpec((TM // CHUNK, SUBLANES, CONV_CH), lambda i: (i, 0, 0)),
        ],
        out_shape=[
            jax.ShapeDtypeStruct((m, Q_LORA), BF16),
            jax.ShapeDtypeStruct((np_rows, KV_LORA), F32),
            jax.ShapeDtypeStruct((n_prompt_seq, ROPE_DIM, seq_p), F32),
            jax.ShapeDtypeStruct((ns_rows, KV_LORA), F32),
            jax.ShapeDtypeStruct((ns_rows // seq_s, ROPE_DIM, seq_s), F32),
            jax.ShapeDtypeStruct((m, CONV_CH), BF16),
            jax.ShapeDtypeStruct((m // CHUNK, SUBLANES, CONV_CH), F32),
        ],
        scratch_shapes=[pltpu.VMEM((TM + SUBLANES, CONV_CH), F32)],
        compiler_params=pltpu.CompilerParams(dimension_semantics=("arbitrary",),
                                             vmem_limit_bytes=VMEM_LIMIT),
        name="in_proj",
    )(xp, xs, gmix, w_t, gq, gkv, gco, convw, cosk, sink, state)


def _attn_kernel(*refs, tq, n_past, causal):
    refs = list(refs)
    cqn_ref, wq_ref, wuk_ref, wuv_ref, cos_ref, sin_ref, gao_ref = refs[:7]
    refs = refs[7:]
    if n_past:
        pkv_ref, pkr_ref = refs[:2]
        refs = refs[2:]
    kv_ref, kr_ref, out_ref, qlat_ref, qr_ref, m_ref, l_ref, acc_ref, s_ref, klim_ref = refs

    qi = pl.program_id(1)
    rows = N_HEADS * tq

    q = jnp.dot(cqn_ref[...], wq_ref[...], preferred_element_type=F32)
    nope_w = N_HEADS * QK_NOPE
    rope_w = N_HEADS * ROPE_DIM
    qrope = q[:, nope_w:nope_w + rope_w] * cos_ref[...] + q[:, nope_w + rope_w:] * sin_ref[...]
    for h in range(N_HEADS):
        qn = q[:, h * QK_NOPE:(h + 1) * QK_NOPE].astype(BF16)
        ql = jnp.dot(qn, wuk_ref[h], preferred_element_type=F32)
        qlat_ref[h * tq:(h + 1) * tq, :] = ql.astype(BF16)
        qr_ref[h * tq:(h + 1) * tq, :] = qrope[:, h * ROPE_DIM:(h + 1) * ROPE_DIM].astype(BF16)


    nt = (((1,), (1,)), ((), ()))

    def scores(kc_f32, krt_f32):
        s = lax.dot_general(qlat_ref[...], kc_f32.astype(BF16), nt, preferred_element_type=F32)
        return s + jnp.dot(qr_ref[...], krt_f32.astype(BF16), preferred_element_type=F32)

    def update(s, kc_f32, mask, first=False):
        if mask is not None:
            s = jnp.where(mask, s, NEG_BIG)
        m_cur = jnp.max(s, axis=-1, keepdims=True)
        if first:
            m_new = jnp.broadcast_to(m_cur, m_ref.shape)
        else:
            m_prev = m_ref[...]
            m_new = jnp.maximum(m_prev, m_cur)
            alpha = jnp.exp2((m_prev - m_new) * EXP2_SCALE)
        p = jnp.exp2((s - _lane_bcast(m_new, s.shape[1])) * EXP2_SCALE)
        l_cur = jnp.sum(p, axis=-1, keepdims=True)
        pv = jnp.dot(p.astype(BF16), kc_f32.astype(BF16), preferred_element_type=F32)
        if first:
            l_ref[...] = jnp.broadcast_to(l_cur, l_ref.shape)
            acc_ref[...] = pv
        else:
            l_ref[...] = alpha * l_ref[...] + l_cur
            acc_ref[...] = _lane_bcast(alpha, KV_LORA) * acc_ref[...] + pv
        m_ref[...] = m_new

    def pipelined(kv, kr, lo, hi, last, mask_fn):
        def body(j, c):
            k0 = pl.multiple_of(j * TK, TK)
            k1 = pl.multiple_of(jnp.minimum(j + 1, last) * TK, TK)
            s_cur = s_ref[j % 2]
            s_ref[(j + 1) % 2] = scores(kv[pl.ds(k1, TK), :], kr[:, pl.ds(k1, TK)])
            update(s_cur, kv[pl.ds(k0, TK), :], None if mask_fn is None else mask_fn(k0))
            return c
        lax.fori_loop(lo, hi, body, 0)

    def pipelined_pairs(kv, kr, n_pairs, last):
        def body(i, c):
            ka = pl.multiple_of((2 * i + 1) * TK, TK)
            kb = pl.multiple_of((2 * i + 2) * TK, TK)
            kc = pl.multiple_of(jnp.minimum(2 * i + 3, last) * TK, TK)
            s_ref[0] = scores(kv[pl.ds(kb, TK), :], kr[:, pl.ds(kb, TK)])
            update(s_ref[1], kv[pl.ds(ka, TK), :], None)
            s_ref[1] = scores(kv[pl.ds(kc, TK), :], kr[:, pl.ds(kc, TK)])
            update(s_ref[0], kv[pl.ds(kb, TK), :], None)
            return c
        lax.fori_loop(0, n_pairs, body, 0)

    def first_block(kv, kr, last, mask):
        k1 = pl.multiple_of(jnp.minimum(1, last) * TK, TK)
        s_ref[0] = scores(kv[pl.ds(0, TK), :], kr[:, pl.ds(0, TK)])
        s_ref[1] = scores(kv[pl.ds(k1, TK), :], kr[:, pl.ds(k1, TK)])
        update(s_ref[0], kv[pl.ds(0, TK), :], mask, first=True)

    if n_past:
        n_pb = n_past // TK
        first_block(pkv_ref, pkr_ref, n_pb - 1, None)
        n_pairs = (n_pb - 1) // 2
        pipelined_pairs(pkv_ref, pkr_ref, n_pairs, n_pb - 1)
        if 1 + 2 * n_pairs < n_pb:
            pipelined(pkv_ref, pkr_ref, 1 + 2 * n_pairs, n_pb, n_pb - 1, None)

    if causal:
        n_blocks = ((qi + 1) * tq + TK - 1) // TK
        n_full = jnp.minimum((qi * tq // CHUNK + 1) * CHUNK // TK, n_blocks)

        assert tq & (tq - 1) == 0 and CHUNK & (CHUNK - 1) == 0
        r = lax.broadcasted_iota(jnp.int32, (rows, LANES), 0)
        q_pos = qi * tq + (r & (tq - 1))
        klim_ref[...] = (q_pos & ~(CHUNK - 1)) + CHUNK

        def mask_fn(k0):
            cidx = lax.broadcasted_iota(jnp.int32, (rows, TK), 1)
            return cidx < _lane_bcast(klim_ref[...] - k0, TK)

        first_block(kv_ref, kr_ref, n_blocks - 1, mask_fn(0))
        n_pairs = jnp.maximum(n_full - 1, 0) // 2
        pipelined_pairs(kv_ref, kr_ref, n_pairs, n_blocks - 1)
        pipelined(kv_ref, kr_ref, 1 + 2 * n_pairs, n_full, n_blocks - 1, None)
        pipelined(kv_ref, kr_ref, jnp.maximum(n_full, 1), n_blocks, n_blocks - 1, mask_fn)
    else:
        update(scores(kv_ref[...], kr_ref[...]), kv_ref[...], None)

    o = acc_ref[...] / _lane_bcast(l_ref[...], KV_LORA)
    parts = []
    for h in range(N_HEADS):
        oh = o[h * tq:(h + 1) * tq, :].astype(BF16)
        parts.append(jnp.dot(oh, wuv_ref[h], preferred_element_type=F32))
    attn = jnp.concatenate(parts, axis=-1)
    out_ref[...] = _rms(attn, gao_ref[...]).astype(BF16)


def _attention(cqn, w_q, w_ukt, w_uv, cosq, sinq, gao, ckv, krope, *, n_batch, seq, row0,
               past_kv=None, past_kr=None):
    causal = past_kv is None
    tq = TQ if causal else seq
    nq = seq // tq
    n_past = 0 if causal else past_kv.shape[1]
    if not causal:
        assert n_past % CHUNK == 0 and seq <= CHUNK and n_past % TK == 0
    blk0 = row0 // tq
    qrow = lambda b, q: (blk0 + b * nq + q, 0)
    in_specs = [
        pl.BlockSpec((tq, Q_LORA), qrow),
        _const_spec(w_q.shape),
        _const_spec(w_ukt.shape),
        _const_spec(w_uv.shape),
        pl.BlockSpec((tq, N_HEADS * ROPE_DIM), lambda b, q: (q, 0)),
        pl.BlockSpec((tq, N_HEADS * ROPE_DIM), lambda b, q: (q, 0)),
        _const_spec((1, ATTN_W)),
    ]
    args = [cqn, w_q, w_ukt, w_uv, cosq, sinq, gao]
    if n_past:
        in_specs += [pl.BlockSpec((None, n_past, KV_LORA), lambda b, q: (b, 0, 0)),
                     pl.BlockSpec((None, ROPE_DIM, n_past), lambda b, q: (b, 0, 0))]
        args += [past_kv, past_kr]
    in_specs += [pl.BlockSpec((seq, KV_LORA), lambda b, q: (b, 0)),
                 pl.BlockSpec((None, ROPE_DIM, seq), lambda b, q: (b, 0, 0))]
    args += [ckv, krope]
    rows = N_HEADS * tq
    kern = functools.partial(_attn_kernel, tq=tq, n_past=n_past, causal=causal)
    return pl.pallas_call(
        kern,
        grid=(n_batch, nq),
        in_specs=in_specs,
        out_specs=pl.BlockSpec((tq, ATTN_W), lambda b, q: (b * nq + q, 0)),
        out_shape=jax.ShapeDtypeStruct((n_batch * seq, ATTN_W), BF16),
        scratch_shapes=[
            pltpu.VMEM((rows, KV_LORA), BF16),
            pltpu.VMEM((rows, ROPE_DIM), BF16),
            pltpu.VMEM((rows, LANES), F32),
            pltpu.VMEM((rows, LANES), F32),
            pltpu.VMEM((rows, KV_LORA), F32),
            pltpu.VMEM((2, rows, TK), F32),
            pltpu.VMEM((rows, LANES), jnp.int32),
        ],
        compiler_params=pltpu.CompilerParams(dimension_semantics=("arbitrary", "arbitrary"),
                                             vmem_limit_bytes=VMEM_LIMIT),
        name="attn_prompt" if causal else "attn_sample",
    )(*args)


def _attn_heads_kernel(cqn_ref, wq_ref, wkv_ref, cos_ref, sin_ref, gao_ref, kv_ref, krt_ref, out_ref,
                       kcat_ref, vh_ref, qcat_ref, m_ref, l_ref, acc_ref, klim_ref, *, tq, seq):
    qi = pl.program_id(1)
    nt = (((1,), (1,)), ((), ()))
    kw = QK_NOPE + ROPE_DIM
    kpad = kcat_ref.shape[-1]

    @pl.when(qi == 0)
    def _():
        def expand(j, c):
            k0 = pl.multiple_of(j * TK, TK)
            latent = kv_ref[pl.ds(k0, TK), :].astype(BF16)
            kvh = jnp.dot(latent, wkv_ref[...], preferred_element_type=F32)
            k_rope = krt_ref[:, pl.ds(k0, TK)].T.astype(BF16)
            for h in range(N_HEADS):
                kcat_ref[h, pl.ds(k0, TK), :QK_NOPE] = kvh[:, h * QK_NOPE:(h + 1) * QK_NOPE].astype(BF16)
                kcat_ref[h, pl.ds(k0, TK), QK_NOPE:kw] = k_rope
                kcat_ref[h, pl.ds(k0, TK), kw:] = jnp.zeros((TK, kpad - kw), BF16)
                v0 = N_HEADS * QK_NOPE + h * V_DIM
                vh_ref[h, pl.ds(k0, TK), :] = kvh[:, v0:v0 + V_DIM].astype(BF16)
            return c
        lax.fori_loop(0, seq // TK, expand, 0)

    q = jnp.dot(cqn_ref[...], wq_ref[...], preferred_element_type=F32)
    nope_w = N_HEADS * QK_NOPE
    rope_w = N_HEADS * ROPE_DIM
    qrope = q[:, nope_w:nope_w + rope_w] * cos_ref[...] + q[:, nope_w + rope_w:] * sin_ref[...]
    for h in range(N_HEADS):
        qcat_ref[h, :, :QK_NOPE] = (q[:, h * QK_NOPE:(h + 1) * QK_NOPE] * EXP2_SCALE).astype(BF16)
        qcat_ref[h, :, QK_NOPE:kw] = (qrope[:, h * ROPE_DIM:(h + 1) * ROPE_DIM] * EXP2_SCALE).astype(BF16)
        qcat_ref[h, :, kw:] = jnp.zeros((tq, kpad - kw), BF16)

    r = lax.broadcasted_iota(jnp.int32, (tq, LANES), 0)
    klim_ref[...] = ((qi * tq + r) & ~(CHUNK - 1)) + CHUNK

    def block(k0, masked, first):
        if masked:
            cidx = lax.broadcasted_iota(jnp.int32, (tq, TK), 1)
            mask = cidx < _lane_bcast(klim_ref[...] - k0, TK)
        for h in range(N_HEADS):
            s = lax.dot_general(qcat_ref[h], kcat_ref[h, pl.ds(k0, TK), :], nt, preferred_element_type=F32)
            if masked:
                s = jnp.where(mask, s, NEG_BIG)
            m_cur = jnp.max(s, axis=-1, keepdims=True)
            if first:
                m_new = jnp.broadcast_to(m_cur, (tq, LANES))
            else:
                m_prev = m_ref[h]
                m_new = jnp.maximum(m_prev, m_cur)
                alpha = jnp.exp2(m_prev - m_new)
            p = jnp.exp2(s - _lane_bcast(m_new, TK))
            l_cur = jnp.sum(p, axis=-1, keepdims=True)
            pv = jnp.dot(p.astype(BF16), vh_ref[h, pl.ds(k0, TK), :], preferred_element_type=F32)
            if first:
                l_ref[h] = jnp.broadcast_to(l_cur, (tq, LANES))
                acc_ref[h] = pv
            else:
                l_ref[h] = alpha * l_ref[h] + l_cur
                acc_ref[h] = _lane_bcast(alpha, V_DIM) * acc_ref[h] + pv
            m_ref[h] = m_new

    n_blocks = ((qi + 1) * tq + TK - 1) // TK
    n_full = jnp.minimum((qi * tq // CHUNK + 1) * CHUNK // TK, n_blocks)

    def loop(lo, hi, masked):
        def body(j, c):
            block(pl.multiple_of(j * TK, TK), masked, False)
            return c
        lax.fori_loop(lo, hi, body, 0)

    block(0, True, True)
    loop(1, n_full, False)
    loop(jnp.maximum(n_full, 1), n_blocks, True)

    attn = jnp.concatenate([acc_ref[h] / _lane_bcast(l_ref[h], V_DIM) for h in range(N_HEADS)], axis=-1)
    out_ref[...] = _rms(attn, gao_ref[...]).astype(BF16)


def _attention_heads(cqn, w_q, w_kv, cosq, sinq, gao, ckv, krope_t, *, n_batch, seq):
    nq = seq // TQ
    kpad = 2 * LANES
    assert QK_NOPE + ROPE_DIM <= kpad and V_DIM == LANES
    return pl.pallas_call(
        functools.partial(_attn_heads_kernel, tq=TQ, seq=seq),
        grid=(n_batch, nq),
        in_specs=[
            pl.BlockSpec((TQ, Q_LORA), lambda b, q: (b * nq + q, 0)),
            _const_spec(w_q.shape),
            _const_spec(w_kv.shape),
            pl.BlockSpec((TQ, N_HEADS * ROPE_DIM), lambda b, q: (q, 0)),
            pl.BlockSpec((TQ, N_HEADS * ROPE_DIM), lambda b, q: (q, 0)),
            _const_spec((1, ATTN_W)),
            pl.BlockSpec((seq, KV_LORA), lambda b, q: (b, 0)),
            pl.BlockSpec((None, ROPE_DIM, seq), lambda b, q: (b, 0, 0)),
        ],
        out_specs=pl.BlockSpec((TQ, ATTN_W), lambda b, q: (b * nq + q, 0)),
        out_shape=jax.ShapeDtypeStruct((n_batch * seq, ATTN_W), BF16),
        scratch_shapes=[
            pltpu.VMEM((N_HEADS, seq, kpad), BF16),
            pltpu.VMEM((N_HEADS, seq, V_DIM), BF16),
            pltpu.VMEM((N_HEADS, TQ, kpad), BF16),
            pltpu.VMEM((N_HEADS, TQ, LANES), F32),
            pltpu.VMEM((N_HEADS, TQ, LANES), F32),
            pltpu.VMEM((N_HEADS, TQ, V_DIM), F32),
            pltpu.VMEM((TQ, LANES), jnp.int32),
        ],
        compiler_params=pltpu.CompilerParams(dimension_semantics=("arbitrary", "arbitrary"),
                                             vmem_limit_bytes=VMEM_LIMIT),
        name="attn_prompt",
    )(cqn, w_q, w_kv, cosq, sinq, gao, ckv, krope_t)


def _out_proj_kernel(attnp_ref, attns_ref, convn_ref, xp_ref, xs_ref, wo_ref, gffn_ref, wr_ref,
                     br_ref, h_ref, xpk_ref, mi_ref, mf_ref, cnt_ref, carry_ref, logit_ref, *, n_prompt_tiles):
    i = pl.program_id(0)

    @pl.when(i == 0)
    def _():
        carry_ref[...] = jnp.zeros(carry_ref.shape, F32)
        logit_ref[...] = jnp.zeros(logit_ref.shape, F32)

    def tile(x_ref, attn_ref):
        prev_logits = logit_ref[...]
        y = jnp.dot(attn_ref[...], wo_ref[:ATTN_W, :], preferred_element_type=F32)
        y = y + jnp.dot(convn_ref[...], wo_ref[ATTN_W:, :], preferred_element_type=F32)
        h = x_ref[...] + y
        h_ref[...] = h
        xn = _rms(h, gffn_ref[...])

        half = D_MODEL // 2
        xh = xn.astype(BF16)
        xh32 = xh.astype(F32)
        lo = lax.bitcast_convert_type(xh32[:, :half], jnp.uint32)
        hi = lax.bitcast_convert_type(xh32[:, half:], jnp.uint32)
        xpk_ref[...] = (lo >> 16) | (hi & jnp.uint32(0xFFFF0000))

        xl = (xn - xh32).astype(BF16)
        hh_hl = jnp.dot(xh, wr_ref[...], preferred_element_type=F32)
        lh = jnp.dot(xl, wr_ref[:, :LANES], preferred_element_type=F32)
        logit_ref[...] = hh_hl[:, :LANES] + (lh + hh_hl[:, LANES:]) + br_ref[...]

        logits = prev_logits
        counted = (i > 0).astype(F32)
        lane = lax.broadcasted_iota(jnp.int32, (TM, LANES), 1).astype(F32)
        ninf = -jnp.inf
        far = float(LANES)

        def first_argmax(v):
            vmax = jnp.max(v, axis=-1, keepdims=True)
            return vmax, jnp.min(jnp.where(v == vmax, lane, far), axis=-1, keepdims=True)

        gl = jnp.where(lane < N_GROUPS, logits, ninf)
        gmax, gidx = first_argmax(gl)
        g_p = 1.0 / jnp.sum(jnp.exp(gl - gmax), axis=-1, keepdims=True)
        e_lo = N_GROUPS + EXPERTS_PER_GROUP * gidx
        el = jnp.where((lane >= e_lo) & (lane < e_lo + EXPERTS_PER_GROUP), logits, ninf)
        e1max, i1 = first_argmax(el)
        z = jnp.sum(jnp.exp(el - e1max), axis=-1, keepdims=True)
        el2 = jnp.where(lane == i1, ninf, el)
        e2max, i2 = first_argmax(el2)
        p1 = 1.0 / z
        p2 = jnp.exp(e2max - e1max) / z
        den = p1 + p2
        g0 = g_p * p1 / den
        g1 = g_p * p2 / den
        e0 = i1 - N_GROUPS
        e1 = i2 - N_GROUPS

        oh0 = lane == e0
        oh1 = lane == e1
        oh = jnp.where(oh0 | oh1, 1.0, 0.0)
        r = lax.broadcasted_iota(jnp.int32, (TM, TM), 0)
        c = lax.broadcasted_iota(jnp.int32, (TM, TM), 1)
        ltri = jnp.where(r > c, 1.0, 0.0).astype(BF16)
        before = jnp.dot(ltri, oh.astype(BF16), preferred_element_type=F32) + carry_ref[...]
        rank0 = jnp.sum(jnp.where(oh0, before, 0.0), axis=-1, keepdims=True)
        rank1 = jnp.sum(jnp.where(oh1, before, 0.0), axis=-1, keepdims=True)
        total = carry_ref[...] + counted * jnp.sum(oh, axis=0, keepdims=True)
        carry_ref[...] = total
        cnt_ref[...] = jnp.broadcast_to(total, cnt_ref.shape)

        mi = jnp.where(lane == 0, e0, jnp.where(lane == 1, e1, jnp.where(lane == 2, rank0, rank1)))
        mi_ref[...] = jnp.transpose(mi)[:SUBLANES, :].astype(jnp.int32)
        mf_ref[...] = jnp.where(lane == 0, g0, g1)

    @pl.when(i < n_prompt_tiles)
    def _():
        tile(xp_ref, attnp_ref)

    @pl.when(i >= n_prompt_tiles)
    def _():
        tile(xs_ref, attns_ref)


def _out_proj(attn_p, attn_s, conv_n, xp, xs, w_ob, gffn, w_r2, b_r):
    m = conv_n.shape[0]
    npt = xp.shape[0] // TM
    n_tiles = m // TM
    last_p, last_s, last = npt - 1, n_tiles - npt - 1, n_tiles - 1
    row = lambda i: (jnp.minimum(i, last), 0)
    prow = lambda i: (jnp.minimum(i, last_p), 0)
    srow = lambda i: (jnp.clip(i - npt, 0, last_s), 0)
    lag = lambda i: jnp.maximum(i - 1, 0)
    return pl.pallas_call(
        functools.partial(_out_proj_kernel, n_prompt_tiles=npt),
        grid=(n_tiles + 1,),
        in_specs=[
            pl.BlockSpec((TM, ATTN_W), prow),
            pl.BlockSpec((TM, ATTN_W), srow),
            pl.BlockSpec((TM, CONV_CH), row),
            pl.BlockSpec((TM, D_MODEL), prow),
            pl.BlockSpec((TM, D_MODEL), srow),
            _const_spec(w_ob.shape),
            _const_spec((1, D_MODEL)),
            _const_spec(w_r2.shape),
            _const_spec((1, LANES)),
        ],
        out_specs=[
            pl.BlockSpec((TM, D_MODEL), row),
            pl.BlockSpec((TM, D_MODEL // 2), row),
            pl.BlockSpec((SUBLANES, TM), lambda i: (0, lag(i))),
            pl.BlockSpec((TM, LANES), lambda i: (lag(i), 0)),
            pl.BlockSpec((SUBLANES, LANES), lambda i: (0, 0)),
        ],
        out_shape=[
            jax.ShapeDtypeStruct((m, D_MODEL), F32),
            jax.ShapeDtypeStruct((m, D_MODEL // 2), jnp.uint32),
            jax.ShapeDtypeStruct((SUBLANES, m), jnp.int32),
            jax.ShapeDtypeStruct((m, LANES), F32),
            jax.ShapeDtypeStruct((SUBLANES, LANES), F32),
        ],
        scratch_shapes=[pltpu.VMEM((1, LANES), F32), pltpu.VMEM((TM, LANES), F32)],
        compiler_params=pltpu.CompilerParams(dimension_semantics=("arbitrary",),
                                             vmem_limit_bytes=VMEM_LIMIT),
        name="out_proj",
    )(attn_p, attn_s, conv_n, xp, xs, w_ob, gffn, w_r2, b_r)


def _dispatch_kernel(d0_ref, d1_ref, zlo_ref, zn_ref, nu_ref, xpk_ref, xs_hbm, zeros_ref, sems, *, n_blocks):
    i = pl.program_id(0)
    sem = sems.at[0]
    zsem = sems.at[1]

    def zero_fill(act):
        def per_expert(e, c):
            lo = zlo_ref[e]
            n = zn_ref[e]
            head = (-lo) & (SUBLANES - 1)
            for r in range(SUBLANES - 1):
                @pl.when(r < head)
                def _(r=r):
                    act(pltpu.make_async_copy(zeros_ref.at[pl.ds(0, 1)], xs_hbm.at[pl.ds(lo + r, 1)], zsem))
            off = lo + head
            rest = n - head
            size = MOE_BLOCK // 2
            while size >= SUBLANES:
                @pl.when((rest & size) != 0)
                def _(off=off, size=size):
                    dst = xs_hbm.at[pl.ds(pl.multiple_of(off, SUBLANES), size)]
                    act(pltpu.make_async_copy(zeros_ref.at[pl.ds(0, size)], dst, zsem))
                off = off + (rest & size)
                size //= 2
            return c

        def per_block(b, c):
            dst = xs_hbm.at[pl.ds(pl.multiple_of(b * MOE_BLOCK, MOE_BLOCK), MOE_BLOCK)]
            act(pltpu.make_async_copy(zeros_ref, dst, zsem))
            return c

        lax.fori_loop(0, N_EXPERTS, per_expert, 0)
        lax.fori_loop(nu_ref[0], n_blocks, per_block, 0)

    @pl.when(i == 0)
    def _():
        zeros_ref[...] = jnp.zeros(zeros_ref.shape, zeros_ref.dtype)
        zero_fill(lambda cp: cp.start())

    @pl.when(i == pl.num_programs(0) - 1)
    def _():
        zero_fill(lambda cp: cp.wait())

    base = i * TD

    def start(g, c):
        for u in range(SUBLANES):
            r = base + g * SUBLANES + u
            src = xpk_ref.at[g, pl.ds(u, 1)]
            pltpu.make_async_copy(src, xs_hbm.at[pl.ds(d0_ref[r], 1)], sem).start()
            pltpu.make_async_copy(src, xs_hbm.at[pl.ds(d1_ref[r], 1)], sem).start()
        return c

    lax.fori_loop(0, TD // SUBLANES, start, 0)
    for _ in range(2):
        pltpu.make_async_copy(xs_hbm.at[pl.ds(0, TD)], xs_hbm.at[pl.ds(0, TD)], sem).wait()


def _dispatch(dest0, dest1, pad_lo, n_pad, n_used, xpk, n_blocks):
    m = xpk.shape[0]
    grid_spec = pltpu.PrefetchScalarGridSpec(
        num_scalar_prefetch=5,
        grid=(m // TD,),
        in_specs=[pl.BlockSpec((TD // SUBLANES, SUBLANES, D_MODEL // 2), lambda i, *_: (i, 0, 0))],
        out_specs=pl.BlockSpec(memory_space=pl.ANY),
        scratch_shapes=[pltpu.VMEM((MOE_BLOCK, D_MODEL // 2), jnp.uint32),
                        pltpu.SemaphoreType.DMA((2,))],
    )
    return pl.pallas_call(
        functools.partial(_dispatch_kernel, n_blocks=n_blocks),
        grid_spec=grid_spec,
        out_shape=jax.ShapeDtypeStruct((n_blocks * MOE_BLOCK, D_MODEL // 2), jnp.uint32),
        compiler_params=pltpu.CompilerParams(dimension_semantics=("arbitrary",)),
        name="dispatch",
    )(dest0, dest1, pad_lo, n_pad, n_used, xpk.reshape(m // SUBLANES, SUBLANES, D_MODEL // 2))


def _experts_kernel(be_ref, nu_ref, nxt_ref, x_ref, wg_hbm, wu_hbm, wd_hbm, y_ref,
                    sg_ref, su_ref, sd_ref, wgb_ref, wub_ref, wdb_ref, sems):
    b = pl.program_id(0)
    active = b < nu_ref[0]
    new_expert = jnp.logical_or(b == 0, be_ref[b] != be_ref[jnp.maximum(b - 1, 0)])

    def weight_copies(e):
        return (pltpu.make_async_copy(wg_hbm.at[e], sg_ref, sems.at[0]),
                pltpu.make_async_copy(wu_hbm.at[e], su_ref, sems.at[1]),
                pltpu.make_async_copy(wd_hbm.at[e], sd_ref, sems.at[2]))

    @pl.when(b == 0)
    def _():
        for cp in weight_copies(be_ref[0]):
            cp.start()

    @pl.when(jnp.logical_and(active, new_expert))
    def _():
        for cp in weight_copies(be_ref[b]):
            cp.wait()
        wgb_ref[...] = sg_ref[...].astype(BF16)
        wub_ref[...] = su_ref[...].astype(BF16)
        wdb_ref[...] = sd_ref[...].astype(BF16)

        @pl.when(nxt_ref[b] >= 0)
        def _():
            for cp in weight_copies(nxt_ref[b]):
                cp.start()

    @pl.when(active)
    def _():
        half = D_MODEL // 2
        xa, xb = (v.astype(BF16) for v in _unpack_bf16_pairs(x_ref[...]))
        g = jnp.dot(xa, wgb_ref[:half, :], preferred_element_type=F32)
        g = g + jnp.dot(xb, wgb_ref[half:, :], preferred_element_type=F32)
        u = jnp.dot(xa, wub_ref[:half, :], preferred_element_type=F32)
        u = u + jnp.dot(xb, wub_ref[half:, :], preferred_element_type=F32)
        hmid = (g * jax.nn.sigmoid(g)) * u
        y = jnp.dot(hmid.astype(BF16), wdb_ref[...], preferred_element_type=F32)
        y_ref[...] = _pack_bf16_pairs(y)

    @pl.when(b >= nu_ref[0])
    def _():
        y_ref[...] = jnp.zeros(y_ref.shape, y_ref.dtype)


def _experts(block_e, n_used, next_e, x_sorted, w_gate, w_up, w_down):
    p = x_sorted.shape[0]
    nb = p // MOE_BLOCK

    def xrow(b, be, nu, nxt):
        return (jnp.maximum(jnp.minimum(b, nu[0] - 1), 0), 0)

    grid_spec = pltpu.PrefetchScalarGridSpec(
        num_scalar_prefetch=3,
        grid=(nb,),
        in_specs=[
            pl.BlockSpec((MOE_BLOCK, D_MODEL // 2), xrow),
            pl.BlockSpec(memory_space=pl.ANY),
            pl.BlockSpec(memory_space=pl.ANY),
            pl.BlockSpec(memory_space=pl.ANY),
        ],
        out_specs=pl.BlockSpec((MOE_BLOCK, D_MODEL // 2), lambda b, be, nu, nxt: (b, 0)),
        scratch_shapes=[pltpu.VMEM((D_MODEL, D_FF), F32), pltpu.VMEM((D_MODEL, D_FF), F32),
                        pltpu.VMEM((D_FF, D_MODEL), F32),
                        pltpu.VMEM((D_MODEL, D_FF), BF16), pltpu.VMEM((D_MODEL, D_FF), BF16),
                        pltpu.VMEM((D_FF, D_MODEL), BF16),
                        pltpu.SemaphoreType.DMA((3,))],
    )
    return pl.pallas_call(
        _experts_kernel,
        grid_spec=grid_spec,
        out_shape=jax.ShapeDtypeStruct((p, D_MODEL // 2), jnp.uint32),
        compiler_params=pltpu.CompilerParams(dimension_semantics=("arbitrary",),
                                             vmem_limit_bytes=VMEM_LIMIT),
        name="experts",
    )(block_e, n_used, next_e, x_sorted, w_gate, w_up, w_down)


def _combine_kernel(d0_ref, d1_ref, h_ref, mf_ref, gfin_ref, y_hbm, outp_ref, outs_ref, y0_ref, y1_ref, sems,
                    *, n_tiles, n_prompt_tiles):
    i = pl.program_id(0)

    def gather(tile, slot, act):
        base = tile * TM

        def body(g, c):
            for u in range(SUBLANES):
                r = base + g * SUBLANES + u
                act(pltpu.make_async_copy(y_hbm.at[pl.ds(d0_ref[r], 1)], y0_ref.at[slot, g, pl.ds(u, 1)],
                                          sems.at[slot]))
                act(pltpu.make_async_copy(y_hbm.at[pl.ds(d1_ref[r], 1)], y1_ref.at[slot, g, pl.ds(u, 1)],
                                          sems.at[slot]))
            return c
        lax.fori_loop(0, TM // SUBLANES, body, 0)

    @pl.when(i == 0)
    def _():
        gather(0, 0, lambda cp: cp.start())

    @pl.when(i + 1 < n_tiles)
    def _():
        gather(i + 1, (i + 1) % 2, lambda cp: cp.start())

    slot = i % 2
    for _ in range(2):
        pltpu.make_async_copy(y_hbm.at[pl.ds(0, TM)], y_hbm.at[pl.ds(0, TM)], sems.at[slot]).wait()

    def finish(out_ref):
        mf = mf_ref[...]
        g0, g1 = mf[:, :, 0:1], mf[:, :, 1:2]
        half = D_MODEL // 2
        a0, b0 = _unpack_bf16_pairs(y0_ref[slot])
        a1, b1 = _unpack_bf16_pairs(y1_ref[slot])
        o_lo = h_ref[:, :, :half] + (g0 * a0 + g1 * a1)
        o_hi = h_ref[:, :, half:] + (g0 * b0 + g1 * b1)
        sumsq = jnp.sum(o_lo * o_lo, axis=-1, keepdims=True) + jnp.sum(o_hi * o_hi, axis=-1, keepdims=True)
        inv_rms = lax.rsqrt(sumsq / D_MODEL + EPS)
        out_ref[:, :, :half] = o_lo * inv_rms * gfin_ref[:, :, :half]
        out_ref[:, :, half:] = o_hi * inv_rms * gfin_ref[:, :, half:]

    @pl.when(i < n_prompt_tiles)
    def _():
        finish(outp_ref)

    @pl.when(i >= n_prompt_tiles)
    def _():
        finish(outs_ref)


def _combine(dest0, dest1, h, mf, gfin, y_sorted, *, n_prompt_rows):
    m = h.shape[0]
    npt = n_prompt_rows // TM
    tg = TM // SUBLANES
    grouped = lambda a: a.reshape(a.shape[0] // SUBLANES, SUBLANES, a.shape[1])
    grid_spec = pltpu.PrefetchScalarGridSpec(
        num_scalar_prefetch=2,
        grid=(m // TM,),
        in_specs=[
            pl.BlockSpec((tg, SUBLANES, D_MODEL), lambda i, *_: (i, 0, 0)),
            pl.BlockSpec((tg, SUBLANES, LANES), lambda i, *_: (i, 0, 0)),
            pl.BlockSpec((1, 1, D_MODEL), lambda i, *_: (0, 0, 0)),
            pl.BlockSpec(memory_space=pl.ANY),
        ],
        out_specs=[pl.BlockSpec((tg, SUBLANES, D_MODEL), lambda i, *_: (jnp.minimum(i, npt - 1), 0, 0)),
                   pl.BlockSpec((tg, SUBLANES, D_MODEL), lambda i, *_: (jnp.maximum(i - npt, 0), 0, 0))],
        scratch_shapes=[pltpu.VMEM((2, tg, SUBLANES, D_MODEL // 2), jnp.uint32),
                        pltpu.VMEM((2, tg, SUBLANES, D_MODEL // 2), jnp.uint32),
                        pltpu.SemaphoreType.DMA((2,))],
    )
    y_p, y_s = pl.pallas_call(
        functools.partial(_combine_kernel, n_tiles=m // TM, n_prompt_tiles=npt),
        grid_spec=grid_spec,
        out_shape=[jax.ShapeDtypeStruct((n_prompt_rows // SUBLANES, SUBLANES, D_MODEL), F32),
                   jax.ShapeDtypeStruct(((m - n_prompt_rows) // SUBLANES, SUBLANES, D_MODEL), F32)],
        compiler_params=pltpu.CompilerParams(dimension_semantics=("arbitrary",),
                                             vmem_limit_bytes=VMEM_LIMIT),
        name="combine",
    )(dest0, dest1, grouped(h), grouped(mf), gfin.reshape(1, 1, D_MODEL), y_sorted)
    return y_p.reshape(n_prompt_rows, D_MODEL), y_s.reshape(m - n_prompt_rows, D_MODEL)


def _rope_tables(pos):
    f32 = np.float32
    inv = np.power(f32(ROPE_THETA), -np.arange(0, ROPE_DIM, 2, dtype=f32) / f32(ROPE_DIM)).astype(f32)
    ang = (pos.astype(f32)[:, None] * inv[None, :]).astype(f32)
    cos, sin = np.cos(ang).astype(f32), np.sin(ang).astype(f32)
    return np.concatenate([cos, cos], axis=-1), np.concatenate([-sin, sin], axis=-1)


def _swap_halves(w):
    return jnp.concatenate([w[..., ROPE_DIM // 2:], w[..., :ROPE_DIM // 2]], axis=-1)


def kernel(x_prompt, x_sample, cache_kv_latent, cache_k_rope, state_conv, norm_mix, w_in, norm_q, w_uq,
           norm_kv, w_uk, w_uv, conv_w, norm_attn_out, norm_conv_out, w_o, norm_ffn, w_router_group,
           b_router_group, w_router_expert, b_router_expert, w_gate, w_up, w_down, norm_final):
    assert w_in.shape[0] == 1, "single-layer trunk"
    bp, seq_p, _ = x_prompt.shape
    bs, seq_s, _ = x_sample.shape
    past_len = cache_kv_latent.shape[2]
    np_rows, ns_rows = bp * seq_p, bs * seq_s
    m = np_rows + ns_rows
    assert seq_p % TM == 0 and TM % seq_s == 0 and ns_rows % TM == 0 and seq_s == CHUNK
    assert m % TD == 0

    xp = x_prompt.reshape(np_rows, D_MODEL)
    xs = x_sample.reshape(ns_rows, D_MODEL)
    row_vec = lambda v: v.reshape(1, -1)

    assert w_in.shape[2] == Q_LORA + KV_LORA + ROPE_DIM + 3 * CONV_CH
    w_t = jnp.swapaxes(w_in[0], 0, 1).astype(BF16)
    wq4 = w_uq[0].reshape(Q_LORA, N_HEADS, QK_NOPE + ROPE_DIM)
    wq_rope = wq4[:, :, QK_NOPE:]
    w_q = jnp.concatenate([wq4[:, :, :QK_NOPE].reshape(Q_LORA, -1), wq_rope.reshape(Q_LORA, -1),
                           _swap_halves(wq_rope).reshape(Q_LORA, -1)], axis=1).astype(BF16)
    w_ukt = jnp.transpose(w_uk[0], (1, 2, 0)).astype(BF16)
    w_uvh = jnp.transpose(w_uv[0], (1, 0, 2)).astype(BF16)
    w_ob = w_o[0].astype(BF16)
    n_router = N_GROUPS + N_EXPERTS
    w_r = jnp.concatenate([w_router_group[0], w_router_expert[0].reshape(D_MODEL, N_EXPERTS)], axis=1)
    w_r = jnp.pad(w_r, ((0, 0), (0, LANES - n_router)))
    w_rh = w_r.astype(BF16)
    w_rl = (w_r - w_rh.astype(F32)).astype(BF16)
    w_r2 = jnp.concatenate([w_rh, w_rl], axis=1)
    b_r =jnp.pad(jnp.concatenate([b_router_group[0], b_router_expert[0].reshape(N_EXPERTS)]),
                  (0, LANES - n_router)).reshape(1, LANES)

    cos_p, sin_p = _rope_tables(np.arange(seq_p))
    cos_s, sin_s = _rope_tables(past_len + np.arange(seq_s))
    cosk = np.concatenate([cos_p, np.tile(cos_s, (TM // seq_s, 1))], axis=0)
    sink = np.concatenate([sin_p, np.tile(sin_s, (TM // seq_s, 1))], axis=0)
    state = jnp.concatenate([jnp.zeros((bp, CONV_W - 1, CONV_CH), F32), state_conv[0]], axis=0)

    cqn, ckv_p, kr_p, ckv_s, kr_s, conv_n, utail = _in_proj(
        xp, xs, row_vec(norm_mix[0]), w_t, row_vec(norm_q[0]), row_vec(norm_kv[0]),
        row_vec(norm_conv_out[0]), conv_w[0], cosk, sink, state, seq_p=seq_p, seq_s=seq_s)

    gao = row_vec(norm_attn_out[0])
    w_kv = jnp.concatenate([w_uk[0].reshape(KV_LORA, N_HEADS * QK_NOPE),
                            w_uv[0].reshape(KV_LORA, N_HEADS * V_DIM)], axis=1).astype(BF16)
    attn_p = _attention_heads(cqn, w_q, w_kv, np.tile(cos_p, (1, N_HEADS)), np.tile(sin_p, (1, N_HEADS)),
                              gao, ckv_p, kr_p, n_batch=bp, seq=seq_p)
    attn_s = _attention(cqn, w_q, w_ukt, w_uvh, np.tile(cos_s, (1, N_HEADS)), np.tile(sin_s, (1, N_HEADS)),
                        gao, ckv_s, kr_s, n_batch=bs, seq=seq_s, row0=np_rows,
                        past_kv=cache_kv_latent[0], past_kr=jnp.swapaxes(cache_k_rope[0], 1, 2))

    h, xpk, mi, mf, cnt = _out_proj(attn_p, attn_s, conv_n, xp, xs, w_ob, row_vec(norm_ffn[0]),
                                    w_r2, b_r)

    counts = cnt[0, :N_EXPERTS].astype(jnp.int32)
    padded = (counts + MOE_BLOCK - 1) // MOE_BLOCK * MOE_BLOCK
    pad_end = jnp.cumsum(padded)
    pad_start = pad_end - padded
    n_blocks = -(-(m * 2) // MOE_BLOCK) + N_EXPERTS
    block_row0 = jnp.arange(n_blocks, dtype=jnp.int32) * MOE_BLOCK
    block_e = jnp.minimum(jnp.sum((pad_end[None, :] <= block_row0[:, None]).astype(jnp.int32), axis=1),
                          N_EXPERTS - 1)
    n_used = (pad_end[-1:] // MOE_BLOCK).astype(jnp.int32)
    expert_ids = jnp.arange(N_EXPERTS, dtype=jnp.int32)[:, None]

    def seg_start(e):
        return jnp.sum(jnp.where(expert_ids == e[None, :], pad_start[:, None], 0), axis=0)

    dest0 = seg_start(mi[0]) + mi[2]
    dest1 = seg_start(mi[1]) + mi[3]

    x_sorted = _dispatch(dest0, dest1, pad_start + counts, padded - counts, n_used, xpk, n_blocks)
    later = (expert_ids.T > block_e[:, None]) & (padded > 0)[None, :]
    next_e = jnp.min(jnp.where(later, expert_ids.T, N_EXPERTS), axis=1)
    next_e = jnp.where(next_e == N_EXPERTS, -1, next_e).astype(jnp.int32)
    y_sorted = _experts(block_e, n_used, next_e, x_sorted, w_gate[0], w_up[0], w_down[0])
    gfin = row_vec(norm_final)
    y_p, y_s = _combine(dest0, dest1, h, mf, gfin, y_sorted, n_prompt_rows=np_rows)

    ut = utail.reshape(m // CHUNK, SUBLANES, CONV_CH)
    tails = ut[:, SUBLANES - (CONV_W - 1):, :]
    p_last = (jnp.arange(bp) + 1) * (seq_p // CHUNK) - 1
    s_last = np_rows // CHUNK + (jnp.arange(bs) + 1) * (seq_s // CHUNK) - 1
    return (y_p.reshape(bp, seq_p, D_MODEL),
            y_s.reshape(bs, seq_s, D_MODEL),
            ckv_p.reshape(1, bp, seq_p, KV_LORA),
            jnp.swapaxes(kr_p, 1, 2)[None],
            tails[p_last][None],
            ckv_s.reshape(1, bs, seq_s, KV_LORA),
            jnp.swapaxes(kr_s, 1, 2)[None],
            tails[s_last][None])
```

```python
import functools

import jax
import jax.numpy as jnp
import numpy as np
from jax import lax
from jax.experimental import pallas as pl
from jax.experimental.pallas import tpu as pltpu

F32 = jnp.float32
BF16 = jnp.bfloat16

D_MODEL = 2048
N_HEADS = 8
QK_NOPE = 128
ROPE_DIM = 64
V_DIM = 128
Q_LORA = 512
KV_LORA = 512
ATTN_W = N_HEADS * V_DIM
CONV_CH = D_MODEL - ATTN_W
CONV_W = 3
CHUNK = 64
N_GROUPS = 4
EXPERTS_PER_GROUP = 8
N_EXPERTS = N_GROUPS * EXPERTS_PER_GROUP
D_FF = 512
ROPE_THETA = 10000.0
EPS = 1e-6
ATTN_SCALE = (QK_NOPE + ROPE_DIM) ** -0.5
EXP2_SCALE = ATTN_SCALE * 1.4426950408889634

LANES = 128
SUBLANES = 8
TM = 256
TD = 2304
MOE_BLOCK = 256
TQ = 256
TK = 256
NEG_BIG = -1e30
V7X_VMEM_BYTES = 64 * 1024 * 1024
VMEM_LIMIT = V7X_VMEM_BYTES * 7 // 8


def _rms(v, g):
    return v * lax.rsqrt(jnp.mean(v * v, axis=-1, keepdims=True) + EPS) * g


def _lane_bcast(v, width):
    if width % LANES == 0:
        return jnp.concatenate([v] * (width // LANES), axis=1)
    assert width < LANES
    return v[:, :width]


def _pack_bf16_pairs(v):
    half = v.shape[-1] // 2
    lo = lax.bitcast_convert_type(v[..., :half].astype(BF16).astype(F32), jnp.uint32)
    hi = lax.bitcast_convert_type(v[..., half:].astype(BF16).astype(F32), jnp.uint32)
    return (lo >> 16) | (hi & jnp.uint32(0xFFFF0000))


def _unpack_bf16_pairs(w):
    return (lax.bitcast_convert_type(w << 16, F32),
            lax.bitcast_convert_type(w & jnp.uint32(0xFFFF0000), F32))


def _const_spec(shape):
    nd = len(shape)
    return pl.BlockSpec(shape, lambda *_: (0,) * nd, pipeline_mode=pl.Buffered(1))


def _in_proj_kernel(xp_ref, xs_ref, gmix_ref, wt_ref, gq_ref, gkv_ref, gco_ref, convw_ref,
                    cos_ref, sin_ref, state_ref,
                    cqn_ref, ckvp_ref, krp_ref, ckvs_ref, krs_ref, convn_ref, utail_ref, ext_ref,
                    *, n_prompt_tiles, tiles_per_seq, n_prompt_seq, sample_seq_len):
    i = pl.program_id(0)

    @pl.when(i == 0)
    def _():
        ext_ref[...] = jnp.zeros(ext_ref.shape, F32)

    def conv_block(u_sub, gate_sub, row0, length):
        ext_ref[SUBLANES:SUBLANES + length, :] = u_sub
        um1 = ext_ref[SUBLANES - 1:SUBLANES - 1 + length, :]
        um2 = ext_ref[SUBLANES - 2:SUBLANES - 2 + length, :]
        cw = convw_ref[...]
        conv = cw[0:1] * um2 + cw[1:2] * um1 + cw[2:3] * u_sub
        convn_ref[row0:row0 + length, :] = _rms(gate_sub * conv, gco_ref[...]).astype(BF16)

    def tile(x_ref, is_prompt):
        ckv_ref, krt_ref = (ckvp_ref, krp_ref) if is_prompt else (ckvs_ref, krs_ref)
        x = x_ref[...]
        xg = (x * gmix_ref[...]).astype(BF16)
        inv_rms = lax.rsqrt(jnp.mean(x * x, axis=-1, keepdims=True) + EPS)
        lat_w = Q_LORA + KV_LORA
        conv0 = lat_w + ROPE_DIM
        nt = (((1,), (1,)), ((), ()))

        def project(lo, hi):
            return inv_rms * lax.dot_general(xg, wt_ref[lo:hi, :], nt, preferred_element_type=F32)

        z_ch = project(conv0 + CONV_CH, conv0 + 3 * CONV_CH)
        u = z_ch[:, :CONV_CH] * z_ch[:, CONV_CH:]
        for j in range(TM // CHUNK):
            utail_ref[j] = u[CHUNK * (j + 1) - SUBLANES:CHUNK * (j + 1), :]
        gate_b = project(conv0, conv0 + CONV_CH)

        if is_prompt:
            first = (i % tiles_per_seq) == 0
            carried = ext_ref[TM + SUBLANES - 2:TM + SUBLANES, :]
            ext_ref[SUBLANES - 2:SUBLANES, :] = jnp.where(first, state_ref[i // tiles_per_seq], carried)
            conv_block(u, gate_b, 0, TM)
        else:
            n_sub = TM // sample_seq_len
            seq0 = n_prompt_seq + (i - n_prompt_tiles) * n_sub
            for k in range(n_sub):
                ext_ref[SUBLANES - 2:SUBLANES, :] = state_ref[seq0 + k]
                lo = k * sample_seq_len
                conv_block(u[lo:lo + sample_seq_len], gate_b[lo:lo + sample_seq_len], lo, sample_seq_len)

        zk = project(lat_w, conv0)
        zk_swapped = jnp.concatenate([zk[:, ROPE_DIM // 2:], zk[:, :ROPE_DIM // 2]], axis=1)
        k_rope = zk * cos_ref[...] + zk_swapped * sin_ref[...]
        if is_prompt:
            krt_ref[...] = k_rope.T
        else:
            for k in range(TM // sample_seq_len):
                krt_ref[k] = k_rope[k * sample_seq_len:(k + 1) * sample_seq_len, :].T
        ckv_ref[...] = _rms(project(Q_LORA, lat_w), gkv_ref[...])
        cqn_ref[...] = _rms(project(0, Q_LORA), gq_ref[...]).astype(BF16)

    @pl.when(i < n_prompt_tiles)
    def _():
        tile(xp_ref, True)

    @pl.when(i >= n_prompt_tiles)
    def _():
        tile(xs_ref, False)


def _in_proj(xp, xs, gmix, w_t, gq, gkv, gco, convw, cosk, sink, state, *, seq_p, seq_s):
    np_rows, ns_rows = xp.shape[0], xs.shape[0]
    m = np_rows + ns_rows
    npt, nst = np_rows // TM, ns_rows // TM
    tps = seq_p // TM
    n_prompt_seq = np_rows // seq_p
    last_p = npt - 1

    def tab_idx(i):
        return (jnp.where(i < npt, i % tps, tps), 0)

    row = lambda i: (i, 0)
    prow = lambda i: (jnp.minimum(i, last_p), 0)
    srow = lambda i: (jnp.maximum(i - npt, 0), 0)
    kern = functools.partial(_in_proj_kernel, n_prompt_tiles=npt, tiles_per_seq=tps,
                             n_prompt_seq=n_prompt_seq, sample_seq_len=seq_s)
    return pl.pallas_call(
        kern,
        grid=(npt + nst,),
        in_specs=[
            pl.BlockSpec((TM, D_MODEL), prow),
            pl.BlockSpec((TM, D_MODEL), srow),
            _const_spec((1, D_MODEL)),
            _const_spec(w_t.shape),
            _const_spec((1, Q_LORA)),
            _const_spec((1, KV_LORA)),
            _const_spec((1, CONV_CH)),
            _const_spec((CONV_W, CONV_CH)),
            pl.BlockSpec((TM, ROPE_DIM), tab_idx),
            pl.BlockSpec((TM, ROPE_DIM), tab_idx),
            _const_spec(state.shape),
        ],
        out_specs=[
            pl.BlockSpec((TM, Q_LORA), row),
            pl.BlockSpec((TM, KV_LORA), prow),
            pl.BlockSpec((None, ROPE_DIM, TM), lambda i: (jnp.minimum(i, last_p) // tps, 0,
                                                          jnp.minimum(i, last_p) % tps)),
            pl.BlockSpec((TM, KV_LORA), srow),
            pl.BlockSpec((TM // seq_s, ROPE_DIM, seq_s), lambda i: (jnp.maximum(i - npt, 0), 0, 0)),
            pl.BlockSpec((TM, CONV_CH), row),
            pl.BlockSpec((TM // CHUNK, SUBLANES, CONV_CH), lambda i: (i, 0, 0)),
        ],
        out_shape=[
            jax.ShapeDtypeStruct((m, Q_LORA), BF16),
            jax.ShapeDtypeStruct((np_rows, KV_LORA), F32),
            jax.ShapeDtypeStruct((n_prompt_seq, ROPE_DIM, seq_p), F32),
            jax.ShapeDtypeStruct((ns_rows, KV_LORA), F32),
            jax.ShapeDtypeStruct((ns_rows // seq_s, ROPE_DIM, seq_s), F32),
            jax.ShapeDtypeStruct((m, CONV_CH), BF16),
            jax.ShapeDtypeStruct((m // CHUNK, SUBLANES, CONV_CH), F32),
        ],
        scratch_shapes=[pltpu.VMEM((TM + SUBLANES, CONV_CH), F32)],
        compiler_params=pltpu.CompilerParams(dimension_semantics=("arbitrary",),
                                             vmem_limit_bytes=VMEM_LIMIT),
        name="in_proj",
    )(xp, xs, gmix, w_t, gq, gkv, gco, convw, cosk, sink, state)


def _attn_kernel(cqn_ref, wq_ref, wuk_ref, wuv_ref, cos_ref, sin_ref, gao_ref, pkv_ref, pkr_ref, kv_ref, kr_ref,
                 out_ref, qlat_ref, qr_ref, m_ref, l_ref, acc_ref, s_ref, *, tq, n_past):
    rows = N_HEADS * tq

    q = jnp.dot(cqn_ref[...], wq_ref[...], preferred_element_type=F32)
    nope_w = N_HEADS * QK_NOPE
    rope_w = N_HEADS * ROPE_DIM
    qrope = q[:, nope_w:nope_w + rope_w] * cos_ref[...] + q[:, nope_w + rope_w:] * sin_ref[...]
    for h in range(N_HEADS):
        qn = q[:, h * QK_NOPE:(h + 1) * QK_NOPE].astype(BF16)
        ql = jnp.dot(qn, wuk_ref[h], preferred_element_type=F32)
        qlat_ref[h * tq:(h + 1) * tq, :] = ql.astype(BF16)
        qr_ref[h * tq:(h + 1) * tq, :] = qrope[:, h * ROPE_DIM:(h + 1) * ROPE_DIM].astype(BF16)


    nt = (((1,), (1,)), ((), ()))

    def scores(kc_f32, krt_f32):
        s = lax.dot_general(qlat_ref[...], kc_f32.astype(BF16), nt, preferred_element_type=F32)
        return s + jnp.dot(qr_ref[...], krt_f32.astype(BF16), preferred_element_type=F32)

    def update(s, kc_f32, mask, first=False):
        if mask is not None:
            s = jnp.where(mask, s, NEG_BIG)
        m_cur = jnp.max(s, axis=-1, keepdims=True)
        if first:
            m_new = jnp.broadcast_to(m_cur, m_ref.shape)
        else:
            m_prev = m_ref[...]
            m_new = jnp.maximum(m_prev, m_cur)
            alpha = jnp.exp2((m_prev - m_new) * EXP2_SCALE)
        p = jnp.exp2((s - _lane_bcast(m_new, s.shape[1])) * EXP2_SCALE)
        l_cur = jnp.sum(p, axis=-1, keepdims=True)
        pv = jnp.dot(p.astype(BF16), kc_f32.astype(BF16), preferred_element_type=F32)
        if first:
            l_ref[...] = jnp.broadcast_to(l_cur, l_ref.shape)
            acc_ref[...] = pv
        else:
            l_ref[...] = alpha * l_ref[...] + l_cur
            acc_ref[...] = _lane_bcast(alpha, KV_LORA) * acc_ref[...] + pv
        m_ref[...] = m_new

    def pipelined(kv, kr, lo, hi, last, mask_fn):
        def body(j, c):
            k0 = pl.multiple_of(j * TK, TK)
            k1 = pl.multiple_of(jnp.minimum(j + 1, last) * TK, TK)
            s_cur = s_ref[j % 2]
            s_ref[(j + 1) % 2] = scores(kv[pl.ds(k1, TK), :], kr[:, pl.ds(k1, TK)])
            update(s_cur, kv[pl.ds(k0, TK), :], None if mask_fn is None else mask_fn(k0))
            return c
        lax.fori_loop(lo, hi, body, 0)

    def pipelined_pairs(kv, kr, n_pairs, last):
        def body(i, c):
            ka = pl.multiple_of((2 * i + 1) * TK, TK)
            kb = pl.multiple_of((2 * i + 2) * TK, TK)
            kc = pl.multiple_of(jnp.minimum(2 * i + 3, last) * TK, TK)
            s_ref[0] = scores(kv[pl.ds(kb, TK), :], kr[:, pl.ds(kb, TK)])
            update(s_ref[1], kv[pl.ds(ka, TK), :], None)
            s_ref[1] = scores(kv[pl.ds(kc, TK), :], kr[:, pl.ds(kc, TK)])
            update(s_ref[0], kv[pl.ds(kb, TK), :], None)
            return c
        lax.fori_loop(0, n_pairs, body, 0)

    def first_block(kv, kr, last, mask):
        k1 = pl.multiple_of(jnp.minimum(1, last) * TK, TK)
        s_ref[0] = scores(kv[pl.ds(0, TK), :], kr[:, pl.ds(0, TK)])
        s_ref[1] = scores(kv[pl.ds(k1, TK), :], kr[:, pl.ds(k1, TK)])
        update(s_ref[0], kv[pl.ds(0, TK), :], mask, first=True)

    n_pb = n_past // TK
    first_block(pkv_ref, pkr_ref, n_pb - 1, None)
    n_pairs = (n_pb - 1) // 2
    pipelined_pairs(pkv_ref, pkr_ref, n_pairs, n_pb - 1)
    if 1 + 2 * n_pairs < n_pb:
        pipelined(pkv_ref, pkr_ref, 1 + 2 * n_pairs, n_pb, n_pb - 1, None)

    update(scores(kv_ref[...], kr_ref[...]), kv_ref[...], None)

    o = acc_ref[...] / _lane_bcast(l_ref[...], KV_LORA)
    parts = []
    for h in range(N_HEADS):
        oh = o[h * tq:(h + 1) * tq, :].astype(BF16)
        parts.append(jnp.dot(oh, wuv_ref[h], preferred_element_type=F32))
    attn = jnp.concatenate(parts, axis=-1)
    out_ref[...] = _rms(attn, gao_ref[...]).astype(BF16)


def _attention(cqn, w_q, w_ukt, w_uv, cosq, sinq, gao, ckv, krope, past_kv, past_kr, *, n_batch, seq, row0):
    tq = seq
    n_past = past_kv.shape[1]
    assert n_past % CHUNK == 0 and seq <= CHUNK and n_past % TK == 0
    blk0 = row0 // tq
    in_specs = [
        pl.BlockSpec((tq, Q_LORA), lambda b, q: (blk0 + b, 0)),
        _const_spec(w_q.shape),
        _const_spec(w_ukt.shape),
        _const_spec(w_uv.shape),
        pl.BlockSpec((tq, N_HEADS * ROPE_DIM), lambda b, q: (0, 0)),
        pl.BlockSpec((tq, N_HEADS * ROPE_DIM), lambda b, q: (0, 0)),
        _const_spec((1, ATTN_W)),
        pl.BlockSpec((None, n_past, KV_LORA), lambda b, q: (b, 0, 0)),
        pl.BlockSpec((None, ROPE_DIM, n_past), lambda b, q: (b, 0, 0)),
        pl.BlockSpec((seq, KV_LORA), lambda b, q: (b, 0)),
        pl.BlockSpec((None, ROPE_DIM, seq), lambda b, q: (b, 0, 0)),
    ]
    args = [cqn, w_q, w_ukt, w_uv, cosq, sinq, gao, past_kv, past_kr, ckv, krope]
    rows = N_HEADS * tq
    return pl.pallas_call(
        functools.partial(_attn_kernel, tq=tq, n_past=n_past),
        grid=(n_batch, 1),
        in_specs=in_specs,
        out_specs=pl.BlockSpec((tq, ATTN_W), lambda b, q: (b, 0)),
        out_shape=jax.ShapeDtypeStruct((n_batch * seq, ATTN_W), BF16),
        scratch_shapes=[
            pltpu.VMEM((rows, KV_LORA), BF16),
            pltpu.VMEM((rows, ROPE_DIM), BF16),
            pltpu.VMEM((rows, LANES), F32),
            pltpu.VMEM((rows, LANES), F32),
            pltpu.VMEM((rows, KV_LORA), F32),
            pltpu.VMEM((2, rows, TK), F32),
        ],
        compiler_params=pltpu.CompilerParams(dimension_semantics=("arbitrary", "arbitrary"),
                                             vmem_limit_bytes=VMEM_LIMIT),
        name="attn_sample",
    )(*args)


def _attn_heads_kernel(cqn_ref, wq_ref, wkv_ref, cos_ref, sin_ref, gao_ref, kv_ref, krt_ref, out_ref,
                       kcat_ref, vh_ref, qcat_ref, m_ref, l_ref, acc_ref, klim_ref, *, tq, seq):
    qi = pl.program_id(1)
    nt = (((1,), (1,)), ((), ()))
    kw = QK_NOPE + ROPE_DIM
    kpad = kcat_ref.shape[-1]

    @pl.when(qi == 0)
    def _():
        def expand(j, c):
            k0 = pl.multiple_of(j * TK, TK)
            latent = kv_ref[pl.ds(k0, TK), :].astype(BF16)
            kvh = jnp.dot(latent, wkv_ref[...], preferred_element_type=F32)
            k_rope = krt_ref[:, pl.ds(k0, TK)].T.astype(BF16)
            for h in range(N_HEADS):
                kcat_ref[h, pl.ds(k0, TK), :QK_NOPE] = kvh[:, h * QK_NOPE:(h + 1) * QK_NOPE].astype(BF16)
                kcat_ref[h, pl.ds(k0, TK), QK_NOPE:kw] = k_rope
                kcat_ref[h, pl.ds(k0, TK), kw:] = jnp.zeros((TK, kpad - kw), BF16)
                v0 = N_HEADS * QK_NOPE + h * V_DIM
                vh_ref[h, pl.ds(k0, TK), :] = kvh[:, v0:v0 + V_DIM].astype(BF16)
            return c
        lax.fori_loop(0, seq // TK, expand, 0)

    q = jnp.dot(cqn_ref[...], wq_ref[...], preferred_element_type=F32)
    nope_w = N_HEADS * QK_NOPE
    rope_w = N_HEADS * ROPE_DIM
    qrope = q[:, nope_w:nope_w + rope_w] * cos_ref[...] + q[:, nope_w + rope_w:] * sin_ref[...]
    for h in range(N_HEADS):
        qcat_ref[h, :, :QK_NOPE] = (q[:, h * QK_NOPE:(h + 1) * QK_NOPE] * EXP2_SCALE).astype(BF16)
        qcat_ref[h, :, QK_NOPE:kw] = (qrope[:, h * ROPE_DIM:(h + 1) * ROPE_DIM] * EXP2_SCALE).astype(BF16)
        qcat_ref[h, :, kw:] = jnp.zeros((tq, kpad - kw), BF16)

    r = lax.broadcasted_iota(jnp.int32, (tq, LANES), 0)
    klim_ref[...] = ((qi * tq + r) & ~(CHUNK - 1)) + CHUNK

    def block(k0, masked, first):
        if masked:
            cidx = lax.broadcasted_iota(jnp.int32, (tq, TK), 1)
            mask = cidx < _lane_bcast(klim_ref[...] - k0, TK)
        for h in range(N_HEADS):
            s = lax.dot_general(qcat_ref[h], kcat_ref[h, pl.ds(k0, TK), :], nt, preferred_element_type=F32)
            if masked:
                s = jnp.where(mask, s, NEG_BIG)
            m_cur = jnp.max(s, axis=-1, keepdims=True)
            if first:
                m_new = jnp.broadcast_to(m_cur, (tq, LANES))
            else:
                m_prev = m_ref[h]
                m_new = jnp.maximum(m_prev, m_cur)
                alpha = jnp.exp2(m_prev - m_new)
            p = jnp.exp2(s - _lane_bcast(m_new, TK))
            l_cur = jnp.sum(p, axis=-1, keepdims=True)
            pv = jnp.dot(p.astype(BF16), vh_ref[h, pl.ds(k0, TK), :], preferred_element_type=F32)
            if first:
                l_ref[h] = jnp.broadcast_to(l_cur, (tq, LANES))
                acc_ref[h] = pv
            else:
                l_ref[h] = alpha * l_ref[h] + l_cur
                acc_ref[h] = _lane_bcast(alpha, V_DIM) * acc_ref[h] + pv
            m_ref[h] = m_new

    n_blocks = ((qi + 1) * tq + TK - 1) // TK
    n_full = jnp.minimum((qi * tq // CHUNK + 1) * CHUNK // TK, n_blocks)

    def loop(lo, hi, masked):
        def body(j, c):
            block(pl.multiple_of(j * TK, TK), masked, False)
            return c
        lax.fori_loop(lo, hi, body, 0)

    block(0, True, True)
    loop(1, n_full, False)
    loop(jnp.maximum(n_full, 1), n_blocks, True)

    attn = jnp.concatenate([acc_ref[h] / _lane_bcast(l_ref[h], V_DIM) for h in range(N_HEADS)], axis=-1)
    out_ref[...] = _rms(attn, gao_ref[...]).astype(BF16)


def _attention_heads(cqn, w_q, w_kv, cosq, sinq, gao, ckv, krope_t, *, n_batch, seq):
    nq = seq // TQ
    kpad = 2 * LANES
    assert QK_NOPE + ROPE_DIM <= kpad and V_DIM == LANES
    return pl.pallas_call(
        functools.partial(_attn_heads_kernel, tq=TQ, seq=seq),
        grid=(n_batch, nq),
        in_specs=[
            pl.BlockSpec((TQ, Q_LORA), lambda b, q: (b * nq + q, 0)),
            _const_spec(w_q.shape),
            _const_spec(w_kv.shape),
            pl.BlockSpec((TQ, N_HEADS * ROPE_DIM), lambda b, q: (q, 0)),
            pl.BlockSpec((TQ, N_HEADS * ROPE_DIM), lambda b, q: (q, 0)),
            _const_spec((1, ATTN_W)),
            pl.BlockSpec((seq, KV_LORA), lambda b, q: (b, 0)),
            pl.BlockSpec((None, ROPE_DIM, seq), lambda b, q: (b, 0, 0)),
        ],
        out_specs=pl.BlockSpec((TQ, ATTN_W), lambda b, q: (b * nq + q, 0)),
        out_shape=jax.ShapeDtypeStruct((n_batch * seq, ATTN_W), BF16),
        scratch_shapes=[
            pltpu.VMEM((N_HEADS, seq, kpad), BF16),
            pltpu.VMEM((N_HEADS, seq, V_DIM), BF16),
            pltpu.VMEM((N_HEADS, TQ, kpad), BF16),
            pltpu.VMEM((N_HEADS, TQ, LANES), F32),
            pltpu.VMEM((N_HEADS, TQ, LANES), F32),
            pltpu.VMEM((N_HEADS, TQ, V_DIM), F32),
            pltpu.VMEM((TQ, LANES), jnp.int32),
        ],
        compiler_params=pltpu.CompilerParams(dimension_semantics=("arbitrary", "arbitrary"),
                                             vmem_limit_bytes=VMEM_LIMIT),
        name="attn_prompt",
    )(cqn, w_q, w_kv, cosq, sinq, gao, ckv, krope_t)


def _out_proj_kernel(attnp_ref, attns_ref, convn_ref, xp_ref, xs_ref, wo_ref, gffn_ref, wr_ref,
                     br_ref, h_ref, xpk_ref, mi_ref, mf_ref, cnt_ref, carry_ref, logit_ref, *, n_prompt_tiles):
    i = pl.program_id(0)

    @pl.when(i == 0)
    def _():
        carry_ref[...] = jnp.zeros(carry_ref.shape, F32)
        logit_ref[...] = jnp.zeros(logit_ref.shape, F32)

    def tile(x_ref, attn_ref):
        prev_logits = logit_ref[...]
        y = jnp.dot(attn_ref[...], wo_ref[:ATTN_W, :], preferred_element_type=F32)
        y = y + jnp.dot(convn_ref[...], wo_ref[ATTN_W:, :], preferred_element_type=F32)
        h = x_ref[...] + y
        h_ref[...] = h
        xn = _rms(h, gffn_ref[...])

        half = D_MODEL // 2
        xh = xn.astype(BF16)
        xh32 = xh.astype(F32)
        lo = lax.bitcast_convert_type(xh32[:, :half], jnp.uint32)
        hi = lax.bitcast_convert_type(xh32[:, half:], jnp.uint32)
        xpk_ref[...] = (lo >> 16) | (hi & jnp.uint32(0xFFFF0000))

        xl = (xn - xh32).astype(BF16)
        hh_hl = jnp.dot(xh, wr_ref[...], preferred_element_type=F32)
        lh = jnp.dot(xl, wr_ref[:, :LANES], preferred_element_type=F32)
        logit_ref[...] = hh_hl[:, :LANES] + (lh + hh_hl[:, LANES:]) + br_ref[...]

        logits = prev_logits
        counted = (i > 0).astype(F32)
        lane = lax.broadcasted_iota(jnp.int32, (TM, LANES), 1).astype(F32)
        ninf = -jnp.inf
        far = float(LANES)

        def first_argmax(v):
            vmax = jnp.max(v, axis=-1, keepdims=True)
            return vmax, jnp.min(jnp.where(v == vmax, lane, far), axis=-1, keepdims=True)

        gl = jnp.where(lane < N_GROUPS, logits, ninf)
        gmax, gidx = first_argmax(gl)
        g_p = 1.0 / jnp.sum(jnp.exp(gl - gmax), axis=-1, keepdims=True)
        e_lo = N_GROUPS + EXPERTS_PER_GROUP * gidx
        el = jnp.where((lane >= e_lo) & (lane < e_lo + EXPERTS_PER_GROUP), logits, ninf)
        e1max, i1 = first_argmax(el)
        z = jnp.sum(jnp.exp(el - e1max), axis=-1, keepdims=True)
        el2 = jnp.where(lane == i1, ninf, el)
        e2max, i2 = first_argmax(el2)
        p1 = 1.0 / z
        p2 = jnp.exp(e2max - e1max) / z
        den = p1 + p2
        g0 = g_p * p1 / den
        g1 = g_p * p2 / den
        e0 = i1 - N_GROUPS
        e1 = i2 - N_GROUPS

        oh0 = lane == e0
        oh1 = lane == e1
        oh = jnp.where(oh0 | oh1, 1.0, 0.0)
        r = lax.broadcasted_iota(jnp.int32, (TM, TM), 0)
        c = lax.broadcasted_iota(jnp.int32, (TM, TM), 1)
        ltri = jnp.where(r > c, 1.0, 0.0).astype(BF16)
        before = jnp.dot(ltri, oh.astype(BF16), preferred_element_type=F32) + carry_ref[...]
        rank0 = jnp.sum(jnp.where(oh0, before, 0.0), axis=-1, keepdims=True)
        rank1 = jnp.sum(jnp.where(oh1, before, 0.0), axis=-1, keepdims=True)
        total = carry_ref[...] + counted * jnp.sum(oh, axis=0, keepdims=True)
        carry_ref[...] = total
        cnt_ref[...] = jnp.broadcast_to(total, cnt_ref.shape)

        mi = jnp.where(lane == 0, e0, jnp.where(lane == 1, e1, jnp.where(lane == 2, rank0, rank1)))
        mi_ref[...] = jnp.transpose(mi)[:SUBLANES, :].astype(jnp.int32)
        mf_ref[...] = jnp.where(lane == 0, g0, g1)

    @pl.when(i < n_prompt_tiles)
    def _():
        tile(xp_ref, attnp_ref)

    @pl.when(i >= n_prompt_tiles)
    def _():
        tile(xs_ref, attns_ref)


def _out_proj(attn_p, attn_s, conv_n, xp, xs, w_ob, gffn, w_r2, b_r):
    m = conv_n.shape[0]
    npt = xp.shape[0] // TM
    n_tiles = m // TM
    last_p, last_s, last = npt - 1, n_tiles - npt - 1, n_tiles - 1
    row = lambda i: (jnp.minimum(i, last), 0)
    prow = lambda i: (jnp.minimum(i, last_p), 0)
    srow = lambda i: (jnp.clip(i - npt, 0, last_s), 0)
    lag = lambda i: jnp.maximum(i - 1, 0)
    return pl.pallas_call(
        functools.partial(_out_proj_kernel, n_prompt_tiles=npt),
        grid=(n_tiles + 1,),
        in_specs=[
            pl.BlockSpec((TM, ATTN_W), prow),
            pl.BlockSpec((TM, ATTN_W), srow),
            pl.BlockSpec((TM, CONV_CH), row),
            pl.BlockSpec((TM, D_MODEL), prow),
            pl.BlockSpec((TM, D_MODEL), srow),
            _const_spec(w_ob.shape),
            _const_spec((1, D_MODEL)),
            _const_spec(w_r2.shape),
            _const_spec((1, LANES)),
        ],
        out_specs=[
            pl.BlockSpec((TM, D_MODEL), row),
            pl.BlockSpec((TM, D_MODEL // 2), row),
            pl.BlockSpec((SUBLANES, TM), lambda i: (0, lag(i))),
            pl.BlockSpec((TM, LANES), lambda i: (lag(i), 0)),
            pl.BlockSpec((SUBLANES, LANES), lambda i: (0, 0)),
        ],
        out_shape=[
            jax.ShapeDtypeStruct((m, D_MODEL), F32),
            jax.ShapeDtypeStruct((m, D_MODEL // 2), jnp.uint32),
            jax.ShapeDtypeStruct((SUBLANES, m), jnp.int32),
            jax.ShapeDtypeStruct((m, LANES), F32),
            jax.ShapeDtypeStruct((SUBLANES, LANES), F32),
        ],
        scratch_shapes=[pltpu.VMEM((1, LANES), F32), pltpu.VMEM((TM, LANES), F32)],
        compiler_params=pltpu.CompilerParams(dimension_semantics=("arbitrary",),
                                             vmem_limit_bytes=VMEM_LIMIT),
        name="out_proj",
    )(attn_p, attn_s, conv_n, xp, xs, w_ob, gffn, w_r2, b_r)


def _dispatch_kernel(d0_ref, d1_ref, zlo_ref, zn_ref, nu_ref, xpk_ref, xs_hbm, zeros_ref, sems, *, n_blocks):
    i = pl.program_id(0)
    sem = sems.at[0]
    zsem = sems.at[1]

    def zero_fill(act):
        def per_expert(e, c):
            lo = zlo_ref[e]
            n = zn_ref[e]
            head = (-lo) & (SUBLANES - 1)
            for r in range(SUBLANES - 1):
                @pl.when(r < head)
                def _(r=r):
                    act(pltpu.make_async_copy(zeros_ref.at[pl.ds(0, 1)], xs_hbm.at[pl.ds(lo + r, 1)], zsem))
            off = lo + head
            rest = n - head
            size = MOE_BLOCK // 2
            while size >= SUBLANES:
                @pl.when((rest & size) != 0)
                def _(off=off, size=size):
                    dst = xs_hbm.at[pl.ds(pl.multiple_of(off, SUBLANES), size)]
                    act(pltpu.make_async_copy(zeros_ref.at[pl.ds(0, size)], dst, zsem))
                off = off + (rest & size)
                size //= 2
            return c

        def per_block(b, c):
            dst = xs_hbm.at[pl.ds(pl.multiple_of(b * MOE_BLOCK, MOE_BLOCK), MOE_BLOCK)]
            act(pltpu.make_async_copy(zeros_ref, dst, zsem))
            return c

        lax.fori_loop(0, N_EXPERTS, per_expert, 0)
        lax.fori_loop(nu_ref[0], n_blocks, per_block, 0)

    @pl.when(i == 0)
    def _():
        zeros_ref[...] = jnp.zeros(zeros_ref.shape, zeros_ref.dtype)
        zero_fill(lambda cp: cp.start())

    @pl.when(i == pl.num_programs(0) - 1)
    def _():
        zero_fill(lambda cp: cp.wait())

    base = i * TD

    def start(g, c):
        for u in range(SUBLANES):
            r = base + g * SUBLANES + u
            src = xpk_ref.at[g, pl.ds(u, 1)]
            pltpu.make_async_copy(src, xs_hbm.at[pl.ds(d0_ref[r], 1)], sem).start()
            pltpu.make_async_copy(src, xs_hbm.at[pl.ds(d1_ref[r], 1)], sem).start()
        return c

    lax.fori_loop(0, TD // SUBLANES, start, 0)
    for _ in range(2):
        pltpu.make_async_copy(xs_hbm.at[pl.ds(0, TD)], xs_hbm.at[pl.ds(0, TD)], sem).wait()


def _dispatch(dest0, dest1, pad_lo, n_pad, n_used, xpk, n_blocks):
    m = xpk.shape[0]
    grid_spec = pltpu.PrefetchScalarGridSpec(
        num_scalar_prefetch=5,
        grid=(m // TD,),
        in_specs=[pl.BlockSpec((TD // SUBLANES, SUBLANES, D_MODEL // 2), lambda i, *_: (i, 0, 0))],
        out_specs=pl.BlockSpec(memory_space=pl.ANY),
        scratch_shapes=[pltpu.VMEM((MOE_BLOCK, D_MODEL // 2), jnp.uint32),
                        pltpu.SemaphoreType.DMA((2,))],
    )
    return pl.pallas_call(
        functools.partial(_dispatch_kernel, n_blocks=n_blocks),
        grid_spec=grid_spec,
        out_shape=jax.ShapeDtypeStruct((n_blocks * MOE_BLOCK, D_MODEL // 2), jnp.uint32),
        compiler_params=pltpu.CompilerParams(dimension_semantics=("arbitrary",)),
        name="dispatch",
    )(dest0, dest1, pad_lo, n_pad, n_used, xpk.reshape(m // SUBLANES, SUBLANES, D_MODEL // 2))


def _experts_kernel(be_ref, nu_ref, nxt_ref, x_ref, wg_hbm, wu_hbm, wd_hbm, y_ref,
                    sg_ref, su_ref, sd_ref, wgb_ref, wub_ref, wdb_ref, sems):
    b = pl.program_id(0)
    active = b < nu_ref[0]
    new_expert = jnp.logical_or(b == 0, be_ref[b] != be_ref[jnp.maximum(b - 1, 0)])

    def weight_copies(e):
        return (pltpu.make_async_copy(wg_hbm.at[e], sg_ref, sems.at[0]),
                pltpu.make_async_copy(wu_hbm.at[e], su_ref, sems.at[1]),
                pltpu.make_async_copy(wd_hbm.at[e], sd_ref, sems.at[2]))

    @pl.when(b == 0)
    def _():
        for cp in weight_copies(be_ref[0]):
            cp.start()

    @pl.when(jnp.logical_and(active, new_expert))
    def _():
        for cp in weight_copies(be_ref[b]):
            cp.wait()
        wgb_ref[...] = sg_ref[...].astype(BF16)
        wub_ref[...] = su_ref[...].astype(BF16)
        wdb_ref[...] = sd_ref[...].astype(BF16)

        @pl.when(nxt_ref[b] >= 0)
        def _():
            for cp in weight_copies(nxt_ref[b]):
                cp.start()

    @pl.when(active)
    def _():
        half = D_MODEL // 2
        xa, xb = (v.astype(BF16) for v in _unpack_bf16_pairs(x_ref[...]))
        g = jnp.dot(xa, wgb_ref[:half, :], preferred_element_type=F32)
        g = g + jnp.dot(xb, wgb_ref[half:, :], preferred_element_type=F32)
        u = jnp.dot(xa, wub_ref[:half, :], preferred_element_type=F32)
        u = u + jnp.dot(xb, wub_ref[half:, :], preferred_element_type=F32)
        hmid = (g * jax.nn.sigmoid(g)) * u
        y = jnp.dot(hmid.astype(BF16), wdb_ref[...], preferred_element_type=F32)
        y_ref[...] = _pack_bf16_pairs(y)

    @pl.when(b >= nu_ref[0])
    def _():
        y_ref[...] = jnp.zeros(y_ref.shape, y_ref.dtype)


def _experts(block_e, n_used, next_e, x_sorted, w_gate, w_up, w_down):
    p = x_sorted.shape[0]
    nb = p // MOE_BLOCK

    def xrow(b, be, nu, nxt):
        return (jnp.maximum(jnp.minimum(b, nu[0] - 1), 0), 0)

    grid_spec = pltpu.PrefetchScalarGridSpec(
        num_scalar_prefetch=3,
        grid=(nb,),
        in_specs=[
            pl.BlockSpec((MOE_BLOCK, D_MODEL // 2), xrow),
            pl.BlockSpec(memory_space=pl.ANY),
            pl.BlockSpec(memory_space=pl.ANY),
            pl.BlockSpec(memory_space=pl.ANY),
        ],
        out_specs=pl.BlockSpec((MOE_BLOCK, D_MODEL // 2), lambda b, be, nu, nxt: (b, 0)),
        scratch_shapes=[pltpu.VMEM((D_MODEL, D_FF), F32), pltpu.VMEM((D_MODEL, D_FF), F32),
                        pltpu.VMEM((D_FF, D_MODEL), F32),
                        pltpu.VMEM((D_MODEL, D_FF), BF16), pltpu.VMEM((D_MODEL, D_FF), BF16),
                        pltpu.VMEM((D_FF, D_MODEL), BF16),
                        pltpu.SemaphoreType.DMA((3,))],
    )
    return pl.pallas_call(
        _experts_kernel,
        grid_spec=grid_spec,
        out_shape=jax.ShapeDtypeStruct((p, D_MODEL // 2), jnp.uint32),
        compiler_params=pltpu.CompilerParams(dimension_semantics=("arbitrary",),
                                             vmem_limit_bytes=VMEM_LIMIT),
        name="experts",
    )(block_e, n_used, next_e, x_sorted, w_gate, w_up, w_down)


def _combine_kernel(d0_ref, d1_ref, h_ref, mf_ref, gfin_ref, y_hbm, outp_ref, outs_ref, y0_ref, y1_ref, sems,
                    *, n_tiles, n_prompt_tiles):
    i = pl.program_id(0)

    def gather(tile, slot, act):
        base = tile * TM

        def body(g, c):
            for u in range(SUBLANES):
                r = base + g * SUBLANES + u
                act(pltpu.make_async_copy(y_hbm.at[pl.ds(d0_ref[r], 1)], y0_ref.at[slot, g, pl.ds(u, 1)],
                                          sems.at[slot]))
                act(pltpu.make_async_copy(y_hbm.at[pl.ds(d1_ref[r], 1)], y1_ref.at[slot, g, pl.ds(u, 1)],
                                          sems.at[slot]))
            return c
        lax.fori_loop(0, TM // SUBLANES, body, 0)

    @pl.when(i == 0)
    def _():
        gather(0, 0, lambda cp: cp.start())

    @pl.when(i + 1 < n_tiles)
    def _():
        gather(i + 1, (i + 1) % 2, lambda cp: cp.start())

    slot = i % 2
    for _ in range(2):
        pltpu.make_async_copy(y_hbm.at[pl.ds(0, TM)], y_hbm.at[pl.ds(0, TM)], sems.at[slot]).wait()

    def finish(out_ref):
        mf = mf_ref[...]
        g0, g1 = mf[:, :, 0:1], mf[:, :, 1:2]
        half = D_MODEL // 2
        a0, b0 = _unpack_bf16_pairs(y0_ref[slot])
        a1, b1 = _unpack_bf16_pairs(y1_ref[slot])
        o_lo = h_ref[:, :, :half] + (g0 * a0 + g1 * a1)
        o_hi = h_ref[:, :, half:] + (g0 * b0 + g1 * b1)
        sumsq = jnp.sum(o_lo * o_lo, axis=-1, keepdims=True) + jnp.sum(o_hi * o_hi, axis=-1, keepdims=True)
        inv_rms = lax.rsqrt(sumsq / D_MODEL + EPS)
        out_ref[:, :, :half] = o_lo * inv_rms * gfin_ref[:, :, :half]
        out_ref[:, :, half:] = o_hi * inv_rms * gfin_ref[:, :, half:]

    @pl.when(i < n_prompt_tiles)
    def _():
        finish(outp_ref)

    @pl.when(i >= n_prompt_tiles)
    def _():
        finish(outs_ref)


def _combine(dest0, dest1, h, mf, gfin, y_sorted, *, n_prompt_rows):
    m = h.shape[0]
    npt = n_prompt_rows // TM
    tg = TM // SUBLANES
    grouped = lambda a: a.reshape(a.shape[0] // SUBLANES, SUBLANES, a.shape[1])
    grid_spec = pltpu.PrefetchScalarGridSpec(
        num_scalar_prefetch=2,
        grid=(m // TM,),
        in_specs=[
            pl.BlockSpec((tg, SUBLANES, D_MODEL), lambda i, *_: (i, 0, 0)),
            pl.BlockSpec((tg, SUBLANES, LANES), lambda i, *_: (i, 0, 0)),
            pl.BlockSpec((1, 1, D_MODEL), lambda i, *_: (0, 0, 0)),
            pl.BlockSpec(memory_space=pl.ANY),
        ],
        out_specs=[pl.BlockSpec((tg, SUBLANES, D_MODEL), lambda i, *_: (jnp.minimum(i, npt - 1), 0, 0)),
                   pl.BlockSpec((tg, SUBLANES, D_MODEL), lambda i, *_: (jnp.maximum(i - npt, 0), 0, 0))],
        scratch_shapes=[pltpu.VMEM((2, tg, SUBLANES, D_MODEL // 2), jnp.uint32),
                        pltpu.VMEM((2, tg, SUBLANES, D_MODEL // 2), jnp.uint32),
                        pltpu.SemaphoreType.DMA((2,))],
    )
    y_p, y_s = pl.pallas_call(
        functools.partial(_combine_kernel, n_tiles=m // TM, n_prompt_tiles=npt),
        grid_spec=grid_spec,
        out_shape=[jax.ShapeDtypeStruct((n_prompt_rows // SUBLANES, SUBLANES, D_MODEL), F32),
                   jax.ShapeDtypeStruct(((m - n_prompt_rows) // SUBLANES, SUBLANES, D_MODEL), F32)],
        compiler_params=pltpu.CompilerParams(dimension_semantics=("arbitrary",),
                                             vmem_limit_bytes=VMEM_LIMIT),
        name="combine",
    )(dest0, dest1, grouped(h), grouped(mf), gfin.reshape(1, 1, D_MODEL), y_sorted)
    return y_p.reshape(n_prompt_rows, D_MODEL), y_s.reshape(m - n_prompt_rows, D_MODEL)


def _rope_tables(pos):
    f32 = np.float32
    inv = np.power(f32(ROPE_THETA), -np.arange(0, ROPE_DIM, 2, dtype=f32) / f32(ROPE_DIM)).astype(f32)
    ang = (pos.astype(f32)[:, None] * inv[None, :]).astype(f32)
    cos, sin = np.cos(ang).astype(f32), np.sin(ang).astype(f32)
    return np.concatenate([cos, cos], axis=-1), np.concatenate([-sin, sin], axis=-1)


def _swap_halves(w):
    return jnp.concatenate([w[..., ROPE_DIM // 2:], w[..., :ROPE_DIM // 2]], axis=-1)


def kernel(x_prompt, x_sample, cache_kv_latent, cache_k_rope, state_conv, norm_mix, w_in, norm_q, w_uq,
           norm_kv, w_uk, w_uv, conv_w, norm_attn_out, norm_conv_out, w_o, norm_ffn, w_router_group,
           b_router_group, w_router_expert, b_router_expert, w_gate, w_up, w_down, norm_final):
    assert w_in.shape[0] == 1, "single-layer trunk"
    bp, seq_p, _ = x_prompt.shape
    bs, seq_s, _ = x_sample.shape
    past_len = cache_kv_latent.shape[2]
    np_rows, ns_rows = bp * seq_p, bs * seq_s
    m = np_rows + ns_rows
    assert seq_p % TM == 0 and TM % seq_s == 0 and ns_rows % TM == 0 and seq_s == CHUNK
    assert m % TD == 0

    xp = x_prompt.reshape(np_rows, D_MODEL)
    xs = x_sample.reshape(ns_rows, D_MODEL)
    row_vec = lambda v: v.reshape(1, -1)

    assert w_in.shape[2] == Q_LORA + KV_LORA + ROPE_DIM + 3 * CONV_CH
    w_t = jnp.swapaxes(w_in[0], 0, 1).astype(BF16)
    wq4 = w_uq[0].reshape(Q_LORA, N_HEADS, QK_NOPE + ROPE_DIM)
    wq_rope = wq4[:, :, QK_NOPE:]
    w_q = jnp.concatenate([wq4[:, :, :QK_NOPE].reshape(Q_LORA, -1), wq_rope.reshape(Q_LORA, -1),
                           _swap_halves(wq_rope).reshape(Q_LORA, -1)], axis=1).astype(BF16)
    w_ukt = jnp.transpose(w_uk[0], (1, 2, 0)).astype(BF16)
    w_uvh = jnp.transpose(w_uv[0], (1, 0, 2)).astype(BF16)
    w_ob = w_o[0].astype(BF16)
    n_router = N_GROUPS + N_EXPERTS
    w_r = jnp.concatenate([w_router_group[0], w_router_expert[0].reshape(D_MODEL, N_EXPERTS)], axis=1)
    w_r = jnp.pad(w_r, ((0, 0), (0, LANES - n_router)))
    w_rh = w_r.astype(BF16)
    w_rl = (w_r - w_rh.astype(F32)).astype(BF16)
    w_r2 = jnp.concatenate([w_rh, w_rl], axis=1)
    b_r =jnp.pad(jnp.concatenate([b_router_group[0], b_router_expert[0].reshape(N_EXPERTS)]),
                  (0, LANES - n_router)).reshape(1, LANES)

    cos_p, sin_p = _rope_tables(np.arange(seq_p))
    cos_s, sin_s = _rope_tables(past_len + np.arange(seq_s))
    cosk = np.concatenate([cos_p, np.tile(cos_s, (TM // seq_s, 1))], axis=0)
    sink = np.concatenate([sin_p, np.tile(sin_s, (TM // seq_s, 1))], axis=0)
    state = jnp.concatenate([jnp.zeros((bp, CONV_W - 1, CONV_CH), F32), state_conv[0]], axis=0)

    cqn, ckv_p, kr_p, ckv_s, kr_s, conv_n, utail = _in_proj(
        xp, xs, row_vec(norm_mix[0]), w_t, row_vec(norm_q[0]), row_vec(norm_kv[0]),
        row_vec(norm_conv_out[0]), conv_w[0], cosk, sink, state, seq_p=seq_p, seq_s=seq_s)

    gao = row_vec(norm_attn_out[0])
    w_kv = jnp.concatenate([w_uk[0].reshape(KV_LORA, N_HEADS * QK_NOPE),
                            w_uv[0].reshape(KV_LORA, N_HEADS * V_DIM)], axis=1).astype(BF16)
    attn_p = _attention_heads(cqn, w_q, w_kv, np.tile(cos_p, (1, N_HEADS)), np.tile(sin_p, (1, N_HEADS)),
                              gao, ckv_p, kr_p, n_batch=bp, seq=seq_p)
    attn_s = _attention(cqn, w_q, w_ukt, w_uvh, np.tile(cos_s, (1, N_HEADS)), np.tile(sin_s, (1, N_HEADS)),
                        gao, ckv_s, kr_s, cache_kv_latent[0], jnp.swapaxes(cache_k_rope[0], 1, 2),
                        n_batch=bs, seq=seq_s, row0=np_rows)

    h, xpk, mi, mf, cnt = _out_proj(attn_p, attn_s, conv_n, xp, xs, w_ob, row_vec(norm_ffn[0]),
                                    w_r2, b_r)

    counts = cnt[0, :N_EXPERTS].astype(jnp.int32)
    padded = (counts + MOE_BLOCK - 1) // MOE_BLOCK * MOE_BLOCK
    pad_end = jnp.cumsum(padded)
    pad_start = pad_end - padded
    n_blocks = -(-(m * 2) // MOE_BLOCK) + N_EXPERTS
    block_row0 = jnp.arange(n_blocks, dtype=jnp.int32) * MOE_BLOCK
    block_e = jnp.minimum(jnp.sum((pad_end[None, :] <= block_row0[:, None]).astype(jnp.int32), axis=1),
                          N_EXPERTS - 1)
    n_used = (pad_end[-1:] // MOE_BLOCK).astype(jnp.int32)
    expert_ids = jnp.arange(N_EXPERTS, dtype=jnp.int32)[:, None]

    def seg_start(e):
        return jnp.sum(jnp.where(expert_ids == e[None, :], pad_start[:, None], 0), axis=0)

    dest0 = seg_start(mi[0]) + mi[2]
    dest1 = seg_start(mi[1]) + mi[3]

    x_sorted = _dispatch(dest0, dest1, pad_start + counts, padded - counts, n_used, xpk, n_blocks)
    later = (expert_ids.T > block_e[:, None]) & (padded > 0)[None, :]
    next_e = jnp.min(jnp.where(later, expert_ids.T, N_EXPERTS), axis=1)
    next_e = jnp.where(next_e == N_EXPERTS, -1, next_e).astype(jnp.int32)
    y_sorted = _experts(block_e, n_used, next_e, x_sorted, w_gate[0], w_up[0], w_down[0])
    gfin = row_vec(norm_final)
    y_p, y_s = _combine(dest0, dest1, h, mf, gfin, y_sorted, n_prompt_rows=np_rows)

    ut = utail.reshape(m // CHUNK, SUBLANES, CONV_CH)
    tails = ut[:, SUBLANES - (CONV_W - 1):, :]
    p_last = (jnp.arange(bp) + 1) * (seq_p // CHUNK) - 1
    s_last = np_rows // CHUNK + (jnp.arange(bs) + 1) * (seq_s // CHUNK) - 1
    return (y_p.reshape(bp, seq_p, D_MODEL),
            y_s.reshape(bs, seq_s, D_MODEL),
            ckv_p.reshape(1, bp, seq_p, KV_LORA),
            jnp.swapaxes(kr_p, 1, 2)[None],
            tails[p_last][None],
            ckv_s.reshape(1, bs, seq_s, KV_LORA),
            jnp.swapaxes(kr_s, 1, 2)[None],
            tails[s_last][None])
```

```python
import functools

import jax
import jax.numpy as jnp
import numpy as np
from jax import lax
from jax.experimental import pallas as pl
from jax.experimental.pallas import tpu as pltpu

F32 = jnp.float32
BF16 = jnp.bfloat16

D_MODEL = 2048
N_HEADS = 8
QK_NOPE = 128
ROPE_DIM = 64
V_DIM = 128
Q_LORA = 512
KV_LORA = 512
ATTN_W = N_HEADS * V_DIM
CONV_CH = D_MODEL - ATTN_W
CONV_W = 3
CHUNK = 64
N_GROUPS = 4
EXPERTS_PER_GROUP = 8
N_EXPERTS = N_GROUPS * EXPERTS_PER_GROUP
D_FF = 512
ROPE_THETA = 10000.0
EPS = 1e-6
ATTN_SCALE = (QK_NOPE + ROPE_DIM) ** -0.5
EXP2_SCALE = ATTN_SCALE * 1.4426950408889634

LANES = 128
SUBLANES = 8
TM = 256
TD = 2304
MOE_BLOCK = 256
TQ = 512
TK = 256
TKH = 512
W_IN_SLABS = 10
NEG_BIG = -1e30
V7X_VMEM_BYTES = 64 * 1024 * 1024
VMEM_LIMIT = V7X_VMEM_BYTES * 7 // 8


def _rms(v, g):
    return v * lax.rsqrt(jnp.mean(v * v, axis=-1, keepdims=True) + EPS) * g


def _lane_bcast(v, width):
    if width % LANES == 0:
        return jnp.concatenate([v] * (width // LANES), axis=1)
    assert width < LANES
    return v[:, :width]


def _pack_bf16_pairs(v):
    half = v.shape[-1] // 2
    lo = lax.bitcast_convert_type(v[..., :half].astype(BF16).astype(F32), jnp.uint32)
    hi = lax.bitcast_convert_type(v[..., half:].astype(BF16).astype(F32), jnp.uint32)
    return (lo >> 16) | (hi & jnp.uint32(0xFFFF0000))


def _unpack_bf16_pairs(w):
    return (lax.bitcast_convert_type(w << 16, F32),
            lax.bitcast_convert_type(w & jnp.uint32(0xFFFF0000), F32))


def _const_spec(shape):
    nd = len(shape)
    return pl.BlockSpec(shape, lambda *_: (0,) * nd, pipeline_mode=pl.Buffered(1))


def _in_proj_kernel(xp_ref, xs_ref, gmix_ref, w_hbm, gq_ref, gkv_ref, gco_ref, convw_ref,
                    cos_ref, sin_ref, state_ref,
                    cqn_ref, ckvp_ref, krp_ref, ckvs_ref, krs_ref, convn_ref, utail_ref,
                    ext_ref, wt_ref, stage_ref, wsems,
                    *, n_prompt_tiles, tiles_per_seq, n_prompt_seq, sample_seq_len):
    i = pl.program_id(0)

    @pl.when(i == 0)
    def _():
        ext_ref[...] = jnp.zeros(ext_ref.shape, F32)
        n_slabs = wt_ref.shape[0] // stage_ref.shape[1]
        rows = stage_ref.shape[1]

        def slab_copy(c):
            return pltpu.make_async_copy(w_hbm.at[pl.ds(c * rows, rows)], stage_ref.at[c % 2], wsems.at[c % 2])

        slab_copy(0).start()
        for c in range(n_slabs):
            if c + 1 < n_slabs:
                slab_copy(c + 1).start()
            slab_copy(c).wait()
            wt_ref[c * rows:(c + 1) * rows, :] = stage_ref[c % 2].astype(BF16)

    def conv_block(u_sub, gate_sub, row0, length):
        ext_ref[SUBLANES:SUBLANES + length, :] = u_sub
        um1 = ext_ref[SUBLANES - 1:SUBLANES - 1 + length, :]
        um2 = ext_ref[SUBLANES - 2:SUBLANES - 2 + length, :]
        cw = convw_ref[...]
        conv = cw[0:1] * um2 + cw[1:2] * um1 + cw[2:3] * u_sub
        convn_ref[row0:row0 + length, :] = _rms(gate_sub * conv, gco_ref[...]).astype(BF16)

    def tile(x_ref, is_prompt):
        ckv_ref, krt_ref = (ckvp_ref, krp_ref) if is_prompt else (ckvs_ref, krs_ref)
        x = x_ref[...]
        xg = (x * gmix_ref[...]).astype(BF16)
        inv_rms = lax.rsqrt(jnp.mean(x * x, axis=-1, keepdims=True) + EPS)
        lat_w = Q_LORA + KV_LORA
        conv0 = lat_w + ROPE_DIM
        nt = (((1,), (1,)), ((), ()))

        def project(lo, hi):
            return inv_rms * lax.dot_general(xg, wt_ref[lo:hi, :], nt, preferred_element_type=F32)

        z_ch = project(conv0 + CONV_CH, conv0 + 3 * CONV_CH)
        u = z_ch[:, :CONV_CH] * z_ch[:, CONV_CH:]
        for j in range(TM // CHUNK):
            utail_ref[j] = u[CHUNK * (j + 1) - SUBLANES:CHUNK * (j + 1), :]
        gate_b = project(conv0, conv0 + CONV_CH)

        if is_prompt:
            first = (i % tiles_per_seq) == 0
            carried = ext_ref[TM + SUBLANES - 2:TM + SUBLANES, :]
            ext_ref[SUBLANES - 2:SUBLANES, :] = jnp.where(first, state_ref[i // tiles_per_seq], carried)
            conv_block(u, gate_b, 0, TM)
        else:
            n_sub = TM // sample_seq_len
            seq0 = n_prompt_seq + (i - n_prompt_tiles) * n_sub
            for k in range(n_sub):
                ext_ref[SUBLANES - 2:SUBLANES, :] = state_ref[seq0 + k]
                lo = k * sample_seq_len
                conv_block(u[lo:lo + sample_seq_len], gate_b[lo:lo + sample_seq_len], lo, sample_seq_len)

        zk = project(lat_w, conv0)
        zk_swapped = jnp.concatenate([zk[:, ROPE_DIM // 2:], zk[:, :ROPE_DIM // 2]], axis=1)
        k_rope = zk * cos_ref[...] + zk_swapped * sin_ref[...]
        if is_prompt:
            krt_ref[...] = k_rope.T
        else:
            for k in range(TM // sample_seq_len):
                krt_ref[k] = k_rope[k * sample_seq_len:(k + 1) * sample_seq_len, :].T
        ckv_ref[...] = _rms(project(Q_LORA, lat_w), gkv_ref[...])
        cqn_ref[...] = _rms(project(0, Q_LORA), gq_ref[...]).astype(BF16)

    @pl.when(i < n_prompt_tiles)
    def _():
        tile(xp_ref, True)

    @pl.when(i >= n_prompt_tiles)
    def _():
        tile(xs_ref, False)


def _in_proj(xp, xs, gmix, w_t, gq, gkv, gco, convw, cosk, sink, state, *, seq_p, seq_s):
    np_rows, ns_rows = xp.shape[0], xs.shape[0]
    m = np_rows + ns_rows
    npt, nst = np_rows // TM, ns_rows // TM
    tps = seq_p // TM
    n_prompt_seq = np_rows // seq_p
    last_p = npt - 1

    def tab_idx(i):
        return (jnp.where(i < npt, i % tps, tps), 0)

    row = lambda i: (i, 0)
    prow = lambda i: (jnp.minimum(i, last_p), 0)
    srow = lambda i: (jnp.maximum(i - npt, 0), 0)
    kern = functools.partial(_in_proj_kernel, n_prompt_tiles=npt, tiles_per_seq=tps,
                             n_prompt_seq=n_prompt_seq, sample_seq_len=seq_s)
    return pl.pallas_call(
        kern,
        grid=(npt + nst,),
        in_specs=[
            pl.BlockSpec((TM, D_MODEL), prow),
            pl.BlockSpec((TM, D_MODEL), srow),
            _const_spec((1, D_MODEL)),
            pl.BlockSpec(memory_space=pl.ANY),
            _const_spec((1, Q_LORA)),
            _const_spec((1, KV_LORA)),
            _const_spec((1, CONV_CH)),
            _const_spec((CONV_W, CONV_CH)),
            pl.BlockSpec((TM, ROPE_DIM), tab_idx),
            pl.BlockSpec((TM, ROPE_DIM), tab_idx),
            _const_spec(state.shape),
        ],
        out_specs=[
            pl.BlockSpec((TM, Q_LORA), row),
            pl.BlockSpec((TM, KV_LORA), prow),
            pl.BlockSpec((None, ROPE_DIM, TM), lambda i: (jnp.minimum(i, last_p) // tps, 0,
                                                          jnp.minimum(i, last_p) % tps)),
            pl.BlockSpec((TM, KV_LORA), srow),
            pl.BlockSpec((TM // seq_s, ROPE_DIM, seq_s), lambda i: (jnp.maximum(i - npt, 0), 0, 0)),
            pl.BlockSpec((TM, CONV_CH), row),
            pl.BlockSpec((TM // CHUNK, SUBLANES, CONV_CH), lambda i: (i, 0, 0)),
        ],
        out_shape=[
            jax.ShapeDtypeStruct((m, Q_LORA), BF16),
            jax.ShapeDtypeStruct((np_rows, KV_LORA), F32),
            jax.ShapeDtypeStruct((n_prompt_seq, ROPE_DIM, seq_p), F32),
            jax.ShapeDtypeStruct((ns_rows, KV_LORA), F32),
            jax.ShapeDtypeStruct((ns_rows // seq_s, ROPE_DIM, seq_s), F32),
            jax.ShapeDtypeStruct((m, CONV_CH), BF16),
            jax.ShapeDtypeStruct((m // CHUNK, SUBLANES, CONV_CH), F32),
        ],
        scratch_shapes=[pltpu.VMEM((TM + SUBLANES, CONV_CH), F32),
                        pltpu.VMEM(w_t.shape, BF16),
                        pltpu.VMEM((2, w_t.shape[0] // W_IN_SLABS, w_t.shape[1]), F32),
                        pltpu.SemaphoreType.DMA((2,))],
        compiler_params=pltpu.CompilerParams(dimension_semantics=("arbitrary",),
                                             vmem_limit_bytes=VMEM_LIMIT),
        name="in_proj",
    )(xp, xs, gmix, w_t, gq, gkv, gco, convw, cosk, sink, state)


def _attn_kernel(cqn_ref, wq_ref, wuk_ref, wuv_ref, cos_ref, sin_ref, gao_ref, pkv_ref, pkr_ref, kv_ref, kr_ref,
                 out_ref, qlat_ref, qr_ref, m_ref, l_ref, acc_ref, s_ref, *, tq, n_past):
    rows = N_HEADS * tq

    q = jnp.dot(cqn_ref[...], wq_ref[...], preferred_element_type=F32)
    nope_w = N_HEADS * QK_NOPE
    rope_w = N_HEADS * ROPE_DIM
    qrope = q[:, nope_w:nope_w + rope_w] * cos_ref[...] + q[:, nope_w + rope_w:] * sin_ref[...]
    for h in range(N_HEADS):
        qn = q[:, h * QK_NOPE:(h + 1) * QK_NOPE].astype(BF16)
        ql = jnp.dot(qn, wuk_ref[h], preferred_element_type=F32)
        qlat_ref[h * tq:(h + 1) * tq, :] = ql.astype(BF16)
        qr_ref[h * tq:(h + 1) * tq, :] = qrope[:, h * ROPE_DIM:(h + 1) * ROPE_DIM].astype(BF16)


    nt = (((1,), (1,)), ((), ()))

    def scores(kc_f32, krt_f32):
        s = lax.dot_general(qlat_ref[...], kc_f32.astype(BF16), nt, preferred_element_type=F32)
        return s + jnp.dot(qr_ref[...], krt_f32.astype(BF16), preferred_element_type=F32)

    def update(s, kc_f32, mask, first=False):
        if mask is not None:
            s = jnp.where(mask, s, NEG_BIG)
        m_cur = jnp.max(s, axis=-1, keepdims=True)
        if first:
            m_new = jnp.broadcast_to(m_cur, m_ref.shape)
        else:
            m_prev = m_ref[...]
            m_new = jnp.maximum(m_prev, m_cur)
            alpha = jnp.exp2((m_prev - m_new) * EXP2_SCALE)
        p = jnp.exp2((s - _lane_bcast(m_new, s.shape[1])) * EXP2_SCALE)
        l_cur = jnp.sum(p, axis=-1, keepdims=True)
        pv = jnp.dot(p.astype(BF16), kc_f32.astype(BF16), preferred_element_type=F32)
        if first:
            l_ref[...] = jnp.broadcast_to(l_cur, l_ref.shape)
            acc_ref[...] = pv
        else:
            l_ref[...] = alpha * l_ref[...] + l_cur
            acc_ref[...] = _lane_bcast(alpha, KV_LORA) * acc_ref[...] + pv
        m_ref[...] = m_new

    def pipelined(kv, kr, lo, hi, last, mask_fn):
        def body(j, c):
            k0 = pl.multiple_of(j * TK, TK)
            k1 = pl.multiple_of(jnp.minimum(j + 1, last) * TK, TK)
            s_cur = s_ref[j % 2]
            s_ref[(j + 1) % 2] = scores(kv[pl.ds(k1, TK), :], kr[:, pl.ds(k1, TK)])
            update(s_cur, kv[pl.ds(k0, TK), :], None if mask_fn is None else mask_fn(k0))
            return c
        lax.fori_loop(lo, hi, body, 0)

    def pipelined_pairs(kv, kr, n_pairs, last):
        def body(i, c):
            ka = pl.multiple_of((2 * i + 1) * TK, TK)
            kb = pl.multiple_of((2 * i + 2) * TK, TK)
            kc = pl.multiple_of(jnp.minimum(2 * i + 3, last) * TK, TK)
            s_ref[0] = scores(kv[pl.ds(kb, TK), :], kr[:, pl.ds(kb, TK)])
            update(s_ref[1], kv[pl.ds(ka, TK), :], None)
            s_ref[1] = scores(kv[pl.ds(kc, TK), :], kr[:, pl.ds(kc, TK)])
            update(s_ref[0], kv[pl.ds(kb, TK), :], None)
            return c
        lax.fori_loop(0, n_pairs, body, 0)

    def first_block(kv, kr, last, mask):
        k1 = pl.multiple_of(jnp.minimum(1, last) * TK, TK)
        s_ref[0] = scores(kv[pl.ds(0, TK), :], kr[:, pl.ds(0, TK)])
        s_ref[1] = scores(kv[pl.ds(k1, TK), :], kr[:, pl.ds(k1, TK)])
        update(s_ref[0], kv[pl.ds(0, TK), :], mask, first=True)

    n_pb = n_past // TK
    first_block(pkv_ref, pkr_ref, n_pb - 1, None)
    n_pairs = (n_pb - 1) // 2
    pipelined_pairs(pkv_ref, pkr_ref, n_pairs, n_pb - 1)
    if 1 + 2 * n_pairs < n_pb:
        pipelined(pkv_ref, pkr_ref, 1 + 2 * n_pairs, n_pb, n_pb - 1, None)

    update(scores(kv_ref[...], kr_ref[...]), kv_ref[...], None)

    o = acc_ref[...] / _lane_bcast(l_ref[...], KV_LORA)
    parts = []
    for h in range(N_HEADS):
        oh = o[h * tq:(h + 1) * tq, :].astype(BF16)
        parts.append(jnp.dot(oh, wuv_ref[h], preferred_element_type=F32))
    attn = jnp.concatenate(parts, axis=-1)
    out_ref[...] = _rms(attn, gao_ref[...]).astype(BF16)


def _attention(cqn, w_q, w_ukt, w_uv, cosq, sinq, gao, ckv, krope, past_kv, past_kr, *, n_batch, seq, row0):
    tq = seq
    n_past = past_kv.shape[1]
    assert n_past % CHUNK == 0 and seq <= CHUNK and n_past % TK == 0
    blk0 = row0 // tq
    in_specs = [
        pl.BlockSpec((tq, Q_LORA), lambda b, q: (blk0 + b, 0)),
        _const_spec(w_q.shape),
        _const_spec(w_ukt.shape),
        _const_spec(w_uv.shape),
        pl.BlockSpec((tq, N_HEADS * ROPE_DIM), lambda b, q: (0, 0)),
        pl.BlockSpec((tq, N_HEADS * ROPE_DIM), lambda b, q: (0, 0)),
        _const_spec((1, ATTN_W)),
        pl.BlockSpec((None, n_past, KV_LORA), lambda b, q: (b, 0, 0)),
        pl.BlockSpec((None, ROPE_DIM, n_past), lambda b, q: (b, 0, 0)),
        pl.BlockSpec((seq, KV_LORA), lambda b, q: (b, 0)),
        pl.BlockSpec((None, ROPE_DIM, seq), lambda b, q: (b, 0, 0)),
    ]
    args = [cqn, w_q, w_ukt, w_uv, cosq, sinq, gao, past_kv, past_kr, ckv, krope]
    rows = N_HEADS * tq
    return pl.pallas_call(
        functools.partial(_attn_kernel, tq=tq, n_past=n_past),
        grid=(n_batch, 1),
        in_specs=in_specs,
        out_specs=pl.BlockSpec((tq, ATTN_W), lambda b, q: (b, 0)),
        out_shape=jax.ShapeDtypeStruct((n_batch * seq, ATTN_W), BF16),
        scratch_shapes=[
            pltpu.VMEM((rows, KV_LORA), BF16),
            pltpu.VMEM((rows, ROPE_DIM), BF16),
            pltpu.VMEM((rows, LANES), F32),
            pltpu.VMEM((rows, LANES), F32),
            pltpu.VMEM((rows, KV_LORA), F32),
            pltpu.VMEM((2, rows, TK), F32),
        ],
        compiler_params=pltpu.CompilerParams(dimension_semantics=("arbitrary", "arbitrary"),
                                             vmem_limit_bytes=VMEM_LIMIT),
        name="attn_sample",
    )(*args)


def _attn_heads_kernel(cqn_ref, wq_ref, wkv_ref, cos_ref, sin_ref, gao_ref, kv_ref, krt_ref, out_ref,
                       kcat_ref, vh_ref, qcat_ref, m_ref, l_ref, acc_ref, klim_ref, *, tq, seq):
    qi = pl.program_id(1)
    nt = (((1,), (1,)), ((), ()))
    kw = QK_NOPE + ROPE_DIM
    kpad = kcat_ref.shape[-1]

    @pl.when(qi == 0)
    def _():
        def expand(j, c):
            k0 = pl.multiple_of(j * TK, TK)
            latent = kv_ref[pl.ds(k0, TK), :].astype(BF16)
            kvh = jnp.dot(latent, wkv_ref[...], preferred_element_type=F32)
            k_rope = krt_ref[:, pl.ds(k0, TK)].T.astype(BF16)
            for h in range(N_HEADS):
                kcat_ref[h, pl.ds(k0, TK), :QK_NOPE] = kvh[:, h * QK_NOPE:(h + 1) * QK_NOPE].astype(BF16)
                kcat_ref[h, pl.ds(k0, TK), QK_NOPE:kw] = k_rope
                kcat_ref[h, pl.ds(k0, TK), kw:] = jnp.zeros((TK, kpad - kw), BF16)
                v0 = N_HEADS * QK_NOPE + h * V_DIM
                vh_ref[h, pl.ds(k0, TK), :] = kvh[:, v0:v0 + V_DIM].astype(BF16)
            return c
        lax.fori_loop(0, seq // TK, expand, 0)

    q = jnp.dot(cqn_ref[...], wq_ref[...], preferred_element_type=F32)
    nope_w = N_HEADS * QK_NOPE
    rope_w = N_HEADS * ROPE_DIM
    qrope = q[:, nope_w:nope_w + rope_w] * cos_ref[...] + q[:, nope_w + rope_w:] * sin_ref[...]
    for h in range(N_HEADS):
        qcat_ref[h, :, :QK_NOPE] = (q[:, h * QK_NOPE:(h + 1) * QK_NOPE] * EXP2_SCALE).astype(BF16)
        qcat_ref[h, :, QK_NOPE:kw] = (qrope[:, h * ROPE_DIM:(h + 1) * ROPE_DIM] * EXP2_SCALE).astype(BF16)
        qcat_ref[h, :, kw:] = jnp.zeros((tq, kpad - kw), BF16)

    r = lax.broadcasted_iota(jnp.int32, (tq, LANES), 0)
    klim_ref[...] = ((qi * tq + r) & ~(CHUNK - 1)) + CHUNK

    tk = TKH

    def block(k0, masked, first):
        if masked:
            cidx = lax.broadcasted_iota(jnp.int32, (tq, tk), 1)
            mask = cidx < _lane_bcast(klim_ref[...] - k0, tk)
        for h in range(N_HEADS):
            s = lax.dot_general(qcat_ref[h], kcat_ref[h, pl.ds(k0, tk), :], nt, preferred_element_type=F32)
            if masked:
                s = jnp.where(mask, s, NEG_BIG)
            m_cur = jnp.max(s, axis=-1, keepdims=True)
            if first:
                m_new = jnp.broadcast_to(m_cur, (tq, LANES))
            else:
                m_prev = m_ref[h]
                m_new = jnp.maximum(m_prev, m_cur)
                alpha = jnp.exp2(m_prev - m_new)
            p = jnp.exp2(s - _lane_bcast(m_new, tk))
            l_cur = jnp.sum(p, axis=-1, keepdims=True)
            pv = jnp.dot(p.astype(BF16), vh_ref[h, pl.ds(k0, tk), :], preferred_element_type=F32)
            if first:
                l_ref[h] = jnp.broadcast_to(l_cur, (tq, LANES))
                acc_ref[h] = pv
            else:
                l_ref[h] = alpha * l_ref[h] + l_cur
                acc_ref[h] = _lane_bcast(alpha, V_DIM) * acc_ref[h] + pv
            m_ref[h] = m_new

    n_blocks = ((qi + 1) * tq + tk - 1) // tk
    n_full = jnp.minimum((qi * tq // CHUNK + 1) * CHUNK // tk, n_blocks)

    def loop(lo, hi, masked):
        def body(j, c):
            block(pl.multiple_of(j * tk, tk), masked, False)
            return c
        lax.fori_loop(lo, hi, body, 0)

    block(0, True, True)
    loop(1, n_full, False)
    loop(jnp.maximum(n_full, 1), n_blocks, True)

    attn = jnp.concatenate([acc_ref[h] / _lane_bcast(l_ref[h], V_DIM) for h in range(N_HEADS)], axis=-1)
    out_ref[...] = _rms(attn, gao_ref[...]).astype(BF16)


def _attention_heads(cqn, w_q, w_kv, cosq, sinq, gao, ckv, krope_t, *, n_batch, seq):
    nq = seq // TQ
    kpad = 2 * LANES
    assert QK_NOPE + ROPE_DIM <= kpad and V_DIM == LANES
    return pl.pallas_call(
        functools.partial(_attn_heads_kernel, tq=TQ, seq=seq),
        grid=(n_batch, nq),
        in_specs=[
            pl.BlockSpec((TQ, Q_LORA), lambda b, q: (b * nq + q, 0)),
            _const_spec(w_q.shape),
            _const_spec(w_kv.shape),
            pl.BlockSpec((TQ, N_HEADS * ROPE_DIM), lambda b, q: (q, 0)),
            pl.BlockSpec((TQ, N_HEADS * ROPE_DIM), lambda b, q: (q, 0)),
            _const_spec((1, ATTN_W)),
            pl.BlockSpec((seq, KV_LORA), lambda b, q: (b, 0)),
            pl.BlockSpec((None, ROPE_DIM, seq), lambda b, q: (b, 0, 0)),
        ],
        out_specs=pl.BlockSpec((TQ, ATTN_W), lambda b, q: (b * nq + q, 0)),
        out_shape=jax.ShapeDtypeStruct((n_batch * seq, ATTN_W), BF16),
        scratch_shapes=[
            pltpu.VMEM((N_HEADS, seq, kpad), BF16),
            pltpu.VMEM((N_HEADS, seq, V_DIM), BF16),
            pltpu.VMEM((N_HEADS, TQ, kpad), BF16),
            pltpu.VMEM((N_HEADS, TQ, LANES), F32),
            pltpu.VMEM((N_HEADS, TQ, LANES), F32),
            pltpu.VMEM((N_HEADS, TQ, V_DIM), F32),
            pltpu.VMEM((TQ, LANES), jnp.int32),
        ],
        compiler_params=pltpu.CompilerParams(dimension_semantics=("arbitrary", "arbitrary"),
                                             vmem_limit_bytes=VMEM_LIMIT),
        name="attn_prompt",
    )(cqn, w_q, w_kv, cosq, sinq, gao, ckv, krope_t)


def _out_proj_kernel(attnp_ref, attns_ref, convn_ref, xp_ref, xs_ref, wo_ref, gffn_ref, wr_ref,
                     br_ref, h_ref, xpk_ref, mi_ref, mf_ref, cnt_ref, carry_ref, logit_ref, *, n_prompt_tiles):
    i = pl.program_id(0)

    @pl.when(i == 0)
    def _():
        carry_ref[...] = jnp.zeros(carry_ref.shape, F32)
        logit_ref[...] = jnp.zeros(logit_ref.shape, F32)

    def tile(x_ref, attn_ref):
        prev_logits = logit_ref[...]
        y = jnp.dot(attn_ref[...], wo_ref[:ATTN_W, :], preferred_element_type=F32)
        y = y + jnp.dot(convn_ref[...], wo_ref[ATTN_W:, :], preferred_element_type=F32)
        h = x_ref[...] + y
        h_ref[...] = h
        xn = _rms(h, gffn_ref[...])

        half = D_MODEL // 2
        xh = xn.astype(BF16)
        xh32 = xh.astype(F32)
        lo = lax.bitcast_convert_type(xh32[:, :half], jnp.uint32)
        hi = lax.bitcast_convert_type(xh32[:, half:], jnp.uint32)
        xpk_ref[...] = (lo >> 16) | (hi & jnp.uint32(0xFFFF0000))

        xl = (xn - xh32).astype(BF16)
        hh_hl = jnp.dot(xh, wr_ref[...], preferred_element_type=F32)
        lh = jnp.dot(xl, wr_ref[:, :LANES], preferred_element_type=F32)
        logit_ref[...] = hh_hl[:, :LANES] + (lh + hh_hl[:, LANES:]) + br_ref[...]

        logits = prev_logits
        counted = (i > 0).astype(F32)
        lane = lax.broadcasted_iota(jnp.int32, (TM, LANES), 1).astype(F32)
        ninf = -jnp.inf
        far = float(LANES)

        def first_argmax(v):
            vmax = jnp.max(v, axis=-1, keepdims=True)
            return vmax, jnp.min(jnp.where(v == vmax, lane, far), axis=-1, keepdims=True)

        gl = jnp.where(lane < N_GROUPS, logits, ninf)
        gmax, gidx = first_argmax(gl)
        g_p = 1.0 / jnp.sum(jnp.exp(gl - gmax), axis=-1, keepdims=True)
        e_lo = N_GROUPS + EXPERTS_PER_GROUP * gidx
        el = jnp.where((lane >= e_lo) & (lane < e_lo + EXPERTS_PER_GROUP), logits, ninf)
        e1max, i1 = first_argmax(el)
        z = jnp.sum(jnp.exp(el - e1max), axis=-1, keepdims=True)
        el2 = jnp.where(lane == i1, ninf, el)
        e2max, i2 = first_argmax(el2)
        p1 = 1.0 / z
        p2 = jnp.exp(e2max - e1max) / z
        den = p1 + p2
        g0 = g_p * p1 / den
        g1 = g_p * p2 / den
        e0 = i1 - N_GROUPS
        e1 = i2 - N_GROUPS

        oh0 = lane == e0
        oh1 = lane == e1
        oh = jnp.where(oh0 | oh1, 1.0, 0.0)
        r = lax.broadcasted_iota(jnp.int32, (TM, TM), 0)
        c = lax.broadcasted_iota(jnp.int32, (TM, TM), 1)
        ltri = jnp.where(r > c, 1.0, 0.0).astype(BF16)
        before = jnp.dot(ltri, oh.astype(BF16), preferred_element_type=F32) + carry_ref[...]
        rank0 = jnp.sum(jnp.where(oh0, before, 0.0), axis=-1, keepdims=True)
        rank1 = jnp.sum(jnp.where(oh1, before, 0.0), axis=-1, keepdims=True)
        total = carry_ref[...] + counted * jnp.sum(oh, axis=0, keepdims=True)
        carry_ref[...] = total
        cnt_ref[...] = jnp.broadcast_to(total, cnt_ref.shape)

        mi = jnp.where(lane == 0, e0, jnp.where(lane == 1, e1, jnp.where(lane == 2, rank0, rank1)))
        mi_ref[...] = jnp.transpose(mi)[:SUBLANES, :].astype(jnp.int32)
        mf_ref[...] = jnp.where(lane == 0, g0, g1)

    @pl.when(i < n_prompt_tiles)
    def _():
        tile(xp_ref, attnp_ref)

    @pl.when(i >= n_prompt_tiles)
    def _():
        tile(xs_ref, attns_ref)


def _out_proj(attn_p, attn_s, conv_n, xp, xs, w_ob, gffn, w_r2, b_r):
    m = conv_n.shape[0]
    npt = xp.shape[0] // TM
    n_tiles = m // TM
    last_p, last_s, last = npt - 1, n_tiles - npt - 1, n_tiles - 1
    row = lambda i: (jnp.minimum(i, last), 0)
    prow = lambda i: (jnp.minimum(i, last_p), 0)
    srow = lambda i: (jnp.clip(i - npt, 0, last_s), 0)
    lag = lambda i: jnp.maximum(i - 1, 0)
    return pl.pallas_call(
        functools.partial(_out_proj_kernel, n_prompt_tiles=npt),
        grid=(n_tiles + 1,),
        in_specs=[
            pl.BlockSpec((TM, ATTN_W), prow),
            pl.BlockSpec((TM, ATTN_W), srow),
            pl.BlockSpec((TM, CONV_CH), row),
            pl.BlockSpec((TM, D_MODEL), prow),
            pl.BlockSpec((TM, D_MODEL), srow),
            _const_spec(w_ob.shape),
            _const_spec((1, D_MODEL)),
            _const_spec(w_r2.shape),
            _const_spec((1, LANES)),
        ],
        out_specs=[
            pl.BlockSpec((TM, D_MODEL), row),
            pl.BlockSpec((TM, D_MODEL // 2), row),
            pl.BlockSpec((SUBLANES, TM), lambda i: (0, lag(i))),
            pl.BlockSpec((TM, LANES), lambda i: (lag(i), 0)),
            pl.BlockSpec((SUBLANES, LANES), lambda i: (0, 0)),
        ],
        out_shape=[
            jax.ShapeDtypeStruct((m, D_MODEL), F32),
            jax.ShapeDtypeStruct((m, D_MODEL // 2), jnp.uint32),
            jax.ShapeDtypeStruct((SUBLANES, m), jnp.int32),
            jax.ShapeDtypeStruct((m, LANES), F32),
            jax.ShapeDtypeStruct((SUBLANES, LANES), F32),
        ],
        scratch_shapes=[pltpu.VMEM((1, LANES), F32), pltpu.VMEM((TM, LANES), F32)],
        compiler_params=pltpu.CompilerParams(dimension_semantics=("arbitrary",),
                                             vmem_limit_bytes=VMEM_LIMIT),
        name="out_proj",
    )(attn_p, attn_s, conv_n, xp, xs, w_ob, gffn, w_r2, b_r)


def _dispatch_kernel(d0_ref, d1_ref, zlo_ref, zn_ref, nu_ref, xpk_ref, xs_hbm, zeros_ref, sems, *, n_blocks):
    i = pl.program_id(0)
    sem = sems.at[0]
    zsem = sems.at[1]

    def zero_fill(act):
        def per_expert(e, c):
            lo = zlo_ref[e]
            n = zn_ref[e]
            head = (-lo) & (SUBLANES - 1)
            for r in range(SUBLANES - 1):
                @pl.when(r < head)
                def _(r=r):
                    act(pltpu.make_async_copy(zeros_ref.at[pl.ds(0, 1)], xs_hbm.at[pl.ds(lo + r, 1)], zsem))
            off = lo + head
            rest = n - head
            size = MOE_BLOCK // 2
            while size >= SUBLANES:
                @pl.when((rest & size) != 0)
                def _(off=off, size=size):
                    dst = xs_hbm.at[pl.ds(pl.multiple_of(off, SUBLANES), size)]
                    act(pltpu.make_async_copy(zeros_ref.at[pl.ds(0, size)], dst, zsem))
                off = off + (rest & size)
                size //= 2
            return c

        def per_block(b, c):
            dst = xs_hbm.at[pl.ds(pl.multiple_of(b * MOE_BLOCK, MOE_BLOCK), MOE_BLOCK)]
            act(pltpu.make_async_copy(zeros_ref, dst, zsem))
            return c

        lax.fori_loop(0, N_EXPERTS, per_expert, 0)
        lax.fori_loop(nu_ref[0], n_blocks, per_block, 0)

    @pl.when(i == 0)
    def _():
        zeros_ref[...] = jnp.zeros(zeros_ref.shape, zeros_ref.dtype)
        zero_fill(lambda cp: cp.start())

    @pl.when(i == pl.num_programs(0) - 1)
    def _():
        zero_fill(lambda cp: cp.wait())

    base = i * TD

    def start(g, c):
        for u in range(SUBLANES):
            r = base + g * SUBLANES + u
            src = xpk_ref.at[g, pl.ds(u, 1)]
            pltpu.make_async_copy(src, xs_hbm.at[pl.ds(d0_ref[r], 1)], sem).start()
            pltpu.make_async_copy(src, xs_hbm.at[pl.ds(d1_ref[r], 1)], sem).start()
        return c

    lax.fori_loop(0, TD // SUBLANES, start, 0)
    for _ in range(2):
        pltpu.make_async_copy(xs_hbm.at[pl.ds(0, TD)], xs_hbm.at[pl.ds(0, TD)], sem).wait()


def _dispatch(dest0, dest1, pad_lo, n_pad, n_used, xpk, n_blocks):
    m = xpk.shape[0]
    grid_spec = pltpu.PrefetchScalarGridSpec(
        num_scalar_prefetch=5,
        grid=(m // TD,),
        in_specs=[pl.BlockSpec((TD // SUBLANES, SUBLANES, D_MODEL // 2), lambda i, *_: (i, 0, 0))],
        out_specs=pl.BlockSpec(memory_space=pl.ANY),
        scratch_shapes=[pltpu.VMEM((MOE_BLOCK, D_MODEL // 2), jnp.uint32),
                        pltpu.SemaphoreType.DMA((2,))],
    )
    return pl.pallas_call(
        functools.partial(_dispatch_kernel, n_blocks=n_blocks),
        grid_spec=grid_spec,
        out_shape=jax.ShapeDtypeStruct((n_blocks * MOE_BLOCK, D_MODEL // 2), jnp.uint32),
        compiler_params=pltpu.CompilerParams(dimension_semantics=("arbitrary",)),
        name="dispatch",
    )(dest0, dest1, pad_lo, n_pad, n_used, xpk.reshape(m // SUBLANES, SUBLANES, D_MODEL // 2))


def _experts_kernel(be_ref, nu_ref, nxt_ref, x_ref, wg_hbm, wu_hbm, wd_hbm, y_ref,
                    sg_ref, su_ref, sd_ref, wgb_ref, wub_ref, wdb_ref, sems):
    b = pl.program_id(0)
    active = b < nu_ref[0]
    new_expert = jnp.logical_or(b == 0, be_ref[b] != be_ref[jnp.maximum(b - 1, 0)])

    def weight_copies(e):
        return (pltpu.make_async_copy(wg_hbm.at[e], sg_ref, sems.at[0]),
                pltpu.make_async_copy(wu_hbm.at[e], su_ref, sems.at[1]),
                pltpu.make_async_copy(wd_hbm.at[e], sd_ref, sems.at[2]))

    @pl.when(b == 0)
    def _():
        for cp in weight_copies(be_ref[0]):
            cp.start()

    @pl.when(jnp.logical_and(active, new_expert))
    def _():
        for cp in weight_copies(be_ref[b]):
            cp.wait()
        wgb_ref[...] = sg_ref[...].astype(BF16)
        wub_ref[...] = su_ref[...].astype(BF16)
        wdb_ref[...] = sd_ref[...].astype(BF16)

        @pl.when(nxt_ref[b] >= 0)
        def _():
            for cp in weight_copies(nxt_ref[b]):
                cp.start()

    @pl.when(active)
    def _():
        half = D_MODEL // 2
        xa, xb = (v.astype(BF16) for v in _unpack_bf16_pairs(x_ref[...]))
        g = jnp.dot(xa, wgb_ref[:half, :], preferred_element_type=F32)
        g = g + jnp.dot(xb, wgb_ref[half:, :], preferred_element_type=F32)
        u = jnp.dot(xa, wub_ref[:half, :], preferred_element_type=F32)
        u = u + jnp.dot(xb, wub_ref[half:, :], preferred_element_type=F32)
        hmid = (g * jax.nn.sigmoid(g)) * u
        y = jnp.dot(hmid.astype(BF16), wdb_ref[...], preferred_element_type=F32)
        y_ref[...] = _pack_bf16_pairs(y)

    @pl.when(b >= nu_ref[0])
    def _():
        y_ref[...] = jnp.zeros(y_ref.shape, y_ref.dtype)


def _experts(block_e, n_used, next_e, x_sorted, w_gate, w_up, w_down):
    p = x_sorted.shape[0]
    nb = p // MOE_BLOCK

    def xrow(b, be, nu, nxt):
        return (jnp.maximum(jnp.minimum(b, nu[0] - 1), 0), 0)

    grid_spec = pltpu.PrefetchScalarGridSpec(
        num_scalar_prefetch=3,
        grid=(nb,),
        in_specs=[
            pl.BlockSpec((MOE_BLOCK, D_MODEL // 2), xrow),
            pl.BlockSpec(memory_space=pl.ANY),
            pl.BlockSpec(memory_space=pl.ANY),
            pl.BlockSpec(memory_space=pl.ANY),
        ],
        out_specs=pl.BlockSpec((MOE_BLOCK, D_MODEL // 2), lambda b, be, nu, nxt: (b, 0)),
        scratch_shapes=[pltpu.VMEM((D_MODEL, D_FF), F32), pltpu.VMEM((D_MODEL, D_FF), F32),
                        pltpu.VMEM((D_FF, D_MODEL), F32),
                        pltpu.VMEM((D_MODEL, D_FF), BF16), pltpu.VMEM((D_MODEL, D_FF), BF16),
                        pltpu.VMEM((D_FF, D_MODEL), BF16),
                        pltpu.SemaphoreType.DMA((3,))],
    )
    return pl.pallas_call(
        _experts_kernel,
        grid_spec=grid_spec,
        out_shape=jax.ShapeDtypeStruct((p, D_MODEL // 2), jnp.uint32),
        compiler_params=pltpu.CompilerParams(dimension_semantics=("arbitrary",),
                                             vmem_limit_bytes=VMEM_LIMIT),
        name="experts",
    )(block_e, n_used, next_e, x_sorted, w_gate, w_up, w_down)


def _combine_kernel(d0_ref, d1_ref, h_ref, mf_ref, gfin_ref, y_hbm, outp_ref, outs_ref, y0_ref, y1_ref, sems,
                    *, n_tiles, n_prompt_tiles):
    i = pl.program_id(0)

    def gather(tile, slot, act):
        base = tile * TM

        def body(g, c):
            for u in range(SUBLANES):
                r = base + g * SUBLANES + u
                act(pltpu.make_async_copy(y_hbm.at[pl.ds(d0_ref[r], 1)], y0_ref.at[slot, g, pl.ds(u, 1)],
                                          sems.at[slot]))
                act(pltpu.make_async_copy(y_hbm.at[pl.ds(d1_ref[r], 1)], y1_ref.at[slot, g, pl.ds(u, 1)],
                                          sems.at[slot]))
            return c
        lax.fori_loop(0, TM // SUBLANES, body, 0)

    @pl.when(i == 0)
    def _():
        gather(0, 0, lambda cp: cp.start())

    @pl.when(i + 1 < n_tiles)
    def _():
        gather(i + 1, (i + 1) % 2, lambda cp: cp.start())

    slot = i % 2
    for _ in range(2):
        pltpu.make_async_copy(y_hbm.at[pl.ds(0, TM)], y_hbm.at[pl.ds(0, TM)], sems.at[slot]).wait()

    def finish(out_ref):
        mf = mf_ref[...]
        g0, g1 = mf[:, :, 0:1], mf[:, :, 1:2]
        half = D_MODEL // 2
        a0, b0 = _unpack_bf16_pairs(y0_ref[slot])
        a1, b1 = _unpack_bf16_pairs(y1_ref[slot])
        o_lo = h_ref[:, :, :half] + (g0 * a0 + g1 * a1)
        o_hi = h_ref[:, :, half:] + (g0 * b0 + g1 * b1)
        sumsq = jnp.sum(o_lo * o_lo, axis=-1, keepdims=True) + jnp.sum(o_hi * o_hi, axis=-1, keepdims=True)
        inv_rms = lax.rsqrt(sumsq / D_MODEL + EPS)
        out_ref[:, :, :half] = o_lo * inv_rms * gfin_ref[:, :, :half]
        out_ref[:, :, half:] = o_hi * inv_rms * gfin_ref[:, :, half:]

    @pl.when(i < n_prompt_tiles)
    def _():
        finish(outp_ref)

    @pl.when(i >= n_prompt_tiles)
    def _():
        finish(outs_ref)


def _combine(dest0, dest1, h, mf, gfin, y_sorted, *, n_prompt_rows):
    m = h.shape[0]
    npt = n_prompt_rows // TM
    tg = TM // SUBLANES
    grouped = lambda a: a.reshape(a.shape[0] // SUBLANES, SUBLANES, a.shape[1])
    grid_spec = pltpu.PrefetchScalarGridSpec(
        num_scalar_prefetch=2,
        grid=(m // TM,),
        in_specs=[
            pl.BlockSpec((tg, SUBLANES, D_MODEL), lambda i, *_: (i, 0, 0)),
            pl.BlockSpec((tg, SUBLANES, LANES), lambda i, *_: (i, 0, 0)),
            pl.BlockSpec((1, 1, D_MODEL), lambda i, *_: (0, 0, 0)),
            pl.BlockSpec(memory_space=pl.ANY),
        ],
        out_specs=[pl.BlockSpec((tg, SUBLANES, D_MODEL), lambda i, *_: (jnp.minimum(i, npt - 1), 0, 0)),
                   pl.BlockSpec((tg, SUBLANES, D_MODEL), lambda i, *_: (jnp.maximum(i - npt, 0), 0, 0))],
        scratch_shapes=[pltpu.VMEM((2, tg, SUBLANES, D_MODEL // 2), jnp.uint32),
                        pltpu.VMEM((2, tg, SUBLANES, D_MODEL // 2), jnp.uint32),
                        pltpu.SemaphoreType.DMA((2,))],
    )
    y_p, y_s = pl.pallas_call(
        functools.partial(_combine_kernel, n_tiles=m // TM, n_prompt_tiles=npt),
        grid_spec=grid_spec,
        out_shape=[jax.ShapeDtypeStruct((n_prompt_rows // SUBLANES, SUBLANES, D_MODEL), F32),
                   jax.ShapeDtypeStruct(((m - n_prompt_rows) // SUBLANES, SUBLANES, D_MODEL), F32)],
        compiler_params=pltpu.CompilerParams(dimension_semantics=("arbitrary",),
                                             vmem_limit_bytes=VMEM_LIMIT),
        name="combine",
    )(dest0, dest1, grouped(h), grouped(mf), gfin.reshape(1, 1, D_MODEL), y_sorted)
    return y_p.reshape(n_prompt_rows, D_MODEL), y_s.reshape(m - n_prompt_rows, D_MODEL)


def _rope_tables(pos):
    f32 = np.float32
    inv = np.power(f32(ROPE_THETA), -np.arange(0, ROPE_DIM, 2, dtype=f32) / f32(ROPE_DIM)).astype(f32)
    ang = (pos.astype(f32)[:, None] * inv[None, :]).astype(f32)
    cos, sin = np.cos(ang).astype(f32), np.sin(ang).astype(f32)
    return np.concatenate([cos, cos], axis=-1), np.concatenate([-sin, sin], axis=-1)


def _swap_halves(w):
    return jnp.concatenate([w[..., ROPE_DIM // 2:], w[..., :ROPE_DIM // 2]], axis=-1)


def kernel(x_prompt, x_sample, cache_kv_latent, cache_k_rope, state_conv, norm_mix, w_in, norm_q, w_uq,
           norm_kv, w_uk, w_uv, conv_w, norm_attn_out, norm_conv_out, w_o, norm_ffn, w_router_group,
           b_router_group, w_router_expert, b_router_expert, w_gate, w_up, w_down, norm_final):
    assert w_in.shape[0] == 1, "single-layer trunk"
    bp, seq_p, _ = x_prompt.shape
    bs, seq_s, _ = x_sample.shape
    past_len = cache_kv_latent.shape[2]
    np_rows, ns_rows = bp * seq_p, bs * seq_s
    m = np_rows + ns_rows
    assert seq_p % TM == 0 and TM % seq_s == 0 and ns_rows % TM == 0 and seq_s == CHUNK
    assert m % TD == 0

    xp = x_prompt.reshape(np_rows, D_MODEL)
    xs = x_sample.reshape(ns_rows, D_MODEL)
    row_vec = lambda v: v.reshape(1, -1)

    assert w_in.shape[2] == Q_LORA + KV_LORA + ROPE_DIM + 3 * CONV_CH
    w_t = jnp.swapaxes(w_in[0], 0, 1)
    assert w_t.shape[0] % (W_IN_SLABS * 2 * SUBLANES) == 0
    wq4 = w_uq[0].reshape(Q_LORA, N_HEADS, QK_NOPE + ROPE_DIM)
    wq_rope = wq4[:, :, QK_NOPE:]
    w_q = jnp.concatenate([wq4[:, :, :QK_NOPE].reshape(Q_LORA, -1), wq_rope.reshape(Q_LORA, -1),
                           _swap_halves(wq_rope).reshape(Q_LORA, -1)], axis=1).astype(BF16)
    w_ukt = jnp.transpose(w_uk[0], (1, 2, 0)).astype(BF16)
    w_uvh = jnp.transpose(w_uv[0], (1, 0, 2)).astype(BF16)
    w_ob = w_o[0].astype(BF16)
    n_router = N_GROUPS + N_EXPERTS
    w_r = jnp.concatenate([w_router_group[0], w_router_expert[0].reshape(D_MODEL, N_EXPERTS)], axis=1)
    w_r = jnp.pad(w_r, ((0, 0), (0, LANES - n_router)))
    w_rh = w_r.astype(BF16)
    w_rl = (w_r - w_rh.astype(F32)).astype(BF16)
    w_r2 = jnp.concatenate([w_rh, w_rl], axis=1)
    b_r =jnp.pad(jnp.concatenate([b_router_group[0], b_router_expert[0].reshape(N_EXPERTS)]),
                  (0, LANES - n_router)).reshape(1, LANES)

    cos_p, sin_p = _rope_tables(np.arange(seq_p))
    cos_s, sin_s = _rope_tables(past_len + np.arange(seq_s))
    cosk = np.concatenate([cos_p, np.tile(cos_s, (TM // seq_s, 1))], axis=0)
    sink = np.concatenate([sin_p, np.tile(sin_s, (TM // seq_s, 1))], axis=0)
    state = jnp.concatenate([jnp.zeros((bp, CONV_W - 1, CONV_CH), F32), state_conv[0]], axis=0)

    cqn, ckv_p, kr_p, ckv_s, kr_s, conv_n, utail = _in_proj(
        xp, xs, row_vec(norm_mix[0]), w_t, row_vec(norm_q[0]), row_vec(norm_kv[0]),
        row_vec(norm_conv_out[0]), conv_w[0], cosk, sink, state, seq_p=seq_p, seq_s=seq_s)

    gao = row_vec(norm_attn_out[0])
    w_kv = jnp.concatenate([w_uk[0].reshape(KV_LORA, N_HEADS * QK_NOPE),
                            w_uv[0].reshape(KV_LORA, N_HEADS * V_DIM)], axis=1).astype(BF16)
    attn_p = _attention_heads(cqn, w_q, w_kv, np.tile(cos_p, (1, N_HEADS)), np.tile(sin_p, (1, N_HEADS)),
                              gao, ckv_p, kr_p, n_batch=bp, seq=seq_p)
    attn_s = _attention(cqn, w_q, w_ukt, w_uvh, np.tile(cos_s, (1, N_HEADS)), np.tile(sin_s, (1, N_HEADS)),
                        gao, ckv_s, kr_s, cache_kv_latent[0], jnp.swapaxes(cache_k_rope[0], 1, 2),
                        n_batch=bs, seq=seq_s, row0=np_rows)

    h, xpk, mi, mf, cnt = _out_proj(attn_p, attn_s, conv_n, xp, xs, w_ob, row_vec(norm_ffn[0]),
                                    w_r2, b_r)

    counts = cnt[0, :N_EXPERTS].astype(jnp.int32)
    padded = (counts + MOE_BLOCK - 1) // MOE_BLOCK * MOE_BLOCK
    pad_end = jnp.cumsum(padded)
    pad_start = pad_end - padded
    n_blocks = -(-(m * 2) // MOE_BLOCK) + N_EXPERTS
    block_row0 = jnp.arange(n_blocks, dtype=jnp.int32) * MOE_BLOCK
    block_e = jnp.minimum(jnp.sum((pad_end[None, :] <= block_row0[:, None]).astype(jnp.int32), axis=1),
                          N_EXPERTS - 1)
    n_used = (pad_end[-1:] // MOE_BLOCK).astype(jnp.int32)
    expert_ids = jnp.arange(N_EXPERTS, dtype=jnp.int32)[:, None]

    def seg_start(e):
        return jnp.sum(jnp.where(expert_ids == e[None, :], pad_start[:, None], 0), axis=0)

    dest0 = seg_start(mi[0]) + mi[2]
    dest1 = seg_start(mi[1]) + mi[3]

    x_sorted = _dispatch(dest0, dest1, pad_start + counts, padded - counts, n_used, xpk, n_blocks)
    later = (expert_ids.T > block_e[:, None]) & (padded > 0)[None, :]
    next_e = jnp.min(jnp.where(later, expert_ids.T, N_EXPERTS), axis=1)
    next_e = jnp.where(next_e == N_EXPERTS, -1, next_e).astype(jnp.int32)
    y_sorted = _experts(block_e, n_used, next_e, x_sorted, w_gate[0], w_up[0], w_down[0])
    gfin = row_vec(norm_final)
    y_p, y_s = _combine(dest0, dest1, h, mf, gfin, y_sorted, n_prompt_rows=np_rows)

    ut = utail.reshape(m // CHUNK, SUBLANES, CONV_CH)
    tails = ut[:, SUBLANES - (CONV_W - 1):, :]
    p_last = (jnp.arange(bp) + 1) * (seq_p // CHUNK) - 1
    s_last = np_rows // CHUNK + (jnp.arange(bs) + 1) * (seq_s // CHUNK) - 1
    return (y_p.reshape(bp, seq_p, D_MODEL),
            y_s.reshape(bs, seq_s, D_MODEL),
            ckv_p.reshape(1, bp, seq_p, KV_LORA),
            jnp.swapaxes(kr_p, 1, 2)[None],
            tails[p_last][None],
            ckv_s.reshape(1, bs, seq_s, KV_LORA),
            jnp.swapaxes(kr_s, 1, 2)[None],
            tails[s_last][None])
```

```python
import functools

import jax
import jax.numpy as jnp
import numpy as np
from jax import lax
from jax.experimental import pallas as pl
from jax.experimental.pallas import tpu as pltpu

F32 = jnp.float32
BF16 = jnp.bfloat16

D_MODEL = 2048
N_HEADS = 8
QK_NOPE = 128
ROPE_DIM = 64
V_DIM = 128
Q_LORA = 512
KV_LORA = 512
ATTN_W = N_HEADS * V_DIM
CONV_CH = D_MODEL - ATTN_W
CONV_W = 3
CHUNK = 64
N_GROUPS = 4
EXPERTS_PER_GROUP = 8
N_EXPERTS = N_GROUPS * EXPERTS_PER_GROUP
D_FF = 512
ROPE_THETA = 10000.0
EPS = 1e-6
ATTN_SCALE = (QK_NOPE + ROPE_DIM) ** -0.5
EXP2_SCALE = ATTN_SCALE * 1.4426950408889634

LANES = 128
SUBLANES = 8
TM = 256
TD = 2304
MOE_BLOCK = 256
TQ = 512
TK = 256
TKH = 512
W_O_SLABS = 4
W_IN_SLABS = 10
NEG_BIG = -1e30
V7X_VMEM_BYTES = 64 * 1024 * 1024
VMEM_LIMIT = V7X_VMEM_BYTES * 7 // 8


def _rms(v, g):
    return v * lax.rsqrt(jnp.mean(v * v, axis=-1, keepdims=True) + EPS) * g


def _lane_bcast(v, width):
    if width % LANES == 0:
        return jnp.concatenate([v] * (width // LANES), axis=1)
    assert width < LANES
    return v[:, :width]


def _pack_bf16_pairs(v):
    half = v.shape[-1] // 2
    lo = lax.bitcast_convert_type(v[..., :half].astype(BF16).astype(F32), jnp.uint32)
    hi = lax.bitcast_convert_type(v[..., half:].astype(BF16).astype(F32), jnp.uint32)
    return (lo >> 16) | (hi & jnp.uint32(0xFFFF0000))


def _unpack_bf16_pairs(w):
    return (lax.bitcast_convert_type(w << 16, F32),
            lax.bitcast_convert_type(w & jnp.uint32(0xFFFF0000), F32))


def _load_weight_as_bf16(w_hbm, dst_ref, stage_ref, sems):
    rows = stage_ref.shape[1]
    n_slabs = dst_ref.shape[0] // rows

    def slab_copy(c):
        return pltpu.make_async_copy(w_hbm.at[pl.ds(c * rows, rows)], stage_ref.at[c % 2], sems.at[c % 2])

    slab_copy(0).start()
    for c in range(n_slabs):
        if c + 1 < n_slabs:
            slab_copy(c + 1).start()
        slab_copy(c).wait()
        dst_ref[c * rows:(c + 1) * rows, :] = stage_ref[c % 2].astype(BF16)


def _const_spec(shape):
    nd = len(shape)
    return pl.BlockSpec(shape, lambda *_: (0,) * nd, pipeline_mode=pl.Buffered(1))


def _in_proj_kernel(xp_ref, xs_ref, gmix_ref, w_hbm, gq_ref, gkv_ref, gco_ref, convw_ref,
                    cos_ref, sin_ref, state_ref,
                    cqn_ref, ckvp_ref, krp_ref, ckvs_ref, krs_ref, convn_ref, utail_ref,
                    ext_ref, wt_ref, stage_ref, wsems,
                    *, n_prompt_tiles, tiles_per_seq, n_prompt_seq, sample_seq_len):
    i = pl.program_id(0)

    @pl.when(i == 0)
    def _():
        ext_ref[...] = jnp.zeros(ext_ref.shape, F32)
        _load_weight_as_bf16(w_hbm, wt_ref, stage_ref, wsems)

    def conv_block(u_sub, gate_sub, row0, length):
        ext_ref[SUBLANES:SUBLANES + length, :] = u_sub
        um1 = ext_ref[SUBLANES - 1:SUBLANES - 1 + length, :]
        um2 = ext_ref[SUBLANES - 2:SUBLANES - 2 + length, :]
        cw = convw_ref[...]
        conv = cw[0:1] * um2 + cw[1:2] * um1 + cw[2:3] * u_sub
        convn_ref[row0:row0 + length, :] = _rms(gate_sub * conv, gco_ref[...]).astype(BF16)

    def tile(x_ref, is_prompt):
        ckv_ref, krt_ref = (ckvp_ref, krp_ref) if is_prompt else (ckvs_ref, krs_ref)
        x = x_ref[...]
        xg = (x * gmix_ref[...]).astype(BF16)
        inv_rms = lax.rsqrt(jnp.mean(x * x, axis=-1, keepdims=True) + EPS)
        lat_w = Q_LORA + KV_LORA
        conv0 = lat_w + ROPE_DIM
        nt = (((1,), (1,)), ((), ()))

        def project(lo, hi):
            return inv_rms * lax.dot_general(xg, wt_ref[lo:hi, :], nt, preferred_element_type=F32)

        z_ch = project(conv0 + CONV_CH, conv0 + 3 * CONV_CH)
        u = z_ch[:, :CONV_CH] * z_ch[:, CONV_CH:]
        for j in range(TM // CHUNK):
            utail_ref[j] = u[CHUNK * (j + 1) - SUBLANES:CHUNK * (j + 1), :]
        gate_b = project(conv0, conv0 + CONV_CH)

        if is_prompt:
            first = (i % tiles_per_seq) == 0
            carried = ext_ref[TM + SUBLANES - 2:TM + SUBLANES, :]
            ext_ref[SUBLANES - 2:SUBLANES, :] = jnp.where(first, state_ref[i // tiles_per_seq], carried)
            conv_block(u, gate_b, 0, TM)
        else:
            n_sub = TM // sample_seq_len
            seq0 = n_prompt_seq + (i - n_prompt_tiles) * n_sub
            for k in range(n_sub):
                ext_ref[SUBLANES - 2:SUBLANES, :] = state_ref[seq0 + k]
                lo = k * sample_seq_len
                conv_block(u[lo:lo + sample_seq_len], gate_b[lo:lo + sample_seq_len], lo, sample_seq_len)

        zk = project(lat_w, conv0)
        zk_swapped = jnp.concatenate([zk[:, ROPE_DIM // 2:], zk[:, :ROPE_DIM // 2]], axis=1)
        k_rope = zk * cos_ref[...] + zk_swapped * sin_ref[...]
        if is_prompt:
            krt_ref[...] = k_rope.T
        else:
            for k in range(TM // sample_seq_len):
                krt_ref[k] = k_rope[k * sample_seq_len:(k + 1) * sample_seq_len, :].T
        ckv_ref[...] = _rms(project(Q_LORA, lat_w), gkv_ref[...])
        cqn_ref[...] = _rms(project(0, Q_LORA), gq_ref[...]).astype(BF16)

    @pl.when(i < n_prompt_tiles)
    def _():
        tile(xp_ref, True)

    @pl.when(i >= n_prompt_tiles)
    def _():
        tile(xs_ref, False)


def _in_proj(xp, xs, gmix, w_t, gq, gkv, gco, convw, cosk, sink, state, *, seq_p, seq_s):
    np_rows, ns_rows = xp.shape[0], xs.shape[0]
    m = np_rows + ns_rows
    npt, nst = np_rows // TM, ns_rows // TM
    tps = seq_p // TM
    n_prompt_seq = np_rows // seq_p
    last_p = npt - 1

    def tab_idx(i):
        return (jnp.where(i < npt, i % tps, tps), 0)

    row = lambda i: (i, 0)
    prow = lambda i: (jnp.minimum(i, last_p), 0)
    srow = lambda i: (jnp.maximum(i - npt, 0), 0)
    kern = functools.partial(_in_proj_kernel, n_prompt_tiles=npt, tiles_per_seq=tps,
                             n_prompt_seq=n_prompt_seq, sample_seq_len=seq_s)
    return pl.pallas_call(
        kern,
        grid=(npt + nst,),
        in_specs=[
            pl.BlockSpec((TM, D_MODEL), prow),
            pl.BlockSpec((TM, D_MODEL), srow),
            _const_spec((1, D_MODEL)),
            pl.BlockSpec(memory_space=pl.ANY),
            _const_spec((1, Q_LORA)),
            _const_spec((1, KV_LORA)),
            _const_spec((1, CONV_CH)),
            _const_spec((CONV_W, CONV_CH)),
            pl.BlockSpec((TM, ROPE_DIM), tab_idx),
            pl.BlockSpec((TM, ROPE_DIM), tab_idx),
            _const_spec(state.shape),
        ],
        out_specs=[
            pl.BlockSpec((TM, Q_LORA), row),
            pl.BlockSpec((TM, KV_LORA), prow),
            pl.BlockSpec((None, ROPE_DIM, TM), lambda i: (jnp.minimum(i, last_p) // tps, 0,
                                                          jnp.minimum(i, last_p) % tps)),
            pl.BlockSpec((TM, KV_LORA), srow),
            pl.BlockSpec((TM // seq_s, ROPE_DIM, seq_s), lambda i: (jnp.maximum(i - npt, 0), 0, 0)),
            pl.BlockSpec((TM, CONV_CH), row),
            pl.BlockSpec((TM // CHUNK, SUBLANES, CONV_CH), lambda i: (i, 0, 0)),
        ],
        out_shape=[
            jax.ShapeDtypeStruct((m, Q_LORA), BF16),
            jax.ShapeDtypeStruct((np_rows, KV_LORA), F32),
            jax.ShapeDtypeStruct((n_prompt_seq, ROPE_DIM, seq_p), F32),
            jax.ShapeDtypeStruct((ns_rows, KV_LORA), F32),
            jax.ShapeDtypeStruct((ns_rows // seq_s, ROPE_DIM, seq_s), F32),
            jax.ShapeDtypeStruct((m, CONV_CH), BF16),
            jax.ShapeDtypeStruct((m // CHUNK, SUBLANES, CONV_CH), F32),
        ],
        scratch_shapes=[pltpu.VMEM((TM + SUBLANES, CONV_CH), F32),
                        pltpu.VMEM(w_t.shape, BF16),
                        pltpu.VMEM((2, w_t.shape[0] // W_IN_SLABS, w_t.shape[1]), F32),
                        pltpu.SemaphoreType.DMA((2,))],
        compiler_params=pltpu.CompilerParams(dimension_semantics=("arbitrary",),
                                             vmem_limit_bytes=VMEM_LIMIT),
        name="in_proj",
    )(xp, xs, gmix, w_t, gq, gkv, gco, convw, cosk, sink, state)


def _attn_kernel(cqn_ref, wq_ref, wuk_ref, wuv_ref, cos_ref, sin_ref, gao_ref, pkv_ref, pkr_ref, kv_ref, kr_ref,
                 out_ref, qlat_ref, qr_ref, m_ref, l_ref, acc_ref, s_ref, *, tq, n_past):
    rows = N_HEADS * tq

    q = jnp.dot(cqn_ref[...], wq_ref[...], preferred_element_type=F32)
    nope_w = N_HEADS * QK_NOPE
    rope_w = N_HEADS * ROPE_DIM
    qrope = q[:, nope_w:nope_w + rope_w] * cos_ref[...] + q[:, nope_w + rope_w:] * sin_ref[...]
    for h in range(N_HEADS):
        qn = q[:, h * QK_NOPE:(h + 1) * QK_NOPE].astype(BF16)
        ql = jnp.dot(qn, wuk_ref[h], preferred_element_type=F32)
        qlat_ref[h * tq:(h + 1) * tq, :] = ql.astype(BF16)
        qr_ref[h * tq:(h + 1) * tq, :] = qrope[:, h * ROPE_DIM:(h + 1) * ROPE_DIM].astype(BF16)


    nt = (((1,), (1,)), ((), ()))

    def scores(kc_f32, krt_f32):
        s = lax.dot_general(qlat_ref[...], kc_f32.astype(BF16), nt, preferred_element_type=F32)
        return s + jnp.dot(qr_ref[...], krt_f32.astype(BF16), preferred_element_type=F32)

    def update(s, kc_f32, mask, first=False):
        if mask is not None:
            s = jnp.where(mask, s, NEG_BIG)
        m_cur = jnp.max(s, axis=-1, keepdims=True)
        if first:
            m_new = jnp.broadcast_to(m_cur, m_ref.shape)
        else:
            m_prev = m_ref[...]
            m_new = jnp.maximum(m_prev, m_cur)
            alpha = jnp.exp2((m_prev - m_new) * EXP2_SCALE)
        p = jnp.exp2((s - _lane_bcast(m_new, s.shape[1])) * EXP2_SCALE)
        l_cur = jnp.sum(p, axis=-1, keepdims=True)
        pv = jnp.dot(p.astype(BF16), kc_f32.astype(BF16), preferred_element_type=F32)
        if first:
            l_ref[...] = jnp.broadcast_to(l_cur, l_ref.shape)
            acc_ref[...] = pv
        else:
            l_ref[...] = alpha * l_ref[...] + l_cur
            acc_ref[...] = _lane_bcast(alpha, KV_LORA) * acc_ref[...] + pv
        m_ref[...] = m_new

    def pipelined(kv, kr, lo, hi, last, mask_fn):
        def body(j, c):
            k0 = pl.multiple_of(j * TK, TK)
            k1 = pl.multiple_of(jnp.minimum(j + 1, last) * TK, TK)
            s_cur = s_ref[j % 2]
            s_ref[(j + 1) % 2] = scores(kv[pl.ds(k1, TK), :], kr[:, pl.ds(k1, TK)])
            update(s_cur, kv[pl.ds(k0, TK), :], None if mask_fn is None else mask_fn(k0))
            return c
        lax.fori_loop(lo, hi, body, 0)

    def pipelined_pairs(kv, kr, n_pairs, last):
        def body(i, c):
            ka = pl.multiple_of((2 * i + 1) * TK, TK)
            kb = pl.multiple_of((2 * i + 2) * TK, TK)
            kc = pl.multiple_of(jnp.minimum(2 * i + 3, last) * TK, TK)
            s_ref[0] = scores(kv[pl.ds(kb, TK), :], kr[:, pl.ds(kb, TK)])
            update(s_ref[1], kv[pl.ds(ka, TK), :], None)
            s_ref[1] = scores(kv[pl.ds(kc, TK), :], kr[:, pl.ds(kc, TK)])
            update(s_ref[0], kv[pl.ds(kb, TK), :], None)
            return c
        lax.fori_loop(0, n_pairs, body, 0)

    def first_block(kv, kr, last, mask):
        k1 = pl.multiple_of(jnp.minimum(1, last) * TK, TK)
        s_ref[0] = scores(kv[pl.ds(0, TK), :], kr[:, pl.ds(0, TK)])
        s_ref[1] = scores(kv[pl.ds(k1, TK), :], kr[:, pl.ds(k1, TK)])
        update(s_ref[0], kv[pl.ds(0, TK), :], mask, first=True)

    n_pb = n_past // TK
    first_block(pkv_ref, pkr_ref, n_pb - 1, None)
    n_pairs = (n_pb - 1) // 2
    pipelined_pairs(pkv_ref, pkr_ref, n_pairs, n_pb - 1)
    if 1 + 2 * n_pairs < n_pb:
        pipelined(pkv_ref, pkr_ref, 1 + 2 * n_pairs, n_pb, n_pb - 1, None)

    update(scores(kv_ref[...], kr_ref[...]), kv_ref[...], None)

    o = acc_ref[...] / _lane_bcast(l_ref[...], KV_LORA)
    parts = []
    for h in range(N_HEADS):
        oh = o[h * tq:(h + 1) * tq, :].astype(BF16)
        parts.append(jnp.dot(oh, wuv_ref[h], preferred_element_type=F32))
    attn = jnp.concatenate(parts, axis=-1)
    out_ref[...] = _rms(attn, gao_ref[...]).astype(BF16)


def _attention(cqn, w_q, w_ukt, w_uv, cosq, sinq, gao, ckv, krope, past_kv, past_kr, *, n_batch, seq, row0):
    tq = seq
    n_past = past_kv.shape[1]
    assert n_past % CHUNK == 0 and seq <= CHUNK and n_past % TK == 0
    blk0 = row0 // tq
    in_specs = [
        pl.BlockSpec((tq, Q_LORA), lambda b, q: (blk0 + b, 0)),
        _const_spec(w_q.shape),
        _const_spec(w_ukt.shape),
        _const_spec(w_uv.shape),
        pl.BlockSpec((tq, N_HEADS * ROPE_DIM), lambda b, q: (0, 0)),
        pl.BlockSpec((tq, N_HEADS * ROPE_DIM), lambda b, q: (0, 0)),
        _const_spec((1, ATTN_W)),
        pl.BlockSpec((None, n_past, KV_LORA), lambda b, q: (b, 0, 0)),
        pl.BlockSpec((None, ROPE_DIM, n_past), lambda b, q: (b, 0, 0)),
        pl.BlockSpec((seq, KV_LORA), lambda b, q: (b, 0)),
        pl.BlockSpec((None, ROPE_DIM, seq), lambda b, q: (b, 0, 0)),
    ]
    args = [cqn, w_q, w_ukt, w_uv, cosq, sinq, gao, past_kv, past_kr, ckv, krope]
    rows = N_HEADS * tq
    return pl.pallas_call(
        functools.partial(_attn_kernel, tq=tq, n_past=n_past),
        grid=(n_batch, 1),
        in_specs=in_specs,
        out_specs=pl.BlockSpec((tq, ATTN_W), lambda b, q: (b, 0)),
        out_shape=jax.ShapeDtypeStruct((n_batch * seq, ATTN_W), BF16),
        scratch_shapes=[
            pltpu.VMEM((rows, KV_LORA), BF16),
            pltpu.VMEM((rows, ROPE_DIM), BF16),
            pltpu.VMEM((rows, LANES), F32),
            pltpu.VMEM((rows, LANES), F32),
            pltpu.VMEM((rows, KV_LORA), F32),
            pltpu.VMEM((2, rows, TK), F32),
        ],
        compiler_params=pltpu.CompilerParams(dimension_semantics=("arbitrary", "arbitrary"),
                                             vmem_limit_bytes=VMEM_LIMIT),
        name="attn_sample",
    )(*args)


def _attn_heads_kernel(cqn_ref, wq_ref, wkv_ref, cos_ref, sin_ref, gao_ref, kv_ref, krt_ref, out_ref,
                       kcat_ref, vh_ref, qcat_ref, m_ref, l_ref, acc_ref, klim_ref, *, tq, seq):
    qi = pl.program_id(1)
    nt = (((1,), (1,)), ((), ()))
    kw = QK_NOPE + ROPE_DIM
    kpad = kcat_ref.shape[-1]

    @pl.when(qi == 0)
    def _():
        def expand(j, c):
            k0 = pl.multiple_of(j * TK, TK)
            latent = kv_ref[pl.ds(k0, TK), :].astype(BF16)
            kvh = jnp.dot(latent, wkv_ref[...], preferred_element_type=F32)
            k_rope = krt_ref[:, pl.ds(k0, TK)].T.astype(BF16)
            for h in range(N_HEADS):
                kcat_ref[h, pl.ds(k0, TK), :QK_NOPE] = kvh[:, h * QK_NOPE:(h + 1) * QK_NOPE].astype(BF16)
                kcat_ref[h, pl.ds(k0, TK), QK_NOPE:kw] = k_rope
                kcat_ref[h, pl.ds(k0, TK), kw:] = jnp.zeros((TK, kpad - kw), BF16)
                v0 = N_HEADS * QK_NOPE + h * V_DIM
                vh_ref[h, pl.ds(k0, TK), :] = kvh[:, v0:v0 + V_DIM].astype(BF16)
            return c
        lax.fori_loop(0, seq // TK, expand, 0)

    q = jnp.dot(cqn_ref[...], wq_ref[...], preferred_element_type=F32)
    nope_w = N_HEADS * QK_NOPE
    rope_w = N_HEADS * ROPE_DIM
    qrope = q[:, nope_w:nope_w + rope_w] * cos_ref[...] + q[:, nope_w + rope_w:] * sin_ref[...]
    for h in range(N_HEADS):
        qcat_ref[h, :, :QK_NOPE] = (q[:, h * QK_NOPE:(h + 1) * QK_NOPE] * EXP2_SCALE).astype(BF16)
        qcat_ref[h, :, QK_NOPE:kw] = (qrope[:, h * ROPE_DIM:(h + 1) * ROPE_DIM] * EXP2_SCALE).astype(BF16)
        qcat_ref[h, :, kw:] = jnp.zeros((tq, kpad - kw), BF16)

    r = lax.broadcasted_iota(jnp.int32, (tq, LANES), 0)
    klim_ref[...] = ((qi * tq + r) & ~(CHUNK - 1)) + CHUNK

    tk = TKH

    def block(k0, masked, first):
        if masked:
            cidx = lax.broadcasted_iota(jnp.int32, (tq, tk), 1)
            mask = cidx < _lane_bcast(klim_ref[...] - k0, tk)
        for h in range(N_HEADS):
            s = lax.dot_general(qcat_ref[h], kcat_ref[h, pl.ds(k0, tk), :], nt, preferred_element_type=F32)
            if masked:
                s = jnp.where(mask, s, NEG_BIG)
            m_cur = jnp.max(s, axis=-1, keepdims=True)
            if first:
                m_new = jnp.broadcast_to(m_cur, (tq, LANES))
            else:
                m_prev = m_ref[h]
                m_new = jnp.maximum(m_prev, m_cur)
                alpha = jnp.exp2(m_prev - m_new)
            p = jnp.exp2(s - _lane_bcast(m_new, tk))
            l_cur = jnp.sum(p, axis=-1, keepdims=True)
            pv = jnp.dot(p.astype(BF16), vh_ref[h, pl.ds(k0, tk), :], preferred_element_type=F32)
            if first:
                l_ref[h] = jnp.broadcast_to(l_cur, (tq, LANES))
                acc_ref[h] = pv
            else:
                l_ref[h] = alpha * l_ref[h] + l_cur
                acc_ref[h] = _lane_bcast(alpha, V_DIM) * acc_ref[h] + pv
            m_ref[h] = m_new

    n_blocks = ((qi + 1) * tq + tk - 1) // tk
    n_full = jnp.minimum((qi * tq // CHUNK + 1) * CHUNK // tk, n_blocks)

    def loop(lo, hi, masked):
        def body(j, c):
            block(pl.multiple_of(j * tk, tk), masked, False)
            return c
        lax.fori_loop(lo, hi, body, 0)

    block(0, True, True)
    loop(1, n_full, False)
    loop(jnp.maximum(n_full, 1), n_blocks, True)

    attn = jnp.concatenate([acc_ref[h] / _lane_bcast(l_ref[h], V_DIM) for h in range(N_HEADS)], axis=-1)
    out_ref[...] = _rms(attn, gao_ref[...]).astype(BF16)


def _attention_heads(cqn, w_q, w_kv, cosq, sinq, gao, ckv, krope_t, *, n_batch, seq):
    nq = seq // TQ
    kpad = 2 * LANES
    assert QK_NOPE + ROPE_DIM <= kpad and V_DIM == LANES
    return pl.pallas_call(
        functools.partial(_attn_heads_kernel, tq=TQ, seq=seq),
        grid=(n_batch, nq),
        in_specs=[
            pl.BlockSpec((TQ, Q_LORA), lambda b, q: (b * nq + q, 0)),
            _const_spec(w_q.shape),
            _const_spec(w_kv.shape),
            pl.BlockSpec((TQ, N_HEADS * ROPE_DIM), lambda b, q: (q, 0)),
            pl.BlockSpec((TQ, N_HEADS * ROPE_DIM), lambda b, q: (q, 0)),
            _const_spec((1, ATTN_W)),
            pl.BlockSpec((seq, KV_LORA), lambda b, q: (b, 0)),
            pl.BlockSpec((None, ROPE_DIM, seq), lambda b, q: (b, 0, 0)),
        ],
        out_specs=pl.BlockSpec((TQ, ATTN_W), lambda b, q: (b * nq + q, 0)),
        out_shape=jax.ShapeDtypeStruct((n_batch * seq, ATTN_W), BF16),
        scratch_shapes=[
            pltpu.VMEM((N_HEADS, seq, kpad), BF16),
            pltpu.VMEM((N_HEADS, seq, V_DIM), BF16),
            pltpu.VMEM((N_HEADS, TQ, kpad), BF16),
            pltpu.VMEM((N_HEADS, TQ, LANES), F32),
            pltpu.VMEM((N_HEADS, TQ, LANES), F32),
            pltpu.VMEM((N_HEADS, TQ, V_DIM), F32),
            pltpu.VMEM((TQ, LANES), jnp.int32),
        ],
        compiler_params=pltpu.CompilerParams(dimension_semantics=("arbitrary", "arbitrary"),
                                             vmem_limit_bytes=VMEM_LIMIT),
        name="attn_prompt",
    )(cqn, w_q, w_kv, cosq, sinq, gao, ckv, krope_t)


def _out_proj_kernel(attnp_ref, attns_ref, convn_ref, xp_ref, xs_ref, wo_hbm, gffn_ref, wr_ref,
                     br_ref, h_ref, xpk_ref, mi_ref, mf_ref, cnt_ref, carry_ref, logit_ref,
                     wo_ref, stage_ref, wsems, *, n_prompt_tiles):
    i = pl.program_id(0)

    @pl.when(i == 0)
    def _():
        carry_ref[...] = jnp.zeros(carry_ref.shape, F32)
        logit_ref[...] = jnp.zeros(logit_ref.shape, F32)
        _load_weight_as_bf16(wo_hbm, wo_ref, stage_ref, wsems)

    def tile(x_ref, attn_ref):
        prev_logits = logit_ref[...]
        y = jnp.dot(attn_ref[...], wo_ref[:ATTN_W, :], preferred_element_type=F32)
        y = y + jnp.dot(convn_ref[...], wo_ref[ATTN_W:, :], preferred_element_type=F32)
        h = x_ref[...] + y
        h_ref[...] = h
        xn = _rms(h, gffn_ref[...])

        half = D_MODEL // 2
        xh = xn.astype(BF16)
        xh32 = xh.astype(F32)
        lo = lax.bitcast_convert_type(xh32[:, :half], jnp.uint32)
        hi = lax.bitcast_convert_type(xh32[:, half:], jnp.uint32)
        xpk_ref[...] = (lo >> 16) | (hi & jnp.uint32(0xFFFF0000))

        xl = (xn - xh32).astype(BF16)
        hh_hl = jnp.dot(xh, wr_ref[...], preferred_element_type=F32)
        lh = jnp.dot(xl, wr_ref[:, :LANES], preferred_element_type=F32)
        logit_ref[...] = hh_hl[:, :LANES] + (lh + hh_hl[:, LANES:]) + br_ref[...]

        logits = prev_logits
        counted = (i > 0).astype(F32)
        lane = lax.broadcasted_iota(jnp.int32, (TM, LANES), 1).astype(F32)
        ninf = -jnp.inf
        far = float(LANES)

        def first_argmax(v):
            vmax = jnp.max(v, axis=-1, keepdims=True)
            return vmax, jnp.min(jnp.where(v == vmax, lane, far), axis=-1, keepdims=True)

        gl = jnp.where(lane < N_GROUPS, logits, ninf)
        gmax, gidx = first_argmax(gl)
        g_p = 1.0 / jnp.sum(jnp.exp(gl - gmax), axis=-1, keepdims=True)
        e_lo = N_GROUPS + EXPERTS_PER_GROUP * gidx
        el = jnp.where((lane >= e_lo) & (lane < e_lo + EXPERTS_PER_GROUP), logits, ninf)
        e1max, i1 = first_argmax(el)
        z = jnp.sum(jnp.exp(el - e1max), axis=-1, keepdims=True)
        el2 = jnp.where(lane == i1, ninf, el)
        e2max, i2 = first_argmax(el2)
        p1 = 1.0 / z
        p2 = jnp.exp(e2max - e1max) / z
        den = p1 + p2
        g0 = g_p * p1 / den
        g1 = g_p * p2 / den
        e0 = i1 - N_GROUPS
        e1 = i2 - N_GROUPS

        oh0 = lane == e0
        oh1 = lane == e1
        oh = jnp.where(oh0 | oh1, 1.0, 0.0)
        r = lax.broadcasted_iota(jnp.int32, (TM, TM), 0)
        c = lax.broadcasted_iota(jnp.int32, (TM, TM), 1)
        ltri = jnp.where(r > c, 1.0, 0.0).astype(BF16)
        before = jnp.dot(ltri, oh.astype(BF16), preferred_element_type=F32) + carry_ref[...]
        rank0 = jnp.sum(jnp.where(oh0, before, 0.0), axis=-1, keepdims=True)
        rank1 = jnp.sum(jnp.where(oh1, before, 0.0), axis=-1, keepdims=True)
        total = carry_ref[...] + counted * jnp.sum(oh, axis=0, keepdims=True)
        carry_ref[...] = total
        cnt_ref[...] = jnp.broadcast_to(total, cnt_ref.shape)

        mi = jnp.where(lane == 0, e0, jnp.where(lane == 1, e1, jnp.where(lane == 2, rank0, rank1)))
        mi_ref[...] = jnp.transpose(mi)[:SUBLANES, :].astype(jnp.int32)
        mf_ref[...] = jnp.where(lane == 0, g0, g1)

    @pl.when(i < n_prompt_tiles)
    def _():
        tile(xp_ref, attnp_ref)

    @pl.when(i >= n_prompt_tiles)
    def _():
        tile(xs_ref, attns_ref)


def _out_proj(attn_p, attn_s, conv_n, xp, xs, w_ob, gffn, w_r2, b_r):
    m = conv_n.shape[0]
    npt = xp.shape[0] // TM
    n_tiles = m // TM
    last_p, last_s, last = npt - 1, n_tiles - npt - 1, n_tiles - 1
    row = lambda i: (jnp.minimum(i, last), 0)
    prow = lambda i: (jnp.minimum(i, last_p), 0)
    srow = lambda i: (jnp.clip(i - npt, 0, last_s), 0)
    lag = lambda i: jnp.maximum(i - 1, 0)
    return pl.pallas_call(
        functools.partial(_out_proj_kernel, n_prompt_tiles=npt),
        grid=(n_tiles + 1,),
        in_specs=[
            pl.BlockSpec((TM, ATTN_W), prow),
            pl.BlockSpec((TM, ATTN_W), srow),
            pl.BlockSpec((TM, CONV_CH), row),
            pl.BlockSpec((TM, D_MODEL), prow),
            pl.BlockSpec((TM, D_MODEL), srow),
            pl.BlockSpec(memory_space=pl.ANY),
            _const_spec((1, D_MODEL)),
            _const_spec(w_r2.shape),
            _const_spec((1, LANES)),
        ],
        out_specs=[
            pl.BlockSpec((TM, D_MODEL), row),
            pl.BlockSpec((TM, D_MODEL // 2), row),
            pl.BlockSpec((SUBLANES, TM), lambda i: (0, lag(i))),
            pl.BlockSpec((TM, LANES), lambda i: (lag(i), 0)),
            pl.BlockSpec((SUBLANES, LANES), lambda i: (0, 0)),
        ],
        out_shape=[
            jax.ShapeDtypeStruct((m, D_MODEL), F32),
            jax.ShapeDtypeStruct((m, D_MODEL // 2), jnp.uint32),
            jax.ShapeDtypeStruct((SUBLANES, m), jnp.int32),
            jax.ShapeDtypeStruct((m, LANES), F32),
            jax.ShapeDtypeStruct((SUBLANES, LANES), F32),
        ],
        scratch_shapes=[pltpu.VMEM((1, LANES), F32), pltpu.VMEM((TM, LANES), F32),
                        pltpu.VMEM(w_ob.shape, BF16),
                        pltpu.VMEM((2, w_ob.shape[0] // W_O_SLABS, w_ob.shape[1]), F32),
                        pltpu.SemaphoreType.DMA((2,))],
        compiler_params=pltpu.CompilerParams(dimension_semantics=("arbitrary",),
                                             vmem_limit_bytes=VMEM_LIMIT),
        name="out_proj",
    )(attn_p, attn_s, conv_n, xp, xs, w_ob, gffn, w_r2, b_r)


def _dispatch_kernel(d0_ref, d1_ref, zlo_ref, zn_ref, nu_ref, xpk_ref, xs_hbm, zeros_ref, sems, *, n_blocks):
    i = pl.program_id(0)
    sem = sems.at[0]
    zsem = sems.at[1]

    def zero_fill(act):
        def per_expert(e, c):
            lo = zlo_ref[e]
            n = zn_ref[e]
            head = (-lo) & (SUBLANES - 1)
            for r in range(SUBLANES - 1):
                @pl.when(r < head)
                def _(r=r):
                    act(pltpu.make_async_copy(zeros_ref.at[pl.ds(0, 1)], xs_hbm.at[pl.ds(lo + r, 1)], zsem))
            off = lo + head
            rest = n - head
            size = MOE_BLOCK // 2
            while size >= SUBLANES:
                @pl.when((rest & size) != 0)
                def _(off=off, size=size):
                    dst = xs_hbm.at[pl.ds(pl.multiple_of(off, SUBLANES), size)]
                    act(pltpu.make_async_copy(zeros_ref.at[pl.ds(0, size)], dst, zsem))
                off = off + (rest & size)
                size //= 2
            return c

        def per_block(b, c):
            dst = xs_hbm.at[pl.ds(pl.multiple_of(b * MOE_BLOCK, MOE_BLOCK), MOE_BLOCK)]
            act(pltpu.make_async_copy(zeros_ref, dst, zsem))
            return c

        lax.fori_loop(0, N_EXPERTS, per_expert, 0)
        lax.fori_loop(nu_ref[0], n_blocks, per_block, 0)

    @pl.when(i == 0)
    def _():
        zeros_ref[...] = jnp.zeros(zeros_ref.shape, zeros_ref.dtype)
        zero_fill(lambda cp: cp.start())

    @pl.when(i == pl.num_programs(0) - 1)
    def _():
        zero_fill(lambda cp: cp.wait())

    base = i * TD

    def start(g, c):
        for u in range(SUBLANES):
            r = base + g * SUBLANES + u
            src = xpk_ref.at[g, pl.ds(u, 1)]
            pltpu.make_async_copy(src, xs_hbm.at[pl.ds(d0_ref[r], 1)], sem).start()
            pltpu.make_async_copy(src, xs_hbm.at[pl.ds(d1_ref[r], 1)], sem).start()
        return c

    lax.fori_loop(0, TD // SUBLANES, start, 0)
    for _ in range(2):
        pltpu.make_async_copy(xs_hbm.at[pl.ds(0, TD)], xs_hbm.at[pl.ds(0, TD)], sem).wait()


def _dispatch(dest0, dest1, pad_lo, n_pad, n_used, xpk, n_blocks):
    m = xpk.shape[0]
    grid_spec = pltpu.PrefetchScalarGridSpec(
        num_scalar_prefetch=5,
        grid=(m // TD,),
        in_specs=[pl.BlockSpec((TD // SUBLANES, SUBLANES, D_MODEL // 2), lambda i, *_: (i, 0, 0))],
        out_specs=pl.BlockSpec(memory_space=pl.ANY),
        scratch_shapes=[pltpu.VMEM((MOE_BLOCK, D_MODEL // 2), jnp.uint32),
                        pltpu.SemaphoreType.DMA((2,))],
    )
    return pl.pallas_call(
        functools.partial(_dispatch_kernel, n_blocks=n_blocks),
        grid_spec=grid_spec,
        out_shape=jax.ShapeDtypeStruct((n_blocks * MOE_BLOCK, D_MODEL // 2), jnp.uint32),
        compiler_params=pltpu.CompilerParams(dimension_semantics=("arbitrary",)),
        name="dispatch",
    )(dest0, dest1, pad_lo, n_pad, n_used, xpk.reshape(m // SUBLANES, SUBLANES, D_MODEL // 2))


def _experts_kernel(be_ref, nu_ref, nxt_ref, x_ref, wg_hbm, wu_hbm, wd_hbm, y_ref,
                    sg_ref, su_ref, sd_ref, wgb_ref, wub_ref, wdb_ref, sems):
    b = pl.program_id(0)
    active = b < nu_ref[0]
    new_expert = jnp.logical_or(b == 0, be_ref[b] != be_ref[jnp.maximum(b - 1, 0)])

    def weight_copies(e):
        return (pltpu.make_async_copy(wg_hbm.at[e], sg_ref, sems.at[0]),
                pltpu.make_async_copy(wu_hbm.at[e], su_ref, sems.at[1]),
                pltpu.make_async_copy(wd_hbm.at[e], sd_ref, sems.at[2]))

    @pl.when(b == 0)
    def _():
        for cp in weight_copies(be_ref[0]):
            cp.start()

    @pl.when(jnp.logical_and(active, new_expert))
    def _():
        for cp in weight_copies(be_ref[b]):
            cp.wait()
        wgb_ref[...] = sg_ref[...].astype(BF16)
        wub_ref[...] = su_ref[...].astype(BF16)
        wdb_ref[...] = sd_ref[...].astype(BF16)

        @pl.when(nxt_ref[b] >= 0)
        def _():
            for cp in weight_copies(nxt_ref[b]):
                cp.start()

    @pl.when(active)
    def _():
        half = D_MODEL // 2
        xa, xb = (v.astype(BF16) for v in _unpack_bf16_pairs(x_ref[...]))
        g = jnp.dot(xa, wgb_ref[:half, :], preferred_element_type=F32)
        g = g + jnp.dot(xb, wgb_ref[half:, :], preferred_element_type=F32)
        u = jnp.dot(xa, wub_ref[:half, :], preferred_element_type=F32)
        u = u + jnp.dot(xb, wub_ref[half:, :], preferred_element_type=F32)
        hmid = (g * jax.nn.sigmoid(g)) * u
        y = jnp.dot(hmid.astype(BF16), wdb_ref[...], preferred_element_type=F32)
        y_ref[...] = _pack_bf16_pairs(y)

    @pl.when(b >= nu_ref[0])
    def _():
        y_ref[...] = jnp.zeros(y_ref.shape, y_ref.dtype)


def _experts(block_e, n_used, next_e, x_sorted, w_gate, w_up, w_down):
    p = x_sorted.shape[0]
    nb = p // MOE_BLOCK

    def xrow(b, be, nu, nxt):
        return (jnp.maximum(jnp.minimum(b, nu[0] - 1), 0), 0)

    grid_spec = pltpu.PrefetchScalarGridSpec(
        num_scalar_prefetch=3,
        grid=(nb,),
        in_specs=[
            pl.BlockSpec((MOE_BLOCK, D_MODEL // 2), xrow),
            pl.BlockSpec(memory_space=pl.ANY),
            pl.BlockSpec(memory_space=pl.ANY),
            pl.BlockSpec(memory_space=pl.ANY),
        ],
        out_specs=pl.BlockSpec((MOE_BLOCK, D_MODEL // 2), lambda b, be, nu, nxt: (b, 0)),
        scratch_shapes=[pltpu.VMEM((D_MODEL, D_FF), F32), pltpu.VMEM((D_MODEL, D_FF), F32),
                        pltpu.VMEM((D_FF, D_MODEL), F32),
                        pltpu.VMEM((D_MODEL, D_FF), BF16), pltpu.VMEM((D_MODEL, D_FF), BF16),
                        pltpu.VMEM((D_FF, D_MODEL), BF16),
                        pltpu.SemaphoreType.DMA((3,))],
    )
    return pl.pallas_call(
        _experts_kernel,
        grid_spec=grid_spec,
        out_shape=jax.ShapeDtypeStruct((p, D_MODEL // 2), jnp.uint32),
        compiler_params=pltpu.CompilerParams(dimension_semantics=("arbitrary",),
                                             vmem_limit_bytes=VMEM_LIMIT),
        name="experts",
    )(block_e, n_used, next_e, x_sorted, w_gate, w_up, w_down)


def _combine_kernel(d0_ref, d1_ref, h_ref, mf_ref, gfin_ref, y_hbm, outp_ref, outs_ref, y0_ref, y1_ref, sems,
                    *, n_tiles, n_prompt_tiles):
    i = pl.program_id(0)

    def gather(tile, slot, act):
        base = tile * TM

        def body(g, c):
            for u in range(SUBLANES):
                r = base + g * SUBLANES + u
                act(pltpu.make_async_copy(y_hbm.at[pl.ds(d0_ref[r], 1)], y0_ref.at[slot, g, pl.ds(u, 1)],
                                          sems.at[slot]))
                act(pltpu.make_async_copy(y_hbm.at[pl.ds(d1_ref[r], 1)], y1_ref.at[slot, g, pl.ds(u, 1)],
                                          sems.at[slot]))
            return c
        lax.fori_loop(0, TM // SUBLANES, body, 0)

    @pl.when(i == 0)
    def _():
        gather(0, 0, lambda cp: cp.start())

    @pl.when(i + 1 < n_tiles)
    def _():
        gather(i + 1, (i + 1) % 2, lambda cp: cp.start())

    slot = i % 2
    for _ in range(2):
        pltpu.make_async_copy(y_hbm.at[pl.ds(0, TM)], y_hbm.at[pl.ds(0, TM)], sems.at[slot]).wait()

    def finish(out_ref):
        mf = mf_ref[...]
        g0, g1 = mf[:, :, 0:1], mf[:, :, 1:2]
        half = D_MODEL // 2
        a0, b0 = _unpack_bf16_pairs(y0_ref[slot])
        a1, b1 = _unpack_bf16_pairs(y1_ref[slot])
        o_lo = h_ref[:, :, :half] + (g0 * a0 + g1 * a1)
        o_hi = h_ref[:, :, half:] + (g0 * b0 + g1 * b1)
        sumsq = jnp.sum(o_lo * o_lo, axis=-1, keepdims=True) + jnp.sum(o_hi * o_hi, axis=-1, keepdims=True)
        inv_rms = lax.rsqrt(sumsq / D_MODEL + EPS)
        out_ref[:, :, :half] = o_lo * inv_rms * gfin_ref[:, :, :half]
        out_ref[:, :, half:] = o_hi * inv_rms * gfin_ref[:, :, half:]

    @pl.when(i < n_prompt_tiles)
    def _():
        finish(outp_ref)

    @pl.when(i >= n_prompt_tiles)
    def _():
        finish(outs_ref)


def _combine(dest0, dest1, h, mf, gfin, y_sorted, *, n_prompt_rows):
    m = h.shape[0]
    npt = n_prompt_rows // TM
    tg = TM // SUBLANES
    grouped = lambda a: a.reshape(a.shape[0] // SUBLANES, SUBLANES, a.shape[1])
    grid_spec = pltpu.PrefetchScalarGridSpec(
        num_scalar_prefetch=2,
        grid=(m // TM,),
        in_specs=[
            pl.BlockSpec((tg, SUBLANES, D_MODEL), lambda i, *_: (i, 0, 0)),
            pl.BlockSpec((tg, SUBLANES, LANES), lambda i, *_: (i, 0, 0)),
            pl.BlockSpec((1, 1, D_MODEL), lambda i, *_: (0, 0, 0)),
            pl.BlockSpec(memory_space=pl.ANY),
        ],
        out_specs=[pl.BlockSpec((tg, SUBLANES, D_MODEL), lambda i, *_: (jnp.minimum(i, npt - 1), 0, 0)),
                   pl.BlockSpec((tg, SUBLANES, D_MODEL), lambda i, *_: (jnp.maximum(i - npt, 0), 0, 0))],
        scratch_shapes=[pltpu.VMEM((2, tg, SUBLANES, D_MODEL // 2), jnp.uint32),
                        pltpu.VMEM((2, tg, SUBLANES, D_MODEL // 2), jnp.uint32),
                        pltpu.SemaphoreType.DMA((2,))],
    )
    y_p, y_s = pl.pallas_call(
        functools.partial(_combine_kernel, n_tiles=m // TM, n_prompt_tiles=npt),
        grid_spec=grid_spec,
        out_shape=[jax.ShapeDtypeStruct((n_prompt_rows // SUBLANES, SUBLANES, D_MODEL), F32),
                   jax.ShapeDtypeStruct(((m - n_prompt_rows) // SUBLANES, SUBLANES, D_MODEL), F32)],
        compiler_params=pltpu.CompilerParams(dimension_semantics=("arbitrary",),
                                             vmem_limit_bytes=VMEM_LIMIT),
        name="combine",
    )(dest0, dest1, grouped(h), grouped(mf), gfin.reshape(1, 1, D_MODEL), y_sorted)
    return y_p.reshape(n_prompt_rows, D_MODEL), y_s.reshape(m - n_prompt_rows, D_MODEL)


def _rope_tables(pos):
    f32 = np.float32
    inv = np.power(f32(ROPE_THETA), -np.arange(0, ROPE_DIM, 2, dtype=f32) / f32(ROPE_DIM)).astype(f32)
    ang = (pos.astype(f32)[:, None] * inv[None, :]).astype(f32)
    cos, sin = np.cos(ang).astype(f32), np.sin(ang).astype(f32)
    return np.concatenate([cos, cos], axis=-1), np.concatenate([-sin, sin], axis=-1)


def _swap_halves(w):
    return jnp.concatenate([w[..., ROPE_DIM // 2:], w[..., :ROPE_DIM // 2]], axis=-1)


def kernel(x_prompt, x_sample, cache_kv_latent, cache_k_rope, state_conv, norm_mix, w_in, norm_q, w_uq,
           norm_kv, w_uk, w_uv, conv_w, norm_attn_out, norm_conv_out, w_o, norm_ffn, w_router_group,
           b_router_group, w_router_expert, b_router_expert, w_gate, w_up, w_down, norm_final):
    assert w_in.shape[0] == 1, "single-layer trunk"
    bp, seq_p, _ = x_prompt.shape
    bs, seq_s, _ = x_sample.shape
    past_len = cache_kv_latent.shape[2]
    np_rows, ns_rows = bp * seq_p, bs * seq_s
    m = np_rows + ns_rows
    assert seq_p % TM == 0 and TM % seq_s == 0 and ns_rows % TM == 0 and seq_s == CHUNK
    assert m % TD == 0

    xp = x_prompt.reshape(np_rows, D_MODEL)
    xs = x_sample.reshape(ns_rows, D_MODEL)
    row_vec = lambda v: v.reshape(1, -1)

    assert w_in.shape[2] == Q_LORA + KV_LORA + ROPE_DIM + 3 * CONV_CH
    w_t = jnp.swapaxes(w_in[0], 0, 1)
    assert w_t.shape[0] % (W_IN_SLABS * 2 * SUBLANES) == 0
    wq4 = w_uq[0].reshape(Q_LORA, N_HEADS, QK_NOPE + ROPE_DIM)
    wq_rope = wq4[:, :, QK_NOPE:]
    w_q = jnp.concatenate([wq4[:, :, :QK_NOPE].reshape(Q_LORA, -1), wq_rope.reshape(Q_LORA, -1),
                           _swap_halves(wq_rope).reshape(Q_LORA, -1)], axis=1).astype(BF16)
    w_ukt = jnp.transpose(w_uk[0], (1, 2, 0)).astype(BF16)
    w_uvh = jnp.transpose(w_uv[0], (1, 0, 2)).astype(BF16)
    w_ob = w_o[0]
    n_router = N_GROUPS + N_EXPERTS
    w_r = jnp.concatenate([w_router_group[0], w_router_expert[0].reshape(D_MODEL, N_EXPERTS)], axis=1)
    w_r = jnp.pad(w_r, ((0, 0), (0, LANES - n_router)))
    w_rh = w_r.astype(BF16)
    w_rl = (w_r - w_rh.astype(F32)).astype(BF16)
    w_r2 = jnp.concatenate([w_rh, w_rl], axis=1)
    b_r =jnp.pad(jnp.concatenate([b_router_group[0], b_router_expert[0].reshape(N_EXPERTS)]),
                  (0, LANES - n_router)).reshape(1, LANES)

    cos_p, sin_p = _rope_tables(np.arange(seq_p))
    cos_s, sin_s = _rope_tables(past_len + np.arange(seq_s))
    cosk = np.concatenate([cos_p, np.tile(cos_s, (TM // seq_s, 1))], axis=0)
    sink = np.concatenate([sin_p, np.tile(sin_s, (TM // seq_s, 1))], axis=0)
    state = jnp.concatenate([jnp.zeros((bp, CONV_W - 1, CONV_CH), F32), state_conv[0]], axis=0)

    cqn, ckv_p, kr_p, ckv_s, kr_s, conv_n, utail = _in_proj(
        xp, xs, row_vec(norm_mix[0]), w_t, row_vec(norm_q[0]), row_vec(norm_kv[0]),
        row_vec(norm_conv_out[0]), conv_w[0], cosk, sink, state, seq_p=seq_p, seq_s=seq_s)

    gao = row_vec(norm_attn_out[0])
    w_kv = jnp.concatenate([w_uk[0].reshape(KV_LORA, N_HEADS * QK_NOPE),
                            w_uv[0].reshape(KV_LORA, N_HEADS * V_DIM)], axis=1).astype(BF16)
    attn_p = _attention_heads(cqn, w_q, w_kv, np.tile(cos_p, (1, N_HEADS)), np.tile(sin_p, (1, N_HEADS)),
                              gao, ckv_p, kr_p, n_batch=bp, seq=seq_p)
    attn_s = _attention(cqn, w_q, w_ukt, w_uvh, np.tile(cos_s, (1, N_HEADS)), np.tile(sin_s, (1, N_HEADS)),
                        gao, ckv_s, kr_s, cache_kv_latent[0], jnp.swapaxes(cache_k_rope[0], 1, 2),
                        n_batch=bs, seq=seq_s, row0=np_rows)

    h, xpk, mi, mf, cnt = _out_proj(attn_p, attn_s, conv_n, xp, xs, w_ob, row_vec(norm_ffn[0]),
                                    w_r2, b_r)

    counts = cnt[0, :N_EXPERTS].astype(jnp.int32)
    padded = (counts + MOE_BLOCK - 1) // MOE_BLOCK * MOE_BLOCK
    pad_end = jnp.cumsum(padded)
    pad_start = pad_end - padded
    n_blocks = -(-(m * 2) // MOE_BLOCK) + N_EXPERTS
    block_row0 = jnp.arange(n_blocks, dtype=jnp.int32) * MOE_BLOCK
    block_e = jnp.minimum(jnp.sum((pad_end[None, :] <= block_row0[:, None]).astype(jnp.int32), axis=1),
                          N_EXPERTS - 1)
    n_used = (pad_end[-1:] // MOE_BLOCK).astype(jnp.int32)
    expert_ids = jnp.arange(N_EXPERTS, dtype=jnp.int32)[:, None]

    def seg_start(e):
        return jnp.sum(jnp.where(expert_ids == e[None, :], pad_start[:, None], 0), axis=0)

    dest0 = seg_start(mi[0]) + mi[2]
    dest1 = seg_start(mi[1]) + mi[3]

    x_sorted = _dispatch(dest0, dest1, pad_start + counts, padded - counts, n_used, xpk, n_blocks)
    later = (expert_ids.T > block_e[:, None]) & (padded > 0)[None, :]
    next_e = jnp.min(jnp.where(later, expert_ids.T, N_EXPERTS), axis=1)
    next_e = jnp.where(next_e == N_EXPERTS, -1, next_e).astype(jnp.int32)
    y_sorted = _experts(block_e, n_used, next_e, x_sorted, w_gate[0], w_up[0], w_down[0])
    gfin = row_vec(norm_final)
    y_p, y_s = _combine(dest0, dest1, h, mf, gfin, y_sorted, n_prompt_rows=np_rows)

    ut = utail.reshape(m // CHUNK, SUBLANES, CONV_CH)
    tails = ut[:, SUBLANES - (CONV_W - 1):, :]
    p_last = (jnp.arange(bp) + 1) * (seq_p // CHUNK) - 1
    s_last = np_rows // CHUNK + (jnp.arange(bs) + 1) * (seq_s // CHUNK) - 1
    return (y_p.reshape(bp, seq_p, D_MODEL),
            y_s.reshape(bs, seq_s, D_MODEL),
            ckv_p.reshape(1, bp, seq_p, KV_LORA),
            jnp.swapaxes(kr_p, 1, 2)[None],
            tails[p_last][None],
            ckv_s.reshape(1, bs, seq_s, KV_LORA),
            jnp.swapaxes(kr_s, 1, 2)[None],
            tails[s_last][None])
```

```python
import functools

import jax
import jax.numpy as jnp
import numpy as np
from jax import lax
from jax.experimental import pallas as pl
from jax.experimental.pallas import tpu as pltpu

F32 = jnp.float32
BF16 = jnp.bfloat16

D_MODEL = 2048
N_HEADS = 8
QK_NOPE = 128
ROPE_DIM = 64
V_DIM = 128
Q_LORA = 512
KV_LORA = 512
ATTN_W = N_HEADS * V_DIM
CONV_CH = D_MODEL - ATTN_W
CONV_W = 3
CHUNK = 64
N_GROUPS = 4
EXPERTS_PER_GROUP = 8
N_EXPERTS = N_GROUPS * EXPERTS_PER_GROUP
D_FF = 512
ROPE_THETA = 10000.0
EPS = 1e-6
ATTN_SCALE = (QK_NOPE + ROPE_DIM) ** -0.5
EXP2_SCALE = ATTN_SCALE * 1.4426950408889634

LANES = 128
SUBLANES = 8
TM = 256
TD = 2304
MOE_BLOCK = 256
TQ = 512
TK = 256
TKH = 512
W_O_SLABS = 4
W_IN_SLABS = 10
NEG_BIG = -1e30
V7X_VMEM_BYTES = 64 * 1024 * 1024
VMEM_LIMIT = V7X_VMEM_BYTES * 7 // 8


def _rms(v, g):
    return v * lax.rsqrt(jnp.mean(v * v, axis=-1, keepdims=True) + EPS) * g


def _lane_bcast(v, width):
    if width % LANES == 0:
        return jnp.concatenate([v] * (width // LANES), axis=1)
    assert width < LANES
    return v[:, :width]


def _pack_bf16_pairs(v):
    half = v.shape[-1] // 2
    lo = lax.bitcast_convert_type(v[..., :half].astype(BF16).astype(F32), jnp.uint32)
    hi = lax.bitcast_convert_type(v[..., half:].astype(BF16).astype(F32), jnp.uint32)
    return (lo >> 16) | (hi & jnp.uint32(0xFFFF0000))


def _unpack_bf16_pairs(w):
    return (lax.bitcast_convert_type(w << 16, F32),
            lax.bitcast_convert_type(w & jnp.uint32(0xFFFF0000), F32))


def _load_weight_as_bf16(w_hbm, dst_ref, stage_ref, sems):
    rows = stage_ref.shape[1]
    n_slabs = dst_ref.shape[0] // rows

    def slab_copy(c):
        return pltpu.make_async_copy(w_hbm.at[pl.ds(c * rows, rows)], stage_ref.at[c % 2], sems.at[c % 2])

    slab_copy(0).start()
    for c in range(n_slabs):
        if c + 1 < n_slabs:
            slab_copy(c + 1).start()
        slab_copy(c).wait()
        dst_ref[c * rows:(c + 1) * rows, :] = stage_ref[c % 2].astype(BF16)


def _const_spec(shape):
    nd = len(shape)
    return pl.BlockSpec(shape, lambda *_: (0,) * nd, pipeline_mode=pl.Buffered(1))


def _in_proj_kernel(xp_ref, xs_ref, gmix_ref, w_hbm, gq_ref, gkv_ref, gco_ref, convw_ref,
                    cos_ref, sin_ref, state_ref,
                    cqn_ref, ckvp_ref, krp_ref, ckvs_ref, krs_ref, convn_ref, utail_ref,
                    ext_ref, wt_ref, stage_ref, wsems,
                    *, n_prompt_tiles, tiles_per_seq, n_prompt_seq, sample_seq_len):
    i = pl.program_id(0)

    @pl.when(i == 0)
    def _():
        ext_ref[...] = jnp.zeros(ext_ref.shape, F32)
        _load_weight_as_bf16(w_hbm, wt_ref, stage_ref, wsems)

    def conv_block(u_sub, gate_sub, row0, length):
        ext_ref[SUBLANES:SUBLANES + length, :] = u_sub
        um1 = ext_ref[SUBLANES - 1:SUBLANES - 1 + length, :]
        um2 = ext_ref[SUBLANES - 2:SUBLANES - 2 + length, :]
        cw = convw_ref[...]
        conv = cw[0:1] * um2 + cw[1:2] * um1 + cw[2:3] * u_sub
        convn_ref[row0:row0 + length, :] = _rms(gate_sub * conv, gco_ref[...]).astype(BF16)

    def tile(x_ref, is_prompt):
        ckv_ref, krt_ref = (ckvp_ref, krp_ref) if is_prompt else (ckvs_ref, krs_ref)
        x = x_ref[...]
        xg = (x * gmix_ref[...]).astype(BF16)
        inv_rms = lax.rsqrt(jnp.mean(x * x, axis=-1, keepdims=True) + EPS)
        lat_w = Q_LORA + KV_LORA
        conv0 = lat_w + ROPE_DIM
        nt = (((1,), (1,)), ((), ()))

        def project(lo, hi):
            return inv_rms * lax.dot_general(xg, wt_ref[lo:hi, :], nt, preferred_element_type=F32)

        z_ch = project(conv0 + CONV_CH, conv0 + 3 * CONV_CH)
        u = z_ch[:, :CONV_CH] * z_ch[:, CONV_CH:]
        for j in range(TM // CHUNK):
            utail_ref[j] = u[CHUNK * (j + 1) - SUBLANES:CHUNK * (j + 1), :]
        gate_b = project(conv0, conv0 + CONV_CH)

        if is_prompt:
            first = (i % tiles_per_seq) == 0
            carried = ext_ref[TM + SUBLANES - 2:TM + SUBLANES, :]
            ext_ref[SUBLANES - 2:SUBLANES, :] = jnp.where(first, state_ref[i // tiles_per_seq], carried)
            conv_block(u, gate_b, 0, TM)
        else:
            n_sub = TM // sample_seq_len
            seq0 = n_prompt_seq + (i - n_prompt_tiles) * n_sub
            for k in range(n_sub):
                ext_ref[SUBLANES - 2:SUBLANES, :] = state_ref[seq0 + k]
                lo = k * sample_seq_len
                conv_block(u[lo:lo + sample_seq_len], gate_b[lo:lo + sample_seq_len], lo, sample_seq_len)

        zk = project(lat_w, conv0)
        zk_swapped = jnp.concatenate([zk[:, ROPE_DIM // 2:], zk[:, :ROPE_DIM // 2]], axis=1)
        k_rope = zk * cos_ref[...] + zk_swapped * sin_ref[...]
        if is_prompt:
            krt_ref[...] = k_rope.T
        else:
            for k in range(TM // sample_seq_len):
                krt_ref[k] = k_rope[k * sample_seq_len:(k + 1) * sample_seq_len, :].T
        ckv_ref[...] = _rms(project(Q_LORA, lat_w), gkv_ref[...])
        cqn_ref[...] = _rms(project(0, Q_LORA), gq_ref[...]).astype(BF16)

    @pl.when(i < n_prompt_tiles)
    def _():
        tile(xp_ref, True)

    @pl.when(i >= n_prompt_tiles)
    def _():
        tile(xs_ref, False)


def _in_proj(xp, xs, gmix, w_t, gq, gkv, gco, convw, cosk, sink, state, *, seq_p, seq_s):
    np_rows, ns_rows = xp.shape[0], xs.shape[0]
    m = np_rows + ns_rows
    npt, nst = np_rows // TM, ns_rows // TM
    tps = seq_p // TM
    n_prompt_seq = np_rows // seq_p
    last_p = npt - 1

    def tab_idx(i):
        return (jnp.where(i < npt, i % tps, tps), 0)

    row = lambda i: (i, 0)
    prow = lambda i: (jnp.minimum(i, last_p), 0)
    srow = lambda i: (jnp.maximum(i - npt, 0), 0)
    kern = functools.partial(_in_proj_kernel, n_prompt_tiles=npt, tiles_per_seq=tps,
                             n_prompt_seq=n_prompt_seq, sample_seq_len=seq_s)
    return pl.pallas_call(
        kern,
        grid=(npt + nst,),
        in_specs=[
            pl.BlockSpec((TM, D_MODEL), prow),
            pl.BlockSpec((TM, D_MODEL), srow),
            _const_spec((1, D_MODEL)),
            pl.BlockSpec(memory_space=pl.ANY),
            _const_spec((1, Q_LORA)),
            _const_spec((1, KV_LORA)),
            _const_spec((1, CONV_CH)),
            _const_spec((CONV_W, CONV_CH)),
            pl.BlockSpec((TM, ROPE_DIM), tab_idx),
            pl.BlockSpec((TM, ROPE_DIM), tab_idx),
            _const_spec(state.shape),
        ],
        out_specs=[
            pl.BlockSpec((TM, Q_LORA), row),
            pl.BlockSpec((TM, KV_LORA), prow),
            pl.BlockSpec((None, ROPE_DIM, TM), lambda i: (jnp.minimum(i, last_p) // tps, 0,
                                                          jnp.minimum(i, last_p) % tps)),
            pl.BlockSpec((TM, KV_LORA), srow),
            pl.BlockSpec((TM // seq_s, ROPE_DIM, seq_s), lambda i: (jnp.maximum(i - npt, 0), 0, 0)),
            pl.BlockSpec((TM, CONV_CH), row),
            pl.BlockSpec((TM // CHUNK, SUBLANES, CONV_CH), lambda i: (i, 0, 0)),
        ],
        out_shape=[
            jax.ShapeDtypeStruct((m, Q_LORA), BF16),
            jax.ShapeDtypeStruct((np_rows, KV_LORA), F32),
            jax.ShapeDtypeStruct((n_prompt_seq, ROPE_DIM, seq_p), F32),
            jax.ShapeDtypeStruct((ns_rows, KV_LORA), F32),
            jax.ShapeDtypeStruct((ns_rows // seq_s, ROPE_DIM, seq_s), F32),
            jax.ShapeDtypeStruct((m, CONV_CH), BF16),
            jax.ShapeDtypeStruct((m // CHUNK, SUBLANES, CONV_CH), F32),
        ],
        scratch_shapes=[pltpu.VMEM((TM + SUBLANES, CONV_CH), F32),
                        pltpu.VMEM(w_t.shape, BF16),
                        pltpu.VMEM((2, w_t.shape[0] // W_IN_SLABS, w_t.shape[1]), F32),
                        pltpu.SemaphoreType.DMA((2,))],
        compiler_params=pltpu.CompilerParams(dimension_semantics=("arbitrary",),
                                             vmem_limit_bytes=VMEM_LIMIT),
        name="in_proj",
    )(xp, xs, gmix, w_t, gq, gkv, gco, convw, cosk, sink, state)


def _attn_kernel(cqn_ref, wq_ref, wuk_ref, wuv_ref, cos_ref, sin_ref, gao_ref, pkv_ref, pkr_ref, kv_ref, kr_ref,
                 out_ref, qlat_ref, qr_ref, m_ref, l_ref, acc_ref, s_ref, *, tq, n_past):
    rows = N_HEADS * tq

    q = jnp.dot(cqn_ref[...], wq_ref[...], preferred_element_type=F32)
    nope_w = N_HEADS * QK_NOPE
    rope_w = N_HEADS * ROPE_DIM
    qrope = q[:, nope_w:nope_w + rope_w] * cos_ref[...] + q[:, nope_w + rope_w:] * sin_ref[...]
    for h in range(N_HEADS):
        qn = q[:, h * QK_NOPE:(h + 1) * QK_NOPE].astype(BF16)
        ql = jnp.dot(qn, wuk_ref[h], preferred_element_type=F32)
        qlat_ref[h * tq:(h + 1) * tq, :] = (ql * EXP2_SCALE).astype(BF16)
        qr_ref[h * tq:(h + 1) * tq, :] = (qrope[:, h * ROPE_DIM:(h + 1) * ROPE_DIM] * EXP2_SCALE).astype(BF16)


    nt = (((1,), (1,)), ((), ()))

    def scores(kc_f32, krt_f32):
        s = lax.dot_general(qlat_ref[...], kc_f32.astype(BF16), nt, preferred_element_type=F32)
        return s + jnp.dot(qr_ref[...], krt_f32.astype(BF16), preferred_element_type=F32)

    def update(s, kc_f32, mask, first=False):
        if mask is not None:
            s = jnp.where(mask, s, NEG_BIG)
        m_cur = jnp.max(s, axis=-1, keepdims=True)
        if first:
            m_new = jnp.broadcast_to(m_cur, m_ref.shape)
        else:
            m_prev = m_ref[...]
            m_new = jnp.maximum(m_prev, m_cur)
            alpha = jnp.exp2(m_prev - m_new)
        p = jnp.exp2(s - _lane_bcast(m_new, s.shape[1]))
        l_cur = jnp.sum(p, axis=-1, keepdims=True)
        pv = jnp.dot(p.astype(BF16), kc_f32.astype(BF16), preferred_element_type=F32)
        if first:
            l_ref[...] = jnp.broadcast_to(l_cur, l_ref.shape)
            acc_ref[...] = pv
        else:
            l_ref[...] = alpha * l_ref[...] + l_cur
            acc_ref[...] = _lane_bcast(alpha, KV_LORA) * acc_ref[...] + pv
        m_ref[...] = m_new

    def pipelined(kv, kr, lo, hi, last, mask_fn):
        def body(j, c):
            k0 = pl.multiple_of(j * TK, TK)
            k1 = pl.multiple_of(jnp.minimum(j + 1, last) * TK, TK)
            s_cur = s_ref[j % 2]
            s_ref[(j + 1) % 2] = scores(kv[pl.ds(k1, TK), :], kr[:, pl.ds(k1, TK)])
            update(s_cur, kv[pl.ds(k0, TK), :], None if mask_fn is None else mask_fn(k0))
            return c
        lax.fori_loop(lo, hi, body, 0)

    def pipelined_pairs(kv, kr, n_pairs, last):
        def body(i, c):
            ka = pl.multiple_of((2 * i + 1) * TK, TK)
            kb = pl.multiple_of((2 * i + 2) * TK, TK)
            kc = pl.multiple_of(jnp.minimum(2 * i + 3, last) * TK, TK)
            s_ref[0] = scores(kv[pl.ds(kb, TK), :], kr[:, pl.ds(kb, TK)])
            update(s_ref[1], kv[pl.ds(ka, TK), :], None)
            s_ref[1] = scores(kv[pl.ds(kc, TK), :], kr[:, pl.ds(kc, TK)])
            update(s_ref[0], kv[pl.ds(kb, TK), :], None)
            return c
        lax.fori_loop(0, n_pairs, body, 0)

    def first_block(kv, kr, last, mask):
        k1 = pl.multiple_of(jnp.minimum(1, last) * TK, TK)
        s_ref[0] = scores(kv[pl.ds(0, TK), :], kr[:, pl.ds(0, TK)])
        s_ref[1] = scores(kv[pl.ds(k1, TK), :], kr[:, pl.ds(k1, TK)])
        update(s_ref[0], kv[pl.ds(0, TK), :], mask, first=True)

    n_pb = n_past // TK
    first_block(pkv_ref, pkr_ref, n_pb - 1, None)
    n_pairs = (n_pb - 1) // 2
    pipelined_pairs(pkv_ref, pkr_ref, n_pairs, n_pb - 1)
    if 1 + 2 * n_pairs < n_pb:
        pipelined(pkv_ref, pkr_ref, 1 + 2 * n_pairs, n_pb, n_pb - 1, None)

    update(scores(kv_ref[...], kr_ref[...]), kv_ref[...], None)

    o = acc_ref[...] / _lane_bcast(l_ref[...], KV_LORA)
    parts = []
    for h in range(N_HEADS):
        oh = o[h * tq:(h + 1) * tq, :].astype(BF16)
        parts.append(jnp.dot(oh, wuv_ref[h], preferred_element_type=F32))
    attn = jnp.concatenate(parts, axis=-1)
    out_ref[...] = _rms(attn, gao_ref[...]).astype(BF16)


def _attention(cqn, w_q, w_ukt, w_uv, cosq, sinq, gao, ckv, krope, past_kv, past_kr, *, n_batch, seq, row0):
    tq = seq
    n_past = past_kv.shape[1]
    assert n_past % CHUNK == 0 and seq <= CHUNK and n_past % TK == 0
    blk0 = row0 // tq
    in_specs = [
        pl.BlockSpec((tq, Q_LORA), lambda b, q: (blk0 + b, 0)),
        _const_spec(w_q.shape),
        _const_spec(w_ukt.shape),
        _const_spec(w_uv.shape),
        pl.BlockSpec((tq, N_HEADS * ROPE_DIM), lambda b, q: (0, 0)),
        pl.BlockSpec((tq, N_HEADS * ROPE_DIM), lambda b, q: (0, 0)),
        _const_spec((1, ATTN_W)),
        pl.BlockSpec((None, n_past, KV_LORA), lambda b, q: (b, 0, 0)),
        pl.BlockSpec((None, ROPE_DIM, n_past), lambda b, q: (b, 0, 0)),
        pl.BlockSpec((seq, KV_LORA), lambda b, q: (b, 0)),
        pl.BlockSpec((None, ROPE_DIM, seq), lambda b, q: (b, 0, 0)),
    ]
    args = [cqn, w_q, w_ukt, w_uv, cosq, sinq, gao, past_kv, past_kr, ckv, krope]
    rows = N_HEADS * tq
    return pl.pallas_call(
        functools.partial(_attn_kernel, tq=tq, n_past=n_past),
        grid=(n_batch, 1),
        in_specs=in_specs,
        out_specs=pl.BlockSpec((tq, ATTN_W), lambda b, q: (b, 0)),
        out_shape=jax.ShapeDtypeStruct((n_batch * seq, ATTN_W), BF16),
        scratch_shapes=[
            pltpu.VMEM((rows, KV_LORA), BF16),
            pltpu.VMEM((rows, ROPE_DIM), BF16),
            pltpu.VMEM((rows, LANES), F32),
            pltpu.VMEM((rows, LANES), F32),
            pltpu.VMEM((rows, KV_LORA), F32),
            pltpu.VMEM((2, rows, TK), F32),
        ],
        compiler_params=pltpu.CompilerParams(dimension_semantics=("arbitrary", "arbitrary"),
                                             vmem_limit_bytes=VMEM_LIMIT),
        name="attn_sample",
    )(*args)


def _attn_heads_kernel(cqn_ref, wq_ref, wkv_ref, cos_ref, sin_ref, gao_ref, kv_ref, krt_ref, out_ref,
                       kcat_ref, vh_ref, qcat_ref, m_ref, l_ref, acc_ref, klim_ref, *, tq, seq):
    qi = pl.program_id(1)
    nt = (((1,), (1,)), ((), ()))
    kw = QK_NOPE + ROPE_DIM
    kpad = kcat_ref.shape[-1]

    @pl.when(qi == 0)
    def _():
        def expand(j, c):
            k0 = pl.multiple_of(j * TK, TK)
            latent = kv_ref[pl.ds(k0, TK), :].astype(BF16)
            kvh = jnp.dot(latent, wkv_ref[...], preferred_element_type=F32)
            k_rope = krt_ref[:, pl.ds(k0, TK)].T.astype(BF16)
            for h in range(N_HEADS):
                kcat_ref[h, pl.ds(k0, TK), :QK_NOPE] = kvh[:, h * QK_NOPE:(h + 1) * QK_NOPE].astype(BF16)
                kcat_ref[h, pl.ds(k0, TK), QK_NOPE:kw] = k_rope
                kcat_ref[h, pl.ds(k0, TK), kw:] = jnp.zeros((TK, kpad - kw), BF16)
                v0 = N_HEADS * QK_NOPE + h * V_DIM
                vh_ref[h, pl.ds(k0, TK), :] = kvh[:, v0:v0 + V_DIM].astype(BF16)
            return c
        lax.fori_loop(0, seq // TK, expand, 0)

    q = jnp.dot(cqn_ref[...], wq_ref[...], preferred_element_type=F32)
    nope_w = N_HEADS * QK_NOPE
    rope_w = N_HEADS * ROPE_DIM
    qrope = q[:, nope_w:nope_w + rope_w] * cos_ref[...] + q[:, nope_w + rope_w:] * sin_ref[...]
    for h in range(N_HEADS):
        qcat_ref[h, :, :QK_NOPE] = (q[:, h * QK_NOPE:(h + 1) * QK_NOPE] * EXP2_SCALE).astype(BF16)
        qcat_ref[h, :, QK_NOPE:kw] = (qrope[:, h * ROPE_DIM:(h + 1) * ROPE_DIM] * EXP2_SCALE).astype(BF16)
        qcat_ref[h, :, kw:] = jnp.zeros((tq, kpad - kw), BF16)

    r = lax.broadcasted_iota(jnp.int32, (tq, LANES), 0)
    klim_ref[...] = ((qi * tq + r) & ~(CHUNK - 1)) + CHUNK

    tk = TKH

    def block(k0, masked, first):
        if masked:
            cidx = lax.broadcasted_iota(jnp.int32, (tq, tk), 1)
            mask = cidx < _lane_bcast(klim_ref[...] - k0, tk)
        for h in range(N_HEADS):
            s = lax.dot_general(qcat_ref[h], kcat_ref[h, pl.ds(k0, tk), :], nt, preferred_element_type=F32)
            if masked:
                s = jnp.where(mask, s, NEG_BIG)
            m_cur = jnp.max(s, axis=-1, keepdims=True)
            if first:
                m_new = jnp.broadcast_to(m_cur, (tq, LANES))
            else:
                m_prev = m_ref[h]
                m_new = jnp.maximum(m_prev, m_cur)
                alpha = jnp.exp2(m_prev - m_new)
            p = jnp.exp2(s - _lane_bcast(m_new, tk))
            l_cur = jnp.sum(p, axis=-1, keepdims=True)
            pv = jnp.dot(p.astype(BF16), vh_ref[h, pl.ds(k0, tk), :], preferred_element_type=F32)
            if first:
                l_ref[h] = jnp.broadcast_to(l_cur, (tq, LANES))
                acc_ref[h] = pv
            else:
                l_ref[h] = alpha * l_ref[h] + l_cur
                acc_ref[h] = _lane_bcast(alpha, V_DIM) * acc_ref[h] + pv
            m_ref[h] = m_new

    n_blocks = ((qi + 1) * tq + tk - 1) // tk
    n_full = jnp.minimum((qi * tq // CHUNK + 1) * CHUNK // tk, n_blocks)

    def loop(lo, hi, masked):
        def body(j, c):
            block(pl.multiple_of(j * tk, tk), masked, False)
            return c
        lax.fori_loop(lo, hi, body, 0)

    block(0, True, True)
    loop(1, n_full, False)
    loop(jnp.maximum(n_full, 1), n_blocks, True)

    attn = jnp.concatenate([acc_ref[h] / _lane_bcast(l_ref[h], V_DIM) for h in range(N_HEADS)], axis=-1)
    out_ref[...] = _rms(attn, gao_ref[...]).astype(BF16)


def _attention_heads(cqn, w_q, w_kv, cosq, sinq, gao, ckv, krope_t, *, n_batch, seq):
    nq = seq // TQ
    kpad = 2 * LANES
    assert QK_NOPE + ROPE_DIM <= kpad and V_DIM == LANES
    assert seq % TQ == 0 and seq % TKH == 0 and seq % TK == 0 and TQ % CHUNK == 0 and TKH % CHUNK == 0
    return pl.pallas_call(
        functools.partial(_attn_heads_kernel, tq=TQ, seq=seq),
        grid=(n_batch, nq),
        in_specs=[
            pl.BlockSpec((TQ, Q_LORA), lambda b, q: (b * nq + q, 0)),
            _const_spec(w_q.shape),
            _const_spec(w_kv.shape),
            pl.BlockSpec((TQ, N_HEADS * ROPE_DIM), lambda b, q: (q, 0)),
            pl.BlockSpec((TQ, N_HEADS * ROPE_DIM), lambda b, q: (q, 0)),
            _const_spec((1, ATTN_W)),
            pl.BlockSpec((seq, KV_LORA), lambda b, q: (b, 0)),
            pl.BlockSpec((None, ROPE_DIM, seq), lambda b, q: (b, 0, 0)),
        ],
        out_specs=pl.BlockSpec((TQ, ATTN_W), lambda b, q: (b * nq + q, 0)),
        out_shape=jax.ShapeDtypeStruct((n_batch * seq, ATTN_W), BF16),
        scratch_shapes=[
            pltpu.VMEM((N_HEADS, seq, kpad), BF16),
            pltpu.VMEM((N_HEADS, seq, V_DIM), BF16),
            pltpu.VMEM((N_HEADS, TQ, kpad), BF16),
            pltpu.VMEM((N_HEADS, TQ, LANES), F32),
            pltpu.VMEM((N_HEADS, TQ, LANES), F32),
            pltpu.VMEM((N_HEADS, TQ, V_DIM), F32),
            pltpu.VMEM((TQ, LANES), jnp.int32),
        ],
        compiler_params=pltpu.CompilerParams(dimension_semantics=("arbitrary", "arbitrary"),
                                             vmem_limit_bytes=VMEM_LIMIT),
        name="attn_prompt",
    )(cqn, w_q, w_kv, cosq, sinq, gao, ckv, krope_t)


def _out_proj_kernel(attnp_ref, attns_ref, convn_ref, xp_ref, xs_ref, wo_hbm, gffn_ref, wr_ref,
                     br_ref, h_ref, xpk_ref, mi_ref, mf_ref, cnt_ref, carry_ref, logit_ref,
                     wo_ref, stage_ref, wsems, *, n_prompt_tiles):
    i = pl.program_id(0)

    @pl.when(i == 0)
    def _():
        carry_ref[...] = jnp.zeros(carry_ref.shape, F32)
        logit_ref[...] = jnp.zeros(logit_ref.shape, F32)
        _load_weight_as_bf16(wo_hbm, wo_ref, stage_ref, wsems)

    def tile(x_ref, attn_ref):
        prev_logits = logit_ref[...]
        y = jnp.dot(attn_ref[...], wo_ref[:ATTN_W, :], preferred_element_type=F32)
        y = y + jnp.dot(convn_ref[...], wo_ref[ATTN_W:, :], preferred_element_type=F32)
        h = x_ref[...] + y
        h_ref[...] = h
        xn = _rms(h, gffn_ref[...])

        half = D_MODEL // 2
        xh = xn.astype(BF16)
        xh32 = xh.astype(F32)
        lo = lax.bitcast_convert_type(xh32[:, :half], jnp.uint32)
        hi = lax.bitcast_convert_type(xh32[:, half:], jnp.uint32)
        xpk_ref[...] = (lo >> 16) | (hi & jnp.uint32(0xFFFF0000))

        xl = (xn - xh32).astype(BF16)
        hh_hl = jnp.dot(xh, wr_ref[...], preferred_element_type=F32)
        lh = jnp.dot(xl, wr_ref[:, :LANES], preferred_element_type=F32)
        logit_ref[...] = hh_hl[:, :LANES] + (lh + hh_hl[:, LANES:]) + br_ref[...]

        logits = prev_logits
        counted = (i > 0).astype(F32)
        lane = lax.broadcasted_iota(jnp.int32, (TM, LANES), 1).astype(F32)
        ninf = -jnp.inf
        far = float(LANES)

        def first_argmax(v):
            vmax = jnp.max(v, axis=-1, keepdims=True)
            return vmax, jnp.min(jnp.where(v == vmax, lane, far), axis=-1, keepdims=True)

        gl = jnp.where(lane < N_GROUPS, logits, ninf)
        gmax, gidx = first_argmax(gl)
        g_p = 1.0 / jnp.sum(jnp.exp(gl - gmax), axis=-1, keepdims=True)
        e_lo = N_GROUPS + EXPERTS_PER_GROUP * gidx
        el = jnp.where((lane >= e_lo) & (lane < e_lo + EXPERTS_PER_GROUP), logits, ninf)
        e1max, i1 = first_argmax(el)
        z = jnp.sum(jnp.exp(el - e1max), axis=-1, keepdims=True)
        el2 = jnp.where(lane == i1, ninf, el)
        e2max, i2 = first_argmax(el2)
        p1 = 1.0 / z
        p2 = jnp.exp(e2max - e1max) / z
        den = p1 + p2
        g0 = g_p * p1 / den
        g1 = g_p * p2 / den
        e0 = i1 - N_GROUPS
        e1 = i2 - N_GROUPS

        oh0 = lane == e0
        oh1 = lane == e1
        oh = jnp.where(oh0 | oh1, 1.0, 0.0)
        r = lax.broadcasted_iota(jnp.int32, (TM, TM), 0)
        c = lax.broadcasted_iota(jnp.int32, (TM, TM), 1)
        ltri = jnp.where(r > c, 1.0, 0.0).astype(BF16)
        before = jnp.dot(ltri, oh.astype(BF16), preferred_element_type=F32) + carry_ref[...]
        rank0 = jnp.sum(jnp.where(oh0, before, 0.0), axis=-1, keepdims=True)
        rank1 = jnp.sum(jnp.where(oh1, before, 0.0), axis=-1, keepdims=True)
        total = carry_ref[...] + counted * jnp.sum(oh, axis=0, keepdims=True)
        carry_ref[...] = total
        cnt_ref[...] = jnp.broadcast_to(total, cnt_ref.shape)

        mi = jnp.where(lane == 0, e0, jnp.where(lane == 1, e1, jnp.where(lane == 2, rank0, rank1)))
        mi_ref[...] = jnp.transpose(mi)[:SUBLANES, :].astype(jnp.int32)
        mf_ref[...] = jnp.where(lane == 0, g0, g1)

    @pl.when(i < n_prompt_tiles)
    def _():
        tile(xp_ref, attnp_ref)

    @pl.when(i >= n_prompt_tiles)
    def _():
        tile(xs_ref, attns_ref)


def _out_proj(attn_p, attn_s, conv_n, xp, xs, w_ob, gffn, w_r2, b_r):
    m = conv_n.shape[0]
    npt = xp.shape[0] // TM
    n_tiles = m // TM
    last_p, last_s, last = npt - 1, n_tiles - npt - 1, n_tiles - 1
    row = lambda i: (jnp.minimum(i, last), 0)
    prow = lambda i: (jnp.minimum(i, last_p), 0)
    srow = lambda i: (jnp.clip(i - npt, 0, last_s), 0)
    lag = lambda i: jnp.maximum(i - 1, 0)
    return pl.pallas_call(
        functools.partial(_out_proj_kernel, n_prompt_tiles=npt),
        grid=(n_tiles + 1,),
        in_specs=[
            pl.BlockSpec((TM, ATTN_W), prow),
            pl.BlockSpec((TM, ATTN_W), srow),
            pl.BlockSpec((TM, CONV_CH), row),
            pl.BlockSpec((TM, D_MODEL), prow),
            pl.BlockSpec((TM, D_MODEL), srow),
            pl.BlockSpec(memory_space=pl.ANY),
            _const_spec((1, D_MODEL)),
            _const_spec(w_r2.shape),
            _const_spec((1, LANES)),
        ],
        out_specs=[
            pl.BlockSpec((TM, D_MODEL), row),
            pl.BlockSpec((TM, D_MODEL // 2), row),
            pl.BlockSpec((SUBLANES, TM), lambda i: (0, lag(i))),
            pl.BlockSpec((TM, LANES), lambda i: (lag(i), 0)),
            pl.BlockSpec((SUBLANES, LANES), lambda i: (0, 0)),
        ],
        out_shape=[
            jax.ShapeDtypeStruct((m, D_MODEL), F32),
            jax.ShapeDtypeStruct((m, D_MODEL // 2), jnp.uint32),
            jax.ShapeDtypeStruct((SUBLANES, m), jnp.int32),
            jax.ShapeDtypeStruct((m, LANES), F32),
            jax.ShapeDtypeStruct((SUBLANES, LANES), F32),
        ],
        scratch_shapes=[pltpu.VMEM((1, LANES), F32), pltpu.VMEM((TM, LANES), F32),
                        pltpu.VMEM(w_ob.shape, BF16),
                        pltpu.VMEM((2, w_ob.shape[0] // W_O_SLABS, w_ob.shape[1]), F32),
                        pltpu.SemaphoreType.DMA((2,))],
        compiler_params=pltpu.CompilerParams(dimension_semantics=("arbitrary",),
                                             vmem_limit_bytes=VMEM_LIMIT),
        name="out_proj",
    )(attn_p, attn_s, conv_n, xp, xs, w_ob, gffn, w_r2, b_r)


def _dispatch_kernel(d0_ref, d1_ref, zlo_ref, zn_ref, nu_ref, xpk_ref, xs_hbm, zeros_ref, sems, *, n_blocks):
    i = pl.program_id(0)
    sem = sems.at[0]
    zsem = sems.at[1]

    def zero_fill(act):
        def per_expert(e, c):
            lo = zlo_ref[e]
            n = zn_ref[e]
            head = (-lo) & (SUBLANES - 1)
            for r in range(SUBLANES - 1):
                @pl.when(r < head)
                def _(r=r):
                    act(pltpu.make_async_copy(zeros_ref.at[pl.ds(0, 1)], xs_hbm.at[pl.ds(lo + r, 1)], zsem))
            off = lo + head
            rest = n - head
            size = MOE_BLOCK // 2
            while size >= SUBLANES:
                @pl.when((rest & size) != 0)
                def _(off=off, size=size):
                    dst = xs_hbm.at[pl.ds(pl.multiple_of(off, SUBLANES), size)]
                    act(pltpu.make_async_copy(zeros_ref.at[pl.ds(0, size)], dst, zsem))
                off = off + (rest & size)
                size //= 2
            return c

        def per_block(b, c):
            dst = xs_hbm.at[pl.ds(pl.multiple_of(b * MOE_BLOCK, MOE_BLOCK), MOE_BLOCK)]
            act(pltpu.make_async_copy(zeros_ref, dst, zsem))
            return c

        lax.fori_loop(0, N_EXPERTS, per_expert, 0)
        lax.fori_loop(nu_ref[0], n_blocks, per_block, 0)

    @pl.when(i == 0)
    def _():
        zeros_ref[...] = jnp.zeros(zeros_ref.shape, zeros_ref.dtype)
        zero_fill(lambda cp: cp.start())

    @pl.when(i == pl.num_programs(0) - 1)
    def _():
        zero_fill(lambda cp: cp.wait())

    base = i * TD

    def start(g, c):
        for u in range(SUBLANES):
            r = base + g * SUBLANES + u
            src = xpk_ref.at[g, pl.ds(u, 1)]
            pltpu.make_async_copy(src, xs_hbm.at[pl.ds(d0_ref[r], 1)], sem).start()
            pltpu.make_async_copy(src, xs_hbm.at[pl.ds(d1_ref[r], 1)], sem).start()
        return c

    lax.fori_loop(0, TD // SUBLANES, start, 0)
    for _ in range(2):
        pltpu.make_async_copy(xs_hbm.at[pl.ds(0, TD)], xs_hbm.at[pl.ds(0, TD)], sem).wait()


def _dispatch(dest0, dest1, pad_lo, n_pad, n_used, xpk, n_blocks):
    m = xpk.shape[0]
    grid_spec = pltpu.PrefetchScalarGridSpec(
        num_scalar_prefetch=5,
        grid=(m // TD,),
        in_specs=[pl.BlockSpec((TD // SUBLANES, SUBLANES, D_MODEL // 2), lambda i, *_: (i, 0, 0))],
        out_specs=pl.BlockSpec(memory_space=pl.ANY),
        scratch_shapes=[pltpu.VMEM((MOE_BLOCK, D_MODEL // 2), jnp.uint32),
                        pltpu.SemaphoreType.DMA((2,))],
    )
    return pl.pallas_call(
        functools.partial(_dispatch_kernel, n_blocks=n_blocks),
        grid_spec=grid_spec,
        out_shape=jax.ShapeDtypeStruct((n_blocks * MOE_BLOCK, D_MODEL // 2), jnp.uint32),
        compiler_params=pltpu.CompilerParams(dimension_semantics=("arbitrary",)),
        name="dispatch",
    )(dest0, dest1, pad_lo, n_pad, n_used, xpk.reshape(m // SUBLANES, SUBLANES, D_MODEL // 2))


def _experts_kernel(be_ref, nu_ref, nxt_ref, x_ref, wg_hbm, wu_hbm, wd_hbm, y_ref,
                    sg_ref, su_ref, sd_ref, wgb_ref, wub_ref, wdb_ref, sems):
    b = pl.program_id(0)
    active = b < nu_ref[0]
    new_expert = jnp.logical_or(b == 0, be_ref[b] != be_ref[jnp.maximum(b - 1, 0)])

    def weight_copies(e):
        return (pltpu.make_async_copy(wg_hbm.at[e], sg_ref, sems.at[0]),
                pltpu.make_async_copy(wu_hbm.at[e], su_ref, sems.at[1]),
                pltpu.make_async_copy(wd_hbm.at[e], sd_ref, sems.at[2]))

    @pl.when(b == 0)
    def _():
        for cp in weight_copies(be_ref[0]):
            cp.start()

    @pl.when(jnp.logical_and(active, new_expert))
    def _():
        for cp in weight_copies(be_ref[b]):
            cp.wait()
        wgb_ref[...] = sg_ref[...].astype(BF16)
        wub_ref[...] = su_ref[...].astype(BF16)
        wdb_ref[...] = sd_ref[...].astype(BF16)

        @pl.when(nxt_ref[b] >= 0)
        def _():
            for cp in weight_copies(nxt_ref[b]):
                cp.start()

    @pl.when(active)
    def _():
        half = D_MODEL // 2
        xa, xb = (v.astype(BF16) for v in _unpack_bf16_pairs(x_ref[...]))
        g = jnp.dot(xa, wgb_ref[:half, :], preferred_element_type=F32)
        g = g + jnp.dot(xb, wgb_ref[half:, :], preferred_element_type=F32)
        u = jnp.dot(xa, wub_ref[:half, :], preferred_element_type=F32)
        u = u + jnp.dot(xb, wub_ref[half:, :], preferred_element_type=F32)
        hmid = (g * jax.nn.sigmoid(g)) * u
        y = jnp.dot(hmid.astype(BF16), wdb_ref[...], preferred_element_type=F32)
        y_ref[...] = _pack_bf16_pairs(y)

    @pl.when(b >= nu_ref[0])
    def _():
        y_ref[...] = jnp.zeros(y_ref.shape, y_ref.dtype)


def _experts(block_e, n_used, next_e, x_sorted, w_gate, w_up, w_down):
    p = x_sorted.shape[0]
    nb = p // MOE_BLOCK

    def xrow(b, be, nu, nxt):
        return (jnp.maximum(jnp.minimum(b, nu[0] - 1), 0), 0)

    grid_spec = pltpu.PrefetchScalarGridSpec(
        num_scalar_prefetch=3,
        grid=(nb,),
        in_specs=[
            pl.BlockSpec((MOE_BLOCK, D_MODEL // 2), xrow),
            pl.BlockSpec(memory_space=pl.ANY),
            pl.BlockSpec(memory_space=pl.ANY),
            pl.BlockSpec(memory_space=pl.ANY),
        ],
        out_specs=pl.BlockSpec((MOE_BLOCK, D_MODEL // 2), lambda b, be, nu, nxt: (b, 0)),
        scratch_shapes=[pltpu.VMEM((D_MODEL, D_FF), F32), pltpu.VMEM((D_MODEL, D_FF), F32),
                        pltpu.VMEM((D_FF, D_MODEL), F32),
                        pltpu.VMEM((D_MODEL, D_FF), BF16), pltpu.VMEM((D_MODEL, D_FF), BF16),
                        pltpu.VMEM((D_FF, D_MODEL), BF16),
                        pltpu.SemaphoreType.DMA((3,))],
    )
    return pl.pallas_call(
        _experts_kernel,
        grid_spec=grid_spec,
        out_shape=jax.ShapeDtypeStruct((p, D_MODEL // 2), jnp.uint32),
        compiler_params=pltpu.CompilerParams(dimension_semantics=("arbitrary",),
                                             vmem_limit_bytes=VMEM_LIMIT),
        name="experts",
    )(block_e, n_used, next_e, x_sorted, w_gate, w_up, w_down)


def _combine_kernel(d0_ref, d1_ref, h_ref, mf_ref, gfin_ref, y_hbm, outp_ref, outs_ref, y0_ref, y1_ref, sems,
                    *, n_tiles, n_prompt_tiles):
    i = pl.program_id(0)

    def gather(tile, slot, act):
        base = tile * TM

        def body(g, c):
            for u in range(SUBLANES):
                r = base + g * SUBLANES + u
                act(pltpu.make_async_copy(y_hbm.at[pl.ds(d0_ref[r], 1)], y0_ref.at[slot, g, pl.ds(u, 1)],
                                          sems.at[slot]))
                act(pltpu.make_async_copy(y_hbm.at[pl.ds(d1_ref[r], 1)], y1_ref.at[slot, g, pl.ds(u, 1)],
                                          sems.at[slot]))
            return c
        lax.fori_loop(0, TM // SUBLANES, body, 0)

    @pl.when(i == 0)
    def _():
        gather(0, 0, lambda cp: cp.start())

    @pl.when(i + 1 < n_tiles)
    def _():
        gather(i + 1, (i + 1) % 2, lambda cp: cp.start())

    slot = i % 2
    for _ in range(2):
        pltpu.make_async_copy(y_hbm.at[pl.ds(0, TM)], y_hbm.at[pl.ds(0, TM)], sems.at[slot]).wait()

    def finish(out_ref):
        mf = mf_ref[...]
        g0, g1 = mf[:, :, 0:1], mf[:, :, 1:2]
        half = D_MODEL // 2
        a0, b0 = _unpack_bf16_pairs(y0_ref[slot])
        a1, b1 = _unpack_bf16_pairs(y1_ref[slot])
        o_lo = h_ref[:, :, :half] + (g0 * a0 + g1 * a1)
        o_hi = h_ref[:, :, half:] + (g0 * b0 + g1 * b1)
        sumsq = jnp.sum(o_lo * o_lo, axis=-1, keepdims=True) + jnp.sum(o_hi * o_hi, axis=-1, keepdims=True)
        inv_rms = lax.rsqrt(sumsq / D_MODEL + EPS)
        out_ref[:, :, :half] = o_lo * inv_rms * gfin_ref[:, :, :half]
        out_ref[:, :, half:] = o_hi * inv_rms * gfin_ref[:, :, half:]

    @pl.when(i < n_prompt_tiles)
    def _():
        finish(outp_ref)

    @pl.when(i >= n_prompt_tiles)
    def _():
        finish(outs_ref)


def _combine(dest0, dest1, h, mf, gfin, y_sorted, *, n_prompt_rows):
    m = h.shape[0]
    npt = n_prompt_rows // TM
    tg = TM // SUBLANES
    grouped = lambda a: a.reshape(a.shape[0] // SUBLANES, SUBLANES, a.shape[1])
    grid_spec = pltpu.PrefetchScalarGridSpec(
        num_scalar_prefetch=2,
        grid=(m // TM,),
        in_specs=[
            pl.BlockSpec((tg, SUBLANES, D_MODEL), lambda i, *_: (i, 0, 0)),
            pl.BlockSpec((tg, SUBLANES, LANES), lambda i, *_: (i, 0, 0)),
            pl.BlockSpec((1, 1, D_MODEL), lambda i, *_: (0, 0, 0)),
            pl.BlockSpec(memory_space=pl.ANY),
        ],
        out_specs=[pl.BlockSpec((tg, SUBLANES, D_MODEL), lambda i, *_: (jnp.minimum(i, npt - 1), 0, 0)),
                   pl.BlockSpec((tg, SUBLANES, D_MODEL), lambda i, *_: (jnp.maximum(i - npt, 0), 0, 0))],
        scratch_shapes=[pltpu.VMEM((2, tg, SUBLANES, D_MODEL // 2), jnp.uint32),
                        pltpu.VMEM((2, tg, SUBLANES, D_MODEL // 2), jnp.uint32),
                        pltpu.SemaphoreType.DMA((2,))],
    )
    y_p, y_s = pl.pallas_call(
        functools.partial(_combine_kernel, n_tiles=m // TM, n_prompt_tiles=npt),
        grid_spec=grid_spec,
        out_shape=[jax.ShapeDtypeStruct((n_prompt_rows // SUBLANES, SUBLANES, D_MODEL), F32),
                   jax.ShapeDtypeStruct(((m - n_prompt_rows) // SUBLANES, SUBLANES, D_MODEL), F32)],
        compiler_params=pltpu.CompilerParams(dimension_semantics=("arbitrary",),
                                             vmem_limit_bytes=VMEM_LIMIT),
        name="combine",
    )(dest0, dest1, grouped(h), grouped(mf), gfin.reshape(1, 1, D_MODEL), y_sorted)
    return y_p.reshape(n_prompt_rows, D_MODEL), y_s.reshape(m - n_prompt_rows, D_MODEL)


def _rope_tables(pos):
    f32 = np.float32
    inv = np.power(f32(ROPE_THETA), -np.arange(0, ROPE_DIM, 2, dtype=f32) / f32(ROPE_DIM)).astype(f32)
    ang = (pos.astype(f32)[:, None] * inv[None, :]).astype(f32)
    cos, sin = np.cos(ang).astype(f32), np.sin(ang).astype(f32)
    return np.concatenate([cos, cos], axis=-1), np.concatenate([-sin, sin], axis=-1)


def _swap_halves(w):
    return jnp.concatenate([w[..., ROPE_DIM // 2:], w[..., :ROPE_DIM // 2]], axis=-1)


def kernel(x_prompt, x_sample, cache_kv_latent, cache_k_rope, state_conv, norm_mix, w_in, norm_q, w_uq,
           norm_kv, w_uk, w_uv, conv_w, norm_attn_out, norm_conv_out, w_o, norm_ffn, w_router_group,
           b_router_group, w_router_expert, b_router_expert, w_gate, w_up, w_down, norm_final):
    assert w_in.shape[0] == 1, "single-layer trunk"
    bp, seq_p, _ = x_prompt.shape
    bs, seq_s, _ = x_sample.shape
    past_len = cache_kv_latent.shape[2]
    np_rows, ns_rows = bp * seq_p, bs * seq_s
    m = np_rows + ns_rows
    assert seq_p % TM == 0 and TM % seq_s == 0 and ns_rows % TM == 0 and seq_s == CHUNK
    assert m % TD == 0

    xp = x_prompt.reshape(np_rows, D_MODEL)
    xs = x_sample.reshape(ns_rows, D_MODEL)
    row_vec = lambda v: v.reshape(1, -1)

    assert w_in.shape[2] == Q_LORA + KV_LORA + ROPE_DIM + 3 * CONV_CH
    w_t = jnp.swapaxes(w_in[0], 0, 1)
    assert w_t.shape[0] % (W_IN_SLABS * 2 * SUBLANES) == 0
    wq4 = w_uq[0].reshape(Q_LORA, N_HEADS, QK_NOPE + ROPE_DIM)
    wq_rope = wq4[:, :, QK_NOPE:]
    w_q = jnp.concatenate([wq4[:, :, :QK_NOPE].reshape(Q_LORA, -1), wq_rope.reshape(Q_LORA, -1),
                           _swap_halves(wq_rope).reshape(Q_LORA, -1)], axis=1).astype(BF16)
    w_ukt = jnp.transpose(w_uk[0], (1, 2, 0)).astype(BF16)
    w_uvh = jnp.transpose(w_uv[0], (1, 0, 2)).astype(BF16)
    w_ob = w_o[0]
    n_router = N_GROUPS + N_EXPERTS
    w_r = jnp.concatenate([w_router_group[0], w_router_expert[0].reshape(D_MODEL, N_EXPERTS)], axis=1)
    w_r = jnp.pad(w_r, ((0, 0), (0, LANES - n_router)))
    w_rh = w_r.astype(BF16)
    w_rl = (w_r - w_rh.astype(F32)).astype(BF16)
    w_r2 = jnp.concatenate([w_rh, w_rl], axis=1)
    b_r =jnp.pad(jnp.concatenate([b_router_group[0], b_router_expert[0].reshape(N_EXPERTS)]),
                  (0, LANES - n_router)).reshape(1, LANES)

    cos_p, sin_p = _rope_tables(np.arange(seq_p))
    cos_s, sin_s = _rope_tables(past_len + np.arange(seq_s))
    cosk = np.concatenate([cos_p, np.tile(cos_s, (TM // seq_s, 1))], axis=0)
    sink = np.concatenate([sin_p, np.tile(sin_s, (TM // seq_s, 1))], axis=0)
    state = jnp.concatenate([jnp.zeros((bp, CONV_W - 1, CONV_CH), F32), state_conv[0]], axis=0)

    cqn, ckv_p, kr_p, ckv_s, kr_s, conv_n, utail = _in_proj(
        xp, xs, row_vec(norm_mix[0]), w_t, row_vec(norm_q[0]), row_vec(norm_kv[0]),
        row_vec(norm_conv_out[0]), conv_w[0], cosk, sink, state, seq_p=seq_p, seq_s=seq_s)

    gao = row_vec(norm_attn_out[0])
    w_kv = jnp.concatenate([w_uk[0].reshape(KV_LORA, N_HEADS * QK_NOPE),
                            w_uv[0].reshape(KV_LORA, N_HEADS * V_DIM)], axis=1).astype(BF16)
    attn_p = _attention_heads(cqn, w_q, w_kv, np.tile(cos_p, (1, N_HEADS)), np.tile(sin_p, (1, N_HEADS)),
                              gao, ckv_p, kr_p, n_batch=bp, seq=seq_p)
    attn_s = _attention(cqn, w_q, w_ukt, w_uvh, np.tile(cos_s, (1, N_HEADS)), np.tile(sin_s, (1, N_HEADS)),
                        gao, ckv_s, kr_s, cache_kv_latent[0], jnp.swapaxes(cache_k_rope[0], 1, 2),
                        n_batch=bs, seq=seq_s, row0=np_rows)

    h, xpk, mi, mf, cnt = _out_proj(attn_p, attn_s, conv_n, xp, xs, w_ob, row_vec(norm_ffn[0]),
                                    w_r2, b_r)

    counts = cnt[0, :N_EXPERTS].astype(jnp.int32)
    padded = (counts + MOE_BLOCK - 1) // MOE_BLOCK * MOE_BLOCK
    pad_end = jnp.cumsum(padded)
    pad_start = pad_end - padded
    n_blocks = -(-(m * 2) // MOE_BLOCK) + N_EXPERTS
    block_row0 = jnp.arange(n_blocks, dtype=jnp.int32) * MOE_BLOCK
    block_e = jnp.minimum(jnp.sum((pad_end[None, :] <= block_row0[:, None]).astype(jnp.int32), axis=1),
                          N_EXPERTS - 1)
    n_used = (pad_end[-1:] // MOE_BLOCK).astype(jnp.int32)
    expert_ids = jnp.arange(N_EXPERTS, dtype=jnp.int32)[:, None]

    def seg_start(e):
        return jnp.sum(jnp.where(expert_ids == e[None, :], pad_start[:, None], 0), axis=0)

    dest0 = seg_start(mi[0]) + mi[2]
    dest1 = seg_start(mi[1]) + mi[3]

    x_sorted = _dispatch(dest0, dest1, pad_start + counts, padded - counts, n_used, xpk, n_blocks)
    later = (expert_ids.T > block_e[:, None]) & (padded > 0)[None, :]
    next_e = jnp.min(jnp.where(later, expert_ids.T, N_EXPERTS), axis=1)
    next_e = jnp.where(next_e == N_EXPERTS, -1, next_e).astype(jnp.int32)
    y_sorted = _experts(block_e, n_used, next_e, x_sorted, w_gate[0], w_up[0], w_down[0])
    gfin = row_vec(norm_final)
    y_p, y_s = _combine(dest0, dest1, h, mf, gfin, y_sorted, n_prompt_rows=np_rows)

    ut = utail.reshape(m // CHUNK, SUBLANES, CONV_CH)
    tails = ut[:, SUBLANES - (CONV_W - 1):, :]
    p_last = (jnp.arange(bp) + 1) * (seq_p // CHUNK) - 1
    s_last = np_rows // CHUNK + (jnp.arange(bs) + 1) * (seq_s // CHUNK) - 1
    return (y_p.reshape(bp, seq_p, D_MODEL),
            y_s.reshape(bs, seq_s, D_MODEL),
            ckv_p.reshape(1, bp, seq_p, KV_LORA),
            jnp.swapaxes(kr_p, 1, 2)[None],
            tails[p_last][None],
            ckv_s.reshape(1, bs, seq_s, KV_LORA),
            jnp.swapaxes(kr_s, 1, 2)[None],
            tails[s_last][None])
```

```python
import functools

import jax
import jax.numpy as jnp
import numpy as np
from jax import lax
from jax.experimental import pallas as pl
from jax.experimental.pallas import tpu as pltpu

F32 = jnp.float32
BF16 = jnp.bfloat16

D_MODEL = 2048
N_HEADS = 8
QK_NOPE = 128
ROPE_DIM = 64
V_DIM = 128
Q_LORA = 512
KV_LORA = 512
ATTN_W = N_HEADS * V_DIM
CONV_CH = D_MODEL - ATTN_W
CONV_W = 3
CHUNK = 64
N_GROUPS = 4
EXPERTS_PER_GROUP = 8
N_EXPERTS = N_GROUPS * EXPERTS_PER_GROUP
D_FF = 512
ROPE_THETA = 10000.0
EPS = 1e-6
ATTN_SCALE = (QK_NOPE + ROPE_DIM) ** -0.5
EXP2_SCALE = ATTN_SCALE * 1.4426950408889634

LANES = 128
SUBLANES = 8
TM = 256
TD = 2304
MOE_BLOCK = 256
TQ = 512
TK = 512
TKH = 512
TKX = 256
W_O_SLABS = 4
W_IN_SLABS = 10
NEG_BIG = -1e30
V7X_VMEM_BYTES = 64 * 1024 * 1024
VMEM_LIMIT = V7X_VMEM_BYTES * 7 // 8


def _rms(v, g):
    return v * lax.rsqrt(jnp.mean(v * v, axis=-1, keepdims=True) + EPS) * g


def _lane_bcast(v, width):
    if width % LANES == 0:
        return jnp.concatenate([v] * (width // LANES), axis=1)
    assert width < LANES
    return v[:, :width]


def _pack_bf16_pairs(v):
    half = v.shape[-1] // 2
    lo = lax.bitcast_convert_type(v[..., :half].astype(BF16).astype(F32), jnp.uint32)
    hi = lax.bitcast_convert_type(v[..., half:].astype(BF16).astype(F32), jnp.uint32)
    return (lo >> 16) | (hi & jnp.uint32(0xFFFF0000))


def _unpack_bf16_pairs(w):
    return (lax.bitcast_convert_type(w << 16, F32),
            lax.bitcast_convert_type(w & jnp.uint32(0xFFFF0000), F32))


def _load_weight_as_bf16(w_hbm, dst_ref, stage_ref, sems):
    rows = stage_ref.shape[1]
    n_slabs = dst_ref.shape[0] // rows

    def slab_copy(c):
        return pltpu.make_async_copy(w_hbm.at[pl.ds(c * rows, rows)], stage_ref.at[c % 2], sems.at[c % 2])

    slab_copy(0).start()
    for c in range(n_slabs):
        if c + 1 < n_slabs:
            slab_copy(c + 1).start()
        slab_copy(c).wait()
        dst_ref[c * rows:(c + 1) * rows, :] = stage_ref[c % 2].astype(BF16)


def _const_spec(shape):
    nd = len(shape)
    return pl.BlockSpec(shape, lambda *_: (0,) * nd, pipeline_mode=pl.Buffered(1))


def _in_proj_kernel(xp_ref, xs_ref, gmix_ref, w_hbm, gq_ref, gkv_ref, gco_ref, convw_ref,
                    cos_ref, sin_ref, state_ref,
                    cqn_ref, ckvp_ref, krp_ref, ckvs_ref, krs_ref, convn_ref, utail_ref,
                    ext_ref, wt_ref, stage_ref, wsems,
                    *, n_prompt_tiles, tiles_per_seq, n_prompt_seq, sample_seq_len):
    i = pl.program_id(0)

    @pl.when(i == 0)
    def _():
        ext_ref[...] = jnp.zeros(ext_ref.shape, F32)
        _load_weight_as_bf16(w_hbm, wt_ref, stage_ref, wsems)

    def conv_block(u_sub, gate_sub, row0, length):
        ext_ref[SUBLANES:SUBLANES + length, :] = u_sub
        um1 = ext_ref[SUBLANES - 1:SUBLANES - 1 + length, :]
        um2 = ext_ref[SUBLANES - 2:SUBLANES - 2 + length, :]
        cw = convw_ref[...]
        conv = cw[0:1] * um2 + cw[1:2] * um1 + cw[2:3] * u_sub
        convn_ref[row0:row0 + length, :] = _rms(gate_sub * conv, gco_ref[...]).astype(BF16)

    def tile(x_ref, is_prompt):
        ckv_ref, krt_ref = (ckvp_ref, krp_ref) if is_prompt else (ckvs_ref, krs_ref)
        x = x_ref[...]
        xg = (x * gmix_ref[...]).astype(BF16)
        inv_rms = lax.rsqrt(jnp.mean(x * x, axis=-1, keepdims=True) + EPS)
        lat_w = Q_LORA + KV_LORA
        conv0 = lat_w + ROPE_DIM
        nt = (((1,), (1,)), ((), ()))

        def project(lo, hi):
            return inv_rms * lax.dot_general(xg, wt_ref[lo:hi, :], nt, preferred_element_type=F32)

        z_ch = project(conv0 + CONV_CH, conv0 + 3 * CONV_CH)
        u = z_ch[:, :CONV_CH] * z_ch[:, CONV_CH:]
        for j in range(TM // CHUNK):
            utail_ref[j] = u[CHUNK * (j + 1) - SUBLANES:CHUNK * (j + 1), :]
        gate_b = project(conv0, conv0 + CONV_CH)

        if is_prompt:
            first = (i % tiles_per_seq) == 0
            carried = ext_ref[TM + SUBLANES - 2:TM + SUBLANES, :]
            ext_ref[SUBLANES - 2:SUBLANES, :] = jnp.where(first, state_ref[i // tiles_per_seq], carried)
            conv_block(u, gate_b, 0, TM)
        else:
            n_sub = TM // sample_seq_len
            seq0 = n_prompt_seq + (i - n_prompt_tiles) * n_sub
            for k in range(n_sub):
                ext_ref[SUBLANES - 2:SUBLANES, :] = state_ref[seq0 + k]
                lo = k * sample_seq_len
                conv_block(u[lo:lo + sample_seq_len], gate_b[lo:lo + sample_seq_len], lo, sample_seq_len)

        zk = project(lat_w, conv0)
        zk_swapped = jnp.concatenate([zk[:, ROPE_DIM // 2:], zk[:, :ROPE_DIM // 2]], axis=1)
        k_rope = zk * cos_ref[...] + zk_swapped * sin_ref[...]
        if is_prompt:
            krt_ref[...] = k_rope.T
        else:
            for k in range(TM // sample_seq_len):
                krt_ref[k] = k_rope[k * sample_seq_len:(k + 1) * sample_seq_len, :].T
        ckv_ref[...] = _rms(project(Q_LORA, lat_w), gkv_ref[...])
        cqn_ref[...] = _rms(project(0, Q_LORA), gq_ref[...]).astype(BF16)

    @pl.when(i < n_prompt_tiles)
    def _():
        tile(xp_ref, True)

    @pl.when(i >= n_prompt_tiles)
    def _():
        tile(xs_ref, False)


def _in_proj(xp, xs, gmix, w_t, gq, gkv, gco, convw, cosk, sink, state, *, seq_p, seq_s):
    np_rows, ns_rows = xp.shape[0], xs.shape[0]
    m = np_rows + ns_rows
    npt, nst = np_rows // TM, ns_rows // TM
    tps = seq_p // TM
    n_prompt_seq = np_rows // seq_p
    last_p = npt - 1

    def tab_idx(i):
        return (jnp.where(i < npt, i % tps, tps), 0)

    row = lambda i: (i, 0)
    prow = lambda i: (jnp.minimum(i, last_p), 0)
    srow = lambda i: (jnp.maximum(i - npt, 0), 0)
    kern = functools.partial(_in_proj_kernel, n_prompt_tiles=npt, tiles_per_seq=tps,
                             n_prompt_seq=n_prompt_seq, sample_seq_len=seq_s)
    return pl.pallas_call(
        kern,
        grid=(npt + nst,),
        in_specs=[
            pl.BlockSpec((TM, D_MODEL), prow),
            pl.BlockSpec((TM, D_MODEL), srow),
            _const_spec((1, D_MODEL)),
            pl.BlockSpec(memory_space=pl.ANY),
            _const_spec((1, Q_LORA)),
            _const_spec((1, KV_LORA)),
            _const_spec((1, CONV_CH)),
            _const_spec((CONV_W, CONV_CH)),
            pl.BlockSpec((TM, ROPE_DIM), tab_idx),
            pl.BlockSpec((TM, ROPE_DIM), tab_idx),
            _const_spec(state.shape),
        ],
        out_specs=[
            pl.BlockSpec((TM, Q_LORA), row),
            pl.BlockSpec((TM, KV_LORA), prow),
            pl.BlockSpec((None, ROPE_DIM, TM), lambda i: (jnp.minimum(i, last_p) // tps, 0,
                                                          jnp.minimum(i, last_p) % tps)),
            pl.BlockSpec((TM, KV_LORA), srow),
            pl.BlockSpec((TM // seq_s, ROPE_DIM, seq_s), lambda i: (jnp.maximum(i - npt, 0), 0, 0)),
            pl.BlockSpec((TM, CONV_CH), row),
            pl.BlockSpec((TM // CHUNK, SUBLANES, CONV_CH), lambda i: (i, 0, 0)),
        ],
        out_shape=[
            jax.ShapeDtypeStruct((m, Q_LORA), BF16),
            jax.ShapeDtypeStruct((np_rows, KV_LORA), F32),
            jax.ShapeDtypeStruct((n_prompt_seq, ROPE_DIM, seq_p), F32),
            jax.ShapeDtypeStruct((ns_rows, KV_LORA), F32),
            jax.ShapeDtypeStruct((ns_rows // seq_s, ROPE_DIM, seq_s), F32),
            jax.ShapeDtypeStruct((m, CONV_CH), BF16),
            jax.ShapeDtypeStruct((m // CHUNK, SUBLANES, CONV_CH), F32),
        ],
        scratch_shapes=[pltpu.VMEM((TM + SUBLANES, CONV_CH), F32),
                        pltpu.VMEM(w_t.shape, BF16),
                        pltpu.VMEM((2, w_t.shape[0] // W_IN_SLABS, w_t.shape[1]), F32),
                        pltpu.SemaphoreType.DMA((2,))],
        compiler_params=pltpu.CompilerParams(dimension_semantics=("arbitrary",),
                                             vmem_limit_bytes=VMEM_LIMIT),
        name="in_proj",
    )(xp, xs, gmix, w_t, gq, gkv, gco, convw, cosk, sink, state)


def _attn_kernel(cqn_ref, wq_ref, wuk_ref, wuv_ref, cos_ref, sin_ref, gao_ref, pkv_ref, pkr_ref, kv_ref, kr_ref,
                 out_ref, qlat_ref, qr_ref, m_ref, l_ref, acc_ref, s_ref, *, tq, n_past):
    rows = N_HEADS * tq

    q = jnp.dot(cqn_ref[...], wq_ref[...], preferred_element_type=F32)
    nope_w = N_HEADS * QK_NOPE
    rope_w = N_HEADS * ROPE_DIM
    qrope = q[:, nope_w:nope_w + rope_w] * cos_ref[...] + q[:, nope_w + rope_w:] * sin_ref[...]
    for h in range(N_HEADS):
        qn = q[:, h * QK_NOPE:(h + 1) * QK_NOPE].astype(BF16)
        ql = jnp.dot(qn, wuk_ref[h], preferred_element_type=F32)
        qlat_ref[h * tq:(h + 1) * tq, :] = (ql * EXP2_SCALE).astype(BF16)
        qr_ref[h * tq:(h + 1) * tq, :] = (qrope[:, h * ROPE_DIM:(h + 1) * ROPE_DIM] * EXP2_SCALE).astype(BF16)


    nt = (((1,), (1,)), ((), ()))

    def scores(kc_f32, krt_f32):
        s = lax.dot_general(qlat_ref[...], kc_f32.astype(BF16), nt, preferred_element_type=F32)
        return s + jnp.dot(qr_ref[...], krt_f32.astype(BF16), preferred_element_type=F32)

    def update(s, kc_f32, mask, first=False):
        if mask is not None:
            s = jnp.where(mask, s, NEG_BIG)
        m_cur = jnp.max(s, axis=-1, keepdims=True)
        if first:
            m_new = jnp.broadcast_to(m_cur, m_ref.shape)
        else:
            m_prev = m_ref[...]
            m_new = jnp.maximum(m_prev, m_cur)
            alpha = jnp.exp2(m_prev - m_new)
        p = jnp.exp2(s - _lane_bcast(m_new, s.shape[1]))
        l_cur = jnp.sum(p, axis=-1, keepdims=True)
        pv = jnp.dot(p.astype(BF16), kc_f32.astype(BF16), preferred_element_type=F32)
        if first:
            l_ref[...] = jnp.broadcast_to(l_cur, l_ref.shape)
            acc_ref[...] = pv
        else:
            l_ref[...] = alpha * l_ref[...] + l_cur
            acc_ref[...] = _lane_bcast(alpha, KV_LORA) * acc_ref[...] + pv
        m_ref[...] = m_new

    def pipelined(kv, kr, lo, hi, last, mask_fn):
        def body(j, c):
            k0 = pl.multiple_of(j * TK, TK)
            k1 = pl.multiple_of(jnp.minimum(j + 1, last) * TK, TK)
            s_cur = s_ref[j % 2]
            s_ref[(j + 1) % 2] = scores(kv[pl.ds(k1, TK), :], kr[:, pl.ds(k1, TK)])
            update(s_cur, kv[pl.ds(k0, TK), :], None if mask_fn is None else mask_fn(k0))
            return c
        lax.fori_loop(lo, hi, body, 0)

    def pipelined_pairs(kv, kr, n_pairs, last):
        def body(i, c):
            ka = pl.multiple_of((2 * i + 1) * TK, TK)
            kb = pl.multiple_of((2 * i + 2) * TK, TK)
            kc = pl.multiple_of(jnp.minimum(2 * i + 3, last) * TK, TK)
            s_ref[0] = scores(kv[pl.ds(kb, TK), :], kr[:, pl.ds(kb, TK)])
            update(s_ref[1], kv[pl.ds(ka, TK), :], None)
            s_ref[1] = scores(kv[pl.ds(kc, TK), :], kr[:, pl.ds(kc, TK)])
            update(s_ref[0], kv[pl.ds(kb, TK), :], None)
            return c
        lax.fori_loop(0, n_pairs, body, 0)

    def first_block(kv, kr, last, mask):
        k1 = pl.multiple_of(jnp.minimum(1, last) * TK, TK)
        s_ref[0] = scores(kv[pl.ds(0, TK), :], kr[:, pl.ds(0, TK)])
        s_ref[1] = scores(kv[pl.ds(k1, TK), :], kr[:, pl.ds(k1, TK)])
        update(s_ref[0], kv[pl.ds(0, TK), :], mask, first=True)

    n_pb = n_past // TK
    first_block(pkv_ref, pkr_ref, n_pb - 1, None)
    n_pairs = (n_pb - 1) // 2
    pipelined_pairs(pkv_ref, pkr_ref, n_pairs, n_pb - 1)
    if 1 + 2 * n_pairs < n_pb:
        pipelined(pkv_ref, pkr_ref, 1 + 2 * n_pairs, n_pb, n_pb - 1, None)

    update(scores(kv_ref[...], kr_ref[...]), kv_ref[...], None)

    o = acc_ref[...] / _lane_bcast(l_ref[...], KV_LORA)
    parts = []
    for h in range(N_HEADS):
        oh = o[h * tq:(h + 1) * tq, :].astype(BF16)
        parts.append(jnp.dot(oh, wuv_ref[h], preferred_element_type=F32))
    attn = jnp.concatenate(parts, axis=-1)
    out_ref[...] = _rms(attn, gao_ref[...]).astype(BF16)


def _attention(cqn, w_q, w_ukt, w_uv, cosq, sinq, gao, ckv, krope, past_kv, past_kr, *, n_batch, seq, row0):
    tq = seq
    n_past = past_kv.shape[1]
    assert n_past % CHUNK == 0 and seq <= CHUNK and n_past % TK == 0
    blk0 = row0 // tq
    in_specs = [
        pl.BlockSpec((tq, Q_LORA), lambda b, q: (blk0 + b, 0)),
        _const_spec(w_q.shape),
        _const_spec(w_ukt.shape),
        _const_spec(w_uv.shape),
        pl.BlockSpec((tq, N_HEADS * ROPE_DIM), lambda b, q: (0, 0)),
        pl.BlockSpec((tq, N_HEADS * ROPE_DIM), lambda b, q: (0, 0)),
        _const_spec((1, ATTN_W)),
        pl.BlockSpec((None, n_past, KV_LORA), lambda b, q: (b, 0, 0)),
        pl.BlockSpec((None, ROPE_DIM, n_past), lambda b, q: (b, 0, 0)),
        pl.BlockSpec((seq, KV_LORA), lambda b, q: (b, 0)),
        pl.BlockSpec((None, ROPE_DIM, seq), lambda b, q: (b, 0, 0)),
    ]
    args = [cqn, w_q, w_ukt, w_uv, cosq, sinq, gao, past_kv, past_kr, ckv, krope]
    rows = N_HEADS * tq
    return pl.pallas_call(
        functools.partial(_attn_kernel, tq=tq, n_past=n_past),
        grid=(n_batch, 1),
        in_specs=in_specs,
        out_specs=pl.BlockSpec((tq, ATTN_W), lambda b, q: (b, 0)),
        out_shape=jax.ShapeDtypeStruct((n_batch * seq, ATTN_W), BF16),
        scratch_shapes=[
            pltpu.VMEM((rows, KV_LORA), BF16),
            pltpu.VMEM((rows, ROPE_DIM), BF16),
            pltpu.VMEM((rows, LANES), F32),
            pltpu.VMEM((rows, LANES), F32),
            pltpu.VMEM((rows, KV_LORA), F32),
            pltpu.VMEM((2, rows, TK), F32),
        ],
        compiler_params=pltpu.CompilerParams(dimension_semantics=("arbitrary", "arbitrary"),
                                             vmem_limit_bytes=VMEM_LIMIT),
        name="attn_sample",
    )(*args)


def _attn_heads_kernel(cqn_ref, wq_ref, wkv_ref, cos_ref, sin_ref, gao_ref, kv_ref, krt_ref, out_ref,
                       kcat_ref, vh_ref, qcat_ref, m_ref, l_ref, acc_ref, klim_ref, *, tq, seq):
    qi = pl.program_id(1)
    nt = (((1,), (1,)), ((), ()))
    kw = QK_NOPE + ROPE_DIM
    kpad = kcat_ref.shape[-1]

    @pl.when(qi == 0)
    def _():
        def expand(j, c):
            k0 = pl.multiple_of(j * TKX, TKX)
            latent = kv_ref[pl.ds(k0, TKX), :].astype(BF16)
            kvh = jnp.dot(latent, wkv_ref[...], preferred_element_type=F32)
            k_rope = krt_ref[:, pl.ds(k0, TKX)].T.astype(BF16)
            for h in range(N_HEADS):
                kcat_ref[h, pl.ds(k0, TKX), :QK_NOPE] = kvh[:, h * QK_NOPE:(h + 1) * QK_NOPE].astype(BF16)
                kcat_ref[h, pl.ds(k0, TKX), QK_NOPE:kw] = k_rope
                kcat_ref[h, pl.ds(k0, TKX), kw:] = jnp.zeros((TKX, kpad - kw), BF16)
                v0 = N_HEADS * QK_NOPE + h * V_DIM
                vh_ref[h, pl.ds(k0, TKX), :] = kvh[:, v0:v0 + V_DIM].astype(BF16)
            return c
        lax.fori_loop(0, seq // TKX, expand, 0)

    q = jnp.dot(cqn_ref[...], wq_ref[...], preferred_element_type=F32)
    nope_w = N_HEADS * QK_NOPE
    rope_w = N_HEADS * ROPE_DIM
    qrope = q[:, nope_w:nope_w + rope_w] * cos_ref[...] + q[:, nope_w + rope_w:] * sin_ref[...]
    for h in range(N_HEADS):
        qcat_ref[h, :, :QK_NOPE] = (q[:, h * QK_NOPE:(h + 1) * QK_NOPE] * EXP2_SCALE).astype(BF16)
        qcat_ref[h, :, QK_NOPE:kw] = (qrope[:, h * ROPE_DIM:(h + 1) * ROPE_DIM] * EXP2_SCALE).astype(BF16)
        qcat_ref[h, :, kw:] = jnp.zeros((tq, kpad - kw), BF16)

    r = lax.broadcasted_iota(jnp.int32, (tq, LANES), 0)
    klim_ref[...] = ((qi * tq + r) & ~(CHUNK - 1)) + CHUNK

    tk = TKH

    def block(k0, masked, first):
        if masked:
            cidx = lax.broadcasted_iota(jnp.int32, (tq, tk), 1)
            mask = cidx < _lane_bcast(klim_ref[...] - k0, tk)
        for h in range(N_HEADS):
            s = lax.dot_general(qcat_ref[h], kcat_ref[h, pl.ds(k0, tk), :], nt, preferred_element_type=F32)
            if masked:
                s = jnp.where(mask, s, NEG_BIG)
            m_cur = jnp.max(s, axis=-1, keepdims=True)
            if first:
                m_new = jnp.broadcast_to(m_cur, (tq, LANES))
            else:
                m_prev = m_ref[h]
                m_new = jnp.maximum(m_prev, m_cur)
                alpha = jnp.exp2(m_prev - m_new)
            p = jnp.exp2(s - _lane_bcast(m_new, tk))
            l_cur = jnp.sum(p, axis=-1, keepdims=True)
            pv = jnp.dot(p.astype(BF16), vh_ref[h, pl.ds(k0, tk), :], preferred_element_type=F32)
            if first:
                l_ref[h] = jnp.broadcast_to(l_cur, (tq, LANES))
                acc_ref[h] = pv
            else:
                l_ref[h] = alpha * l_ref[h] + l_cur
                acc_ref[h] = _lane_bcast(alpha, V_DIM) * acc_ref[h] + pv
            m_ref[h] = m_new

    n_blocks = ((qi + 1) * tq + tk - 1) // tk
    n_full = jnp.minimum((qi * tq // CHUNK + 1) * CHUNK // tk, n_blocks)

    def loop(lo, hi, masked):
        def body(j, c):
            block(pl.multiple_of(j * tk, tk), masked, False)
            return c
        lax.fori_loop(lo, hi, body, 0)

    block(0, True, True)
    loop(1, n_full, False)
    loop(jnp.maximum(n_full, 1), n_blocks, True)

    attn = jnp.concatenate([acc_ref[h] / _lane_bcast(l_ref[h], V_DIM) for h in range(N_HEADS)], axis=-1)
    out_ref[...] = _rms(attn, gao_ref[...]).astype(BF16)


def _attention_heads(cqn, w_q, w_kv, cosq, sinq, gao, ckv, krope_t, *, n_batch, seq):
    nq = seq // TQ
    kpad = 2 * LANES
    assert QK_NOPE + ROPE_DIM <= kpad and V_DIM == LANES
    assert seq % TQ == 0 and seq % TKH == 0 and seq % TKX == 0 and TQ % CHUNK == 0 and TKH % CHUNK == 0
    return pl.pallas_call(
        functools.partial(_attn_heads_kernel, tq=TQ, seq=seq),
        grid=(n_batch, nq),
        in_specs=[
            pl.BlockSpec((TQ, Q_LORA), lambda b, q: (b * nq + q, 0)),
            _const_spec(w_q.shape),
            _const_spec(w_kv.shape),
            pl.BlockSpec((TQ, N_HEADS * ROPE_DIM), lambda b, q: (q, 0)),
            pl.BlockSpec((TQ, N_HEADS * ROPE_DIM), lambda b, q: (q, 0)),
            _const_spec((1, ATTN_W)),
            pl.BlockSpec((seq, KV_LORA), lambda b, q: (b, 0)),
            pl.BlockSpec((None, ROPE_DIM, seq), lambda b, q: (b, 0, 0)),
        ],
        out_specs=pl.BlockSpec((TQ, ATTN_W), lambda b, q: (b * nq + q, 0)),
        out_shape=jax.ShapeDtypeStruct((n_batch * seq, ATTN_W), BF16),
        scratch_shapes=[
            pltpu.VMEM((N_HEADS, seq, kpad), BF16),
            pltpu.VMEM((N_HEADS, seq, V_DIM), BF16),
            pltpu.VMEM((N_HEADS, TQ, kpad), BF16),
            pltpu.VMEM((N_HEADS, TQ, LANES), F32),
            pltpu.VMEM((N_HEADS, TQ, LANES), F32),
            pltpu.VMEM((N_HEADS, TQ, V_DIM), F32),
            pltpu.VMEM((TQ, LANES), jnp.int32),
        ],
        compiler_params=pltpu.CompilerParams(dimension_semantics=("arbitrary", "arbitrary"),
                                             vmem_limit_bytes=VMEM_LIMIT),
        name="attn_prompt",
    )(cqn, w_q, w_kv, cosq, sinq, gao, ckv, krope_t)


def _out_proj_kernel(attnp_ref, attns_ref, convn_ref, xp_ref, xs_ref, wo_hbm, gffn_ref, wr_ref,
                     br_ref, h_ref, xpk_ref, mi_ref, mf_ref, cnt_ref, carry_ref, logit_ref,
                     wo_ref, stage_ref, wsems, *, n_prompt_tiles):
    i = pl.program_id(0)

    @pl.when(i == 0)
    def _():
        carry_ref[...] = jnp.zeros(carry_ref.shape, F32)
        logit_ref[...] = jnp.zeros(logit_ref.shape, F32)
        _load_weight_as_bf16(wo_hbm, wo_ref, stage_ref, wsems)

    def tile(x_ref, attn_ref):
        prev_logits = logit_ref[...]
        y = jnp.dot(attn_ref[...], wo_ref[:ATTN_W, :], preferred_element_type=F32)
        y = y + jnp.dot(convn_ref[...], wo_ref[ATTN_W:, :], preferred_element_type=F32)
        h = x_ref[...] + y
        h_ref[...] = h
        xn = _rms(h, gffn_ref[...])

        half = D_MODEL // 2
        xh = xn.astype(BF16)
        xh32 = xh.astype(F32)
        lo = lax.bitcast_convert_type(xh32[:, :half], jnp.uint32)
        hi = lax.bitcast_convert_type(xh32[:, half:], jnp.uint32)
        xpk_ref[...] = (lo >> 16) | (hi & jnp.uint32(0xFFFF0000))

        xl = (xn - xh32).astype(BF16)
        hh_hl = jnp.dot(xh, wr_ref[...], preferred_element_type=F32)
        lh = jnp.dot(xl, wr_ref[:, :LANES], preferred_element_type=F32)
        logit_ref[...] = hh_hl[:, :LANES] + (lh + hh_hl[:, LANES:]) + br_ref[...]

        logits = prev_logits
        counted = (i > 0).astype(F32)
        lane = lax.broadcasted_iota(jnp.int32, (TM, LANES), 1).astype(F32)
        ninf = -jnp.inf
        far = float(LANES)

        def first_argmax(v):
            vmax = jnp.max(v, axis=-1, keepdims=True)
            return vmax, jnp.min(jnp.where(v == vmax, lane, far), axis=-1, keepdims=True)

        gl = jnp.where(lane < N_GROUPS, logits, ninf)
        gmax, gidx = first_argmax(gl)
        g_p = 1.0 / jnp.sum(jnp.exp(gl - gmax), axis=-1, keepdims=True)
        e_lo = N_GROUPS + EXPERTS_PER_GROUP * gidx
        el = jnp.where((lane >= e_lo) & (lane < e_lo + EXPERTS_PER_GROUP), logits, ninf)
        e1max, i1 = first_argmax(el)
        z = jnp.sum(jnp.exp(el - e1max), axis=-1, keepdims=True)
        el2 = jnp.where(lane == i1, ninf, el)
        e2max, i2 = first_argmax(el2)
        p1 = 1.0 / z
        p2 = jnp.exp(e2max - e1max) / z
        den = p1 + p2
        g0 = g_p * p1 / den
        g1 = g_p * p2 / den
        e0 = i1 - N_GROUPS
        e1 = i2 - N_GROUPS

        oh0 = lane == e0
        oh1 = lane == e1
        oh = jnp.where(oh0 | oh1, 1.0, 0.0)
        r = lax.broadcasted_iota(jnp.int32, (TM, TM), 0)
        c = lax.broadcasted_iota(jnp.int32, (TM, TM), 1)
        ltri = jnp.where(r > c, 1.0, 0.0).astype(BF16)
        before = jnp.dot(ltri, oh.astype(BF16), preferred_element_type=F32) + carry_ref[...]
        rank0 = jnp.sum(jnp.where(oh0, before, 0.0), axis=-1, keepdims=True)
        rank1 = jnp.sum(jnp.where(oh1, before, 0.0), axis=-1, keepdims=True)
        total = carry_ref[...] + counted * jnp.sum(oh, axis=0, keepdims=True)
        carry_ref[...] = total
        cnt_ref[...] = jnp.broadcast_to(total, cnt_ref.shape)

        mi = jnp.where(lane == 0, e0, jnp.where(lane == 1, e1, jnp.where(lane == 2, rank0, rank1)))
        mi_ref[...] = jnp.transpose(mi)[:SUBLANES, :].astype(jnp.int32)
        mf_ref[...] = jnp.where(lane == 0, g0, g1)

    @pl.when(i < n_prompt_tiles)
    def _():
        tile(xp_ref, attnp_ref)

    @pl.when(i >= n_prompt_tiles)
    def _():
        tile(xs_ref, attns_ref)


def _out_proj(attn_p, attn_s, conv_n, xp, xs, w_ob, gffn, w_r2, b_r):
    m = conv_n.shape[0]
    npt = xp.shape[0] // TM
    n_tiles = m // TM
    last_p, last_s, last = npt - 1, n_tiles - npt - 1, n_tiles - 1
    row = lambda i: (jnp.minimum(i, last), 0)
    prow = lambda i: (jnp.minimum(i, last_p), 0)
    srow = lambda i: (jnp.clip(i - npt, 0, last_s), 0)
    lag = lambda i: jnp.maximum(i - 1, 0)
    return pl.pallas_call(
        functools.partial(_out_proj_kernel, n_prompt_tiles=npt),
        grid=(n_tiles + 1,),
        in_specs=[
            pl.BlockSpec((TM, ATTN_W), prow),
            pl.BlockSpec((TM, ATTN_W), srow),
            pl.BlockSpec((TM, CONV_CH), row),
            pl.BlockSpec((TM, D_MODEL), prow),
            pl.BlockSpec((TM, D_MODEL), srow),
            pl.BlockSpec(memory_space=pl.ANY),
            _const_spec((1, D_MODEL)),
            _const_spec(w_r2.shape),
            _const_spec((1, LANES)),
        ],
        out_specs=[
            pl.BlockSpec((TM, D_MODEL), row),
            pl.BlockSpec((TM, D_MODEL // 2), row),
            pl.BlockSpec((SUBLANES, TM), lambda i: (0, lag(i))),
            pl.BlockSpec((TM, LANES), lambda i: (lag(i), 0)),
            pl.BlockSpec((SUBLANES, LANES), lambda i: (0, 0)),
        ],
        out_shape=[
            jax.ShapeDtypeStruct((m, D_MODEL), F32),
            jax.ShapeDtypeStruct((m, D_MODEL // 2), jnp.uint32),
            jax.ShapeDtypeStruct((SUBLANES, m), jnp.int32),
            jax.ShapeDtypeStruct((m, LANES), F32),
            jax.ShapeDtypeStruct((SUBLANES, LANES), F32),
        ],
        scratch_shapes=[pltpu.VMEM((1, LANES), F32), pltpu.VMEM((TM, LANES), F32),
                        pltpu.VMEM(w_ob.shape, BF16),
                        pltpu.VMEM((2, w_ob.shape[0] // W_O_SLABS, w_ob.shape[1]), F32),
                        pltpu.SemaphoreType.DMA((2,))],
        compiler_params=pltpu.CompilerParams(dimension_semantics=("arbitrary",),
                                             vmem_limit_bytes=VMEM_LIMIT),
        name="out_proj",
    )(attn_p, attn_s, conv_n, xp, xs, w_ob, gffn, w_r2, b_r)


def _dispatch_kernel(d0_ref, d1_ref, zlo_ref, zn_ref, nu_ref, xpk_ref, xs_hbm, zeros_ref, sems, *, n_blocks):
    i = pl.program_id(0)
    sem = sems.at[0]
    zsem = sems.at[1]

    def zero_fill(act):
        def per_expert(e, c):
            lo = zlo_ref[e]
            n = zn_ref[e]
            head = (-lo) & (SUBLANES - 1)
            for r in range(SUBLANES - 1):
                @pl.when(r < head)
                def _(r=r):
                    act(pltpu.make_async_copy(zeros_ref.at[pl.ds(0, 1)], xs_hbm.at[pl.ds(lo + r, 1)], zsem))
            off = lo + head
            rest = n - head
            size = MOE_BLOCK // 2
            while size >= SUBLANES:
                @pl.when((rest & size) != 0)
                def _(off=off, size=size):
                    dst = xs_hbm.at[pl.ds(pl.multiple_of(off, SUBLANES), size)]
                    act(pltpu.make_async_copy(zeros_ref.at[pl.ds(0, size)], dst, zsem))
                off = off + (rest & size)
                size //= 2
            return c

        def per_block(b, c):
            dst = xs_hbm.at[pl.ds(pl.multiple_of(b * MOE_BLOCK, MOE_BLOCK), MOE_BLOCK)]
            act(pltpu.make_async_copy(zeros_ref, dst, zsem))
            return c

        lax.fori_loop(0, N_EXPERTS, per_expert, 0)
        lax.fori_loop(nu_ref[0], n_blocks, per_block, 0)

    @pl.when(i == 0)
    def _():
        zeros_ref[...] = jnp.zeros(zeros_ref.shape, zeros_ref.dtype)
        zero_fill(lambda cp: cp.start())

    @pl.when(i == pl.num_programs(0) - 1)
    def _():
        zero_fill(lambda cp: cp.wait())

    base = i * TD

    def start(g, c):
        for u in range(SUBLANES):
            r = base + g * SUBLANES + u
            src = xpk_ref.at[g, pl.ds(u, 1)]
            pltpu.make_async_copy(src, xs_hbm.at[pl.ds(d0_ref[r], 1)], sem).start()
            pltpu.make_async_copy(src, xs_hbm.at[pl.ds(d1_ref[r], 1)], sem).start()
        return c

    lax.fori_loop(0, TD // SUBLANES, start, 0)
    for _ in range(2):
        pltpu.make_async_copy(xs_hbm.at[pl.ds(0, TD)], xs_hbm.at[pl.ds(0, TD)], sem).wait()


def _dispatch(dest0, dest1, pad_lo, n_pad, n_used, xpk, n_blocks):
    m = xpk.shape[0]
    grid_spec = pltpu.PrefetchScalarGridSpec(
        num_scalar_prefetch=5,
        grid=(m // TD,),
        in_specs=[pl.BlockSpec((TD // SUBLANES, SUBLANES, D_MODEL // 2), lambda i, *_: (i, 0, 0))],
        out_specs=pl.BlockSpec(memory_space=pl.ANY),
        scratch_shapes=[pltpu.VMEM((MOE_BLOCK, D_MODEL // 2), jnp.uint32),
                        pltpu.SemaphoreType.DMA((2,))],
    )
    return pl.pallas_call(
        functools.partial(_dispatch_kernel, n_blocks=n_blocks),
        grid_spec=grid_spec,
        out_shape=jax.ShapeDtypeStruct((n_blocks * MOE_BLOCK, D_MODEL // 2), jnp.uint32),
        compiler_params=pltpu.CompilerParams(dimension_semantics=("arbitrary",)),
        name="dispatch",
    )(dest0, dest1, pad_lo, n_pad, n_used, xpk.reshape(m // SUBLANES, SUBLANES, D_MODEL // 2))


def _experts_kernel(be_ref, nu_ref, nxt_ref, x_ref, wg_hbm, wu_hbm, wd_hbm, y_ref,
                    sg_ref, su_ref, sd_ref, wgb_ref, wub_ref, wdb_ref, sems):
    b = pl.program_id(0)
    active = b < nu_ref[0]
    new_expert = jnp.logical_or(b == 0, be_ref[b] != be_ref[jnp.maximum(b - 1, 0)])

    def weight_copies(e):
        return (pltpu.make_async_copy(wg_hbm.at[e], sg_ref, sems.at[0]),
                pltpu.make_async_copy(wu_hbm.at[e], su_ref, sems.at[1]),
                pltpu.make_async_copy(wd_hbm.at[e], sd_ref, sems.at[2]))

    @pl.when(b == 0)
    def _():
        for cp in weight_copies(be_ref[0]):
            cp.start()

    @pl.when(jnp.logical_and(active, new_expert))
    def _():
        for cp in weight_copies(be_ref[b]):
            cp.wait()
        wgb_ref[...] = sg_ref[...].astype(BF16)
        wub_ref[...] = su_ref[...].astype(BF16)
        wdb_ref[...] = sd_ref[...].astype(BF16)

        @pl.when(nxt_ref[b] >= 0)
        def _():
            for cp in weight_copies(nxt_ref[b]):
                cp.start()

    @pl.when(active)
    def _():
        half = D_MODEL // 2
        xa, xb = (v.astype(BF16) for v in _unpack_bf16_pairs(x_ref[...]))
        g = jnp.dot(xa, wgb_ref[:half, :], preferred_element_type=F32)
        g = g + jnp.dot(xb, wgb_ref[half:, :], preferred_element_type=F32)
        u = jnp.dot(xa, wub_ref[:half, :], preferred_element_type=F32)
        u = u + jnp.dot(xb, wub_ref[half:, :], preferred_element_type=F32)
        hmid = (g * jax.nn.sigmoid(g)) * u
        y = jnp.dot(hmid.astype(BF16), wdb_ref[...], preferred_element_type=F32)
        y_ref[...] = _pack_bf16_pairs(y)

    @pl.when(b >= nu_ref[0])
    def _():
        y_ref[...] = jnp.zeros(y_ref.shape, y_ref.dtype)


def _experts(block_e, n_used, next_e, x_sorted, w_gate, w_up, w_down):
    p = x_sorted.shape[0]
    nb = p // MOE_BLOCK

    def xrow(b, be, nu, nxt):
        return (jnp.maximum(jnp.minimum(b, nu[0] - 1), 0), 0)

    grid_spec = pltpu.PrefetchScalarGridSpec(
        num_scalar_prefetch=3,
        grid=(nb,),
        in_specs=[
            pl.BlockSpec((MOE_BLOCK, D_MODEL // 2), xrow),
            pl.BlockSpec(memory_space=pl.ANY),
            pl.BlockSpec(memory_space=pl.ANY),
            pl.BlockSpec(memory_space=pl.ANY),
        ],
        out_specs=pl.BlockSpec((MOE_BLOCK, D_MODEL // 2), lambda b, be, nu, nxt: (b, 0)),
        scratch_shapes=[pltpu.VMEM((D_MODEL, D_FF), F32), pltpu.VMEM((D_MODEL, D_FF), F32),
                        pltpu.VMEM((D_FF, D_MODEL), F32),
                        pltpu.VMEM((D_MODEL, D_FF), BF16), pltpu.VMEM((D_MODEL, D_FF), BF16),
                        pltpu.VMEM((D_FF, D_MODEL), BF16),
                        pltpu.SemaphoreType.DMA((3,))],
    )
    return pl.pallas_call(
        _experts_kernel,
        grid_spec=grid_spec,
        out_shape=jax.ShapeDtypeStruct((p, D_MODEL // 2), jnp.uint32),
        compiler_params=pltpu.CompilerParams(dimension_semantics=("arbitrary",),
                                             vmem_limit_bytes=VMEM_LIMIT),
        name="experts",
    )(block_e, n_used, next_e, x_sorted, w_gate, w_up, w_down)


def _combine_kernel(d0_ref, d1_ref, h_ref, mf_ref, gfin_ref, y_hbm, outp_ref, outs_ref, y0_ref, y1_ref, sems,
                    *, n_tiles, n_prompt_tiles):
    i = pl.program_id(0)

    def gather(tile, slot, act):
        base = tile * TM

        def body(g, c):
            for u in range(SUBLANES):
                r = base + g * SUBLANES + u
                act(pltpu.make_async_copy(y_hbm.at[pl.ds(d0_ref[r], 1)], y0_ref.at[slot, g, pl.ds(u, 1)],
                                          sems.at[slot]))
                act(pltpu.make_async_copy(y_hbm.at[pl.ds(d1_ref[r], 1)], y1_ref.at[slot, g, pl.ds(u, 1)],
                                          sems.at[slot]))
            return c
        lax.fori_loop(0, TM // SUBLANES, body, 0)

    @pl.when(i == 0)
    def _():
        gather(0, 0, lambda cp: cp.start())

    @pl.when(i + 1 < n_tiles)
    def _():
        gather(i + 1, (i + 1) % 2, lambda cp: cp.start())

    slot = i % 2
    for _ in range(2):
        pltpu.make_async_copy(y_hbm.at[pl.ds(0, TM)], y_hbm.at[pl.ds(0, TM)], sems.at[slot]).wait()

    def finish(out_ref):
        mf = mf_ref[...]
        g0, g1 = mf[:, :, 0:1], mf[:, :, 1:2]
        half = D_MODEL // 2
        a0, b0 = _unpack_bf16_pairs(y0_ref[slot])
        a1, b1 = _unpack_bf16_pairs(y1_ref[slot])
        o_lo = h_ref[:, :, :half] + (g0 * a0 + g1 * a1)
        o_hi = h_ref[:, :, half:] + (g0 * b0 + g1 * b1)
        sumsq = jnp.sum(o_lo * o_lo, axis=-1, keepdims=True) + jnp.sum(o_hi * o_hi, axis=-1, keepdims=True)
        inv_rms = lax.rsqrt(sumsq / D_MODEL + EPS)
        out_ref[:, :, :half] = o_lo * inv_rms * gfin_ref[:, :, :half]
        out_ref[:, :, half:] = o_hi * inv_rms * gfin_ref[:, :, half:]

    @pl.when(i < n_prompt_tiles)
    def _():
        finish(outp_ref)

    @pl.when(i >= n_prompt_tiles)
    def _():
        finish(outs_ref)


def _combine(dest0, dest1, h, mf, gfin, y_sorted, *, n_prompt_rows):
    m = h.shape[0]
    npt = n_prompt_rows // TM
    tg = TM // SUBLANES
    grouped = lambda a: a.reshape(a.shape[0] // SUBLANES, SUBLANES, a.shape[1])
    grid_spec = pltpu.PrefetchScalarGridSpec(
        num_scalar_prefetch=2,
        grid=(m // TM,),
        in_specs=[
            pl.BlockSpec((tg, SUBLANES, D_MODEL), lambda i, *_: (i, 0, 0)),
            pl.BlockSpec((tg, SUBLANES, LANES), lambda i, *_: (i, 0, 0)),
            pl.BlockSpec((1, 1, D_MODEL), lambda i, *_: (0, 0, 0)),
            pl.BlockSpec(memory_space=pl.ANY),
        ],
        out_specs=[pl.BlockSpec((tg, SUBLANES, D_MODEL), lambda i, *_: (jnp.minimum(i, npt - 1), 0, 0)),
                   pl.BlockSpec((tg, SUBLANES, D_MODEL), lambda i, *_: (jnp.maximum(i - npt, 0), 0, 0))],
        scratch_shapes=[pltpu.VMEM((2, tg, SUBLANES, D_MODEL // 2), jnp.uint32),
                        pltpu.VMEM((2, tg, SUBLANES, D_MODEL // 2), jnp.uint32),
                        pltpu.SemaphoreType.DMA((2,))],
    )
    y_p, y_s = pl.pallas_call(
        functools.partial(_combine_kernel, n_tiles=m // TM, n_prompt_tiles=npt),
        grid_spec=grid_spec,
        out_shape=[jax.ShapeDtypeStruct((n_prompt_rows // SUBLANES, SUBLANES, D_MODEL), F32),
                   jax.ShapeDtypeStruct(((m - n_prompt_rows) // SUBLANES, SUBLANES, D_MODEL), F32)],
        compiler_params=pltpu.CompilerParams(dimension_semantics=("arbitrary",),
                                             vmem_limit_bytes=VMEM_LIMIT),
        name="combine",
    )(dest0, dest1, grouped(h), grouped(mf), gfin.reshape(1, 1, D_MODEL), y_sorted)
    return y_p.reshape(n_prompt_rows, D_MODEL), y_s.reshape(m - n_prompt_rows, D_MODEL)


def _rope_tables(pos):
    f32 = np.float32
    inv = np.power(f32(ROPE_THETA), -np.arange(0, ROPE_DIM, 2, dtype=f32) / f32(ROPE_DIM)).astype(f32)
    ang = (pos.astype(f32)[:, None] * inv[None, :]).astype(f32)
    cos, sin = np.cos(ang).astype(f32), np.sin(ang).astype(f32)
    return np.concatenate([cos, cos], axis=-1), np.concatenate([-sin, sin], axis=-1)


def _swap_halves(w):
    return jnp.concatenate([w[..., ROPE_DIM // 2:], w[..., :ROPE_DIM // 2]], axis=-1)


def kernel(x_prompt, x_sample, cache_kv_latent, cache_k_rope, state_conv, norm_mix, w_in, norm_q, w_uq,
           norm_kv, w_uk, w_uv, conv_w, norm_attn_out, norm_conv_out, w_o, norm_ffn, w_router_group,
           b_router_group, w_router_expert, b_router_expert, w_gate, w_up, w_down, norm_final):
    assert w_in.shape[0] == 1, "single-layer trunk"
    bp, seq_p, _ = x_prompt.shape
    bs, seq_s, _ = x_sample.shape
    past_len = cache_kv_latent.shape[2]
    np_rows, ns_rows = bp * seq_p, bs * seq_s
    m = np_rows + ns_rows
    assert seq_p % TM == 0 and TM % seq_s == 0 and ns_rows % TM == 0 and seq_s == CHUNK
    assert m % TD == 0

    xp = x_prompt.reshape(np_rows, D_MODEL)
    xs = x_sample.reshape(ns_rows, D_MODEL)
    row_vec = lambda v: v.reshape(1, -1)

    assert w_in.shape[2] == Q_LORA + KV_LORA + ROPE_DIM + 3 * CONV_CH
    w_t = jnp.swapaxes(w_in[0], 0, 1)
    assert w_t.shape[0] % (W_IN_SLABS * 2 * SUBLANES) == 0
    wq4 = w_uq[0].reshape(Q_LORA, N_HEADS, QK_NOPE + ROPE_DIM)
    wq_rope = wq4[:, :, QK_NOPE:]
    w_q = jnp.concatenate([wq4[:, :, :QK_NOPE].reshape(Q_LORA, -1), wq_rope.reshape(Q_LORA, -1),
                           _swap_halves(wq_rope).reshape(Q_LORA, -1)], axis=1).astype(BF16)
    w_ukt = jnp.transpose(w_uk[0], (1, 2, 0)).astype(BF16)
    w_uvh = jnp.transpose(w_uv[0], (1, 0, 2)).astype(BF16)
    w_ob = w_o[0]
    n_router = N_GROUPS + N_EXPERTS
    w_r = jnp.concatenate([w_router_group[0], w_router_expert[0].reshape(D_MODEL, N_EXPERTS)], axis=1)
    w_r = jnp.pad(w_r, ((0, 0), (0, LANES - n_router)))
    w_rh = w_r.astype(BF16)
    w_rl = (w_r - w_rh.astype(F32)).astype(BF16)
    w_r2 = jnp.concatenate([w_rh, w_rl], axis=1)
    b_r =jnp.pad(jnp.concatenate([b_router_group[0], b_router_expert[0].reshape(N_EXPERTS)]),
                  (0, LANES - n_router)).reshape(1, LANES)

    cos_p, sin_p = _rope_tables(np.arange(seq_p))
    cos_s, sin_s = _rope_tables(past_len + np.arange(seq_s))
    cosk = np.concatenate([cos_p, np.tile(cos_s, (TM // seq_s, 1))], axis=0)
    sink = np.concatenate([sin_p, np.tile(sin_s, (TM // seq_s, 1))], axis=0)
    state = jnp.concatenate([jnp.zeros((bp, CONV_W - 1, CONV_CH), F32), state_conv[0]], axis=0)

    cqn, ckv_p, kr_p, ckv_s, kr_s, conv_n, utail = _in_proj(
        xp, xs, row_vec(norm_mix[0]), w_t, row_vec(norm_q[0]), row_vec(norm_kv[0]),
        row_vec(norm_conv_out[0]), conv_w[0], cosk, sink, state, seq_p=seq_p, seq_s=seq_s)

    gao = row_vec(norm_attn_out[0])
    w_kv = jnp.concatenate([w_uk[0].reshape(KV_LORA, N_HEADS * QK_NOPE),
                            w_uv[0].reshape(KV_LORA, N_HEADS * V_DIM)], axis=1).astype(BF16)
    attn_p = _attention_heads(cqn, w_q, w_kv, np.tile(cos_p, (1, N_HEADS)), np.tile(sin_p, (1, N_HEADS)),
                              gao, ckv_p, kr_p, n_batch=bp, seq=seq_p)
    attn_s = _attention(cqn, w_q, w_ukt, w_uvh, np.tile(cos_s, (1, N_HEADS)), np.tile(sin_s, (1, N_HEADS)),
                        gao, ckv_s, kr_s, cache_kv_latent[0], jnp.swapaxes(cache_k_rope[0], 1, 2),
                        n_batch=bs, seq=seq_s, row0=np_rows)

    h, xpk, mi, mf, cnt = _out_proj(attn_p, attn_s, conv_n, xp, xs, w_ob, row_vec(norm_ffn[0]),
                                    w_r2, b_r)

    counts = cnt[0, :N_EXPERTS].astype(jnp.int32)
    padded = (counts + MOE_BLOCK - 1) // MOE_BLOCK * MOE_BLOCK
    pad_end = jnp.cumsum(padded)
    pad_start = pad_end - padded
    n_blocks = -(-(m * 2) // MOE_BLOCK) + N_EXPERTS
    block_row0 = jnp.arange(n_blocks, dtype=jnp.int32) * MOE_BLOCK
    block_e = jnp.minimum(jnp.sum((pad_end[None, :] <= block_row0[:, None]).astype(jnp.int32), axis=1),
                          N_EXPERTS - 1)
    n_used = (pad_end[-1:] // MOE_BLOCK).astype(jnp.int32)
    expert_ids = jnp.arange(N_EXPERTS, dtype=jnp.int32)[:, None]

    def seg_start(e):
        return jnp.sum(jnp.where(expert_ids == e[None, :], pad_start[:, None], 0), axis=0)

    dest0 = seg_start(mi[0]) + mi[2]
    dest1 = seg_start(mi[1]) + mi[3]

    x_sorted = _dispatch(dest0, dest1, pad_start + counts, padded - counts, n_used, xpk, n_blocks)
    later = (expert_ids.T > block_e[:, None]) & (padded > 0)[None, :]
    next_e = jnp.min(jnp.where(later, expert_ids.T, N_EXPERTS), axis=1)
    next_e = jnp.where(next_e == N_EXPERTS, -1, next_e).astype(jnp.int32)
    y_sorted = _experts(block_e, n_used, next_e, x_sorted, w_gate[0], w_up[0], w_down[0])
    gfin = row_vec(norm_final)
    y_p, y_s = _combine(dest0, dest1, h, mf, gfin, y_sorted, n_prompt_rows=np_rows)

    ut = utail.reshape(m // CHUNK, SUBLANES, CONV_CH)
    tails = ut[:, SUBLANES - (CONV_W - 1):, :]
    p_last = (jnp.arange(bp) + 1) * (seq_p // CHUNK) - 1
    s_last = np_rows // CHUNK + (jnp.arange(bs) + 1) * (seq_s // CHUNK) - 1
    return (y_p.reshape(bp, seq_p, D_MODEL),
            y_s.reshape(bs, seq_s, D_MODEL),
            ckv_p.reshape(1, bp, seq_p, KV_LORA),
            jnp.swapaxes(kr_p, 1, 2)[None],
            tails[p_last][None],
            ckv_s.reshape(1, bs, seq_s, KV_LORA),
            jnp.swapaxes(kr_s, 1, 2)[None],
            tails[s_last][None])
```

```python
import functools

import jax
import jax.numpy as jnp
import numpy as np
from jax import lax
from jax.experimental import pallas as pl
from jax.experimental.pallas import tpu as pltpu

F32 = jnp.float32
BF16 = jnp.bfloat16

D_MODEL = 2048
N_HEADS = 8
QK_NOPE = 128
ROPE_DIM = 64
V_DIM = 128
Q_LORA = 512
KV_LORA = 512
ATTN_W = N_HEADS * V_DIM
CONV_CH = D_MODEL - ATTN_W
CONV_W = 3
CHUNK = 64
N_GROUPS = 4
EXPERTS_PER_GROUP = 8
N_EXPERTS = N_GROUPS * EXPERTS_PER_GROUP
D_FF = 512
ROPE_THETA = 10000.0
EPS = 1e-6
ATTN_SCALE = (QK_NOPE + ROPE_DIM) ** -0.5
EXP2_SCALE = ATTN_SCALE * 1.4426950408889634

LANES = 128
SUBLANES = 8
TM = 256
TD = 2304
MOE_BLOCK = 256
TQ = 512
TK = 1024
TKH = 512
TKX = 256
W_O_SLABS = 4
W_IN_SLABS = 10
NEG_BIG = -1e30
V7X_VMEM_BYTES = 64 * 1024 * 1024
VMEM_LIMIT = V7X_VMEM_BYTES * 7 // 8


def _rms(v, g):
    return v * lax.rsqrt(jnp.mean(v * v, axis=-1, keepdims=True) + EPS) * g


def _lane_bcast(v, width):
    if width % LANES == 0:
        return jnp.concatenate([v] * (width // LANES), axis=1)
    assert width < LANES
    return v[:, :width]


def _pack_bf16_pairs(v):
    half = v.shape[-1] // 2
    lo = lax.bitcast_convert_type(v[..., :half].astype(BF16).astype(F32), jnp.uint32)
    hi = lax.bitcast_convert_type(v[..., half:].astype(BF16).astype(F32), jnp.uint32)
    return (lo >> 16) | (hi & jnp.uint32(0xFFFF0000))


def _unpack_bf16_pairs(w):
    return (lax.bitcast_convert_type(w << 16, F32),
            lax.bitcast_convert_type(w & jnp.uint32(0xFFFF0000), F32))


def _load_weight_as_bf16(w_hbm, dst_ref, stage_ref, sems):
    rows = stage_ref.shape[1]
    n_slabs = dst_ref.shape[0] // rows

    def slab_copy(c):
        return pltpu.make_async_copy(w_hbm.at[pl.ds(c * rows, rows)], stage_ref.at[c % 2], sems.at[c % 2])

    slab_copy(0).start()
    for c in range(n_slabs):
        if c + 1 < n_slabs:
            slab_copy(c + 1).start()
        slab_copy(c).wait()
        dst_ref[c * rows:(c + 1) * rows, :] = stage_ref[c % 2].astype(BF16)


def _const_spec(shape):
    nd = len(shape)
    return pl.BlockSpec(shape, lambda *_: (0,) * nd, pipeline_mode=pl.Buffered(1))


def _in_proj_kernel(xp_ref, xs_ref, gmix_ref, w_hbm, gq_ref, gkv_ref, gco_ref, convw_ref,
                    cos_ref, sin_ref, state_ref,
                    cqn_ref, ckvp_ref, krp_ref, ckvs_ref, krs_ref, convn_ref, utail_ref,
                    ext_ref, wt_ref, stage_ref, wsems,
                    *, n_prompt_tiles, tiles_per_seq, n_prompt_seq, sample_seq_len):
    i = pl.program_id(0)

    @pl.when(i == 0)
    def _():
        ext_ref[...] = jnp.zeros(ext_ref.shape, F32)
        _load_weight_as_bf16(w_hbm, wt_ref, stage_ref, wsems)

    def conv_block(u_sub, gate_sub, row0, length):
        ext_ref[SUBLANES:SUBLANES + length, :] = u_sub
        um1 = ext_ref[SUBLANES - 1:SUBLANES - 1 + length, :]
        um2 = ext_ref[SUBLANES - 2:SUBLANES - 2 + length, :]
        cw = convw_ref[...]
        conv = cw[0:1] * um2 + cw[1:2] * um1 + cw[2:3] * u_sub
        convn_ref[row0:row0 + length, :] = _rms(gate_sub * conv, gco_ref[...]).astype(BF16)

    def tile(x_ref, is_prompt):
        ckv_ref, krt_ref = (ckvp_ref, krp_ref) if is_prompt else (ckvs_ref, krs_ref)
        x = x_ref[...]
        xg = (x * gmix_ref[...]).astype(BF16)
        inv_rms = lax.rsqrt(jnp.mean(x * x, axis=-1, keepdims=True) + EPS)
        lat_w = Q_LORA + KV_LORA
        conv0 = lat_w + ROPE_DIM
        nt = (((1,), (1,)), ((), ()))

        def project(lo, hi):
            return inv_rms * lax.dot_general(xg, wt_ref[lo:hi, :], nt, preferred_element_type=F32)

        z_ch = project(conv0 + CONV_CH, conv0 + 3 * CONV_CH)
        u = z_ch[:, :CONV_CH] * z_ch[:, CONV_CH:]
        for j in range(TM // CHUNK):
            utail_ref[j] = u[CHUNK * (j + 1) - SUBLANES:CHUNK * (j + 1), :]
        gate_b = project(conv0, conv0 + CONV_CH)

        if is_prompt:
            first = (i % tiles_per_seq) == 0
            carried = ext_ref[TM + SUBLANES - 2:TM + SUBLANES, :]
            ext_ref[SUBLANES - 2:SUBLANES, :] = jnp.where(first, state_ref[i // tiles_per_seq], carried)
            conv_block(u, gate_b, 0, TM)
        else:
            n_sub = TM // sample_seq_len
            seq0 = n_prompt_seq + (i - n_prompt_tiles) * n_sub
            for k in range(n_sub):
                ext_ref[SUBLANES - 2:SUBLANES, :] = state_ref[seq0 + k]
                lo = k * sample_seq_len
                conv_block(u[lo:lo + sample_seq_len], gate_b[lo:lo + sample_seq_len], lo, sample_seq_len)

        zk = project(lat_w, conv0)
        zk_swapped = jnp.concatenate([zk[:, ROPE_DIM // 2:], zk[:, :ROPE_DIM // 2]], axis=1)
        k_rope = zk * cos_ref[...] + zk_swapped * sin_ref[...]
        if is_prompt:
            krt_ref[...] = k_rope.T
        else:
            for k in range(TM // sample_seq_len):
                krt_ref[k] = k_rope[k * sample_seq_len:(k + 1) * sample_seq_len, :].T
        ckv_ref[...] = _rms(project(Q_LORA, lat_w), gkv_ref[...])
        cqn_ref[...] = _rms(project(0, Q_LORA), gq_ref[...]).astype(BF16)

    @pl.when(i < n_prompt_tiles)
    def _():
        tile(xp_ref, True)

    @pl.when(i >= n_prompt_tiles)
    def _():
        tile(xs_ref, False)


def _in_proj(xp, xs, gmix, w_t, gq, gkv, gco, convw, cosk, sink, state, *, seq_p, seq_s):
    np_rows, ns_rows = xp.shape[0], xs.shape[0]
    m = np_rows + ns_rows
    npt, nst = np_rows // TM, ns_rows // TM
    tps = seq_p // TM
    n_prompt_seq = np_rows // seq_p
    last_p = npt - 1

    def tab_idx(i):
        return (jnp.where(i < npt, i % tps, tps), 0)

    row = lambda i: (i, 0)
    prow = lambda i: (jnp.minimum(i, last_p), 0)
    srow = lambda i: (jnp.maximum(i - npt, 0), 0)
    kern = functools.partial(_in_proj_kernel, n_prompt_tiles=npt, tiles_per_seq=tps,
                             n_prompt_seq=n_prompt_seq, sample_seq_len=seq_s)
    return pl.pallas_call(
        kern,
        grid=(npt + nst,),
        in_specs=[
            pl.BlockSpec((TM, D_MODEL), prow),
            pl.BlockSpec((TM, D_MODEL), srow),
            _const_spec((1, D_MODEL)),
            pl.BlockSpec(memory_space=pl.ANY),
            _const_spec((1, Q_LORA)),
            _const_spec((1, KV_LORA)),
            _const_spec((1, CONV_CH)),
            _const_spec((CONV_W, CONV_CH)),
            pl.BlockSpec((TM, ROPE_DIM), tab_idx),
            pl.BlockSpec((TM, ROPE_DIM), tab_idx),
            _const_spec(state.shape),
        ],
        out_specs=[
            pl.BlockSpec((TM, Q_LORA), row),
            pl.BlockSpec((TM, KV_LORA), prow),
            pl.BlockSpec((None, ROPE_DIM, TM), lambda i: (jnp.minimum(i, last_p) // tps, 0,
                                                          jnp.minimum(i, last_p) % tps)),
            pl.BlockSpec((TM, KV_LORA), srow),
            pl.BlockSpec((TM // seq_s, ROPE_DIM, seq_s), lambda i: (jnp.maximum(i - npt, 0), 0, 0)),
            pl.BlockSpec((TM, CONV_CH), row),
            pl.BlockSpec((TM // CHUNK, SUBLANES, CONV_CH), lambda i: (i, 0, 0)),
        ],
        out_shape=[
            jax.ShapeDtypeStruct((m, Q_LORA), BF16),
            jax.ShapeDtypeStruct((np_rows, KV_LORA), F32),
            jax.ShapeDtypeStruct((n_prompt_seq, ROPE_DIM, seq_p), F32),
            jax.ShapeDtypeStruct((ns_rows, KV_LORA), F32),
            jax.ShapeDtypeStruct((ns_rows // seq_s, ROPE_DIM, seq_s), F32),
            jax.ShapeDtypeStruct((m, CONV_CH), BF16),
            jax.ShapeDtypeStruct((m // CHUNK, SUBLANES, CONV_CH), F32),
        ],
        scratch_shapes=[pltpu.VMEM((TM + SUBLANES, CONV_CH), F32),
                        pltpu.VMEM(w_t.shape, BF16),
                        pltpu.VMEM((2, w_t.shape[0] // W_IN_SLABS, w_t.shape[1]), F32),
                        pltpu.SemaphoreType.DMA((2,))],
        compiler_params=pltpu.CompilerParams(dimension_semantics=("arbitrary",),
                                             vmem_limit_bytes=VMEM_LIMIT),
        name="in_proj",
    )(xp, xs, gmix, w_t, gq, gkv, gco, convw, cosk, sink, state)


def _attn_kernel(cqn_ref, wq_ref, wuk_ref, wuv_ref, cos_ref, sin_ref, gao_ref, pkv_ref, pkr_ref, kv_ref, kr_ref,
                 out_ref, qlat_ref, qr_ref, m_ref, l_ref, acc_ref, s_ref, *, tq, n_past):
    rows = N_HEADS * tq

    q = jnp.dot(cqn_ref[...], wq_ref[...], preferred_element_type=F32)
    nope_w = N_HEADS * QK_NOPE
    rope_w = N_HEADS * ROPE_DIM
    qrope = q[:, nope_w:nope_w + rope_w] * cos_ref[...] + q[:, nope_w + rope_w:] * sin_ref[...]
    for h in range(N_HEADS):
        qn = q[:, h * QK_NOPE:(h + 1) * QK_NOPE].astype(BF16)
        ql = jnp.dot(qn, wuk_ref[h], preferred_element_type=F32)
        qlat_ref[h * tq:(h + 1) * tq, :] = (ql * EXP2_SCALE).astype(BF16)
        qr_ref[h * tq:(h + 1) * tq, :] = (qrope[:, h * ROPE_DIM:(h + 1) * ROPE_DIM] * EXP2_SCALE).astype(BF16)


    nt = (((1,), (1,)), ((), ()))

    def scores(kc_f32, krt_f32):
        s = lax.dot_general(qlat_ref[...], kc_f32.astype(BF16), nt, preferred_element_type=F32)
        return s + jnp.dot(qr_ref[...], krt_f32.astype(BF16), preferred_element_type=F32)

    def update(s, kc_f32, mask, first=False):
        if mask is not None:
            s = jnp.where(mask, s, NEG_BIG)
        m_cur = jnp.max(s, axis=-1, keepdims=True)
        if first:
            m_new = jnp.broadcast_to(m_cur, m_ref.shape)
        else:
            m_prev = m_ref[...]
            m_new = jnp.maximum(m_prev, m_cur)
            alpha = jnp.exp2(m_prev - m_new)
        p = jnp.exp2(s - _lane_bcast(m_new, s.shape[1]))
        l_cur = jnp.sum(p, axis=-1, keepdims=True)
        pv = jnp.dot(p.astype(BF16), kc_f32.astype(BF16), preferred_element_type=F32)
        if first:
            l_ref[...] = jnp.broadcast_to(l_cur, l_ref.shape)
            acc_ref[...] = pv
        else:
            l_ref[...] = alpha * l_ref[...] + l_cur
            acc_ref[...] = _lane_bcast(alpha, KV_LORA) * acc_ref[...] + pv
        m_ref[...] = m_new

    def pipelined(kv, kr, lo, hi, last, mask_fn):
        def body(j, c):
            k0 = pl.multiple_of(j * TK, TK)
            k1 = pl.multiple_of(jnp.minimum(j + 1, last) * TK, TK)
            s_cur = s_ref[j % 2]
            s_ref[(j + 1) % 2] = scores(kv[pl.ds(k1, TK), :], kr[:, pl.ds(k1, TK)])
            update(s_cur, kv[pl.ds(k0, TK), :], None if mask_fn is None else mask_fn(k0))
            return c
        lax.fori_loop(lo, hi, body, 0)

    def pipelined_pairs(kv, kr, n_pairs, last):
        def body(i, c):
            ka = pl.multiple_of((2 * i + 1) * TK, TK)
            kb = pl.multiple_of((2 * i + 2) * TK, TK)
            kc = pl.multiple_of(jnp.minimum(2 * i + 3, last) * TK, TK)
            s_ref[0] = scores(kv[pl.ds(kb, TK), :], kr[:, pl.ds(kb, TK)])
            update(s_ref[1], kv[pl.ds(ka, TK), :], None)
            s_ref[1] = scores(kv[pl.ds(kc, TK), :], kr[:, pl.ds(kc, TK)])
            update(s_ref[0], kv[pl.ds(kb, TK), :], None)
            return c
        lax.fori_loop(0, n_pairs, body, 0)

    def first_block(kv, kr, last, mask):
        k1 = pl.multiple_of(jnp.minimum(1, last) * TK, TK)
        s_ref[0] = scores(kv[pl.ds(0, TK), :], kr[:, pl.ds(0, TK)])
        s_ref[1] = scores(kv[pl.ds(k1, TK), :], kr[:, pl.ds(k1, TK)])
        update(s_ref[0], kv[pl.ds(0, TK), :], mask, first=True)

    n_pb = n_past // TK
    first_block(pkv_ref, pkr_ref, n_pb - 1, None)
    n_pairs = (n_pb - 1) // 2
    pipelined_pairs(pkv_ref, pkr_ref, n_pairs, n_pb - 1)
    if 1 + 2 * n_pairs < n_pb:
        pipelined(pkv_ref, pkr_ref, 1 + 2 * n_pairs, n_pb, n_pb - 1, None)

    update(scores(kv_ref[...], kr_ref[...]), kv_ref[...], None)

    o = acc_ref[...] / _lane_bcast(l_ref[...], KV_LORA)
    parts = []
    for h in range(N_HEADS):
        oh = o[h * tq:(h + 1) * tq, :].astype(BF16)
        parts.append(jnp.dot(oh, wuv_ref[h], preferred_element_type=F32))
    attn = jnp.concatenate(parts, axis=-1)
    out_ref[...] = _rms(attn, gao_ref[...]).astype(BF16)


def _attention(cqn, w_q, w_ukt, w_uv, cosq, sinq, gao, ckv, krope, past_kv, past_kr, *, n_batch, seq, row0):
    tq = seq
    n_past = past_kv.shape[1]
    assert n_past % CHUNK == 0 and seq <= CHUNK and n_past % TK == 0
    blk0 = row0 // tq
    in_specs = [
        pl.BlockSpec((tq, Q_LORA), lambda b, q: (blk0 + b, 0)),
        _const_spec(w_q.shape),
        _const_spec(w_ukt.shape),
        _const_spec(w_uv.shape),
        pl.BlockSpec((tq, N_HEADS * ROPE_DIM), lambda b, q: (0, 0)),
        pl.BlockSpec((tq, N_HEADS * ROPE_DIM), lambda b, q: (0, 0)),
        _const_spec((1, ATTN_W)),
        pl.BlockSpec((None, n_past, KV_LORA), lambda b, q: (b, 0, 0)),
        pl.BlockSpec((None, ROPE_DIM, n_past), lambda b, q: (b, 0, 0)),
        pl.BlockSpec((seq, KV_LORA), lambda b, q: (b, 0)),
        pl.BlockSpec((None, ROPE_DIM, seq), lambda b, q: (b, 0, 0)),
    ]
    args = [cqn, w_q, w_ukt, w_uv, cosq, sinq, gao, past_kv, past_kr, ckv, krope]
    rows = N_HEADS * tq
    return pl.pallas_call(
        functools.partial(_attn_kernel, tq=tq, n_past=n_past),
        grid=(n_batch, 1),
        in_specs=in_specs,
        out_specs=pl.BlockSpec((tq, ATTN_W), lambda b, q: (b, 0)),
        out_shape=jax.ShapeDtypeStruct((n_batch * seq, ATTN_W), BF16),
        scratch_shapes=[
            pltpu.VMEM((rows, KV_LORA), BF16),
            pltpu.VMEM((rows, ROPE_DIM), BF16),
            pltpu.VMEM((rows, LANES), F32),
            pltpu.VMEM((rows, LANES), F32),
            pltpu.VMEM((rows, KV_LORA), F32),
            pltpu.VMEM((2, rows, TK), F32),
        ],
        compiler_params=pltpu.CompilerParams(dimension_semantics=("arbitrary", "arbitrary"),
                                             vmem_limit_bytes=VMEM_LIMIT),
        name="attn_sample",
    )(*args)


def _attn_heads_kernel(cqn_ref, wq_ref, wkv_ref, cos_ref, sin_ref, gao_ref, kv_ref, krt_ref, out_ref,
                       kcat_ref, vh_ref, qcat_ref, m_ref, l_ref, acc_ref, klim_ref, *, tq, seq):
    qi = pl.program_id(1)
    nt = (((1,), (1,)), ((), ()))
    kw = QK_NOPE + ROPE_DIM
    kpad = kcat_ref.shape[-1]

    @pl.when(qi == 0)
    def _():
        def expand(j, c):
            k0 = pl.multiple_of(j * TKX, TKX)
            latent = kv_ref[pl.ds(k0, TKX), :].astype(BF16)
            kvh = jnp.dot(latent, wkv_ref[...], preferred_element_type=F32)
            k_rope = krt_ref[:, pl.ds(k0, TKX)].T.astype(BF16)
            for h in range(N_HEADS):
                kcat_ref[h, pl.ds(k0, TKX), :QK_NOPE] = kvh[:, h * QK_NOPE:(h + 1) * QK_NOPE].astype(BF16)
                kcat_ref[h, pl.ds(k0, TKX), QK_NOPE:kw] = k_rope
                kcat_ref[h, pl.ds(k0, TKX), kw:] = jnp.zeros((TKX, kpad - kw), BF16)
                v0 = N_HEADS * QK_NOPE + h * V_DIM
                vh_ref[h, pl.ds(k0, TKX), :] = kvh[:, v0:v0 + V_DIM].astype(BF16)
            return c
        lax.fori_loop(0, seq // TKX, expand, 0)

    q = jnp.dot(cqn_ref[...], wq_ref[...], preferred_element_type=F32)
    nope_w = N_HEADS * QK_NOPE
    rope_w = N_HEADS * ROPE_DIM
    qrope = q[:, nope_w:nope_w + rope_w] * cos_ref[...] + q[:, nope_w + rope_w:] * sin_ref[...]
    for h in range(N_HEADS):
        qcat_ref[h, :, :QK_NOPE] = (q[:, h * QK_NOPE:(h + 1) * QK_NOPE] * EXP2_SCALE).astype(BF16)
        qcat_ref[h, :, QK_NOPE:kw] = (qrope[:, h * ROPE_DIM:(h + 1) * ROPE_DIM] * EXP2_SCALE).astype(BF16)
        qcat_ref[h, :, kw:] = jnp.zeros((tq, kpad - kw), BF16)

    r = lax.broadcasted_iota(jnp.int32, (tq, LANES), 0)
    klim_ref[...] = ((qi * tq + r) & ~(CHUNK - 1)) + CHUNK

    tk = TKH

    def block(k0, masked, first):
        if masked:
            cidx = lax.broadcasted_iota(jnp.int32, (tq, tk), 1)
            mask = cidx < _lane_bcast(klim_ref[...] - k0, tk)
        for h in range(N_HEADS):
            s = lax.dot_general(qcat_ref[h], kcat_ref[h, pl.ds(k0, tk), :], nt, preferred_element_type=F32)
            if masked:
                s = jnp.where(mask, s, NEG_BIG)
            m_cur = jnp.max(s, axis=-1, keepdims=True)
            if first:
                m_new = jnp.broadcast_to(m_cur, (tq, LANES))
            else:
                m_prev = m_ref[h]
                m_new = jnp.maximum(m_prev, m_cur)
                alpha = jnp.exp2(m_prev - m_new)
            p = jnp.exp2(s - _lane_bcast(m_new, tk))
            l_cur = jnp.sum(p, axis=-1, keepdims=True)
            pv = jnp.dot(p.astype(BF16), vh_ref[h, pl.ds(k0, tk), :], preferred_element_type=F32)
            if first:
                l_ref[h] = jnp.broadcast_to(l_cur, (tq, LANES))
                acc_ref[h] = pv
            else:
                l_ref[h] = alpha * l_ref[h] + l_cur
                acc_ref[h] = _lane_bcast(alpha, V_DIM) * acc_ref[h] + pv
            m_ref[h] = m_new

    n_blocks = ((qi + 1) * tq + tk - 1) // tk
    n_full = jnp.minimum((qi * tq // CHUNK + 1) * CHUNK // tk, n_blocks)

    def loop(lo, hi, masked):
        def body(j, c):
            block(pl.multiple_of(j * tk, tk), masked, False)
            return c
        lax.fori_loop(lo, hi, body, 0)

    block(0, True, True)
    loop(1, n_full, False)
    loop(jnp.maximum(n_full, 1), n_blocks, True)

    attn = jnp.concatenate([acc_ref[h] / _lane_bcast(l_ref[h], V_DIM) for h in range(N_HEADS)], axis=-1)
    out_ref[...] = _rms(attn, gao_ref[...]).astype(BF16)


def _attention_heads(cqn, w_q, w_kv, cosq, sinq, gao, ckv, krope_t, *, n_batch, seq):
    nq = seq // TQ
    kpad = 2 * LANES
    assert QK_NOPE + ROPE_DIM <= kpad and V_DIM == LANES
    assert seq % TQ == 0 and seq % TKH == 0 and seq % TKX == 0 and TQ % CHUNK == 0 and TKH % CHUNK == 0
    return pl.pallas_call(
        functools.partial(_attn_heads_kernel, tq=TQ, seq=seq),
        grid=(n_batch, nq),
        in_specs=[
            pl.BlockSpec((TQ, Q_LORA), lambda b, q: (b * nq + q, 0)),
            _const_spec(w_q.shape),
            _const_spec(w_kv.shape),
            pl.BlockSpec((TQ, N_HEADS * ROPE_DIM), lambda b, q: (q, 0)),
            pl.BlockSpec((TQ, N_HEADS * ROPE_DIM), lambda b, q: (q, 0)),
            _const_spec((1, ATTN_W)),
            pl.BlockSpec((seq, KV_LORA), lambda b, q: (b, 0)),
            pl.BlockSpec((None, ROPE_DIM, seq), lambda b, q: (b, 0, 0)),
        ],
        out_specs=pl.BlockSpec((TQ, ATTN_W), lambda b, q: (b * nq + q, 0)),
        out_shape=jax.ShapeDtypeStruct((n_batch * seq, ATTN_W), BF16),
        scratch_shapes=[
            pltpu.VMEM((N_HEADS, seq, kpad), BF16),
            pltpu.VMEM((N_HEADS, seq, V_DIM), BF16),
            pltpu.VMEM((N_HEADS, TQ, kpad), BF16),
            pltpu.VMEM((N_HEADS, TQ, LANES), F32),
            pltpu.VMEM((N_HEADS, TQ, LANES), F32),
            pltpu.VMEM((N_HEADS, TQ, V_DIM), F32),
            pltpu.VMEM((TQ, LANES), jnp.int32),
        ],
        compiler_params=pltpu.CompilerParams(dimension_semantics=("arbitrary", "arbitrary"),
                                             vmem_limit_bytes=VMEM_LIMIT),
        name="attn_prompt",
    )(cqn, w_q, w_kv, cosq, sinq, gao, ckv, krope_t)


def _out_proj_kernel(attnp_ref, attns_ref, convn_ref, xp_ref, xs_ref, wo_hbm, gffn_ref, wr_ref,
                     br_ref, h_ref, xpk_ref, mi_ref, mf_ref, cnt_ref, carry_ref, logit_ref,
                     wo_ref, stage_ref, wsems, *, n_prompt_tiles):
    i = pl.program_id(0)

    @pl.when(i == 0)
    def _():
        carry_ref[...] = jnp.zeros(carry_ref.shape, F32)
        logit_ref[...] = jnp.zeros(logit_ref.shape, F32)
        _load_weight_as_bf16(wo_hbm, wo_ref, stage_ref, wsems)

    def tile(x_ref, attn_ref):
        prev_logits = logit_ref[...]
        y = jnp.dot(attn_ref[...], wo_ref[:ATTN_W, :], preferred_element_type=F32)
        y = y + jnp.dot(convn_ref[...], wo_ref[ATTN_W:, :], preferred_element_type=F32)
        h = x_ref[...] + y
        h_ref[...] = h
        xn = _rms(h, gffn_ref[...])

        half = D_MODEL // 2
        xh = xn.astype(BF16)
        xh32 = xh.astype(F32)
        lo = lax.bitcast_convert_type(xh32[:, :half], jnp.uint32)
        hi = lax.bitcast_convert_type(xh32[:, half:], jnp.uint32)
        xpk_ref[...] = (lo >> 16) | (hi & jnp.uint32(0xFFFF0000))

        xl = (xn - xh32).astype(BF16)
        hh_hl = jnp.dot(xh, wr_ref[...], preferred_element_type=F32)
        lh = jnp.dot(xl, wr_ref[:, :LANES], preferred_element_type=F32)
        logit_ref[...] = hh_hl[:, :LANES] + (lh + hh_hl[:, LANES:]) + br_ref[...]

        logits = prev_logits
        counted = (i > 0).astype(F32)
        lane = lax.broadcasted_iota(jnp.int32, (TM, LANES), 1).astype(F32)
        ninf = -jnp.inf
        far = float(LANES)

        def first_argmax(v):
            vmax = jnp.max(v, axis=-1, keepdims=True)
            return vmax, jnp.min(jnp.where(v == vmax, lane, far), axis=-1, keepdims=True)

        gl = jnp.where(lane < N_GROUPS, logits, ninf)
        gmax, gidx = first_argmax(gl)
        g_p = 1.0 / jnp.sum(jnp.exp(gl - gmax), axis=-1, keepdims=True)
        e_lo = N_GROUPS + EXPERTS_PER_GROUP * gidx
        el = jnp.where((lane >= e_lo) & (lane < e_lo + EXPERTS_PER_GROUP), logits, ninf)
        e1max, i1 = first_argmax(el)
        z = jnp.sum(jnp.exp(el - e1max), axis=-1, keepdims=True)
        el2 = jnp.where(lane == i1, ninf, el)
        e2max, i2 = first_argmax(el2)
        p1 = 1.0 / z
        p2 = jnp.exp(e2max - e1max) / z
        den = p1 + p2
        g0 = g_p * p1 / den
        g1 = g_p * p2 / den
        e0 = i1 - N_GROUPS
        e1 = i2 - N_GROUPS

        oh0 = lane == e0
        oh1 = lane == e1
        oh = jnp.where(oh0 | oh1, 1.0, 0.0)
        r = lax.broadcasted_iota(jnp.int32, (TM, TM), 0)
        c = lax.broadcasted_iota(jnp.int32, (TM, TM), 1)
        ltri = jnp.where(r > c, 1.0, 0.0).astype(BF16)
        before = jnp.dot(ltri, oh.astype(BF16), preferred_element_type=F32) + carry_ref[...]
        rank0 = jnp.sum(jnp.where(oh0, before, 0.0), axis=-1, keepdims=True)
        rank1 = jnp.sum(jnp.where(oh1, before, 0.0), axis=-1, keepdims=True)
        total = carry_ref[...] + counted * jnp.sum(oh, axis=0, keepdims=True)
        carry_ref[...] = total
        cnt_ref[...] = jnp.broadcast_to(total, cnt_ref.shape)

        mi = jnp.where(lane == 0, e0, jnp.where(lane == 1, e1, jnp.where(lane == 2, rank0, rank1)))
        mi_ref[...] = jnp.transpose(mi)[:SUBLANES, :].astype(jnp.int32)
        mf_ref[...] = jnp.where(lane == 0, g0, g1)

    @pl.when(i < n_prompt_tiles)
    def _():
        tile(xp_ref, attnp_ref)

    @pl.when(i >= n_prompt_tiles)
    def _():
        tile(xs_ref, attns_ref)


def _out_proj(attn_p, attn_s, conv_n, xp, xs, w_ob, gffn, w_r2, b_r):
    m = conv_n.shape[0]
    npt = xp.shape[0] // TM
    n_tiles = m // TM
    last_p, last_s, last = npt - 1, n_tiles - npt - 1, n_tiles - 1
    row = lambda i: (jnp.minimum(i, last), 0)
    prow = lambda i: (jnp.minimum(i, last_p), 0)
    srow = lambda i: (jnp.clip(i - npt, 0, last_s), 0)
    lag = lambda i: jnp.maximum(i - 1, 0)
    return pl.pallas_call(
        functools.partial(_out_proj_kernel, n_prompt_tiles=npt),
        grid=(n_tiles + 1,),
        in_specs=[
            pl.BlockSpec((TM, ATTN_W), prow),
            pl.BlockSpec((TM, ATTN_W), srow),
            pl.BlockSpec((TM, CONV_CH), row),
            pl.BlockSpec((TM, D_MODEL), prow),
            pl.BlockSpec((TM, D_MODEL), srow),
            pl.BlockSpec(memory_space=pl.ANY),
            _const_spec((1, D_MODEL)),
            _const_spec(w_r2.shape),
            _const_spec((1, LANES)),
        ],
        out_specs=[
            pl.BlockSpec((TM, D_MODEL), row),
            pl.BlockSpec((TM, D_MODEL // 2), row),
            pl.BlockSpec((SUBLANES, TM), lambda i: (0, lag(i))),
            pl.BlockSpec((TM, LANES), lambda i: (lag(i), 0)),
            pl.BlockSpec((SUBLANES, LANES), lambda i: (0, 0)),
        ],
        out_shape=[
            jax.ShapeDtypeStruct((m, D_MODEL), F32),
            jax.ShapeDtypeStruct((m, D_MODEL // 2), jnp.uint32),
            jax.ShapeDtypeStruct((SUBLANES, m), jnp.int32),
            jax.ShapeDtypeStruct((m, LANES), F32),
            jax.ShapeDtypeStruct((SUBLANES, LANES), F32),
        ],
        scratch_shapes=[pltpu.VMEM((1, LANES), F32), pltpu.VMEM((TM, LANES), F32),
                        pltpu.VMEM(w_ob.shape, BF16),
                        pltpu.VMEM((2, w_ob.shape[0] // W_O_SLABS, w_ob.shape[1]), F32),
                        pltpu.SemaphoreType.DMA((2,))],
        compiler_params=pltpu.CompilerParams(dimension_semantics=("arbitrary",),
                                             vmem_limit_bytes=VMEM_LIMIT),
        name="out_proj",
    )(attn_p, attn_s, conv_n, xp, xs, w_ob, gffn, w_r2, b_r)


def _dispatch_kernel(d0_ref, d1_ref, zlo_ref, zn_ref, nu_ref, xpk_ref, xs_hbm, zeros_ref, sems, *, n_blocks):
    i = pl.program_id(0)
    sem = sems.at[0]
    zsem = sems.at[1]

    def zero_fill(act):
        def per_expert(e, c):
            lo = zlo_ref[e]
            n = zn_ref[e]
            head = (-lo) & (SUBLANES - 1)
            for r in range(SUBLANES - 1):
                @pl.when(r < head)
                def _(r=r):
                    act(pltpu.make_async_copy(zeros_ref.at[pl.ds(0, 1)], xs_hbm.at[pl.ds(lo + r, 1)], zsem))
            off = lo + head
            rest = n - head
            size = MOE_BLOCK // 2
            while size >= SUBLANES:
                @pl.when((rest & size) != 0)
                def _(off=off, size=size):
                    dst = xs_hbm.at[pl.ds(pl.multiple_of(off, SUBLANES), size)]
                    act(pltpu.make_async_copy(zeros_ref.at[pl.ds(0, size)], dst, zsem))
                off = off + (rest & size)
                size //= 2
            return c

        def per_block(b, c):
            dst = xs_hbm.at[pl.ds(pl.multiple_of(b * MOE_BLOCK, MOE_BLOCK), MOE_BLOCK)]
            act(pltpu.make_async_copy(zeros_ref, dst, zsem))
            return c

        lax.fori_loop(0, N_EXPERTS, per_expert, 0)
        lax.fori_loop(nu_ref[0], n_blocks, per_block, 0)

    @pl.when(i == 0)
    def _():
        zeros_ref[...] = jnp.zeros(zeros_ref.shape, zeros_ref.dtype)
        zero_fill(lambda cp: cp.start())

    @pl.when(i == pl.num_programs(0) - 1)
    def _():
        zero_fill(lambda cp: cp.wait())

    base = i * TD

    def start(g, c):
        for u in range(SUBLANES):
            r = base + g * SUBLANES + u
            src = xpk_ref.at[g, pl.ds(u, 1)]
            pltpu.make_async_copy(src, xs_hbm.at[pl.ds(d0_ref[r], 1)], sem).start()
            pltpu.make_async_copy(src, xs_hbm.at[pl.ds(d1_ref[r], 1)], sem).start()
        return c

    lax.fori_loop(0, TD // SUBLANES, start, 0)
    for _ in range(2):
        pltpu.make_async_copy(xs_hbm.at[pl.ds(0, TD)], xs_hbm.at[pl.ds(0, TD)], sem).wait()


def _dispatch(dest0, dest1, pad_lo, n_pad, n_used, xpk, n_blocks):
    m = xpk.shape[0]
    grid_spec = pltpu.PrefetchScalarGridSpec(
        num_scalar_prefetch=5,
        grid=(m // TD,),
        in_specs=[pl.BlockSpec((TD // SUBLANES, SUBLANES, D_MODEL // 2), lambda i, *_: (i, 0, 0))],
        out_specs=pl.BlockSpec(memory_space=pl.ANY),
        scratch_shapes=[pltpu.VMEM((MOE_BLOCK, D_MODEL // 2), jnp.uint32),
                        pltpu.SemaphoreType.DMA((2,))],
    )
    return pl.pallas_call(
        functools.partial(_dispatch_kernel, n_blocks=n_blocks),
        grid_spec=grid_spec,
        out_shape=jax.ShapeDtypeStruct((n_blocks * MOE_BLOCK, D_MODEL // 2), jnp.uint32),
        compiler_params=pltpu.CompilerParams(dimension_semantics=("arbitrary",)),
        name="dispatch",
    )(dest0, dest1, pad_lo, n_pad, n_used, xpk.reshape(m // SUBLANES, SUBLANES, D_MODEL // 2))


def _experts_kernel(be_ref, nu_ref, nxt_ref, x_ref, wg_hbm, wu_hbm, wd_hbm, y_ref,
                    sg_ref, su_ref, sd_ref, wgb_ref, wub_ref, wdb_ref, sems):
    b = pl.program_id(0)
    active = b < nu_ref[0]
    new_expert = jnp.logical_or(b == 0, be_ref[b] != be_ref[jnp.maximum(b - 1, 0)])

    def weight_copies(e):
        return (pltpu.make_async_copy(wg_hbm.at[e], sg_ref, sems.at[0]),
                pltpu.make_async_copy(wu_hbm.at[e], su_ref, sems.at[1]),
                pltpu.make_async_copy(wd_hbm.at[e], sd_ref, sems.at[2]))

    @pl.when(b == 0)
    def _():
        for cp in weight_copies(be_ref[0]):
            cp.start()

    @pl.when(jnp.logical_and(active, new_expert))
    def _():
        for cp in weight_copies(be_ref[b]):
            cp.wait()
        wgb_ref[...] = sg_ref[...].astype(BF16)
        wub_ref[...] = su_ref[...].astype(BF16)
        wdb_ref[...] = sd_ref[...].astype(BF16)

        @pl.when(nxt_ref[b] >= 0)
        def _():
            for cp in weight_copies(nxt_ref[b]):
                cp.start()

    @pl.when(active)
    def _():
        half = D_MODEL // 2
        xa, xb = (v.astype(BF16) for v in _unpack_bf16_pairs(x_ref[...]))
        g = jnp.dot(xa, wgb_ref[:half, :], preferred_element_type=F32)
        g = g + jnp.dot(xb, wgb_ref[half:, :], preferred_element_type=F32)
        u = jnp.dot(xa, wub_ref[:half, :], preferred_element_type=F32)
        u = u + jnp.dot(xb, wub_ref[half:, :], preferred_element_type=F32)
        hmid = (g * jax.nn.sigmoid(g)) * u
        y = jnp.dot(hmid.astype(BF16), wdb_ref[...], preferred_element_type=F32)
        y_ref[...] = _pack_bf16_pairs(y)

    @pl.when(b >= nu_ref[0])
    def _():
        y_ref[...] = jnp.zeros(y_ref.shape, y_ref.dtype)


def _experts(block_e, n_used, next_e, x_sorted, w_gate, w_up, w_down):
    p = x_sorted.shape[0]
    nb = p // MOE_BLOCK

    def xrow(b, be, nu, nxt):
        return (jnp.maximum(jnp.minimum(b, nu[0] - 1), 0), 0)

    grid_spec = pltpu.PrefetchScalarGridSpec(
        num_scalar_prefetch=3,
        grid=(nb,),
        in_specs=[
            pl.BlockSpec((MOE_BLOCK, D_MODEL // 2), xrow),
            pl.BlockSpec(memory_space=pl.ANY),
            pl.BlockSpec(memory_space=pl.ANY),
            pl.BlockSpec(memory_space=pl.ANY),
        ],
        out_specs=pl.BlockSpec((MOE_BLOCK, D_MODEL // 2), lambda b, be, nu, nxt: (b, 0)),
        scratch_shapes=[pltpu.VMEM((D_MODEL, D_FF), F32), pltpu.VMEM((D_MODEL, D_FF), F32),
                        pltpu.VMEM((D_FF, D_MODEL), F32),
                        pltpu.VMEM((D_MODEL, D_FF), BF16), pltpu.VMEM((D_MODEL, D_FF), BF16),
                        pltpu.VMEM((D_FF, D_MODEL), BF16),
                        pltpu.SemaphoreType.DMA((3,))],
    )
    return pl.pallas_call(
        _experts_kernel,
        grid_spec=grid_spec,
        out_shape=jax.ShapeDtypeStruct((p, D_MODEL // 2), jnp.uint32),
        compiler_params=pltpu.CompilerParams(dimension_semantics=("arbitrary",),
                                             vmem_limit_bytes=VMEM_LIMIT),
        name="experts",
    )(block_e, n_used, next_e, x_sorted, w_gate, w_up, w_down)


def _combine_kernel(d0_ref, d1_ref, h_ref, mf_ref, gfin_ref, y_hbm, outp_ref, outs_ref, y0_ref, y1_ref, sems,
                    *, n_tiles, n_prompt_tiles):
    i = pl.program_id(0)

    def gather(tile, slot, act):
        base = tile * TM

        def body(g, c):
            for u in range(SUBLANES):
                r = base + g * SUBLANES + u
                act(pltpu.make_async_copy(y_hbm.at[pl.ds(d0_ref[r], 1)], y0_ref.at[slot, g, pl.ds(u, 1)],
                                          sems.at[slot]))
                act(pltpu.make_async_copy(y_hbm.at[pl.ds(d1_ref[r], 1)], y1_ref.at[slot, g, pl.ds(u, 1)],
                                          sems.at[slot]))
            return c
        lax.fori_loop(0, TM // SUBLANES, body, 0)

    @pl.when(i == 0)
    def _():
        gather(0, 0, lambda cp: cp.start())

    @pl.when(i + 1 < n_tiles)
    def _():
        gather(i + 1, (i + 1) % 2, lambda cp: cp.start())

    slot = i % 2
    for _ in range(2):
        pltpu.make_async_copy(y_hbm.at[pl.ds(0, TM)], y_hbm.at[pl.ds(0, TM)], sems.at[slot]).wait()

    def finish(out_ref):
        mf = mf_ref[...]
        g0, g1 = mf[:, :, 0:1], mf[:, :, 1:2]
        half = D_MODEL // 2
        a0, b0 = _unpack_bf16_pairs(y0_ref[slot])
        a1, b1 = _unpack_bf16_pairs(y1_ref[slot])
        o_lo = h_ref[:, :, :half] + (g0 * a0 + g1 * a1)
        o_hi = h_ref[:, :, half:] + (g0 * b0 + g1 * b1)
        sumsq = jnp.sum(o_lo * o_lo, axis=-1, keepdims=True) + jnp.sum(o_hi * o_hi, axis=-1, keepdims=True)
        inv_rms = lax.rsqrt(sumsq / D_MODEL + EPS)
        out_ref[:, :, :half] = o_lo * inv_rms * gfin_ref[:, :, :half]
        out_ref[:, :, half:] = o_hi * inv_rms * gfin_ref[:, :, half:]

    @pl.when(i < n_prompt_tiles)
    def _():
        finish(outp_ref)

    @pl.when(i >= n_prompt_tiles)
    def _():
        finish(outs_ref)


def _combine(dest0, dest1, h, mf, gfin, y_sorted, *, n_prompt_rows):
    m = h.shape[0]
    npt = n_prompt_rows // TM
    tg = TM // SUBLANES
    grouped = lambda a: a.reshape(a.shape[0] // SUBLANES, SUBLANES, a.shape[1])
    grid_spec = pltpu.PrefetchScalarGridSpec(
        num_scalar_prefetch=2,
        grid=(m // TM,),
        in_specs=[
            pl.BlockSpec((tg, SUBLANES, D_MODEL), lambda i, *_: (i, 0, 0)),
            pl.BlockSpec((tg, SUBLANES, LANES), lambda i, *_: (i, 0, 0)),
            pl.BlockSpec((1, 1, D_MODEL), lambda i, *_: (0, 0, 0)),
            pl.BlockSpec(memory_space=pl.ANY),
        ],
        out_specs=[pl.BlockSpec((tg, SUBLANES, D_MODEL), lambda i, *_: (jnp.minimum(i, npt - 1), 0, 0)),
                   pl.BlockSpec((tg, SUBLANES, D_MODEL), lambda i, *_: (jnp.maximum(i - npt, 0), 0, 0))],
        scratch_shapes=[pltpu.VMEM((2, tg, SUBLANES, D_MODEL // 2), jnp.uint32),
                        pltpu.VMEM((2, tg, SUBLANES, D_MODEL // 2), jnp.uint32),
                        pltpu.SemaphoreType.DMA((2,))],
    )
    y_p, y_s = pl.pallas_call(
        functools.partial(_combine_kernel, n_tiles=m // TM, n_prompt_tiles=npt),
        grid_spec=grid_spec,
        out_shape=[jax.ShapeDtypeStruct((n_prompt_rows // SUBLANES, SUBLANES, D_MODEL), F32),
                   jax.ShapeDtypeStruct(((m - n_prompt_rows) // SUBLANES, SUBLANES, D_MODEL), F32)],
        compiler_params=pltpu.CompilerParams(dimension_semantics=("arbitrary",),
                                             vmem_limit_bytes=VMEM_LIMIT),
        name="combine",
    )(dest0, dest1, grouped(h), grouped(mf), gfin.reshape(1, 1, D_MODEL), y_sorted)
    return y_p.reshape(n_prompt_rows, D_MODEL), y_s.reshape(m - n_prompt_rows, D_MODEL)


def _rope_tables(pos):
    f32 = np.float32
    inv = np.power(f32(ROPE_THETA), -np.arange(0, ROPE_DIM, 2, dtype=f32) / f32(ROPE_DIM)).astype(f32)
    ang = (pos.astype(f32)[:, None] * inv[None, :]).astype(f32)
    cos, sin = np.cos(ang).astype(f32), np.sin(ang).astype(f32)
    return np.concatenate([cos, cos], axis=-1), np.concatenate([-sin, sin], axis=-1)


def _swap_halves(w):
    return jnp.concatenate([w[..., ROPE_DIM // 2:], w[..., :ROPE_DIM // 2]], axis=-1)


def kernel(x_prompt, x_sample, cache_kv_latent, cache_k_rope, state_conv, norm_mix, w_in, norm_q, w_uq,
           norm_kv, w_uk, w_uv, conv_w, norm_attn_out, norm_conv_out, w_o, norm_ffn, w_router_group,
           b_router_group, w_router_expert, b_router_expert, w_gate, w_up, w_down, norm_final):
    assert w_in.shape[0] == 1, "single-layer trunk"
    bp, seq_p, _ = x_prompt.shape
    bs, seq_s, _ = x_sample.shape
    past_len = cache_kv_latent.shape[2]
    np_rows, ns_rows = bp * seq_p, bs * seq_s
    m = np_rows + ns_rows
    assert seq_p % TM == 0 and TM % seq_s == 0 and ns_rows % TM == 0 and seq_s == CHUNK
    assert m % TD == 0

    xp = x_prompt.reshape(np_rows, D_MODEL)
    xs = x_sample.reshape(ns_rows, D_MODEL)
    row_vec = lambda v: v.reshape(1, -1)

    assert w_in.shape[2] == Q_LORA + KV_LORA + ROPE_DIM + 3 * CONV_CH
    w_t = jnp.swapaxes(w_in[0], 0, 1)
    assert w_t.shape[0] % (W_IN_SLABS * 2 * SUBLANES) == 0
    wq4 = w_uq[0].reshape(Q_LORA, N_HEADS, QK_NOPE + ROPE_DIM)
    wq_rope = wq4[:, :, QK_NOPE:]
    w_q = jnp.concatenate([wq4[:, :, :QK_NOPE].reshape(Q_LORA, -1), wq_rope.reshape(Q_LORA, -1),
                           _swap_halves(wq_rope).reshape(Q_LORA, -1)], axis=1).astype(BF16)
    w_ukt = jnp.transpose(w_uk[0], (1, 2, 0)).astype(BF16)
    w_uvh = jnp.transpose(w_uv[0], (1, 0, 2)).astype(BF16)
    w_ob = w_o[0]
    n_router = N_GROUPS + N_EXPERTS
    w_r = jnp.concatenate([w_router_group[0], w_router_expert[0].reshape(D_MODEL, N_EXPERTS)], axis=1)
    w_r = jnp.pad(w_r, ((0, 0), (0, LANES - n_router)))
    w_rh = w_r.astype(BF16)
    w_rl = (w_r - w_rh.astype(F32)).astype(BF16)
    w_r2 = jnp.concatenate([w_rh, w_rl], axis=1)
    b_r =jnp.pad(jnp.concatenate([b_router_group[0], b_router_expert[0].reshape(N_EXPERTS)]),
                  (0, LANES - n_router)).reshape(1, LANES)

    cos_p, sin_p = _rope_tables(np.arange(seq_p))
    cos_s, sin_s = _rope_tables(past_len + np.arange(seq_s))
    cosk = np.concatenate([cos_p, np.tile(cos_s, (TM // seq_s, 1))], axis=0)
    sink = np.concatenate([sin_p, np.tile(sin_s, (TM // seq_s, 1))], axis=0)
    state = jnp.concatenate([jnp.zeros((bp, CONV_W - 1, CONV_CH), F32), state_conv[0]], axis=0)

    cqn, ckv_p, kr_p, ckv_s, kr_s, conv_n, utail = _in_proj(
        xp, xs, row_vec(norm_mix[0]), w_t, row_vec(norm_q[0]), row_vec(norm_kv[0]),
        row_vec(norm_conv_out[0]), conv_w[0], cosk, sink, state, seq_p=seq_p, seq_s=seq_s)

    gao = row_vec(norm_attn_out[0])
    w_kv = jnp.concatenate([w_uk[0].reshape(KV_LORA, N_HEADS * QK_NOPE),
                            w_uv[0].reshape(KV_LORA, N_HEADS * V_DIM)], axis=1).astype(BF16)
    attn_p = _attention_heads(cqn, w_q, w_kv, np.tile(cos_p, (1, N_HEADS)), np.tile(sin_p, (1, N_HEADS)),
                              gao, ckv_p, kr_p, n_batch=bp, seq=seq_p)
    attn_s = _attention(cqn, w_q, w_ukt, w_uvh, np.tile(cos_s, (1, N_HEADS)), np.tile(sin_s, (1, N_HEADS)),
                        gao, ckv_s, kr_s, cache_kv_latent[0], jnp.swapaxes(cache_k_rope[0], 1, 2),
                        n_batch=bs, seq=seq_s, row0=np_rows)

    h, xpk, mi, mf, cnt = _out_proj(attn_p, attn_s, conv_n, xp, xs, w_ob, row_vec(norm_ffn[0]),
                                    w_r2, b_r)

    counts = cnt[0, :N_EXPERTS].astype(jnp.int32)
    padded = (counts + MOE_BLOCK - 1) // MOE_BLOCK * MOE_BLOCK
    pad_end = jnp.cumsum(padded)
    pad_start = pad_end - padded
    n_blocks = -(-(m * 2) // MOE_BLOCK) + N_EXPERTS
    block_row0 = jnp.arange(n_blocks, dtype=jnp.int32) * MOE_BLOCK
    block_e = jnp.minimum(jnp.sum((pad_end[None, :] <= block_row0[:, None]).astype(jnp.int32), axis=1),
                          N_EXPERTS - 1)
    n_used = (pad_end[-1:] // MOE_BLOCK).astype(jnp.int32)
    expert_ids = jnp.arange(N_EXPERTS, dtype=jnp.int32)[:, None]

    def seg_start(e):
        return jnp.sum(jnp.where(expert_ids == e[None, :], pad_start[:, None], 0), axis=0)

    dest0 = seg_start(mi[0]) + mi[2]
    dest1 = seg_start(mi[1]) + mi[3]

    x_sorted = _dispatch(dest0, dest1, pad_start + counts, padded - counts, n_used, xpk, n_blocks)
    later = (expert_ids.T > block_e[:, None]) & (padded > 0)[None, :]
    next_e = jnp.min(jnp.where(later, expert_ids.T, N_EXPERTS), axis=1)
    next_e = jnp.where(next_e == N_EXPERTS, -1, next_e).astype(jnp.int32)
    y_sorted = _experts(block_e, n_used, next_e, x_sorted, w_gate[0], w_up[0], w_down[0])
    gfin = row_vec(norm_final)
    y_p, y_s = _combine(dest0, dest1, h, mf, gfin, y_sorted, n_prompt_rows=np_rows)

    ut = utail.reshape(m // CHUNK, SUBLANES, CONV_CH)
    tails = ut[:, SUBLANES - (CONV_W - 1):, :]
    p_last = (jnp.arange(bp) + 1) * (seq_p // CHUNK) - 1
    s_last = np_rows // CHUNK + (jnp.arange(bs) + 1) * (seq_s // CHUNK) - 1
    return (y_p.reshape(bp, seq_p, D_MODEL),
            y_s.reshape(bs, seq_s, D_MODEL),
            ckv_p.reshape(1, bp, seq_p, KV_LORA),
            jnp.swapaxes(kr_p, 1, 2)[None],
            tails[p_last][None],
            ckv_s.reshape(1, bs, seq_s, KV_LORA),
            jnp.swapaxes(kr_s, 1, 2)[None],
            tails[s_last][None])
```

```python
import functools

import jax
import jax.numpy as jnp
import numpy as np
from jax import lax
from jax.experimental import pallas as pl
from jax.experimental.pallas import tpu as pltpu

F32 = jnp.float32
BF16 = jnp.bfloat16

D_MODEL = 2048
N_HEADS = 8
QK_NOPE = 128
ROPE_DIM = 64
V_DIM = 128
Q_LORA = 512
KV_LORA = 512
ATTN_W = N_HEADS * V_DIM
CONV_CH = D_MODEL - ATTN_W
CONV_W = 3
CHUNK = 64
N_GROUPS = 4
EXPERTS_PER_GROUP = 8
N_EXPERTS = N_GROUPS * EXPERTS_PER_GROUP
D_FF = 512
ROPE_THETA = 10000.0
EPS = 1e-6
ATTN_SCALE = (QK_NOPE + ROPE_DIM) ** -0.5
EXP2_SCALE = ATTN_SCALE * 1.4426950408889634

LANES = 128
SUBLANES = 8
TM = 256
TD = 2304
MOE_BLOCK = 256
TQ = 512
TK = 512
TKH = 512
TKX = 256
W_O_SLABS = 4
W_IN_SLABS = 10
NEG_BIG = -1e30
V7X_VMEM_BYTES = 64 * 1024 * 1024
VMEM_LIMIT = V7X_VMEM_BYTES * 7 // 8


def _rms(v, g):
    return v * lax.rsqrt(jnp.mean(v * v, axis=-1, keepdims=True) + EPS) * g


def _lane_bcast(v, width):
    if width % LANES == 0:
        return jnp.concatenate([v] * (width // LANES), axis=1)
    assert width < LANES
    return v[:, :width]


def _pack_bf16_pairs(v):
    half = v.shape[-1] // 2
    lo = lax.bitcast_convert_type(v[..., :half].astype(BF16).astype(F32), jnp.uint32)
    hi = lax.bitcast_convert_type(v[..., half:].astype(BF16).astype(F32), jnp.uint32)
    return (lo >> 16) | (hi & jnp.uint32(0xFFFF0000))


def _unpack_bf16_pairs(w):
    return (lax.bitcast_convert_type(w << 16, F32),
            lax.bitcast_convert_type(w & jnp.uint32(0xFFFF0000), F32))


def _load_weight_as_bf16(w_hbm, dst_ref, stage_ref, sems):
    rows = stage_ref.shape[1]
    n_slabs = dst_ref.shape[0] // rows

    def slab_copy(c):
        return pltpu.make_async_copy(w_hbm.at[pl.ds(c * rows, rows)], stage_ref.at[c % 2], sems.at[c % 2])

    slab_copy(0).start()
    for c in range(n_slabs):
        if c + 1 < n_slabs:
            slab_copy(c + 1).start()
        slab_copy(c).wait()
        dst_ref[c * rows:(c + 1) * rows, :] = stage_ref[c % 2].astype(BF16)


def _const_spec(shape):
    nd = len(shape)
    return pl.BlockSpec(shape, lambda *_: (0,) * nd, pipeline_mode=pl.Buffered(1))


def _in_proj_kernel(xp_ref, xs_ref, gmix_ref, w_hbm, gq_ref, gkv_ref, gco_ref, convw_ref,
                    cos_ref, sin_ref, state_ref,
                    cqn_ref, ckvp_ref, krp_ref, ckvs_ref, krs_ref, convn_ref, utail_ref,
                    ext_ref, wt_ref, stage_ref, wsems,
                    *, n_prompt_tiles, tiles_per_seq, n_prompt_seq, sample_seq_len):
    i = pl.program_id(0)

    @pl.when(i == 0)
    def _():
        ext_ref[...] = jnp.zeros(ext_ref.shape, F32)
        _load_weight_as_bf16(w_hbm, wt_ref, stage_ref, wsems)

    def conv_block(u_sub, gate_sub, row0, length):
        ext_ref[SUBLANES:SUBLANES + length, :] = u_sub
        um1 = ext_ref[SUBLANES - 1:SUBLANES - 1 + length, :]
        um2 = ext_ref[SUBLANES - 2:SUBLANES - 2 + length, :]
        cw = convw_ref[...]
        conv = cw[0:1] * um2 + cw[1:2] * um1 + cw[2:3] * u_sub
        convn_ref[row0:row0 + length, :] = _rms(gate_sub * conv, gco_ref[...]).astype(BF16)

    def tile(x_ref, is_prompt):
        ckv_ref, krt_ref = (ckvp_ref, krp_ref) if is_prompt else (ckvs_ref, krs_ref)
        x = x_ref[...]
        xg = (x * gmix_ref[...]).astype(BF16)
        inv_rms = lax.rsqrt(jnp.mean(x * x, axis=-1, keepdims=True) + EPS)
        lat_w = Q_LORA + KV_LORA
        conv0 = lat_w + ROPE_DIM
        nt = (((1,), (1,)), ((), ()))

        def project(lo, hi):
            return inv_rms * lax.dot_general(xg, wt_ref[lo:hi, :], nt, preferred_element_type=F32)

        z_ch = project(conv0 + CONV_CH, conv0 + 3 * CONV_CH)
        u = z_ch[:, :CONV_CH] * z_ch[:, CONV_CH:]
        for j in range(TM // CHUNK):
            utail_ref[j] = u[CHUNK * (j + 1) - SUBLANES:CHUNK * (j + 1), :]
        gate_b = project(conv0, conv0 + CONV_CH)

        if is_prompt:
            first = (i % tiles_per_seq) == 0
            carried = ext_ref[TM + SUBLANES - 2:TM + SUBLANES, :]
            ext_ref[SUBLANES - 2:SUBLANES, :] = jnp.where(first, state_ref[i // tiles_per_seq], carried)
            conv_block(u, gate_b, 0, TM)
        else:
            n_sub = TM // sample_seq_len
            seq0 = n_prompt_seq + (i - n_prompt_tiles) * n_sub
            for k in range(n_sub):
                ext_ref[SUBLANES - 2:SUBLANES, :] = state_ref[seq0 + k]
                lo = k * sample_seq_len
                conv_block(u[lo:lo + sample_seq_len], gate_b[lo:lo + sample_seq_len], lo, sample_seq_len)

        zk = project(lat_w, conv0)
        zk_swapped = jnp.concatenate([zk[:, ROPE_DIM // 2:], zk[:, :ROPE_DIM // 2]], axis=1)
        k_rope = zk * cos_ref[...] + zk_swapped * sin_ref[...]
        if is_prompt:
            krt_ref[...] = k_rope.T
        else:
            for k in range(TM // sample_seq_len):
                krt_ref[k] = k_rope[k * sample_seq_len:(k + 1) * sample_seq_len, :].T
        ckv_ref[...] = _rms(project(Q_LORA, lat_w), gkv_ref[...])
        cqn_ref[...] = _rms(project(0, Q_LORA), gq_ref[...]).astype(BF16)

    @pl.when(i < n_prompt_tiles)
    def _():
        tile(xp_ref, True)

    @pl.when(i >= n_prompt_tiles)
    def _():
        tile(xs_ref, False)


def _in_proj(xp, xs, gmix, w_t, gq, gkv, gco, convw, cosk, sink, state, *, seq_p, seq_s):
    np_rows, ns_rows = xp.shape[0], xs.shape[0]
    m = np_rows + ns_rows
    npt, nst = np_rows // TM, ns_rows // TM
    tps = seq_p // TM
    n_prompt_seq = np_rows // seq_p
    last_p = npt - 1

    def tab_idx(i):
        return (jnp.where(i < npt, i % tps, tps), 0)

    row = lambda i: (i, 0)
    prow = lambda i: (jnp.minimum(i, last_p), 0)
    srow = lambda i: (jnp.maximum(i - npt, 0), 0)
    kern = functools.partial(_in_proj_kernel, n_prompt_tiles=npt, tiles_per_seq=tps,
                             n_prompt_seq=n_prompt_seq, sample_seq_len=seq_s)
    return pl.pallas_call(
        kern,
        grid=(npt + nst,),
        in_specs=[
            pl.BlockSpec((TM, D_MODEL), prow),
            pl.BlockSpec((TM, D_MODEL), srow),
            _const_spec((1, D_MODEL)),
            pl.BlockSpec(memory_space=pl.ANY),
            _const_spec((1, Q_LORA)),
            _const_spec((1, KV_LORA)),
            _const_spec((1, CONV_CH)),
            _const_spec((CONV_W, CONV_CH)),
            pl.BlockSpec((TM, ROPE_DIM), tab_idx),
            pl.BlockSpec((TM, ROPE_DIM), tab_idx),
            _const_spec(state.shape),
        ],
        out_specs=[
            pl.BlockSpec((TM, Q_LORA), row),
            pl.BlockSpec((TM, KV_LORA), prow),
            pl.BlockSpec((None, ROPE_DIM, TM), lambda i: (jnp.minimum(i, last_p) // tps, 0,
                                                          jnp.minimum(i, last_p) % tps)),
            pl.BlockSpec((TM, KV_LORA), srow),
            pl.BlockSpec((TM // seq_s, ROPE_DIM, seq_s), lambda i: (jnp.maximum(i - npt, 0), 0, 0)),
            pl.BlockSpec((TM, CONV_CH), row),
            pl.BlockSpec((TM // CHUNK, SUBLANES, CONV_CH), lambda i: (i, 0, 0)),
        ],
        out_shape=[
            jax.ShapeDtypeStruct((m, Q_LORA), BF16),
            jax.ShapeDtypeStruct((np_rows, KV_LORA), F32),
            jax.ShapeDtypeStruct((n_prompt_seq, ROPE_DIM, seq_p), F32),
            jax.ShapeDtypeStruct((ns_rows, KV_LORA), F32),
            jax.ShapeDtypeStruct((ns_rows // seq_s, ROPE_DIM, seq_s), F32),
            jax.ShapeDtypeStruct((m, CONV_CH), BF16),
            jax.ShapeDtypeStruct((m // CHUNK, SUBLANES, CONV_CH), F32),
        ],
        scratch_shapes=[pltpu.VMEM((TM + SUBLANES, CONV_CH), F32),
                        pltpu.VMEM(w_t.shape, BF16),
                        pltpu.VMEM((2, w_t.shape[0] // W_IN_SLABS, w_t.shape[1]), F32),
                        pltpu.SemaphoreType.DMA((2,))],
        compiler_params=pltpu.CompilerParams(dimension_semantics=("arbitrary",),
                                             vmem_limit_bytes=VMEM_LIMIT),
        name="in_proj",
    )(xp, xs, gmix, w_t, gq, gkv, gco, convw, cosk, sink, state)


def _attn_kernel(cqn_ref, wq_ref, wuk_ref, wuv_ref, cos_ref, sin_ref, gao_ref, pkv_ref, pkr_ref, kv_ref, kr_ref,
                 out_ref, qlat_ref, qr_ref, m_ref, l_ref, acc_ref, s_ref, *, tq, n_past):
    rows = N_HEADS * tq

    q = jnp.dot(cqn_ref[...], wq_ref[...], preferred_element_type=F32)
    nope_w = N_HEADS * QK_NOPE
    rope_w = N_HEADS * ROPE_DIM
    qrope = q[:, nope_w:nope_w + rope_w] * cos_ref[...] + q[:, nope_w + rope_w:] * sin_ref[...]
    for h in range(N_HEADS):
        qn = q[:, h * QK_NOPE:(h + 1) * QK_NOPE].astype(BF16)
        ql = jnp.dot(qn, wuk_ref[h], preferred_element_type=F32)
        qlat_ref[h * tq:(h + 1) * tq, :] = (ql * EXP2_SCALE).astype(BF16)
        qr_ref[h * tq:(h + 1) * tq, :] = (qrope[:, h * ROPE_DIM:(h + 1) * ROPE_DIM] * EXP2_SCALE).astype(BF16)


    nt = (((1,), (1,)), ((), ()))

    def scores(kc_f32, krt_f32):
        s = lax.dot_general(qlat_ref[...], kc_f32.astype(BF16), nt, preferred_element_type=F32)
        return s + jnp.dot(qr_ref[...], krt_f32.astype(BF16), preferred_element_type=F32)

    def update(s, kc_f32, mask, first=False):
        if mask is not None:
            s = jnp.where(mask, s, NEG_BIG)
        m_cur = jnp.max(s, axis=-1, keepdims=True)
        if first:
            m_new = jnp.broadcast_to(m_cur, m_ref.shape)
        else:
            m_prev = m_ref[...]
            m_new = jnp.maximum(m_prev, m_cur)
            alpha = jnp.exp2(m_prev - m_new)
        p = jnp.exp2(s - _lane_bcast(m_new, s.shape[1]))
        l_cur = jnp.sum(p, axis=-1, keepdims=True)
        pv = jnp.dot(p.astype(BF16), kc_f32.astype(BF16), preferred_element_type=F32)
        if first:
            l_ref[...] = jnp.broadcast_to(l_cur, l_ref.shape)
            acc_ref[...] = pv
        else:
            l_ref[...] = alpha * l_ref[...] + l_cur
            acc_ref[...] = _lane_bcast(alpha, KV_LORA) * acc_ref[...] + pv
        m_ref[...] = m_new

    def pipelined(kv, kr, lo, hi, last, mask_fn):
        def body(j, c):
            k0 = pl.multiple_of(j * TK, TK)
            k1 = pl.multiple_of(jnp.minimum(j + 1, last) * TK, TK)
            s_cur = s_ref[j % 2]
            s_ref[(j + 1) % 2] = scores(kv[pl.ds(k1, TK), :], kr[:, pl.ds(k1, TK)])
            update(s_cur, kv[pl.ds(k0, TK), :], None if mask_fn is None else mask_fn(k0))
            return c
        lax.fori_loop(lo, hi, body, 0)

    def pipelined_pairs(kv, kr, n_pairs, last):
        def body(i, c):
            ka = pl.multiple_of((2 * i + 1) * TK, TK)
            kb = pl.multiple_of((2 * i + 2) * TK, TK)
            kc = pl.multiple_of(jnp.minimum(2 * i + 3, last) * TK, TK)
            s_ref[0] = scores(kv[pl.ds(kb, TK), :], kr[:, pl.ds(kb, TK)])
            update(s_ref[1], kv[pl.ds(ka, TK), :], None)
            s_ref[1] = scores(kv[pl.ds(kc, TK), :], kr[:, pl.ds(kc, TK)])
            update(s_ref[0], kv[pl.ds(kb, TK), :], None)
            return c
        lax.fori_loop(0, n_pairs, body, 0)

    def first_block(kv, kr, last, mask):
        k1 = pl.multiple_of(jnp.minimum(1, last) * TK, TK)
        s_ref[0] = scores(kv[pl.ds(0, TK), :], kr[:, pl.ds(0, TK)])
        s_ref[1] = scores(kv[pl.ds(k1, TK), :], kr[:, pl.ds(k1, TK)])
        update(s_ref[0], kv[pl.ds(0, TK), :], mask, first=True)

    n_pb = n_past // TK
    first_block(pkv_ref, pkr_ref, n_pb - 1, None)
    n_pairs = (n_pb - 1) // 2
    pipelined_pairs(pkv_ref, pkr_ref, n_pairs, n_pb - 1)
    if 1 + 2 * n_pairs < n_pb:
        pipelined(pkv_ref, pkr_ref, 1 + 2 * n_pairs, n_pb, n_pb - 1, None)

    update(scores(kv_ref[...], kr_ref[...]), kv_ref[...], None)

    o = acc_ref[...] / _lane_bcast(l_ref[...], KV_LORA)
    parts = []
    for h in range(N_HEADS):
        oh = o[h * tq:(h + 1) * tq, :].astype(BF16)
        parts.append(jnp.dot(oh, wuv_ref[h], preferred_element_type=F32))
    attn = jnp.concatenate(parts, axis=-1)
    out_ref[...] = _rms(attn, gao_ref[...]).astype(BF16)


def _attention(cqn, w_q, w_ukt, w_uv, cosq, sinq, gao, ckv, krope, past_kv, past_kr, *, n_batch, seq, row0):
    tq = seq
    n_past = past_kv.shape[1]
    assert n_past % CHUNK == 0 and seq <= CHUNK and n_past % TK == 0
    blk0 = row0 // tq
    in_specs = [
        pl.BlockSpec((tq, Q_LORA), lambda b, q: (blk0 + b, 0)),
        _const_spec(w_q.shape),
        _const_spec(w_ukt.shape),
        _const_spec(w_uv.shape),
        pl.BlockSpec((tq, N_HEADS * ROPE_DIM), lambda b, q: (0, 0)),
        pl.BlockSpec((tq, N_HEADS * ROPE_DIM), lambda b, q: (0, 0)),
        _const_spec((1, ATTN_W)),
        pl.BlockSpec((None, n_past, KV_LORA), lambda b, q: (b, 0, 0)),
        pl.BlockSpec((None, ROPE_DIM, n_past), lambda b, q: (b, 0, 0)),
        pl.BlockSpec((seq, KV_LORA), lambda b, q: (b, 0)),
        pl.BlockSpec((None, ROPE_DIM, seq), lambda b, q: (b, 0, 0)),
    ]
    args = [cqn, w_q, w_ukt, w_uv, cosq, sinq, gao, past_kv, past_kr, ckv, krope]
    rows = N_HEADS * tq
    return pl.pallas_call(
        functools.partial(_attn_kernel, tq=tq, n_past=n_past),
        grid=(n_batch, 1),
        in_specs=in_specs,
        out_specs=pl.BlockSpec((tq, ATTN_W), lambda b, q: (b, 0)),
        out_shape=jax.ShapeDtypeStruct((n_batch * seq, ATTN_W), BF16),
        scratch_shapes=[
            pltpu.VMEM((rows, KV_LORA), BF16),
            pltpu.VMEM((rows, ROPE_DIM), BF16),
            pltpu.VMEM((rows, LANES), F32),
            pltpu.VMEM((rows, LANES), F32),
            pltpu.VMEM((rows, KV_LORA), F32),
            pltpu.VMEM((2, rows, TK), F32),
        ],
        compiler_params=pltpu.CompilerParams(dimension_semantics=("arbitrary", "arbitrary"),
                                             vmem_limit_bytes=VMEM_LIMIT),
        name="attn_sample",
    )(*args)


def _attn_heads_kernel(cqn_ref, wq_ref, wkv_ref, cos_ref, sin_ref, gao_ref, kv_ref, krt_ref, out_ref,
                       kcat_ref, vh_ref, qcat_ref, m_ref, l_ref, acc_ref, klim_ref, *, tq, seq):
    qi = pl.program_id(1)
    nt = (((1,), (1,)), ((), ()))
    kw = QK_NOPE + ROPE_DIM
    kpad = kcat_ref.shape[-1]

    @pl.when(qi == 0)
    def _():
        def expand(j, c):
            k0 = pl.multiple_of(j * TKX, TKX)
            latent = kv_ref[pl.ds(k0, TKX), :].astype(BF16)
            kvh = jnp.dot(latent, wkv_ref[...], preferred_element_type=F32)
            k_rope = krt_ref[:, pl.ds(k0, TKX)].T.astype(BF16)
            for h in range(N_HEADS):
                kcat_ref[h, pl.ds(k0, TKX), :QK_NOPE] = kvh[:, h * QK_NOPE:(h + 1) * QK_NOPE].astype(BF16)
                kcat_ref[h, pl.ds(k0, TKX), QK_NOPE:kw] = k_rope
                kcat_ref[h, pl.ds(k0, TKX), kw:] = jnp.zeros((TKX, kpad - kw), BF16)
                v0 = N_HEADS * QK_NOPE + h * V_DIM
                vh_ref[h, pl.ds(k0, TKX), :] = kvh[:, v0:v0 + V_DIM].astype(BF16)
            return c
        lax.fori_loop(0, seq // TKX, expand, 0)

    q = jnp.dot(cqn_ref[...], wq_ref[...], preferred_element_type=F32)
    nope_w = N_HEADS * QK_NOPE
    rope_w = N_HEADS * ROPE_DIM
    qrope = q[:, nope_w:nope_w + rope_w] * cos_ref[...] + q[:, nope_w + rope_w:] * sin_ref[...]
    for h in range(N_HEADS):
        qcat_ref[h, :, :QK_NOPE] = (q[:, h * QK_NOPE:(h + 1) * QK_NOPE] * EXP2_SCALE).astype(BF16)
        qcat_ref[h, :, QK_NOPE:kw] = (qrope[:, h * ROPE_DIM:(h + 1) * ROPE_DIM] * EXP2_SCALE).astype(BF16)
        qcat_ref[h, :, kw:] = jnp.zeros((tq, kpad - kw), BF16)

    r = lax.broadcasted_iota(jnp.int32, (tq, LANES), 0)
    klim_ref[...] = ((qi * tq + r) & ~(CHUNK - 1)) + CHUNK

    tk = TKH

    def block(k0, masked, first):
        if masked:
            cidx = lax.broadcasted_iota(jnp.int32, (tq, tk), 1)
            mask = cidx < _lane_bcast(klim_ref[...] - k0, tk)
        for h in range(N_HEADS):
            s = lax.dot_general(qcat_ref[h], kcat_ref[h, pl.ds(k0, tk), :], nt, preferred_element_type=F32)
            if masked:
                s = jnp.where(mask, s, NEG_BIG)
            m_cur = jnp.max(s, axis=-1, keepdims=True)
            if first:
                m_new = jnp.broadcast_to(m_cur, (tq, LANES))
            else:
                m_prev = m_ref[h]
                m_new = jnp.maximum(m_prev, m_cur)
                alpha = jnp.exp2(m_prev - m_new)
            p = jnp.exp2(s - _lane_bcast(m_new, tk))
            l_cur = jnp.sum(p, axis=-1, keepdims=True)
            pv = jnp.dot(p.astype(BF16), vh_ref[h, pl.ds(k0, tk), :], preferred_element_type=F32)
            if first:
                l_ref[h] = jnp.broadcast_to(l_cur, (tq, LANES))
                acc_ref[h] = pv
            else:
                l_ref[h] = alpha * l_ref[h] + l_cur
                acc_ref[h] = _lane_bcast(alpha, V_DIM) * acc_ref[h] + pv
            m_ref[h] = m_new

    n_blocks = ((qi + 1) * tq + tk - 1) // tk
    n_full = jnp.minimum((qi * tq // CHUNK + 1) * CHUNK // tk, n_blocks)

    def loop(lo, hi, masked):
        def body(j, c):
            block(pl.multiple_of(j * tk, tk), masked, False)
            return c
        lax.fori_loop(lo, hi, body, 0)

    @pl.when(n_full == 0)
    def _():
        block(0, True, True)

    @pl.when(n_full > 0)
    def _():
        block(0, False, True)

    loop(1, n_full, False)
    loop(jnp.maximum(n_full, 1), n_blocks, True)

    attn = jnp.concatenate([acc_ref[h] / _lane_bcast(l_ref[h], V_DIM) for h in range(N_HEADS)], axis=-1)
    out_ref[...] = _rms(attn, gao_ref[...]).astype(BF16)


def _attention_heads(cqn, w_q, w_kv, cosq, sinq, gao, ckv, krope_t, *, n_batch, seq):
    nq = seq // TQ
    kpad = 2 * LANES
    assert QK_NOPE + ROPE_DIM <= kpad and V_DIM == LANES
    assert seq % TQ == 0 and seq % TKH == 0 and seq % TKX == 0 and TQ % CHUNK == 0 and TKH % CHUNK == 0
    return pl.pallas_call(
        functools.partial(_attn_heads_kernel, tq=TQ, seq=seq),
        grid=(n_batch, nq),
        in_specs=[
            pl.BlockSpec((TQ, Q_LORA), lambda b, q: (b * nq + q, 0)),
            _const_spec(w_q.shape),
            _const_spec(w_kv.shape),
            pl.BlockSpec((TQ, N_HEADS * ROPE_DIM), lambda b, q: (q, 0)),
            pl.BlockSpec((TQ, N_HEADS * ROPE_DIM), lambda b, q: (q, 0)),
            _const_spec((1, ATTN_W)),
            pl.BlockSpec((seq, KV_LORA), lambda b, q: (b, 0)),
            pl.BlockSpec((None, ROPE_DIM, seq), lambda b, q: (b, 0, 0)),
        ],
        out_specs=pl.BlockSpec((TQ, ATTN_W), lambda b, q: (b * nq + q, 0)),
        out_shape=jax.ShapeDtypeStruct((n_batch * seq, ATTN_W), BF16),
        scratch_shapes=[
            pltpu.VMEM((N_HEADS, seq, kpad), BF16),
            pltpu.VMEM((N_HEADS, seq, V_DIM), BF16),
            pltpu.VMEM((N_HEADS, TQ, kpad), BF16),
            pltpu.VMEM((N_HEADS, TQ, LANES), F32),
            pltpu.VMEM((N_HEADS, TQ, LANES), F32),
            pltpu.VMEM((N_HEADS, TQ, V_DIM), F32),
            pltpu.VMEM((TQ, LANES), jnp.int32),
        ],
        compiler_params=pltpu.CompilerParams(dimension_semantics=("arbitrary", "arbitrary"),
                                             vmem_limit_bytes=VMEM_LIMIT),
        name="attn_prompt",
    )(cqn, w_q, w_kv, cosq, sinq, gao, ckv, krope_t)


def _out_proj_kernel(attnp_ref, attns_ref, convn_ref, xp_ref, xs_ref, wo_hbm, gffn_ref, wr_ref,
                     br_ref, h_ref, xpk_ref, mi_ref, mf_ref, cnt_ref, carry_ref, logit_ref,
                     wo_ref, stage_ref, wsems, *, n_prompt_tiles):
    i = pl.program_id(0)

    @pl.when(i == 0)
    def _():
        carry_ref[...] = jnp.zeros(carry_ref.shape, F32)
        logit_ref[...] = jnp.zeros(logit_ref.shape, F32)
        _load_weight_as_bf16(wo_hbm, wo_ref, stage_ref, wsems)

    def tile(x_ref, attn_ref):
        prev_logits = logit_ref[...]
        y = jnp.dot(attn_ref[...], wo_ref[:ATTN_W, :], preferred_element_type=F32)
        y = y + jnp.dot(convn_ref[...], wo_ref[ATTN_W:, :], preferred_element_type=F32)
        h = x_ref[...] + y
        h_ref[...] = h
        xn = _rms(h, gffn_ref[...])

        half = D_MODEL // 2
        xh = xn.astype(BF16)
        xh32 = xh.astype(F32)
        lo = lax.bitcast_convert_type(xh32[:, :half], jnp.uint32)
        hi = lax.bitcast_convert_type(xh32[:, half:], jnp.uint32)
        xpk_ref[...] = (lo >> 16) | (hi & jnp.uint32(0xFFFF0000))

        xl = (xn - xh32).astype(BF16)
        hh_hl = jnp.dot(xh, wr_ref[...], preferred_element_type=F32)
        lh = jnp.dot(xl, wr_ref[:, :LANES], preferred_element_type=F32)
        logit_ref[...] = hh_hl[:, :LANES] + (lh + hh_hl[:, LANES:]) + br_ref[...]

        logits = prev_logits
        counted = (i > 0).astype(F32)
        lane = lax.broadcasted_iota(jnp.int32, (TM, LANES), 1).astype(F32)
        ninf = -jnp.inf
        far = float(LANES)

        def first_argmax(v):
            vmax = jnp.max(v, axis=-1, keepdims=True)
            return vmax, jnp.min(jnp.where(v == vmax, lane, far), axis=-1, keepdims=True)

        gl = jnp.where(lane < N_GROUPS, logits, ninf)
        gmax, gidx = first_argmax(gl)
        g_p = 1.0 / jnp.sum(jnp.exp(gl - gmax), axis=-1, keepdims=True)
        e_lo = N_GROUPS + EXPERTS_PER_GROUP * gidx
        el = jnp.where((lane >= e_lo) & (lane < e_lo + EXPERTS_PER_GROUP), logits, ninf)
        e1max, i1 = first_argmax(el)
        z = jnp.sum(jnp.exp(el - e1max), axis=-1, keepdims=True)
        el2 = jnp.where(lane == i1, ninf, el)
        e2max, i2 = first_argmax(el2)
        p1 = 1.0 / z
        p2 = jnp.exp(e2max - e1max) / z
        den = p1 + p2
        g0 = g_p * p1 / den
        g1 = g_p * p2 / den
        e0 = i1 - N_GROUPS
        e1 = i2 - N_GROUPS

        oh0 = lane == e0
        oh1 = lane == e1
        oh = jnp.where(oh0 | oh1, 1.0, 0.0)
        r = lax.broadcasted_iota(jnp.int32, (TM, TM), 0)
        c = lax.broadcasted_iota(jnp.int32, (TM, TM), 1)
        ltri = jnp.where(r > c, 1.0, 0.0).astype(BF16)
        before = jnp.dot(ltri, oh.astype(BF16), preferred_element_type=F32) + carry_ref[...]
        rank0 = jnp.sum(jnp.where(oh0, before, 0.0), axis=-1, keepdims=True)
        rank1 = jnp.sum(jnp.where(oh1, before, 0.0), axis=-1, keepdims=True)
        total = carry_ref[...] + counted * jnp.sum(oh, axis=0, keepdims=True)
        carry_ref[...] = total
        cnt_ref[...] = jnp.broadcast_to(total, cnt_ref.shape)

        mi = jnp.where(lane == 0, e0, jnp.where(lane == 1, e1, jnp.where(lane == 2, rank0, rank1)))
        mi_ref[...] = jnp.transpose(mi)[:SUBLANES, :].astype(jnp.int32)
        mf_ref[...] = jnp.where(lane == 0, g0, g1)

    @pl.when(i < n_prompt_tiles)
    def _():
        tile(xp_ref, attnp_ref)

    @pl.when(i >= n_prompt_tiles)
    def _():
        tile(xs_ref, attns_ref)


def _out_proj(attn_p, attn_s, conv_n, xp, xs, w_ob, gffn, w_r2, b_r):
    m = conv_n.shape[0]
    npt = xp.shape[0] // TM
    n_tiles = m // TM
    last_p, last_s, last = npt - 1, n_tiles - npt - 1, n_tiles - 1
    row = lambda i: (jnp.minimum(i, last), 0)
    prow = lambda i: (jnp.minimum(i, last_p), 0)
    srow = lambda i: (jnp.clip(i - npt, 0, last_s), 0)
    lag = lambda i: jnp.maximum(i - 1, 0)
    return pl.pallas_call(
        functools.partial(_out_proj_kernel, n_prompt_tiles=npt),
        grid=(n_tiles + 1,),
        in_specs=[
            pl.BlockSpec((TM, ATTN_W), prow),
            pl.BlockSpec((TM, ATTN_W), srow),
            pl.BlockSpec((TM, CONV_CH), row),
            pl.BlockSpec((TM, D_MODEL), prow),
            pl.BlockSpec((TM, D_MODEL), srow),
            pl.BlockSpec(memory_space=pl.ANY),
            _const_spec((1, D_MODEL)),
            _const_spec(w_r2.shape),
            _const_spec((1, LANES)),
        ],
        out_specs=[
            pl.BlockSpec((TM, D_MODEL), row),
            pl.BlockSpec((TM, D_MODEL // 2), row),
            pl.BlockSpec((SUBLANES, TM), lambda i: (0, lag(i))),
            pl.BlockSpec((TM, LANES), lambda i: (lag(i), 0)),
            pl.BlockSpec((SUBLANES, LANES), lambda i: (0, 0)),
        ],
        out_shape=[
            jax.ShapeDtypeStruct((m, D_MODEL), F32),
            jax.ShapeDtypeStruct((m, D_MODEL // 2), jnp.uint32),
            jax.ShapeDtypeStruct((SUBLANES, m), jnp.int32),
            jax.ShapeDtypeStruct((m, LANES), F32),
            jax.ShapeDtypeStruct((SUBLANES, LANES), F32),
        ],
        scratch_shapes=[pltpu.VMEM((1, LANES), F32), pltpu.VMEM((TM, LANES), F32),
                        pltpu.VMEM(w_ob.shape, BF16),
                        pltpu.VMEM((2, w_ob.shape[0] // W_O_SLABS, w_ob.shape[1]), F32),
                        pltpu.SemaphoreType.DMA((2,))],
        compiler_params=pltpu.CompilerParams(dimension_semantics=("arbitrary",),
                                             vmem_limit_bytes=VMEM_LIMIT),
        name="out_proj",
    )(attn_p, attn_s, conv_n, xp, xs, w_ob, gffn, w_r2, b_r)


def _dispatch_kernel(d0_ref, d1_ref, zlo_ref, zn_ref, nu_ref, xpk_ref, xs_hbm, zeros_ref, sems, *, n_blocks):
    i = pl.program_id(0)
    sem = sems.at[0]
    zsem = sems.at[1]

    def zero_fill(act):
        def per_expert(e, c):
            lo = zlo_ref[e]
            n = zn_ref[e]
            head = (-lo) & (SUBLANES - 1)
            for r in range(SUBLANES - 1):
                @pl.when(r < head)
                def _(r=r):
                    act(pltpu.make_async_copy(zeros_ref.at[pl.ds(0, 1)], xs_hbm.at[pl.ds(lo + r, 1)], zsem))
            off = lo + head
            rest = n - head
            size = MOE_BLOCK // 2
            while size >= SUBLANES:
                @pl.when((rest & size) != 0)
                def _(off=off, size=size):
                    dst = xs_hbm.at[pl.ds(pl.multiple_of(off, SUBLANES), size)]
                    act(pltpu.make_async_copy(zeros_ref.at[pl.ds(0, size)], dst, zsem))
                off = off + (rest & size)
                size //= 2
            return c

        def per_block(b, c):
            dst = xs_hbm.at[pl.ds(pl.multiple_of(b * MOE_BLOCK, MOE_BLOCK), MOE_BLOCK)]
            act(pltpu.make_async_copy(zeros_ref, dst, zsem))
            return c

        lax.fori_loop(0, N_EXPERTS, per_expert, 0)
        lax.fori_loop(nu_ref[0], n_blocks, per_block, 0)

    @pl.when(i == 0)
    def _():
        zeros_ref[...] = jnp.zeros(zeros_ref.shape, zeros_ref.dtype)
        zero_fill(lambda cp: cp.start())

    @pl.when(i == pl.num_programs(0) - 1)
    def _():
        zero_fill(lambda cp: cp.wait())

    base = i * TD

    def start(g, c):
        for u in range(SUBLANES):
            r = base + g * SUBLANES + u
            src = xpk_ref.at[g, pl.ds(u, 1)]
            pltpu.make_async_copy(src, xs_hbm.at[pl.ds(d0_ref[r], 1)], sem).start()
            pltpu.make_async_copy(src, xs_hbm.at[pl.ds(d1_ref[r], 1)], sem).start()
        return c

    lax.fori_loop(0, TD // SUBLANES, start, 0)
    for _ in range(2):
        pltpu.make_async_copy(xs_hbm.at[pl.ds(0, TD)], xs_hbm.at[pl.ds(0, TD)], sem).wait()


def _dispatch(dest0, dest1, pad_lo, n_pad, n_used, xpk, n_blocks):
    m = xpk.shape[0]
    grid_spec = pltpu.PrefetchScalarGridSpec(
        num_scalar_prefetch=5,
        grid=(m // TD,),
        in_specs=[pl.BlockSpec((TD // SUBLANES, SUBLANES, D_MODEL // 2), lambda i, *_: (i, 0, 0))],
        out_specs=pl.BlockSpec(memory_space=pl.ANY),
        scratch_shapes=[pltpu.VMEM((MOE_BLOCK, D_MODEL // 2), jnp.uint32),
                        pltpu.SemaphoreType.DMA((2,))],
    )
    return pl.pallas_call(
        functools.partial(_dispatch_kernel, n_blocks=n_blocks),
        grid_spec=grid_spec,
        out_shape=jax.ShapeDtypeStruct((n_blocks * MOE_BLOCK, D_MODEL // 2), jnp.uint32),
        compiler_params=pltpu.CompilerParams(dimension_semantics=("arbitrary",)),
        name="dispatch",
    )(dest0, dest1, pad_lo, n_pad, n_used, xpk.reshape(m // SUBLANES, SUBLANES, D_MODEL // 2))


def _experts_kernel(be_ref, nu_ref, nxt_ref, x_ref, wg_hbm, wu_hbm, wd_hbm, y_ref,
                    sg_ref, su_ref, sd_ref, wgb_ref, wub_ref, wdb_ref, sems):
    b = pl.program_id(0)
    active = b < nu_ref[0]
    new_expert = jnp.logical_or(b == 0, be_ref[b] != be_ref[jnp.maximum(b - 1, 0)])

    def weight_copies(e):
        return (pltpu.make_async_copy(wg_hbm.at[e], sg_ref, sems.at[0]),
                pltpu.make_async_copy(wu_hbm.at[e], su_ref, sems.at[1]),
                pltpu.make_async_copy(wd_hbm.at[e], sd_ref, sems.at[2]))

    @pl.when(b == 0)
    def _():
        for cp in weight_copies(be_ref[0]):
            cp.start()

    @pl.when(jnp.logical_and(active, new_expert))
    def _():
        for cp in weight_copies(be_ref[b]):
            cp.wait()
        wgb_ref[...] = sg_ref[...].astype(BF16)
        wub_ref[...] = su_ref[...].astype(BF16)
        wdb_ref[...] = sd_ref[...].astype(BF16)

        @pl.when(nxt_ref[b] >= 0)
        def _():
            for cp in weight_copies(nxt_ref[b]):
                cp.start()

    @pl.when(active)
    def _():
        half = D_MODEL // 2
        xa, xb = (v.astype(BF16) for v in _unpack_bf16_pairs(x_ref[...]))
        g = jnp.dot(xa, wgb_ref[:half, :], preferred_element_type=F32)
        g = g + jnp.dot(xb, wgb_ref[half:, :], preferred_element_type=F32)
        u = jnp.dot(xa, wub_ref[:half, :], preferred_element_type=F32)
        u = u + jnp.dot(xb, wub_ref[half:, :], preferred_element_type=F32)
        hmid = (g * jax.nn.sigmoid(g)) * u
        y = jnp.dot(hmid.astype(BF16), wdb_ref[...], preferred_element_type=F32)
        y_ref[...] = _pack_bf16_pairs(y)

    @pl.when(b >= nu_ref[0])
    def _():
        y_ref[...] = jnp.zeros(y_ref.shape, y_ref.dtype)


def _experts(block_e, n_used, next_e, x_sorted, w_gate, w_up, w_down):
    p = x_sorted.shape[0]
    nb = p // MOE_BLOCK

    def xrow(b, be, nu, nxt):
        return (jnp.maximum(jnp.minimum(b, nu[0] - 1), 0), 0)

    grid_spec = pltpu.PrefetchScalarGridSpec(
        num_scalar_prefetch=3,
        grid=(nb,),
        in_specs=[
            pl.BlockSpec((MOE_BLOCK, D_MODEL // 2), xrow),
            pl.BlockSpec(memory_space=pl.ANY),
            pl.BlockSpec(memory_space=pl.ANY),
            pl.BlockSpec(memory_space=pl.ANY),
        ],
        out_specs=pl.BlockSpec((MOE_BLOCK, D_MODEL // 2), lambda b, be, nu, nxt: (b, 0)),
        scratch_shapes=[pltpu.VMEM((D_MODEL, D_FF), F32), pltpu.VMEM((D_MODEL, D_FF), F32),
                        pltpu.VMEM((D_FF, D_MODEL), F32),
                        pltpu.VMEM((D_MODEL, D_FF), BF16), pltpu.VMEM((D_MODEL, D_FF), BF16),
                        pltpu.VMEM((D_FF, D_MODEL), BF16),
                        pltpu.SemaphoreType.DMA((3,))],
    )
    return pl.pallas_call(
        _experts_kernel,
        grid_spec=grid_spec,
        out_shape=jax.ShapeDtypeStruct((p, D_MODEL // 2), jnp.uint32),
        compiler_params=pltpu.CompilerParams(dimension_semantics=("arbitrary",),
                                             vmem_limit_bytes=VMEM_LIMIT),
        name="experts",
    )(block_e, n_used, next_e, x_sorted, w_gate, w_up, w_down)


def _combine_kernel(d0_ref, d1_ref, h_ref, mf_ref, gfin_ref, y_hbm, outp_ref, outs_ref, y0_ref, y1_ref, sems,
                    *, n_tiles, n_prompt_tiles):
    i = pl.program_id(0)

    def gather(tile, slot, act):
        base = tile * TM

        def body(g, c):
            for u in range(SUBLANES):
                r = base + g * SUBLANES + u
                act(pltpu.make_async_copy(y_hbm.at[pl.ds(d0_ref[r], 1)], y0_ref.at[slot, g, pl.ds(u, 1)],
                                          sems.at[slot]))
                act(pltpu.make_async_copy(y_hbm.at[pl.ds(d1_ref[r], 1)], y1_ref.at[slot, g, pl.ds(u, 1)],
                                          sems.at[slot]))
            return c
        lax.fori_loop(0, TM // SUBLANES, body, 0)

    @pl.when(i == 0)
    def _():
        gather(0, 0, lambda cp: cp.start())

    @pl.when(i + 1 < n_tiles)
    def _():
        gather(i + 1, (i + 1) % 2, lambda cp: cp.start())

    slot = i % 2
    for _ in range(2):
        pltpu.make_async_copy(y_hbm.at[pl.ds(0, TM)], y_hbm.at[pl.ds(0, TM)], sems.at[slot]).wait()

    def finish(out_ref):
        mf = mf_ref[...]
        g0, g1 = mf[:, :, 0:1], mf[:, :, 1:2]
        half = D_MODEL // 2
        a0, b0 = _unpack_bf16_pairs(y0_ref[slot])
        a1, b1 = _unpack_bf16_pairs(y1_ref[slot])
        o_lo = h_ref[:, :, :half] + (g0 * a0 + g1 * a1)
        o_hi = h_ref[:, :, half:] + (g0 * b0 + g1 * b1)
        sumsq = jnp.sum(o_lo * o_lo, axis=-1, keepdims=True) + jnp.sum(o_hi * o_hi, axis=-1, keepdims=True)
        inv_rms = lax.rsqrt(sumsq / D_MODEL + EPS)
        out_ref[:, :, :half] = o_lo * inv_rms * gfin_ref[:, :, :half]
        out_ref[:, :, half:] = o_hi * inv_rms * gfin_ref[:, :, half:]

    @pl.when(i < n_prompt_tiles)
    def _():
        finish(outp_ref)

    @pl.when(i >= n_prompt_tiles)
    def _():
        finish(outs_ref)


def _combine(dest0, dest1, h, mf, gfin, y_sorted, *, n_prompt_rows):
    m = h.shape[0]
    npt = n_prompt_rows // TM
    tg = TM // SUBLANES
    grouped = lambda a: a.reshape(a.shape[0] // SUBLANES, SUBLANES, a.shape[1])
    grid_spec = pltpu.PrefetchScalarGridSpec(
        num_scalar_prefetch=2,
        grid=(m // TM,),
        in_specs=[
            pl.BlockSpec((tg, SUBLANES, D_MODEL), lambda i, *_: (i, 0, 0)),
            pl.BlockSpec((tg, SUBLANES, LANES), lambda i, *_: (i, 0, 0)),
            pl.BlockSpec((1, 1, D_MODEL), lambda i, *_: (0, 0, 0)),
            pl.BlockSpec(memory_space=pl.ANY),
        ],
        out_specs=[pl.BlockSpec((tg, SUBLANES, D_MODEL), lambda i, *_: (jnp.minimum(i, npt - 1), 0, 0)),
                   pl.BlockSpec((tg, SUBLANES, D_MODEL), lambda i, *_: (jnp.maximum(i - npt, 0), 0, 0))],
        scratch_shapes=[pltpu.VMEM((2, tg, SUBLANES, D_MODEL // 2), jnp.uint32),
                        pltpu.VMEM((2, tg, SUBLANES, D_MODEL // 2), jnp.uint32),
                        pltpu.SemaphoreType.DMA((2,))],
    )
    y_p, y_s = pl.pallas_call(
        functools.partial(_combine_kernel, n_tiles=m // TM, n_prompt_tiles=npt),
        grid_spec=grid_spec,
        out_shape=[jax.ShapeDtypeStruct((n_prompt_rows // SUBLANES, SUBLANES, D_MODEL), F32),
                   jax.ShapeDtypeStruct(((m - n_prompt_rows) // SUBLANES, SUBLANES, D_MODEL), F32)],
        compiler_params=pltpu.CompilerParams(dimension_semantics=("arbitrary",),
                                             vmem_limit_bytes=VMEM_LIMIT),
        name="combine",
    )(dest0, dest1, grouped(h), grouped(mf), gfin.reshape(1, 1, D_MODEL), y_sorted)
    return y_p.reshape(n_prompt_rows, D_MODEL), y_s.reshape(m - n_prompt_rows, D_MODEL)


def _rope_tables(pos):
    f32 = np.float32
    inv = np.power(f32(ROPE_THETA), -np.arange(0, ROPE_DIM, 2, dtype=f32) / f32(ROPE_DIM)).astype(f32)
    ang = (pos.astype(f32)[:, None] * inv[None, :]).astype(f32)
    cos, sin = np.cos(ang).astype(f32), np.sin(ang).astype(f32)
    return np.concatenate([cos, cos], axis=-1), np.concatenate([-sin, sin], axis=-1)


def _swap_halves(w):
    return jnp.concatenate([w[..., ROPE_DIM // 2:], w[..., :ROPE_DIM // 2]], axis=-1)


def kernel(x_prompt, x_sample, cache_kv_latent, cache_k_rope, state_conv, norm_mix, w_in, norm_q, w_uq,
           norm_kv, w_uk, w_uv, conv_w, norm_attn_out, norm_conv_out, w_o, norm_ffn, w_router_group,
           b_router_group, w_router_expert, b_router_expert, w_gate, w_up, w_down, norm_final):
    assert w_in.shape[0] == 1, "single-layer trunk"
    bp, seq_p, _ = x_prompt.shape
    bs, seq_s, _ = x_sample.shape
    past_len = cache_kv_latent.shape[2]
    np_rows, ns_rows = bp * seq_p, bs * seq_s
    m = np_rows + ns_rows
    assert seq_p % TM == 0 and TM % seq_s == 0 and ns_rows % TM == 0 and seq_s == CHUNK
    assert m % TD == 0

    xp = x_prompt.reshape(np_rows, D_MODEL)
    xs = x_sample.reshape(ns_rows, D_MODEL)
    row_vec = lambda v: v.reshape(1, -1)

    assert w_in.shape[2] == Q_LORA + KV_LORA + ROPE_DIM + 3 * CONV_CH
    w_t = jnp.swapaxes(w_in[0], 0, 1)
    assert w_t.shape[0] % (W_IN_SLABS * 2 * SUBLANES) == 0
    wq4 = w_uq[0].reshape(Q_LORA, N_HEADS, QK_NOPE + ROPE_DIM)
    wq_rope = wq4[:, :, QK_NOPE:]
    w_q = jnp.concatenate([wq4[:, :, :QK_NOPE].reshape(Q_LORA, -1), wq_rope.reshape(Q_LORA, -1),
                           _swap_halves(wq_rope).reshape(Q_LORA, -1)], axis=1).astype(BF16)
    w_ukt = jnp.transpose(w_uk[0], (1, 2, 0)).astype(BF16)
    w_uvh = jnp.transpose(w_uv[0], (1, 0, 2)).astype(BF16)
    w_ob = w_o[0]
    n_router = N_GROUPS + N_EXPERTS
    w_r = jnp.concatenate([w_router_group[0], w_router_expert[0].reshape(D_MODEL, N_EXPERTS)], axis=1)
    w_r = jnp.pad(w_r, ((0, 0), (0, LANES - n_router)))
    w_rh = w_r.astype(BF16)
    w_rl = (w_r - w_rh.astype(F32)).astype(BF16)
    w_r2 = jnp.concatenate([w_rh, w_rl], axis=1)
    b_r =jnp.pad(jnp.concatenate([b_router_group[0], b_router_expert[0].reshape(N_EXPERTS)]),
                  (0, LANES - n_router)).reshape(1, LANES)

    cos_p, sin_p = _rope_tables(np.arange(seq_p))
    cos_s, sin_s = _rope_tables(past_len + np.arange(seq_s))
    cosk = np.concatenate([cos_p, np.tile(cos_s, (TM // seq_s, 1))], axis=0)
    sink = np.concatenate([sin_p, np.tile(sin_s, (TM // seq_s, 1))], axis=0)
    state = jnp.concatenate([jnp.zeros((bp, CONV_W - 1, CONV_CH), F32), state_conv[0]], axis=0)

    cqn, ckv_p, kr_p, ckv_s, kr_s, conv_n, utail = _in_proj(
        xp, xs, row_vec(norm_mix[0]), w_t, row_vec(norm_q[0]), row_vec(norm_kv[0]),
        row_vec(norm_conv_out[0]), conv_w[0], cosk, sink, state, seq_p=seq_p, seq_s=seq_s)

    gao = row_vec(norm_attn_out[0])
    w_kv = jnp.concatenate([w_uk[0].reshape(KV_LORA, N_HEADS * QK_NOPE),
                            w_uv[0].reshape(KV_LORA, N_HEADS * V_DIM)], axis=1).astype(BF16)
    attn_p = _attention_heads(cqn, w_q, w_kv, np.tile(cos_p, (1, N_HEADS)), np.tile(sin_p, (1, N_HEADS)),
                              gao, ckv_p, kr_p, n_batch=bp, seq=seq_p)
    attn_s = _attention(cqn, w_q, w_ukt, w_uvh, np.tile(cos_s, (1, N_HEADS)), np.tile(sin_s, (1, N_HEADS)),
                        gao, ckv_s, kr_s, cache_kv_latent[0], jnp.swapaxes(cache_k_rope[0], 1, 2),
                        n_batch=bs, seq=seq_s, row0=np_rows)

    h, xpk, mi, mf, cnt = _out_proj(attn_p, attn_s, conv_n, xp, xs, w_ob, row_vec(norm_ffn[0]),
                                    w_r2, b_r)

    counts = cnt[0, :N_EXPERTS].astype(jnp.int32)
    padded = (counts + MOE_BLOCK - 1) // MOE_BLOCK * MOE_BLOCK
    pad_end = jnp.cumsum(padded)
    pad_start = pad_end - padded
    n_blocks = -(-(m * 2) // MOE_BLOCK) + N_EXPERTS
    block_row0 = jnp.arange(n_blocks, dtype=jnp.int32) * MOE_BLOCK
    block_e = jnp.minimum(jnp.sum((pad_end[None, :] <= block_row0[:, None]).astype(jnp.int32), axis=1),
                          N_EXPERTS - 1)
    n_used = (pad_end[-1:] // MOE_BLOCK).astype(jnp.int32)
    expert_ids = jnp.arange(N_EXPERTS, dtype=jnp.int32)[:, None]

    def seg_start(e):
        return jnp.sum(jnp.where(expert_ids == e[None, :], pad_start[:, None], 0), axis=0)

    dest0 = seg_start(mi[0]) + mi[2]
    dest1 = seg_start(mi[1]) + mi[3]

    x_sorted = _dispatch(dest0, dest1, pad_start + counts, padded - counts, n_used, xpk, n_blocks)
    later = (expert_ids.T > block_e[:, None]) & (padded > 0)[None, :]
    next_e = jnp.min(jnp.where(later, expert_ids.T, N_EXPERTS), axis=1)
    next_e = jnp.where(next_e == N_EXPERTS, -1, next_e).astype(jnp.int32)
    y_sorted = _experts(block_e, n_used, next_e, x_sorted, w_gate[0], w_up[0], w_down[0])
    gfin = row_vec(norm_final)
    y_p, y_s = _combine(dest0, dest1, h, mf, gfin, y_sorted, n_prompt_rows=np_rows)

    ut = utail.reshape(m // CHUNK, SUBLANES, CONV_CH)
    tails = ut[:, SUBLANES - (CONV_W - 1):, :]
    p_last = (jnp.arange(bp) + 1) * (seq_p // CHUNK) - 1
    s_last = np_rows // CHUNK + (jnp.arange(bs) + 1) * (seq_s // CHUNK) - 1
    return (y_p.reshape(bp, seq_p, D_MODEL),
            y_s.reshape(bs, seq_s, D_MODEL),
            ckv_p.reshape(1, bp, seq_p, KV_LORA),
            jnp.swapaxes(kr_p, 1, 2)[None],
            tails[p_last][None],
            ckv_s.reshape(1, bs, seq_s, KV_LORA),
            jnp.swapaxes(kr_s, 1, 2)[None],
            tails[s_last][None])
```

```python
import functools

import jax
import jax.numpy as jnp
import numpy as np
from jax import lax
from jax.experimental import pallas as pl
from jax.experimental.pallas import tpu as pltpu

F32 = jnp.float32
BF16 = jnp.bfloat16

D_MODEL = 2048
N_HEADS = 8
QK_NOPE = 128
ROPE_DIM = 64
V_DIM = 128
Q_LORA = 512
KV_LORA = 512
ATTN_W = N_HEADS * V_DIM
CONV_CH = D_MODEL - ATTN_W
CONV_W = 3
CHUNK = 64
N_GROUPS = 4
EXPERTS_PER_GROUP = 8
N_EXPERTS = N_GROUPS * EXPERTS_PER_GROUP
D_FF = 512
ROPE_THETA = 10000.0
EPS = 1e-6
ATTN_SCALE = (QK_NOPE + ROPE_DIM) ** -0.5
EXP2_SCALE = ATTN_SCALE * 1.4426950408889634

LANES = 128
SUBLANES = 8
TM = 256
TD = 2304
MOE_BLOCK = 256
TQ = 512
TK = 512
TKH = 512
TKX = 512
W_O_SLABS = 4
W_IN_SLABS = 10
NEG_BIG = -1e30
V7X_VMEM_BYTES = 64 * 1024 * 1024
VMEM_LIMIT = V7X_VMEM_BYTES * 7 // 8


def _rms(v, g):
    return v * lax.rsqrt(jnp.mean(v * v, axis=-1, keepdims=True) + EPS) * g


def _lane_bcast(v, width):
    if width % LANES == 0:
        return jnp.concatenate([v] * (width // LANES), axis=1)
    assert width < LANES
    return v[:, :width]


def _pack_bf16_pairs(v):
    half = v.shape[-1] // 2
    lo = lax.bitcast_convert_type(v[..., :half].astype(BF16).astype(F32), jnp.uint32)
    hi = lax.bitcast_convert_type(v[..., half:].astype(BF16).astype(F32), jnp.uint32)
    return (lo >> 16) | (hi & jnp.uint32(0xFFFF0000))


def _unpack_bf16_pairs(w):
    return (lax.bitcast_convert_type(w << 16, F32),
            lax.bitcast_convert_type(w & jnp.uint32(0xFFFF0000), F32))


def _load_weight_as_bf16(w_hbm, dst_ref, stage_ref, sems):
    rows = stage_ref.shape[1]
    n_slabs = dst_ref.shape[0] // rows

    def slab_copy(c):
        return pltpu.make_async_copy(w_hbm.at[pl.ds(c * rows, rows)], stage_ref.at[c % 2], sems.at[c % 2])

    slab_copy(0).start()
    for c in range(n_slabs):
        if c + 1 < n_slabs:
            slab_copy(c + 1).start()
        slab_copy(c).wait()
        dst_ref[c * rows:(c + 1) * rows, :] = stage_ref[c % 2].astype(BF16)


def _const_spec(shape):
    nd = len(shape)
    return pl.BlockSpec(shape, lambda *_: (0,) * nd, pipeline_mode=pl.Buffered(1))


def _in_proj_kernel(xp_ref, xs_ref, gmix_ref, w_hbm, gq_ref, gkv_ref, gco_ref, convw_ref,
                    cos_ref, sin_ref, state_ref,
                    cqn_ref, ckvp_ref, krp_ref, ckvs_ref, krs_ref, convn_ref, utail_ref,
                    ext_ref, wt_ref, stage_ref, wsems,
                    *, n_prompt_tiles, tiles_per_seq, n_prompt_seq, sample_seq_len):
    i = pl.program_id(0)

    @pl.when(i == 0)
    def _():
        ext_ref[...] = jnp.zeros(ext_ref.shape, F32)
        _load_weight_as_bf16(w_hbm, wt_ref, stage_ref, wsems)

    def conv_block(u_sub, gate_sub, row0, length):
        ext_ref[SUBLANES:SUBLANES + length, :] = u_sub
        um1 = ext_ref[SUBLANES - 1:SUBLANES - 1 + length, :]
        um2 = ext_ref[SUBLANES - 2:SUBLANES - 2 + length, :]
        cw = convw_ref[...]
        conv = cw[0:1] * um2 + cw[1:2] * um1 + cw[2:3] * u_sub
        convn_ref[row0:row0 + length, :] = _rms(gate_sub * conv, gco_ref[...]).astype(BF16)

    def tile(x_ref, is_prompt):
        ckv_ref, krt_ref = (ckvp_ref, krp_ref) if is_prompt else (ckvs_ref, krs_ref)
        x = x_ref[...]
        xg = (x * gmix_ref[...]).astype(BF16)
        inv_rms = lax.rsqrt(jnp.mean(x * x, axis=-1, keepdims=True) + EPS)
        lat_w = Q_LORA + KV_LORA
        conv0 = lat_w + ROPE_DIM
        nt = (((1,), (1,)), ((), ()))

        def project(lo, hi):
            return inv_rms * lax.dot_general(xg, wt_ref[lo:hi, :], nt, preferred_element_type=F32)

        z_ch = project(conv0 + CONV_CH, conv0 + 3 * CONV_CH)
        u = z_ch[:, :CONV_CH] * z_ch[:, CONV_CH:]
        for j in range(TM // CHUNK):
            utail_ref[j] = u[CHUNK * (j + 1) - SUBLANES:CHUNK * (j + 1), :]
        gate_b = project(conv0, conv0 + CONV_CH)

        if is_prompt:
            first = (i % tiles_per_seq) == 0
            carried = ext_ref[TM + SUBLANES - 2:TM + SUBLANES, :]
            ext_ref[SUBLANES - 2:SUBLANES, :] = jnp.where(first, state_ref[i // tiles_per_seq], carried)
            conv_block(u, gate_b, 0, TM)
        else:
            n_sub = TM // sample_seq_len
            seq0 = n_prompt_seq + (i - n_prompt_tiles) * n_sub
            for k in range(n_sub):
                ext_ref[SUBLANES - 2:SUBLANES, :] = state_ref[seq0 + k]
                lo = k * sample_seq_len
                conv_block(u[lo:lo + sample_seq_len], gate_b[lo:lo + sample_seq_len], lo, sample_seq_len)

        zk = project(lat_w, conv0)
        zk_swapped = jnp.concatenate([zk[:, ROPE_DIM // 2:], zk[:, :ROPE_DIM // 2]], axis=1)
        k_rope = zk * cos_ref[...] + zk_swapped * sin_ref[...]
        if is_prompt:
            krt_ref[...] = k_rope.T
        else:
            for k in range(TM // sample_seq_len):
                krt_ref[k] = k_rope[k * sample_seq_len:(k + 1) * sample_seq_len, :].T
        ckv_ref[...] = _rms(project(Q_LORA, lat_w), gkv_ref[...])
        cqn_ref[...] = _rms(project(0, Q_LORA), gq_ref[...]).astype(BF16)

    @pl.when(i < n_prompt_tiles)
    def _():
        tile(xp_ref, True)

    @pl.when(i >= n_prompt_tiles)
    def _():
        tile(xs_ref, False)


def _in_proj(xp, xs, gmix, w_t, gq, gkv, gco, convw, cosk, sink, state, *, seq_p, seq_s):
    np_rows, ns_rows = xp.shape[0], xs.shape[0]
    m = np_rows + ns_rows
    npt, nst = np_rows // TM, ns_rows // TM
    tps = seq_p // TM
    n_prompt_seq = np_rows // seq_p
    last_p = npt - 1

    def tab_idx(i):
        return (jnp.where(i < npt, i % tps, tps), 0)

    row = lambda i: (i, 0)
    prow = lambda i: (jnp.minimum(i, last_p), 0)
    srow = lambda i: (jnp.maximum(i - npt, 0), 0)
    kern = functools.partial(_in_proj_kernel, n_prompt_tiles=npt, tiles_per_seq=tps,
                             n_prompt_seq=n_prompt_seq, sample_seq_len=seq_s)
    return pl.pallas_call(
        kern,
        grid=(npt + nst,),
        in_specs=[
            pl.BlockSpec((TM, D_MODEL), prow),
            pl.BlockSpec((TM, D_MODEL), srow),
            _const_spec((1, D_MODEL)),
            pl.BlockSpec(memory_space=pl.ANY),
            _const_spec((1, Q_LORA)),
            _const_spec((1, KV_LORA)),
            _const_spec((1, CONV_CH)),
            _const_spec((CONV_W, CONV_CH)),
            pl.BlockSpec((TM, ROPE_DIM), tab_idx),
            pl.BlockSpec((TM, ROPE_DIM), tab_idx),
            _const_spec(state.shape),
        ],
        out_specs=[
            pl.BlockSpec((TM, Q_LORA), row),
            pl.BlockSpec((TM, KV_LORA), prow),
            pl.BlockSpec((None, ROPE_DIM, TM), lambda i: (jnp.minimum(i, last_p) // tps, 0,
                                                          jnp.minimum(i, last_p) % tps)),
            pl.BlockSpec((TM, KV_LORA), srow),
            pl.BlockSpec((TM // seq_s, ROPE_DIM, seq_s), lambda i: (jnp.maximum(i - npt, 0), 0, 0)),
            pl.BlockSpec((TM, CONV_CH), row),
            pl.BlockSpec((TM // CHUNK, SUBLANES, CONV_CH), lambda i: (i, 0, 0)),
        ],
        out_shape=[
            jax.ShapeDtypeStruct((m, Q_LORA), BF16),
            jax.ShapeDtypeStruct((np_rows, KV_LORA), F32),
            jax.ShapeDtypeStruct((n_prompt_seq, ROPE_DIM, seq_p), F32),
            jax.ShapeDtypeStruct((ns_rows, KV_LORA), F32),
            jax.ShapeDtypeStruct((ns_rows // seq_s, ROPE_DIM, seq_s), F32),
            jax.ShapeDtypeStruct((m, CONV_CH), BF16),
            jax.ShapeDtypeStruct((m // CHUNK, SUBLANES, CONV_CH), F32),
        ],
        scratch_shapes=[pltpu.VMEM((TM + SUBLANES, CONV_CH), F32),
                        pltpu.VMEM(w_t.shape, BF16),
                        pltpu.VMEM((2, w_t.shape[0] // W_IN_SLABS, w_t.shape[1]), F32),
                        pltpu.SemaphoreType.DMA((2,))],
        compiler_params=pltpu.CompilerParams(dimension_semantics=("arbitrary",),
                                             vmem_limit_bytes=VMEM_LIMIT),
        name="in_proj",
    )(xp, xs, gmix, w_t, gq, gkv, gco, convw, cosk, sink, state)


def _attn_kernel(cqn_ref, wq_ref, wuk_ref, wuv_ref, cos_ref, sin_ref, gao_ref, pkv_ref, pkr_ref, kv_ref, kr_ref,
                 out_ref, qlat_ref, qr_ref, m_ref, l_ref, acc_ref, s_ref, *, tq, n_past):
    rows = N_HEADS * tq

    q = jnp.dot(cqn_ref[...], wq_ref[...], preferred_element_type=F32)
    nope_w = N_HEADS * QK_NOPE
    rope_w = N_HEADS * ROPE_DIM
    qrope = q[:, nope_w:nope_w + rope_w] * cos_ref[...] + q[:, nope_w + rope_w:] * sin_ref[...]
    for h in range(N_HEADS):
        qn = q[:, h * QK_NOPE:(h + 1) * QK_NOPE].astype(BF16)
        ql = jnp.dot(qn, wuk_ref[h], preferred_element_type=F32)
        qlat_ref[h * tq:(h + 1) * tq, :] = (ql * EXP2_SCALE).astype(BF16)
        qr_ref[h * tq:(h + 1) * tq, :] = (qrope[:, h * ROPE_DIM:(h + 1) * ROPE_DIM] * EXP2_SCALE).astype(BF16)


    nt = (((1,), (1,)), ((), ()))

    def scores(kc_f32, krt_f32):
        s = lax.dot_general(qlat_ref[...], kc_f32.astype(BF16), nt, preferred_element_type=F32)
        return s + jnp.dot(qr_ref[...], krt_f32.astype(BF16), preferred_element_type=F32)

    def update(s, kc_f32, mask, first=False):
        if mask is not None:
            s = jnp.where(mask, s, NEG_BIG)
        m_cur = jnp.max(s, axis=-1, keepdims=True)
        if first:
            m_new = jnp.broadcast_to(m_cur, m_ref.shape)
        else:
            m_prev = m_ref[...]
            m_new = jnp.maximum(m_prev, m_cur)
            alpha = jnp.exp2(m_prev - m_new)
        p = jnp.exp2(s - _lane_bcast(m_new, s.shape[1]))
        l_cur = jnp.sum(p, axis=-1, keepdims=True)
        pv = jnp.dot(p.astype(BF16), kc_f32.astype(BF16), preferred_element_type=F32)
        if first:
            l_ref[...] = jnp.broadcast_to(l_cur, l_ref.shape)
            acc_ref[...] = pv
        else:
            l_ref[...] = alpha * l_ref[...] + l_cur
            acc_ref[...] = _lane_bcast(alpha, KV_LORA) * acc_ref[...] + pv
        m_ref[...] = m_new

    def pipelined(kv, kr, lo, hi, last, mask_fn):
        def body(j, c):
            k0 = pl.multiple_of(j * TK, TK)
            k1 = pl.multiple_of(jnp.minimum(j + 1, last) * TK, TK)
            s_cur = s_ref[j % 2]
            s_ref[(j + 1) % 2] = scores(kv[pl.ds(k1, TK), :], kr[:, pl.ds(k1, TK)])
            update(s_cur, kv[pl.ds(k0, TK), :], None if mask_fn is None else mask_fn(k0))
            return c
        lax.fori_loop(lo, hi, body, 0)

    def pipelined_pairs(kv, kr, n_pairs, last):
        def body(i, c):
            ka = pl.multiple_of((2 * i + 1) * TK, TK)
            kb = pl.multiple_of((2 * i + 2) * TK, TK)
            kc = pl.multiple_of(jnp.minimum(2 * i + 3, last) * TK, TK)
            s_ref[0] = scores(kv[pl.ds(kb, TK), :], kr[:, pl.ds(kb, TK)])
            update(s_ref[1], kv[pl.ds(ka, TK), :], None)
            s_ref[1] = scores(kv[pl.ds(kc, TK), :], kr[:, pl.ds(kc, TK)])
            update(s_ref[0], kv[pl.ds(kb, TK), :], None)
            return c
        lax.fori_loop(0, n_pairs, body, 0)

    def first_block(kv, kr, last, mask):
        k1 = pl.multiple_of(jnp.minimum(1, last) * TK, TK)
        s_ref[0] = scores(kv[pl.ds(0, TK), :], kr[:, pl.ds(0, TK)])
        s_ref[1] = scores(kv[pl.ds(k1, TK), :], kr[:, pl.ds(k1, TK)])
        update(s_ref[0], kv[pl.ds(0, TK), :], mask, first=True)

    n_pb = n_past // TK
    first_block(pkv_ref, pkr_ref, n_pb - 1, None)
    n_pairs = (n_pb - 1) // 2
    pipelined_pairs(pkv_ref, pkr_ref, n_pairs, n_pb - 1)
    if 1 + 2 * n_pairs < n_pb:
        pipelined(pkv_ref, pkr_ref, 1 + 2 * n_pairs, n_pb, n_pb - 1, None)

    update(scores(kv_ref[...], kr_ref[...]), kv_ref[...], None)

    o = acc_ref[...] / _lane_bcast(l_ref[...], KV_LORA)
    parts = []
    for h in range(N_HEADS):
        oh = o[h * tq:(h + 1) * tq, :].astype(BF16)
        parts.append(jnp.dot(oh, wuv_ref[h], preferred_element_type=F32))
    attn = jnp.concatenate(parts, axis=-1)
    out_ref[...] = _rms(attn, gao_ref[...]).astype(BF16)


def _attention(cqn, w_q, w_ukt, w_uv, cosq, sinq, gao, ckv, krope, past_kv, past_kr, *, n_batch, seq, row0):
    tq = seq
    n_past = past_kv.shape[1]
    assert n_past % CHUNK == 0 and seq <= CHUNK and n_past % TK == 0
    blk0 = row0 // tq
    in_specs = [
        pl.BlockSpec((tq, Q_LORA), lambda b, q: (blk0 + b, 0)),
        _const_spec(w_q.shape),
        _const_spec(w_ukt.shape),
        _const_spec(w_uv.shape),
        pl.BlockSpec((tq, N_HEADS * ROPE_DIM), lambda b, q: (0, 0)),
        pl.BlockSpec((tq, N_HEADS * ROPE_DIM), lambda b, q: (0, 0)),
        _const_spec((1, ATTN_W)),
        pl.BlockSpec((None, n_past, KV_LORA), lambda b, q: (b, 0, 0)),
        pl.BlockSpec((None, ROPE_DIM, n_past), lambda b, q: (b, 0, 0)),
        pl.BlockSpec((seq, KV_LORA), lambda b, q: (b, 0)),
        pl.BlockSpec((None, ROPE_DIM, seq), lambda b, q: (b, 0, 0)),
    ]
    args = [cqn, w_q, w_ukt, w_uv, cosq, sinq, gao, past_kv, past_kr, ckv, krope]
    rows = N_HEADS * tq
    return pl.pallas_call(
        functools.partial(_attn_kernel, tq=tq, n_past=n_past),
        grid=(n_batch, 1),
        in_specs=in_specs,
        out_specs=pl.BlockSpec((tq, ATTN_W), lambda b, q: (b, 0)),
        out_shape=jax.ShapeDtypeStruct((n_batch * seq, ATTN_W), BF16),
        scratch_shapes=[
            pltpu.VMEM((rows, KV_LORA), BF16),
            pltpu.VMEM((rows, ROPE_DIM), BF16),
            pltpu.VMEM((rows, LANES), F32),
            pltpu.VMEM((rows, LANES), F32),
            pltpu.VMEM((rows, KV_LORA), F32),
            pltpu.VMEM((2, rows, TK), F32),
        ],
        compiler_params=pltpu.CompilerParams(dimension_semantics=("arbitrary", "arbitrary"),
                                             vmem_limit_bytes=VMEM_LIMIT),
        name="attn_sample",
    )(*args)


def _attn_heads_kernel(cqn_ref, wq_ref, wkv_ref, cos_ref, sin_ref, gao_ref, kv_ref, krt_ref, out_ref,
                       kcat_ref, vh_ref, qcat_ref, m_ref, l_ref, acc_ref, klim_ref, *, tq, seq):
    qi = pl.program_id(1)
    nt = (((1,), (1,)), ((), ()))
    kw = QK_NOPE + ROPE_DIM
    kpad = kcat_ref.shape[-1]

    @pl.when(qi == 0)
    def _():
        def expand(j, c):
            k0 = pl.multiple_of(j * TKX, TKX)
            latent = kv_ref[pl.ds(k0, TKX), :].astype(BF16)
            kvh = jnp.dot(latent, wkv_ref[...], preferred_element_type=F32)
            k_rope = krt_ref[:, pl.ds(k0, TKX)].T.astype(BF16)
            for h in range(N_HEADS):
                kcat_ref[h, pl.ds(k0, TKX), :QK_NOPE] = kvh[:, h * QK_NOPE:(h + 1) * QK_NOPE].astype(BF16)
                kcat_ref[h, pl.ds(k0, TKX), QK_NOPE:kw] = k_rope
                kcat_ref[h, pl.ds(k0, TKX), kw:] = jnp.zeros((TKX, kpad - kw), BF16)
                v0 = N_HEADS * QK_NOPE + h * V_DIM
                vh_ref[h, pl.ds(k0, TKX), :] = kvh[:, v0:v0 + V_DIM].astype(BF16)
            return c
        lax.fori_loop(0, seq // TKX, expand, 0)

    q = jnp.dot(cqn_ref[...], wq_ref[...], preferred_element_type=F32)
    nope_w = N_HEADS * QK_NOPE
    rope_w = N_HEADS * ROPE_DIM
    qrope = q[:, nope_w:nope_w + rope_w] * cos_ref[...] + q[:, nope_w + rope_w:] * sin_ref[...]
    for h in range(N_HEADS):
        qcat_ref[h, :, :QK_NOPE] = (q[:, h * QK_NOPE:(h + 1) * QK_NOPE] * EXP2_SCALE).astype(BF16)
        qcat_ref[h, :, QK_NOPE:kw] = (qrope[:, h * ROPE_DIM:(h + 1) * ROPE_DIM] * EXP2_SCALE).astype(BF16)
        qcat_ref[h, :, kw:] = jnp.zeros((tq, kpad - kw), BF16)

    r = lax.broadcasted_iota(jnp.int32, (tq, LANES), 0)
    klim_ref[...] = ((qi * tq + r) & ~(CHUNK - 1)) + CHUNK

    tk = TKH

    def block(k0, masked, first):
        if masked:
            cidx = lax.broadcasted_iota(jnp.int32, (tq, tk), 1)
            mask = cidx < _lane_bcast(klim_ref[...] - k0, tk)
        for h in range(N_HEADS):
            s = lax.dot_general(qcat_ref[h], kcat_ref[h, pl.ds(k0, tk), :], nt, preferred_element_type=F32)
            if masked:
                s = jnp.where(mask, s, NEG_BIG)
            m_cur = jnp.max(s, axis=-1, keepdims=True)
            if first:
                m_new = jnp.broadcast_to(m_cur, (tq, LANES))
            else:
                m_prev = m_ref[h]
                m_new = jnp.maximum(m_prev, m_cur)
                alpha = jnp.exp2(m_prev - m_new)
            p = jnp.exp2(s - _lane_bcast(m_new, tk))
            l_cur = jnp.sum(p, axis=-1, keepdims=True)
            pv = jnp.dot(p.astype(BF16), vh_ref[h, pl.ds(k0, tk), :], preferred_element_type=F32)
            if first:
                l_ref[h] = jnp.broadcast_to(l_cur, (tq, LANES))
                acc_ref[h] = pv
            else:
                l_ref[h] = alpha * l_ref[h] + l_cur
                acc_ref[h] = _lane_bcast(alpha, V_DIM) * acc_ref[h] + pv
            m_ref[h] = m_new

    n_blocks = ((qi + 1) * tq + tk - 1) // tk
    n_full = jnp.minimum((qi * tq // CHUNK + 1) * CHUNK // tk, n_blocks)

    def loop(lo, hi, masked):
        def body(j, c):
            block(pl.multiple_of(j * tk, tk), masked, False)
            return c
        lax.fori_loop(lo, hi, body, 0)

    @pl.when(n_full == 0)
    def _():
        block(0, True, True)

    @pl.when(n_full > 0)
    def _():
        block(0, False, True)

    loop(1, n_full, False)
    loop(jnp.maximum(n_full, 1), n_blocks, True)

    attn = jnp.concatenate([acc_ref[h] / _lane_bcast(l_ref[h], V_DIM) for h in range(N_HEADS)], axis=-1)
    out_ref[...] = _rms(attn, gao_ref[...]).astype(BF16)


def _attention_heads(cqn, w_q, w_kv, cosq, sinq, gao, ckv, krope_t, *, n_batch, seq):
    nq = seq // TQ
    kpad = 2 * LANES
    assert QK_NOPE + ROPE_DIM <= kpad and V_DIM == LANES
    assert seq % TQ == 0 and seq % TKH == 0 and seq % TKX == 0 and TQ % CHUNK == 0 and TKH % CHUNK == 0
    return pl.pallas_call(
        functools.partial(_attn_heads_kernel, tq=TQ, seq=seq),
        grid=(n_batch, nq),
        in_specs=[
            pl.BlockSpec((TQ, Q_LORA), lambda b, q: (b * nq + q, 0)),
            _const_spec(w_q.shape),
            _const_spec(w_kv.shape),
            pl.BlockSpec((TQ, N_HEADS * ROPE_DIM), lambda b, q: (q, 0)),
            pl.BlockSpec((TQ, N_HEADS * ROPE_DIM), lambda b, q: (q, 0)),
            _const_spec((1, ATTN_W)),
            pl.BlockSpec((seq, KV_LORA), lambda b, q: (b, 0)),
            pl.BlockSpec((None, ROPE_DIM, seq), lambda b, q: (b, 0, 0)),
        ],
        out_specs=pl.BlockSpec((TQ, ATTN_W), lambda b, q: (b * nq + q, 0)),
        out_shape=jax.ShapeDtypeStruct((n_batch * seq, ATTN_W), BF16),
        scratch_shapes=[
            pltpu.VMEM((N_HEADS, seq, kpad), BF16),
            pltpu.VMEM((N_HEADS, seq, V_DIM), BF16),
            pltpu.VMEM((N_HEADS, TQ, kpad), BF16),
            pltpu.VMEM((N_HEADS, TQ, LANES), F32),
            pltpu.VMEM((N_HEADS, TQ, LANES), F32),
            pltpu.VMEM((N_HEADS, TQ, V_DIM), F32),
            pltpu.VMEM((TQ, LANES), jnp.int32),
        ],
        compiler_params=pltpu.CompilerParams(dimension_semantics=("arbitrary", "arbitrary"),
                                             vmem_limit_bytes=VMEM_LIMIT),
        name="attn_prompt",
    )(cqn, w_q, w_kv, cosq, sinq, gao, ckv, krope_t)


def _out_proj_kernel(attnp_ref, attns_ref, convn_ref, xp_ref, xs_ref, wo_hbm, gffn_ref, wr_ref,
                     br_ref, h_ref, xpk_ref, mi_ref, mf_ref, cnt_ref, carry_ref, logit_ref,
                     wo_ref, stage_ref, wsems, *, n_prompt_tiles):
    i = pl.program_id(0)

    @pl.when(i == 0)
    def _():
        carry_ref[...] = jnp.zeros(carry_ref.shape, F32)
        logit_ref[...] = jnp.zeros(logit_ref.shape, F32)
        _load_weight_as_bf16(wo_hbm, wo_ref, stage_ref, wsems)

    def tile(x_ref, attn_ref):
        prev_logits = logit_ref[...]
        y = jnp.dot(attn_ref[...], wo_ref[:ATTN_W, :], preferred_element_type=F32)
        y = y + jnp.dot(convn_ref[...], wo_ref[ATTN_W:, :], preferred_element_type=F32)
        h = x_ref[...] + y
        h_ref[...] = h
        xn = _rms(h, gffn_ref[...])

        half = D_MODEL // 2
        xh = xn.astype(BF16)
        xh32 = xh.astype(F32)
        lo = lax.bitcast_convert_type(xh32[:, :half], jnp.uint32)
        hi = lax.bitcast_convert_type(xh32[:, half:], jnp.uint32)
        xpk_ref[...] = (lo >> 16) | (hi & jnp.uint32(0xFFFF0000))

        xl = (xn - xh32).astype(BF16)
        hh_hl = jnp.dot(xh, wr_ref[...], preferred_element_type=F32)
        lh = jnp.dot(xl, wr_ref[:, :LANES], preferred_element_type=F32)
        logit_ref[...] = hh_hl[:, :LANES] + (lh + hh_hl[:, LANES:]) + br_ref[...]

        logits = prev_logits
        counted = (i > 0).astype(F32)
        lane = lax.broadcasted_iota(jnp.int32, (TM, LANES), 1).astype(F32)
        ninf = -jnp.inf
        far = float(LANES)

        def first_argmax(v):
            vmax = jnp.max(v, axis=-1, keepdims=True)
            return vmax, jnp.min(jnp.where(v == vmax, lane, far), axis=-1, keepdims=True)

        gl = jnp.where(lane < N_GROUPS, logits, ninf)
        gmax, gidx = first_argmax(gl)
        g_p = 1.0 / jnp.sum(jnp.exp(gl - gmax), axis=-1, keepdims=True)
        e_lo = N_GROUPS + EXPERTS_PER_GROUP * gidx
        el = jnp.where((lane >= e_lo) & (lane < e_lo + EXPERTS_PER_GROUP), logits, ninf)
        e1max, i1 = first_argmax(el)
        z = jnp.sum(jnp.exp(el - e1max), axis=-1, keepdims=True)
        el2 = jnp.where(lane == i1, ninf, el)
        e2max, i2 = first_argmax(el2)
        p1 = 1.0 / z
        p2 = jnp.exp(e2max - e1max) / z
        den = p1 + p2
        g0 = g_p * p1 / den
        g1 = g_p * p2 / den
        e0 = i1 - N_GROUPS
        e1 = i2 - N_GROUPS

        oh0 = lane == e0
        oh1 = lane == e1
        oh = jnp.where(oh0 | oh1, 1.0, 0.0)
        r = lax.broadcasted_iota(jnp.int32, (TM, TM), 0)
        c = lax.broadcasted_iota(jnp.int32, (TM, TM), 1)
        ltri = jnp.where(r > c, 1.0, 0.0).astype(BF16)
        before = jnp.dot(ltri, oh.astype(BF16), preferred_element_type=F32) + carry_ref[...]
        rank0 = jnp.sum(jnp.where(oh0, before, 0.0), axis=-1, keepdims=True)
        rank1 = jnp.sum(jnp.where(oh1, before, 0.0), axis=-1, keepdims=True)
        total = carry_ref[...] + counted * jnp.sum(oh, axis=0, keepdims=True)
        carry_ref[...] = total
        cnt_ref[...] = jnp.broadcast_to(total, cnt_ref.shape)

        mi = jnp.where(lane == 0, e0, jnp.where(lane == 1, e1, jnp.where(lane == 2, rank0, rank1)))
        mi_ref[...] = jnp.transpose(mi)[:SUBLANES, :].astype(jnp.int32)
        mf_ref[...] = jnp.where(lane == 0, g0, g1)

    @pl.when(i < n_prompt_tiles)
    def _():
        tile(xp_ref, attnp_ref)

    @pl.when(i >= n_prompt_tiles)
    def _():
        tile(xs_ref, attns_ref)


def _out_proj(attn_p, attn_s, conv_n, xp, xs, w_ob, gffn, w_r2, b_r):
    m = conv_n.shape[0]
    npt = xp.shape[0] // TM
    n_tiles = m // TM
    last_p, last_s, last = npt - 1, n_tiles - npt - 1, n_tiles - 1
    row = lambda i: (jnp.minimum(i, last), 0)
    prow = lambda i: (jnp.minimum(i, last_p), 0)
    srow = lambda i: (jnp.clip(i - npt, 0, last_s), 0)
    lag = lambda i: jnp.maximum(i - 1, 0)
    return pl.pallas_call(
        functools.partial(_out_proj_kernel, n_prompt_tiles=npt),
        grid=(n_tiles + 1,),
        in_specs=[
            pl.BlockSpec((TM, ATTN_W), prow),
            pl.BlockSpec((TM, ATTN_W), srow),
            pl.BlockSpec((TM, CONV_CH), row),
            pl.BlockSpec((TM, D_MODEL), prow),
            pl.BlockSpec((TM, D_MODEL), srow),
            pl.BlockSpec(memory_space=pl.ANY),
            _const_spec((1, D_MODEL)),
            _const_spec(w_r2.shape),
            _const_spec((1, LANES)),
        ],
        out_specs=[
            pl.BlockSpec((TM, D_MODEL), row),
            pl.BlockSpec((TM, D_MODEL // 2), row),
            pl.BlockSpec((SUBLANES, TM), lambda i: (0, lag(i))),
            pl.BlockSpec((TM, LANES), lambda i: (lag(i), 0)),
            pl.BlockSpec((SUBLANES, LANES), lambda i: (0, 0)),
        ],
        out_shape=[
            jax.ShapeDtypeStruct((m, D_MODEL), F32),
            jax.ShapeDtypeStruct((m, D_MODEL // 2), jnp.uint32),
            jax.ShapeDtypeStruct((SUBLANES, m), jnp.int32),
            jax.ShapeDtypeStruct((m, LANES), F32),
            jax.ShapeDtypeStruct((SUBLANES, LANES), F32),
        ],
        scratch_shapes=[pltpu.VMEM((1, LANES), F32), pltpu.VMEM((TM, LANES), F32),
                        pltpu.VMEM(w_ob.shape, BF16),
                        pltpu.VMEM((2, w_ob.shape[0] // W_O_SLABS, w_ob.shape[1]), F32),
                        pltpu.SemaphoreType.DMA((2,))],
        compiler_params=pltpu.CompilerParams(dimension_semantics=("arbitrary",),
                                             vmem_limit_bytes=VMEM_LIMIT),
        name="out_proj",
    )(attn_p, attn_s, conv_n, xp, xs, w_ob, gffn, w_r2, b_r)


def _dispatch_kernel(d0_ref, d1_ref, zlo_ref, zn_ref, nu_ref, xpk_ref, xs_hbm, zeros_ref, sems, *, n_blocks):
    i = pl.program_id(0)
    sem = sems.at[0]
    zsem = sems.at[1]

    def zero_fill(act):
        def per_expert(e, c):
            lo = zlo_ref[e]
            n = zn_ref[e]
            head = (-lo) & (SUBLANES - 1)
            for r in range(SUBLANES - 1):
                @pl.when(r < head)
                def _(r=r):
                    act(pltpu.make_async_copy(zeros_ref.at[pl.ds(0, 1)], xs_hbm.at[pl.ds(lo + r, 1)], zsem))
            off = lo + head
            rest = n - head
            size = MOE_BLOCK // 2
            while size >= SUBLANES:
                @pl.when((rest & size) != 0)
                def _(off=off, size=size):
                    dst = xs_hbm.at[pl.ds(pl.multiple_of(off, SUBLANES), size)]
                    act(pltpu.make_async_copy(zeros_ref.at[pl.ds(0, size)], dst, zsem))
                off = off + (rest & size)
                size //= 2
            return c

        def per_block(b, c):
            dst = xs_hbm.at[pl.ds(pl.multiple_of(b * MOE_BLOCK, MOE_BLOCK), MOE_BLOCK)]
            act(pltpu.make_async_copy(zeros_ref, dst, zsem))
            return c

        lax.fori_loop(0, N_EXPERTS, per_expert, 0)
        lax.fori_loop(nu_ref[0], n_blocks, per_block, 0)

    @pl.when(i == 0)
    def _():
        zeros_ref[...] = jnp.zeros(zeros_ref.shape, zeros_ref.dtype)
        zero_fill(lambda cp: cp.start())

    @pl.when(i == pl.num_programs(0) - 1)
    def _():
        zero_fill(lambda cp: cp.wait())

    base = i * TD

    def start(g, c):
        for u in range(SUBLANES):
            r = base + g * SUBLANES + u
            src = xpk_ref.at[g, pl.ds(u, 1)]
            pltpu.make_async_copy(src, xs_hbm.at[pl.ds(d0_ref[r], 1)], sem).start()
            pltpu.make_async_copy(src, xs_hbm.at[pl.ds(d1_ref[r], 1)], sem).start()
        return c

    lax.fori_loop(0, TD // SUBLANES, start, 0)
    for _ in range(2):
        pltpu.make_async_copy(xs_hbm.at[pl.ds(0, TD)], xs_hbm.at[pl.ds(0, TD)], sem).wait()


def _dispatch(dest0, dest1, pad_lo, n_pad, n_used, xpk, n_blocks):
    m = xpk.shape[0]
    grid_spec = pltpu.PrefetchScalarGridSpec(
        num_scalar_prefetch=5,
        grid=(m // TD,),
        in_specs=[pl.BlockSpec((TD // SUBLANES, SUBLANES, D_MODEL // 2), lambda i, *_: (i, 0, 0))],
        out_specs=pl.BlockSpec(memory_space=pl.ANY),
        scratch_shapes=[pltpu.VMEM((MOE_BLOCK, D_MODEL // 2), jnp.uint32),
                        pltpu.SemaphoreType.DMA((2,))],
    )
    return pl.pallas_call(
        functools.partial(_dispatch_kernel, n_blocks=n_blocks),
        grid_spec=grid_spec,
        out_shape=jax.ShapeDtypeStruct((n_blocks * MOE_BLOCK, D_MODEL // 2), jnp.uint32),
        compiler_params=pltpu.CompilerParams(dimension_semantics=("arbitrary",)),
        name="dispatch",
    )(dest0, dest1, pad_lo, n_pad, n_used, xpk.reshape(m // SUBLANES, SUBLANES, D_MODEL // 2))


def _experts_kernel(be_ref, nu_ref, nxt_ref, x_ref, wg_hbm, wu_hbm, wd_hbm, y_ref,
                    sg_ref, su_ref, sd_ref, wgb_ref, wub_ref, wdb_ref, sems):
    b = pl.program_id(0)
    active = b < nu_ref[0]
    new_expert = jnp.logical_or(b == 0, be_ref[b] != be_ref[jnp.maximum(b - 1, 0)])

    def weight_copies(e):
        return (pltpu.make_async_copy(wg_hbm.at[e], sg_ref, sems.at[0]),
                pltpu.make_async_copy(wu_hbm.at[e], su_ref, sems.at[1]),
                pltpu.make_async_copy(wd_hbm.at[e], sd_ref, sems.at[2]))

    @pl.when(b == 0)
    def _():
        for cp in weight_copies(be_ref[0]):
            cp.start()

    @pl.when(jnp.logical_and(active, new_expert))
    def _():
        for cp in weight_copies(be_ref[b]):
            cp.wait()
        wgb_ref[...] = sg_ref[...].astype(BF16)
        wub_ref[...] = su_ref[...].astype(BF16)
        wdb_ref[...] = sd_ref[...].astype(BF16)

        @pl.when(nxt_ref[b] >= 0)
        def _():
            for cp in weight_copies(nxt_ref[b]):
                cp.start()

    @pl.when(active)
    def _():
        half = D_MODEL // 2
        xa, xb = (v.astype(BF16) for v in _unpack_bf16_pairs(x_ref[...]))
        g = jnp.dot(xa, wgb_ref[:half, :], preferred_element_type=F32)
        g = g + jnp.dot(xb, wgb_ref[half:, :], preferred_element_type=F32)
        u = jnp.dot(xa, wub_ref[:half, :], preferred_element_type=F32)
        u = u + jnp.dot(xb, wub_ref[half:, :], preferred_element_type=F32)
        hmid = (g * jax.nn.sigmoid(g)) * u
        y = jnp.dot(hmid.astype(BF16), wdb_ref[...], preferred_element_type=F32)
        y_ref[...] = _pack_bf16_pairs(y)

    @pl.when(b >= nu_ref[0])
    def _():
        y_ref[...] = jnp.zeros(y_ref.shape, y_ref.dtype)


def _experts(block_e, n_used, next_e, x_sorted, w_gate, w_up, w_down):
    p = x_sorted.shape[0]
    nb = p // MOE_BLOCK

    def xrow(b, be, nu, nxt):
        return (jnp.maximum(jnp.minimum(b, nu[0] - 1), 0), 0)

    grid_spec = pltpu.PrefetchScalarGridSpec(
        num_scalar_prefetch=3,
        grid=(nb,),
        in_specs=[
            pl.BlockSpec((MOE_BLOCK, D_MODEL // 2), xrow),
            pl.BlockSpec(memory_space=pl.ANY),
            pl.BlockSpec(memory_space=pl.ANY),
            pl.BlockSpec(memory_space=pl.ANY),
        ],
        out_specs=pl.BlockSpec((MOE_BLOCK, D_MODEL // 2), lambda b, be, nu, nxt: (b, 0)),
        scratch_shapes=[pltpu.VMEM((D_MODEL, D_FF), F32), pltpu.VMEM((D_MODEL, D_FF), F32),
                        pltpu.VMEM((D_FF, D_MODEL), F32),
                        pltpu.VMEM((D_MODEL, D_FF), BF16), pltpu.VMEM((D_MODEL, D_FF), BF16),
                        pltpu.VMEM((D_FF, D_MODEL), BF16),
                        pltpu.SemaphoreType.DMA((3,))],
    )
    return pl.pallas_call(
        _experts_kernel,
        grid_spec=grid_spec,
        out_shape=jax.ShapeDtypeStruct((p, D_MODEL // 2), jnp.uint32),
        compiler_params=pltpu.CompilerParams(dimension_semantics=("arbitrary",),
                                             vmem_limit_bytes=VMEM_LIMIT),
        name="experts",
    )(block_e, n_used, next_e, x_sorted, w_gate, w_up, w_down)


def _combine_kernel(d0_ref, d1_ref, h_ref, mf_ref, gfin_ref, y_hbm, outp_ref, outs_ref, y0_ref, y1_ref, sems,
                    *, n_tiles, n_prompt_tiles):
    i = pl.program_id(0)

    def gather(tile, slot, act):
        base = tile * TM

        def body(g, c):
            for u in range(SUBLANES):
                r = base + g * SUBLANES + u
                act(pltpu.make_async_copy(y_hbm.at[pl.ds(d0_ref[r], 1)], y0_ref.at[slot, g, pl.ds(u, 1)],
                                          sems.at[slot]))
                act(pltpu.make_async_copy(y_hbm.at[pl.ds(d1_ref[r], 1)], y1_ref.at[slot, g, pl.ds(u, 1)],
                                          sems.at[slot]))
            return c
        lax.fori_loop(0, TM // SUBLANES, body, 0)

    @pl.when(i == 0)
    def _():
        gather(0, 0, lambda cp: cp.start())

    @pl.when(i + 1 < n_tiles)
    def _():
        gather(i + 1, (i + 1) % 2, lambda cp: cp.start())

    slot = i % 2
    for _ in range(2):
        pltpu.make_async_copy(y_hbm.at[pl.ds(0, TM)], y_hbm.at[pl.ds(0, TM)], sems.at[slot]).wait()

    def finish(out_ref):
        mf = mf_ref[...]
        g0, g1 = mf[:, :, 0:1], mf[:, :, 1:2]
        half = D_MODEL // 2
        a0, b0 = _unpack_bf16_pairs(y0_ref[slot])
        a1, b1 = _unpack_bf16_pairs(y1_ref[slot])
        o_lo = h_ref[:, :, :half] + (g0 * a0 + g1 * a1)
        o_hi = h_ref[:, :, half:] + (g0 * b0 + g1 * b1)
        sumsq = jnp.sum(o_lo * o_lo, axis=-1, keepdims=True) + jnp.sum(o_hi * o_hi, axis=-1, keepdims=True)
        inv_rms = lax.rsqrt(sumsq / D_MODEL + EPS)
        out_ref[:, :, :half] = o_lo * inv_rms * gfin_ref[:, :, :half]
        out_ref[:, :, half:] = o_hi * inv_rms * gfin_ref[:, :, half:]

    @pl.when(i < n_prompt_tiles)
    def _():
        finish(outp_ref)

    @pl.when(i >= n_prompt_tiles)
    def _():
        finish(outs_ref)


def _combine(dest0, dest1, h, mf, gfin, y_sorted, *, n_prompt_rows):
    m = h.shape[0]
    npt = n_prompt_rows // TM
    tg = TM // SUBLANES
    grouped = lambda a: a.reshape(a.shape[0] // SUBLANES, SUBLANES, a.shape[1])
    grid_spec = pltpu.PrefetchScalarGridSpec(
        num_scalar_prefetch=2,
        grid=(m // TM,),
        in_specs=[
            pl.BlockSpec((tg, SUBLANES, D_MODEL), lambda i, *_: (i, 0, 0)),
            pl.BlockSpec((tg, SUBLANES, LANES), lambda i, *_: (i, 0, 0)),
            pl.BlockSpec((1, 1, D_MODEL), lambda i, *_: (0, 0, 0)),
            pl.BlockSpec(memory_space=pl.ANY),
        ],
        out_specs=[pl.BlockSpec((tg, SUBLANES, D_MODEL), lambda i, *_: (jnp.minimum(i, npt - 1), 0, 0)),
                   pl.BlockSpec((tg, SUBLANES, D_MODEL), lambda i, *_: (jnp.maximum(i - npt, 0), 0, 0))],
        scratch_shapes=[pltpu.VMEM((2, tg, SUBLANES, D_MODEL // 2), jnp.uint32),
                        pltpu.VMEM((2, tg, SUBLANES, D_MODEL // 2), jnp.uint32),
                        pltpu.SemaphoreType.DMA((2,))],
    )
    y_p, y_s = pl.pallas_call(
        functools.partial(_combine_kernel, n_tiles=m // TM, n_prompt_tiles=npt),
        grid_spec=grid_spec,
        out_shape=[jax.ShapeDtypeStruct((n_prompt_rows // SUBLANES, SUBLANES, D_MODEL), F32),
                   jax.ShapeDtypeStruct(((m - n_prompt_rows) // SUBLANES, SUBLANES, D_MODEL), F32)],
        compiler_params=pltpu.CompilerParams(dimension_semantics=("arbitrary",),
                                             vmem_limit_bytes=VMEM_LIMIT),
        name="combine",
    )(dest0, dest1, grouped(h), grouped(mf), gfin.reshape(1, 1, D_MODEL), y_sorted)
    return y_p.reshape(n_prompt_rows, D_MODEL), y_s.reshape(m - n_prompt_rows, D_MODEL)


def _rope_tables(pos):
    f32 = np.float32
    inv = np.power(f32(ROPE_THETA), -np.arange(0, ROPE_DIM, 2, dtype=f32) / f32(ROPE_DIM)).astype(f32)
    ang = (pos.astype(f32)[:, None] * inv[None, :]).astype(f32)
    cos, sin = np.cos(ang).astype(f32), np.sin(ang).astype(f32)
    return np.concatenate([cos, cos], axis=-1), np.concatenate([-sin, sin], axis=-1)


def _swap_halves(w):
    return jnp.concatenate([w[..., ROPE_DIM // 2:], w[..., :ROPE_DIM // 2]], axis=-1)


def kernel(x_prompt, x_sample, cache_kv_latent, cache_k_rope, state_conv, norm_mix, w_in, norm_q, w_uq,
           norm_kv, w_uk, w_uv, conv_w, norm_attn_out, norm_conv_out, w_o, norm_ffn, w_router_group,
           b_router_group, w_router_expert, b_router_expert, w_gate, w_up, w_down, norm_final):
    assert w_in.shape[0] == 1, "single-layer trunk"
    bp, seq_p, _ = x_prompt.shape
    bs, seq_s, _ = x_sample.shape
    past_len = cache_kv_latent.shape[2]
    np_rows, ns_rows = bp * seq_p, bs * seq_s
    m = np_rows + ns_rows
    assert seq_p % TM == 0 and TM % seq_s == 0 and ns_rows % TM == 0 and seq_s == CHUNK
    assert m % TD == 0

    xp = x_prompt.reshape(np_rows, D_MODEL)
    xs = x_sample.reshape(ns_rows, D_MODEL)
    row_vec = lambda v: v.reshape(1, -1)

    assert w_in.shape[2] == Q_LORA + KV_LORA + ROPE_DIM + 3 * CONV_CH
    w_t = jnp.swapaxes(w_in[0], 0, 1)
    assert w_t.shape[0] % (W_IN_SLABS * 2 * SUBLANES) == 0
    wq4 = w_uq[0].reshape(Q_LORA, N_HEADS, QK_NOPE + ROPE_DIM)
    wq_rope = wq4[:, :, QK_NOPE:]
    w_q = jnp.concatenate([wq4[:, :, :QK_NOPE].reshape(Q_LORA, -1), wq_rope.reshape(Q_LORA, -1),
                           _swap_halves(wq_rope).reshape(Q_LORA, -1)], axis=1).astype(BF16)
    w_ukt = jnp.transpose(w_uk[0], (1, 2, 0)).astype(BF16)
    w_uvh = jnp.transpose(w_uv[0], (1, 0, 2)).astype(BF16)
    w_ob = w_o[0]
    n_router = N_GROUPS + N_EXPERTS
    w_r = jnp.concatenate([w_router_group[0], w_router_expert[0].reshape(D_MODEL, N_EXPERTS)], axis=1)
    w_r = jnp.pad(w_r, ((0, 0), (0, LANES - n_router)))
    w_rh = w_r.astype(BF16)
    w_rl = (w_r - w_rh.astype(F32)).astype(BF16)
    w_r2 = jnp.concatenate([w_rh, w_rl], axis=1)
    b_r =jnp.pad(jnp.concatenate([b_router_group[0], b_router_expert[0].reshape(N_EXPERTS)]),
                  (0, LANES - n_router)).reshape(1, LANES)

    cos_p, sin_p = _rope_tables(np.arange(seq_p))
    cos_s, sin_s = _rope_tables(past_len + np.arange(seq_s))
    cosk = np.concatenate([cos_p, np.tile(cos_s, (TM // seq_s, 1))], axis=0)
    sink = np.concatenate([sin_p, np.tile(sin_s, (TM // seq_s, 1))], axis=0)
    state = jnp.concatenate([jnp.zeros((bp, CONV_W - 1, CONV_CH), F32), state_conv[0]], axis=0)

    cqn, ckv_p, kr_p, ckv_s, kr_s, conv_n, utail = _in_proj(
        xp, xs, row_vec(norm_mix[0]), w_t, row_vec(norm_q[0]), row_vec(norm_kv[0]),
        row_vec(norm_conv_out[0]), conv_w[0], cosk, sink, state, seq_p=seq_p, seq_s=seq_s)

    gao = row_vec(norm_attn_out[0])
    w_kv = jnp.concatenate([w_uk[0].reshape(KV_LORA, N_HEADS * QK_NOPE),
                            w_uv[0].reshape(KV_LORA, N_HEADS * V_DIM)], axis=1).astype(BF16)
    attn_p = _attention_heads(cqn, w_q, w_kv, np.tile(cos_p, (1, N_HEADS)), np.tile(sin_p, (1, N_HEADS)),
                              gao, ckv_p, kr_p, n_batch=bp, seq=seq_p)
    attn_s = _attention(cqn, w_q, w_ukt, w_uvh, np.tile(cos_s, (1, N_HEADS)), np.tile(sin_s, (1, N_HEADS)),
                        gao, ckv_s, kr_s, cache_kv_latent[0], jnp.swapaxes(cache_k_rope[0], 1, 2),
                        n_batch=bs, seq=seq_s, row0=np_rows)

    h, xpk, mi, mf, cnt = _out_proj(attn_p, attn_s, conv_n, xp, xs, w_ob, row_vec(norm_ffn[0]),
                                    w_r2, b_r)

    counts = cnt[0, :N_EXPERTS].astype(jnp.int32)
    padded = (counts + MOE_BLOCK - 1) // MOE_BLOCK * MOE_BLOCK
    pad_end = jnp.cumsum(padded)
    pad_start = pad_end - padded
    n_blocks = -(-(m * 2) // MOE_BLOCK) + N_EXPERTS
    block_row0 = jnp.arange(n_blocks, dtype=jnp.int32) * MOE_BLOCK
    block_e = jnp.minimum(jnp.sum((pad_end[None, :] <= block_row0[:, None]).astype(jnp.int32), axis=1),
                          N_EXPERTS - 1)
    n_used = (pad_end[-1:] // MOE_BLOCK).astype(jnp.int32)
    expert_ids = jnp.arange(N_EXPERTS, dtype=jnp.int32)[:, None]

    def seg_start(e):
        return jnp.sum(jnp.where(expert_ids == e[None, :], pad_start[:, None], 0), axis=0)

    dest0 = seg_start(mi[0]) + mi[2]
    dest1 = seg_start(mi[1]) + mi[3]

    x_sorted = _dispatch(dest0, dest1, pad_start + counts, padded - counts, n_used, xpk, n_blocks)
    later = (expert_ids.T > block_e[:, None]) & (padded > 0)[None, :]
    next_e = jnp.min(jnp.where(later, expert_ids.T, N_EXPERTS), axis=1)
    next_e = jnp.where(next_e == N_EXPERTS, -1, next_e).astype(jnp.int32)
    y_sorted = _experts(block_e, n_used, next_e, x_sorted, w_gate[0], w_up[0], w_down[0])
    gfin = row_vec(norm_final)
    y_p, y_s = _combine(dest0, dest1, h, mf, gfin, y_sorted, n_prompt_rows=np_rows)

    ut = utail.reshape(m // CHUNK, SUBLANES, CONV_CH)
    tails = ut[:, SUBLANES - (CONV_W - 1):, :]
    p_last = (jnp.arange(bp) + 1) * (seq_p // CHUNK) - 1
    s_last = np_rows // CHUNK + (jnp.arange(bs) + 1) * (seq_s // CHUNK) - 1
    return (y_p.reshape(bp, seq_p, D_MODEL),
            y_s.reshape(bs, seq_s, D_MODEL),
            ckv_p.reshape(1, bp, seq_p, KV_LORA),
            jnp.swapaxes(kr_p, 1, 2)[None],
            tails[p_last][None],
            ckv_s.reshape(1, bs, seq_s, KV_LORA),
            jnp.swapaxes(kr_s, 1, 2)[None],
            tails[s_last][None])
```

```python
import functools

import jax
import jax.numpy as jnp
import numpy as np
from jax import lax
from jax.experimental import pallas as pl
from jax.experimental.pallas import tpu as pltpu

F32 = jnp.float32
BF16 = jnp.bfloat16

D_MODEL = 2048
N_HEADS = 8
QK_NOPE = 128
ROPE_DIM = 64
V_DIM = 128
Q_LORA = 512
KV_LORA = 512
ATTN_W = N_HEADS * V_DIM
CONV_CH = D_MODEL - ATTN_W
CONV_W = 3
CHUNK = 64
N_GROUPS = 4
EXPERTS_PER_GROUP = 8
N_EXPERTS = N_GROUPS * EXPERTS_PER_GROUP
D_FF = 512
ROPE_THETA = 10000.0
EPS = 1e-6
ATTN_SCALE = (QK_NOPE + ROPE_DIM) ** -0.5
EXP2_SCALE = ATTN_SCALE * 1.4426950408889634

LANES = 128
SUBLANES = 8
TM = 256
TD = 2304
MOE_BLOCK = 256
TQ = 512
TK = 512
TKH = 512
TKX = 256
W_O_SLABS = 4
W_IN_SLABS = 10
NEG_BIG = -1e30
V7X_VMEM_BYTES = 64 * 1024 * 1024
VMEM_LIMIT = V7X_VMEM_BYTES * 7 // 8


def _rms(v, g):
    return v * lax.rsqrt(jnp.mean(v * v, axis=-1, keepdims=True) + EPS) * g


def _lane_bcast(v, width):
    if width % LANES == 0:
        return jnp.concatenate([v] * (width // LANES), axis=1)
    assert width < LANES
    return v[:, :width]


def _pack_bf16_pairs(v):
    half = v.shape[-1] // 2
    lo = lax.bitcast_convert_type(v[..., :half].astype(BF16).astype(F32), jnp.uint32)
    hi = lax.bitcast_convert_type(v[..., half:].astype(BF16).astype(F32), jnp.uint32)
    return (lo >> 16) | (hi & jnp.uint32(0xFFFF0000))


def _unpack_bf16_pairs(w):
    return (lax.bitcast_convert_type(w << 16, F32),
            lax.bitcast_convert_type(w & jnp.uint32(0xFFFF0000), F32))


def _load_weight_as_bf16(w_hbm, dst_ref, stage_ref, sems):
    rows = stage_ref.shape[1]
    n_slabs = dst_ref.shape[0] // rows

    def slab_copy(c):
        return pltpu.make_async_copy(w_hbm.at[pl.ds(c * rows, rows)], stage_ref.at[c % 2], sems.at[c % 2])

    slab_copy(0).start()
    for c in range(n_slabs):
        if c + 1 < n_slabs:
            slab_copy(c + 1).start()
        slab_copy(c).wait()
        dst_ref[c * rows:(c + 1) * rows, :] = stage_ref[c % 2].astype(BF16)


def _const_spec(shape):
    nd = len(shape)
    return pl.BlockSpec(shape, lambda *_: (0,) * nd, pipeline_mode=pl.Buffered(1))


def _in_proj_kernel(xp_ref, xs_ref, gmix_ref, w_hbm, gq_ref, gkv_ref, gco_ref, convw_ref,
                    cos_ref, sin_ref, state_ref,
                    cqn_ref, ckvp_ref, krp_ref, ckvs_ref, krs_ref, convn_ref, utail_ref,
                    ext_ref, wt_ref, stage_ref, wsems,
                    *, n_prompt_tiles, tiles_per_seq, n_prompt_seq, sample_seq_len):
    i = pl.program_id(0)

    @pl.when(i == 0)
    def _():
        ext_ref[...] = jnp.zeros(ext_ref.shape, F32)
        _load_weight_as_bf16(w_hbm, wt_ref, stage_ref, wsems)

    def conv_block(u_sub, gate_sub, row0, length):
        ext_ref[SUBLANES:SUBLANES + length, :] = u_sub
        um1 = ext_ref[SUBLANES - 1:SUBLANES - 1 + length, :]
        um2 = ext_ref[SUBLANES - 2:SUBLANES - 2 + length, :]
        cw = convw_ref[...]
        conv = cw[0:1] * um2 + cw[1:2] * um1 + cw[2:3] * u_sub
        convn_ref[row0:row0 + length, :] = _rms(gate_sub * conv, gco_ref[...]).astype(BF16)

    def tile(x_ref, is_prompt):
        ckv_ref, krt_ref = (ckvp_ref, krp_ref) if is_prompt else (ckvs_ref, krs_ref)
        x = x_ref[...]
        xg = (x * gmix_ref[...]).astype(BF16)
        inv_rms = lax.rsqrt(jnp.mean(x * x, axis=-1, keepdims=True) + EPS)
        lat_w = Q_LORA + KV_LORA
        conv0 = lat_w + ROPE_DIM
        nt = (((1,), (1,)), ((), ()))

        def project(lo, hi):
            return inv_rms * lax.dot_general(xg, wt_ref[lo:hi, :], nt, preferred_element_type=F32)

        z_ch = project(conv0 + CONV_CH, conv0 + 3 * CONV_CH)
        u = z_ch[:, :CONV_CH] * z_ch[:, CONV_CH:]
        for j in range(TM // CHUNK):
            utail_ref[j] = u[CHUNK * (j + 1) - SUBLANES:CHUNK * (j + 1), :]
        gate_b = project(conv0, conv0 + CONV_CH)

        if is_prompt:
            first = (i % tiles_per_seq) == 0
            carried = ext_ref[TM + SUBLANES - 2:TM + SUBLANES, :]
            ext_ref[SUBLANES - 2:SUBLANES, :] = jnp.where(first, state_ref[i // tiles_per_seq], carried)
            conv_block(u, gate_b, 0, TM)
        else:
            n_sub = TM // sample_seq_len
            seq0 = n_prompt_seq + (i - n_prompt_tiles) * n_sub
            for k in range(n_sub):
                ext_ref[SUBLANES - 2:SUBLANES, :] = state_ref[seq0 + k]
                lo = k * sample_seq_len
                conv_block(u[lo:lo + sample_seq_len], gate_b[lo:lo + sample_seq_len], lo, sample_seq_len)

        zk = project(lat_w, conv0)
        zk_swapped = jnp.concatenate([zk[:, ROPE_DIM // 2:], zk[:, :ROPE_DIM // 2]], axis=1)
        k_rope = zk * cos_ref[...] + zk_swapped * sin_ref[...]
        if is_prompt:
            krt_ref[...] = k_rope.T
        else:
            for k in range(TM // sample_seq_len):
                krt_ref[k] = k_rope[k * sample_seq_len:(k + 1) * sample_seq_len, :].T
        ckv_ref[...] = _rms(project(Q_LORA, lat_w), gkv_ref[...])
        cqn_ref[...] = _rms(project(0, Q_LORA), gq_ref[...]).astype(BF16)

    @pl.when(i < n_prompt_tiles)
    def _():
        tile(xp_ref, True)

    @pl.when(i >= n_prompt_tiles)
    def _():
        tile(xs_ref, False)


def _in_proj(xp, xs, gmix, w_t, gq, gkv, gco, convw, cosk, sink, state, *, seq_p, seq_s):
    np_rows, ns_rows = xp.shape[0], xs.shape[0]
    m = np_rows + ns_rows
    npt, nst = np_rows // TM, ns_rows // TM
    tps = seq_p // TM
    n_prompt_seq = np_rows // seq_p
    last_p = npt - 1

    def tab_idx(i):
        return (jnp.where(i < npt, i % tps, tps), 0)

    row = lambda i: (i, 0)
    prow = lambda i: (jnp.minimum(i, last_p), 0)
    srow = lambda i: (jnp.maximum(i - npt, 0), 0)
    kern = functools.partial(_in_proj_kernel, n_prompt_tiles=npt, tiles_per_seq=tps,
                             n_prompt_seq=n_prompt_seq, sample_seq_len=seq_s)
    return pl.pallas_call(
        kern,
        grid=(npt + nst,),
        in_specs=[
            pl.BlockSpec((TM, D_MODEL), prow),
            pl.BlockSpec((TM, D_MODEL), srow),
            _const_spec((1, D_MODEL)),
            pl.BlockSpec(memory_space=pl.ANY),
            _const_spec((1, Q_LORA)),
            _const_spec((1, KV_LORA)),
            _const_spec((1, CONV_CH)),
            _const_spec((CONV_W, CONV_CH)),
            pl.BlockSpec((TM, ROPE_DIM), tab_idx),
            pl.BlockSpec((TM, ROPE_DIM), tab_idx),
            _const_spec(state.shape),
        ],
        out_specs=[
            pl.BlockSpec((TM, Q_LORA), row),
            pl.BlockSpec((TM, KV_LORA), prow),
            pl.BlockSpec((None, ROPE_DIM, TM), lambda i: (jnp.minimum(i, last_p) // tps, 0,
                                                          jnp.minimum(i, last_p) % tps)),
            pl.BlockSpec((TM, KV_LORA), srow),
            pl.BlockSpec((TM // seq_s, ROPE_DIM, seq_s), lambda i: (jnp.maximum(i - npt, 0), 0, 0)),
            pl.BlockSpec((TM, CONV_CH), row),
            pl.BlockSpec((TM // CHUNK, SUBLANES, CONV_CH), lambda i: (i, 0, 0)),
        ],
        out_shape=[
            jax.ShapeDtypeStruct((m, Q_LORA), BF16),
            jax.ShapeDtypeStruct((np_rows, KV_LORA), F32),
            jax.ShapeDtypeStruct((n_prompt_seq, ROPE_DIM, seq_p), F32),
            jax.ShapeDtypeStruct((ns_rows, KV_LORA), F32),
            jax.ShapeDtypeStruct((ns_rows // seq_s, ROPE_DIM, seq_s), F32),
            jax.ShapeDtypeStruct((m, CONV_CH), BF16),
            jax.ShapeDtypeStruct((m // CHUNK, SUBLANES, CONV_CH), F32),
        ],
        scratch_shapes=[pltpu.VMEM((TM + SUBLANES, CONV_CH), F32),
                        pltpu.VMEM(w_t.shape, BF16),
                        pltpu.VMEM((2, w_t.shape[0] // W_IN_SLABS, w_t.shape[1]), F32),
                        pltpu.SemaphoreType.DMA((2,))],
        compiler_params=pltpu.CompilerParams(dimension_semantics=("arbitrary",),
                                             vmem_limit_bytes=VMEM_LIMIT),
        name="in_proj",
    )(xp, xs, gmix, w_t, gq, gkv, gco, convw, cosk, sink, state)


def _attn_kernel(cqn_ref, wq_ref, wuk_ref, wuv_ref, cos_ref, sin_ref, gao_ref, pkv_ref, pkr_ref, kv_ref, kr_ref,
                 out_ref, qlat_ref, qr_ref, m_ref, l_ref, acc_ref, s_ref, *, tq, n_past):
    rows = N_HEADS * tq

    q = jnp.dot(cqn_ref[...], wq_ref[...], preferred_element_type=F32)
    nope_w = N_HEADS * QK_NOPE
    rope_w = N_HEADS * ROPE_DIM
    qrope = q[:, nope_w:nope_w + rope_w] * cos_ref[...] + q[:, nope_w + rope_w:] * sin_ref[...]
    for h in range(N_HEADS):
        qn = q[:, h * QK_NOPE:(h + 1) * QK_NOPE].astype(BF16)
        ql = jnp.dot(qn, wuk_ref[h], preferred_element_type=F32)
        qlat_ref[h * tq:(h + 1) * tq, :] = (ql * EXP2_SCALE).astype(BF16)
        qr_ref[h * tq:(h + 1) * tq, :] = (qrope[:, h * ROPE_DIM:(h + 1) * ROPE_DIM] * EXP2_SCALE).astype(BF16)


    nt = (((1,), (1,)), ((), ()))

    def scores(kc_f32, krt_f32):
        s = lax.dot_general(qlat_ref[...], kc_f32.astype(BF16), nt, preferred_element_type=F32)
        return s + jnp.dot(qr_ref[...], krt_f32.astype(BF16), preferred_element_type=F32)

    def update(s, kc_f32, mask, first=False):
        if mask is not None:
            s = jnp.where(mask, s, NEG_BIG)
        m_cur = jnp.max(s, axis=-1, keepdims=True)
        if first:
            m_new = jnp.broadcast_to(m_cur, m_ref.shape)
        else:
            m_prev = m_ref[...]
            m_new = jnp.maximum(m_prev, m_cur)
            alpha = jnp.exp2(m_prev - m_new)
        p = jnp.exp2(s - _lane_bcast(m_new, s.shape[1]))
        l_cur = jnp.sum(p, axis=-1, keepdims=True)
        pv = jnp.dot(p.astype(BF16), kc_f32.astype(BF16), preferred_element_type=F32)
        if first:
            l_ref[...] = jnp.broadcast_to(l_cur, l_ref.shape)
            acc_ref[...] = pv
        else:
            l_ref[...] = alpha * l_ref[...] + l_cur
            acc_ref[...] = _lane_bcast(alpha, KV_LORA) * acc_ref[...] + pv
        m_ref[...] = m_new

    def pipelined(kv, kr, lo, hi, last, mask_fn):
        def body(j, c):
            k0 = pl.multiple_of(j * TK, TK)
            k1 = pl.multiple_of(jnp.minimum(j + 1, last) * TK, TK)
            s_cur = s_ref[j % 2]
            s_ref[(j + 1) % 2] = scores(kv[pl.ds(k1, TK), :], kr[:, pl.ds(k1, TK)])
            update(s_cur, kv[pl.ds(k0, TK), :], None if mask_fn is None else mask_fn(k0))
            return c
        lax.fori_loop(lo, hi, body, 0)

    def pipelined_pairs(kv, kr, n_pairs, last):
        def body(i, c):
            ka = pl.multiple_of((2 * i + 1) * TK, TK)
            kb = pl.multiple_of((2 * i + 2) * TK, TK)
            kc = pl.multiple_of(jnp.minimum(2 * i + 3, last) * TK, TK)
            s_ref[0] = scores(kv[pl.ds(kb, TK), :], kr[:, pl.ds(kb, TK)])
            update(s_ref[1], kv[pl.ds(ka, TK), :], None)
            s_ref[1] = scores(kv[pl.ds(kc, TK), :], kr[:, pl.ds(kc, TK)])
            update(s_ref[0], kv[pl.ds(kb, TK), :], None)
            return c
        lax.fori_loop(0, n_pairs, body, 0)

    def first_block(kv, kr, last, mask):
        k1 = pl.multiple_of(jnp.minimum(1, last) * TK, TK)
        s_ref[0] = scores(kv[pl.ds(0, TK), :], kr[:, pl.ds(0, TK)])
        s_ref[1] = scores(kv[pl.ds(k1, TK), :], kr[:, pl.ds(k1, TK)])
        update(s_ref[0], kv[pl.ds(0, TK), :], mask, first=True)

    n_pb = n_past // TK
    first_block(pkv_ref, pkr_ref, n_pb - 1, None)
    n_pairs = (n_pb - 1) // 2
    pipelined_pairs(pkv_ref, pkr_ref, n_pairs, n_pb - 1)
    if 1 + 2 * n_pairs < n_pb:
        pipelined(pkv_ref, pkr_ref, 1 + 2 * n_pairs, n_pb, n_pb - 1, None)

    update(scores(kv_ref[...], kr_ref[...]), kv_ref[...], None)

    o = acc_ref[...] / _lane_bcast(l_ref[...], KV_LORA)
    parts = []
    for h in range(N_HEADS):
        oh = o[h * tq:(h + 1) * tq, :].astype(BF16)
        parts.append(jnp.dot(oh, wuv_ref[h], preferred_element_type=F32))
    attn = jnp.concatenate(parts, axis=-1)
    out_ref[...] = _rms(attn, gao_ref[...]).astype(BF16)


def _attention(cqn, w_q, w_ukt, w_uv, cosq, sinq, gao, ckv, krope, past_kv, past_kr, *, n_batch, seq, row0):
    tq = seq
    n_past = past_kv.shape[1]
    assert n_past % CHUNK == 0 and seq <= CHUNK and n_past % TK == 0
    blk0 = row0 // tq
    in_specs = [
        pl.BlockSpec((tq, Q_LORA), lambda b, q: (blk0 + b, 0)),
        _const_spec(w_q.shape),
        _const_spec(w_ukt.shape),
        _const_spec(w_uv.shape),
        pl.BlockSpec((tq, N_HEADS * ROPE_DIM), lambda b, q: (0, 0)),
        pl.BlockSpec((tq, N_HEADS * ROPE_DIM), lambda b, q: (0, 0)),
        _const_spec((1, ATTN_W)),
        pl.BlockSpec((None, n_past, KV_LORA), lambda b, q: (b, 0, 0)),
        pl.BlockSpec((None, ROPE_DIM, n_past), lambda b, q: (b, 0, 0)),
        pl.BlockSpec((seq, KV_LORA), lambda b, q: (b, 0)),
        pl.BlockSpec((None, ROPE_DIM, seq), lambda b, q: (b, 0, 0)),
    ]
    args = [cqn, w_q, w_ukt, w_uv, cosq, sinq, gao, past_kv, past_kr, ckv, krope]
    rows = N_HEADS * tq
    return pl.pallas_call(
        functools.partial(_attn_kernel, tq=tq, n_past=n_past),
        grid=(n_batch, 1),
        in_specs=in_specs,
        out_specs=pl.BlockSpec((tq, ATTN_W), lambda b, q: (b, 0)),
        out_shape=jax.ShapeDtypeStruct((n_batch * seq, ATTN_W), BF16),
        scratch_shapes=[
            pltpu.VMEM((rows, KV_LORA), BF16),
            pltpu.VMEM((rows, ROPE_DIM), BF16),
            pltpu.VMEM((rows, LANES), F32),
            pltpu.VMEM((rows, LANES), F32),
            pltpu.VMEM((rows, KV_LORA), F32),
            pltpu.VMEM((2, rows, TK), F32),
        ],
        compiler_params=pltpu.CompilerParams(dimension_semantics=("arbitrary", "arbitrary"),
                                             vmem_limit_bytes=VMEM_LIMIT),
        name="attn_sample",
    )(*args)


def _attn_heads_kernel(cqn_ref, wq_ref, wkv_ref, cos_ref, sin_ref, gao_ref, kv_ref, krt_ref, out_ref,
                       kcat_ref, vh_ref, qcat_ref, m_ref, l_ref, acc_ref, klim_ref, *, tq, seq):
    qi = pl.program_id(1)
    nt = (((1,), (1,)), ((), ()))
    kw = QK_NOPE + ROPE_DIM
    kpad = kcat_ref.shape[-1]

    @pl.when(qi == 0)
    def _():
        def expand(j, c):
            k0 = pl.multiple_of(j * TKX, TKX)
            latent = kv_ref[pl.ds(k0, TKX), :].astype(BF16)
            kvh = jnp.dot(latent, wkv_ref[...], preferred_element_type=F32)
            k_rope = krt_ref[:, pl.ds(k0, TKX)].T.astype(BF16)
            for h in range(N_HEADS):
                kcat_ref[h, pl.ds(k0, TKX), :QK_NOPE] = kvh[:, h * QK_NOPE:(h + 1) * QK_NOPE].astype(BF16)
                kcat_ref[h, pl.ds(k0, TKX), QK_NOPE:kw] = k_rope
                kcat_ref[h, pl.ds(k0, TKX), kw:] = jnp.zeros((TKX, kpad - kw), BF16)
                v0 = N_HEADS * QK_NOPE + h * V_DIM
                vh_ref[h, pl.ds(k0, TKX), :] = kvh[:, v0:v0 + V_DIM].astype(BF16)
            return c
        lax.fori_loop(0, seq // TKX, expand, 0)

    q = jnp.dot(cqn_ref[...], wq_ref[...], preferred_element_type=F32)
    nope_w = N_HEADS * QK_NOPE
    rope_w = N_HEADS * ROPE_DIM
    qrope = q[:, nope_w:nope_w + rope_w] * cos_ref[...] + q[:, nope_w + rope_w:] * sin_ref[...]
    for h in range(N_HEADS):
        qcat_ref[h, :, :QK_NOPE] = (q[:, h * QK_NOPE:(h + 1) * QK_NOPE] * EXP2_SCALE).astype(BF16)
        qcat_ref[h, :, QK_NOPE:kw] = (qrope[:, h * ROPE_DIM:(h + 1) * ROPE_DIM] * EXP2_SCALE).astype(BF16)
        qcat_ref[h, :, kw:] = jnp.zeros((tq, kpad - kw), BF16)

    r = lax.broadcasted_iota(jnp.int32, (tq, LANES), 0)
    klim_ref[...] = ((qi * tq + r) & ~(CHUNK - 1)) + CHUNK

    tk = TKH

    def block(k0, masked, first):
        if masked:
            cidx = lax.broadcasted_iota(jnp.int32, (tq, tk), 1)
            mask = cidx < _lane_bcast(klim_ref[...] - k0, tk)
        for h in range(N_HEADS):
            s = lax.dot_general(qcat_ref[h], kcat_ref[h, pl.ds(k0, tk), :], nt, preferred_element_type=F32)
            if masked:
                s = jnp.where(mask, s, NEG_BIG)
            m_cur = jnp.max(s, axis=-1, keepdims=True)
            if first:
                m_new = jnp.broadcast_to(m_cur, (tq, LANES))
            else:
                m_prev = m_ref[h]
                m_new = jnp.maximum(m_prev, m_cur)
                alpha = jnp.exp2(m_prev - m_new)
            p = jnp.exp2(s - _lane_bcast(m_new, tk))
            l_cur = jnp.sum(p, axis=-1, keepdims=True)
            pv = jnp.dot(p.astype(BF16), vh_ref[h, pl.ds(k0, tk), :], preferred_element_type=F32)
            if first:
                l_ref[h] = jnp.broadcast_to(l_cur, (tq, LANES))
                acc_ref[h] = pv
            else:
                l_ref[h] = alpha * l_ref[h] + l_cur
                acc_ref[h] = _lane_bcast(alpha, V_DIM) * acc_ref[h] + pv
            m_ref[h] = m_new

    n_blocks = ((qi + 1) * tq + tk - 1) // tk
    n_full = jnp.minimum((qi * tq // CHUNK + 1) * CHUNK // tk, n_blocks)

    def loop(lo, hi, masked):
        def body(j, c):
            block(pl.multiple_of(j * tk, tk), masked, False)
            return c
        lax.fori_loop(lo, hi, body, 0)

    @pl.when(n_full == 0)
    def _():
        block(0, True, True)

    @pl.when(n_full > 0)
    def _():
        block(0, False, True)

    loop(1, n_full, False)
    loop(jnp.maximum(n_full, 1), n_blocks, True)

    attn = jnp.concatenate([acc_ref[h] / _lane_bcast(l_ref[h], V_DIM) for h in range(N_HEADS)], axis=-1)
    out_ref[...] = _rms(attn, gao_ref[...]).astype(BF16)


def _attention_heads(cqn, w_q, w_kv, cosq, sinq, gao, ckv, krope_t, *, n_batch, seq):
    nq = seq // TQ
    kpad = 2 * LANES
    assert QK_NOPE + ROPE_DIM <= kpad and V_DIM == LANES
    assert seq % TQ == 0 and seq % TKH == 0 and seq % TKX == 0 and TQ % CHUNK == 0 and TKH % CHUNK == 0
    return pl.pallas_call(
        functools.partial(_attn_heads_kernel, tq=TQ, seq=seq),
        grid=(n_batch, nq),
        in_specs=[
            pl.BlockSpec((TQ, Q_LORA), lambda b, q: (b * nq + q, 0)),
            _const_spec(w_q.shape),
            _const_spec(w_kv.shape),
            pl.BlockSpec((TQ, N_HEADS * ROPE_DIM), lambda b, q: (q, 0)),
            pl.BlockSpec((TQ, N_HEADS * ROPE_DIM), lambda b, q: (q, 0)),
            _const_spec((1, ATTN_W)),
            pl.BlockSpec((seq, KV_LORA), lambda b, q: (b, 0)),
            pl.BlockSpec((None, ROPE_DIM, seq), lambda b, q: (b, 0, 0)),
        ],
        out_specs=pl.BlockSpec((TQ, ATTN_W), lambda b, q: (b * nq + q, 0)),
        out_shape=jax.ShapeDtypeStruct((n_batch * seq, ATTN_W), BF16),
        scratch_shapes=[
            pltpu.VMEM((N_HEADS, seq, kpad), BF16),
            pltpu.VMEM((N_HEADS, seq, V_DIM), BF16),
            pltpu.VMEM((N_HEADS, TQ, kpad), BF16),
            pltpu.VMEM((N_HEADS, TQ, LANES), F32),
            pltpu.VMEM((N_HEADS, TQ, LANES), F32),
            pltpu.VMEM((N_HEADS, TQ, V_DIM), F32),
            pltpu.VMEM((TQ, LANES), jnp.int32),
        ],
        compiler_params=pltpu.CompilerParams(dimension_semantics=("arbitrary", "arbitrary"),
                                             vmem_limit_bytes=VMEM_LIMIT),
        name="attn_prompt",
    )(cqn, w_q, w_kv, cosq, sinq, gao, ckv, krope_t)


def _out_proj_kernel(attnp_ref, attns_ref, convn_ref, xp_ref, xs_ref, wo_hbm, gffn_ref, wr_ref,
                     br_ref, h_ref, xpk_ref, mi_ref, mf_ref, cnt_ref, carry_ref, logit_ref,
                     wo_ref, stage_ref, wsems, *, n_prompt_tiles):
    i = pl.program_id(0)

    @pl.when(i == 0)
    def _():
        carry_ref[...] = jnp.zeros(carry_ref.shape, F32)
        logit_ref[...] = jnp.zeros(logit_ref.shape, F32)
        _load_weight_as_bf16(wo_hbm, wo_ref, stage_ref, wsems)

    def tile(x_ref, attn_ref):
        prev_logits = logit_ref[...]
        y = jnp.dot(attn_ref[...], wo_ref[:ATTN_W, :], preferred_element_type=F32)
        y = y + jnp.dot(convn_ref[...], wo_ref[ATTN_W:, :], preferred_element_type=F32)
        h = x_ref[...] + y
        h_ref[...] = h
        xn = _rms(h, gffn_ref[...])

        half = D_MODEL // 2
        xh = xn.astype(BF16)
        xh32 = xh.astype(F32)
        lo = lax.bitcast_convert_type(xh32[:, :half], jnp.uint32)
        hi = lax.bitcast_convert_type(xh32[:, half:], jnp.uint32)
        xpk_ref[...] = (lo >> 16) | (hi & jnp.uint32(0xFFFF0000))

        xl = (xn - xh32).astype(BF16)
        hh_hl = jnp.dot(xh, wr_ref[...], preferred_element_type=F32)
        lh = jnp.dot(xl, wr_ref[:, :LANES], preferred_element_type=F32)
        logit_ref[...] = hh_hl[:, :LANES] + (lh + hh_hl[:, LANES:]) + br_ref[...]

        logits = prev_logits
        counted = (i > 0).astype(F32)
        lane = lax.broadcasted_iota(jnp.int32, (TM, LANES), 1).astype(F32)
        ninf = -jnp.inf
        far = float(LANES)

        def first_argmax(v):
            vmax = jnp.max(v, axis=-1, keepdims=True)
            return vmax, jnp.min(jnp.where(v == vmax, lane, far), axis=-1, keepdims=True)

        gl = jnp.where(lane < N_GROUPS, logits, ninf)
        gmax, gidx = first_argmax(gl)
        g_p = 1.0 / jnp.sum(jnp.exp(gl - gmax), axis=-1, keepdims=True)
        e_lo = N_GROUPS + EXPERTS_PER_GROUP * gidx
        el = jnp.where((lane >= e_lo) & (lane < e_lo + EXPERTS_PER_GROUP), logits, ninf)
        e1max, i1 = first_argmax(el)
        z = jnp.sum(jnp.exp(el - e1max), axis=-1, keepdims=True)
        el2 = jnp.where(lane == i1, ninf, el)
        e2max, i2 = first_argmax(el2)
        p1 = 1.0 / z
        p2 = jnp.exp(e2max - e1max) / z
        den = p1 + p2
        g0 = g_p * p1 / den
        g1 = g_p * p2 / den
        e0 = i1 - N_GROUPS
        e1 = i2 - N_GROUPS

        oh0 = lane == e0
        oh1 = lane == e1
        oh = jnp.where(oh0 | oh1, 1.0, 0.0)
        r = lax.broadcasted_iota(jnp.int32, (TM, TM), 0)
        c = lax.broadcasted_iota(jnp.int32, (TM, TM), 1)
        ltri = jnp.where(r > c, 1.0, 0.0).astype(BF16)
        before = jnp.dot(ltri, oh.astype(BF16), preferred_element_type=F32) + carry_ref[...]
        rank0 = jnp.sum(jnp.where(oh0, before, 0.0), axis=-1, keepdims=True)
        rank1 = jnp.sum(jnp.where(oh1, before, 0.0), axis=-1, keepdims=True)
        total = carry_ref[...] + counted * jnp.sum(oh, axis=0, keepdims=True)
        carry_ref[...] = total
        cnt_ref[...] = jnp.broadcast_to(total, cnt_ref.shape)

        mi = jnp.where(lane == 0, e0, jnp.where(lane == 1, e1, jnp.where(lane == 2, rank0, rank1)))
        mi_ref[...] = jnp.transpose(mi)[:SUBLANES, :].astype(jnp.int32)
        mf_ref[...] = jnp.where(lane == 0, g0, g1)

    @pl.when(i < n_prompt_tiles)
    def _():
        tile(xp_ref, attnp_ref)

    @pl.when(i >= n_prompt_tiles)
    def _():
        tile(xs_ref, attns_ref)


def _out_proj(attn_p, attn_s, conv_n, xp, xs, w_ob, gffn, w_r2, b_r):
    m = conv_n.shape[0]
    npt = xp.shape[0] // TM
    n_tiles = m // TM
    last_p, last_s, last = npt - 1, n_tiles - npt - 1, n_tiles - 1
    row = lambda i: (jnp.minimum(i, last), 0)
    prow = lambda i: (jnp.minimum(i, last_p), 0)
    srow = lambda i: (jnp.clip(i - npt, 0, last_s), 0)
    lag = lambda i: jnp.maximum(i - 1, 0)
    return pl.pallas_call(
        functools.partial(_out_proj_kernel, n_prompt_tiles=npt),
        grid=(n_tiles + 1,),
        in_specs=[
            pl.BlockSpec((TM, ATTN_W), prow),
            pl.BlockSpec((TM, ATTN_W), srow),
            pl.BlockSpec((TM, CONV_CH), row),
            pl.BlockSpec((TM, D_MODEL), prow),
            pl.BlockSpec((TM, D_MODEL), srow),
            pl.BlockSpec(memory_space=pl.ANY),
            _const_spec((1, D_MODEL)),
            _const_spec(w_r2.shape),
            _const_spec((1, LANES)),
        ],
        out_specs=[
            pl.BlockSpec((TM, D_MODEL), row),
            pl.BlockSpec((TM, D_MODEL // 2), row),
            pl.BlockSpec((SUBLANES, TM), lambda i: (0, lag(i))),
            pl.BlockSpec((TM, LANES), lambda i: (lag(i), 0)),
            pl.BlockSpec((SUBLANES, LANES), lambda i: (0, 0)),
        ],
        out_shape=[
            jax.ShapeDtypeStruct((m, D_MODEL), F32),
            jax.ShapeDtypeStruct((m, D_MODEL // 2), jnp.uint32),
            jax.ShapeDtypeStruct((SUBLANES, m), jnp.int32),
            jax.ShapeDtypeStruct((m, LANES), F32),
            jax.ShapeDtypeStruct((SUBLANES, LANES), F32),
        ],
        scratch_shapes=[pltpu.VMEM((1, LANES), F32), pltpu.VMEM((TM, LANES), F32),
                        pltpu.VMEM(w_ob.shape, BF16),
                        pltpu.VMEM((2, w_ob.shape[0] // W_O_SLABS, w_ob.shape[1]), F32),
                        pltpu.SemaphoreType.DMA((2,))],
        compiler_params=pltpu.CompilerParams(dimension_semantics=("arbitrary",),
                                             vmem_limit_bytes=VMEM_LIMIT),
        name="out_proj",
    )(attn_p, attn_s, conv_n, xp, xs, w_ob, gffn, w_r2, b_r)


def _dispatch_kernel(d0_ref, d1_ref, zlo_ref, zn_ref, nu_ref, xpk_ref, xs_hbm, zeros_ref, sems, *, n_blocks):
    i = pl.program_id(0)
    sem = sems.at[0]
    zsem = sems.at[1]

    def zero_fill(act):
        def per_expert(e, c):
            lo = zlo_ref[e]
            n = zn_ref[e]
            head = (-lo) & (SUBLANES - 1)
            for r in range(SUBLANES - 1):
                @pl.when(r < head)
                def _(r=r):
                    act(pltpu.make_async_copy(zeros_ref.at[pl.ds(0, 1)], xs_hbm.at[pl.ds(lo + r, 1)], zsem))
            off = lo + head
            rest = n - head
            size = MOE_BLOCK // 2
            while size >= SUBLANES:
                @pl.when((rest & size) != 0)
                def _(off=off, size=size):
                    dst = xs_hbm.at[pl.ds(pl.multiple_of(off, SUBLANES), size)]
                    act(pltpu.make_async_copy(zeros_ref.at[pl.ds(0, size)], dst, zsem))
                off = off + (rest & size)
                size //= 2
            return c

        def per_block(b, c):
            dst = xs_hbm.at[pl.ds(pl.multiple_of(b * MOE_BLOCK, MOE_BLOCK), MOE_BLOCK)]
            act(pltpu.make_async_copy(zeros_ref, dst, zsem))
            return c

        lax.fori_loop(0, N_EXPERTS, per_expert, 0)
        lax.fori_loop(nu_ref[0], n_blocks, per_block, 0)

    @pl.when(i == 0)
    def _():
        zeros_ref[...] = jnp.zeros(zeros_ref.shape, zeros_ref.dtype)
        zero_fill(lambda cp: cp.start())

    @pl.when(i == pl.num_programs(0) - 1)
    def _():
        zero_fill(lambda cp: cp.wait())

    base = i * TD

    def start(g, c):
        for u in range(SUBLANES):
            r = base + g * SUBLANES + u
            src = xpk_ref.at[g, pl.ds(u, 1)]
            pltpu.make_async_copy(src, xs_hbm.at[pl.ds(d0_ref[r], 1)], sem).start(priority=0)
            pltpu.make_async_copy(src, xs_hbm.at[pl.ds(d1_ref[r], 1)], sem).start(priority=1)
        return c

    lax.fori_loop(0, TD // SUBLANES, start, 0)
    for _ in range(2):
        pltpu.make_async_copy(xs_hbm.at[pl.ds(0, TD)], xs_hbm.at[pl.ds(0, TD)], sem).wait()


def _dispatch(dest0, dest1, pad_lo, n_pad, n_used, xpk, n_blocks):
    m = xpk.shape[0]
    grid_spec = pltpu.PrefetchScalarGridSpec(
        num_scalar_prefetch=5,
        grid=(m // TD,),
        in_specs=[pl.BlockSpec((TD // SUBLANES, SUBLANES, D_MODEL // 2), lambda i, *_: (i, 0, 0))],
        out_specs=pl.BlockSpec(memory_space=pl.ANY),
        scratch_shapes=[pltpu.VMEM((MOE_BLOCK, D_MODEL // 2), jnp.uint32),
                        pltpu.SemaphoreType.DMA((2,))],
    )
    return pl.pallas_call(
        functools.partial(_dispatch_kernel, n_blocks=n_blocks),
        grid_spec=grid_spec,
        out_shape=jax.ShapeDtypeStruct((n_blocks * MOE_BLOCK, D_MODEL // 2), jnp.uint32),
        compiler_params=pltpu.CompilerParams(dimension_semantics=("arbitrary",)),
        name="dispatch",
    )(dest0, dest1, pad_lo, n_pad, n_used, xpk.reshape(m // SUBLANES, SUBLANES, D_MODEL // 2))


def _experts_kernel(be_ref, nu_ref, nxt_ref, x_ref, wg_hbm, wu_hbm, wd_hbm, y_ref,
                    sg_ref, su_ref, sd_ref, wgb_ref, wub_ref, wdb_ref, sems):
    b = pl.program_id(0)
    active = b < nu_ref[0]
    new_expert = jnp.logical_or(b == 0, be_ref[b] != be_ref[jnp.maximum(b - 1, 0)])

    def weight_copies(e):
        return (pltpu.make_async_copy(wg_hbm.at[e], sg_ref, sems.at[0]),
                pltpu.make_async_copy(wu_hbm.at[e], su_ref, sems.at[1]),
                pltpu.make_async_copy(wd_hbm.at[e], sd_ref, sems.at[2]))

    @pl.when(b == 0)
    def _():
        for cp in weight_copies(be_ref[0]):
            cp.start()

    @pl.when(jnp.logical_and(active, new_expert))
    def _():
        for cp in weight_copies(be_ref[b]):
            cp.wait()
        wgb_ref[...] = sg_ref[...].astype(BF16)
        wub_ref[...] = su_ref[...].astype(BF16)
        wdb_ref[...] = sd_ref[...].astype(BF16)

        @pl.when(nxt_ref[b] >= 0)
        def _():
            for cp in weight_copies(nxt_ref[b]):
                cp.start()

    @pl.when(active)
    def _():
        half = D_MODEL // 2
        xa, xb = (v.astype(BF16) for v in _unpack_bf16_pairs(x_ref[...]))
        g = jnp.dot(xa, wgb_ref[:half, :], preferred_element_type=F32)
        g = g + jnp.dot(xb, wgb_ref[half:, :], preferred_element_type=F32)
        u = jnp.dot(xa, wub_ref[:half, :], preferred_element_type=F32)
        u = u + jnp.dot(xb, wub_ref[half:, :], preferred_element_type=F32)
        hmid = (g * jax.nn.sigmoid(g)) * u
        y = jnp.dot(hmid.astype(BF16), wdb_ref[...], preferred_element_type=F32)
        y_ref[...] = _pack_bf16_pairs(y)

    @pl.when(b >= nu_ref[0])
    def _():
        y_ref[...] = jnp.zeros(y_ref.shape, y_ref.dtype)


def _experts(block_e, n_used, next_e, x_sorted, w_gate, w_up, w_down):
    p = x_sorted.shape[0]
    nb = p // MOE_BLOCK

    def xrow(b, be, nu, nxt):
        return (jnp.maximum(jnp.minimum(b, nu[0] - 1), 0), 0)

    grid_spec = pltpu.PrefetchScalarGridSpec(
        num_scalar_prefetch=3,
        grid=(nb,),
        in_specs=[
            pl.BlockSpec((MOE_BLOCK, D_MODEL // 2), xrow),
            pl.BlockSpec(memory_space=pl.ANY),
            pl.BlockSpec(memory_space=pl.ANY),
            pl.BlockSpec(memory_space=pl.ANY),
        ],
        out_specs=pl.BlockSpec((MOE_BLOCK, D_MODEL // 2), lambda b, be, nu, nxt: (b, 0)),
        scratch_shapes=[pltpu.VMEM((D_MODEL, D_FF), F32), pltpu.VMEM((D_MODEL, D_FF), F32),
                        pltpu.VMEM((D_FF, D_MODEL), F32),
                        pltpu.VMEM((D_MODEL, D_FF), BF16), pltpu.VMEM((D_MODEL, D_FF), BF16),
                        pltpu.VMEM((D_FF, D_MODEL), BF16),
                        pltpu.SemaphoreType.DMA((3,))],
    )
    return pl.pallas_call(
        _experts_kernel,
        grid_spec=grid_spec,
        out_shape=jax.ShapeDtypeStruct((p, D_MODEL // 2), jnp.uint32),
        compiler_params=pltpu.CompilerParams(dimension_semantics=("arbitrary",),
                                             vmem_limit_bytes=VMEM_LIMIT),
        name="experts",
    )(block_e, n_used, next_e, x_sorted, w_gate, w_up, w_down)


def _combine_kernel(d0_ref, d1_ref, h_ref, mf_ref, gfin_ref, y_hbm, outp_ref, outs_ref, y0_ref, y1_ref, sems,
                    *, n_tiles, n_prompt_tiles):
    i = pl.program_id(0)

    def gather(tile, slot):
        base = tile * TM

        def body(g, c):
            for u in range(SUBLANES):
                r = base + g * SUBLANES + u
                pltpu.make_async_copy(y_hbm.at[pl.ds(d0_ref[r], 1)], y0_ref.at[slot, g, pl.ds(u, 1)],
                                      sems.at[slot]).start(priority=0)
                pltpu.make_async_copy(y_hbm.at[pl.ds(d1_ref[r], 1)], y1_ref.at[slot, g, pl.ds(u, 1)],
                                      sems.at[slot]).start(priority=1)
            return c
        lax.fori_loop(0, TM // SUBLANES, body, 0)

    @pl.when(i == 0)
    def _():
        gather(0, 0)

    @pl.when(i + 1 < n_tiles)
    def _():
        gather(i + 1, (i + 1) % 2)

    slot = i % 2
    for _ in range(2):
        pltpu.make_async_copy(y_hbm.at[pl.ds(0, TM)], y_hbm.at[pl.ds(0, TM)], sems.at[slot]).wait()

    def finish(out_ref):
        mf = mf_ref[...]
        g0, g1 = mf[:, :, 0:1], mf[:, :, 1:2]
        half = D_MODEL // 2
        a0, b0 = _unpack_bf16_pairs(y0_ref[slot])
        a1, b1 = _unpack_bf16_pairs(y1_ref[slot])
        o_lo = h_ref[:, :, :half] + (g0 * a0 + g1 * a1)
        o_hi = h_ref[:, :, half:] + (g0 * b0 + g1 * b1)
        sumsq = jnp.sum(o_lo * o_lo, axis=-1, keepdims=True) + jnp.sum(o_hi * o_hi, axis=-1, keepdims=True)
        inv_rms = lax.rsqrt(sumsq / D_MODEL + EPS)
        out_ref[:, :, :half] = o_lo * inv_rms * gfin_ref[:, :, :half]
        out_ref[:, :, half:] = o_hi * inv_rms * gfin_ref[:, :, half:]

    @pl.when(i < n_prompt_tiles)
    def _():
        finish(outp_ref)

    @pl.when(i >= n_prompt_tiles)
    def _():
        finish(outs_ref)


def _combine(dest0, dest1, h, mf, gfin, y_sorted, *, n_prompt_rows):
    m = h.shape[0]
    npt = n_prompt_rows // TM
    tg = TM // SUBLANES
    grouped = lambda a: a.reshape(a.shape[0] // SUBLANES, SUBLANES, a.shape[1])
    grid_spec = pltpu.PrefetchScalarGridSpec(
        num_scalar_prefetch=2,
        grid=(m // TM,),
        in_specs=[
            pl.BlockSpec((tg, SUBLANES, D_MODEL), lambda i, *_: (i, 0, 0)),
            pl.BlockSpec((tg, SUBLANES, LANES), lambda i, *_: (i, 0, 0)),
            pl.BlockSpec((1, 1, D_MODEL), lambda i, *_: (0, 0, 0)),
            pl.BlockSpec(memory_space=pl.ANY),
        ],
        out_specs=[pl.BlockSpec((tg, SUBLANES, D_MODEL), lambda i, *_: (jnp.minimum(i, npt - 1), 0, 0)),
                   pl.BlockSpec((tg, SUBLANES, D_MODEL), lambda i, *_: (jnp.maximum(i - npt, 0), 0, 0))],
        scratch_shapes=[pltpu.VMEM((2, tg, SUBLANES, D_MODEL // 2), jnp.uint32),
                        pltpu.VMEM((2, tg, SUBLANES, D_MODEL // 2), jnp.uint32),
                        pltpu.SemaphoreType.DMA((2,))],
    )
    y_p, y_s = pl.pallas_call(
        functools.partial(_combine_kernel, n_tiles=m // TM, n_prompt_tiles=npt),
        grid_spec=grid_spec,
        out_shape=[jax.ShapeDtypeStruct((n_prompt_rows // SUBLANES, SUBLANES, D_MODEL), F32),
                   jax.ShapeDtypeStruct(((m - n_prompt_rows) // SUBLANES, SUBLANES, D_MODEL), F32)],
        compiler_params=pltpu.CompilerParams(dimension_semantics=("arbitrary",),
                                             vmem_limit_bytes=VMEM_LIMIT),
        name="combine",
    )(dest0, dest1, grouped(h), grouped(mf), gfin.reshape(1, 1, D_MODEL), y_sorted)
    return y_p.reshape(n_prompt_rows, D_MODEL), y_s.reshape(m - n_prompt_rows, D_MODEL)


def _rope_tables(pos):
    f32 = np.float32
    inv = np.power(f32(ROPE_THETA), -np.arange(0, ROPE_DIM, 2, dtype=f32) / f32(ROPE_DIM)).astype(f32)
    ang = (pos.astype(f32)[:, None] * inv[None, :]).astype(f32)
    cos, sin = np.cos(ang).astype(f32), np.sin(ang).astype(f32)
    return np.concatenate([cos, cos], axis=-1), np.concatenate([-sin, sin], axis=-1)


def _swap_halves(w):
    return jnp.concatenate([w[..., ROPE_DIM // 2:], w[..., :ROPE_DIM // 2]], axis=-1)


def kernel(x_prompt, x_sample, cache_kv_latent, cache_k_rope, state_conv, norm_mix, w_in, norm_q, w_uq,
           norm_kv, w_uk, w_uv, conv_w, norm_attn_out, norm_conv_out, w_o, norm_ffn, w_router_group,
           b_router_group, w_router_expert, b_router_expert, w_gate, w_up, w_down, norm_final):
    assert w_in.shape[0] == 1, "single-layer trunk"
    bp, seq_p, _ = x_prompt.shape
    bs, seq_s, _ = x_sample.shape
    past_len = cache_kv_latent.shape[2]
    np_rows, ns_rows = bp * seq_p, bs * seq_s
    m = np_rows + ns_rows
    assert seq_p % TM == 0 and TM % seq_s == 0 and ns_rows % TM == 0 and seq_s == CHUNK
    assert m % TD == 0

    xp = x_prompt.reshape(np_rows, D_MODEL)
    xs = x_sample.reshape(ns_rows, D_MODEL)
    row_vec = lambda v: v.reshape(1, -1)

    assert w_in.shape[2] == Q_LORA + KV_LORA + ROPE_DIM + 3 * CONV_CH
    w_t = jnp.swapaxes(w_in[0], 0, 1)
    assert w_t.shape[0] % (W_IN_SLABS * 2 * SUBLANES) == 0
    wq4 = w_uq[0].reshape(Q_LORA, N_HEADS, QK_NOPE + ROPE_DIM)
    wq_rope = wq4[:, :, QK_NOPE:]
    w_q = jnp.concatenate([wq4[:, :, :QK_NOPE].reshape(Q_LORA, -1), wq_rope.reshape(Q_LORA, -1),
                           _swap_halves(wq_rope).reshape(Q_LORA, -1)], axis=1).astype(BF16)
    w_ukt = jnp.transpose(w_uk[0], (1, 2, 0)).astype(BF16)
    w_uvh = jnp.transpose(w_uv[0], (1, 0, 2)).astype(BF16)
    w_ob = w_o[0]
    n_router = N_GROUPS + N_EXPERTS
    w_r = jnp.concatenate([w_router_group[0], w_router_expert[0].reshape(D_MODEL, N_EXPERTS)], axis=1)
    w_r = jnp.pad(w_r, ((0, 0), (0, LANES - n_router)))
    w_rh = w_r.astype(BF16)
    w_rl = (w_r - w_rh.astype(F32)).astype(BF16)
    w_r2 = jnp.concatenate([w_rh, w_rl], axis=1)
    b_r =jnp.pad(jnp.concatenate([b_router_group[0], b_router_expert[0].reshape(N_EXPERTS)]),
                  (0, LANES - n_router)).reshape(1, LANES)

    cos_p, sin_p = _rope_tables(np.arange(seq_p))
    cos_s, sin_s = _rope_tables(past_len + np.arange(seq_s))
    cosk = np.concatenate([cos_p, np.tile(cos_s, (TM // seq_s, 1))], axis=0)
    sink = np.concatenate([sin_p, np.tile(sin_s, (TM // seq_s, 1))], axis=0)
    state = jnp.concatenate([jnp.zeros((bp, CONV_W - 1, CONV_CH), F32), state_conv[0]], axis=0)

    cqn, ckv_p, kr_p, ckv_s, kr_s, conv_n, utail = _in_proj(
        xp, xs, row_vec(norm_mix[0]), w_t, row_vec(norm_q[0]), row_vec(norm_kv[0]),
        row_vec(norm_conv_out[0]), conv_w[0], cosk, sink, state, seq_p=seq_p, seq_s=seq_s)

    gao = row_vec(norm_attn_out[0])
    w_kv = jnp.concatenate([w_uk[0].reshape(KV_LORA, N_HEADS * QK_NOPE),
                            w_uv[0].reshape(KV_LORA, N_HEADS * V_DIM)], axis=1).astype(BF16)
    attn_p = _attention_heads(cqn, w_q, w_kv, np.tile(cos_p, (1, N_HEADS)), np.tile(sin_p, (1, N_HEADS)),
                              gao, ckv_p, kr_p, n_batch=bp, seq=seq_p)
    attn_s = _attention(cqn, w_q, w_ukt, w_uvh, np.tile(cos_s, (1, N_HEADS)), np.tile(sin_s, (1, N_HEADS)),
                        gao, ckv_s, kr_s, cache_kv_latent[0], jnp.swapaxes(cache_k_rope[0], 1, 2),
                        n_batch=bs, seq=seq_s, row0=np_rows)

    h, xpk, mi, mf, cnt = _out_proj(attn_p, attn_s, conv_n, xp, xs, w_ob, row_vec(norm_ffn[0]),
                                    w_r2, b_r)

    counts = cnt[0, :N_EXPERTS].astype(jnp.int32)
    padded = (counts + MOE_BLOCK - 1) // MOE_BLOCK * MOE_BLOCK
    pad_end = jnp.cumsum(padded)
    pad_start = pad_end - padded
    n_blocks = -(-(m * 2) // MOE_BLOCK) + N_EXPERTS
    block_row0 = jnp.arange(n_blocks, dtype=jnp.int32) * MOE_BLOCK
    block_e = jnp.minimum(jnp.sum((pad_end[None, :] <= block_row0[:, None]).astype(jnp.int32), axis=1),
                          N_EXPERTS - 1)
    n_used = (pad_end[-1:] // MOE_BLOCK).astype(jnp.int32)
    expert_ids = jnp.arange(N_EXPERTS, dtype=jnp.int32)[:, None]

    def seg_start(e):
        return jnp.sum(jnp.where(expert_ids == e[None, :], pad_start[:, None], 0), axis=0)

    dest0 = seg_start(mi[0]) + mi[2]
    dest1 = seg_start(mi[1]) + mi[3]

    x_sorted = _dispatch(dest0, dest1, pad_start + counts, padded - counts, n_used, xpk, n_blocks)
    later = (expert_ids.T > block_e[:, None]) & (padded > 0)[None, :]
    next_e = jnp.min(jnp.where(later, expert_ids.T, N_EXPERTS), axis=1)
    next_e = jnp.where(next_e == N_EXPERTS, -1, next_e).astype(jnp.int32)
    y_sorted = _experts(block_e, n_used, next_e, x_sorted, w_gate[0], w_up[0], w_down[0])
    gfin = row_vec(norm_final)
    y_p, y_s = _combine(dest0, dest1, h, mf, gfin, y_sorted, n_prompt_rows=np_rows)

    ut = utail.reshape(m // CHUNK, SUBLANES, CONV_CH)
    tails = ut[:, SUBLANES - (CONV_W - 1):, :]
    p_last = (jnp.arange(bp) + 1) * (seq_p // CHUNK) - 1
    s_last = np_rows // CHUNK + (jnp.arange(bs) + 1) * (seq_s // CHUNK) - 1
    return (y_p.reshape(bp, seq_p, D_MODEL),
            y_s.reshape(bs, seq_s, D_MODEL),
            ckv_p.reshape(1, bp, seq_p, KV_LORA),
            jnp.swapaxes(kr_p, 1, 2)[None],
            tails[p_last][None],
            ckv_s.reshape(1, bs, seq_s, KV_LORA),
            jnp.swapaxes(kr_s, 1, 2)[None],
            tails[s_last][None])
```

```python
import functools

import jax
import jax.numpy as jnp
import numpy as np
from jax import lax
from jax.experimental import pallas as pl
from jax.experimental.pallas import tpu as pltpu

F32 = jnp.float32
BF16 = jnp.bfloat16

D_MODEL = 2048
N_HEADS = 8
QK_NOPE = 128
ROPE_DIM = 64
V_DIM = 128
Q_LORA = 512
KV_LORA = 512
ATTN_W = N_HEADS * V_DIM
CONV_CH = D_MODEL - ATTN_W
CONV_W = 3
CHUNK = 64
N_GROUPS = 4
EXPERTS_PER_GROUP = 8
N_EXPERTS = N_GROUPS * EXPERTS_PER_GROUP
D_FF = 512
ROPE_THETA = 10000.0
EPS = 1e-6
ATTN_SCALE = (QK_NOPE + ROPE_DIM) ** -0.5
EXP2_SCALE = ATTN_SCALE * 1.4426950408889634

LANES = 128
SUBLANES = 8
TM = 256
TD = 2304
MOE_BLOCK = 256
TQ = 512
TK = 512
TKH = 512
TKX = 256
W_O_SLABS = 4
W_IN_SLABS = 10
NEG_BIG = -1e30
V7X_VMEM_BYTES = 64 * 1024 * 1024
VMEM_LIMIT = V7X_VMEM_BYTES * 7 // 8


def _rms(v, g):
    return v * lax.rsqrt(jnp.mean(v * v, axis=-1, keepdims=True) + EPS) * g


def _lane_bcast(v, width):
    if width % LANES == 0:
        return jnp.concatenate([v] * (width // LANES), axis=1)
    assert width < LANES
    return v[:, :width]


def _pack_bf16_pairs(v):
    half = v.shape[-1] // 2
    lo = lax.bitcast_convert_type(v[..., :half].astype(BF16).astype(F32), jnp.uint32)
    hi = lax.bitcast_convert_type(v[..., half:].astype(BF16).astype(F32), jnp.uint32)
    return (lo >> 16) | (hi & jnp.uint32(0xFFFF0000))


def _unpack_bf16_pairs(w):
    return (lax.bitcast_convert_type(w << 16, F32),
            lax.bitcast_convert_type(w & jnp.uint32(0xFFFF0000), F32))


def _load_weight_as_bf16(w_hbm, dst_ref, stage_ref, sems):
    rows = stage_ref.shape[1]
    n_slabs = dst_ref.shape[0] // rows

    def slab_copy(c):
        return pltpu.make_async_copy(w_hbm.at[pl.ds(c * rows, rows)], stage_ref.at[c % 2], sems.at[c % 2])

    slab_copy(0).start()
    for c in range(n_slabs):
        if c + 1 < n_slabs:
            slab_copy(c + 1).start()
        slab_copy(c).wait()
        dst_ref[c * rows:(c + 1) * rows, :] = stage_ref[c % 2].astype(BF16)


def _const_spec(shape):
    nd = len(shape)
    return pl.BlockSpec(shape, lambda *_: (0,) * nd, pipeline_mode=pl.Buffered(1))


def _in_proj_kernel(xp_ref, xs_ref, gmix_ref, w_hbm, gq_ref, gkv_ref, gco_ref, convw_ref,
                    cos_ref, sin_ref, state_ref,
                    cqn_ref, ckvp_ref, krp_ref, ckvs_ref, krs_ref, convn_ref, utail_ref,
                    ext_ref, wt_ref, stage_ref, wsems,
                    *, n_prompt_tiles, tiles_per_seq, n_prompt_seq, sample_seq_len):
    i = pl.program_id(0)

    @pl.when(i == 0)
    def _():
        ext_ref[...] = jnp.zeros(ext_ref.shape, F32)
        _load_weight_as_bf16(w_hbm, wt_ref, stage_ref, wsems)

    def conv_block(u_sub, gate_sub, row0, length):
        ext_ref[SUBLANES:SUBLANES + length, :] = u_sub
        um1 = ext_ref[SUBLANES - 1:SUBLANES - 1 + length, :]
        um2 = ext_ref[SUBLANES - 2:SUBLANES - 2 + length, :]
        cw = convw_ref[...]
        conv = cw[0:1] * um2 + cw[1:2] * um1 + cw[2:3] * u_sub
        convn_ref[row0:row0 + length, :] = _rms(gate_sub * conv, gco_ref[...]).astype(BF16)

    def tile(x_ref, is_prompt):
        ckv_ref, krt_ref = (ckvp_ref, krp_ref) if is_prompt else (ckvs_ref, krs_ref)
        x = x_ref[...]
        xg = (x * gmix_ref[...]).astype(BF16)
        inv_rms = lax.rsqrt(jnp.mean(x * x, axis=-1, keepdims=True) + EPS)
        lat_w = Q_LORA + KV_LORA
        conv0 = lat_w + ROPE_DIM
        nt = (((1,), (1,)), ((), ()))

        def project(lo, hi):
            return inv_rms * lax.dot_general(xg, wt_ref[lo:hi, :], nt, preferred_element_type=F32)

        z_ch = project(conv0 + CONV_CH, conv0 + 3 * CONV_CH)
        u = z_ch[:, :CONV_CH] * z_ch[:, CONV_CH:]
        for j in range(TM // CHUNK):
            utail_ref[j] = u[CHUNK * (j + 1) - SUBLANES:CHUNK * (j + 1), :]
        gate_b = project(conv0, conv0 + CONV_CH)

        if is_prompt:
            first = (i % tiles_per_seq) == 0
            carried = ext_ref[TM + SUBLANES - 2:TM + SUBLANES, :]
            ext_ref[SUBLANES - 2:SUBLANES, :] = jnp.where(first, state_ref[i // tiles_per_seq], carried)
            conv_block(u, gate_b, 0, TM)
        else:
            n_sub = TM // sample_seq_len
            seq0 = n_prompt_seq + (i - n_prompt_tiles) * n_sub
            for k in range(n_sub):
                ext_ref[SUBLANES - 2:SUBLANES, :] = state_ref[seq0 + k]
                lo = k * sample_seq_len
                conv_block(u[lo:lo + sample_seq_len], gate_b[lo:lo + sample_seq_len], lo, sample_seq_len)

        zk = project(lat_w, conv0)
        zk_swapped = jnp.concatenate([zk[:, ROPE_DIM // 2:], zk[:, :ROPE_DIM // 2]], axis=1)
        k_rope = zk * cos_ref[...] + zk_swapped * sin_ref[...]
        if is_prompt:
            krt_ref[...] = k_rope.T
        else:
            for k in range(TM // sample_seq_len):
                krt_ref[k] = k_rope[k * sample_seq_len:(k + 1) * sample_seq_len, :].T
        ckv_ref[...] = _rms(project(Q_LORA, lat_w), gkv_ref[...])
        cqn_ref[...] = _rms(project(0, Q_LORA), gq_ref[...]).astype(BF16)

    @pl.when(i < n_prompt_tiles)
    def _():
        tile(xp_ref, True)

    @pl.when(i >= n_prompt_tiles)
    def _():
        tile(xs_ref, False)


def _in_proj(xp, xs, gmix, w_t, gq, gkv, gco, convw, cosk, sink, state, *, seq_p, seq_s):
    np_rows, ns_rows = xp.shape[0], xs.shape[0]
    m = np_rows + ns_rows
    npt, nst = np_rows // TM, ns_rows // TM
    tps = seq_p // TM
    n_prompt_seq = np_rows // seq_p
    last_p = npt - 1

    def tab_idx(i):
        return (jnp.where(i < npt, i % tps, tps), 0)

    row = lambda i: (i, 0)
    prow = lambda i: (jnp.minimum(i, last_p), 0)
    srow = lambda i: (jnp.maximum(i - npt, 0), 0)
    kern = functools.partial(_in_proj_kernel, n_prompt_tiles=npt, tiles_per_seq=tps,
                             n_prompt_seq=n_prompt_seq, sample_seq_len=seq_s)
    return pl.pallas_call(
        kern,
        grid=(npt + nst,),
        in_specs=[
            pl.BlockSpec((TM, D_MODEL), prow),
            pl.BlockSpec((TM, D_MODEL), srow),
            _const_spec((1, D_MODEL)),
            pl.BlockSpec(memory_space=pl.ANY),
            _const_spec((1, Q_LORA)),
            _const_spec((1, KV_LORA)),
            _const_spec((1, CONV_CH)),
            _const_spec((CONV_W, CONV_CH)),
            pl.BlockSpec((TM, ROPE_DIM), tab_idx),
            pl.BlockSpec((TM, ROPE_DIM), tab_idx),
            _const_spec(state.shape),
        ],
        out_specs=[
            pl.BlockSpec((TM, Q_LORA), row),
            pl.BlockSpec((TM, KV_LORA), prow),
            pl.BlockSpec((None, ROPE_DIM, TM), lambda i: (jnp.minimum(i, last_p) // tps, 0,
                                                          jnp.minimum(i, last_p) % tps)),
            pl.BlockSpec((TM, KV_LORA), srow),
            pl.BlockSpec((TM // seq_s, ROPE_DIM, seq_s), lambda i: (jnp.maximum(i - npt, 0), 0, 0)),
            pl.BlockSpec((TM, CONV_CH), row),
            pl.BlockSpec((TM // CHUNK, SUBLANES, CONV_CH), lambda i: (i, 0, 0)),
        ],
        out_shape=[
            jax.ShapeDtypeStruct((m, Q_LORA), BF16),
            jax.ShapeDtypeStruct((np_rows, KV_LORA), F32),
            jax.ShapeDtypeStruct((n_prompt_seq, ROPE_DIM, seq_p), F32),
            jax.ShapeDtypeStruct((ns_rows, KV_LORA), F32),
            jax.ShapeDtypeStruct((ns_rows // seq_s, ROPE_DIM, seq_s), F32),
            jax.ShapeDtypeStruct((m, CONV_CH), BF16),
            jax.ShapeDtypeStruct((m // CHUNK, SUBLANES, CONV_CH), F32),
        ],
        scratch_shapes=[pltpu.VMEM((TM + SUBLANES, CONV_CH), F32),
                        pltpu.VMEM(w_t.shape, BF16),
                        pltpu.VMEM((2, w_t.shape[0] // W_IN_SLABS, w_t.shape[1]), F32),
                        pltpu.SemaphoreType.DMA((2,))],
        compiler_params=pltpu.CompilerParams(dimension_semantics=("arbitrary",),
                                             vmem_limit_bytes=VMEM_LIMIT),
        name="in_proj",
    )(xp, xs, gmix, w_t, gq, gkv, gco, convw, cosk, sink, state)


def _attn_kernel(cqn_ref, wq_ref, wuk_ref, wuv_ref, cos_ref, sin_ref, gao_ref, pkv_ref, pkr_ref, kv_ref, kr_ref,
                 out_ref, qlat_ref, qr_ref, m_ref, l_ref, acc_ref, s_ref, *, tq, n_past):
    rows = N_HEADS * tq

    q = jnp.dot(cqn_ref[...], wq_ref[...], preferred_element_type=F32)
    nope_w = N_HEADS * QK_NOPE
    rope_w = N_HEADS * ROPE_DIM
    qrope = q[:, nope_w:nope_w + rope_w] * cos_ref[...] + q[:, nope_w + rope_w:] * sin_ref[...]
    for h in range(N_HEADS):
        qn = q[:, h * QK_NOPE:(h + 1) * QK_NOPE].astype(BF16)
        ql = jnp.dot(qn, wuk_ref[h], preferred_element_type=F32)
        qlat_ref[h * tq:(h + 1) * tq, :] = (ql * EXP2_SCALE).astype(BF16)
        qr_ref[h * tq:(h + 1) * tq, :] = (qrope[:, h * ROPE_DIM:(h + 1) * ROPE_DIM] * EXP2_SCALE).astype(BF16)


    nt = (((1,), (1,)), ((), ()))

    def scores(kc_f32, krt_f32):
        s = lax.dot_general(qlat_ref[...], kc_f32.astype(BF16), nt, preferred_element_type=F32)
        return s + jnp.dot(qr_ref[...], krt_f32.astype(BF16), preferred_element_type=F32)

    def update(s, kc_f32, mask, first=False):
        if mask is not None:
            s = jnp.where(mask, s, NEG_BIG)
        m_cur = jnp.max(s, axis=-1, keepdims=True)
        if first:
            m_new = jnp.broadcast_to(m_cur, m_ref.shape)
        else:
            m_prev = m_ref[...]
            m_new = jnp.maximum(m_prev, m_cur)
            alpha = jnp.exp2(m_prev - m_new)
        p = jnp.exp2(s - _lane_bcast(m_new, s.shape[1]))
        l_cur = jnp.sum(p, axis=-1, keepdims=True)
        pv = jnp.dot(p.astype(BF16), kc_f32.astype(BF16), preferred_element_type=F32)
        if first:
            l_ref[...] = jnp.broadcast_to(l_cur, l_ref.shape)
            acc_ref[...] = pv
        else:
            l_ref[...] = alpha * l_ref[...] + l_cur
            acc_ref[...] = _lane_bcast(alpha, KV_LORA) * acc_ref[...] + pv
        m_ref[...] = m_new

    def pipelined(kv, kr, lo, hi, last, mask_fn):
        def body(j, c):
            k0 = pl.multiple_of(j * TK, TK)
            k1 = pl.multiple_of(jnp.minimum(j + 1, last) * TK, TK)
            s_cur = s_ref[j % 2]
            s_ref[(j + 1) % 2] = scores(kv[pl.ds(k1, TK), :], kr[:, pl.ds(k1, TK)])
            update(s_cur, kv[pl.ds(k0, TK), :], None if mask_fn is None else mask_fn(k0))
            return c
        lax.fori_loop(lo, hi, body, 0)

    def pipelined_pairs(kv, kr, n_pairs, last):
        def body(i, c):
            ka = pl.multiple_of((2 * i + 1) * TK, TK)
            kb = pl.multiple_of((2 * i + 2) * TK, TK)
            kc = pl.multiple_of(jnp.minimum(2 * i + 3, last) * TK, TK)
            s_ref[0] = scores(kv[pl.ds(kb, TK), :], kr[:, pl.ds(kb, TK)])
            update(s_ref[1], kv[pl.ds(ka, TK), :], None)
            s_ref[1] = scores(kv[pl.ds(kc, TK), :], kr[:, pl.ds(kc, TK)])
            update(s_ref[0], kv[pl.ds(kb, TK), :], None)
            return c
        lax.fori_loop(0, n_pairs, body, 0)

    def first_block(kv, kr, last, mask):
        k1 = pl.multiple_of(jnp.minimum(1, last) * TK, TK)
        s_ref[0] = scores(kv[pl.ds(0, TK), :], kr[:, pl.ds(0, TK)])
        s_ref[1] = scores(kv[pl.ds(k1, TK), :], kr[:, pl.ds(k1, TK)])
        update(s_ref[0], kv[pl.ds(0, TK), :], mask, first=True)

    n_pb = n_past // TK
    first_block(pkv_ref, pkr_ref, n_pb - 1, None)
    n_pairs = (n_pb - 1) // 2
    pipelined_pairs(pkv_ref, pkr_ref, n_pairs, n_pb - 1)
    if 1 + 2 * n_pairs < n_pb:
        pipelined(pkv_ref, pkr_ref, 1 + 2 * n_pairs, n_pb, n_pb - 1, None)

    update(scores(kv_ref[...], kr_ref[...]), kv_ref[...], None)

    o = acc_ref[...] / _lane_bcast(l_ref[...], KV_LORA)
    parts = []
    for h in range(N_HEADS):
        oh = o[h * tq:(h + 1) * tq, :].astype(BF16)
        parts.append(jnp.dot(oh, wuv_ref[h], preferred_element_type=F32))
    attn = jnp.concatenate(parts, axis=-1)
    out_ref[...] = _rms(attn, gao_ref[...]).astype(BF16)


def _attention(cqn, w_q, w_ukt, w_uv, cosq, sinq, gao, ckv, krope, past_kv, past_kr, *, n_batch, seq, row0):
    tq = seq
    n_past = past_kv.shape[1]
    assert n_past % CHUNK == 0 and seq <= CHUNK and n_past % TK == 0
    blk0 = row0 // tq
    in_specs = [
        pl.BlockSpec((tq, Q_LORA), lambda b, q: (blk0 + b, 0)),
        _const_spec(w_q.shape),
        _const_spec(w_ukt.shape),
        _const_spec(w_uv.shape),
        pl.BlockSpec((tq, N_HEADS * ROPE_DIM), lambda b, q: (0, 0)),
        pl.BlockSpec((tq, N_HEADS * ROPE_DIM), lambda b, q: (0, 0)),
        _const_spec((1, ATTN_W)),
        pl.BlockSpec((None, n_past, KV_LORA), lambda b, q: (b, 0, 0)),
        pl.BlockSpec((None, ROPE_DIM, n_past), lambda b, q: (b, 0, 0)),
        pl.BlockSpec((seq, KV_LORA), lambda b, q: (b, 0)),
        pl.BlockSpec((None, ROPE_DIM, seq), lambda b, q: (b, 0, 0)),
    ]
    args = [cqn, w_q, w_ukt, w_uv, cosq, sinq, gao, past_kv, past_kr, ckv, krope]
    rows = N_HEADS * tq
    return pl.pallas_call(
        functools.partial(_attn_kernel, tq=tq, n_past=n_past),
        grid=(n_batch, 1),
        in_specs=in_specs,
        out_specs=pl.BlockSpec((tq, ATTN_W), lambda b, q: (b, 0)),
        out_shape=jax.ShapeDtypeStruct((n_batch * seq, ATTN_W), BF16),
        scratch_shapes=[
            pltpu.VMEM((rows, KV_LORA), BF16),
            pltpu.VMEM((rows, ROPE_DIM), BF16),
            pltpu.VMEM((rows, LANES), F32),
            pltpu.VMEM((rows, LANES), F32),
            pltpu.VMEM((rows, KV_LORA), F32),
            pltpu.VMEM((2, rows, TK), F32),
        ],
        compiler_params=pltpu.CompilerParams(dimension_semantics=("arbitrary", "arbitrary"),
                                             vmem_limit_bytes=VMEM_LIMIT),
        name="attn_sample",
    )(*args)


def _attn_heads_kernel(cqn_ref, wq_ref, wkv_ref, cos_ref, sin_ref, gao_ref, kv_ref, krt_ref, out_ref,
                       kcat_ref, vh_ref, qcat_ref, m_ref, l_ref, acc_ref, klim_ref, *, tq, seq):
    qi = pl.program_id(1)
    nt = (((1,), (1,)), ((), ()))
    kw = QK_NOPE + ROPE_DIM
    kpad = kcat_ref.shape[-1]

    @pl.when(qi == 0)
    def _():
        def expand(j, c):
            k0 = pl.multiple_of(j * TKX, TKX)
            latent = kv_ref[pl.ds(k0, TKX), :].astype(BF16)
            kvh = jnp.dot(latent, wkv_ref[...], preferred_element_type=F32)
            k_rope = krt_ref[:, pl.ds(k0, TKX)].T.astype(BF16)
            for h in range(N_HEADS):
                kcat_ref[h, pl.ds(k0, TKX), :QK_NOPE] = kvh[:, h * QK_NOPE:(h + 1) * QK_NOPE].astype(BF16)
                kcat_ref[h, pl.ds(k0, TKX), QK_NOPE:kw] = k_rope
                kcat_ref[h, pl.ds(k0, TKX), kw:] = jnp.zeros((TKX, kpad - kw), BF16)
                v0 = N_HEADS * QK_NOPE + h * V_DIM
                vh_ref[h, pl.ds(k0, TKX), :] = kvh[:, v0:v0 + V_DIM].astype(BF16)
            return c
        lax.fori_loop(0, seq // TKX, expand, 0)

    q = jnp.dot(cqn_ref[...], wq_ref[...], preferred_element_type=F32)
    nope_w = N_HEADS * QK_NOPE
    rope_w = N_HEADS * ROPE_DIM
    qrope = q[:, nope_w:nope_w + rope_w] * cos_ref[...] + q[:, nope_w + rope_w:] * sin_ref[...]
    for h in range(N_HEADS):
        qcat_ref[h, :, :QK_NOPE] = (q[:, h * QK_NOPE:(h + 1) * QK_NOPE] * EXP2_SCALE).astype(BF16)
        qcat_ref[h, :, QK_NOPE:kw] = (qrope[:, h * ROPE_DIM:(h + 1) * ROPE_DIM] * EXP2_SCALE).astype(BF16)
        qcat_ref[h, :, kw:] = jnp.zeros((tq, kpad - kw), BF16)

    r = lax.broadcasted_iota(jnp.int32, (tq, LANES), 0)
    klim_ref[...] = ((qi * tq + r) & ~(CHUNK - 1)) + CHUNK

    tk = TKH

    def block(k0, masked, first):
        if masked:
            cidx = lax.broadcasted_iota(jnp.int32, (tq, tk), 1)
            mask = cidx < _lane_bcast(klim_ref[...] - k0, tk)
        for h in range(N_HEADS):
            s = lax.dot_general(qcat_ref[h], kcat_ref[h, pl.ds(k0, tk), :], nt, preferred_element_type=F32)
            if masked:
                s = jnp.where(mask, s, NEG_BIG)
            m_cur = jnp.max(s, axis=-1, keepdims=True)
            if first:
                m_new = jnp.broadcast_to(m_cur, (tq, LANES))
            else:
                m_prev = m_ref[h]
                m_new = jnp.maximum(m_prev, m_cur)
                alpha = jnp.exp2(m_prev - m_new)
            p = jnp.exp2(s - _lane_bcast(m_new, tk))
            l_cur = jnp.sum(p, axis=-1, keepdims=True)
            pv = jnp.dot(p.astype(BF16), vh_ref[h, pl.ds(k0, tk), :], preferred_element_type=F32)
            if first:
                l_ref[h] = jnp.broadcast_to(l_cur, (tq, LANES))
                acc_ref[h] = pv
            else:
                l_ref[h] = alpha * l_ref[h] + l_cur
                acc_ref[h] = _lane_bcast(alpha, V_DIM) * acc_ref[h] + pv
            m_ref[h] = m_new

    n_blocks = ((qi + 1) * tq + tk - 1) // tk
    n_full = jnp.minimum((qi * tq // CHUNK + 1) * CHUNK // tk, n_blocks)

    def loop(lo, hi, masked):
        def body(j, c):
            block(pl.multiple_of(j * tk, tk), masked, False)
            return c
        lax.fori_loop(lo, hi, body, 0)

    @pl.when(n_full == 0)
    def _():
        block(0, True, True)

    @pl.when(n_full > 0)
    def _():
        block(0, False, True)

    loop(1, n_full, False)
    loop(jnp.maximum(n_full, 1), n_blocks, True)

    attn = jnp.concatenate([acc_ref[h] / _lane_bcast(l_ref[h], V_DIM) for h in range(N_HEADS)], axis=-1)
    out_ref[...] = _rms(attn, gao_ref[...]).astype(BF16)


def _attention_heads(cqn, w_q, w_kv, cosq, sinq, gao, ckv, krope_t, *, n_batch, seq):
    nq = seq // TQ
    kpad = 2 * LANES
    assert QK_NOPE + ROPE_DIM <= kpad and V_DIM == LANES
    assert seq % TQ == 0 and seq % TKH == 0 and seq % TKX == 0 and TQ % CHUNK == 0 and TKH % CHUNK == 0
    return pl.pallas_call(
        functools.partial(_attn_heads_kernel, tq=TQ, seq=seq),
        grid=(n_batch, nq),
        in_specs=[
            pl.BlockSpec((TQ, Q_LORA), lambda b, q: (b * nq + q, 0)),
            _const_spec(w_q.shape),
            _const_spec(w_kv.shape),
            pl.BlockSpec((TQ, N_HEADS * ROPE_DIM), lambda b, q: (q, 0)),
            pl.BlockSpec((TQ, N_HEADS * ROPE_DIM), lambda b, q: (q, 0)),
            _const_spec((1, ATTN_W)),
            pl.BlockSpec((seq, KV_LORA), lambda b, q: (b, 0)),
            pl.BlockSpec((None, ROPE_DIM, seq), lambda b, q: (b, 0, 0)),
        ],
        out_specs=pl.BlockSpec((TQ, ATTN_W), lambda b, q: (b * nq + q, 0)),
        out_shape=jax.ShapeDtypeStruct((n_batch * seq, ATTN_W), BF16),
        scratch_shapes=[
            pltpu.VMEM((N_HEADS, seq, kpad), BF16),
            pltpu.VMEM((N_HEADS, seq, V_DIM), BF16),
            pltpu.VMEM((N_HEADS, TQ, kpad), BF16),
            pltpu.VMEM((N_HEADS, TQ, LANES), F32),
            pltpu.VMEM((N_HEADS, TQ, LANES), F32),
            pltpu.VMEM((N_HEADS, TQ, V_DIM), F32),
            pltpu.VMEM((TQ, LANES), jnp.int32),
        ],
        compiler_params=pltpu.CompilerParams(dimension_semantics=("arbitrary", "arbitrary"),
                                             vmem_limit_bytes=VMEM_LIMIT),
        name="attn_prompt",
    )(cqn, w_q, w_kv, cosq, sinq, gao, ckv, krope_t)


def _out_proj_kernel(attnp_ref, attns_ref, convn_ref, xp_ref, xs_ref, wo_hbm, gffn_ref, wr_ref,
                     br_ref, h_ref, xpk_ref, mi_ref, mf_ref, cnt_ref, carry_ref, logit_ref,
                     wo_ref, stage_ref, wsems, *, n_prompt_tiles):
    i = pl.program_id(0)

    @pl.when(i == 0)
    def _():
        carry_ref[...] = jnp.zeros(carry_ref.shape, F32)
        logit_ref[...] = jnp.zeros(logit_ref.shape, F32)
        _load_weight_as_bf16(wo_hbm, wo_ref, stage_ref, wsems)

    def tile(x_ref, attn_ref):
        prev_logits = logit_ref[...]
        y = jnp.dot(attn_ref[...], wo_ref[:ATTN_W, :], preferred_element_type=F32)
        y = y + jnp.dot(convn_ref[...], wo_ref[ATTN_W:, :], preferred_element_type=F32)
        h = x_ref[...] + y
        h_ref[...] = h
        xn = _rms(h, gffn_ref[...])

        half = D_MODEL // 2
        xh = xn.astype(BF16)
        xh32 = xh.astype(F32)
        lo = lax.bitcast_convert_type(xh32[:, :half], jnp.uint32)
        hi = lax.bitcast_convert_type(xh32[:, half:], jnp.uint32)
        xpk_ref[...] = (lo >> 16) | (hi & jnp.uint32(0xFFFF0000))

        xl = (xn - xh32).astype(BF16)
        hh_hl = jnp.dot(xh, wr_ref[...], preferred_element_type=F32)
        lh = jnp.dot(xl, wr_ref[:, :LANES], preferred_element_type=F32)
        logit_ref[...] = hh_hl[:, :LANES] + (lh + hh_hl[:, LANES:]) + br_ref[...]

        logits = prev_logits
        counted = (i > 0).astype(F32)
        lane = lax.broadcasted_iota(jnp.int32, (TM, LANES), 1).astype(F32)
        ninf = -jnp.inf
        far = float(LANES)

        def first_argmax(v):
            vmax = jnp.max(v, axis=-1, keepdims=True)
            return vmax, jnp.min(jnp.where(v == vmax, lane, far), axis=-1, keepdims=True)

        gl = jnp.where(lane < N_GROUPS, logits, ninf)
        gmax, gidx = first_argmax(gl)
        g_p = 1.0 / jnp.sum(jnp.exp(gl - gmax), axis=-1, keepdims=True)
        e_lo = N_GROUPS + EXPERTS_PER_GROUP * gidx
        el = jnp.where((lane >= e_lo) & (lane < e_lo + EXPERTS_PER_GROUP), logits, ninf)
        e1max, i1 = first_argmax(el)
        z = jnp.sum(jnp.exp(el - e1max), axis=-1, keepdims=True)
        el2 = jnp.where(lane == i1, ninf, el)
        e2max, i2 = first_argmax(el2)
        p1 = 1.0 / z
        p2 = jnp.exp(e2max - e1max) / z
        den = p1 + p2
        g0 = g_p * p1 / den
        g1 = g_p * p2 / den
        e0 = i1 - N_GROUPS
        e1 = i2 - N_GROUPS

        oh0 = lane == e0
        oh1 = lane == e1
        oh = jnp.where(oh0 | oh1, 1.0, 0.0)
        r = lax.broadcasted_iota(jnp.int32, (TM, TM), 0)
        c = lax.broadcasted_iota(jnp.int32, (TM, TM), 1)
        ltri = jnp.where(r > c, 1.0, 0.0).astype(BF16)
        before = jnp.dot(ltri, oh.astype(BF16), preferred_element_type=F32) + carry_ref[...]
        rank0 = jnp.sum(jnp.where(oh0, before, 0.0), axis=-1, keepdims=True)
        rank1 = jnp.sum(jnp.where(oh1, before, 0.0), axis=-1, keepdims=True)
        total = carry_ref[...] + counted * jnp.sum(oh, axis=0, keepdims=True)
        carry_ref[...] = total
        cnt_ref[...] = jnp.broadcast_to(total, cnt_ref.shape)

        mi = jnp.where(lane == 0, e0, jnp.where(lane == 1, e1, jnp.where(lane == 2, rank0, rank1)))
        mi_ref[...] = jnp.transpose(mi)[:SUBLANES, :].astype(jnp.int32)
        mf_ref[...] = jnp.where(lane == 0, g0, g1)

    @pl.when(i < n_prompt_tiles)
    def _():
        tile(xp_ref, attnp_ref)

    @pl.when(i >= n_prompt_tiles)
    def _():
        tile(xs_ref, attns_ref)


def _out_proj(attn_p, attn_s, conv_n, xp, xs, w_ob, gffn, w_r2, b_r):
    m = conv_n.shape[0]
    npt = xp.shape[0] // TM
    n_tiles = m // TM
    last_p, last_s, last = npt - 1, n_tiles - npt - 1, n_tiles - 1
    row = lambda i: (jnp.minimum(i, last), 0)
    prow = lambda i: (jnp.minimum(i, last_p), 0)
    srow = lambda i: (jnp.clip(i - npt, 0, last_s), 0)
    lag = lambda i: jnp.maximum(i - 1, 0)
    return pl.pallas_call(
        functools.partial(_out_proj_kernel, n_prompt_tiles=npt),
        grid=(n_tiles + 1,),
        in_specs=[
            pl.BlockSpec((TM, ATTN_W), prow),
            pl.BlockSpec((TM, ATTN_W), srow),
            pl.BlockSpec((TM, CONV_CH), row),
            pl.BlockSpec((TM, D_MODEL), prow),
            pl.BlockSpec((TM, D_MODEL), srow),
            pl.BlockSpec(memory_space=pl.ANY),
            _const_spec((1, D_MODEL)),
            _const_spec(w_r2.shape),
            _const_spec((1, LANES)),
        ],
        out_specs=[
            pl.BlockSpec((TM, D_MODEL), row),
            pl.BlockSpec((TM, D_MODEL // 2), row),
            pl.BlockSpec((SUBLANES, TM), lambda i: (0, lag(i))),
            pl.BlockSpec((TM, LANES), lambda i: (lag(i), 0)),
            pl.BlockSpec((SUBLANES, LANES), lambda i: (0, 0)),
        ],
        out_shape=[
            jax.ShapeDtypeStruct((m, D_MODEL), F32),
            jax.ShapeDtypeStruct((m, D_MODEL // 2), jnp.uint32),
            jax.ShapeDtypeStruct((SUBLANES, m), jnp.int32),
            jax.ShapeDtypeStruct((m, LANES), F32),
            jax.ShapeDtypeStruct((SUBLANES, LANES), F32),
        ],
        scratch_shapes=[pltpu.VMEM((1, LANES), F32), pltpu.VMEM((TM, LANES), F32),
                        pltpu.VMEM(w_ob.shape, BF16),
                        pltpu.VMEM((2, w_ob.shape[0] // W_O_SLABS, w_ob.shape[1]), F32),
                        pltpu.SemaphoreType.DMA((2,))],
        compiler_params=pltpu.CompilerParams(dimension_semantics=("arbitrary",),
                                             vmem_limit_bytes=VMEM_LIMIT),
        name="out_proj",
    )(attn_p, attn_s, conv_n, xp, xs, w_ob, gffn, w_r2, b_r)


def _dispatch_kernel(d0_ref, d1_ref, zlo_ref, zn_ref, nu_ref, xpk_ref, xs_hbm, zeros_ref, sems, *, n_blocks):
    i = pl.program_id(0)
    sem = sems.at[0]
    zsem = sems.at[1]

    def zero_fill(act):
        def per_expert(e, c):
            lo = zlo_ref[e]
            n = zn_ref[e]
            head = (-lo) & (SUBLANES - 1)
            for r in range(SUBLANES - 1):
                @pl.when(r < head)
                def _(r=r):
                    act(pltpu.make_async_copy(zeros_ref.at[pl.ds(0, 1)], xs_hbm.at[pl.ds(lo + r, 1)], zsem))
            off = lo + head
            rest = n - head
            size = MOE_BLOCK // 2
            while size >= SUBLANES:
                @pl.when((rest & size) != 0)
                def _(off=off, size=size):
                    dst = xs_hbm.at[pl.ds(pl.multiple_of(off, SUBLANES), size)]
                    act(pltpu.make_async_copy(zeros_ref.at[pl.ds(0, size)], dst, zsem))
                off = off + (rest & size)
                size //= 2
            return c

        def per_block(b, c):
            dst = xs_hbm.at[pl.ds(pl.multiple_of(b * MOE_BLOCK, MOE_BLOCK), MOE_BLOCK)]
            act(pltpu.make_async_copy(zeros_ref, dst, zsem))
            return c

        lax.fori_loop(0, N_EXPERTS, per_expert, 0)
        lax.fori_loop(nu_ref[0], n_blocks, per_block, 0)

    @pl.when(i == 0)
    def _():
        zeros_ref[...] = jnp.zeros(zeros_ref.shape, zeros_ref.dtype)
        zero_fill(lambda cp: cp.start())

    @pl.when(i == pl.num_programs(0) - 1)
    def _():
        zero_fill(lambda cp: cp.wait())

    base = i * TD

    def start(g, c):
        for u in range(SUBLANES):
            r = base + g * SUBLANES + u
            src = xpk_ref.at[g, pl.ds(u, 1)]
            pltpu.make_async_copy(src, xs_hbm.at[pl.ds(d0_ref[r], 1)], sem).start(priority=0)
            pltpu.make_async_copy(src, xs_hbm.at[pl.ds(d1_ref[r], 1)], sem).start(priority=1)
        return c

    lax.fori_loop(0, TD // SUBLANES, start, 0)
    for _ in range(2):
        pltpu.make_async_copy(xs_hbm.at[pl.ds(0, TD)], xs_hbm.at[pl.ds(0, TD)], sem).wait()


def _dispatch(dest0, dest1, pad_lo, n_pad, n_used, xpk, n_blocks):
    m = xpk.shape[0]
    grid_spec = pltpu.PrefetchScalarGridSpec(
        num_scalar_prefetch=5,
        grid=(m // TD,),
        in_specs=[pl.BlockSpec((TD // SUBLANES, SUBLANES, D_MODEL // 2), lambda i, *_: (i, 0, 0))],
        out_specs=pl.BlockSpec(memory_space=pl.ANY),
        scratch_shapes=[pltpu.VMEM((MOE_BLOCK, D_MODEL // 2), jnp.uint32),
                        pltpu.SemaphoreType.DMA((2,))],
    )
    return pl.pallas_call(
        functools.partial(_dispatch_kernel, n_blocks=n_blocks),
        grid_spec=grid_spec,
        out_shape=jax.ShapeDtypeStruct((n_blocks * MOE_BLOCK, D_MODEL // 2), jnp.uint32),
        compiler_params=pltpu.CompilerParams(dimension_semantics=("arbitrary",)),
        name="dispatch",
    )(dest0, dest1, pad_lo, n_pad, n_used, xpk.reshape(m // SUBLANES, SUBLANES, D_MODEL // 2))


def _experts_kernel(be_ref, nu_ref, nxt_ref, x_ref, wg_hbm, wu_hbm, wd_hbm, y_ref,
                    sg_ref, su_ref, sd_ref, wgb_ref, wub_ref, wdb_ref, sems):
    b = pl.program_id(0)
    active = b < nu_ref[0]
    new_expert = jnp.logical_or(b == 0, be_ref[b] != be_ref[jnp.maximum(b - 1, 0)])

    def weight_copies(e):
        return (pltpu.make_async_copy(wg_hbm.at[e], sg_ref, sems.at[0]),
                pltpu.make_async_copy(wu_hbm.at[e], su_ref, sems.at[1]),
                pltpu.make_async_copy(wd_hbm.at[e], sd_ref, sems.at[2]))

    @pl.when(b == 0)
    def _():
        for cp in weight_copies(be_ref[0]):
            cp.start(priority=1)

    @pl.when(jnp.logical_and(active, new_expert))
    def _():
        for cp in weight_copies(be_ref[b]):
            cp.wait()
        wgb_ref[...] = sg_ref[...].astype(BF16)
        wub_ref[...] = su_ref[...].astype(BF16)
        wdb_ref[...] = sd_ref[...].astype(BF16)

        @pl.when(nxt_ref[b] >= 0)
        def _():
            for cp in weight_copies(nxt_ref[b]):
                cp.start(priority=1)

    @pl.when(active)
    def _():
        half = D_MODEL // 2
        xa, xb = (v.astype(BF16) for v in _unpack_bf16_pairs(x_ref[...]))
        g = jnp.dot(xa, wgb_ref[:half, :], preferred_element_type=F32)
        g = g + jnp.dot(xb, wgb_ref[half:, :], preferred_element_type=F32)
        u = jnp.dot(xa, wub_ref[:half, :], preferred_element_type=F32)
        u = u + jnp.dot(xb, wub_ref[half:, :], preferred_element_type=F32)
        hmid = (g * jax.nn.sigmoid(g)) * u
        y = jnp.dot(hmid.astype(BF16), wdb_ref[...], preferred_element_type=F32)
        y_ref[...] = _pack_bf16_pairs(y)

    @pl.when(b >= nu_ref[0])
    def _():
        y_ref[...] = jnp.zeros(y_ref.shape, y_ref.dtype)


def _experts(block_e, n_used, next_e, x_sorted, w_gate, w_up, w_down):
    p = x_sorted.shape[0]
    nb = p // MOE_BLOCK

    def xrow(b, be, nu, nxt):
        return (jnp.maximum(jnp.minimum(b, nu[0] - 1), 0), 0)

    grid_spec = pltpu.PrefetchScalarGridSpec(
        num_scalar_prefetch=3,
        grid=(nb,),
        in_specs=[
            pl.BlockSpec((MOE_BLOCK, D_MODEL // 2), xrow),
            pl.BlockSpec(memory_space=pl.ANY),
            pl.BlockSpec(memory_space=pl.ANY),
            pl.BlockSpec(memory_space=pl.ANY),
        ],
        out_specs=pl.BlockSpec((MOE_BLOCK, D_MODEL // 2), lambda b, be, nu, nxt: (b, 0)),
        scratch_shapes=[pltpu.VMEM((D_MODEL, D_FF), F32), pltpu.VMEM((D_MODEL, D_FF), F32),
                        pltpu.VMEM((D_FF, D_MODEL), F32),
                        pltpu.VMEM((D_MODEL, D_FF), BF16), pltpu.VMEM((D_MODEL, D_FF), BF16),
                        pltpu.VMEM((D_FF, D_MODEL), BF16),
                        pltpu.SemaphoreType.DMA((3,))],
    )
    return pl.pallas_call(
        _experts_kernel,
        grid_spec=grid_spec,
        out_shape=jax.ShapeDtypeStruct((p, D_MODEL // 2), jnp.uint32),
        compiler_params=pltpu.CompilerParams(dimension_semantics=("arbitrary",),
                                             vmem_limit_bytes=VMEM_LIMIT),
        name="experts",
    )(block_e, n_used, next_e, x_sorted, w_gate, w_up, w_down)


def _combine_kernel(d0_ref, d1_ref, h_ref, mf_ref, gfin_ref, y_hbm, outp_ref, outs_ref, y0_ref, y1_ref, sems,
                    *, n_tiles, n_prompt_tiles):
    i = pl.program_id(0)

    def gather(tile, slot):
        base = tile * TM

        def body(g, c):
            for u in range(SUBLANES):
                r = base + g * SUBLANES + u
                pltpu.make_async_copy(y_hbm.at[pl.ds(d0_ref[r], 1)], y0_ref.at[slot, g, pl.ds(u, 1)],
                                      sems.at[slot]).start(priority=0)
                pltpu.make_async_copy(y_hbm.at[pl.ds(d1_ref[r], 1)], y1_ref.at[slot, g, pl.ds(u, 1)],
                                      sems.at[slot]).start(priority=1)
            return c
        lax.fori_loop(0, TM // SUBLANES, body, 0)

    @pl.when(i == 0)
    def _():
        gather(0, 0)

    @pl.when(i + 1 < n_tiles)
    def _():
        gather(i + 1, (i + 1) % 2)

    slot = i % 2
    for _ in range(2):
        pltpu.make_async_copy(y_hbm.at[pl.ds(0, TM)], y_hbm.at[pl.ds(0, TM)], sems.at[slot]).wait()

    def finish(out_ref):
        mf = mf_ref[...]
        g0, g1 = mf[:, :, 0:1], mf[:, :, 1:2]
        half = D_MODEL // 2
        a0, b0 = _unpack_bf16_pairs(y0_ref[slot])
        a1, b1 = _unpack_bf16_pairs(y1_ref[slot])
        o_lo = h_ref[:, :, :half] + (g0 * a0 + g1 * a1)
        o_hi = h_ref[:, :, half:] + (g0 * b0 + g1 * b1)
        sumsq = jnp.sum(o_lo * o_lo, axis=-1, keepdims=True) + jnp.sum(o_hi * o_hi, axis=-1, keepdims=True)
        inv_rms = lax.rsqrt(sumsq / D_MODEL + EPS)
        out_ref[:, :, :half] = o_lo * inv_rms * gfin_ref[:, :, :half]
        out_ref[:, :, half:] = o_hi * inv_rms * gfin_ref[:, :, half:]

    @pl.when(i < n_prompt_tiles)
    def _():
        finish(outp_ref)

    @pl.when(i >= n_prompt_tiles)
    def _():
        finish(outs_ref)


def _combine(dest0, dest1, h, mf, gfin, y_sorted, *, n_prompt_rows):
    m = h.shape[0]
    npt = n_prompt_rows // TM
    tg = TM // SUBLANES
    grouped = lambda a: a.reshape(a.shape[0] // SUBLANES, SUBLANES, a.shape[1])
    grid_spec = pltpu.PrefetchScalarGridSpec(
        num_scalar_prefetch=2,
        grid=(m // TM,),
        in_specs=[
            pl.BlockSpec((tg, SUBLANES, D_MODEL), lambda i, *_: (i, 0, 0)),
            pl.BlockSpec((tg, SUBLANES, LANES), lambda i, *_: (i, 0, 0)),
            pl.BlockSpec((1, 1, D_MODEL), lambda i, *_: (0, 0, 0)),
            pl.BlockSpec(memory_space=pl.ANY),
        ],
        out_specs=[pl.BlockSpec((tg, SUBLANES, D_MODEL), lambda i, *_: (jnp.minimum(i, npt - 1), 0, 0)),
                   pl.BlockSpec((tg, SUBLANES, D_MODEL), lambda i, *_: (jnp.maximum(i - npt, 0), 0, 0))],
        scratch_shapes=[pltpu.VMEM((2, tg, SUBLANES, D_MODEL // 2), jnp.uint32),
                        pltpu.VMEM((2, tg, SUBLANES, D_MODEL // 2), jnp.uint32),
                        pltpu.SemaphoreType.DMA((2,))],
    )
    y_p, y_s = pl.pallas_call(
        functools.partial(_combine_kernel, n_tiles=m // TM, n_prompt_tiles=npt),
        grid_spec=grid_spec,
        out_shape=[jax.ShapeDtypeStruct((n_prompt_rows // SUBLANES, SUBLANES, D_MODEL), F32),
                   jax.ShapeDtypeStruct(((m - n_prompt_rows) // SUBLANES, SUBLANES, D_MODEL), F32)],
        compiler_params=pltpu.CompilerParams(dimension_semantics=("arbitrary",),
                                             vmem_limit_bytes=VMEM_LIMIT),
        name="combine",
    )(dest0, dest1, grouped(h), grouped(mf), gfin.reshape(1, 1, D_MODEL), y_sorted)
    return y_p.reshape(n_prompt_rows, D_MODEL), y_s.reshape(m - n_prompt_rows, D_MODEL)


def _rope_tables(pos):
    f32 = np.float32
    inv = np.power(f32(ROPE_THETA), -np.arange(0, ROPE_DIM, 2, dtype=f32) / f32(ROPE_DIM)).astype(f32)
    ang = (pos.astype(f32)[:, None] * inv[None, :]).astype(f32)
    cos, sin = np.cos(ang).astype(f32), np.sin(ang).astype(f32)
    return np.concatenate([cos, cos], axis=-1), np.concatenate([-sin, sin], axis=-1)


def _swap_halves(w):
    return jnp.concatenate([w[..., ROPE_DIM // 2:], w[..., :ROPE_DIM // 2]], axis=-1)


def kernel(x_prompt, x_sample, cache_kv_latent, cache_k_rope, state_conv, norm_mix, w_in, norm_q, w_uq,
           norm_kv, w_uk, w_uv, conv_w, norm_attn_out, norm_conv_out, w_o, norm_ffn, w_router_group,
           b_router_group, w_router_expert, b_router_expert, w_gate, w_up, w_down, norm_final):
    assert w_in.shape[0] == 1, "single-layer trunk"
    bp, seq_p, _ = x_prompt.shape
    bs, seq_s, _ = x_sample.shape
    past_len = cache_kv_latent.shape[2]
    np_rows, ns_rows = bp * seq_p, bs * seq_s
    m = np_rows + ns_rows
    assert seq_p % TM == 0 and TM % seq_s == 0 and ns_rows % TM == 0 and seq_s == CHUNK
    assert m % TD == 0

    xp = x_prompt.reshape(np_rows, D_MODEL)
    xs = x_sample.reshape(ns_rows, D_MODEL)
    row_vec = lambda v: v.reshape(1, -1)

    assert w_in.shape[2] == Q_LORA + KV_LORA + ROPE_DIM + 3 * CONV_CH
    w_t = jnp.swapaxes(w_in[0], 0, 1)
    assert w_t.shape[0] % (W_IN_SLABS * 2 * SUBLANES) == 0
    wq4 = w_uq[0].reshape(Q_LORA, N_HEADS, QK_NOPE + ROPE_DIM)
    wq_rope = wq4[:, :, QK_NOPE:]
    w_q = jnp.concatenate([wq4[:, :, :QK_NOPE].reshape(Q_LORA, -1), wq_rope.reshape(Q_LORA, -1),
                           _swap_halves(wq_rope).reshape(Q_LORA, -1)], axis=1).astype(BF16)
    w_ukt = jnp.transpose(w_uk[0], (1, 2, 0)).astype(BF16)
    w_uvh = jnp.transpose(w_uv[0], (1, 0, 2)).astype(BF16)
    w_ob = w_o[0]
    n_router = N_GROUPS + N_EXPERTS
    w_r = jnp.concatenate([w_router_group[0], w_router_expert[0].reshape(D_MODEL, N_EXPERTS)], axis=1)
    w_r = jnp.pad(w_r, ((0, 0), (0, LANES - n_router)))
    w_rh = w_r.astype(BF16)
    w_rl = (w_r - w_rh.astype(F32)).astype(BF16)
    w_r2 = jnp.concatenate([w_rh, w_rl], axis=1)
    b_r =jnp.pad(jnp.concatenate([b_router_group[0], b_router_expert[0].reshape(N_EXPERTS)]),
                  (0, LANES - n_router)).reshape(1, LANES)

    cos_p, sin_p = _rope_tables(np.arange(seq_p))
    cos_s, sin_s = _rope_tables(past_len + np.arange(seq_s))
    cosk = np.concatenate([cos_p, np.tile(cos_s, (TM // seq_s, 1))], axis=0)
    sink = np.concatenate([sin_p, np.tile(sin_s, (TM // seq_s, 1))], axis=0)
    state = jnp.concatenate([jnp.zeros((bp, CONV_W - 1, CONV_CH), F32), state_conv[0]], axis=0)

    cqn, ckv_p, kr_p, ckv_s, kr_s, conv_n, utail = _in_proj(
        xp, xs, row_vec(norm_mix[0]), w_t, row_vec(norm_q[0]), row_vec(norm_kv[0]),
        row_vec(norm_conv_out[0]), conv_w[0], cosk, sink, state, seq_p=seq_p, seq_s=seq_s)

    gao = row_vec(norm_attn_out[0])
    w_kv = jnp.concatenate([w_uk[0].reshape(KV_LORA, N_HEADS * QK_NOPE),
                            w_uv[0].reshape(KV_LORA, N_HEADS * V_DIM)], axis=1).astype(BF16)
    attn_p = _attention_heads(cqn, w_q, w_kv, np.tile(cos_p, (1, N_HEADS)), np.tile(sin_p, (1, N_HEADS)),
                              gao, ckv_p, kr_p, n_batch=bp, seq=seq_p)
    attn_s = _attention(cqn, w_q, w_ukt, w_uvh, np.tile(cos_s, (1, N_HEADS)), np.tile(sin_s, (1, N_HEADS)),
                        gao, ckv_s, kr_s, cache_kv_latent[0], jnp.swapaxes(cache_k_rope[0], 1, 2),
                        n_batch=bs, seq=seq_s, row0=np_rows)

    h, xpk, mi, mf, cnt = _out_proj(attn_p, attn_s, conv_n, xp, xs, w_ob, row_vec(norm_ffn[0]),
                                    w_r2, b_r)

    counts = cnt[0, :N_EXPERTS].astype(jnp.int32)
    padded = (counts + MOE_BLOCK - 1) // MOE_BLOCK * MOE_BLOCK
    pad_end = jnp.cumsum(padded)
    pad_start = pad_end - padded
    n_blocks = -(-(m * 2) // MOE_BLOCK) + N_EXPERTS
    block_row0 = jnp.arange(n_blocks, dtype=jnp.int32) * MOE_BLOCK
    block_e = jnp.minimum(jnp.sum((pad_end[None, :] <= block_row0[:, None]).astype(jnp.int32), axis=1),
                          N_EXPERTS - 1)
    n_used = (pad_end[-1:] // MOE_BLOCK).astype(jnp.int32)
    expert_ids = jnp.arange(N_EXPERTS, dtype=jnp.int32)[:, None]

    def seg_start(e):
        return jnp.sum(jnp.where(expert_ids == e[None, :], pad_start[:, None], 0), axis=0)

    dest0 = seg_start(mi[0]) + mi[2]
    dest1 = seg_start(mi[1]) + mi[3]

    x_sorted = _dispatch(dest0, dest1, pad_start + counts, padded - counts, n_used, xpk, n_blocks)
    later = (expert_ids.T > block_e[:, None]) & (padded > 0)[None, :]
    next_e = jnp.min(jnp.where(later, expert_ids.T, N_EXPERTS), axis=1)
    next_e = jnp.where(next_e == N_EXPERTS, -1, next_e).astype(jnp.int32)
    y_sorted = _experts(block_e, n_used, next_e, x_sorted, w_gate[0], w_up[0], w_down[0])
    gfin = row_vec(norm_final)
    y_p, y_s = _combine(dest0, dest1, h, mf, gfin, y_sorted, n_prompt_rows=np_rows)

    ut = utail.reshape(m // CHUNK, SUBLANES, CONV_CH)
    tails = ut[:, SUBLANES - (CONV_W - 1):, :]
    p_last = (jnp.arange(bp) + 1) * (seq_p // CHUNK) - 1
    s_last = np_rows // CHUNK + (jnp.arange(bs) + 1) * (seq_s // CHUNK) - 1
    return (y_p.reshape(bp, seq_p, D_MODEL),
            y_s.reshape(bs, seq_s, D_MODEL),
            ckv_p.reshape(1, bp, seq_p, KV_LORA),
            jnp.swapaxes(kr_p, 1, 2)[None],
            tails[p_last][None],
            ckv_s.reshape(1, bs, seq_s, KV_LORA),
            jnp.swapaxes(kr_s, 1, 2)[None],
            tails[s_last][None])
```
